```python
import math
import jax, jax.numpy as jnp
from jax import lax
import numpy as np


D_MODEL = 1024
BATCH = 8
SEQ = 8192
DEPTH = 2

HEAD_DIM = 128
BLOCK = 128
GRID_W = 64
EPS = 1e-6
NEG_INF = -1e30
RET_HEADS = D_MODEL // 256
RET_DK = 128
RET_DV = 256
RET_CHUNK = 128
RET_THETA = 10000.0
SWA_HEADS = D_MODEL // HEAD_DIM
SWA_KV_HEADS = 2
WINDOW = 128
T5_BUCKETS = 32
T5_MAX_DIST = 128
AX_HEADS = D_MODEL // HEAD_DIM
AX_KV_HEADS = 2
AX_THETA = 10000.0
D_FF = 4 * D_MODEL

N_EVEN = (DEPTH + 1) // 2
N_ODD = DEPTH // 2
RET_Q = RET_HEADS * RET_DK
RET_V = RET_HEADS * RET_DV
SWA_Q = SWA_HEADS * HEAD_DIM
SWA_KV = SWA_KV_HEADS * HEAD_DIM
EVEN_IN = 2 * RET_Q + 2 * RET_V + SWA_Q + 2 * SWA_KV
EVEN_OUT = RET_V + SWA_Q
AX_Q = AX_HEADS * HEAD_DIM
AX_KV = AX_KV_HEADS * HEAD_DIM
ODD_IN = AX_Q + 2 * AX_KV

kernel_name = "hybrid_retention_swa_axialrope_encoder"


def rms_norm(x, g):
    xf = x.astype(jnp.float32)
    y = xf * lax.rsqrt(jnp.mean(xf * xf, axis=-1, keepdims=True) + EPS)
    return (y * g.astype(jnp.float32)).astype(x.dtype)


def split_cols(a, sizes):
    offs, o = [], 0
    for s in sizes[:-1]:
        o += s
        offs.append(o)
    return jnp.split(a, offs, axis=-1)


def rope_angles(pos, dim, theta):
    inv = theta ** (-jnp.arange(0, dim, 2, dtype=jnp.float32) / dim)
    return pos.astype(jnp.float32)[:, None] * inv[None, :]


def apply_rope(x, ang):
    d2 = x.shape[-1] // 2
    xf = x.astype(jnp.float32)
    x1, x2 = xf[..., :d2], xf[..., d2:]
    c = jnp.cos(ang)[None, :, None, :]
    s = jnp.sin(ang)[None, :, None, :]
    return jnp.concatenate([x1 * c - x2 * s, x2 * c + x1 * s], axis=-1).astype(x.dtype)


def retention_direction(q, k, v, log_gamma, strict):
    Bn, S, H, dk = q.shape
    dv = v.shape[-1]
    C = RET_CHUNK
    nc = S // C
    qc = q.reshape(Bn, nc, C, H, dk)
    kc = k.reshape(Bn, nc, C, H, dk)
    vc = v.reshape(Bn, nc, C, H, dv)
    idx = jnp.arange(C, dtype=jnp.float32)
    diff = idx[:, None] - idx[None, :]
    mask = (diff > 0) if strict else (diff >= 0)
    decay = jnp.where(mask[None], jnp.exp(jnp.maximum(diff, 0.0)[None] * log_gamma[:, None, None]), 0.0)
    scores = jnp.einsum('bnrhd,bnjhd->bnhrj', qc, kc) * decay[None, None]
    inner = jnp.einsum('bnhrj,bnjhe->bnrhe', scores, vc)
    k_w = kc * jnp.exp((C - 1 - idx)[:, None] * log_gamma[None, :])[None, None, :, :, None]
    U = jnp.einsum('bnjhd,bnjhe->nbhde', k_w, vc)
    chunk_decay = jnp.exp(C * log_gamma)[None, :, None, None]

    def step(state, u):
        return chunk_decay * state + u, state

    _, prev = lax.scan(step, jnp.zeros((Bn, H, dk, dv), jnp.float32), U)
    q_w = qc * jnp.exp((idx + 1)[:, None] * log_gamma[None, :])[None, None, :, :, None]
    cross = jnp.einsum('bnrhd,nbhde->bnrhe', q_w, prev)
    return (inner + cross).reshape(Bn, S, H, dv)


def retention_mixer(q, k, v, g, decay_logit, gn_gain):
    Bn, S, _ = q.shape
    dt = v.dtype
    q = q.reshape(Bn, S, RET_HEADS, RET_DK)
    k = k.reshape(Bn, S, RET_HEADS, RET_DK)
    v = v.reshape(Bn, S, RET_HEADS, RET_DV).astype(jnp.float32)
    ang = rope_angles(jnp.arange(S), RET_DK, RET_THETA)
    q = apply_rope(q, ang).astype(jnp.float32)
    k = apply_rope(k, ang).astype(jnp.float32) * (RET_DK ** -0.5)
    log_gamma = jax.nn.log_sigmoid(decay_logit.astype(jnp.float32))
    fwd = retention_direction(q, k, v, log_gamma[0], False)
    bwd = jnp.flip(retention_direction(jnp.flip(q, 1), jnp.flip(k, 1), jnp.flip(v, 1),
                                       log_gamma[1], True), 1)
    y = rms_norm(fwd + bwd, gn_gain.reshape(RET_HEADS, RET_DV))
    y = y.reshape(Bn, S, RET_V).astype(dt)
    return jax.nn.silu(g) * y


def t5_bucket(rel):
    nb = T5_BUCKETS // 2
    max_exact = nb // 2
    ret = jnp.where(rel > 0, nb, 0)
    n = jnp.abs(rel)
    nf = jnp.maximum(n, 1).astype(jnp.float32)
    large = max_exact + (jnp.log(nf / max_exact) / math.log(T5_MAX_DIST / max_exact)
                         * (nb - max_exact)).astype(jnp.int32)
    large = jnp.minimum(large, nb - 1)
    return ret + jnp.where(n < max_exact, n, large)


def window_attention(q, k, v, sink, t5_table):
    Bn, S, Hq, D = q.shape
    Hkv = k.shape[2]
    G = Hq // Hkv
    nb = S // BLOCK
    pad = ((0, 0), (BLOCK, BLOCK), (0, 0), (0, 0))
    kp = jnp.pad(k, pad).reshape(Bn, nb + 2, BLOCK, Hkv, D)
    vp = jnp.pad(v, pad).reshape(Bn, nb + 2, BLOCK, Hkv, D)
    kw = jnp.concatenate([kp[:, :nb], kp[:, 1:nb + 1], kp[:, 2:]], axis=2)
    vw = jnp.concatenate([vp[:, :nb], vp[:, 1:nb + 1], vp[:, 2:]], axis=2)
    qb = q.reshape(Bn, nb, BLOCK, Hkv, G, D)
    s = jnp.einsum('bnqkgd,bnjkd->bnkgqj', qb, kw,
                   preferred_element_type=jnp.float32) * (D ** -0.5)
    r = jnp.arange(BLOCK)
    j = jnp.arange(3 * BLOCK)
    rel = j[None, :] - BLOCK - r[:, None]
    bias = t5_table.astype(jnp.float32)[t5_bucket(rel)]
    bias = bias.transpose(2, 0, 1).reshape(Hkv, G, BLOCK, 3 * BLOCK)
    kpos = (jnp.arange(nb)[:, None] - 1) * BLOCK + j[None, :]
    valid = (jnp.abs(rel) <= WINDOW)[None] & ((kpos >= 0) & (kpos < S))[:, None, :]
    s = jnp.where(valid[None, :, None, None], s + bias[None, None], NEG_INF)
    sink_l = sink.astype(jnp.float32).reshape(Hkv, G)[None, None, :, :, None, None]
    m = jnp.maximum(jnp.max(s, axis=-1, keepdims=True), sink_l)
    p = jnp.exp(s - m)
    p = p / (jnp.sum(p, axis=-1, keepdims=True) + jnp.exp(sink_l - m))
    o = jnp.einsum('bnkgqj,bnjkd->bnqkgd', p.astype(v.dtype), vw)
    return o.reshape(Bn, S, Hq * D)


def axial_attention(q, k, v):
    Bn, S, Hq, D = q.shape
    Hkv = k.shape[2]
    G = Hq // Hkv
    rows = S // GRID_W
    row = jnp.repeat(jnp.arange(rows), GRID_W)
    col = jnp.tile(jnp.arange(GRID_W), rows)
    half = D // 2
    ang_r = rope_angles(row, half, AX_THETA)
    ang_c = rope_angles(col, half, AX_THETA)
    q = jnp.concatenate([apply_rope(q[..., :half], ang_r), apply_rope(q[..., half:], ang_c)], axis=-1)
    k = jnp.concatenate([apply_rope(k[..., :half], ang_r), apply_rope(k[..., half:], ang_c)], axis=-1)
    nb = S // BLOCK
    qb = q.reshape(Bn, nb, BLOCK, Hkv, G, D).transpose(1, 0, 2, 3, 4, 5)
    scale = D ** -0.5

    def block(qi):
        s = jnp.einsum('bqkgd,bjkd->bkgqj', qi, k, preferred_element_type=jnp.float32) * scale
        p = jax.nn.softmax(s, axis=-1)
        return jnp.einsum('bkgqj,bjkd->bqkgd', p.astype(v.dtype), v)

    o = lax.map(block, qb)
    return o.transpose(1, 0, 2, 3, 4, 5).reshape(Bn, S, Hq * D)


def _fwd_setup_inputs(seed: int = 0) -> dict:
    key = jax.random.key(seed)
    ks = jax.random.split(key, 20)
    f32 = jnp.float32

    def w(k, shape, fan_in):
        return jax.random.normal(k, shape, f32) * (fan_in ** -0.5)

    def gain(k, shape):
        return 1.0 + 0.05 * jax.random.normal(k, shape, f32)

    base_logit = jnp.log(2.0 ** (5.0 + jnp.arange(RET_HEADS, dtype=f32)) - 1.0)
    return {
        'x': jax.random.normal(ks[0], (BATCH, SEQ, D_MODEL), f32),
        'norm_mix': gain(ks[1], (DEPTH, D_MODEL)),
        'norm_mlp': gain(ks[2], (DEPTH, D_MODEL)),
        'w_in_even': w(ks[3], (N_EVEN, D_MODEL, EVEN_IN), D_MODEL),
        'w_out_even': w(ks[4], (N_EVEN, EVEN_OUT, D_MODEL), EVEN_OUT),
        'ret_decay_logit': base_logit[None, None, :] + 0.1 * jax.random.normal(ks[5], (N_EVEN, 2, RET_HEADS), f32),
        'ret_norm': gain(ks[6], (N_EVEN, RET_V)),
        'swa_q_norm': gain(ks[7], (N_EVEN, HEAD_DIM)),
        'swa_k_norm': gain(ks[8], (N_EVEN, HEAD_DIM)),
        'swa_sink': 0.5 * jax.random.normal(ks[9], (N_EVEN, SWA_HEADS), f32),
        't5_table': 0.5 * jax.random.normal(ks[10], (T5_BUCKETS, SWA_HEADS), f32),
        'w_in_odd': w(ks[11], (N_ODD, D_MODEL, ODD_IN), D_MODEL),
        'w_out_odd': w(ks[12], (N_ODD, AX_Q, D_MODEL), AX_Q),
        'ax_q_norm': gain(ks[13], (N_ODD, HEAD_DIM)),
        'ax_k_norm': gain(ks[14], (N_ODD, HEAD_DIM)),
        'w_mlp_up': w(ks[15], (DEPTH, D_MODEL, D_FF), D_MODEL),
        'w_mlp_down': w(ks[16], (DEPTH, D_FF, D_MODEL), D_FF),
    }


def _fwd_reference(x, norm_mix, norm_mlp, w_in_even, w_out_even, ret_decay_logit, ret_norm,
              swa_q_norm, swa_k_norm, swa_sink, t5_table, w_in_odd, w_out_odd,
              ax_q_norm, ax_k_norm, w_mlp_up, w_mlp_down):
    Bn, S, _ = x.shape
    for layer in range(DEPTH):
        h = rms_norm(x, norm_mix[layer])
        if layer % 2 == 0:
            i = layer // 2
            proj = h @ w_in_even[i]
            qa, ka, va, ga, qb, kb, vb = split_cols(
                proj, [RET_Q, RET_Q, RET_V, RET_V, SWA_Q, SWA_KV, SWA_KV])
            ya = retention_mixer(qa, ka, va, ga, ret_decay_logit[i], ret_norm[i])
            qb = rms_norm(qb.reshape(Bn, S, SWA_HEADS, HEAD_DIM), swa_q_norm[i])
            kb = rms_norm(kb.reshape(Bn, S, SWA_KV_HEADS, HEAD_DIM), swa_k_norm[i])
            vb = vb.reshape(Bn, S, SWA_KV_HEADS, HEAD_DIM)
            yb = window_attention(qb, kb, vb, swa_sink[i], t5_table)
            y = jnp.concatenate([ya, yb], axis=-1) @ w_out_even[i]
        else:
            i = layer // 2
            proj = h @ w_in_odd[i]
            qc, kc, vc = split_cols(proj, [AX_Q, AX_KV, AX_KV])
            qc = rms_norm(qc.reshape(Bn, S, AX_HEADS, HEAD_DIM), ax_q_norm[i])
            kc = rms_norm(kc.reshape(Bn, S, AX_KV_HEADS, HEAD_DIM), ax_k_norm[i])
            vc = vc.reshape(Bn, S, AX_KV_HEADS, HEAD_DIM)
            y = axial_attention(qc, kc, vc) @ w_out_odd[i]
        x = x + y
        h = rms_norm(x, norm_mlp[layer])
        x = x + jnp.square(jax.nn.relu(h @ w_mlp_up[layer])) @ w_mlp_down[layer]
    return x


import jax as _jax
import jax.numpy as _jnp

TWIN_FORMAT = 'train_step'
FWD_PARAMS = ['x', 'norm_mix', 'norm_mlp', 'w_in_even', 'w_out_even', 'ret_decay_logit', 'ret_norm', 'swa_q_norm', 'swa_k_norm', 'swa_sink', 't5_table', 'w_in_odd', 'w_out_odd', 'ax_q_norm', 'ax_k_norm', 'w_mlp_up', 'w_mlp_down']
TWIN_WEIGHTS = ['norm_mix', 'norm_mlp', 'w_in_even', 'w_out_even', 'ret_decay_logit', 'ret_norm', 'swa_q_norm', 'swa_k_norm', 'swa_sink', 't5_table', 'w_in_odd', 'w_out_odd', 'ax_q_norm', 'ax_k_norm', 'w_mlp_up', 'w_mlp_down']
TWIN_DIFF_INPUT = 'x'
TWIN_INPUTS = ['x', 'norm_mix', 'norm_mlp', 'w_in_even', 'w_out_even', 'ret_decay_logit', 'ret_norm', 'swa_q_norm', 'swa_k_norm', 'swa_sink', 't5_table', 'w_in_odd', 'w_out_odd', 'ax_q_norm', 'ax_k_norm', 'w_mlp_up', 'w_mlp_down', 'loss_target', 'm_norm_mix', 'm_norm_mlp', 'm_w_in_even', 'm_w_out_even', 'm_ret_decay_logit', 'm_ret_norm', 'm_swa_q_norm', 'm_swa_k_norm', 'm_swa_sink', 'm_t5_table', 'm_w_in_odd', 'm_w_out_odd', 'm_ax_q_norm', 'm_ax_k_norm', 'm_w_mlp_up', 'm_w_mlp_down', 'v_norm_mix', 'v_norm_mlp', 'v_w_in_even', 'v_w_out_even', 'v_ret_decay_logit', 'v_ret_norm', 'v_swa_q_norm', 'v_swa_k_norm', 'v_swa_sink', 'v_t5_table', 'v_w_in_odd', 'v_w_out_odd', 'v_ax_q_norm', 'v_ax_k_norm', 'v_w_mlp_up', 'v_w_mlp_down']
TWIN_OUTPUTS = ['loss', 'grad_x', 'grad_norm_mix', 'grad_norm_mlp', 'grad_w_in_even', 'grad_w_out_even', 'grad_ret_decay_logit', 'grad_ret_norm', 'grad_swa_q_norm', 'grad_swa_k_norm', 'grad_swa_sink', 'grad_t5_table', 'grad_w_in_odd', 'grad_w_out_odd', 'grad_ax_q_norm', 'grad_ax_k_norm', 'grad_w_mlp_up', 'grad_w_mlp_down', 'delta_norm_mix', 'delta_norm_mlp', 'delta_w_in_even', 'delta_w_out_even', 'delta_ret_decay_logit', 'delta_ret_norm', 'delta_swa_q_norm', 'delta_swa_k_norm', 'delta_swa_sink', 'delta_t5_table', 'delta_w_in_odd', 'delta_w_out_odd', 'delta_ax_q_norm', 'delta_ax_k_norm', 'delta_w_mlp_up', 'delta_w_mlp_down', 'new_m_norm_mix', 'new_m_norm_mlp', 'new_m_w_in_even', 'new_m_w_out_even', 'new_m_ret_decay_logit', 'new_m_ret_norm', 'new_m_swa_q_norm', 'new_m_swa_k_norm', 'new_m_swa_sink', 'new_m_t5_table', 'new_m_w_in_odd', 'new_m_w_out_odd', 'new_m_ax_q_norm', 'new_m_ax_k_norm', 'new_m_w_mlp_up', 'new_m_w_mlp_down', 'new_v_norm_mix', 'new_v_norm_mlp', 'new_v_w_in_even', 'new_v_w_out_even', 'new_v_ret_decay_logit', 'new_v_ret_norm', 'new_v_swa_q_norm', 'new_v_swa_k_norm', 'new_v_swa_sink', 'new_v_t5_table', 'new_v_w_in_odd', 'new_v_w_out_odd', 'new_v_ax_q_norm', 'new_v_ax_k_norm', 'new_v_w_mlp_up', 'new_v_w_mlp_down']
TWIN_LEAF_KINDS = {'loss': 'loss', 'grad_x': 'grad_x', 'grad_norm_mix': 'grad_w', 'grad_norm_mlp': 'grad_w', 'grad_w_in_even': 'grad_w', 'grad_w_out_even': 'grad_w', 'grad_ret_decay_logit': 'grad_w', 'grad_ret_norm': 'grad_w', 'grad_swa_q_norm': 'grad_w', 'grad_swa_k_norm': 'grad_w', 'grad_swa_sink': 'grad_w', 'grad_t5_table': 'grad_w', 'grad_w_in_odd': 'grad_w', 'grad_w_out_odd': 'grad_w', 'grad_ax_q_norm': 'grad_w', 'grad_ax_k_norm': 'grad_w', 'grad_w_mlp_up': 'grad_w', 'grad_w_mlp_down': 'grad_w', 'delta_norm_mix': 'delta_w', 'delta_norm_mlp': 'delta_w', 'delta_w_in_even': 'delta_w', 'delta_w_out_even': 'delta_w', 'delta_ret_decay_logit': 'delta_w', 'delta_ret_norm': 'delta_w', 'delta_swa_q_norm': 'delta_w', 'delta_swa_k_norm': 'delta_w', 'delta_swa_sink': 'delta_w', 'delta_t5_table': 'delta_w', 'delta_w_in_odd': 'delta_w', 'delta_w_out_odd': 'delta_w', 'delta_ax_q_norm': 'delta_w', 'delta_ax_k_norm': 'delta_w', 'delta_w_mlp_up': 'delta_w', 'delta_w_mlp_down': 'delta_w', 'new_m_norm_mix': 'new_m', 'new_m_norm_mlp': 'new_m', 'new_m_w_in_even': 'new_m', 'new_m_w_out_even': 'new_m', 'new_m_ret_decay_logit': 'new_m', 'new_m_ret_norm': 'new_m', 'new_m_swa_q_norm': 'new_m', 'new_m_swa_k_norm': 'new_m', 'new_m_swa_sink': 'new_m', 'new_m_t5_table': 'new_m', 'new_m_w_in_odd': 'new_m', 'new_m_w_out_odd': 'new_m', 'new_m_ax_q_norm': 'new_m', 'new_m_ax_k_norm': 'new_m', 'new_m_w_mlp_up': 'new_m', 'new_m_w_mlp_down': 'new_m', 'new_v_norm_mix': 'new_v', 'new_v_norm_mlp': 'new_v', 'new_v_w_in_even': 'new_v', 'new_v_w_out_even': 'new_v', 'new_v_ret_decay_logit': 'new_v', 'new_v_ret_norm': 'new_v', 'new_v_swa_q_norm': 'new_v', 'new_v_swa_k_norm': 'new_v', 'new_v_swa_sink': 'new_v', 'new_v_t5_table': 'new_v', 'new_v_w_in_odd': 'new_v', 'new_v_w_out_odd': 'new_v', 'new_v_ax_q_norm': 'new_v', 'new_v_ax_k_norm': 'new_v', 'new_v_w_mlp_up': 'new_v', 'new_v_w_mlp_down': 'new_v'}


def _forward(args):
    return _fwd_reference(*[args[k] for k in FWD_PARAMS])


def _output_shape():
    def fwd():
        inp = _fwd_setup_inputs(0)
        return _fwd_reference(*[inp[k] for k in FWD_PARAMS])
    out = _jax.eval_shape(fwd)
    return out.shape, out.dtype

N_MICROBATCH = 1
ADAM_LR = 0.001
ADAM_B1 = 0.9
ADAM_B2 = 0.999
ADAM_EPS = 1e-08
ADAM_WD = 0.01
ADAM_STEP = 10
PER_EXAMPLE_BATCH_AXIS = {'x': 0, 'loss_target': 0}
SHARED_INPUTS = []
_WEIGHT_DTYPES = {'norm_mix': _jnp.float32, 'norm_mlp': _jnp.float32, 'w_in_even': _jnp.float32, 'w_out_even': _jnp.float32, 'ret_decay_logit': _jnp.float32, 'ret_norm': _jnp.float32, 'swa_q_norm': _jnp.float32, 'swa_k_norm': _jnp.float32, 'swa_sink': _jnp.float32, 't5_table': _jnp.float32, 'w_in_odd': _jnp.float32, 'w_out_odd': _jnp.float32, 'ax_q_norm': _jnp.float32, 'ax_k_norm': _jnp.float32, 'w_mlp_up': _jnp.float32, 'w_mlp_down': _jnp.float32}
MOMENT_SCALE = {'norm_mix': 2.241708e+01, 'norm_mlp': 1.984683e+02, 'w_in_even': 7.550066e-01, 'w_out_even': 1.044265e+00, 'ret_decay_logit': 3.577593e+00, 'ret_norm': 1.067382e+01, 'swa_q_norm': 1.772030e+00, 'swa_k_norm': 1.778191e+00, 'swa_sink': 1.304561e-01, 't5_table': 5.090729e-01, 'w_in_odd': 2.166080e+01, 'w_out_odd': 2.317115e+01, 'ax_q_norm': 9.236727e-01, 'ax_k_norm': 9.242546e-01, 'w_mlp_up': 9.962426e+00, 'w_mlp_down': 4.041268e+01}


def _to_microbatches(a, axis):
    t = _jnp.moveaxis(a, axis, 0)
    t = t.reshape((N_MICROBATCH, t.shape[0] // N_MICROBATCH) + t.shape[1:])
    return _jnp.moveaxis(t, 1, axis + 1)


def setup_inputs(seed: int = 0) -> dict:
    inp = _fwd_setup_inputs(seed)
    key = _jax.random.fold_in(_jax.random.key(seed), 7919)
    shape, _ = _output_shape()
    out = dict(inp)
    out["loss_target"] = _jax.random.normal(_jax.random.fold_in(key, 0), shape, _jnp.float32)
    for i, name in enumerate(TWIN_WEIGHTS):
        w = inp[name].astype(_jnp.float32)
        if MOMENT_SCALE is None:
            s = _jnp.sqrt(_jnp.mean(_jnp.square(w)) + 1e-30)
        else:
            s = MOMENT_SCALE[name]
        km, kv = _jax.random.split(_jax.random.fold_in(key, i + 1))
        out[name] = w
        out["m_" + name] = s * _jax.random.normal(km, w.shape, _jnp.float32)
        out["v_" + name] = (s * s) * _jax.random.uniform(kv, w.shape, _jnp.float32, 0.5, 1.5)
    if N_MICROBATCH > 1:
        for name, axis in PER_EXAMPLE_BATCH_AXIS.items():
            out[name] = _to_microbatches(out[name], axis)
    return {'x': out['x'], 'norm_mix': out['norm_mix'], 'norm_mlp': out['norm_mlp'], 'w_in_even': out['w_in_even'], 'w_out_even': out['w_out_even'], 'ret_decay_logit': out['ret_decay_logit'], 'ret_norm': out['ret_norm'], 'swa_q_norm': out['swa_q_norm'], 'swa_k_norm': out['swa_k_norm'], 'swa_sink': out['swa_sink'], 't5_table': out['t5_table'], 'w_in_odd': out['w_in_odd'], 'w_out_odd': out['w_out_odd'], 'ax_q_norm': out['ax_q_norm'], 'ax_k_norm': out['ax_k_norm'], 'w_mlp_up': out['w_mlp_up'], 'w_mlp_down': out['w_mlp_down'], 'loss_target': out['loss_target'], 'm_norm_mix': out['m_norm_mix'], 'm_norm_mlp': out['m_norm_mlp'], 'm_w_in_even': out['m_w_in_even'], 'm_w_out_even': out['m_w_out_even'], 'm_ret_decay_logit': out['m_ret_decay_logit'], 'm_ret_norm': out['m_ret_norm'], 'm_swa_q_norm': out['m_swa_q_norm'], 'm_swa_k_norm': out['m_swa_k_norm'], 'm_swa_sink': out['m_swa_sink'], 'm_t5_table': out['m_t5_table'], 'm_w_in_odd': out['m_w_in_odd'], 'm_w_out_odd': out['m_w_out_odd'], 'm_ax_q_norm': out['m_ax_q_norm'], 'm_ax_k_norm': out['m_ax_k_norm'], 'm_w_mlp_up': out['m_w_mlp_up'], 'm_w_mlp_down': out['m_w_mlp_down'], 'v_norm_mix': out['v_norm_mix'], 'v_norm_mlp': out['v_norm_mlp'], 'v_w_in_even': out['v_w_in_even'], 'v_w_out_even': out['v_w_out_even'], 'v_ret_decay_logit': out['v_ret_decay_logit'], 'v_ret_norm': out['v_ret_norm'], 'v_swa_q_norm': out['v_swa_q_norm'], 'v_swa_k_norm': out['v_swa_k_norm'], 'v_swa_sink': out['v_swa_sink'], 'v_t5_table': out['v_t5_table'], 'v_w_in_odd': out['v_w_in_odd'], 'v_w_out_odd': out['v_w_out_odd'], 'v_ax_q_norm': out['v_ax_q_norm'], 'v_ax_k_norm': out['v_ax_k_norm'], 'v_w_mlp_up': out['v_w_mlp_up'], 'v_w_mlp_down': out['v_w_mlp_down']}


def _loss(weights, diff, rest, loss_target):
    with _jax.named_scope("forward"):
        args = {**rest, TWIN_DIFF_INPUT: diff, **{k: w.astype(_WEIGHT_DTYPES[k]) for k, w in weights.items()}}
        y = _forward(args)
    with _jax.named_scope("loss_head"):
        err = _jnp.square(y.astype(_jnp.float32) - loss_target)
        return 0.5 * _jnp.sum(_jnp.mean(err, axis=-1)) if err.ndim else 0.5 * err


def _adamw(w, g, m, v):
    m = ADAM_B1 * m + (1.0 - ADAM_B1) * g
    v = ADAM_B2 * v + (1.0 - ADAM_B2) * _jnp.square(g)
    m_hat = m / (1.0 - ADAM_B1 ** ADAM_STEP)
    v_hat = v / (1.0 - ADAM_B2 ** ADAM_STEP)
    delta = -ADAM_LR * (m_hat / (_jnp.sqrt(v_hat) + ADAM_EPS) + ADAM_WD * w)
    return delta, m, v


def reference(x, norm_mix, norm_mlp, w_in_even, w_out_even, ret_decay_logit, ret_norm, swa_q_norm, swa_k_norm, swa_sink, t5_table, w_in_odd, w_out_odd, ax_q_norm, ax_k_norm, w_mlp_up, w_mlp_down, loss_target, m_norm_mix, m_norm_mlp, m_w_in_even, m_w_out_even, m_ret_decay_logit, m_ret_norm, m_swa_q_norm, m_swa_k_norm, m_swa_sink, m_t5_table, m_w_in_odd, m_w_out_odd, m_ax_q_norm, m_ax_k_norm, m_w_mlp_up, m_w_mlp_down, v_norm_mix, v_norm_mlp, v_w_in_even, v_w_out_even, v_ret_decay_logit, v_ret_norm, v_swa_q_norm, v_swa_k_norm, v_swa_sink, v_t5_table, v_w_in_odd, v_w_out_odd, v_ax_q_norm, v_ax_k_norm, v_w_mlp_up, v_w_mlp_down):
    given = dict(x=x, norm_mix=norm_mix, norm_mlp=norm_mlp, w_in_even=w_in_even, w_out_even=w_out_even, ret_decay_logit=ret_decay_logit, ret_norm=ret_norm, swa_q_norm=swa_q_norm, swa_k_norm=swa_k_norm, swa_sink=swa_sink, t5_table=t5_table, w_in_odd=w_in_odd, w_out_odd=w_out_odd, ax_q_norm=ax_q_norm, ax_k_norm=ax_k_norm, w_mlp_up=w_mlp_up, w_mlp_down=w_mlp_down, loss_target=loss_target, m_norm_mix=m_norm_mix, m_norm_mlp=m_norm_mlp, m_w_in_even=m_w_in_even, m_w_out_even=m_w_out_even, m_ret_decay_logit=m_ret_decay_logit, m_ret_norm=m_ret_norm, m_swa_q_norm=m_swa_q_norm, m_swa_k_norm=m_swa_k_norm, m_swa_sink=m_swa_sink, m_t5_table=m_t5_table, m_w_in_odd=m_w_in_odd, m_w_out_odd=m_w_out_odd, m_ax_q_norm=m_ax_q_norm, m_ax_k_norm=m_ax_k_norm, m_w_mlp_up=m_w_mlp_up, m_w_mlp_down=m_w_mlp_down, v_norm_mix=v_norm_mix, v_norm_mlp=v_norm_mlp, v_w_in_even=v_w_in_even, v_w_out_even=v_w_out_even, v_ret_decay_logit=v_ret_decay_logit, v_ret_norm=v_ret_norm, v_swa_q_norm=v_swa_q_norm, v_swa_k_norm=v_swa_k_norm, v_swa_sink=v_swa_sink, v_t5_table=v_t5_table, v_w_in_odd=v_w_in_odd, v_w_out_odd=v_w_out_odd, v_ax_q_norm=v_ax_q_norm, v_ax_k_norm=v_ax_k_norm, v_w_mlp_up=v_w_mlp_up, v_w_mlp_down=v_w_mlp_down)
    weights = {n: given[n] for n in TWIN_WEIGHTS}
    shared = {n: given[n] for n in SHARED_INPUTS}
    per_example = {n: given[n] for n in ['x']}
    grad_fn = _jax.value_and_grad(_loss, argnums=(0, 1))

    def one_microbatch(ex, loss_target):
        ex = dict(ex)
        diff = ex.pop(TWIN_DIFF_INPUT)
        return grad_fn(weights, diff, {**shared, **ex}, loss_target)

    if N_MICROBATCH == 1:
        loss, (grad_w, grad_x) = one_microbatch(per_example, given["loss_target"])
    else:
        def body(carry, xs):
            loss_sum, grad_sum = carry
            l_k, (gw_k, gx_k) = one_microbatch(xs[0], xs[1])
            with _jax.named_scope("update"):
                return (loss_sum + l_k, _jax.tree.map(_jnp.add, grad_sum, gw_k)), gx_k

        init = (_jnp.zeros((), _jnp.float32), _jax.tree.map(_jnp.zeros_like, weights))
        (loss, grad_w), grad_x = _jax.lax.scan(body, init, (per_example, given["loss_target"]))
    with _jax.named_scope("update"):
        delta_w, new_m, new_v = {}, {}, {}
        for n in TWIN_WEIGHTS:
            delta_w[n], new_m[n], new_v[n] = _adamw(weights[n], grad_w[n], given["m_" + n], given["v_" + n])
    return (loss, grad_x, *[grad_w[n] for n in TWIN_WEIGHTS], *[delta_w[n] for n in TWIN_WEIGHTS],
            *[new_m[n] for n in TWIN_WEIGHTS], *[new_v[n] for n in TWIN_WEIGHTS])
```

```python
import math

import jax
import jax.numpy as jnp
from jax import lax
from jax.experimental import pallas as pl
from jax.experimental.pallas import tpu as pltpu

F32 = jnp.float32
BF = jnp.bfloat16

D_MODEL = 1024
HEAD_DIM = 128
EPS = 1e-6
NEG_INF = -1e30
RET_HEADS, RET_DK, RET_DV = 4, 128, 256
RET_THETA = 10000.0
SWA_HEADS, SWA_KV_HEADS, WINDOW, BLOCK = 8, 2, 128, 128
T5_BUCKETS, T5_MAX_DIST = 32, 128
AX_HEADS, AX_KV_HEADS, AX_THETA, GRID_W = 8, 2, 10000.0, 64
D_FF = 4096
ATT_SCALE = HEAD_DIM ** -0.5
RET_SCALE = RET_DK ** -0.5
N_DEV = 8

ADAM_LR, ADAM_B1, ADAM_B2, ADAM_EPS, ADAM_WD, ADAM_STEP = 0.001, 0.9, 0.999, 1e-08, 0.01, 10

MIB = 1024 * 1024
VMEM_SMALL = 40 * MIB
VMEM_LARGE = 56 * MIB

OFF_QA, OFF_KA, OFF_VA, OFF_GA, OFF_QB, OFF_KB, OFF_VB = 0, 512, 1024, 2048, 3072, 4096, 4352
EVEN_IN = 4608
ODD_IN = 1536

NT = (((1,), (1,)), ((), ()))
TN = (((0,), (0,)), ((), ()))
NN = (((1,), (0,)), ((), ()))


def _dot(a, b, dims=NN):
    return lax.dot_general(a, b, dims, preferred_element_type=F32)


def _params(sem=None, vmem=VMEM_SMALL):
    return pltpu.CompilerParams(dimension_semantics=sem, vmem_limit_bytes=vmem)


def _pick(n, prefs):
    for p in prefs:
        if n % p == 0:
            return p
    return n


def _row_sum(x):
    return jnp.sum(x, axis=0, keepdims=True)


def _all_sum(x):
    return jnp.sum(jnp.sum(x, axis=0, keepdims=True), axis=1, keepdims=True)


def _sigmoid(x):
    return 1.0 / (1.0 + jnp.exp(-x))


SMEM_SPEC = pl.BlockSpec(memory_space=pltpu.SMEM)


def _mm(a, b, *, name, ta=False, tb=False, out_dtypes=(F32,), extras=(), epilogue=None,
        tm=1024, tn=512, tk=1024):
    M, K = (a.shape[1], a.shape[0]) if ta else a.shape
    N = b.shape[0] if tb else b.shape[1]
    assert K == (b.shape[1] if tb else b.shape[0])
    tm = _pick(M, (tm, 512, 256, 128))
    tn = _pick(N, (tn, 512, 384, 256, 128))
    tk = _pick(K, (tk, 1536, 512, 256, 128))
    nk = K // tk
    ne, no = len(extras), len(out_dtypes)
    dims = (((0 if ta else 1,), (1 if tb else 0,)), ((), ()))
    if epilogue is None:
        epilogue = lambda acc: (acc,)

    def body(a_ref, b_ref, *rest):
        extra_refs, out_refs = rest[:ne], rest[ne:ne + no]

        def finish(acc):
            outs = epilogue(acc, *[r[...] for r in extra_refs])
            for o_ref, o in zip(out_refs, outs):
                o_ref[...] = o.astype(o_ref.dtype)

        part = _dot(a_ref[...], b_ref[...], dims)
        if nk == 1:
            finish(part)
        else:
            acc_ref = rest[-1]
            k = pl.program_id(2)

            @pl.when(k == 0)
            def _():
                acc_ref[...] = part

            @pl.when(k > 0)
            def _():
                acc_ref[...] += part

            @pl.when(k == nk - 1)
            def _():
                finish(acc_ref[...])

    a_spec = pl.BlockSpec((tk, tm), lambda i, j, k: (k, i)) if ta else pl.BlockSpec((tm, tk), lambda i, j, k: (i, k))
    b_spec = pl.BlockSpec((tn, tk), lambda i, j, k: (j, k)) if tb else pl.BlockSpec((tk, tn), lambda i, j, k: (k, j))
    o_spec = pl.BlockSpec((tm, tn), lambda i, j, k: (i, j))
    outs = pl.pallas_call(
        body, name=name, grid=(M // tm, N // tn, nk),
        in_specs=[a_spec, b_spec] + [o_spec] * ne,
        out_specs=[o_spec] * no,
        out_shape=[jax.ShapeDtypeStruct((M, N), dt) for dt in out_dtypes],
        scratch_shapes=[pltpu.VMEM((tm, tn), F32)] if nk > 1 else [],
        compiler_params=_params(("parallel", "parallel", "arbitrary")),
    )(a, b, *extras)
    return outs[0] if no == 1 else outs


def _rms_fwd(x, g, name):
    S, Dm = x.shape
    tr = _pick(S, (512,))

    def body(x_ref, g_ref, o_ref):
        xv = x_ref[...]
        r = lax.rsqrt(jnp.mean(xv * xv, axis=-1, keepdims=True) + EPS)
        o_ref[...] = (xv * r * g_ref[...]).astype(o_ref.dtype)

    row = pl.BlockSpec((tr, Dm), lambda i: (i, 0))
    return pl.pallas_call(
        body, name=name, grid=(S // tr,),
        in_specs=[row, pl.BlockSpec((1, Dm), lambda i: (0, 0))], out_specs=row,
        out_shape=jax.ShapeDtypeStruct((S, Dm), BF), compiler_params=_params(("parallel",)),
    )(x, g)


def _rms_bwd(x, g, dh, dres, name):
    S, Dm = x.shape
    tr = _pick(S, (512,))

    def body(x_ref, g_ref, dh_ref, dres_ref, dx_ref, dxb_ref, dg_ref):
        xv = x_ref[...]
        r = lax.rsqrt(jnp.mean(xv * xv, axis=-1, keepdims=True) + EPS)
        xh = xv * r
        dy = dh_ref[...].astype(F32)
        dxh = dy * g_ref[...]
        dx = r * (dxh - xh * jnp.mean(dxh * xh, axis=-1, keepdims=True)) + dres_ref[...]
        dx_ref[...] = dx
        dxb_ref[...] = dx.astype(dxb_ref.dtype)

        @pl.when(pl.program_id(0) == 0)
        def _():
            dg_ref[...] = jnp.zeros_like(dg_ref)

        dg_ref[...] += _row_sum(dy * xh)

    row = pl.BlockSpec((tr, Dm), lambda i: (i, 0))
    vec = pl.BlockSpec((1, Dm), lambda i: (0, 0))
    return pl.pallas_call(
        body, name=name, grid=(S // tr,),
        in_specs=[row, vec, row, row], out_specs=[row, row, vec],
        out_shape=[jax.ShapeDtypeStruct((S, Dm), F32), jax.ShapeDtypeStruct((S, Dm), BF),
                   jax.ShapeDtypeStruct((1, Dm), F32)],
        compiler_params=_params(("arbitrary",)),
    )(x, g, dh, dres)


def _loss_and_grad(y, target):
    S, Dm = y.shape
    tr = _pick(S, (512,))
    n = S // tr

    def body(y_ref, t_ref, dy_ref, dyb_ref, loss_ref, acc_ref):
        i = pl.program_id(0)
        e = y_ref[...] - t_ref[...]
        d = e * (1.0 / Dm)
        dy_ref[...] = d
        dyb_ref[...] = d.astype(dyb_ref.dtype)

        @pl.when(i == 0)
        def _():
            acc_ref[...] = jnp.zeros_like(acc_ref)

        acc_ref[...] += _row_sum(e * e)

        @pl.when(i == n - 1)
        def _():
            loss_ref[...] = jnp.broadcast_to(_all_sum(acc_ref[...]) * (0.5 / Dm), loss_ref.shape)

    row = pl.BlockSpec((tr, Dm), lambda i: (i, 0))
    return pl.pallas_call(
        body, name="loss_head", grid=(n,),
        in_specs=[row, row], out_specs=[row, row, pl.BlockSpec((8, 128), lambda i: (0, 0))],
        out_shape=[jax.ShapeDtypeStruct((S, Dm), F32), jax.ShapeDtypeStruct((S, Dm), BF),
                   jax.ShapeDtypeStruct((8, 128), F32)],
        scratch_shapes=[pltpu.VMEM((1, Dm), F32)],
        compiler_params=_params(("arbitrary",)),
    )(y, target)


def _partner(x, half):
    if half == 64:
        return pltpu.roll(x, 64, 1)
    lane = lax.broadcasted_iota(jnp.int32, x.shape, 1)
    return jnp.where((lane % (2 * half)) < half, pltpu.roll(x, 128 - half, 1), pltpu.roll(x, half, 1))


def _rope(x, cos, sin, half):
    return x * cos + _partner(x, half) * sin


def _rope_t(dy, cos, sin, half):
    return dy * cos - _partner(dy, half) * sin


def _head_norm(x):
    r = lax.rsqrt(jnp.mean(x * x, axis=-1, keepdims=True) + EPS)
    return x * r, r


def _head_norm_bwd(dxh, xh, r):
    return r * (dxh - xh * jnp.mean(dxh * xh, axis=-1, keepdims=True))


def _cols(ref, off, width=HEAD_DIM):
    return ref[:, off:off + width]


def _prep_even_fwd(proj, cos, sin, gq, gk):
    S = proj.shape[0]
    tr = _pick(S, (256,))

    def body(p_ref, cos_ref, sin_ref, gq_ref, gk_ref, qr_ref, kr_ref, vr_ref, qs_ref, ks_ref, vs_ref):
        cos_v, sin_v = cos_ref[...], sin_ref[...]
        for h in range(RET_HEADS):
            o = h * RET_DK
            qr_ref[:, o:o + RET_DK] = _rope(_cols(p_ref, OFF_QA + o), cos_v, sin_v, 64).astype(qr_ref.dtype)
            kr_ref[:, o:o + RET_DK] = (_rope(_cols(p_ref, OFF_KA + o), cos_v, sin_v, 64) * RET_SCALE).astype(kr_ref.dtype)
        vr_ref[...] = p_ref[:, OFF_VA:OFF_VA + 1024].astype(vr_ref.dtype)
        for h in range(SWA_HEADS):
            o = h * HEAD_DIM
            xh, _ = _head_norm(_cols(p_ref, OFF_QB + o))
            qs_ref[:, o:o + HEAD_DIM] = (xh * gq_ref[...] * ATT_SCALE).astype(qs_ref.dtype)
        for h in range(SWA_KV_HEADS):
            o = h * HEAD_DIM
            xh, _ = _head_norm(_cols(p_ref, OFF_KB + o))
            ks_ref[:, o:o + HEAD_DIM] = (xh * gk_ref[...]).astype(ks_ref.dtype)
        vs_ref[...] = p_ref[:, OFF_VB:OFF_VB + 256].astype(vs_ref.dtype)

    def row(w):
        return pl.BlockSpec((tr, w), lambda i: (i, 0))

    vec = pl.BlockSpec((1, HEAD_DIM), lambda i: (0, 0))
    widths = (512, 512, 1024, 1024, 256, 256)
    return pl.pallas_call(
        body, name="prep_even_fwd", grid=(S // tr,),
        in_specs=[row(EVEN_IN), row(128), row(128), vec, vec],
        out_specs=[row(w) for w in widths],
        out_shape=[jax.ShapeDtypeStruct((S, w), BF) for w in widths],
        compiler_params=_params(("parallel",)),
    )(proj, cos, sin, gq, gk)


def _prep_even_bwd(proj, cos, sin, gq, gk, dqr, dkr, dvr, dga, dqs, dks, dvs):
    S = proj.shape[0]
    tr = _pick(S, (256,))

    def body(p_ref, cos_ref, sin_ref, gq_ref, gk_ref, dqr_ref, dkr_ref, dvr_ref, dga_ref, dqs_ref, dks_ref,
             dvs_ref, dp_ref, dgq_ref, dgk_ref):
        cos_v, sin_v = cos_ref[...], sin_ref[...]
        dt = dp_ref.dtype
        for h in range(RET_HEADS):
            o = h * RET_DK
            dp_ref[:, OFF_QA + o:OFF_QA + o + RET_DK] = _rope_t(_cols(dqr_ref, o), cos_v, sin_v, 64).astype(dt)
            dp_ref[:, OFF_KA + o:OFF_KA + o + RET_DK] = _rope_t(_cols(dkr_ref, o) * RET_SCALE, cos_v, sin_v, 64).astype(dt)
        dp_ref[:, OFF_VA:OFF_VA + 1024] = dvr_ref[...].astype(dt)
        dp_ref[:, OFF_GA:OFF_GA + 1024] = dga_ref[...].astype(dt)
        dgq = jnp.zeros((1, HEAD_DIM), F32)
        for h in range(SWA_HEADS):
            o = h * HEAD_DIM
            xh, r = _head_norm(_cols(p_ref, OFF_QB + o))
            dy = _cols(dqs_ref, o) * ATT_SCALE
            dgq = dgq + _row_sum(dy * xh)
            dp_ref[:, OFF_QB + o:OFF_QB + o + HEAD_DIM] = _head_norm_bwd(dy * gq_ref[...], xh, r).astype(dt)
        dgk = jnp.zeros((1, HEAD_DIM), F32)
        for h in range(SWA_KV_HEADS):
            o = h * HEAD_DIM
            xh, r = _head_norm(_cols(p_ref, OFF_KB + o))
            dy = _cols(dks_ref, o)
            dgk = dgk + _row_sum(dy * xh)
            dp_ref[:, OFF_KB + o:OFF_KB + o + HEAD_DIM] = _head_norm_bwd(dy * gk_ref[...], xh, r).astype(dt)
        dp_ref[:, OFF_VB:OFF_VB + 256] = dvs_ref[...].astype(dt)

        @pl.when(pl.program_id(0) == 0)
        def _():
            dgq_ref[...] = jnp.zeros_like(dgq_ref)
            dgk_ref[...] = jnp.zeros_like(dgk_ref)

        dgq_ref[...] += dgq
        dgk_ref[...] += dgk

    def row(w):
        return pl.BlockSpec((tr, w), lambda i: (i, 0))

    vec = pl.BlockSpec((1, HEAD_DIM), lambda i: (0, 0))
    return pl.pallas_call(
        body, name="prep_even_bwd", grid=(S // tr,),
        in_specs=[row(EVEN_IN), row(128), row(128), vec, vec, row(512), row(512), row(1024), row(1024),
                  row(1024), row(256), row(256)],
        out_specs=[row(EVEN_IN), vec, vec],
        out_shape=[jax.ShapeDtypeStruct((S, EVEN_IN), BF), jax.ShapeDtypeStruct((1, HEAD_DIM), F32),
                   jax.ShapeDtypeStruct((1, HEAD_DIM), F32)],
        compiler_params=_params(("arbitrary",)),
    )(proj, cos, sin, gq, gk, dqr, dkr, dvr, dga, dqs, dks, dvs)


def _log_sigmoid_tile(logit_tile):
    def body(x_ref, o_ref):
        xv = x_ref[...]
        t = jnp.exp(-jnp.abs(xv))
        log1p_t = jnp.where(t < 1e-3, t * (1.0 - 0.5 * t), jnp.log(1.0 + t))
        o_ref[...] = jnp.minimum(xv, 0.0) - log1p_t

    full = pl.BlockSpec((8, 128), lambda: (0, 0))
    return pl.pallas_call(body, name="log_sigmoid", in_specs=[full], out_specs=full,
                          out_shape=jax.ShapeDtypeStruct((8, 128), F32))(logit_tile)


def _decay(diff, lf, lb):
    return jnp.exp(jnp.where(diff >= 0, lf * diff, -(lb * diff)))


def _ret_fwd(q, k, v, lg):
    S = q.shape[0]
    tq = _pick(S, (512,))
    tk = _pick(S, (512,))
    nk = S // tk

    def body(lg_ref, q_ref, k_ref, v_ref, y_ref):
        h, i = pl.program_id(0), pl.program_id(1)
        lf, lb = lg_ref[0, h], lg_ref[1, h]
        qv = q_ref[...]
        base = (lax.broadcasted_iota(jnp.int32, (tq, tk), 0) - lax.broadcasted_iota(jnp.int32, (tq, tk), 1)).astype(F32)

        def step(j, acc):
            rows = pl.ds(pl.multiple_of(j * tk, tk), tk)
            s = _dot(qv, k_ref[rows, :], NT)
            diff = base + (i * tq - j * tk).astype(F32)
            return acc + _dot((s * _decay(diff, lf, lb)).astype(v_ref.dtype), v_ref[rows, :])

        y_ref[...] = lax.fori_loop(0, nk, step, jnp.zeros((tq, RET_DV), F32))

    return pl.pallas_call(
        body, name="retention_fwd", grid=(RET_HEADS, S // tq),
        in_specs=[SMEM_SPEC, pl.BlockSpec((tq, RET_DK), lambda h, i: (i, h)),
                  pl.BlockSpec((S, RET_DK), lambda h, i: (0, h)), pl.BlockSpec((S, RET_DV), lambda h, i: (0, h))],
        out_specs=pl.BlockSpec((tq, RET_DV), lambda h, i: (i, h)),
        out_shape=jax.ShapeDtypeStruct((S, RET_HEADS * RET_DV), F32),
        compiler_params=_params(("parallel", "parallel")),
    )(lg, q, k, v)


def _ret_bwd(q, k, v, dy, lg, logit):
    S = q.shape[0]
    tq = _pick(S, (512,))
    tk = _pick(S, (512,))
    nq, nk = S // tq, S // tk

    def body(lg_ref, logit_ref, q_ref, dy_ref, k_ref, v_ref, dq_ref, dk_ref, dv_ref, dlg_ref, accw_ref, accf_ref):
        h, j = pl.program_id(0), pl.program_id(1)
        lf, lb = lg_ref[0, h], lg_ref[1, h]

        @pl.when(j == 0)
        def _():
            dq_ref[...] = jnp.zeros_like(dq_ref)
            accw_ref[...] = jnp.zeros_like(accw_ref)
            accf_ref[...] = jnp.zeros_like(accf_ref)

        kj, vj = k_ref[...], v_ref[...]
        base = (lax.broadcasted_iota(jnp.int32, (tq, tk), 0) - lax.broadcasted_iota(jnp.int32, (tq, tk), 1)).astype(F32)

        def step(i, carry):
            dk, dv = carry
            rows = pl.ds(pl.multiple_of(i * tq, tq), tq)
            qi, dyi = q_ref[rows, :], dy_ref[rows, :]
            diff = base + (i * tq - j * tk).astype(F32)
            dec = _decay(diff, lf, lb)
            sc = _dot(qi, kj, NT) * dec
            dp = _dot(dyi, vj, NT)
            da = (dp * dec).astype(qi.dtype)
            dq_ref[rows, :] += _dot(da, kj)
            dk = dk + _dot(da, qi, TN)
            dv = dv + _dot(sc.astype(dyi.dtype), dyi, TN)
            w = sc * dp * diff
            accw_ref[...] += jnp.sum(w.reshape(tq // 8, 8, tk), axis=0)
            accf_ref[...] += jnp.sum(jnp.where(diff >= 0, w, 0.0).reshape(tq // 8, 8, tk), axis=0)
            return dk, dv

        dk, dv = lax.fori_loop(0, nq, step, (jnp.zeros((tk, RET_DK), F32), jnp.zeros((tk, RET_DV), F32)))
        dk_ref[...] = dk
        dv_ref[...] = dv

        @pl.when(j == nk - 1)
        def _():
            tot_w, tot_f = _all_sum(accw_ref[...]), _all_sum(accf_ref[...])
            for d, val in ((0, tot_f), (1, tot_f - tot_w)):
                gate = 1.0 / (1.0 + jnp.exp(jnp.full((8, 128), logit_ref[d, h], F32)))
                dlg_ref[0, d] = jnp.broadcast_to(val, (8, 128)) * gate

    return pl.pallas_call(
        body, name="retention_bwd", grid=(RET_HEADS, nk),
        in_specs=[SMEM_SPEC, SMEM_SPEC,
                  pl.BlockSpec((S, RET_DK), lambda h, j: (0, h)), pl.BlockSpec((S, RET_DV), lambda h, j: (0, h)),
                  pl.BlockSpec((tk, RET_DK), lambda h, j: (j, h)), pl.BlockSpec((tk, RET_DV), lambda h, j: (j, h))],
        out_specs=[pl.BlockSpec((S, RET_DK), lambda h, j: (0, h)), pl.BlockSpec((tk, RET_DK), lambda h, j: (j, h)),
                   pl.BlockSpec((tk, RET_DV), lambda h, j: (j, h)),
                   pl.BlockSpec((1, 2, 8, 128), lambda h, j: (h, 0, 0, 0))],
        out_shape=[jax.ShapeDtypeStruct((S, RET_HEADS * RET_DK), F32), jax.ShapeDtypeStruct((S, RET_HEADS * RET_DK), F32),
                   jax.ShapeDtypeStruct((S, RET_HEADS * RET_DV), F32),
                   jax.ShapeDtypeStruct((RET_HEADS, 2, 8, 128), F32)],
        scratch_shapes=[pltpu.VMEM((8, tk), F32), pltpu.VMEM((8, tk), F32)],
        compiler_params=_params(("parallel", "arbitrary")),
    )(lg, logit, q, dy, k, v)


def _ret_post_fwd(y, proj, gn):
    S = y.shape[0]
    tr = _pick(S, (512,))

    def body(y_ref, g_ref, gn_ref, o_ref):
        for h in range(RET_HEADS):
            o = h * RET_DV
            yh, _ = _head_norm(_cols(y_ref, o, RET_DV))
            gate = _cols(g_ref, o, RET_DV)
            o_ref[:, o:o + RET_DV] = (gate * _sigmoid(gate) * (yh * gn_ref[:, o:o + RET_DV])).astype(o_ref.dtype)

    row = pl.BlockSpec((tr, 1024), lambda i: (i, 0))
    return pl.pallas_call(
        body, name="retention_post_fwd", grid=(S // tr,),
        in_specs=[row, pl.BlockSpec((tr, 1024), lambda i: (i, OFF_GA // 1024)), pl.BlockSpec((1, 1024), lambda i: (0, 0))],
        out_specs=row, out_shape=jax.ShapeDtypeStruct((S, 1024), BF), compiler_params=_params(("parallel",)),
    )(y, proj, gn)


def _ret_post_bwd(y, proj, gn, do):
    S = y.shape[0]
    tr = _pick(S, (512,))

    def body(y_ref, g_ref, gn_ref, do_ref, dy_ref, dg_ref, dgn_ref):
        @pl.when(pl.program_id(0) == 0)
        def _():
            dgn_ref[...] = jnp.zeros_like(dgn_ref)

        for h in range(RET_HEADS):
            o = h * RET_DV
            yh, r = _head_norm(_cols(y_ref, o, RET_DV))
            gate = _cols(g_ref, o, RET_DV)
            gnh = gn_ref[:, o:o + RET_DV]
            dout = _cols(do_ref, o, RET_DV).astype(F32)
            sg = _sigmoid(gate)
            dz = dout * (gate * sg)
            dg_ref[:, o:o + RET_DV] = dout * (yh * gnh) * (sg * (1.0 + gate * (1.0 - sg)))
            dgn_ref[:, o:o + RET_DV] += _row_sum(dz * yh)
            dy_ref[:, o:o + RET_DV] = _head_norm_bwd(dz * gnh, yh, r).astype(dy_ref.dtype)

    row = pl.BlockSpec((tr, 1024), lambda i: (i, 0))
    vec = pl.BlockSpec((1, 1024), lambda i: (0, 0))
    return pl.pallas_call(
        body, name="retention_post_bwd", grid=(S // tr,),
        in_specs=[row, pl.BlockSpec((tr, 1024), lambda i: (i, OFF_GA // 1024)), vec, row],
        out_specs=[row, row, vec],
        out_shape=[jax.ShapeDtypeStruct((S, 1024), BF), jax.ShapeDtypeStruct((S, 1024), F32),
                   jax.ShapeDtypeStruct((1, 1024), F32)],
        compiler_params=_params(("arbitrary",)),
    )(y, proj, gn, do)


def _t5_bucket_map():
    r = jnp.arange(BLOCK)
    j = jnp.arange(3 * BLOCK)
    rel = j[None, :] - BLOCK - r[:, None]
    nb = T5_BUCKETS // 2
    max_exact = nb // 2
    ret = jnp.where(rel > 0, nb, 0)
    n = jnp.abs(rel)
    nf = jnp.maximum(n, 1).astype(jnp.float32)
    large = max_exact + (jnp.log(nf / max_exact) / math.log(T5_MAX_DIST / max_exact)
                         * (nb - max_exact)).astype(jnp.int32)
    large = jnp.minimum(large, nb - 1)
    bucket = ret + jnp.where(n < max_exact, n, large)
    return jnp.where(jnp.abs(rel) <= WINDOW, bucket, -1).astype(jnp.int32)


def _t5_bias(table, bucket):
    def body(t_ref, b_ref, o_ref):
        bk = b_ref[...]
        for h in range(SWA_HEADS):
            acc = jnp.full(bk.shape, NEG_INF, F32)
            for b in range(T5_BUCKETS):
                acc = jnp.where(bk == b, t_ref[b, h], acc)
            o_ref[h] = acc

    return pl.pallas_call(
        body, name="t5_bias", in_specs=[SMEM_SPEC, pl.BlockSpec((BLOCK, 3 * BLOCK), lambda: (0, 0))],
        out_specs=pl.BlockSpec((SWA_HEADS, BLOCK, 3 * BLOCK), lambda: (0, 0, 0)),
        out_shape=jax.ShapeDtypeStruct((SWA_HEADS, BLOCK, 3 * BLOCK), F32),
    )(table, bucket)


def _t5_table_grad(dbias, bucket):
    def body(d_ref, b_ref, o_ref):
        bk = b_ref[...]
        lane = lax.broadcasted_iota(jnp.int32, (1, 128), 1)
        for b in range(T5_BUCKETS):
            hit = bk == b
            row = jnp.zeros((1, 128), F32)
            for h in range(SWA_HEADS):
                row = row + jnp.where(lane == h, _all_sum(jnp.where(hit, d_ref[h], 0.0)), 0.0)
            o_ref[b:b + 1, :] = row

    return pl.pallas_call(
        body, name="t5_table_grad",
        in_specs=[pl.BlockSpec((SWA_HEADS, BLOCK, 3 * BLOCK), lambda: (0, 0, 0)), pl.BlockSpec((BLOCK, 3 * BLOCK), lambda: (0, 0))],
        out_specs=pl.BlockSpec((T5_BUCKETS, 128), lambda: (0, 0)),
        out_shape=jax.ShapeDtypeStruct((T5_BUCKETS, 128), F32),
    )(dbias, bucket)


def _swa_scores(i, nb, q, kw, bias_h, sink_h):
    s = _dot(q, kw, NT) + bias_h
    col = lax.broadcasted_iota(jnp.int32, s.shape, 1)
    first_col = jnp.where(i == 0, BLOCK, 0)
    end_col = jnp.where(i == nb - 1, 2 * BLOCK, 3 * BLOCK)
    s = jnp.where((col < first_col) | (col >= end_col), NEG_INF, s)
    m = jnp.maximum(jnp.max(s, axis=-1, keepdims=True), sink_h)
    p = jnp.exp(s - m)
    e_sink = jnp.exp(sink_h - m)
    inv = 1.0 / (jnp.sum(p, axis=-1, keepdims=True) + e_sink)
    return p * inv, e_sink * inv


def _swa_window(ref, i, nb, off):
    prev, nxt = jnp.maximum(i - 1, 0), jnp.minimum(i + 1, nb - 1)
    rows = [pl.ds(pl.multiple_of(b * BLOCK, BLOCK), BLOCK) for b in (prev, i, nxt)]
    return jnp.concatenate([ref[r, off:off + HEAD_DIM] for r in rows], axis=0), rows


def _swa_fwd(q, k, v, bias, sink):
    S = q.shape[0]
    nb = S // BLOCK
    G = SWA_HEADS // SWA_KV_HEADS

    def body(sink_ref, q_ref, k_ref, v_ref, bias_ref, o_ref):
        i = pl.program_id(0)
        for kh in range(SWA_KV_HEADS):
            kw, _ = _swa_window(k_ref, i, nb, kh * HEAD_DIM)
            vw, _ = _swa_window(v_ref, i, nb, kh * HEAD_DIM)
            for g in range(G):
                h = kh * G + g
                p, _ = _swa_scores(i, nb, _cols(q_ref, h * HEAD_DIM), kw, bias_ref[h], sink_ref[0, h])
                o_ref[:, h * HEAD_DIM:(h + 1) * HEAD_DIM] = _dot(p.astype(vw.dtype), vw).astype(o_ref.dtype)

    full_kv = pl.BlockSpec((S, SWA_KV_HEADS * HEAD_DIM), lambda i: (0, 0))
    return pl.pallas_call(
        body, name="swa_fwd", grid=(nb,),
        in_specs=[SMEM_SPEC, pl.BlockSpec((BLOCK, 1024), lambda i: (i, 0)), full_kv, full_kv,
                  pl.BlockSpec((SWA_HEADS, BLOCK, 3 * BLOCK), lambda i: (0, 0, 0))],
        out_specs=pl.BlockSpec((BLOCK, 1024), lambda i: (i, 0)),
        out_shape=jax.ShapeDtypeStruct((S, 1024), BF), compiler_params=_params(("parallel",)),
    )(sink, q, k, v, bias)


def _swa_bwd(q, k, v, do, bias, sink):
    S = q.shape[0]
    nb = S // BLOCK
    G = SWA_HEADS // SWA_KV_HEADS

    def body(sink_ref, q_ref, k_ref, v_ref, do_ref, bias_ref, dq_ref, dk_ref, dv_ref, dbias_ref, dsink_ref):
        i = pl.program_id(0)

        @pl.when(i == 0)
        def _():
            dk_ref[...] = jnp.zeros_like(dk_ref)
            dv_ref[...] = jnp.zeros_like(dv_ref)
            dbias_ref[...] = jnp.zeros_like(dbias_ref)
            dsink_ref[...] = jnp.zeros_like(dsink_ref)

        for kh in range(SWA_KV_HEADS):
            off = kh * HEAD_DIM
            kw, rows = _swa_window(k_ref, i, nb, off)
            vw, _ = _swa_window(v_ref, i, nb, off)
            dkw = jnp.zeros((3 * BLOCK, HEAD_DIM), F32)
            dvw = jnp.zeros((3 * BLOCK, HEAD_DIM), F32)
            for g in range(G):
                h = kh * G + g
                qh = _cols(q_ref, h * HEAD_DIM)
                p, p_sink = _swa_scores(i, nb, qh, kw, bias_ref[h], sink_ref[0, h])
                doh = _cols(do_ref, h * HEAD_DIM).astype(vw.dtype)
                dp = _dot(doh, vw, NT)
                delta = jnp.sum(p * dp, axis=-1, keepdims=True)
                ds = p * (dp - delta)
                dsb = ds.astype(qh.dtype)
                dq_ref[:, h * HEAD_DIM:(h + 1) * HEAD_DIM] = _dot(dsb, kw)
                dkw = dkw + _dot(dsb, qh, TN)
                dvw = dvw + _dot(p.astype(doh.dtype), doh, TN)
                dbias_ref[h] += ds
                dsink_ref[h:h + 1, :] += jnp.broadcast_to(-_row_sum(p_sink * delta), (1, 128))
            for b, r in enumerate(rows):
                dk_ref[r, off:off + HEAD_DIM] += dkw[b * BLOCK:(b + 1) * BLOCK]
                dv_ref[r, off:off + HEAD_DIM] += dvw[b * BLOCK:(b + 1) * BLOCK]

    full_kv = pl.BlockSpec((S, SWA_KV_HEADS * HEAD_DIM), lambda i: (0, 0))
    blk = pl.BlockSpec((BLOCK, 1024), lambda i: (i, 0))
    bias_spec = pl.BlockSpec((SWA_HEADS, BLOCK, 3 * BLOCK), lambda i: (0, 0, 0))
    return pl.pallas_call(
        body, name="swa_bwd", grid=(nb,),
        in_specs=[SMEM_SPEC, blk, full_kv, full_kv, blk, bias_spec],
        out_specs=[blk, full_kv, full_kv, bias_spec, pl.BlockSpec((8, 128), lambda i: (0, 0))],
        out_shape=[jax.ShapeDtypeStruct((S, 1024), F32), jax.ShapeDtypeStruct((S, 256), F32),
                   jax.ShapeDtypeStruct((S, 256), F32),
                   jax.ShapeDtypeStruct((SWA_HEADS, BLOCK, 3 * BLOCK), F32), jax.ShapeDtypeStruct((8, 128), F32)],
        compiler_params=_params(("arbitrary",), VMEM_LARGE),
    )(sink, q, k, v, do, bias)


def _prep_odd_fwd(proj, cos, sin, gq, gk):
    S = proj.shape[0]
    tr = _pick(S, (512,))

    def body(p_ref, cos_ref, sin_ref, gq_ref, gk_ref, q_ref, k_ref, v_ref):
        cos_v, sin_v = cos_ref[...], sin_ref[...]
        for h in range(AX_HEADS):
            o = h * HEAD_DIM
            xh, _ = _head_norm(_cols(p_ref, o))
            q_ref[:, o:o + HEAD_DIM] = (_rope(xh * gq_ref[...], cos_v, sin_v, 32) * ATT_SCALE).astype(q_ref.dtype)
        for h in range(AX_KV_HEADS):
            o = h * HEAD_DIM
            xh, _ = _head_norm(_cols(p_ref, 1024 + o))
            k_ref[:, o:o + HEAD_DIM] = _rope(xh * gk_ref[...], cos_v, sin_v, 32).astype(k_ref.dtype)
        v_ref[...] = p_ref[:, 1280:1536].astype(v_ref.dtype)

    def row(w):
        return pl.BlockSpec((tr, w), lambda i: (i, 0))

    vec = pl.BlockSpec((1, HEAD_DIM), lambda i: (0, 0))
    return pl.pallas_call(
        body, name="prep_odd_fwd", grid=(S // tr,),
        in_specs=[row(ODD_IN), row(128), row(128), vec, vec], out_specs=[row(1024), row(256), row(256)],
        out_shape=[jax.ShapeDtypeStruct((S, w), BF) for w in (1024, 256, 256)],
        compiler_params=_params(("parallel",)),
    )(proj, cos, sin, gq, gk)


def _prep_odd_bwd(proj, cos, sin, gq, gk, dq, dk, dv):
    S = proj.shape[0]
    tr = _pick(S, (512,))

    def body(p_ref, cos_ref, sin_ref, gq_ref, gk_ref, dq_ref, dk_ref, dv_ref, dp_ref, dgq_ref, dgk_ref):
        cos_v, sin_v = cos_ref[...], sin_ref[...]
        dt = dp_ref.dtype
        dgq = jnp.zeros((1, HEAD_DIM), F32)
        for h in range(AX_HEADS):
            o = h * HEAD_DIM
            xh, r = _head_norm(_cols(p_ref, o))
            dy = _rope_t(_cols(dq_ref, o) * ATT_SCALE, cos_v, sin_v, 32)
            dgq = dgq + _row_sum(dy * xh)
            dp_ref[:, o:o + HEAD_DIM] = _head_norm_bwd(dy * gq_ref[...], xh, r).astype(dt)
        dgk = jnp.zeros((1, HEAD_DIM), F32)
        for h in range(AX_KV_HEADS):
            o = h * HEAD_DIM
            xh, r = _head_norm(_cols(p_ref, 1024 + o))
            dy = _rope_t(_cols(dk_ref, o), cos_v, sin_v, 32)
            dgk = dgk + _row_sum(dy * xh)
            dp_ref[:, 1024 + o:1024 + o + HEAD_DIM] = _head_norm_bwd(dy * gk_ref[...], xh, r).astype(dt)
        dp_ref[:, 1280:1536] = dv_ref[...].astype(dt)

        @pl.when(pl.program_id(0) == 0)
        def _():
            dgq_ref[...] = jnp.zeros_like(dgq_ref)
            dgk_ref[...] = jnp.zeros_like(dgk_ref)

        dgq_ref[...] += dgq
        dgk_ref[...] += dgk

    def row(w):
        return pl.BlockSpec((tr, w), lambda i: (i, 0))

    vec = pl.BlockSpec((1, HEAD_DIM), lambda i: (0, 0))
    return pl.pallas_call(
        body, name="prep_odd_bwd", grid=(S // tr,),
        in_specs=[row(ODD_IN), row(128), row(128), vec, vec, row(1024), row(256), row(256)],
        out_specs=[row(ODD_IN), vec, vec],
        out_shape=[jax.ShapeDtypeStruct((S, ODD_IN), BF), jax.ShapeDtypeStruct((1, HEAD_DIM), F32),
                   jax.ShapeDtypeStruct((1, HEAD_DIM), F32)],
        compiler_params=_params(("arbitrary",)),
    )(proj, cos, sin, gq, gk, dq, dk, dv)


def _flash_fwd(q, k, v):
    S = q.shape[0]
    tq = _pick(S, (512,))
    tk = _pick(S, (512,))
    nk = S // tk
    G = AX_HEADS // AX_KV_HEADS

    def body(q_ref, k_ref, v_ref, o_ref, lse_ref):
        qv = q_ref[...]

        def step(j, carry):
            m, l, acc = carry
            rows = pl.ds(pl.multiple_of(j * tk, tk), tk)
            s = _dot(qv, k_ref[rows, :], NT)
            m_new = jnp.maximum(m, jnp.max(s, axis=-1, keepdims=True))
            alpha = jnp.exp(m - m_new)
            p = jnp.exp(s - m_new)
            l = alpha * l + jnp.sum(p, axis=-1, keepdims=True)
            acc = alpha * acc + _dot(p.astype(v_ref.dtype), v_ref[rows, :])
            return m_new, l, acc

        init = (jnp.full((tq, 1), NEG_INF, F32), jnp.zeros((tq, 1), F32), jnp.zeros((tq, HEAD_DIM), F32))
        m, l, acc = lax.fori_loop(0, nk, step, init)
        o_ref[...] = (acc / l).astype(o_ref.dtype)
        lse_ref[0] = jnp.broadcast_to(m + jnp.log(l), (tq, 128))

    return pl.pallas_call(
        body, name="flash_fwd", grid=(AX_HEADS, S // tq),
        in_specs=[pl.BlockSpec((tq, HEAD_DIM), lambda h, i: (i, h)),
                  pl.BlockSpec((S, HEAD_DIM), lambda h, i: (0, h // G)),
                  pl.BlockSpec((S, HEAD_DIM), lambda h, i: (0, h // G))],
        out_specs=[pl.BlockSpec((tq, HEAD_DIM), lambda h, i: (i, h)),
                   pl.BlockSpec((1, tq, 128), lambda h, i: (h, i, 0))],
        out_shape=[jax.ShapeDtypeStruct((S, AX_HEADS * HEAD_DIM), BF), jax.ShapeDtypeStruct((AX_HEADS, S, 128), F32)],
        compiler_params=_params(("parallel", "parallel")),
    )(q, k, v)


def _flash_bwd(q, k, v, o, do, lse):
    S = q.shape[0]
    tq = _pick(S, (512,))
    tk = _pick(S, (512,))
    nq, nk = S // tq, S // tk
    G = AX_HEADS // AX_KV_HEADS

    def body(q_ref, k_ref, v_ref, o_ref, do_ref, lse_ref, dq_ref, dk_ref, dv_ref):
        g, i = pl.program_id(1), pl.program_id(2)

        @pl.when((g == 0) & (i == 0))
        def _():
            dk_ref[...] = jnp.zeros_like(dk_ref)
            dv_ref[...] = jnp.zeros_like(dv_ref)

        qv = q_ref[...]
        do_f = do_ref[...].astype(F32)
        dob = do_f.astype(qv.dtype)
        delta = jnp.sum(do_f * o_ref[...].astype(F32), axis=-1, keepdims=True)
        lse_col = lse_ref[0][:, 0:1]

        def step(j, dq):
            rows = pl.ds(pl.multiple_of(j * tk, tk), tk)
            kj, vj = k_ref[rows, :], v_ref[rows, :]
            p = jnp.exp(_dot(qv, kj, NT) - lse_col)
            dp = _dot(dob, vj, NT)
            ds = (p * (dp - delta)).astype(qv.dtype)
            dk_ref[rows, :] += _dot(ds, qv, TN)
            dv_ref[rows, :] += _dot(p.astype(dob.dtype), dob, TN)
            return dq + _dot(ds, kj)

        dq_ref[...] = lax.fori_loop(0, nk, step, jnp.zeros((tq, HEAD_DIM), F32))

    q_spec = pl.BlockSpec((tq, HEAD_DIM), lambda kh, g, i: (i, kh * G + g))
    kv_spec = pl.BlockSpec((S, HEAD_DIM), lambda kh, g, i: (0, kh))
    return pl.pallas_call(
        body, name="flash_bwd", grid=(AX_KV_HEADS, G, nq),
        in_specs=[q_spec, kv_spec, kv_spec, q_spec, q_spec,
                  pl.BlockSpec((1, tq, 128), lambda kh, g, i: (kh * G + g, i, 0))],
        out_specs=[q_spec, kv_spec, kv_spec],
        out_shape=[jax.ShapeDtypeStruct((S, AX_HEADS * HEAD_DIM), F32), jax.ShapeDtypeStruct((S, 256), F32),
                   jax.ShapeDtypeStruct((S, 256), F32)],
        compiler_params=_params(("arbitrary", "arbitrary", "arbitrary"), VMEM_LARGE),
    )(q, k, v, o, do, lse)


def _rope_angles(pos, dim, theta):
    inv = theta ** (-jnp.arange(0, dim, 2, dtype=jnp.float32) / dim)
    return pos.astype(jnp.float32)[:, None] * inv[None, :]


def _rope_tables(S):
    ang = _rope_angles(jnp.arange(S), RET_DK, RET_THETA)
    c, s = jnp.cos(ang), jnp.sin(ang)
    ret = (jnp.concatenate([c, c], -1), jnp.concatenate([-s, s], -1))
    rows = S // GRID_W
    row = jnp.repeat(jnp.arange(rows), GRID_W)
    col = jnp.tile(jnp.arange(GRID_W), rows)
    ar, ac = _rope_angles(row, HEAD_DIM // 2, AX_THETA), _rope_angles(col, HEAD_DIM // 2, AX_THETA)
    cr, sr, cc, sc = jnp.cos(ar), jnp.sin(ar), jnp.cos(ac), jnp.sin(ac)
    ax = (jnp.concatenate([cr, cr, cc, cc], -1), jnp.concatenate([-sr, sr, -sc, sc], -1))
    return ret, ax


def _pad_tile(a):
    return jnp.pad(a.astype(F32), ((0, 8 - a.shape[0]), (0, 128 - a.shape[1])))


def _relu2_epilogue(acc):
    r = jnp.maximum(acc, 0.0)
    return acc, r * r


def _relu2_bwd_epilogue(acc, u):
    return (acc * (2.0 * jnp.maximum(u.astype(F32), 0.0)),)


def _add_epilogue(acc, res):
    return (acc + res,)


def _mlp_fwd(x, g, w_up, w_down, tag):
    h = _rms_fwd(x, g, f"mlp_norm_{tag}")
    u, a = _mm(h, w_up, name=f"mlp_up_{tag}", out_dtypes=(BF, BF), epilogue=_relu2_epilogue)
    y = _mm(a, w_down, name=f"mlp_down_{tag}", extras=(x,), epilogue=_add_epilogue)
    return y, (h, u, a)


def _mlp_bwd(x, g, w_up, w_down, saved, dy, dyb, tag):
    h, u, a = saved
    du = _mm(dyb, w_down, tb=True, name=f"mlp_down_dx_{tag}", out_dtypes=(BF,), extras=(u,), epilogue=_relu2_bwd_epilogue)
    dw_down = _mm(a, dyb, ta=True, name=f"mlp_down_dw_{tag}", out_dtypes=(BF,))
    dw_up = _mm(h, du, ta=True, name=f"mlp_up_dw_{tag}", out_dtypes=(BF,))
    dh = _mm(du, w_up, tb=True, name=f"mlp_up_dx_{tag}")
    dx, dxb, dg = _rms_bwd(x, g, dh, dy, f"mlp_norm_bwd_{tag}")
    return dx, dxb, dg, dw_up, dw_down


def _local_step(x, target, W, P):
    S = x.shape[0]
    (cos_r, sin_r), (cos_a, sin_a) = _rope_tables(S)
    bucket = _t5_bucket_map()
    logit = P["ret_decay_logit"]
    lg = _log_sigmoid_tile(_pad_tile(logit))
    bias = _t5_bias(P["t5_table"], bucket)
    nmix, nmlp = P["norm_mix"], P["norm_mlp"]

    h0 = _rms_fwd(x, nmix[0:1], "mix_norm_0")
    proj_e = _mm(h0, W["in_even"], name="in_even")
    qr, kr, vr, qs, ks, vs = _prep_even_fwd(proj_e, cos_r, sin_r, P["swa_q_norm"], P["swa_k_norm"])
    y_ret = _ret_fwd(qr, kr, vr, lg)
    oa = _ret_post_fwd(y_ret, proj_e, P["ret_norm"])
    ob = _swa_fwd(qs, ks, vs, bias, P["swa_sink"])
    wo_a, wo_b = W["out_even"][:1024], W["out_even"][1024:]
    x1 = _mm(oa, wo_a, name="out_even_a", extras=(x,), epilogue=_add_epilogue)
    x1 = _mm(ob, wo_b, name="out_even_b", extras=(x1,), epilogue=_add_epilogue)
    x2, mlp0 = _mlp_fwd(x1, nmlp[0:1], W["mlp_up"][0], W["mlp_down"][0], "0")

    h2 = _rms_fwd(x2, nmix[1:2], "mix_norm_1")
    proj_o = _mm(h2, W["in_odd"], name="in_odd")
    qx, kx, vx = _prep_odd_fwd(proj_o, cos_a, sin_a, P["ax_q_norm"], P["ax_k_norm"])
    ox, lse = _flash_fwd(qx, kx, vx)
    x3 = _mm(ox, W["out_odd"], name="out_odd", extras=(x2,), epilogue=_add_epilogue)
    x4, mlp1 = _mlp_fwd(x3, nmlp[1:2], W["mlp_up"][1], W["mlp_down"][1], "1")

    d4, d4b, loss_tile = _loss_and_grad(x4, target)

    d3, d3b, dnmlp1, dw_up1, dw_down1 = _mlp_bwd(x3, nmlp[1:2], W["mlp_up"][1], W["mlp_down"][1], mlp1, d4, d4b, "1")
    dox = _mm(d3b, W["out_odd"], tb=True, name="out_odd_dx")
    dw_out_odd = _mm(ox, d3b, ta=True, name="out_odd_dw", out_dtypes=(BF,))
    dqx, dkx, dvx = _flash_bwd(qx, kx, vx, ox, dox, lse)
    dproj_o, dgq_ax, dgk_ax = _prep_odd_bwd(proj_o, cos_a, sin_a, P["ax_q_norm"], P["ax_k_norm"], dqx, dkx, dvx)
    dw_in_odd = _mm(h2, dproj_o, ta=True, name="in_odd_dw", out_dtypes=(BF,))
    dh2 = _mm(dproj_o, W["in_odd"], tb=True, name="in_odd_dx")
    d2, d2b, dnmix1 = _rms_bwd(x2, nmix[1:2], dh2, d3, "mix_norm_bwd_1")

    d1, d1b, dnmlp0, dw_up0, dw_down0 = _mlp_bwd(x1, nmlp[0:1], W["mlp_up"][0], W["mlp_down"][0], mlp0, d2, d2b, "0")
    doa = _mm(d1b, wo_a, tb=True, name="out_even_a_dx")
    dob = _mm(d1b, wo_b, tb=True, name="out_even_b_dx")
    dw_out_even = jnp.concatenate([_mm(oa, d1b, ta=True, name="out_even_a_dw", out_dtypes=(BF,)),
                                   _mm(ob, d1b, ta=True, name="out_even_b_dw", out_dtypes=(BF,))], axis=0)
    dy_ret, dga, dret_norm = _ret_post_bwd(y_ret, proj_e, P["ret_norm"], doa)
    dqr, dkr, dvr, dlogit = _ret_bwd(qr, kr, vr, dy_ret, lg, logit)
    dqs, dks, dvs, dbias, dsink = _swa_bwd(qs, ks, vs, dob, bias, P["swa_sink"])
    dt5 = _t5_table_grad(dbias, bucket)
    dproj_e, dgq_swa, dgk_swa = _prep_even_bwd(proj_e, cos_r, sin_r, P["swa_q_norm"], P["swa_k_norm"],
                                               dqr, dkr, dvr, dga, dqs, dks, dvs)
    dw_in_even = _mm(h0, dproj_e, ta=True, name="in_even_dw", out_dtypes=(BF,))
    dh0 = _mm(dproj_e, W["in_even"], tb=True, name="in_even_dx")
    dx, _, dnmix0 = _rms_bwd(x, nmix[0:1], dh0, d1, "mix_norm_bwd_0")

    dW = {"in_even": dw_in_even, "out_even": dw_out_even, "in_odd": dw_in_odd, "out_odd": dw_out_odd,
          "mlp_up": jnp.stack([dw_up0, dw_up1]), "mlp_down": jnp.stack([dw_down0, dw_down1])}
    dP = {"norm_mix": jnp.concatenate([dnmix0, dnmix1], 0), "norm_mlp": jnp.concatenate([dnmlp0, dnmlp1], 0),
          "ret_decay_logit": dlogit[:, :, 0, 0].T, "ret_norm": dret_norm,
          "swa_q_norm": dgq_swa, "swa_k_norm": dgk_swa, "swa_sink": dsink[:, 0][None, :],
          "t5_table": dt5[:, :SWA_HEADS], "ax_q_norm": dgq_ax, "ax_k_norm": dgk_ax}
    return loss_tile, dx, dW, dP


MESH = pl.DeviceIdType.MESH
ANY_SPEC = pl.BlockSpec(memory_space=pl.ANY)
VMEM_SPEC = pl.BlockSpec(memory_space=pltpu.VMEM)


def _my_place():
    return lax.axis_index("x"), lax.axis_index("y"), lax.axis_index("c")


def _flip(place, k):
    x, y, c = place
    return (1 - x if k & 4 else x, 1 - y if k & 2 else y, 1 - c if k & 1 else c)


def _index(place):
    x, y, c = place
    return 4 * x + 2 * y + c


def _all_gather(shards):
    n = len(shards)

    def body(*refs):
        ins, outs, stage = refs[:n], refs[n:2 * n], refs[2 * n:3 * n]
        send_sems, recv_sems, local_sems = refs[3 * n:]
        me = _my_place()
        sibling = _flip(me, 1)
        chips = [_flip(me, 4), _flip(me, 2), _flip(me, 6)]

        def copy(a, k, block, to, src=None):
            dst = outs[a].at[_index(block)]
            return pltpu.make_async_remote_copy(
                src_ref=dst if src is None else src, dst_ref=dst,
                send_sem=send_sems.at[a, k], recv_sem=recv_sems.at[a, k], device_id=to, device_id_type=MESH)

        first, mine = [], []
        for a in range(n):
            stage[a][...] = ins[a][...].astype(stage[a].dtype)
            mine.append(pltpu.make_async_copy(stage[a], outs[a].at[_index(me)], local_sems.at[a]))
            mine[-1].start()
            first.append(copy(a, 0, me, sibling, src=stage[a]))
            first += [copy(a, 1 + j, me, chip, src=stage[a]) for j, chip in enumerate(chips)]
        for cp in first:
            cp.start()
        passed = []
        for a in range(n):
            for j, chip in enumerate(chips):
                copy(a, 1 + j, chip, me).wait_recv()
                passed.append(copy(a, 4 + j, chip, sibling))
                passed[-1].start()
        for a in range(n):
            copy(a, 0, sibling, me).wait_recv()
            for j, chip in enumerate(chips):
                copy(a, 4 + j, _flip(chip, 1), me).wait_recv()
        for cp in first + passed:
            cp.wait_send()
        for cp in mine:
            cp.wait()

    return pl.pallas_call(
        body, name="weights_all_gather",
        in_specs=[VMEM_SPEC] * n, out_specs=[ANY_SPEC] * n,
        out_shape=[jax.ShapeDtypeStruct((N_DEV,) + s.shape, BF) for s in shards],
        scratch_shapes=[pltpu.VMEM(s.shape, BF) for s in shards]
        + [pltpu.SemaphoreType.DMA((n, 7)), pltpu.SemaphoreType.DMA((n, 7)), pltpu.SemaphoreType.DMA((n,))],
        compiler_params=pltpu.CompilerParams(vmem_limit_bytes=VMEM_SMALL),
    )(*shards)


def _exchange_blocks(parts):
    n = len(parts)

    def body(*refs):
        ins, outs = refs[:n], refs[n:2 * n]
        send_sems, recv_sems, local_sems = refs[2 * n:]
        me = _my_place()
        copies, mine = [], []
        for a in range(n):
            mine.append(pltpu.make_async_copy(ins[a].at[_index(me)], outs[a].at[_index(me)], local_sems.at[a]))
            mine[-1].start()
            for k in range(1, N_DEV):
                peer = _flip(me, k)
                copies.append(pltpu.make_async_remote_copy(
                    src_ref=ins[a].at[_index(peer)], dst_ref=outs[a].at[_index(me)],
                    send_sem=send_sems.at[a, k - 1], recv_sem=recv_sems.at[a, k - 1], device_id=peer, device_id_type=MESH))
                copies[-1].start()
        for a in range(n):
            for k in range(1, N_DEV):
                peer = _flip(me, k)
                pltpu.make_async_remote_copy(
                    src_ref=ins[a].at[_index(peer)], dst_ref=outs[a].at[_index(peer)],
                    send_sem=send_sems.at[a, k - 1], recv_sem=recv_sems.at[a, k - 1], device_id=peer,
                    device_id_type=MESH).wait_recv()
        for cp in copies:
            cp.wait_send()
        for cp in mine:
            cp.wait()

    return pl.pallas_call(
        body, name="grads_exchange",
        in_specs=[ANY_SPEC] * n, out_specs=[ANY_SPEC] * n,
        out_shape=[jax.ShapeDtypeStruct(p.shape, p.dtype) for p in parts],
        scratch_shapes=[pltpu.SemaphoreType.DMA((n, 7)), pltpu.SemaphoreType.DMA((n, 7)), pltpu.SemaphoreType.DMA((n,))],
    )(*parts)


def _all_reduce_small(part):
    R, C = part.shape

    def body(x_ref, o_ref, land_ref, send_sems, recv_sems):
        me = _my_place()
        land_ref[_index(me)] = x_ref[...]
        copies = []
        for k in range(1, N_DEV):
            peer = _flip(me, k)
            copies.append(pltpu.make_async_remote_copy(
                src_ref=x_ref, dst_ref=land_ref.at[_index(me)],
                send_sem=send_sems.at[k - 1], recv_sem=recv_sems.at[k - 1], device_id=peer, device_id_type=MESH))
            copies[-1].start()
        for k in range(1, N_DEV):
            peer = _flip(me, k)
            pltpu.make_async_remote_copy(
                src_ref=x_ref, dst_ref=land_ref.at[_index(peer)],
                send_sem=send_sems.at[k - 1], recv_sem=recv_sems.at[k - 1], device_id=peer, device_id_type=MESH).wait_recv()
        for cp in copies:
            cp.wait_send()
        acc = land_ref[0]
        for s in range(1, N_DEV):
            acc = acc + land_ref[s]
        o_ref[...] = acc

    return pl.pallas_call(
        body, name="small_all_reduce", in_specs=[VMEM_SPEC], out_specs=VMEM_SPEC,
        out_shape=jax.ShapeDtypeStruct((R, C), F32),
        scratch_shapes=[pltpu.VMEM((N_DEV, R, C), F32), pltpu.SemaphoreType.DMA((7,)), pltpu.SemaphoreType.DMA((7,))],
    )(part)


def _adamw_math(w, g, m, v):
    m = ADAM_B1 * m + (1.0 - ADAM_B1) * g
    v = ADAM_B2 * v + (1.0 - ADAM_B2) * jnp.square(g)
    m_hat = m / (1.0 - ADAM_B1 ** ADAM_STEP)
    v_hat = v / (1.0 - ADAM_B2 ** ADAM_STEP)
    delta = -ADAM_LR * (m_hat / (jnp.sqrt(v_hat) + ADAM_EPS) + ADAM_WD * w)
    return delta, m, v


def _sum_and_adamw(landed, w, m, v, name):
    R, C = w.shape
    tr = _pick(R, (256, 128))

    def body(l_ref, w_ref, m_ref, v_ref, g_ref, d_ref, nm_ref, nv_ref):
        g = l_ref[0].astype(F32)
        for s in range(1, N_DEV):
            g = g + l_ref[s].astype(F32)
        g_ref[...] = g
        d_ref[...], nm_ref[...], nv_ref[...] = _adamw_math(w_ref[...], g, m_ref[...], v_ref[...])

    row = pl.BlockSpec((tr, C), lambda i: (i, 0))
    return pl.pallas_call(
        body, name=name, grid=(R // tr,),
        in_specs=[pl.BlockSpec((N_DEV, tr, C), lambda i: (0, i, 0)), row, row, row], out_specs=[row] * 4,
        out_shape=[jax.ShapeDtypeStruct((R, C), F32)] * 4, compiler_params=_params(("parallel",)),
    )(landed, w, m, v)


def _adamw_small(w, g, m, v):
    def body(w_ref, g_ref, m_ref, v_ref, d_ref, nm_ref, nv_ref):
        d_ref[...], nm_ref[...], nv_ref[...] = _adamw_math(w_ref[...], g_ref[...], m_ref[...], v_ref[...])

    full = pl.BlockSpec(w.shape, lambda: (0, 0))
    return pl.pallas_call(body, name="adamw_small", in_specs=[full] * 4, out_specs=[full] * 3,
                          out_shape=[jax.ShapeDtypeStruct(w.shape, F32)] * 3)(w, g, m, v)


MATRICES = ("w_in_even", "w_out_even", "w_in_odd", "w_out_odd", "w_mlp_up", "w_mlp_down")
SMALL = ("norm_mix", "norm_mlp", "ret_decay_logit", "ret_norm", "swa_q_norm", "swa_k_norm", "swa_sink",
         "t5_table", "ax_q_norm", "ax_k_norm")


def _assemble(name, gathered):
    g = gathered
    if name in ("w_in_even", "w_in_odd"):
        return g[:, 0].transpose(1, 0, 2).reshape(g.shape[2], N_DEV * g.shape[3])
    if name in ("w_out_even", "w_out_odd"):
        return g.reshape(N_DEV * g.shape[2], g.shape[3])
    if name == "w_mlp_up":
        return g.transpose(1, 2, 0, 3).reshape(2, g.shape[2], N_DEV * g.shape[3])
    return g.transpose(1, 0, 2, 3).reshape(2, N_DEV * g.shape[2], g.shape[3])


def _split(name, full):
    if name in ("w_in_even", "w_in_odd"):
        K, N = full.shape
        return full.reshape(K, N_DEV, N // N_DEV).transpose(1, 0, 2)
    if name in ("w_out_even", "w_out_odd"):
        R, N = full.shape
        return full.reshape(N_DEV, R // N_DEV, N)
    if name == "w_mlp_up":
        _, K, N = full.shape
        return full.reshape(2, K, N_DEV, N // N_DEV).transpose(2, 0, 1, 3).reshape(N_DEV, 2 * K, N // N_DEV)
    _, R, N = full.shape
    return full.reshape(2, N_DEV, R // N_DEV, N).transpose(1, 0, 2, 3).reshape(N_DEV, 2 * R // N_DEV, N)


SMALL_ROWS = 8
SMALL_AT = {"norm_mix": (0, 0), "norm_mlp": (2, 0), "ret_norm": (4, 0), "swa_q_norm": (5, 0), "swa_k_norm": (5, 128),
            "ax_q_norm": (5, 256), "ax_k_norm": (5, 384), "swa_sink": (5, 512), "ret_decay_logit": (5, 640),
            "t5_table": (6, 0)}
LOSS_AT = (5, 768)


def _pack_small(arrays, loss=None):
    buf = jnp.zeros((SMALL_ROWS, 1024), F32)
    for name, (r, c) in SMALL_AT.items():
        a = arrays[name].astype(F32)
        a = a.reshape(1, -1) if name in ("ret_decay_logit", "t5_table") else a.reshape(-1, a.shape[-1])
        buf = lax.dynamic_update_slice(buf, a, (r, c))
    if loss is not None:
        buf = lax.dynamic_update_slice(buf, loss.reshape(1, 1), LOSS_AT)
    return buf


def _unpack_small(buf, like):
    out = {}
    for name, (r, c) in SMALL_AT.items():
        shape = like[name].shape
        rows = 1 if name in ("ret_decay_logit", "t5_table") else math.prod(shape[:-1])
        cols = math.prod(shape) // rows
        out[name] = buf[r:r + rows, c:c + cols].reshape(shape)
    return out


def kernel(x, norm_mix, norm_mlp, w_in_even, w_out_even, ret_decay_logit, ret_norm, swa_q_norm, swa_k_norm, swa_sink, t5_table, w_in_odd, w_out_odd, ax_q_norm, ax_k_norm, w_mlp_up, w_mlp_down, loss_target, m_norm_mix, m_norm_mlp, m_w_in_even, m_w_out_even, m_ret_decay_logit, m_ret_norm, m_swa_q_norm, m_swa_k_norm, m_swa_sink, m_t5_table, m_w_in_odd, m_w_out_odd, m_ax_q_norm, m_ax_k_norm, m_w_mlp_up, m_w_mlp_down, v_norm_mix, v_norm_mlp, v_w_in_even, v_w_out_even, v_ret_decay_logit, v_ret_norm, v_swa_q_norm, v_swa_k_norm, v_swa_sink, v_t5_table, v_w_in_odd, v_w_out_odd, v_ax_q_norm, v_ax_k_norm, v_w_mlp_up, v_w_mlp_down):
    given = dict(locals())
    weights = {n: given[n] for n in MATRICES + SMALL}
    moments_m = {n: given["m_" + n] for n in MATRICES + SMALL}
    moments_v = {n: given["v_" + n] for n in MATRICES + SMALL}

    gathered = _all_gather([weights[n] for n in MATRICES])
    W = {n[2:]: _assemble(n, g) for n, g in zip(MATRICES, gathered)}
    P = {"norm_mix": norm_mix, "norm_mlp": norm_mlp, "ret_decay_logit": ret_decay_logit[0], "ret_norm": ret_norm,
         "swa_q_norm": swa_q_norm, "swa_k_norm": swa_k_norm, "swa_sink": swa_sink, "t5_table": t5_table,
         "ax_q_norm": ax_q_norm, "ax_k_norm": ax_k_norm}

    loss_tile, dx, dW, dP = _local_step(x[0], loss_target[0], W, P)

    landed = _exchange_blocks([_split(n, dW[n[2:]]) for n in MATRICES])
    grads, deltas, new_m, new_v = {}, {}, {}, {}
    for n, l in zip(MATRICES, landed):
        shape = weights[n].shape
        flat = (l.shape[1], l.shape[2])
        g, d, nm, nv = _sum_and_adamw(l, weights[n].reshape(flat), moments_m[n].reshape(flat),
                                      moments_v[n].reshape(flat), "adamw_" + n)
        grads[n], deltas[n], new_m[n], new_v[n] = (t.reshape(shape) for t in (g, d, nm, nv))

    dP["ret_decay_logit"] = dP["ret_decay_logit"][None]
    total = _all_reduce_small(_pack_small(dP, loss_tile[0, 0]))
    loss = total[LOSS_AT[0], LOSS_AT[1]]
    small_d, small_m, small_v = _adamw_small(_pack_small(weights), total, _pack_small(moments_m), _pack_small(moments_v))
    like = {n: weights[n] for n in SMALL}
    for out, buf in ((grads, total), (deltas, small_d), (new_m, small_m), (new_v, small_v)):
        out.update(_unpack_small(buf, like))

    order = ("norm_mix", "norm_mlp", "w_in_even", "w_out_even", "ret_decay_logit", "ret_norm", "swa_q_norm", "swa_k_norm",
             "swa_sink", "t5_table", "w_in_odd", "w_out_odd", "ax_q_norm", "ax_k_norm", "w_mlp_up", "w_mlp_down")
    return (loss, dx[None], *[grads[n] for n in order], *[deltas[n] for n in order],
            *[new_m[n] for n in order], *[new_v[n] for n in order])
```

```python
import math

import jax
import jax.numpy as jnp
from jax import lax
from jax.experimental import pallas as pl
from jax.experimental.pallas import tpu as pltpu

F32 = jnp.float32
BF = jnp.bfloat16

D_MODEL = 1024
HEAD_DIM = 128
EPS = 1e-6
NEG_INF = -1e30
RET_HEADS, RET_DK, RET_DV = 4, 128, 256
RET_THETA = 10000.0
SWA_HEADS, SWA_KV_HEADS, WINDOW, BLOCK = 8, 2, 128, 128
T5_BUCKETS, T5_MAX_DIST = 32, 128
AX_HEADS, AX_KV_HEADS, AX_THETA, GRID_W = 8, 2, 10000.0, 64
D_FF = 4096
ATT_SCALE = HEAD_DIM ** -0.5
RET_SCALE = RET_DK ** -0.5
N_DEV = 8

ADAM_LR, ADAM_B1, ADAM_B2, ADAM_EPS, ADAM_WD, ADAM_STEP = 0.001, 0.9, 0.999, 1e-08, 0.01, 10

MIB = 1024 * 1024
VMEM_SMALL = 40 * MIB
VMEM_LARGE = 56 * MIB

OFF_QA, OFF_KA, OFF_VA, OFF_GA, OFF_QB, OFF_KB, OFF_VB = 0, 512, 1024, 2048, 3072, 4096, 4352
EVEN_IN = 4608
ODD_IN = 1536

NT = (((1,), (1,)), ((), ()))
TN = (((0,), (0,)), ((), ()))
NN = (((1,), (0,)), ((), ()))


def _dot(a, b, dims=NN):
    return lax.dot_general(a, b, dims, preferred_element_type=F32)


def _params(sem=None, vmem=VMEM_SMALL):
    return pltpu.CompilerParams(dimension_semantics=sem, vmem_limit_bytes=vmem)


def _pick(n, prefs):
    for p in prefs:
        if n % p == 0:
            return p
    return n


def _row_sum(x):
    return jnp.sum(x, axis=0, keepdims=True)


def _all_sum(x):
    return jnp.sum(jnp.sum(x, axis=0, keepdims=True), axis=1, keepdims=True)


def _sigmoid(x):
    return 1.0 / (1.0 + jnp.exp(-x))


SMEM_SPEC = pl.BlockSpec(memory_space=pltpu.SMEM)


def _mm(a, b, *, name, ta=False, tb=False, out_dtypes=(F32,), extras=(), epilogue=None,
        tm=1024, tn=512, tk=1024):
    M, K = (a.shape[1], a.shape[0]) if ta else a.shape
    N = b.shape[0] if tb else b.shape[1]
    assert K == (b.shape[1] if tb else b.shape[0])
    tm = _pick(M, (tm, 512, 256, 128))
    tn = _pick(N, (tn, 512, 384, 256, 128))
    tk = _pick(K, (tk, 1536, 512, 256, 128))
    nk = K // tk
    ne, no = len(extras), len(out_dtypes)
    dims = (((0 if ta else 1,), (1 if tb else 0,)), ((), ()))
    if epilogue is None:
        epilogue = lambda acc: (acc,)

    def body(a_ref, b_ref, *rest):
        extra_refs, out_refs = rest[:ne], rest[ne:ne + no]

        def finish(acc):
            outs = epilogue(acc, *[r[...] for r in extra_refs])
            for o_ref, o in zip(out_refs, outs):
                o_ref[...] = o.astype(o_ref.dtype)

        part = _dot(a_ref[...], b_ref[...], dims)
        if nk == 1:
            finish(part)
        else:
            acc_ref = rest[-1]
            k = pl.program_id(2)

            @pl.when(k == 0)
            def _():
                acc_ref[...] = part

            @pl.when(k > 0)
            def _():
                acc_ref[...] += part

            @pl.when(k == nk - 1)
            def _():
                finish(acc_ref[...])

    a_spec = pl.BlockSpec((tk, tm), lambda i, j, k: (k, i)) if ta else pl.BlockSpec((tm, tk), lambda i, j, k: (i, k))
    b_spec = pl.BlockSpec((tn, tk), lambda i, j, k: (j, k)) if tb else pl.BlockSpec((tk, tn), lambda i, j, k: (k, j))
    o_spec = pl.BlockSpec((tm, tn), lambda i, j, k: (i, j))
    outs = pl.pallas_call(
        body, name=name, grid=(M // tm, N // tn, nk),
        in_specs=[a_spec, b_spec] + [o_spec] * ne,
        out_specs=[o_spec] * no,
        out_shape=[jax.ShapeDtypeStruct((M, N), dt) for dt in out_dtypes],
        scratch_shapes=[pltpu.VMEM((tm, tn), F32)] if nk > 1 else [],
        compiler_params=_params(("parallel", "parallel", "arbitrary")),
    )(a, b, *extras)
    return outs[0] if no == 1 else outs


def _rms_fwd(x, g, name):
    S, Dm = x.shape
    tr = _pick(S, (512,))

    def body(x_ref, g_ref, o_ref):
        xv = x_ref[...]
        r = lax.rsqrt(jnp.mean(xv * xv, axis=-1, keepdims=True) + EPS)
        o_ref[...] = (xv * r * g_ref[...]).astype(o_ref.dtype)

    row = pl.BlockSpec((tr, Dm), lambda i: (i, 0))
    return pl.pallas_call(
        body, name=name, grid=(S // tr,),
        in_specs=[row, pl.BlockSpec((1, Dm), lambda i: (0, 0))], out_specs=row,
        out_shape=jax.ShapeDtypeStruct((S, Dm), BF), compiler_params=_params(("parallel",)),
    )(x, g)


def _rms_bwd(x, g, dh, dres, name):
    S, Dm = x.shape
    tr = _pick(S, (512,))

    def body(x_ref, g_ref, dh_ref, dres_ref, dx_ref, dxb_ref, dg_ref):
        xv = x_ref[...]
        r = lax.rsqrt(jnp.mean(xv * xv, axis=-1, keepdims=True) + EPS)
        xh = xv * r
        dy = dh_ref[...].astype(F32)
        dxh = dy * g_ref[...]
        dx = r * (dxh - xh * jnp.mean(dxh * xh, axis=-1, keepdims=True)) + dres_ref[...]
        dx_ref[...] = dx
        dxb_ref[...] = dx.astype(dxb_ref.dtype)

        @pl.when(pl.program_id(0) == 0)
        def _():
            dg_ref[...] = jnp.zeros_like(dg_ref)

        dg_ref[...] += _row_sum(dy * xh)

    row = pl.BlockSpec((tr, Dm), lambda i: (i, 0))
    vec = pl.BlockSpec((1, Dm), lambda i: (0, 0))
    return pl.pallas_call(
        body, name=name, grid=(S // tr,),
        in_specs=[row, vec, row, row], out_specs=[row, row, vec],
        out_shape=[jax.ShapeDtypeStruct((S, Dm), F32), jax.ShapeDtypeStruct((S, Dm), BF),
                   jax.ShapeDtypeStruct((1, Dm), F32)],
        compiler_params=_params(("arbitrary",)),
    )(x, g, dh, dres)


def _loss_and_grad(y, target):
    S, Dm = y.shape
    tr = _pick(S, (512,))
    n = S // tr

    def body(y_ref, t_ref, dy_ref, dyb_ref, loss_ref, acc_ref):
        i = pl.program_id(0)
        e = y_ref[...] - t_ref[...]
        d = e * (1.0 / Dm)
        dy_ref[...] = d
        dyb_ref[...] = d.astype(dyb_ref.dtype)

        @pl.when(i == 0)
        def _():
            acc_ref[...] = jnp.zeros_like(acc_ref)

        acc_ref[...] += _row_sum(e * e)

        @pl.when(i == n - 1)
        def _():
            loss_ref[...] = jnp.broadcast_to(_all_sum(acc_ref[...]) * (0.5 / Dm), loss_ref.shape)

    row = pl.BlockSpec((tr, Dm), lambda i: (i, 0))
    return pl.pallas_call(
        body, name="loss_head", grid=(n,),
        in_specs=[row, row], out_specs=[row, row, pl.BlockSpec((8, 128), lambda i: (0, 0))],
        out_shape=[jax.ShapeDtypeStruct((S, Dm), F32), jax.ShapeDtypeStruct((S, Dm), BF),
                   jax.ShapeDtypeStruct((8, 128), F32)],
        scratch_shapes=[pltpu.VMEM((1, Dm), F32)],
        compiler_params=_params(("arbitrary",)),
    )(y, target)


def _partner(x, half):
    if half == 64:
        return pltpu.roll(x, 64, 1)
    lane = lax.broadcasted_iota(jnp.int32, x.shape, 1)
    return jnp.where((lane % (2 * half)) < half, pltpu.roll(x, 128 - half, 1), pltpu.roll(x, half, 1))


def _rope(x, cos, sin, half):
    return x * cos + _partner(x, half) * sin


def _rope_t(dy, cos, sin, half):
    return dy * cos - _partner(dy, half) * sin


def _head_norm(x):
    r = lax.rsqrt(jnp.mean(x * x, axis=-1, keepdims=True) + EPS)
    return x * r, r


def _head_norm_bwd(dxh, xh, r):
    return r * (dxh - xh * jnp.mean(dxh * xh, axis=-1, keepdims=True))


def _cols(ref, off, width=HEAD_DIM):
    return ref[:, off:off + width]


def _prep_even_fwd(proj, cos, sin, gq, gk):
    S = proj.shape[0]
    tr = _pick(S, (256,))

    def body(p_ref, cos_ref, sin_ref, gq_ref, gk_ref, qr_ref, kr_ref, vr_ref, qs_ref, ks_ref, vs_ref):
        cos_v, sin_v = cos_ref[...], sin_ref[...]
        for h in range(RET_HEADS):
            o = h * RET_DK
            qr_ref[:, o:o + RET_DK] = _rope(_cols(p_ref, OFF_QA + o), cos_v, sin_v, 64).astype(qr_ref.dtype)
            kr_ref[:, o:o + RET_DK] = (_rope(_cols(p_ref, OFF_KA + o), cos_v, sin_v, 64) * RET_SCALE).astype(kr_ref.dtype)
        vr_ref[...] = p_ref[:, OFF_VA:OFF_VA + 1024].astype(vr_ref.dtype)
        for h in range(SWA_HEADS):
            o = h * HEAD_DIM
            xh, _ = _head_norm(_cols(p_ref, OFF_QB + o))
            qs_ref[:, o:o + HEAD_DIM] = (xh * gq_ref[...] * ATT_SCALE).astype(qs_ref.dtype)
        for h in range(SWA_KV_HEADS):
            o = h * HEAD_DIM
            xh, _ = _head_norm(_cols(p_ref, OFF_KB + o))
            ks_ref[:, o:o + HEAD_DIM] = (xh * gk_ref[...]).astype(ks_ref.dtype)
        vs_ref[...] = p_ref[:, OFF_VB:OFF_VB + 256].astype(vs_ref.dtype)

    def row(w):
        return pl.BlockSpec((tr, w), lambda i: (i, 0))

    vec = pl.BlockSpec((1, HEAD_DIM), lambda i: (0, 0))
    widths = (512, 512, 1024, 1024, 256, 256)
    return pl.pallas_call(
        body, name="prep_even_fwd", grid=(S // tr,),
        in_specs=[row(EVEN_IN), row(128), row(128), vec, vec],
        out_specs=[row(w) for w in widths],
        out_shape=[jax.ShapeDtypeStruct((S, w), BF) for w in widths],
        compiler_params=_params(("parallel",)),
    )(proj, cos, sin, gq, gk)


def _prep_even_bwd(proj, cos, sin, gq, gk, dqr, dkr, dvr, dga, dqs, dks, dvs):
    S = proj.shape[0]
    tr = _pick(S, (256,))

    def body(p_ref, cos_ref, sin_ref, gq_ref, gk_ref, dqr_ref, dkr_ref, dvr_ref, dga_ref, dqs_ref, dks_ref,
             dvs_ref, dp_ref, dgq_ref, dgk_ref):
        cos_v, sin_v = cos_ref[...], sin_ref[...]
        dt = dp_ref.dtype
        for h in range(RET_HEADS):
            o = h * RET_DK
            dp_ref[:, OFF_QA + o:OFF_QA + o + RET_DK] = _rope_t(_cols(dqr_ref, o), cos_v, sin_v, 64).astype(dt)
            dp_ref[:, OFF_KA + o:OFF_KA + o + RET_DK] = _rope_t(_cols(dkr_ref, o) * RET_SCALE, cos_v, sin_v, 64).astype(dt)
        dp_ref[:, OFF_VA:OFF_VA + 1024] = dvr_ref[...].astype(dt)
        dp_ref[:, OFF_GA:OFF_GA + 1024] = dga_ref[...].astype(dt)
        dgq = jnp.zeros((1, HEAD_DIM), F32)
        for h in range(SWA_HEADS):
            o = h * HEAD_DIM
            xh, r = _head_norm(_cols(p_ref, OFF_QB + o))
            dy = _cols(dqs_ref, o) * ATT_SCALE
            dgq = dgq + _row_sum(dy * xh)
            dp_ref[:, OFF_QB + o:OFF_QB + o + HEAD_DIM] = _head_norm_bwd(dy * gq_ref[...], xh, r).astype(dt)
        dgk = jnp.zeros((1, HEAD_DIM), F32)
        for h in range(SWA_KV_HEADS):
            o = h * HEAD_DIM
            xh, r = _head_norm(_cols(p_ref, OFF_KB + o))
            dy = _cols(dks_ref, o)
            dgk = dgk + _row_sum(dy * xh)
            dp_ref[:, OFF_KB + o:OFF_KB + o + HEAD_DIM] = _head_norm_bwd(dy * gk_ref[...], xh, r).astype(dt)
        dp_ref[:, OFF_VB:OFF_VB + 256] = dvs_ref[...].astype(dt)

        @pl.when(pl.program_id(0) == 0)
        def _():
            dgq_ref[...] = jnp.zeros_like(dgq_ref)
            dgk_ref[...] = jnp.zeros_like(dgk_ref)

        dgq_ref[...] += dgq
        dgk_ref[...] += dgk

    def row(w):
        return pl.BlockSpec((tr, w), lambda i: (i, 0))

    vec = pl.BlockSpec((1, HEAD_DIM), lambda i: (0, 0))
    return pl.pallas_call(
        body, name="prep_even_bwd", grid=(S // tr,),
        in_specs=[row(EVEN_IN), row(128), row(128), vec, vec, row(512), row(512), row(1024), row(1024),
                  row(1024), row(256), row(256)],
        out_specs=[row(EVEN_IN), vec, vec],
        out_shape=[jax.ShapeDtypeStruct((S, EVEN_IN), BF), jax.ShapeDtypeStruct((1, HEAD_DIM), F32),
                   jax.ShapeDtypeStruct((1, HEAD_DIM), F32)],
        compiler_params=_params(("arbitrary",)),
    )(proj, cos, sin, gq, gk, dqr, dkr, dvr, dga, dqs, dks, dvs)


RET_CHUNK = 512
def _log_sigmoid_tile(logit_tile):
    def body(x_ref, o_ref):
        xv = x_ref[...]
        t = jnp.exp(-jnp.abs(xv))
        log1p_t = jnp.where(t < 1e-3, t * (1.0 - 0.5 * t), jnp.log(1.0 + t))
        o_ref[...] = jnp.minimum(xv, 0.0) - log1p_t

    full = pl.BlockSpec((8, 128), lambda: (0, 0))
    return pl.pallas_call(body, name="log_sigmoid", in_specs=[full], out_specs=full,
                          out_shape=jax.ShapeDtypeStruct((8, 128), F32))(logit_tile)


def _decay(diff, lf, lb):
    return jnp.exp(jnp.where(diff >= 0, lf * diff, -(lb * diff)))


def _col_iota(n):
    return lax.broadcasted_iota(jnp.int32, (n, 1), 0).astype(F32)


def _ret_scan(x, z, lg, asc, desc, name):
    S = x.shape[0]
    C = _pick(S, (RET_CHUNK,))
    nc = S // C
    (arow, aoff), (drow, doff) = asc, desc

    def body(lg_ref, xa_ref, za_ref, xd_ref, zd_ref, asc_ref, desc_ref, sa_ref, sd_ref):
        h, t = pl.program_id(0), pl.program_id(1)
        la, ld = lg_ref[arow, h], lg_ref[drow, h]

        @pl.when(t == 0)
        def _():
            sa_ref[...] = jnp.zeros_like(sa_ref)
            sd_ref[...] = jnp.zeros_like(sd_ref)

        asc_ref[0, 0] = sa_ref[...]
        desc_ref[0, 0] = sd_ref[...]
        j = _col_iota(C)
        xa = (xa_ref[...].astype(F32) * jnp.exp(la * (C - 1 + aoff - j))).astype(xa_ref.dtype)
        xd = (xd_ref[...].astype(F32) * jnp.exp(ld * (j + doff))).astype(xd_ref.dtype)
        sa_ref[...] = jnp.exp(jnp.full((1, RET_DV), la * C, F32)) * sa_ref[...] + _dot(xa, za_ref[...], TN)
        sd_ref[...] = jnp.exp(jnp.full((1, RET_DV), ld * C, F32)) * sd_ref[...] + _dot(xd, zd_ref[...], TN)

    state = jax.ShapeDtypeStruct((RET_HEADS, nc, RET_DK, RET_DV), F32)
    return pl.pallas_call(
        body, name=name, grid=(RET_HEADS, nc),
        in_specs=[SMEM_SPEC,
                  pl.BlockSpec((C, RET_DK), lambda h, t: (t, h)), pl.BlockSpec((C, RET_DV), lambda h, t: (t, h)),
                  pl.BlockSpec((C, RET_DK), lambda h, t: (nc - 1 - t, h)), pl.BlockSpec((C, RET_DV), lambda h, t: (nc - 1 - t, h))],
        out_specs=[pl.BlockSpec((1, 1, RET_DK, RET_DV), lambda h, t: (h, t, 0, 0)),
                   pl.BlockSpec((1, 1, RET_DK, RET_DV), lambda h, t: (h, nc - 1 - t, 0, 0))],
        out_shape=[state, state],
        scratch_shapes=[pltpu.VMEM((RET_DK, RET_DV), F32), pltpu.VMEM((RET_DK, RET_DV), F32)],
        compiler_params=_params(("parallel", "arbitrary")),
    )(lg, x, z, x, z)


def _ret_fwd(q, k, v, lg, sf, sb):
    S = q.shape[0]
    C = _pick(S, (RET_CHUNK,))

    def body(lg_ref, q_ref, k_ref, v_ref, sf_ref, sb_ref, y_ref):
        h = pl.program_id(0)
        lf, lb = lg_ref[0, h], lg_ref[1, h]
        qv = q_ref[...]
        dt = qv.dtype
        diff = (lax.broadcasted_iota(jnp.int32, (C, C), 0) - lax.broadcasted_iota(jnp.int32, (C, C), 1)).astype(F32)
        y = _dot((_dot(qv, k_ref[...], NT) * _decay(diff, lf, lb)).astype(dt), v_ref[...])
        r = _col_iota(C)
        qf = qv.astype(F32)
        y = y + _dot((qf * jnp.exp(lf * (r + 1.0))).astype(dt), sf_ref[0, 0].astype(dt))
        y_ref[...] = y + _dot((qf * jnp.exp(lb * (C - r))).astype(dt), sb_ref[0, 0].astype(dt))

    state = pl.BlockSpec((1, 1, RET_DK, RET_DV), lambda h, c: (h, c, 0, 0))
    qk = pl.BlockSpec((C, RET_DK), lambda h, c: (c, h))
    vy = pl.BlockSpec((C, RET_DV), lambda h, c: (c, h))
    return pl.pallas_call(
        body, name="retention_fwd", grid=(RET_HEADS, S // C),
        in_specs=[SMEM_SPEC, qk, qk, vy, state, state], out_specs=vy,
        out_shape=jax.ShapeDtypeStruct((S, RET_HEADS * RET_DV), F32),
        compiler_params=_params(("parallel", "parallel")),
    )(lg, q, k, v, sf, sb)


def _ret_bwd(q, k, v, dy, lg, logit, sf, sb, hf, hb):
    S = q.shape[0]
    C = _pick(S, (RET_CHUNK,))
    nc = S // C

    def body(lg_ref, logit_ref, q_ref, k_ref, v_ref, dy_ref, sf_ref, sb_ref, hf_ref, hb_ref,
             dq_ref, dk_ref, dv_ref, dlg_ref, acc_ref):
        h, c = pl.program_id(0), pl.program_id(1)
        lf, lb = lg_ref[0, h], lg_ref[1, h]

        @pl.when(c == 0)
        def _():
            acc_ref[...] = jnp.zeros_like(acc_ref)

        qv, kv, vv, dyv = q_ref[...], k_ref[...], v_ref[...], dy_ref[...]
        dt = qv.dtype
        qf, kf = qv.astype(F32), kv.astype(F32)
        diff = (lax.broadcasted_iota(jnp.int32, (C, C), 0) - lax.broadcasted_iota(jnp.int32, (C, C), 1)).astype(F32)
        dec = _decay(diff, lf, lb)
        sc = _dot(qv, kv, NT) * dec
        dp = _dot(dyv, vv, NT)
        da = (dp * dec).astype(dt)
        dq = _dot(da, kv)
        dk = _dot(da, qv, TN)
        dv = _dot(sc.astype(dt), dyv, TN)
        w = sc * dp * diff
        tot_w, tot_f = _all_sum(w), _all_sum(jnp.where(diff >= 0, w, 0.0))
        d_lf, d_lb = tot_f, tot_f - tot_w
        r = _col_iota(C)
        a, b = jnp.exp(lf * (r + 1.0)), jnp.exp(lb * (C - r))
        e, f = jnp.exp(lf * (C - 1.0 - r)), jnp.exp(lb * r)
        sfv, sbv, hfv, hbv = sf_ref[0, 0], sb_ref[0, 0], hf_ref[0, 0], hb_ref[0, 0]
        t_f, t_b = _dot(dyv, sfv.astype(dt), NT), _dot(dyv, sbv.astype(dt), NT)
        u_f, u_b = _dot(vv, hfv.astype(dt), NT), _dot(vv, hbv.astype(dt), NT)
        dq_ref[...] = dq + a * t_f + b * t_b
        dk_ref[...] = dk + e * u_f + f * u_b
        dv_ref[...] = dv + _dot((kf * e).astype(dt), hfv.astype(dt)) + _dot((kf * f).astype(dt), hbv.astype(dt))
        row_q_f = jnp.sum(qf * t_f, axis=-1, keepdims=True)
        row_q_b = jnp.sum(qf * t_b, axis=-1, keepdims=True)
        row_k_f = jnp.sum(kf * u_f, axis=-1, keepdims=True)
        row_k_b = jnp.sum(kf * u_b, axis=-1, keepdims=True)
        gf_c = jnp.exp(jnp.full((1, 1), lf * C, F32))
        gb_c = jnp.exp(jnp.full((1, 1), lb * C, F32))
        d_lf = d_lf + _all_sum((r + 1.0) * a * row_q_f + (C - 1.0 - r) * e * row_k_f) + C * gf_c * _all_sum(hfv * sfv)
        d_lb = d_lb + _all_sum((C - r) * b * row_q_b + r * f * row_k_b) + C * gb_c * _all_sum(hbv * sbv)
        acc_ref[0] += jnp.broadcast_to(d_lf, (8, 128))
        acc_ref[1] += jnp.broadcast_to(d_lb, (8, 128))

        @pl.when(c == nc - 1)
        def _():
            for d in range(2):
                gate = 1.0 / (1.0 + jnp.exp(jnp.full((8, 128), logit_ref[d, h], F32)))
                dlg_ref[0, d] = acc_ref[d] * gate

    state = pl.BlockSpec((1, 1, RET_DK, RET_DV), lambda h, c: (h, c, 0, 0))
    qk = pl.BlockSpec((C, RET_DK), lambda h, c: (c, h))
    vy = pl.BlockSpec((C, RET_DV), lambda h, c: (c, h))
    return pl.pallas_call(
        body, name="retention_bwd", grid=(RET_HEADS, nc),
        in_specs=[SMEM_SPEC, SMEM_SPEC, qk, qk, vy, vy, state, state, state, state],
        out_specs=[qk, qk, vy, pl.BlockSpec((1, 2, 8, 128), lambda h, c: (h, 0, 0, 0))],
        out_shape=[jax.ShapeDtypeStruct((S, RET_HEADS * RET_DK), F32), jax.ShapeDtypeStruct((S, RET_HEADS * RET_DK), F32),
                   jax.ShapeDtypeStruct((S, RET_HEADS * RET_DV), F32),
                   jax.ShapeDtypeStruct((RET_HEADS, 2, 8, 128), F32)],
        scratch_shapes=[pltpu.VMEM((2, 8, 128), F32)],
        compiler_params=_params(("parallel", "arbitrary")),
    )(lg, logit, q, k, v, dy, sf, sb, hf, hb)


def _ret_post_fwd(y, proj, gn):
    S = y.shape[0]
    tr = _pick(S, (512,))

    def body(y_ref, g_ref, gn_ref, o_ref):
        for h in range(RET_HEADS):
            o = h * RET_DV
            yh, _ = _head_norm(_cols(y_ref, o, RET_DV))
            gate = _cols(g_ref, o, RET_DV)
            o_ref[:, o:o + RET_DV] = (gate * _sigmoid(gate) * (yh * gn_ref[:, o:o + RET_DV])).astype(o_ref.dtype)

    row = pl.BlockSpec((tr, 1024), lambda i: (i, 0))
    return pl.pallas_call(
        body, name="retention_post_fwd", grid=(S // tr,),
        in_specs=[row, pl.BlockSpec((tr, 1024), lambda i: (i, OFF_GA // 1024)), pl.BlockSpec((1, 1024), lambda i: (0, 0))],
        out_specs=row, out_shape=jax.ShapeDtypeStruct((S, 1024), BF), compiler_params=_params(("parallel",)),
    )(y, proj, gn)


def _ret_post_bwd(y, proj, gn, do):
    S = y.shape[0]
    tr = _pick(S, (512,))

    def body(y_ref, g_ref, gn_ref, do_ref, dy_ref, dg_ref, dgn_ref):
        @pl.when(pl.program_id(0) == 0)
        def _():
            dgn_ref[...] = jnp.zeros_like(dgn_ref)

        for h in range(RET_HEADS):
            o = h * RET_DV
            yh, r = _head_norm(_cols(y_ref, o, RET_DV))
            gate = _cols(g_ref, o, RET_DV)
            gnh = gn_ref[:, o:o + RET_DV]
            dout = _cols(do_ref, o, RET_DV).astype(F32)
            sg = _sigmoid(gate)
            dz = dout * (gate * sg)
            dg_ref[:, o:o + RET_DV] = dout * (yh * gnh) * (sg * (1.0 + gate * (1.0 - sg)))
            dgn_ref[:, o:o + RET_DV] += _row_sum(dz * yh)
            dy_ref[:, o:o + RET_DV] = _head_norm_bwd(dz * gnh, yh, r).astype(dy_ref.dtype)

    row = pl.BlockSpec((tr, 1024), lambda i: (i, 0))
    vec = pl.BlockSpec((1, 1024), lambda i: (0, 0))
    return pl.pallas_call(
        body, name="retention_post_bwd", grid=(S // tr,),
        in_specs=[row, pl.BlockSpec((tr, 1024), lambda i: (i, OFF_GA // 1024)), vec, row],
        out_specs=[row, row, vec],
        out_shape=[jax.ShapeDtypeStruct((S, 1024), BF), jax.ShapeDtypeStruct((S, 1024), F32),
                   jax.ShapeDtypeStruct((1, 1024), F32)],
        compiler_params=_params(("arbitrary",)),
    )(y, proj, gn, do)


def _t5_bucket_map():
    r = jnp.arange(BLOCK)
    j = jnp.arange(3 * BLOCK)
    rel = j[None, :] - BLOCK - r[:, None]
    nb = T5_BUCKETS // 2
    max_exact = nb // 2
    ret = jnp.where(rel > 0, nb, 0)
    n = jnp.abs(rel)
    nf = jnp.maximum(n, 1).astype(jnp.float32)
    large = max_exact + (jnp.log(nf / max_exact) / math.log(T5_MAX_DIST / max_exact)
                         * (nb - max_exact)).astype(jnp.int32)
    large = jnp.minimum(large, nb - 1)
    bucket = ret + jnp.where(n < max_exact, n, large)
    return jnp.where(jnp.abs(rel) <= WINDOW, bucket, -1).astype(jnp.int32)


def _t5_bias(table, bucket):
    def body(t_ref, b_ref, o_ref):
        bk = b_ref[...]
        for h in range(SWA_HEADS):
            acc = jnp.full(bk.shape, NEG_INF, F32)
            for b in range(T5_BUCKETS):
                acc = jnp.where(bk == b, t_ref[b, h], acc)
            o_ref[h] = acc

    return pl.pallas_call(
        body, name="t5_bias", in_specs=[SMEM_SPEC, pl.BlockSpec((BLOCK, 3 * BLOCK), lambda: (0, 0))],
        out_specs=pl.BlockSpec((SWA_HEADS, BLOCK, 3 * BLOCK), lambda: (0, 0, 0)),
        out_shape=jax.ShapeDtypeStruct((SWA_HEADS, BLOCK, 3 * BLOCK), F32),
    )(table, bucket)


def _t5_table_grad(dbias, bucket):
    def body(d_ref, b_ref, o_ref):
        bk = b_ref[...]
        lane = lax.broadcasted_iota(jnp.int32, (1, 128), 1)
        for b in range(T5_BUCKETS):
            hit = bk == b
            row = jnp.zeros((1, 128), F32)
            for h in range(SWA_HEADS):
                row = row + jnp.where(lane == h, _all_sum(jnp.where(hit, d_ref[h], 0.0)), 0.0)
            o_ref[b:b + 1, :] = row

    return pl.pallas_call(
        body, name="t5_table_grad",
        in_specs=[pl.BlockSpec((SWA_HEADS, BLOCK, 3 * BLOCK), lambda: (0, 0, 0)), pl.BlockSpec((BLOCK, 3 * BLOCK), lambda: (0, 0))],
        out_specs=pl.BlockSpec((T5_BUCKETS, 128), lambda: (0, 0)),
        out_shape=jax.ShapeDtypeStruct((T5_BUCKETS, 128), F32),
    )(dbias, bucket)


def _swa_scores(i, nb, q, kw, bias_h, sink_h):
    s = _dot(q, kw, NT) + bias_h
    col = lax.broadcasted_iota(jnp.int32, s.shape, 1)
    first_col = jnp.where(i == 0, BLOCK, 0)
    end_col = jnp.where(i == nb - 1, 2 * BLOCK, 3 * BLOCK)
    s = jnp.where((col < first_col) | (col >= end_col), NEG_INF, s)
    m = jnp.maximum(jnp.max(s, axis=-1, keepdims=True), sink_h)
    p = jnp.exp(s - m)
    e_sink = jnp.exp(sink_h - m)
    inv = 1.0 / (jnp.sum(p, axis=-1, keepdims=True) + e_sink)
    return p * inv, e_sink * inv


def _swa_window(ref, i, nb, off):
    prev, nxt = jnp.maximum(i - 1, 0), jnp.minimum(i + 1, nb - 1)
    rows = [pl.ds(pl.multiple_of(b * BLOCK, BLOCK), BLOCK) for b in (prev, i, nxt)]
    return jnp.concatenate([ref[r, off:off + HEAD_DIM] for r in rows], axis=0), rows


def _swa_fwd(q, k, v, bias, sink):
    S = q.shape[0]
    nb = S // BLOCK
    G = SWA_HEADS // SWA_KV_HEADS

    def body(sink_ref, q_ref, k_ref, v_ref, bias_ref, o_ref):
        i = pl.program_id(0)
        for kh in range(SWA_KV_HEADS):
            kw, _ = _swa_window(k_ref, i, nb, kh * HEAD_DIM)
            vw, _ = _swa_window(v_ref, i, nb, kh * HEAD_DIM)
            for g in range(G):
                h = kh * G + g
                p, _ = _swa_scores(i, nb, _cols(q_ref, h * HEAD_DIM), kw, bias_ref[h], sink_ref[0, h])
                o_ref[:, h * HEAD_DIM:(h + 1) * HEAD_DIM] = _dot(p.astype(vw.dtype), vw).astype(o_ref.dtype)

    full_kv = pl.BlockSpec((S, SWA_KV_HEADS * HEAD_DIM), lambda i: (0, 0))
    return pl.pallas_call(
        body, name="swa_fwd", grid=(nb,),
        in_specs=[SMEM_SPEC, pl.BlockSpec((BLOCK, 1024), lambda i: (i, 0)), full_kv, full_kv,
                  pl.BlockSpec((SWA_HEADS, BLOCK, 3 * BLOCK), lambda i: (0, 0, 0))],
        out_specs=pl.BlockSpec((BLOCK, 1024), lambda i: (i, 0)),
        out_shape=jax.ShapeDtypeStruct((S, 1024), BF), compiler_params=_params(("parallel",)),
    )(sink, q, k, v, bias)


def _swa_bwd(q, k, v, do, bias, sink):
    S = q.shape[0]
    nb = S // BLOCK
    G = SWA_HEADS // SWA_KV_HEADS

    def body(sink_ref, q_ref, k_ref, v_ref, do_ref, bias_ref, dq_ref, dk_ref, dv_ref, dbias_ref, dsink_ref):
        i = pl.program_id(0)

        @pl.when(i == 0)
        def _():
            dk_ref[...] = jnp.zeros_like(dk_ref)
            dv_ref[...] = jnp.zeros_like(dv_ref)
            dbias_ref[...] = jnp.zeros_like(dbias_ref)
            dsink_ref[...] = jnp.zeros_like(dsink_ref)

        for kh in range(SWA_KV_HEADS):
            off = kh * HEAD_DIM
            kw, rows = _swa_window(k_ref, i, nb, off)
            vw, _ = _swa_window(v_ref, i, nb, off)
            dkw = jnp.zeros((3 * BLOCK, HEAD_DIM), F32)
            dvw = jnp.zeros((3 * BLOCK, HEAD_DIM), F32)
            for g in range(G):
                h = kh * G + g
                qh = _cols(q_ref, h * HEAD_DIM)
                p, p_sink = _swa_scores(i, nb, qh, kw, bias_ref[h], sink_ref[0, h])
                doh = _cols(do_ref, h * HEAD_DIM).astype(vw.dtype)
                dp = _dot(doh, vw, NT)
                delta = jnp.sum(p * dp, axis=-1, keepdims=True)
                ds = p * (dp - delta)
                dsb = ds.astype(qh.dtype)
                dq_ref[:, h * HEAD_DIM:(h + 1) * HEAD_DIM] = _dot(dsb, kw)
                dkw = dkw + _dot(dsb, qh, TN)
                dvw = dvw + _dot(p.astype(doh.dtype), doh, TN)
                dbias_ref[h] += ds
                dsink_ref[h:h + 1, :] += jnp.broadcast_to(-_row_sum(p_sink * delta), (1, 128))
            for b, r in enumerate(rows):
                dk_ref[r, off:off + HEAD_DIM] += dkw[b * BLOCK:(b + 1) * BLOCK]
                dv_ref[r, off:off + HEAD_DIM] += dvw[b * BLOCK:(b + 1) * BLOCK]

    full_kv = pl.BlockSpec((S, SWA_KV_HEADS * HEAD_DIM), lambda i: (0, 0))
    blk = pl.BlockSpec((BLOCK, 1024), lambda i: (i, 0))
    bias_spec = pl.BlockSpec((SWA_HEADS, BLOCK, 3 * BLOCK), lambda i: (0, 0, 0))
    return pl.pallas_call(
        body, name="swa_bwd", grid=(nb,),
        in_specs=[SMEM_SPEC, blk, full_kv, full_kv, blk, bias_spec],
        out_specs=[blk, full_kv, full_kv, bias_spec, pl.BlockSpec((8, 128), lambda i: (0, 0))],
        out_shape=[jax.ShapeDtypeStruct((S, 1024), F32), jax.ShapeDtypeStruct((S, 256), F32),
                   jax.ShapeDtypeStruct((S, 256), F32),
                   jax.ShapeDtypeStruct((SWA_HEADS, BLOCK, 3 * BLOCK), F32), jax.ShapeDtypeStruct((8, 128), F32)],
        compiler_params=_params(("arbitrary",), VMEM_LARGE),
    )(sink, q, k, v, do, bias)


def _prep_odd_fwd(proj, cos, sin, gq, gk):
    S = proj.shape[0]
    tr = _pick(S, (512,))

    def body(p_ref, cos_ref, sin_ref, gq_ref, gk_ref, q_ref, k_ref, v_ref):
        cos_v, sin_v = cos_ref[...], sin_ref[...]
        for h in range(AX_HEADS):
            o = h * HEAD_DIM
            xh, _ = _head_norm(_cols(p_ref, o))
            q_ref[:, o:o + HEAD_DIM] = (_rope(xh * gq_ref[...], cos_v, sin_v, 32) * ATT_SCALE).astype(q_ref.dtype)
        for h in range(AX_KV_HEADS):
            o = h * HEAD_DIM
            xh, _ = _head_norm(_cols(p_ref, 1024 + o))
            k_ref[:, o:o + HEAD_DIM] = _rope(xh * gk_ref[...], cos_v, sin_v, 32).astype(k_ref.dtype)
        v_ref[...] = p_ref[:, 1280:1536].astype(v_ref.dtype)

    def row(w):
        return pl.BlockSpec((tr, w), lambda i: (i, 0))

    vec = pl.BlockSpec((1, HEAD_DIM), lambda i: (0, 0))
    return pl.pallas_call(
        body, name="prep_odd_fwd", grid=(S // tr,),
        in_specs=[row(ODD_IN), row(128), row(128), vec, vec], out_specs=[row(1024), row(256), row(256)],
        out_shape=[jax.ShapeDtypeStruct((S, w), BF) for w in (1024, 256, 256)],
        compiler_params=_params(("parallel",)),
    )(proj, cos, sin, gq, gk)


def _prep_odd_bwd(proj, cos, sin, gq, gk, dq, dk, dv):
    S = proj.shape[0]
    tr = _pick(S, (512,))

    def body(p_ref, cos_ref, sin_ref, gq_ref, gk_ref, dq_ref, dk_ref, dv_ref, dp_ref, dgq_ref, dgk_ref):
        cos_v, sin_v = cos_ref[...], sin_ref[...]
        dt = dp_ref.dtype
        dgq = jnp.zeros((1, HEAD_DIM), F32)
        for h in range(AX_HEADS):
            o = h * HEAD_DIM
            xh, r = _head_norm(_cols(p_ref, o))
            dy = _rope_t(_cols(dq_ref, o) * ATT_SCALE, cos_v, sin_v, 32)
            dgq = dgq + _row_sum(dy * xh)
            dp_ref[:, o:o + HEAD_DIM] = _head_norm_bwd(dy * gq_ref[...], xh, r).astype(dt)
        dgk = jnp.zeros((1, HEAD_DIM), F32)
        for h in range(AX_KV_HEADS):
            o = h * HEAD_DIM
            xh, r = _head_norm(_cols(p_ref, 1024 + o))
            dy = _rope_t(_cols(dk_ref, o), cos_v, sin_v, 32)
            dgk = dgk + _row_sum(dy * xh)
            dp_ref[:, 1024 + o:1024 + o + HEAD_DIM] = _head_norm_bwd(dy * gk_ref[...], xh, r).astype(dt)
        dp_ref[:, 1280:1536] = dv_ref[...].astype(dt)

        @pl.when(pl.program_id(0) == 0)
        def _():
            dgq_ref[...] = jnp.zeros_like(dgq_ref)
            dgk_ref[...] = jnp.zeros_like(dgk_ref)

        dgq_ref[...] += dgq
        dgk_ref[...] += dgk

    def row(w):
        return pl.BlockSpec((tr, w), lambda i: (i, 0))

    vec = pl.BlockSpec((1, HEAD_DIM), lambda i: (0, 0))
    return pl.pallas_call(
        body, name="prep_odd_bwd", grid=(S // tr,),
        in_specs=[row(ODD_IN), row(128), row(128), vec, vec, row(1024), row(256), row(256)],
        out_specs=[row(ODD_IN), vec, vec],
        out_shape=[jax.ShapeDtypeStruct((S, ODD_IN), BF), jax.ShapeDtypeStruct((1, HEAD_DIM), F32),
                   jax.ShapeDtypeStruct((1, HEAD_DIM), F32)],
        compiler_params=_params(("arbitrary",)),
    )(proj, cos, sin, gq, gk, dq, dk, dv)


def _flash_fwd(q, k, v):
    S = q.shape[0]
    tq = _pick(S, (512,))
    tk = _pick(S, (512,))
    nk = S // tk
    G = AX_HEADS // AX_KV_HEADS

    def body(q_ref, k_ref, v_ref, o_ref, lse_ref):
        qv = q_ref[...]

        def step(j, carry):
            m, l, acc = carry
            rows = pl.ds(pl.multiple_of(j * tk, tk), tk)
            s = _dot(qv, k_ref[rows, :], NT)
            m_new = jnp.maximum(m, jnp.max(s, axis=-1, keepdims=True))
            alpha = jnp.exp(m - m_new)
            p = jnp.exp(s - m_new)
            l = alpha * l + jnp.sum(p, axis=-1, keepdims=True)
            acc = alpha * acc + _dot(p.astype(v_ref.dtype), v_ref[rows, :])
            return m_new, l, acc

        init = (jnp.full((tq, 1), NEG_INF, F32), jnp.zeros((tq, 1), F32), jnp.zeros((tq, HEAD_DIM), F32))
        m, l, acc = lax.fori_loop(0, nk // 4, lambda j, c: step(4 * j + 3, step(4 * j + 2, step(4 * j + 1, step(4 * j, c)))), init)
        o_ref[...] = (acc / l).astype(o_ref.dtype)
        lse_ref[0] = jnp.broadcast_to(m + jnp.log(l), (tq, 128))

    return pl.pallas_call(
        body, name="flash_fwd", grid=(AX_HEADS, S // tq),
        in_specs=[pl.BlockSpec((tq, HEAD_DIM), lambda h, i: (i, h)),
                  pl.BlockSpec((S, HEAD_DIM), lambda h, i: (0, h // G)),
                  pl.BlockSpec((S, HEAD_DIM), lambda h, i: (0, h // G))],
        out_specs=[pl.BlockSpec((tq, HEAD_DIM), lambda h, i: (i, h)),
                   pl.BlockSpec((1, tq, 128), lambda h, i: (h, i, 0))],
        out_shape=[jax.ShapeDtypeStruct((S, AX_HEADS * HEAD_DIM), BF), jax.ShapeDtypeStruct((AX_HEADS, S, 128), F32)],
        compiler_params=_params(("parallel", "parallel")),
    )(q, k, v)


def _flash_bwd(q, k, v, o, do, lse):
    S = q.shape[0]
    tq = _pick(S, (512,))
    tk = _pick(S, (512,))
    nq, nk = S // tq, S // tk
    G = AX_HEADS // AX_KV_HEADS

    def body(q_ref, k_ref, v_ref, o_ref, do_ref, lse_ref, dq_ref, dk_ref, dv_ref):
        g, i = pl.program_id(1), pl.program_id(2)

        @pl.when((g == 0) & (i == 0))
        def _():
            dk_ref[...] = jnp.zeros_like(dk_ref)
            dv_ref[...] = jnp.zeros_like(dv_ref)

        qv = q_ref[...]
        do_f = do_ref[...].astype(F32)
        dob = do_f.astype(qv.dtype)
        delta = jnp.sum(do_f * o_ref[...].astype(F32), axis=-1, keepdims=True)
        lse_col = lse_ref[0][:, 0:1]

        def step(j, dq):
            rows = pl.ds(pl.multiple_of(j * tk, tk), tk)
            kj, vj = k_ref[rows, :], v_ref[rows, :]
            p = jnp.exp(_dot(qv, kj, NT) - lse_col)
            dp = _dot(dob, vj, NT)
            ds = (p * (dp - delta)).astype(qv.dtype)
            dk_ref[rows, :] += _dot(ds, qv, TN)
            dv_ref[rows, :] += _dot(p.astype(dob.dtype), dob, TN)
            return dq + _dot(ds, kj)

        dq_ref[...] = lax.fori_loop(0, nk // 2, lambda j, c: step(2 * j + 1, step(2 * j, c)), jnp.zeros((tq, HEAD_DIM), F32))

    q_spec = pl.BlockSpec((tq, HEAD_DIM), lambda kh, g, i: (i, kh * G + g))
    kv_spec = pl.BlockSpec((S, HEAD_DIM), lambda kh, g, i: (0, kh))
    return pl.pallas_call(
        body, name="flash_bwd", grid=(AX_KV_HEADS, G, nq),
        in_specs=[q_spec, kv_spec, kv_spec, q_spec, q_spec,
                  pl.BlockSpec((1, tq, 128), lambda kh, g, i: (kh * G + g, i, 0))],
        out_specs=[q_spec, kv_spec, kv_spec],
        out_shape=[jax.ShapeDtypeStruct((S, AX_HEADS * HEAD_DIM), F32), jax.ShapeDtypeStruct((S, 256), F32),
                   jax.ShapeDtypeStruct((S, 256), F32)],
        compiler_params=_params(("arbitrary", "arbitrary", "arbitrary"), VMEM_LARGE),
    )(q, k, v, o, do, lse)


def _rope_angles(pos, dim, theta):
    inv = theta ** (-jnp.arange(0, dim, 2, dtype=jnp.float32) / dim)
    return pos.astype(jnp.float32)[:, None] * inv[None, :]


def _rope_tables(S):
    ang = _rope_angles(jnp.arange(S), RET_DK, RET_THETA)
    c, s = jnp.cos(ang), jnp.sin(ang)
    ret = (jnp.concatenate([c, c], -1), jnp.concatenate([-s, s], -1))
    rows = S // GRID_W
    row = jnp.repeat(jnp.arange(rows), GRID_W)
    col = jnp.tile(jnp.arange(GRID_W), rows)
    ar, ac = _rope_angles(row, HEAD_DIM // 2, AX_THETA), _rope_angles(col, HEAD_DIM // 2, AX_THETA)
    cr, sr, cc, sc = jnp.cos(ar), jnp.sin(ar), jnp.cos(ac), jnp.sin(ac)
    ax = (jnp.concatenate([cr, cr, cc, cc], -1), jnp.concatenate([-sr, sr, -sc, sc], -1))
    return ret, ax


def _pad_tile(a):
    return jnp.pad(a.astype(F32), ((0, 8 - a.shape[0]), (0, 128 - a.shape[1])))


def _relu2_epilogue(acc):
    r = jnp.maximum(acc, 0.0)
    return acc, r * r


def _relu2_bwd_epilogue(acc, u):
    return (acc * (2.0 * jnp.maximum(u.astype(F32), 0.0)),)


def _add_epilogue(acc, res):
    return (acc + res,)


def _mlp_fwd(x, g, w_up, w_down, tag):
    h = _rms_fwd(x, g, f"mlp_norm_{tag}")
    u, a = _mm(h, w_up, name=f"mlp_up_{tag}", out_dtypes=(BF, BF), epilogue=_relu2_epilogue)
    y = _mm(a, w_down, name=f"mlp_down_{tag}", extras=(x,), epilogue=_add_epilogue)
    return y, (h, u, a)


def _mlp_bwd(x, g, w_up, w_down, saved, dy, dyb, tag):
    h, u, a = saved
    du = _mm(dyb, w_down, tb=True, name=f"mlp_down_dx_{tag}", out_dtypes=(BF,), extras=(u,), epilogue=_relu2_bwd_epilogue)
    dw_down = _mm(a, dyb, ta=True, name=f"mlp_down_dw_{tag}", out_dtypes=(BF,))
    dw_up = _mm(h, du, ta=True, name=f"mlp_up_dw_{tag}", out_dtypes=(BF,))
    dh = _mm(du, w_up, tb=True, name=f"mlp_up_dx_{tag}")
    dx, dxb, dg = _rms_bwd(x, g, dh, dy, f"mlp_norm_bwd_{tag}")
    return dx, dxb, dg, dw_up, dw_down


def _local_step(x, target, W, P):
    S = x.shape[0]
    (cos_r, sin_r), (cos_a, sin_a) = _rope_tables(S)
    bucket = _t5_bucket_map()
    logit = P["ret_decay_logit"]
    lg = _log_sigmoid_tile(_pad_tile(logit))
    bias = _t5_bias(P["t5_table"], bucket)
    nmix, nmlp = P["norm_mix"], P["norm_mlp"]

    h0 = _rms_fwd(x, nmix[0:1], "mix_norm_0")
    proj_e = _mm(h0, W["in_even"], name="in_even")
    qr, kr, vr, qs, ks, vs = _prep_even_fwd(proj_e, cos_r, sin_r, P["swa_q_norm"], P["swa_k_norm"])
    sf, sb = _ret_scan(kr, vr, lg, (0, 0), (1, 0), "retention_states")
    y_ret = _ret_fwd(qr, kr, vr, lg, sf, sb)
    oa = _ret_post_fwd(y_ret, proj_e, P["ret_norm"])
    ob = _swa_fwd(qs, ks, vs, bias, P["swa_sink"])
    wo_a, wo_b = W["out_even"][:1024], W["out_even"][1024:]
    x1 = _mm(oa, wo_a, name="out_even_a", extras=(x,), epilogue=_add_epilogue)
    x1 = _mm(ob, wo_b, name="out_even_b", extras=(x1,), epilogue=_add_epilogue)
    x2, mlp0 = _mlp_fwd(x1, nmlp[0:1], W["mlp_up"][0], W["mlp_down"][0], "0")

    h2 = _rms_fwd(x2, nmix[1:2], "mix_norm_1")
    proj_o = _mm(h2, W["in_odd"], name="in_odd")
    qx, kx, vx = _prep_odd_fwd(proj_o, cos_a, sin_a, P["ax_q_norm"], P["ax_k_norm"])
    ox, lse = _flash_fwd(qx, kx, vx)
    x3 = _mm(ox, W["out_odd"], name="out_odd", extras=(x2,), epilogue=_add_epilogue)
    x4, mlp1 = _mlp_fwd(x3, nmlp[1:2], W["mlp_up"][1], W["mlp_down"][1], "1")

    d4, d4b, loss_tile = _loss_and_grad(x4, target)

    d3, d3b, dnmlp1, dw_up1, dw_down1 = _mlp_bwd(x3, nmlp[1:2], W["mlp_up"][1], W["mlp_down"][1], mlp1, d4, d4b, "1")
    dox = _mm(d3b, W["out_odd"], tb=True, name="out_odd_dx")
    dw_out_odd = _mm(ox, d3b, ta=True, name="out_odd_dw", out_dtypes=(BF,))
    dqx, dkx, dvx = _flash_bwd(qx, kx, vx, ox, dox, lse)
    dproj_o, dgq_ax, dgk_ax = _prep_odd_bwd(proj_o, cos_a, sin_a, P["ax_q_norm"], P["ax_k_norm"], dqx, dkx, dvx)
    dw_in_odd = _mm(h2, dproj_o, ta=True, name="in_odd_dw", out_dtypes=(BF,))
    dh2 = _mm(dproj_o, W["in_odd"], tb=True, name="in_odd_dx")
    d2, d2b, dnmix1 = _rms_bwd(x2, nmix[1:2], dh2, d3, "mix_norm_bwd_1")

    d1, d1b, dnmlp0, dw_up0, dw_down0 = _mlp_bwd(x1, nmlp[0:1], W["mlp_up"][0], W["mlp_down"][0], mlp0, d2, d2b, "0")
    doa = _mm(d1b, wo_a, tb=True, name="out_even_a_dx")
    dob = _mm(d1b, wo_b, tb=True, name="out_even_b_dx")
    dw_out_even = jnp.concatenate([_mm(oa, d1b, ta=True, name="out_even_a_dw", out_dtypes=(BF,)),
                                   _mm(ob, d1b, ta=True, name="out_even_b_dw", out_dtypes=(BF,))], axis=0)
    dy_ret, dga, dret_norm = _ret_post_bwd(y_ret, proj_e, P["ret_norm"], doa)
    hb, hf = _ret_scan(qr, dy_ret, lg, (1, 1), (0, 1), "retention_state_grads")
    dqr, dkr, dvr, dlogit = _ret_bwd(qr, kr, vr, dy_ret, lg, logit, sf, sb, hf, hb)
    dqs, dks, dvs, dbias, dsink = _swa_bwd(qs, ks, vs, dob, bias, P["swa_sink"])
    dt5 = _t5_table_grad(dbias, bucket)
    dproj_e, dgq_swa, dgk_swa = _prep_even_bwd(proj_e, cos_r, sin_r, P["swa_q_norm"], P["swa_k_norm"],
                                               dqr, dkr, dvr, dga, dqs, dks, dvs)
    dw_in_even = _mm(h0, dproj_e, ta=True, name="in_even_dw", out_dtypes=(BF,))
    dh0 = _mm(dproj_e, W["in_even"], tb=True, name="in_even_dx")
    dx, _, dnmix0 = _rms_bwd(x, nmix[0:1], dh0, d1, "mix_norm_bwd_0")

    dW = {"in_even": dw_in_even, "out_even": dw_out_even, "in_odd": dw_in_odd, "out_odd": dw_out_odd,
          "mlp_up": jnp.stack([dw_up0, dw_up1]), "mlp_down": jnp.stack([dw_down0, dw_down1])}
    dP = {"norm_mix": jnp.concatenate([dnmix0, dnmix1], 0), "norm_mlp": jnp.concatenate([dnmlp0, dnmlp1], 0),
          "ret_decay_logit": dlogit[:, :, 0, 0].T, "ret_norm": dret_norm,
          "swa_q_norm": dgq_swa, "swa_k_norm": dgk_swa, "swa_sink": dsink[:, 0][None, :],
          "t5_table": dt5[:, :SWA_HEADS], "ax_q_norm": dgq_ax, "ax_k_norm": dgk_ax}
    return loss_tile, dx, dW, dP


MESH = pl.DeviceIdType.MESH
ANY_SPEC = pl.BlockSpec(memory_space=pl.ANY)
VMEM_SPEC = pl.BlockSpec(memory_space=pltpu.VMEM)


def _my_place():
    return lax.axis_index("x"), lax.axis_index("y"), lax.axis_index("c")


def _flip(place, k):
    x, y, c = place
    return (1 - x if k & 4 else x, 1 - y if k & 2 else y, 1 - c if k & 1 else c)


def _index(place):
    x, y, c = place
    return 4 * x + 2 * y + c


def _all_gather(shards):
    n = len(shards)

    def body(*refs):
        ins, outs, stage = refs[:n], refs[n:2 * n], refs[2 * n:3 * n]
        send_sems, recv_sems, local_sems = refs[3 * n:]
        me = _my_place()
        sibling = _flip(me, 1)
        chips = [_flip(me, 4), _flip(me, 2), _flip(me, 6)]

        def copy(a, k, block, to, src=None):
            dst = outs[a].at[_index(block)]
            return pltpu.make_async_remote_copy(
                src_ref=dst if src is None else src, dst_ref=dst,
                send_sem=send_sems.at[a, k], recv_sem=recv_sems.at[a, k], device_id=to, device_id_type=MESH)

        first, mine = [], []
        for a in range(n):
            stage[a][...] = ins[a][...].astype(stage[a].dtype)
            mine.append(pltpu.make_async_copy(stage[a], outs[a].at[_index(me)], local_sems.at[a]))
            mine[-1].start()
            first.append(copy(a, 0, me, sibling, src=stage[a]))
            first += [copy(a, 1 + j, me, chip, src=stage[a]) for j, chip in enumerate(chips)]
        for cp in first:
            cp.start()
        passed = []
        for a in range(n):
            for j, chip in enumerate(chips):
                copy(a, 1 + j, chip, me).wait_recv()
                passed.append(copy(a, 4 + j, chip, sibling))
                passed[-1].start()
        for a in range(n):
            copy(a, 0, sibling, me).wait_recv()
            for j, chip in enumerate(chips):
                copy(a, 4 + j, _flip(chip, 1), me).wait_recv()
        for cp in first + passed:
            cp.wait_send()
        for cp in mine:
            cp.wait()

    return pl.pallas_call(
        body, name="weights_all_gather",
        in_specs=[VMEM_SPEC] * n, out_specs=[ANY_SPEC] * n,
        out_shape=[jax.ShapeDtypeStruct((N_DEV,) + s.shape, BF) for s in shards],
        scratch_shapes=[pltpu.VMEM(s.shape, BF) for s in shards]
        + [pltpu.SemaphoreType.DMA((n, 7)), pltpu.SemaphoreType.DMA((n, 7)), pltpu.SemaphoreType.DMA((n,))],
        compiler_params=pltpu.CompilerParams(vmem_limit_bytes=VMEM_SMALL),
    )(*shards)


def _exchange_blocks(parts):
    n = len(parts)

    def body(*refs):
        ins, outs = refs[:n], refs[n:2 * n]
        send_sems, recv_sems, local_sems = refs[2 * n:]
        me = _my_place()
        copies, mine = [], []
        for a in range(n):
            mine.append(pltpu.make_async_copy(ins[a].at[_index(me)], outs[a].at[_index(me)], local_sems.at[a]))
            mine[-1].start()
            for k in range(1, N_DEV):
                peer = _flip(me, k)
                copies.append(pltpu.make_async_remote_copy(
                    src_ref=ins[a].at[_index(peer)], dst_ref=outs[a].at[_index(me)],
                    send_sem=send_sems.at[a, k - 1], recv_sem=recv_sems.at[a, k - 1], device_id=peer, device_id_type=MESH))
                copies[-1].start()
        for a in range(n):
            for k in range(1, N_DEV):
                peer = _flip(me, k)
                pltpu.make_async_remote_copy(
                    src_ref=ins[a].at[_index(peer)], dst_ref=outs[a].at[_index(peer)],
                    send_sem=send_sems.at[a, k - 1], recv_sem=recv_sems.at[a, k - 1], device_id=peer,
                    device_id_type=MESH).wait_recv()
        for cp in copies:
            cp.wait_send()
        for cp in mine:
            cp.wait()

    return pl.pallas_call(
        body, name="grads_exchange",
        in_specs=[ANY_SPEC] * n, out_specs=[ANY_SPEC] * n,
        out_shape=[jax.ShapeDtypeStruct(p.shape, p.dtype) for p in parts],
        scratch_shapes=[pltpu.SemaphoreType.DMA((n, 7)), pltpu.SemaphoreType.DMA((n, 7)), pltpu.SemaphoreType.DMA((n,))],
    )(*parts)


def _all_reduce_small(part):
    R, C = part.shape

    def body(x_ref, o_ref, land_ref, send_sems, recv_sems):
        me = _my_place()
        land_ref[_index(me)] = x_ref[...]
        copies = []
        for k in range(1, N_DEV):
            peer = _flip(me, k)
            copies.append(pltpu.make_async_remote_copy(
                src_ref=x_ref, dst_ref=land_ref.at[_index(me)],
                send_sem=send_sems.at[k - 1], recv_sem=recv_sems.at[k - 1], device_id=peer, device_id_type=MESH))
            copies[-1].start()
        for k in range(1, N_DEV):
            peer = _flip(me, k)
            pltpu.make_async_remote_copy(
                src_ref=x_ref, dst_ref=land_ref.at[_index(peer)],
                send_sem=send_sems.at[k - 1], recv_sem=recv_sems.at[k - 1], device_id=peer, device_id_type=MESH).wait_recv()
        for cp in copies:
            cp.wait_send()
        acc = land_ref[0]
        for s in range(1, N_DEV):
            acc = acc + land_ref[s]
        o_ref[...] = acc

    return pl.pallas_call(
        body, name="small_all_reduce", in_specs=[VMEM_SPEC], out_specs=VMEM_SPEC,
        out_shape=jax.ShapeDtypeStruct((R, C), F32),
        scratch_shapes=[pltpu.VMEM((N_DEV, R, C), F32), pltpu.SemaphoreType.DMA((7,)), pltpu.SemaphoreType.DMA((7,))],
    )(part)


def _adamw_math(w, g, m, v):
    m = ADAM_B1 * m + (1.0 - ADAM_B1) * g
    v = ADAM_B2 * v + (1.0 - ADAM_B2) * jnp.square(g)
    m_hat = m / (1.0 - ADAM_B1 ** ADAM_STEP)
    v_hat = v / (1.0 - ADAM_B2 ** ADAM_STEP)
    delta = -ADAM_LR * (m_hat / (jnp.sqrt(v_hat) + ADAM_EPS) + ADAM_WD * w)
    return delta, m, v


def _sum_and_adamw(landed, w, m, v, name):
    R, C = w.shape
    tr = _pick(R, (256, 128))

    def body(l_ref, w_ref, m_ref, v_ref, g_ref, d_ref, nm_ref, nv_ref):
        g = l_ref[0].astype(F32)
        for s in range(1, N_DEV):
            g = g + l_ref[s].astype(F32)
        g_ref[...] = g
        d_ref[...], nm_ref[...], nv_ref[...] = _adamw_math(w_ref[...], g, m_ref[...], v_ref[...])

    row = pl.BlockSpec((tr, C), lambda i: (i, 0))
    return pl.pallas_call(
        body, name=name, grid=(R // tr,),
        in_specs=[pl.BlockSpec((N_DEV, tr, C), lambda i: (0, i, 0)), row, row, row], out_specs=[row] * 4,
        out_shape=[jax.ShapeDtypeStruct((R, C), F32)] * 4, compiler_params=_params(("parallel",)),
    )(landed, w, m, v)


def _adamw_small(w, g, m, v):
    def body(w_ref, g_ref, m_ref, v_ref, d_ref, nm_ref, nv_ref):
        d_ref[...], nm_ref[...], nv_ref[...] = _adamw_math(w_ref[...], g_ref[...], m_ref[...], v_ref[...])

    full = pl.BlockSpec(w.shape, lambda: (0, 0))
    return pl.pallas_call(body, name="adamw_small", in_specs=[full] * 4, out_specs=[full] * 3,
                          out_shape=[jax.ShapeDtypeStruct(w.shape, F32)] * 3)(w, g, m, v)


MATRICES = ("w_in_even", "w_out_even", "w_in_odd", "w_out_odd", "w_mlp_up", "w_mlp_down")
SMALL = ("norm_mix", "norm_mlp", "ret_decay_logit", "ret_norm", "swa_q_norm", "swa_k_norm", "swa_sink",
         "t5_table", "ax_q_norm", "ax_k_norm")


def _assemble(name, gathered):
    g = gathered
    if name in ("w_in_even", "w_in_odd"):
        return g[:, 0].transpose(1, 0, 2).reshape(g.shape[2], N_DEV * g.shape[3])
    if name in ("w_out_even", "w_out_odd"):
        return g.reshape(N_DEV * g.shape[2], g.shape[3])
    if name == "w_mlp_up":
        return g.transpose(1, 2, 0, 3).reshape(2, g.shape[2], N_DEV * g.shape[3])
    return g.transpose(1, 0, 2, 3).reshape(2, N_DEV * g.shape[2], g.shape[3])


def _split(name, full):
    if name in ("w_in_even", "w_in_odd"):
        K, N = full.shape
        return full.reshape(K, N_DEV, N // N_DEV).transpose(1, 0, 2)
    if name in ("w_out_even", "w_out_odd"):
        R, N = full.shape
        return full.reshape(N_DEV, R // N_DEV, N)
    if name == "w_mlp_up":
        _, K, N = full.shape
        return full.reshape(2, K, N_DEV, N // N_DEV).transpose(2, 0, 1, 3).reshape(N_DEV, 2 * K, N // N_DEV)
    _, R, N = full.shape
    return full.reshape(2, N_DEV, R // N_DEV, N).transpose(1, 0, 2, 3).reshape(N_DEV, 2 * R // N_DEV, N)


SMALL_ROWS = 8
SMALL_AT = {"norm_mix": (0, 0), "norm_mlp": (2, 0), "ret_norm": (4, 0), "swa_q_norm": (5, 0), "swa_k_norm": (5, 128),
            "ax_q_norm": (5, 256), "ax_k_norm": (5, 384), "swa_sink": (5, 512), "ret_decay_logit": (5, 640),
            "t5_table": (6, 0)}
LOSS_AT = (5, 768)


def _pack_small(arrays, loss=None):
    buf = jnp.zeros((SMALL_ROWS, 1024), F32)
    for name, (r, c) in SMALL_AT.items():
        a = arrays[name].astype(F32)
        a = a.reshape(1, -1) if name in ("ret_decay_logit", "t5_table") else a.reshape(-1, a.shape[-1])
        buf = lax.dynamic_update_slice(buf, a, (r, c))
    if loss is not None:
        buf = lax.dynamic_update_slice(buf, loss.reshape(1, 1), LOSS_AT)
    return buf


def _unpack_small(buf, like):
    out = {}
    for name, (r, c) in SMALL_AT.items():
        shape = like[name].shape
        rows = 1 if name in ("ret_decay_logit", "t5_table") else math.prod(shape[:-1])
        cols = math.prod(shape) // rows
        out[name] = buf[r:r + rows, c:c + cols].reshape(shape)
    return out


def kernel(x, norm_mix, norm_mlp, w_in_even, w_out_even, ret_decay_logit, ret_norm, swa_q_norm, swa_k_norm, swa_sink, t5_table, w_in_odd, w_out_odd, ax_q_norm, ax_k_norm, w_mlp_up, w_mlp_down, loss_target, m_norm_mix, m_norm_mlp, m_w_in_even, m_w_out_even, m_ret_decay_logit, m_ret_norm, m_swa_q_norm, m_swa_k_norm, m_swa_sink, m_t5_table, m_w_in_odd, m_w_out_odd, m_ax_q_norm, m_ax_k_norm, m_w_mlp_up, m_w_mlp_down, v_norm_mix, v_norm_mlp, v_w_in_even, v_w_out_even, v_ret_decay_logit, v_ret_norm, v_swa_q_norm, v_swa_k_norm, v_swa_sink, v_t5_table, v_w_in_odd, v_w_out_odd, v_ax_q_norm, v_ax_k_norm, v_w_mlp_up, v_w_mlp_down):
    given = dict(locals())
    weights = {n: given[n] for n in MATRICES + SMALL}
    moments_m = {n: given["m_" + n] for n in MATRICES + SMALL}
    moments_v = {n: given["v_" + n] for n in MATRICES + SMALL}

    gathered = _all_gather([weights[n] for n in MATRICES])
    W = {n[2:]: _assemble(n, g) for n, g in zip(MATRICES, gathered)}
    P = {"norm_mix": norm_mix, "norm_mlp": norm_mlp, "ret_decay_logit": ret_decay_logit[0], "ret_norm": ret_norm,
         "swa_q_norm": swa_q_norm, "swa_k_norm": swa_k_norm, "swa_sink": swa_sink, "t5_table": t5_table,
         "ax_q_norm": ax_q_norm, "ax_k_norm": ax_k_norm}

    loss_tile, dx, dW, dP = _local_step(x[0], loss_target[0], W, P)

    landed = _exchange_blocks([_split(n, dW[n[2:]]) for n in MATRICES])
    grads, deltas, new_m, new_v = {}, {}, {}, {}
    for n, l in zip(MATRICES, landed):
        shape = weights[n].shape
        flat = (l.shape[1], l.shape[2])
        g, d, nm, nv = _sum_and_adamw(l, weights[n].reshape(flat), moments_m[n].reshape(flat),
                                      moments_v[n].reshape(flat), "adamw_" + n)
        grads[n], deltas[n], new_m[n], new_v[n] = (t.reshape(shape) for t in (g, d, nm, nv))

    dP["ret_decay_logit"] = dP["ret_decay_logit"][None]
    total = _all_reduce_small(_pack_small(dP, loss_tile[0, 0]))
    loss = total[LOSS_AT[0], LOSS_AT[1]]
    small_d, small_m, small_v = _adamw_small(_pack_small(weights), total, _pack_small(moments_m), _pack_small(moments_v))
    like = {n: weights[n] for n in SMALL}
    for out, buf in ((grads, total), (deltas, small_d), (new_m, small_m), (new_v, small_v)):
        out.update(_unpack_small(buf, like))

    order = ("norm_mix", "norm_mlp", "w_in_even", "w_out_even", "ret_decay_logit", "ret_norm", "swa_q_norm", "swa_k_norm",
             "swa_sink", "t5_table", "w_in_odd", "w_out_odd", "ax_q_norm", "ax_k_norm", "w_mlp_up", "w_mlp_down")
    return (loss, dx[None], *[grads[n] for n in order], *[deltas[n] for n in order],
            *[new_m[n] for n in order], *[new_v[n] for n in order])
```

```python
import math

import jax
import jax.numpy as jnp
from jax import lax
from jax.experimental import pallas as pl
from jax.experimental.pallas import tpu as pltpu

F32 = jnp.float32
BF = jnp.bfloat16

D_MODEL = 1024
HEAD_DIM = 128
EPS = 1e-6
NEG_INF = -1e30
RET_HEADS, RET_DK, RET_DV = 4, 128, 256
RET_THETA = 10000.0
SWA_HEADS, SWA_KV_HEADS, WINDOW, BLOCK = 8, 2, 128, 128
T5_BUCKETS, T5_MAX_DIST = 32, 128
AX_HEADS, AX_KV_HEADS, AX_THETA, GRID_W = 8, 2, 10000.0, 64
D_FF = 4096
ATT_SCALE = HEAD_DIM ** -0.5
RET_SCALE = RET_DK ** -0.5
N_DEV = 8

ADAM_LR, ADAM_B1, ADAM_B2, ADAM_EPS, ADAM_WD, ADAM_STEP = 0.001, 0.9, 0.999, 1e-08, 0.01, 10

MIB = 1024 * 1024
VMEM_SMALL = 40 * MIB
VMEM_LARGE = 56 * MIB

OFF_QA, OFF_KA, OFF_VA, OFF_GA, OFF_QB, OFF_KB, OFF_VB = 0, 512, 1024, 2048, 3072, 4096, 4352
EVEN_IN = 4608
ODD_IN = 1536

NT = (((1,), (1,)), ((), ()))
TN = (((0,), (0,)), ((), ()))
NN = (((1,), (0,)), ((), ()))


def _dot(a, b, dims=NN):
    return lax.dot_general(a, b, dims, preferred_element_type=F32)


def _params(sem=None, vmem=VMEM_SMALL):
    return pltpu.CompilerParams(dimension_semantics=sem, vmem_limit_bytes=vmem)


def _pick(n, prefs):
    for p in prefs:
        if n % p == 0:
            return p
    return n


def _row_sum(x):
    return jnp.sum(x, axis=0, keepdims=True)


def _all_sum(x):
    return jnp.sum(jnp.sum(x, axis=0, keepdims=True), axis=1, keepdims=True)


def _sigmoid(x):
    return 1.0 / (1.0 + jnp.exp(-x))


SMEM_SPEC = pl.BlockSpec(memory_space=pltpu.SMEM)
ANY_SPEC = pl.BlockSpec(memory_space=pl.ANY)
VMEM_SPEC = pl.BlockSpec(memory_space=pltpu.VMEM)
MESH = pl.DeviceIdType.MESH


def _my_place():
    return lax.axis_index("x"), lax.axis_index("y"), lax.axis_index("c")


def _flip(place, k):
    x, y, c = place
    return (1 - x if k & 4 else x, 1 - y if k & 2 else y, 1 - c if k & 1 else c)


def _index(place):
    x, y, c = place
    return 4 * x + 2 * y + c


class _Exchange:
    def __init__(self, sources, gather):
        self.sources, self.gather, self.n = list(sources), gather, len(sources)
        self.out_shape = [jax.ShapeDtypeStruct(((N_DEV,) + s.shape) if gather else s.shape, s.dtype) for s in self.sources]
        self.scratch = [pltpu.SemaphoreType.DMA((self.n, 7)), pltpu.SemaphoreType.DMA((self.n, 7)),
                        pltpu.SemaphoreType.DMA((self.n,))]

    def _copies(self, ins, outs, sems):
        send_sems, recv_sems, local_sems = sems
        me = _my_place()
        local, sends, arrivals = [], [], []
        for a in range(self.n):
            def source(place, a=a):
                return ins[a] if self.gather else ins[a].at[_index(place)]

            local.append(pltpu.make_async_copy(source(me), outs[a].at[_index(me)], local_sems.at[a]))
            for k in range(1, N_DEV):
                peer = _flip(me, k)
                pair = dict(send_sem=send_sems.at[a, k - 1], recv_sem=recv_sems.at[a, k - 1], device_id=peer, device_id_type=MESH)
                sends.append(pltpu.make_async_remote_copy(src_ref=source(peer), dst_ref=outs[a].at[_index(me)], **pair))
                arrivals.append(pltpu.make_async_remote_copy(src_ref=source(peer), dst_ref=outs[a].at[_index(peer)], **pair))
        return local, sends, arrivals

    def start(self, ins, outs, sems):
        local, sends, _ = self._copies(ins, outs, sems)
        for cp in local + sends:
            cp.start()

    def wait(self, ins, outs, sems):
        local, sends, arrivals = self._copies(ins, outs, sems)
        for cp in arrivals:
            cp.wait_recv()
        for cp in sends:
            cp.wait_send()
        for cp in local:
            cp.wait()


def _carry_call(body, args, *, name, grid, in_specs, out_specs, out_shape, scratch_shapes=(), vmem=VMEM_SMALL,
                semantics=None, carried=None):
    if carried is None:
        outs = pl.pallas_call(body, name=name, grid=grid, in_specs=in_specs, out_specs=out_specs, out_shape=out_shape,
                              scratch_shapes=list(scratch_shapes), compiler_params=_params(semantics, vmem))(*args)
        return list(outs), []
    ni, no, ns, nc = len(in_specs), len(out_specs), len(scratch_shapes), carried.n

    def full_body(*refs):
        ins, cin = refs[:ni], refs[ni:ni + nc]
        outs, cout = refs[ni + nc:ni + nc + no], refs[ni + nc + no:ni + 2 * nc + no]
        scratch, sems = refs[ni + 2 * nc + no:ni + 2 * nc + no + ns], refs[ni + 2 * nc + no + ns:]
        ids = [pl.program_id(d) for d in range(len(grid))]
        first, last = ids[0] == 0, ids[0] == grid[0] - 1
        for d in range(1, len(grid)):
            first, last = first & (ids[d] == 0), last & (ids[d] == grid[d] - 1)

        @pl.when(first)
        def _():
            carried.start(cin, cout, sems)

        body(*ins, *outs, *scratch)

        @pl.when(last)
        def _():
            carried.wait(cin, cout, sems)

    outs = pl.pallas_call(
        full_body, name=name, grid=grid, in_specs=list(in_specs) + [ANY_SPEC] * nc,
        out_specs=list(out_specs) + [ANY_SPEC] * nc, out_shape=list(out_shape) + carried.out_shape,
        scratch_shapes=list(scratch_shapes) + carried.scratch,
        compiler_params=_params(("arbitrary",) * len(grid), vmem))(*args, *carried.sources)
    return list(outs[:no]), list(outs[no:])


def _exchange_blocks(sources, gather, name):
    ex = _Exchange(sources, gather)

    def body(*refs):
        ins, outs, sems = refs[:ex.n], refs[ex.n:2 * ex.n], refs[2 * ex.n:]
        ex.start(ins, outs, sems)
        ex.wait(ins, outs, sems)

    return pl.pallas_call(body, name=name, in_specs=[ANY_SPEC] * ex.n, out_specs=[ANY_SPEC] * ex.n,
                          out_shape=ex.out_shape, scratch_shapes=ex.scratch)(*ex.sources)


def _mm(a, b, *, name, ta=False, tb=False, out_dtypes=(F32,), extras=(), epilogue=None,
        tm=1024, tn=512, tk=1024):
    M, K = (a.shape[1], a.shape[0]) if ta else a.shape
    N = b.shape[0] if tb else b.shape[1]
    assert K == (b.shape[1] if tb else b.shape[0])
    tm = _pick(M, (tm, 512, 256, 128))
    tn = _pick(N, (tn, 512, 384, 256, 128))
    tk = _pick(K, (tk, 1536, 512, 256, 128))
    nk = K // tk
    ne, no = len(extras), len(out_dtypes)
    dims = (((0 if ta else 1,), (1 if tb else 0,)), ((), ()))
    if epilogue is None:
        epilogue = lambda acc: (acc,)

    def body(a_ref, b_ref, *rest):
        extra_refs, out_refs = rest[:ne], rest[ne:ne + no]

        def finish(acc):
            outs = epilogue(acc, *[r[...] for r in extra_refs])
            for o_ref, o in zip(out_refs, outs):
                o_ref[...] = o.astype(o_ref.dtype)

        part = _dot(a_ref[...], b_ref[...], dims)
        if nk == 1:
            finish(part)
        else:
            acc_ref = rest[-1]
            k = pl.program_id(2)

            @pl.when(k == 0)
            def _():
                acc_ref[...] = part

            @pl.when(k > 0)
            def _():
                acc_ref[...] += part

            @pl.when(k == nk - 1)
            def _():
                finish(acc_ref[...])

    a_spec = pl.BlockSpec((tk, tm), lambda i, j, k: (k, i)) if ta else pl.BlockSpec((tm, tk), lambda i, j, k: (i, k))
    b_spec = pl.BlockSpec((tn, tk), lambda i, j, k: (j, k)) if tb else pl.BlockSpec((tk, tn), lambda i, j, k: (k, j))
    o_spec = pl.BlockSpec((tm, tn), lambda i, j, k: (i, j))
    outs = pl.pallas_call(
        body, name=name, grid=(M // tm, N // tn, nk),
        in_specs=[a_spec, b_spec] + [o_spec] * ne,
        out_specs=[o_spec] * no,
        out_shape=[jax.ShapeDtypeStruct((M, N), dt) for dt in out_dtypes],
        scratch_shapes=[pltpu.VMEM((tm, tn), F32)] if nk > 1 else [],
        compiler_params=_params(("parallel", "parallel", "arbitrary")),
    )(a, b, *extras)
    return outs[0] if no == 1 else outs


def _rms_fwd(x, g, name):
    S, Dm = x.shape
    tr = _pick(S, (512,))

    def body(x_ref, g_ref, o_ref):
        xv = x_ref[...]
        r = lax.rsqrt(jnp.mean(xv * xv, axis=-1, keepdims=True) + EPS)
        o_ref[...] = (xv * r * g_ref[...]).astype(o_ref.dtype)

    row = pl.BlockSpec((tr, Dm), lambda i: (i, 0))
    return pl.pallas_call(
        body, name=name, grid=(S // tr,),
        in_specs=[row, pl.BlockSpec((1, Dm), lambda i: (0, 0))], out_specs=row,
        out_shape=jax.ShapeDtypeStruct((S, Dm), BF), compiler_params=_params(("parallel",)),
    )(x, g)


def _rms_bwd(x, g, dh, dres, name):
    S, Dm = x.shape
    tr = _pick(S, (512,))

    def body(x_ref, g_ref, dh_ref, dres_ref, dx_ref, dxb_ref, dg_ref):
        xv = x_ref[...]
        r = lax.rsqrt(jnp.mean(xv * xv, axis=-1, keepdims=True) + EPS)
        xh = xv * r
        dy = dh_ref[...].astype(F32)
        dxh = dy * g_ref[...]
        dx = r * (dxh - xh * jnp.mean(dxh * xh, axis=-1, keepdims=True)) + dres_ref[...]
        dx_ref[...] = dx
        dxb_ref[...] = dx.astype(dxb_ref.dtype)

        @pl.when(pl.program_id(0) == 0)
        def _():
            dg_ref[...] = jnp.zeros_like(dg_ref)

        dg_ref[...] += _row_sum(dy * xh)

    row = pl.BlockSpec((tr, Dm), lambda i: (i, 0))
    vec = pl.BlockSpec((1, Dm), lambda i: (0, 0))
    return pl.pallas_call(
        body, name=name, grid=(S // tr,),
        in_specs=[row, vec, row, row], out_specs=[row, row, vec],
        out_shape=[jax.ShapeDtypeStruct((S, Dm), F32), jax.ShapeDtypeStruct((S, Dm), BF),
                   jax.ShapeDtypeStruct((1, Dm), F32)],
        compiler_params=_params(("arbitrary",)),
    )(x, g, dh, dres)


def _loss_and_grad(y, target):
    S, Dm = y.shape
    tr = _pick(S, (512,))
    n = S // tr

    def body(y_ref, t_ref, dy_ref, dyb_ref, loss_ref, acc_ref):
        i = pl.program_id(0)
        e = y_ref[...] - t_ref[...]
        d = e * (1.0 / Dm)
        dy_ref[...] = d
        dyb_ref[...] = d.astype(dyb_ref.dtype)

        @pl.when(i == 0)
        def _():
            acc_ref[...] = jnp.zeros_like(acc_ref)

        acc_ref[...] += _row_sum(e * e)

        @pl.when(i == n - 1)
        def _():
            loss_ref[...] = jnp.broadcast_to(_all_sum(acc_ref[...]) * (0.5 / Dm), loss_ref.shape)

    row = pl.BlockSpec((tr, Dm), lambda i: (i, 0))
    return pl.pallas_call(
        body, name="loss_head", grid=(n,),
        in_specs=[row, row], out_specs=[row, row, pl.BlockSpec((8, 128), lambda i: (0, 0))],
        out_shape=[jax.ShapeDtypeStruct((S, Dm), F32), jax.ShapeDtypeStruct((S, Dm), BF),
                   jax.ShapeDtypeStruct((8, 128), F32)],
        scratch_shapes=[pltpu.VMEM((1, Dm), F32)],
        compiler_params=_params(("arbitrary",)),
    )(y, target)


def _partner(x, half):
    if half == 64:
        return pltpu.roll(x, 64, 1)
    lane = lax.broadcasted_iota(jnp.int32, x.shape, 1)
    return jnp.where((lane % (2 * half)) < half, pltpu.roll(x, 128 - half, 1), pltpu.roll(x, half, 1))


def _rope(x, cos, sin, half):
    return x * cos + _partner(x, half) * sin


def _rope_t(dy, cos, sin, half):
    return dy * cos - _partner(dy, half) * sin


def _head_norm(x):
    r = lax.rsqrt(jnp.mean(x * x, axis=-1, keepdims=True) + EPS)
    return x * r, r


def _head_norm_bwd(dxh, xh, r):
    return r * (dxh - xh * jnp.mean(dxh * xh, axis=-1, keepdims=True))


def _cols(ref, off, width=HEAD_DIM):
    return ref[:, off:off + width]


def _prep_even_fwd(proj, cos, sin, gq, gk):
    S = proj.shape[0]
    tr = _pick(S, (256,))

    def body(p_ref, cos_ref, sin_ref, gq_ref, gk_ref, qr_ref, kr_ref, vr_ref, qs_ref, ks_ref, vs_ref):
        cos_v, sin_v = cos_ref[...], sin_ref[...]
        for h in range(RET_HEADS):
            o = h * RET_DK
            qr_ref[:, o:o + RET_DK] = _rope(_cols(p_ref, OFF_QA + o), cos_v, sin_v, 64).astype(qr_ref.dtype)
            kr_ref[:, o:o + RET_DK] = (_rope(_cols(p_ref, OFF_KA + o), cos_v, sin_v, 64) * RET_SCALE).astype(kr_ref.dtype)
        vr_ref[...] = p_ref[:, OFF_VA:OFF_VA + 1024].astype(vr_ref.dtype)
        for h in range(SWA_HEADS):
            o = h * HEAD_DIM
            xh, _ = _head_norm(_cols(p_ref, OFF_QB + o))
            qs_ref[:, o:o + HEAD_DIM] = (xh * gq_ref[...] * ATT_SCALE).astype(qs_ref.dtype)
        for h in range(SWA_KV_HEADS):
            o = h * HEAD_DIM
            xh, _ = _head_norm(_cols(p_ref, OFF_KB + o))
            ks_ref[:, o:o + HEAD_DIM] = (xh * gk_ref[...]).astype(ks_ref.dtype)
        vs_ref[...] = p_ref[:, OFF_VB:OFF_VB + 256].astype(vs_ref.dtype)

    def row(w):
        return pl.BlockSpec((tr, w), lambda i: (i, 0))

    vec = pl.BlockSpec((1, HEAD_DIM), lambda i: (0, 0))
    widths = (512, 512, 1024, 1024, 256, 256)
    return pl.pallas_call(
        body, name="prep_even_fwd", grid=(S // tr,),
        in_specs=[row(EVEN_IN), row(128), row(128), vec, vec],
        out_specs=[row(w) for w in widths],
        out_shape=[jax.ShapeDtypeStruct((S, w), BF) for w in widths],
        compiler_params=_params(("parallel",)),
    )(proj, cos, sin, gq, gk)


def _prep_even_bwd(proj, cos, sin, gq, gk, dqr, dkr, dvr, dga, dqs, dks, dvs):
    S = proj.shape[0]
    tr = _pick(S, (256,))

    def body(p_ref, cos_ref, sin_ref, gq_ref, gk_ref, dqr_ref, dkr_ref, dvr_ref, dga_ref, dqs_ref, dks_ref,
             dvs_ref, dp_ref, dgq_ref, dgk_ref):
        cos_v, sin_v = cos_ref[...], sin_ref[...]
        dt = dp_ref.dtype
        for h in range(RET_HEADS):
            o = h * RET_DK
            dp_ref[:, OFF_QA + o:OFF_QA + o + RET_DK] = _rope_t(_cols(dqr_ref, o), cos_v, sin_v, 64).astype(dt)
            dp_ref[:, OFF_KA + o:OFF_KA + o + RET_DK] = _rope_t(_cols(dkr_ref, o) * RET_SCALE, cos_v, sin_v, 64).astype(dt)
        dp_ref[:, OFF_VA:OFF_VA + 1024] = dvr_ref[...].astype(dt)
        dp_ref[:, OFF_GA:OFF_GA + 1024] = dga_ref[...].astype(dt)
        dgq = jnp.zeros((1, HEAD_DIM), F32)
        for h in range(SWA_HEADS):
            o = h * HEAD_DIM
            xh, r = _head_norm(_cols(p_ref, OFF_QB + o))
            dy = _cols(dqs_ref, o) * ATT_SCALE
            dgq = dgq + _row_sum(dy * xh)
            dp_ref[:, OFF_QB + o:OFF_QB + o + HEAD_DIM] = _head_norm_bwd(dy * gq_ref[...], xh, r).astype(dt)
        dgk = jnp.zeros((1, HEAD_DIM), F32)
        for h in range(SWA_KV_HEADS):
            o = h * HEAD_DIM
            xh, r = _head_norm(_cols(p_ref, OFF_KB + o))
            dy = _cols(dks_ref, o)
            dgk = dgk + _row_sum(dy * xh)
            dp_ref[:, OFF_KB + o:OFF_KB + o + HEAD_DIM] = _head_norm_bwd(dy * gk_ref[...], xh, r).astype(dt)
        dp_ref[:, OFF_VB:OFF_VB + 256] = dvs_ref[...].astype(dt)

        @pl.when(pl.program_id(0) == 0)
        def _():
            dgq_ref[...] = jnp.zeros_like(dgq_ref)
            dgk_ref[...] = jnp.zeros_like(dgk_ref)

        dgq_ref[...] += dgq
        dgk_ref[...] += dgk

    def row(w):
        return pl.BlockSpec((tr, w), lambda i: (i, 0))

    vec = pl.BlockSpec((1, HEAD_DIM), lambda i: (0, 0))
    return pl.pallas_call(
        body, name="prep_even_bwd", grid=(S // tr,),
        in_specs=[row(EVEN_IN), row(128), row(128), vec, vec, row(512), row(512), row(1024), row(1024),
                  row(1024), row(256), row(256)],
        out_specs=[row(EVEN_IN), vec, vec],
        out_shape=[jax.ShapeDtypeStruct((S, EVEN_IN), BF), jax.ShapeDtypeStruct((1, HEAD_DIM), F32),
                   jax.ShapeDtypeStruct((1, HEAD_DIM), F32)],
        compiler_params=_params(("arbitrary",)),
    )(proj, cos, sin, gq, gk, dqr, dkr, dvr, dga, dqs, dks, dvs)


RET_CHUNK = 512
def _log_sigmoid_tile(logit_tile):
    def body(x_ref, o_ref):
        xv = x_ref[...]
        t = jnp.exp(-jnp.abs(xv))
        log1p_t = jnp.where(t < 1e-3, t * (1.0 - 0.5 * t), jnp.log(1.0 + t))
        o_ref[...] = jnp.minimum(xv, 0.0) - log1p_t

    full = pl.BlockSpec((8, 128), lambda: (0, 0))
    return pl.pallas_call(body, name="log_sigmoid", in_specs=[full], out_specs=full,
                          out_shape=jax.ShapeDtypeStruct((8, 128), F32))(logit_tile)


def _decay(diff, lf, lb):
    return jnp.exp(jnp.where(diff >= 0, lf * diff, -(lb * diff)))


def _col_iota(n):
    return lax.broadcasted_iota(jnp.int32, (n, 1), 0).astype(F32)


def _ret_scan(x, z, lg, asc, desc, name):
    S = x.shape[0]
    C = _pick(S, (RET_CHUNK,))
    nc = S // C
    (arow, aoff), (drow, doff) = asc, desc

    def body(lg_ref, xa_ref, za_ref, xd_ref, zd_ref, asc_ref, desc_ref, sa_ref, sd_ref):
        h, t = pl.program_id(0), pl.program_id(1)
        la, ld = lg_ref[arow, h], lg_ref[drow, h]

        @pl.when(t == 0)
        def _():
            sa_ref[...] = jnp.zeros_like(sa_ref)
            sd_ref[...] = jnp.zeros_like(sd_ref)

        asc_ref[0, 0] = sa_ref[...]
        desc_ref[0, 0] = sd_ref[...]
        j = _col_iota(C)
        xa = (xa_ref[...].astype(F32) * jnp.exp(la * (C - 1 + aoff - j))).astype(xa_ref.dtype)
        xd = (xd_ref[...].astype(F32) * jnp.exp(ld * (j + doff))).astype(xd_ref.dtype)
        sa_ref[...] = jnp.exp(jnp.full((1, RET_DV), la * C, F32)) * sa_ref[...] + _dot(xa, za_ref[...], TN)
        sd_ref[...] = jnp.exp(jnp.full((1, RET_DV), ld * C, F32)) * sd_ref[...] + _dot(xd, zd_ref[...], TN)

    state = jax.ShapeDtypeStruct((RET_HEADS, nc, RET_DK, RET_DV), F32)
    return pl.pallas_call(
        body, name=name, grid=(RET_HEADS, nc),
        in_specs=[SMEM_SPEC,
                  pl.BlockSpec((C, RET_DK), lambda h, t: (t, h)), pl.BlockSpec((C, RET_DV), lambda h, t: (t, h)),
                  pl.BlockSpec((C, RET_DK), lambda h, t: (nc - 1 - t, h)), pl.BlockSpec((C, RET_DV), lambda h, t: (nc - 1 - t, h))],
        out_specs=[pl.BlockSpec((1, 1, RET_DK, RET_DV), lambda h, t: (h, t, 0, 0)),
                   pl.BlockSpec((1, 1, RET_DK, RET_DV), lambda h, t: (h, nc - 1 - t, 0, 0))],
        out_shape=[state, state],
        scratch_shapes=[pltpu.VMEM((RET_DK, RET_DV), F32), pltpu.VMEM((RET_DK, RET_DV), F32)],
        compiler_params=_params(("parallel", "arbitrary")),
    )(lg, x, z, x, z)


def _ret_fwd(q, k, v, lg, sf, sb):
    S = q.shape[0]
    C = _pick(S, (RET_CHUNK,))

    def body(lg_ref, q_ref, k_ref, v_ref, sf_ref, sb_ref, y_ref):
        h = pl.program_id(0)
        lf, lb = lg_ref[0, h], lg_ref[1, h]
        qv = q_ref[...]
        dt = qv.dtype
        diff = (lax.broadcasted_iota(jnp.int32, (C, C), 0) - lax.broadcasted_iota(jnp.int32, (C, C), 1)).astype(F32)
        y = _dot((_dot(qv, k_ref[...], NT) * _decay(diff, lf, lb)).astype(dt), v_ref[...])
        r = _col_iota(C)
        qf = qv.astype(F32)
        y = y + _dot((qf * jnp.exp(lf * (r + 1.0))).astype(dt), sf_ref[0, 0].astype(dt))
        y_ref[...] = y + _dot((qf * jnp.exp(lb * (C - r))).astype(dt), sb_ref[0, 0].astype(dt))

    state = pl.BlockSpec((1, 1, RET_DK, RET_DV), lambda h, c: (h, c, 0, 0))
    qk = pl.BlockSpec((C, RET_DK), lambda h, c: (c, h))
    vy = pl.BlockSpec((C, RET_DV), lambda h, c: (c, h))
    return pl.pallas_call(
        body, name="retention_fwd", grid=(RET_HEADS, S // C),
        in_specs=[SMEM_SPEC, qk, qk, vy, state, state], out_specs=vy,
        out_shape=jax.ShapeDtypeStruct((S, RET_HEADS * RET_DV), F32),
        compiler_params=_params(("parallel", "parallel")),
    )(lg, q, k, v, sf, sb)


def _ret_bwd(q, k, v, dy, lg, logit, sf, sb, hf, hb):
    S = q.shape[0]
    C = _pick(S, (RET_CHUNK,))
    nc = S // C

    def body(lg_ref, logit_ref, q_ref, k_ref, v_ref, dy_ref, sf_ref, sb_ref, hf_ref, hb_ref,
             dq_ref, dk_ref, dv_ref, dlg_ref, acc_ref):
        h, c = pl.program_id(0), pl.program_id(1)
        lf, lb = lg_ref[0, h], lg_ref[1, h]

        @pl.when(c == 0)
        def _():
            acc_ref[...] = jnp.zeros_like(acc_ref)

        qv, kv, vv, dyv = q_ref[...], k_ref[...], v_ref[...], dy_ref[...]
        dt = qv.dtype
        qf, kf = qv.astype(F32), kv.astype(F32)
        diff = (lax.broadcasted_iota(jnp.int32, (C, C), 0) - lax.broadcasted_iota(jnp.int32, (C, C), 1)).astype(F32)
        dec = _decay(diff, lf, lb)
        sc = _dot(qv, kv, NT) * dec
        dp = _dot(dyv, vv, NT)
        da = (dp * dec).astype(dt)
        dq = _dot(da, kv)
        dk = _dot(da, qv, TN)
        dv = _dot(sc.astype(dt), dyv, TN)
        w = sc * dp * diff
        tot_w, tot_f = _all_sum(w), _all_sum(jnp.where(diff >= 0, w, 0.0))
        d_lf, d_lb = tot_f, tot_f - tot_w
        r = _col_iota(C)
        a, b = jnp.exp(lf * (r + 1.0)), jnp.exp(lb * (C - r))
        e, f = jnp.exp(lf * (C - 1.0 - r)), jnp.exp(lb * r)
        sfv, sbv, hfv, hbv = sf_ref[0, 0], sb_ref[0, 0], hf_ref[0, 0], hb_ref[0, 0]
        t_f, t_b = _dot(dyv, sfv.astype(dt), NT), _dot(dyv, sbv.astype(dt), NT)
        u_f, u_b = _dot(vv, hfv.astype(dt), NT), _dot(vv, hbv.astype(dt), NT)
        dq_ref[...] = dq + a * t_f + b * t_b
        dk_ref[...] = dk + e * u_f + f * u_b
        dv_ref[...] = dv + _dot((kf * e).astype(dt), hfv.astype(dt)) + _dot((kf * f).astype(dt), hbv.astype(dt))
        row_q_f = jnp.sum(qf * t_f, axis=-1, keepdims=True)
        row_q_b = jnp.sum(qf * t_b, axis=-1, keepdims=True)
        row_k_f = jnp.sum(kf * u_f, axis=-1, keepdims=True)
        row_k_b = jnp.sum(kf * u_b, axis=-1, keepdims=True)
        gf_c = jnp.exp(jnp.full((1, 1), lf * C, F32))
        gb_c = jnp.exp(jnp.full((1, 1), lb * C, F32))
        d_lf = d_lf + _all_sum((r + 1.0) * a * row_q_f + (C - 1.0 - r) * e * row_k_f) + C * gf_c * _all_sum(hfv * sfv)
        d_lb = d_lb + _all_sum((C - r) * b * row_q_b + r * f * row_k_b) + C * gb_c * _all_sum(hbv * sbv)
        acc_ref[0] += jnp.broadcast_to(d_lf, (8, 128))
        acc_ref[1] += jnp.broadcast_to(d_lb, (8, 128))

        @pl.when(c == nc - 1)
        def _():
            for d in range(2):
                gate = 1.0 / (1.0 + jnp.exp(jnp.full((8, 128), logit_ref[d, h], F32)))
                dlg_ref[0, d] = acc_ref[d] * gate

    state = pl.BlockSpec((1, 1, RET_DK, RET_DV), lambda h, c: (h, c, 0, 0))
    qk = pl.BlockSpec((C, RET_DK), lambda h, c: (c, h))
    vy = pl.BlockSpec((C, RET_DV), lambda h, c: (c, h))
    return pl.pallas_call(
        body, name="retention_bwd", grid=(RET_HEADS, nc),
        in_specs=[SMEM_SPEC, SMEM_SPEC, qk, qk, vy, vy, state, state, state, state],
        out_specs=[qk, qk, vy, pl.BlockSpec((1, 2, 8, 128), lambda h, c: (h, 0, 0, 0))],
        out_shape=[jax.ShapeDtypeStruct((S, RET_HEADS * RET_DK), F32), jax.ShapeDtypeStruct((S, RET_HEADS * RET_DK), F32),
                   jax.ShapeDtypeStruct((S, RET_HEADS * RET_DV), F32),
                   jax.ShapeDtypeStruct((RET_HEADS, 2, 8, 128), F32)],
        scratch_shapes=[pltpu.VMEM((2, 8, 128), F32)],
        compiler_params=_params(("parallel", "arbitrary")),
    )(lg, logit, q, k, v, dy, sf, sb, hf, hb)


def _ret_post_fwd(y, proj, gn):
    S = y.shape[0]
    tr = _pick(S, (512,))

    def body(y_ref, g_ref, gn_ref, o_ref):
        for h in range(RET_HEADS):
            o = h * RET_DV
            yh, _ = _head_norm(_cols(y_ref, o, RET_DV))
            gate = _cols(g_ref, o, RET_DV)
            o_ref[:, o:o + RET_DV] = (gate * _sigmoid(gate) * (yh * gn_ref[:, o:o + RET_DV])).astype(o_ref.dtype)

    row = pl.BlockSpec((tr, 1024), lambda i: (i, 0))
    return pl.pallas_call(
        body, name="retention_post_fwd", grid=(S // tr,),
        in_specs=[row, pl.BlockSpec((tr, 1024), lambda i: (i, OFF_GA // 1024)), pl.BlockSpec((1, 1024), lambda i: (0, 0))],
        out_specs=row, out_shape=jax.ShapeDtypeStruct((S, 1024), BF), compiler_params=_params(("parallel",)),
    )(y, proj, gn)


def _ret_post_bwd(y, proj, gn, do):
    S = y.shape[0]
    tr = _pick(S, (512,))

    def body(y_ref, g_ref, gn_ref, do_ref, dy_ref, dg_ref, dgn_ref):
        @pl.when(pl.program_id(0) == 0)
        def _():
            dgn_ref[...] = jnp.zeros_like(dgn_ref)

        for h in range(RET_HEADS):
            o = h * RET_DV
            yh, r = _head_norm(_cols(y_ref, o, RET_DV))
            gate = _cols(g_ref, o, RET_DV)
            gnh = gn_ref[:, o:o + RET_DV]
            dout = _cols(do_ref, o, RET_DV).astype(F32)
            sg = _sigmoid(gate)
            dz = dout * (gate * sg)
            dg_ref[:, o:o + RET_DV] = dout * (yh * gnh) * (sg * (1.0 + gate * (1.0 - sg)))
            dgn_ref[:, o:o + RET_DV] += _row_sum(dz * yh)
            dy_ref[:, o:o + RET_DV] = _head_norm_bwd(dz * gnh, yh, r).astype(dy_ref.dtype)

    row = pl.BlockSpec((tr, 1024), lambda i: (i, 0))
    vec = pl.BlockSpec((1, 1024), lambda i: (0, 0))
    return pl.pallas_call(
        body, name="retention_post_bwd", grid=(S // tr,),
        in_specs=[row, pl.BlockSpec((tr, 1024), lambda i: (i, OFF_GA // 1024)), vec, row],
        out_specs=[row, row, vec],
        out_shape=[jax.ShapeDtypeStruct((S, 1024), BF), jax.ShapeDtypeStruct((S, 1024), F32),
                   jax.ShapeDtypeStruct((1, 1024), F32)],
        compiler_params=_params(("arbitrary",)),
    )(y, proj, gn, do)


def _t5_bucket_map():
    r = jnp.arange(BLOCK)
    j = jnp.arange(3 * BLOCK)
    rel = j[None, :] - BLOCK - r[:, None]
    nb = T5_BUCKETS // 2
    max_exact = nb // 2
    ret = jnp.where(rel > 0, nb, 0)
    n = jnp.abs(rel)
    nf = jnp.maximum(n, 1).astype(jnp.float32)
    large = max_exact + (jnp.log(nf / max_exact) / math.log(T5_MAX_DIST / max_exact)
                         * (nb - max_exact)).astype(jnp.int32)
    large = jnp.minimum(large, nb - 1)
    bucket = ret + jnp.where(n < max_exact, n, large)
    return jnp.where(jnp.abs(rel) <= WINDOW, bucket, -1).astype(jnp.int32)


def _t5_bias(table, bucket):
    def body(t_ref, b_ref, o_ref):
        bk = b_ref[...]
        for h in range(SWA_HEADS):
            acc = jnp.full(bk.shape, NEG_INF, F32)
            for b in range(T5_BUCKETS):
                acc = jnp.where(bk == b, t_ref[b, h], acc)
            o_ref[h] = acc

    return pl.pallas_call(
        body, name="t5_bias", in_specs=[SMEM_SPEC, pl.BlockSpec((BLOCK, 3 * BLOCK), lambda: (0, 0))],
        out_specs=pl.BlockSpec((SWA_HEADS, BLOCK, 3 * BLOCK), lambda: (0, 0, 0)),
        out_shape=jax.ShapeDtypeStruct((SWA_HEADS, BLOCK, 3 * BLOCK), F32),
    )(table, bucket)


def _t5_table_grad(dbias, bucket):
    def body(d_ref, b_ref, o_ref):
        bk = b_ref[...]
        lane = lax.broadcasted_iota(jnp.int32, (1, 128), 1)
        for b in range(T5_BUCKETS):
            hit = bk == b
            row = jnp.zeros((1, 128), F32)
            for h in range(SWA_HEADS):
                row = row + jnp.where(lane == h, _all_sum(jnp.where(hit, d_ref[h], 0.0)), 0.0)
            o_ref[b:b + 1, :] = row

    return pl.pallas_call(
        body, name="t5_table_grad",
        in_specs=[pl.BlockSpec((SWA_HEADS, BLOCK, 3 * BLOCK), lambda: (0, 0, 0)), pl.BlockSpec((BLOCK, 3 * BLOCK), lambda: (0, 0))],
        out_specs=pl.BlockSpec((T5_BUCKETS, 128), lambda: (0, 0)),
        out_shape=jax.ShapeDtypeStruct((T5_BUCKETS, 128), F32),
    )(dbias, bucket)


def _swa_scores(i, nb, q, kw, bias_h, sink_h):
    s = _dot(q, kw, NT) + bias_h
    col = lax.broadcasted_iota(jnp.int32, s.shape, 1)
    first_col = jnp.where(i == 0, BLOCK, 0)
    end_col = jnp.where(i == nb - 1, 2 * BLOCK, 3 * BLOCK)
    s = jnp.where((col < first_col) | (col >= end_col), NEG_INF, s)
    m = jnp.maximum(jnp.max(s, axis=-1, keepdims=True), sink_h)
    p = jnp.exp(s - m)
    e_sink = jnp.exp(sink_h - m)
    inv = 1.0 / (jnp.sum(p, axis=-1, keepdims=True) + e_sink)
    return p * inv, e_sink * inv


def _swa_window(ref, i, nb, off):
    prev, nxt = jnp.maximum(i - 1, 0), jnp.minimum(i + 1, nb - 1)
    rows = [pl.ds(pl.multiple_of(b * BLOCK, BLOCK), BLOCK) for b in (prev, i, nxt)]
    return jnp.concatenate([ref[r, off:off + HEAD_DIM] for r in rows], axis=0), rows


def _swa_fwd(q, k, v, bias, sink, carried=None):
    S = q.shape[0]
    nb = S // BLOCK
    G = SWA_HEADS // SWA_KV_HEADS

    def body(sink_ref, q_ref, k_ref, v_ref, bias_ref, o_ref):
        i = pl.program_id(0)
        for kh in range(SWA_KV_HEADS):
            kw, _ = _swa_window(k_ref, i, nb, kh * HEAD_DIM)
            vw, _ = _swa_window(v_ref, i, nb, kh * HEAD_DIM)
            for g in range(G):
                h = kh * G + g
                p, _ = _swa_scores(i, nb, _cols(q_ref, h * HEAD_DIM), kw, bias_ref[h], sink_ref[0, h])
                o_ref[:, h * HEAD_DIM:(h + 1) * HEAD_DIM] = _dot(p.astype(vw.dtype), vw).astype(o_ref.dtype)

    full_kv = pl.BlockSpec((S, SWA_KV_HEADS * HEAD_DIM), lambda i: (0, 0))
    (o,), landed = _carry_call(
        body, (sink, q, k, v, bias), name="swa_fwd", grid=(nb,),
        in_specs=[SMEM_SPEC, pl.BlockSpec((BLOCK, 1024), lambda i: (i, 0)), full_kv, full_kv,
                  pl.BlockSpec((SWA_HEADS, BLOCK, 3 * BLOCK), lambda i: (0, 0, 0))],
        out_specs=[pl.BlockSpec((BLOCK, 1024), lambda i: (i, 0))],
        out_shape=[jax.ShapeDtypeStruct((S, 1024), BF)], semantics=("parallel",), carried=carried)
    return o, landed


def _swa_bwd(q, k, v, do, bias, sink, carried=None):
    S = q.shape[0]
    nb = S // BLOCK
    G = SWA_HEADS // SWA_KV_HEADS

    def body(sink_ref, q_ref, k_ref, v_ref, do_ref, bias_ref, dq_ref, dk_ref, dv_ref, dbias_ref, dsink_ref):
        i = pl.program_id(0)

        @pl.when(i == 0)
        def _():
            dk_ref[...] = jnp.zeros_like(dk_ref)
            dv_ref[...] = jnp.zeros_like(dv_ref)
            dbias_ref[...] = jnp.zeros_like(dbias_ref)
            dsink_ref[...] = jnp.zeros_like(dsink_ref)

        for kh in range(SWA_KV_HEADS):
            off = kh * HEAD_DIM
            kw, rows = _swa_window(k_ref, i, nb, off)
            vw, _ = _swa_window(v_ref, i, nb, off)
            dkw = jnp.zeros((3 * BLOCK, HEAD_DIM), F32)
            dvw = jnp.zeros((3 * BLOCK, HEAD_DIM), F32)
            for g in range(G):
                h = kh * G + g
                qh = _cols(q_ref, h * HEAD_DIM)
                p, p_sink = _swa_scores(i, nb, qh, kw, bias_ref[h], sink_ref[0, h])
                doh = _cols(do_ref, h * HEAD_DIM).astype(vw.dtype)
                dp = _dot(doh, vw, NT)
                delta = jnp.sum(p * dp, axis=-1, keepdims=True)
                ds = p * (dp - delta)
                dsb = ds.astype(qh.dtype)
                dq_ref[:, h * HEAD_DIM:(h + 1) * HEAD_DIM] = _dot(dsb, kw)
                dkw = dkw + _dot(dsb, qh, TN)
                dvw = dvw + _dot(p.astype(doh.dtype), doh, TN)
                dbias_ref[h] += ds
                dsink_ref[h:h + 1, :] += jnp.broadcast_to(-_row_sum(p_sink * delta), (1, 128))
            for b, r in enumerate(rows):
                dk_ref[r, off:off + HEAD_DIM] += dkw[b * BLOCK:(b + 1) * BLOCK]
                dv_ref[r, off:off + HEAD_DIM] += dvw[b * BLOCK:(b + 1) * BLOCK]

    full_kv = pl.BlockSpec((S, SWA_KV_HEADS * HEAD_DIM), lambda i: (0, 0))
    blk = pl.BlockSpec((BLOCK, 1024), lambda i: (i, 0))
    bias_spec = pl.BlockSpec((SWA_HEADS, BLOCK, 3 * BLOCK), lambda i: (0, 0, 0))
    outs, landed = _carry_call(
        body, (sink, q, k, v, do, bias), name="swa_bwd", grid=(nb,),
        in_specs=[SMEM_SPEC, blk, full_kv, full_kv, blk, bias_spec],
        out_specs=[blk, full_kv, full_kv, bias_spec, pl.BlockSpec((8, 128), lambda i: (0, 0))],
        out_shape=[jax.ShapeDtypeStruct((S, 1024), F32), jax.ShapeDtypeStruct((S, 256), F32),
                   jax.ShapeDtypeStruct((S, 256), F32),
                   jax.ShapeDtypeStruct((SWA_HEADS, BLOCK, 3 * BLOCK), F32), jax.ShapeDtypeStruct((8, 128), F32)],
        vmem=VMEM_LARGE, semantics=("arbitrary",), carried=carried)
    return (*outs, landed)


def _prep_odd_fwd(proj, cos, sin, gq, gk):
    S = proj.shape[0]
    tr = _pick(S, (512,))

    def body(p_ref, cos_ref, sin_ref, gq_ref, gk_ref, q_ref, k_ref, v_ref):
        cos_v, sin_v = cos_ref[...], sin_ref[...]
        for h in range(AX_HEADS):
            o = h * HEAD_DIM
            xh, _ = _head_norm(_cols(p_ref, o))
            q_ref[:, o:o + HEAD_DIM] = (_rope(xh * gq_ref[...], cos_v, sin_v, 32) * ATT_SCALE).astype(q_ref.dtype)
        for h in range(AX_KV_HEADS):
            o = h * HEAD_DIM
            xh, _ = _head_norm(_cols(p_ref, 1024 + o))
            k_ref[:, o:o + HEAD_DIM] = _rope(xh * gk_ref[...], cos_v, sin_v, 32).astype(k_ref.dtype)
        v_ref[...] = p_ref[:, 1280:1536].astype(v_ref.dtype)

    def row(w):
        return pl.BlockSpec((tr, w), lambda i: (i, 0))

    vec = pl.BlockSpec((1, HEAD_DIM), lambda i: (0, 0))
    return pl.pallas_call(
        body, name="prep_odd_fwd", grid=(S // tr,),
        in_specs=[row(ODD_IN), row(128), row(128), vec, vec], out_specs=[row(1024), row(256), row(256)],
        out_shape=[jax.ShapeDtypeStruct((S, w), BF) for w in (1024, 256, 256)],
        compiler_params=_params(("parallel",)),
    )(proj, cos, sin, gq, gk)


def _prep_odd_bwd(proj, cos, sin, gq, gk, dq, dk, dv):
    S = proj.shape[0]
    tr = _pick(S, (512,))

    def body(p_ref, cos_ref, sin_ref, gq_ref, gk_ref, dq_ref, dk_ref, dv_ref, dp_ref, dgq_ref, dgk_ref):
        cos_v, sin_v = cos_ref[...], sin_ref[...]
        dt = dp_ref.dtype
        dgq = jnp.zeros((1, HEAD_DIM), F32)
        for h in range(AX_HEADS):
            o = h * HEAD_DIM
            xh, r = _head_norm(_cols(p_ref, o))
            dy = _rope_t(_cols(dq_ref, o) * ATT_SCALE, cos_v, sin_v, 32)
            dgq = dgq + _row_sum(dy * xh)
            dp_ref[:, o:o + HEAD_DIM] = _head_norm_bwd(dy * gq_ref[...], xh, r).astype(dt)
        dgk = jnp.zeros((1, HEAD_DIM), F32)
        for h in range(AX_KV_HEADS):
            o = h * HEAD_DIM
            xh, r = _head_norm(_cols(p_ref, 1024 + o))
            dy = _rope_t(_cols(dk_ref, o), cos_v, sin_v, 32)
            dgk = dgk + _row_sum(dy * xh)
            dp_ref[:, 1024 + o:1024 + o + HEAD_DIM] = _head_norm_bwd(dy * gk_ref[...], xh, r).astype(dt)
        dp_ref[:, 1280:1536] = dv_ref[...].astype(dt)

        @pl.when(pl.program_id(0) == 0)
        def _():
            dgq_ref[...] = jnp.zeros_like(dgq_ref)
            dgk_ref[...] = jnp.zeros_like(dgk_ref)

        dgq_ref[...] += dgq
        dgk_ref[...] += dgk

    def row(w):
        return pl.BlockSpec((tr, w), lambda i: (i, 0))

    vec = pl.BlockSpec((1, HEAD_DIM), lambda i: (0, 0))
    return pl.pallas_call(
        body, name="prep_odd_bwd", grid=(S // tr,),
        in_specs=[row(ODD_IN), row(128), row(128), vec, vec, row(1024), row(256), row(256)],
        out_specs=[row(ODD_IN), vec, vec],
        out_shape=[jax.ShapeDtypeStruct((S, ODD_IN), BF), jax.ShapeDtypeStruct((1, HEAD_DIM), F32),
                   jax.ShapeDtypeStruct((1, HEAD_DIM), F32)],
        compiler_params=_params(("arbitrary",)),
    )(proj, cos, sin, gq, gk, dq, dk, dv)


def _loop_unrolled(n, factor, step, init):
    while n % factor:
        factor //= 2

    def trip(t, carry):
        for u in range(factor):
            carry = step(factor * t + u, carry)
        return carry

    return lax.fori_loop(0, n // factor, trip, init)


def _flash_fwd(q, k, v, carried=None):
    S = q.shape[0]
    tq = _pick(S, (512,))
    tk = _pick(S, (512,))
    nk = S // tk
    G = AX_HEADS // AX_KV_HEADS

    def body(q_ref, k_ref, v_ref, o_ref, lse_ref):
        qv = q_ref[...]

        def step(j, carry):
            m, l, acc = carry
            rows = pl.ds(pl.multiple_of(j * tk, tk), tk)
            s = _dot(qv, k_ref[rows, :], NT)
            m_new = jnp.maximum(m, jnp.max(s, axis=-1, keepdims=True))
            alpha = jnp.exp(m - m_new)
            p = jnp.exp(s - m_new)
            l = alpha * l + jnp.sum(p, axis=-1, keepdims=True)
            acc = alpha * acc + _dot(p.astype(v_ref.dtype), v_ref[rows, :])
            return m_new, l, acc

        init = (jnp.full((tq, 1), NEG_INF, F32), jnp.zeros((tq, 1), F32), jnp.zeros((tq, HEAD_DIM), F32))
        m, l, acc = _loop_unrolled(nk, 4, step, init)
        o_ref[...] = (acc / l).astype(o_ref.dtype)
        lse_ref[0] = jnp.broadcast_to(m + jnp.log(l), (tq, 128))

    (o, lse), landed = _carry_call(
        body, (q, k, v), name="flash_fwd", grid=(AX_HEADS, S // tq),
        in_specs=[pl.BlockSpec((tq, HEAD_DIM), lambda h, i: (i, h)),
                  pl.BlockSpec((S, HEAD_DIM), lambda h, i: (0, h // G)),
                  pl.BlockSpec((S, HEAD_DIM), lambda h, i: (0, h // G))],
        out_specs=[pl.BlockSpec((tq, HEAD_DIM), lambda h, i: (i, h)),
                   pl.BlockSpec((1, tq, 128), lambda h, i: (h, i, 0))],
        out_shape=[jax.ShapeDtypeStruct((S, AX_HEADS * HEAD_DIM), BF), jax.ShapeDtypeStruct((AX_HEADS, S, 128), F32)],
        semantics=("parallel", "parallel"), carried=carried)
    return o, lse, landed


def _flash_bwd(q, k, v, o, do, lse, carried=None):
    S = q.shape[0]
    tq = _pick(S, (512,))
    tk = _pick(S, (512,))
    nq, nk = S // tq, S // tk
    G = AX_HEADS // AX_KV_HEADS

    def body(q_ref, k_ref, v_ref, o_ref, do_ref, lse_ref, dq_ref, dk_ref, dv_ref):
        g, i = pl.program_id(1), pl.program_id(2)

        @pl.when((g == 0) & (i == 0))
        def _():
            dk_ref[...] = jnp.zeros_like(dk_ref)
            dv_ref[...] = jnp.zeros_like(dv_ref)

        qv = q_ref[...]
        do_f = do_ref[...].astype(F32)
        dob = do_f.astype(qv.dtype)
        delta = jnp.sum(do_f * o_ref[...].astype(F32), axis=-1, keepdims=True)
        lse_col = lse_ref[0][:, 0:1]

        def step(j, dq):
            rows = pl.ds(pl.multiple_of(j * tk, tk), tk)
            kj, vj = k_ref[rows, :], v_ref[rows, :]
            p = jnp.exp(_dot(qv, kj, NT) - lse_col)
            dp = _dot(dob, vj, NT)
            ds = (p * (dp - delta)).astype(qv.dtype)
            dk_ref[rows, :] += _dot(ds, qv, TN)
            dv_ref[rows, :] += _dot(p.astype(dob.dtype), dob, TN)
            return dq + _dot(ds, kj)

        dq_ref[...] = _loop_unrolled(nk, 2, step, jnp.zeros((tq, HEAD_DIM), F32))

    q_spec = pl.BlockSpec((tq, HEAD_DIM), lambda kh, g, i: (i, kh * G + g))
    kv_spec = pl.BlockSpec((S, HEAD_DIM), lambda kh, g, i: (0, kh))
    (dq, dk, dv), landed = _carry_call(
        body, (q, k, v, o, do, lse), name="flash_bwd", grid=(AX_KV_HEADS, G, nq),
        in_specs=[q_spec, kv_spec, kv_spec, q_spec, q_spec,
                  pl.BlockSpec((1, tq, 128), lambda kh, g, i: (kh * G + g, i, 0))],
        out_specs=[q_spec, kv_spec, kv_spec],
        out_shape=[jax.ShapeDtypeStruct((S, AX_HEADS * HEAD_DIM), F32), jax.ShapeDtypeStruct((S, 256), F32),
                   jax.ShapeDtypeStruct((S, 256), F32)],
        vmem=VMEM_LARGE, semantics=("arbitrary", "arbitrary", "arbitrary"), carried=carried)
    return dq, dk, dv, landed


def _rope_angles(pos, dim, theta):
    inv = theta ** (-jnp.arange(0, dim, 2, dtype=jnp.float32) / dim)
    return pos.astype(jnp.float32)[:, None] * inv[None, :]


def _rope_tables(S):
    ang = _rope_angles(jnp.arange(S), RET_DK, RET_THETA)
    c, s = jnp.cos(ang), jnp.sin(ang)
    ret = (jnp.concatenate([c, c], -1), jnp.concatenate([-s, s], -1))
    rows = S // GRID_W
    row = jnp.repeat(jnp.arange(rows), GRID_W)
    col = jnp.tile(jnp.arange(GRID_W), rows)
    ar, ac = _rope_angles(row, HEAD_DIM // 2, AX_THETA), _rope_angles(col, HEAD_DIM // 2, AX_THETA)
    cr, sr, cc, sc = jnp.cos(ar), jnp.sin(ar), jnp.cos(ac), jnp.sin(ac)
    ax = (jnp.concatenate([cr, cr, cc, cc], -1), jnp.concatenate([-sr, sr, -sc, sc], -1))
    return ret, ax


def _pad_tile(a):
    return jnp.pad(a.astype(F32), ((0, 8 - a.shape[0]), (0, 128 - a.shape[1])))


def _relu2_epilogue(acc):
    r = jnp.maximum(acc, 0.0)
    return acc, r * r


def _relu2_bwd_epilogue(acc, u):
    return (acc * (2.0 * jnp.maximum(u.astype(F32), 0.0)),)


def _add_epilogue(acc, res):
    return (acc + res,)


def _mlp_fwd(x, g, w_up, w_down, tag):
    h = _rms_fwd(x, g, f"mlp_norm_{tag}")
    u, a = _mm(h, w_up, name=f"mlp_up_{tag}", out_dtypes=(BF, BF), epilogue=_relu2_epilogue)
    y = _mm(a, w_down, name=f"mlp_down_{tag}", extras=(x,), epilogue=_add_epilogue)
    return y, (h, u, a)


def _mlp_bwd(x, g, w_up, w_down, saved, dy, dyb, tag):
    h, u, a = saved
    du = _mm(dyb, w_down, tb=True, name=f"mlp_down_dx_{tag}", out_dtypes=(BF,), extras=(u,), epilogue=_relu2_bwd_epilogue)
    dw_down = _mm(a, dyb, ta=True, name=f"mlp_down_dw_{tag}", out_dtypes=(BF,))
    dw_up = _mm(h, du, ta=True, name=f"mlp_up_dw_{tag}", out_dtypes=(BF,))
    dh = _mm(du, w_up, tb=True, name=f"mlp_up_dx_{tag}")
    dx, dxb, dg = _rms_bwd(x, g, dh, dy, f"mlp_norm_bwd_{tag}")
    return dx, dxb, dg, dw_up, dw_down


COL_SHARDED = ("in_even", "in_odd", "up0", "up1")


def _assemble(key, g):
    if key in COL_SHARDED:
        return g.transpose(1, 0, 2).reshape(g.shape[1], N_DEV * g.shape[2])
    return g.reshape(N_DEV * g.shape[1], g.shape[2])


def _split(key, full):
    rows, cols = full.shape
    if key in COL_SHARDED:
        return full.reshape(rows, N_DEV, cols // N_DEV).transpose(1, 0, 2)
    return full.reshape(N_DEV, rows // N_DEV, cols)


def _local_step(x, target, W, P, late=None):
    S = x.shape[0]
    W = dict(W)
    landed = {}

    def gather_while(stage):
        return None if late is None else _Exchange([s for _, s in late[stage]], gather=True)

    def arrived(stage, outs):
        for (key, _), g in zip([] if late is None else late[stage], outs):
            W[key] = _assemble(key, g)

    def exchange_while(grads):
        return None if late is None else _Exchange([_split(k, g) for k, g in grads], gather=False)

    def left(grads, outs):
        for (key, _), l in zip(grads, outs):
            landed[key] = l

    (cos_r, sin_r), (cos_a, sin_a) = _rope_tables(S)
    bucket = _t5_bucket_map()
    logit = P["ret_decay_logit"]
    lg = _log_sigmoid_tile(_pad_tile(logit))
    bias = _t5_bias(P["t5_table"], bucket)
    nmix, nmlp = P["norm_mix"], P["norm_mlp"]

    h0 = _rms_fwd(x, nmix[0:1], "mix_norm_0")
    proj_e = _mm(h0, W["in_even"], name="in_even")
    qr, kr, vr, qs, ks, vs = _prep_even_fwd(proj_e, cos_r, sin_r, P["swa_q_norm"], P["swa_k_norm"])
    sf, sb = _ret_scan(kr, vr, lg, (0, 0), (1, 0), "retention_states")
    y_ret = _ret_fwd(qr, kr, vr, lg, sf, sb)
    oa = _ret_post_fwd(y_ret, proj_e, P["ret_norm"])
    ob, outs = _swa_fwd(qs, ks, vs, bias, P["swa_sink"], gather_while("swa_fwd"))
    arrived("swa_fwd", outs)
    wo_a, wo_b = W["out_even"][:1024], W["out_even"][1024:]
    x1 = _mm(oa, wo_a, name="out_even_a", extras=(x,), epilogue=_add_epilogue)
    x1 = _mm(ob, wo_b, name="out_even_b", extras=(x1,), epilogue=_add_epilogue)
    x2, mlp0 = _mlp_fwd(x1, nmlp[0:1], W["up0"], W["down0"], "0")

    h2 = _rms_fwd(x2, nmix[1:2], "mix_norm_1")
    proj_o = _mm(h2, W["in_odd"], name="in_odd")
    qx, kx, vx = _prep_odd_fwd(proj_o, cos_a, sin_a, P["ax_q_norm"], P["ax_k_norm"])
    ox, lse, outs = _flash_fwd(qx, kx, vx, gather_while("flash_fwd"))
    arrived("flash_fwd", outs)
    x3 = _mm(ox, W["out_odd"], name="out_odd", extras=(x2,), epilogue=_add_epilogue)
    x4, mlp1 = _mlp_fwd(x3, nmlp[1:2], W["up1"], W["down1"], "1")

    d4, d4b, loss_tile = _loss_and_grad(x4, target)

    d3, d3b, dnmlp1, dw_up1, dw_down1 = _mlp_bwd(x3, nmlp[1:2], W["up1"], W["down1"], mlp1, d4, d4b, "1")
    dox = _mm(d3b, W["out_odd"], tb=True, name="out_odd_dx")
    dw_out_odd = _mm(ox, d3b, ta=True, name="out_odd_dw", out_dtypes=(BF,))
    grads1 = [("up1", dw_up1), ("down1", dw_down1), ("out_odd", dw_out_odd)]
    dqx, dkx, dvx, outs = _flash_bwd(qx, kx, vx, ox, dox, lse, exchange_while(grads1))
    left(grads1, outs)
    dproj_o, dgq_ax, dgk_ax = _prep_odd_bwd(proj_o, cos_a, sin_a, P["ax_q_norm"], P["ax_k_norm"], dqx, dkx, dvx)
    dw_in_odd = _mm(h2, dproj_o, ta=True, name="in_odd_dw", out_dtypes=(BF,))
    dh2 = _mm(dproj_o, W["in_odd"], tb=True, name="in_odd_dx")
    d2, d2b, dnmix1 = _rms_bwd(x2, nmix[1:2], dh2, d3, "mix_norm_bwd_1")

    d1, d1b, dnmlp0, dw_up0, dw_down0 = _mlp_bwd(x1, nmlp[0:1], W["up0"], W["down0"], mlp0, d2, d2b, "0")
    doa = _mm(d1b, wo_a, tb=True, name="out_even_a_dx")
    dob = _mm(d1b, wo_b, tb=True, name="out_even_b_dx")
    dw_out_even = jnp.concatenate([_mm(oa, d1b, ta=True, name="out_even_a_dw", out_dtypes=(BF,)),
                                   _mm(ob, d1b, ta=True, name="out_even_b_dw", out_dtypes=(BF,))], axis=0)
    dy_ret, dga, dret_norm = _ret_post_bwd(y_ret, proj_e, P["ret_norm"], doa)
    hb, hf = _ret_scan(qr, dy_ret, lg, (1, 1), (0, 1), "retention_state_grads")
    dqr, dkr, dvr, dlogit = _ret_bwd(qr, kr, vr, dy_ret, lg, logit, sf, sb, hf, hb)
    grads0 = [("in_odd", dw_in_odd), ("up0", dw_up0), ("down0", dw_down0), ("out_even", dw_out_even)]
    dqs, dks, dvs, dbias, dsink, outs = _swa_bwd(qs, ks, vs, dob, bias, P["swa_sink"], exchange_while(grads0))
    left(grads0, outs)
    dt5 = _t5_table_grad(dbias, bucket)
    dproj_e, dgq_swa, dgk_swa = _prep_even_bwd(proj_e, cos_r, sin_r, P["swa_q_norm"], P["swa_k_norm"],
                                               dqr, dkr, dvr, dga, dqs, dks, dvs)
    dw_in_even = _mm(h0, dproj_e, ta=True, name="in_even_dw", out_dtypes=(BF,))
    dh0 = _mm(dproj_e, W["in_even"], tb=True, name="in_even_dx")
    dx, _, dnmix0 = _rms_bwd(x, nmix[0:1], dh0, d1, "mix_norm_bwd_0")

    if late is None:
        dW = dict(grads1 + grads0, in_even=dw_in_even)
    else:
        dW = ({"in_even": dw_in_even}, landed)
    dP = {"norm_mix": jnp.concatenate([dnmix0, dnmix1], 0), "norm_mlp": jnp.concatenate([dnmlp0, dnmlp1], 0),
          "ret_decay_logit": dlogit[:, :, 0, 0].T, "ret_norm": dret_norm,
          "swa_q_norm": dgq_swa, "swa_k_norm": dgk_swa, "swa_sink": dsink[:, 0][None, :],
          "t5_table": dt5[:, :SWA_HEADS], "ax_q_norm": dgq_ax, "ax_k_norm": dgk_ax}
    return loss_tile, dx, dW, dP


def _cast_shards(shards):
    n = len(shards)

    def body(*refs):
        for i_ref, o_ref in zip(refs[:n], refs[n:]):
            o_ref[...] = i_ref[...].astype(o_ref.dtype)

    return pl.pallas_call(body, name="cast_shards", in_specs=[VMEM_SPEC] * n, out_specs=[VMEM_SPEC] * n,
                          out_shape=[jax.ShapeDtypeStruct(s.shape, BF) for s in shards],
                          compiler_params=pltpu.CompilerParams(vmem_limit_bytes=VMEM_SMALL))(*shards)


def _all_gather(shards):
    n = len(shards)

    def body(*refs):
        ins, outs, stage = refs[:n], refs[n:2 * n], refs[2 * n:3 * n]
        send_sems, recv_sems, local_sems = refs[3 * n:]
        me = _my_place()
        sibling = _flip(me, 1)
        chips = [_flip(me, 4), _flip(me, 2), _flip(me, 6)]

        def copy(a, k, block, to, src=None):
            dst = outs[a].at[_index(block)]
            return pltpu.make_async_remote_copy(
                src_ref=dst if src is None else src, dst_ref=dst,
                send_sem=send_sems.at[a, k], recv_sem=recv_sems.at[a, k], device_id=to, device_id_type=MESH)

        first, mine = [], []
        for a in range(n):
            stage[a][...] = ins[a][...].astype(stage[a].dtype)
            mine.append(pltpu.make_async_copy(stage[a], outs[a].at[_index(me)], local_sems.at[a]))
            mine[-1].start()
            first.append(copy(a, 0, me, sibling, src=stage[a]))
            first += [copy(a, 1 + j, me, chip, src=stage[a]) for j, chip in enumerate(chips)]
        for cp in first:
            cp.start()
        passed = []
        for a in range(n):
            for j, chip in enumerate(chips):
                copy(a, 1 + j, chip, me).wait_recv()
                passed.append(copy(a, 4 + j, chip, sibling))
                passed[-1].start()
        for a in range(n):
            copy(a, 0, sibling, me).wait_recv()
            for j, chip in enumerate(chips):
                copy(a, 4 + j, _flip(chip, 1), me).wait_recv()
        for cp in first + passed:
            cp.wait_send()
        for cp in mine:
            cp.wait()

    return pl.pallas_call(
        body, name="weights_all_gather",
        in_specs=[VMEM_SPEC] * n, out_specs=[ANY_SPEC] * n,
        out_shape=[jax.ShapeDtypeStruct((N_DEV,) + s.shape, BF) for s in shards],
        scratch_shapes=[pltpu.VMEM(s.shape, BF) for s in shards]
        + [pltpu.SemaphoreType.DMA((n, 7)), pltpu.SemaphoreType.DMA((n, 7)), pltpu.SemaphoreType.DMA((n,))],
        compiler_params=pltpu.CompilerParams(vmem_limit_bytes=VMEM_SMALL),
    )(*shards)


def _all_reduce_small(part):
    R, C = part.shape

    def body(x_ref, o_ref, land_ref, send_sems, recv_sems):
        me = _my_place()
        land_ref[_index(me)] = x_ref[...]
        copies = []
        for k in range(1, N_DEV):
            peer = _flip(me, k)
            copies.append(pltpu.make_async_remote_copy(
                src_ref=x_ref, dst_ref=land_ref.at[_index(me)],
                send_sem=send_sems.at[k - 1], recv_sem=recv_sems.at[k - 1], device_id=peer, device_id_type=MESH))
            copies[-1].start()
        for k in range(1, N_DEV):
            peer = _flip(me, k)
            pltpu.make_async_remote_copy(
                src_ref=x_ref, dst_ref=land_ref.at[_index(peer)],
                send_sem=send_sems.at[k - 1], recv_sem=recv_sems.at[k - 1], device_id=peer, device_id_type=MESH).wait_recv()
        for cp in copies:
            cp.wait_send()
        acc = land_ref[0]
        for s in range(1, N_DEV):
            acc = acc + land_ref[s]
        o_ref[...] = acc

    return pl.pallas_call(
        body, name="small_all_reduce", in_specs=[VMEM_SPEC], out_specs=VMEM_SPEC,
        out_shape=jax.ShapeDtypeStruct((R, C), F32),
        scratch_shapes=[pltpu.VMEM((N_DEV, R, C), F32), pltpu.SemaphoreType.DMA((7,)), pltpu.SemaphoreType.DMA((7,))],
    )(part)


def _adamw_math(w, g, m, v):
    m = ADAM_B1 * m + (1.0 - ADAM_B1) * g
    v = ADAM_B2 * v + (1.0 - ADAM_B2) * jnp.square(g)
    m_hat = m / (1.0 - ADAM_B1 ** ADAM_STEP)
    v_hat = v / (1.0 - ADAM_B2 ** ADAM_STEP)
    delta = -ADAM_LR * (m_hat / (jnp.sqrt(v_hat) + ADAM_EPS) + ADAM_WD * w)
    return delta, m, v


def _sum_and_adamw(landed, w, m, v, name):
    R, C = w.shape
    tr = _pick(R, (256, 128))

    def body(l_ref, w_ref, m_ref, v_ref, g_ref, d_ref, nm_ref, nv_ref):
        g = l_ref[0].astype(F32)
        for s in range(1, N_DEV):
            g = g + l_ref[s].astype(F32)
        g_ref[...] = g
        d_ref[...], nm_ref[...], nv_ref[...] = _adamw_math(w_ref[...], g, m_ref[...], v_ref[...])

    row = pl.BlockSpec((tr, C), lambda i: (i, 0))
    return pl.pallas_call(
        body, name=name, grid=(R // tr,),
        in_specs=[pl.BlockSpec((N_DEV, tr, C), lambda i: (0, i, 0)), row, row, row], out_specs=[row] * 4,
        out_shape=[jax.ShapeDtypeStruct((R, C), F32)] * 4, compiler_params=_params(("parallel",)),
    )(landed, w, m, v)


def _adamw_small(w, g, m, v):
    def body(w_ref, g_ref, m_ref, v_ref, d_ref, nm_ref, nv_ref):
        d_ref[...], nm_ref[...], nv_ref[...] = _adamw_math(w_ref[...], g_ref[...], m_ref[...], v_ref[...])

    full = pl.BlockSpec(w.shape, lambda: (0, 0))
    return pl.pallas_call(body, name="adamw_small", in_specs=[full] * 4, out_specs=[full] * 3,
                          out_shape=[jax.ShapeDtypeStruct(w.shape, F32)] * 3)(w, g, m, v)


MATRICES = ("w_in_even", "w_out_even", "w_in_odd", "w_out_odd", "w_mlp_up", "w_mlp_down")
SMALL = ("norm_mix", "norm_mlp", "ret_decay_logit", "ret_norm", "swa_q_norm", "swa_k_norm", "swa_sink",
         "t5_table", "ax_q_norm", "ax_k_norm")
MATRIX_OF = {"in_even": ("w_in_even", 0), "out_even": ("w_out_even", 0), "in_odd": ("w_in_odd", 0),
             "out_odd": ("w_out_odd", 0), "up0": ("w_mlp_up", 0), "up1": ("w_mlp_up", 1),
             "down0": ("w_mlp_down", 0), "down1": ("w_mlp_down", 1)}
GATHER_FIRST = ("in_even", "in_odd", "out_even")
GATHER_WHILE = {"swa_fwd": ("up0", "down0"), "flash_fwd": ("out_odd", "up1", "down1")}


SMALL_ROWS = 8
SMALL_AT = {"norm_mix": (0, 0), "norm_mlp": (2, 0), "ret_norm": (4, 0), "swa_q_norm": (5, 0), "swa_k_norm": (5, 128),
            "ax_q_norm": (5, 256), "ax_k_norm": (5, 384), "swa_sink": (5, 512), "ret_decay_logit": (5, 640),
            "t5_table": (6, 0)}
LOSS_AT = (5, 768)


def _pack_small(arrays, loss=None):
    buf = jnp.zeros((SMALL_ROWS, 1024), F32)
    for name, (r, c) in SMALL_AT.items():
        a = arrays[name].astype(F32)
        a = a.reshape(1, -1) if name in ("ret_decay_logit", "t5_table") else a.reshape(-1, a.shape[-1])
        buf = lax.dynamic_update_slice(buf, a, (r, c))
    if loss is not None:
        buf = lax.dynamic_update_slice(buf, loss.reshape(1, 1), LOSS_AT)
    return buf


def _unpack_small(buf, like):
    out = {}
    for name, (r, c) in SMALL_AT.items():
        shape = like[name].shape
        rows = 1 if name in ("ret_decay_logit", "t5_table") else math.prod(shape[:-1])
        cols = math.prod(shape) // rows
        out[name] = buf[r:r + rows, c:c + cols].reshape(shape)
    return out


def kernel(x, norm_mix, norm_mlp, w_in_even, w_out_even, ret_decay_logit, ret_norm, swa_q_norm, swa_k_norm, swa_sink, t5_table, w_in_odd, w_out_odd, ax_q_norm, ax_k_norm, w_mlp_up, w_mlp_down, loss_target, m_norm_mix, m_norm_mlp, m_w_in_even, m_w_out_even, m_ret_decay_logit, m_ret_norm, m_swa_q_norm, m_swa_k_norm, m_swa_sink, m_t5_table, m_w_in_odd, m_w_out_odd, m_ax_q_norm, m_ax_k_norm, m_w_mlp_up, m_w_mlp_down, v_norm_mix, v_norm_mlp, v_w_in_even, v_w_out_even, v_ret_decay_logit, v_ret_norm, v_swa_q_norm, v_swa_k_norm, v_swa_sink, v_t5_table, v_w_in_odd, v_w_out_odd, v_ax_q_norm, v_ax_k_norm, v_w_mlp_up, v_w_mlp_down):
    given = dict(locals())
    weights = {n: given[n] for n in MATRICES + SMALL}
    moments_m = {n: given["m_" + n] for n in MATRICES + SMALL}
    moments_v = {n: given["v_" + n] for n in MATRICES + SMALL}

    def shard(table, key):
        arg, layer = MATRIX_OF[key]
        return table[arg][layer]

    gathered = _all_gather([shard(weights, k) for k in GATHER_FIRST])
    W = {k: _assemble(k, g) for k, g in zip(GATHER_FIRST, gathered)}
    late_keys = [k for keys in GATHER_WHILE.values() for k in keys]
    cast = dict(zip(late_keys, _cast_shards([shard(weights, k) for k in late_keys])))
    late = {stage: [(k, cast[k]) for k in keys] for stage, keys in GATHER_WHILE.items()}
    P = {"norm_mix": norm_mix, "norm_mlp": norm_mlp, "ret_decay_logit": ret_decay_logit[0], "ret_norm": ret_norm,
         "swa_q_norm": swa_q_norm, "swa_k_norm": swa_k_norm, "swa_sink": swa_sink, "t5_table": t5_table,
         "ax_q_norm": ax_q_norm, "ax_k_norm": ax_k_norm}

    loss_tile, dx, (dW, landed), dP = _local_step(x[0], loss_target[0], W, P, late)

    landed["in_even"], = _exchange_blocks([_split("in_even", dW["in_even"])], False, "grads_exchange_last")
    per_key = {k: _sum_and_adamw(landed[k], shard(weights, k), shard(moments_m, k), shard(moments_v, k), "adamw_" + k)
               for k in MATRIX_OF}
    grads, deltas, new_m, new_v = {}, {}, {}, {}
    for i, out in enumerate((grads, deltas, new_m, new_v)):
        for n in MATRICES:
            out[n] = jnp.stack([per_key[k][i] for k, (arg, _) in MATRIX_OF.items() if arg == n])

    dP["ret_decay_logit"] = dP["ret_decay_logit"][None]
    total = _all_reduce_small(_pack_small(dP, loss_tile[0, 0]))
    loss = total[LOSS_AT[0], LOSS_AT[1]]
    small_d, small_m, small_v = _adamw_small(_pack_small(weights), total, _pack_small(moments_m), _pack_small(moments_v))
    like = {n: weights[n] for n in SMALL}
    for out, buf in ((grads, total), (deltas, small_d), (new_m, small_m), (new_v, small_v)):
        out.update(_unpack_small(buf, like))

    order = ("norm_mix", "norm_mlp", "w_in_even", "w_out_even", "ret_decay_logit", "ret_norm", "swa_q_norm", "swa_k_norm",
             "swa_sink", "t5_table", "w_in_odd", "w_out_odd", "ax_q_norm", "ax_k_norm", "w_mlp_up", "w_mlp_down")
    return (loss, dx[None], *[grads[n] for n in order], *[deltas[n] for n in order],
            *[new_m[n] for n in order], *[new_v[n] for n in order])
```

```python
import math

import jax
import jax.numpy as jnp
from jax import lax
from jax.experimental import pallas as pl
from jax.experimental.pallas import tpu as pltpu

F32 = jnp.float32
BF = jnp.bfloat16

D_MODEL = 1024
HEAD_DIM = 128
EPS = 1e-6
NEG_INF = -1e30
RET_HEADS, RET_DK, RET_DV = 4, 128, 256
RET_THETA = 10000.0
SWA_HEADS, SWA_KV_HEADS, WINDOW, BLOCK = 8, 2, 128, 128
T5_BUCKETS, T5_MAX_DIST = 32, 128
AX_HEADS, AX_KV_HEADS, AX_THETA, GRID_W = 8, 2, 10000.0, 64
D_FF = 4096
ATT_SCALE = HEAD_DIM ** -0.5
LN2 = math.log(2.0)
AX_SCALE = ATT_SCALE / LN2
RET_SCALE = RET_DK ** -0.5
N_DEV = 8

ADAM_LR, ADAM_B1, ADAM_B2, ADAM_EPS, ADAM_WD, ADAM_STEP = 0.001, 0.9, 0.999, 1e-08, 0.01, 10

MIB = 1024 * 1024
VMEM_SMALL = 40 * MIB
VMEM_LARGE = 56 * MIB

OFF_QA, OFF_KA, OFF_VA, OFF_GA, OFF_QB, OFF_KB, OFF_VB = 0, 512, 1024, 2048, 3072, 4096, 4352
EVEN_IN = 4608
ODD_IN = 1536

NT = (((1,), (1,)), ((), ()))
TN = (((0,), (0,)), ((), ()))
NN = (((1,), (0,)), ((), ()))


def _dot(a, b, dims=NN):
    return lax.dot_general(a, b, dims, preferred_element_type=F32)


def _params(sem=None, vmem=VMEM_SMALL):
    return pltpu.CompilerParams(dimension_semantics=sem, vmem_limit_bytes=vmem)


def _pick(n, prefs):
    for p in prefs:
        if n % p == 0:
            return p
    return n


def _row_sum(x):
    return jnp.sum(x, axis=0, keepdims=True)


def _all_sum(x):
    return jnp.sum(jnp.sum(x, axis=0, keepdims=True), axis=1, keepdims=True)


def _sigmoid(x):
    return 1.0 / (1.0 + jnp.exp(-x))


SMEM_SPEC = pl.BlockSpec(memory_space=pltpu.SMEM)
ANY_SPEC = pl.BlockSpec(memory_space=pl.ANY)
VMEM_SPEC = pl.BlockSpec(memory_space=pltpu.VMEM)
MESH = pl.DeviceIdType.MESH


def _my_place():
    return lax.axis_index("x"), lax.axis_index("y"), lax.axis_index("c")


def _flip(place, k):
    x, y, c = place
    return (1 - x if k & 4 else x, 1 - y if k & 2 else y, 1 - c if k & 1 else c)


def _index(place):
    x, y, c = place
    return 4 * x + 2 * y + c


class _Exchange:
    def __init__(self, sources, gather):
        self.sources, self.gather, self.n = list(sources), gather, len(sources)
        self.out_shape = [jax.ShapeDtypeStruct(((N_DEV,) + s.shape) if gather else s.shape, s.dtype) for s in self.sources]
        self.scratch = [pltpu.SemaphoreType.DMA((self.n, 7)), pltpu.SemaphoreType.DMA((self.n, 7)),
                        pltpu.SemaphoreType.DMA((self.n,))]

    def _source(self, ins, a, place):
        return ins[a] if self.gather else ins[a].at[_index(place)]

    def _local(self, ins, outs, sems):
        me = _my_place()
        return [pltpu.make_async_copy(self._source(ins, a, me), outs[a].at[_index(me)], sems[2].at[a]) for a in range(self.n)]

    def _remote(self, ins, outs, sems, arriving):
        send_sems, recv_sems, _ = sems
        me = _my_place()
        copies = []
        for a in range(self.n):
            for k in range(1, N_DEV):
                peer = _flip(me, k)
                copies.append(pltpu.make_async_remote_copy(
                    src_ref=self._source(ins, a, peer), dst_ref=outs[a].at[_index(peer if arriving else me)],
                    send_sem=send_sems.at[a, k - 1], recv_sem=recv_sems.at[a, k - 1], device_id=peer, device_id_type=MESH))
        return copies

    def start(self, ins, outs, sems):
        for cp in self._local(ins, outs, sems) + self._remote(ins, outs, sems, arriving=False):
            cp.start()

    def wait(self, ins, outs, sems):
        for cp in self._remote(ins, outs, sems, arriving=True):
            cp.wait_recv()
        for cp in self._remote(ins, outs, sems, arriving=False):
            cp.wait_send()
        for cp in self._local(ins, outs, sems):
            cp.wait()


def _carry_call(body, args, *, name, grid, in_specs, out_specs, out_shape, scratch_shapes=(), vmem=VMEM_SMALL,
                semantics=None, carried=None):
    if carried is None:
        outs = pl.pallas_call(body, name=name, grid=grid, in_specs=in_specs, out_specs=out_specs, out_shape=out_shape,
                              scratch_shapes=list(scratch_shapes), compiler_params=_params(semantics, vmem))(*args)
        return list(outs), []
    ni, no, ns, nc = len(in_specs), len(out_specs), len(scratch_shapes), carried.n

    def full_body(*refs):
        ins, cin = refs[:ni], refs[ni:ni + nc]
        outs, cout = refs[ni + nc:ni + nc + no], refs[ni + nc + no:ni + 2 * nc + no]
        scratch, sems = refs[ni + 2 * nc + no:ni + 2 * nc + no + ns], refs[ni + 2 * nc + no + ns:]
        ids = [pl.program_id(d) for d in range(len(grid))]
        first, last = ids[0] == 0, ids[0] == grid[0] - 1
        for d in range(1, len(grid)):
            first, last = first & (ids[d] == 0), last & (ids[d] == grid[d] - 1)

        @pl.when(first)
        def _():
            carried.start(cin, cout, sems)

        body(*ins, *outs, *scratch)

        @pl.when(last)
        def _():
            carried.wait(cin, cout, sems)

    outs = pl.pallas_call(
        full_body, name=name, grid=grid, in_specs=list(in_specs) + [ANY_SPEC] * nc,
        out_specs=list(out_specs) + [ANY_SPEC] * nc, out_shape=list(out_shape) + carried.out_shape,
        scratch_shapes=list(scratch_shapes) + carried.scratch,
        compiler_params=_params(("arbitrary",) * len(grid), vmem))(*args, *carried.sources)
    return list(outs[:no]), list(outs[no:])


def _exchange_blocks(sources, gather, name):
    ex = _Exchange(sources, gather)

    def body(*refs):
        ins, outs, sems = refs[:ex.n], refs[ex.n:2 * ex.n], refs[2 * ex.n:]
        ex.start(ins, outs, sems)
        ex.wait(ins, outs, sems)

    return pl.pallas_call(body, name=name, in_specs=[ANY_SPEC] * ex.n, out_specs=[ANY_SPEC] * ex.n,
                          out_shape=ex.out_shape, scratch_shapes=ex.scratch)(*ex.sources)


def _mm(a, b, *, name, ta=False, tb=False, out_dtypes=(F32,), extras=(), epilogue=None,
        tm=1024, tn=512, tk=1024):
    M, K = (a.shape[1], a.shape[0]) if ta else a.shape
    N = b.shape[0] if tb else b.shape[1]
    assert K == (b.shape[1] if tb else b.shape[0])
    tm = _pick(M, (tm, 512, 256, 128))
    tn = _pick(N, (tn, 512, 384, 256, 128))
    tk = _pick(K, (tk, 1536, 512, 256, 128))
    nk = K // tk
    ne, no = len(extras), len(out_dtypes)
    dims = (((0 if ta else 1,), (1 if tb else 0,)), ((), ()))
    if epilogue is None:
        epilogue = lambda acc: (acc,)

    def body(a_ref, b_ref, *rest):
        extra_refs, out_refs = rest[:ne], rest[ne:ne + no]

        def finish(acc):
            outs = epilogue(acc, *[r[...] for r in extra_refs])
            for o_ref, o in zip(out_refs, outs):
                o_ref[...] = o.astype(o_ref.dtype)

        part = _dot(a_ref[...], b_ref[...], dims)
        if nk == 1:
            finish(part)
        else:
            acc_ref = rest[-1]
            k = pl.program_id(2)

            @pl.when(k == 0)
            def _():
                acc_ref[...] = part

            @pl.when(k > 0)
            def _():
                acc_ref[...] += part

            @pl.when(k == nk - 1)
            def _():
                finish(acc_ref[...])

    a_spec = pl.BlockSpec((tk, tm), lambda i, j, k: (k, i)) if ta else pl.BlockSpec((tm, tk), lambda i, j, k: (i, k))
    b_spec = pl.BlockSpec((tn, tk), lambda i, j, k: (j, k)) if tb else pl.BlockSpec((tk, tn), lambda i, j, k: (k, j))
    o_spec = pl.BlockSpec((tm, tn), lambda i, j, k: (i, j))
    outs = pl.pallas_call(
        body, name=name, grid=(M // tm, N // tn, nk),
        in_specs=[a_spec, b_spec] + [o_spec] * ne,
        out_specs=[o_spec] * no,
        out_shape=[jax.ShapeDtypeStruct((M, N), dt) for dt in out_dtypes],
        scratch_shapes=[pltpu.VMEM((tm, tn), F32)] if nk > 1 else [],
        compiler_params=_params(("parallel", "parallel", "arbitrary")),
    )(a, b, *extras)
    return outs[0] if no == 1 else outs


def _rms_fwd(x, g, name):
    S, Dm = x.shape
    tr = _pick(S, (512,))

    def body(x_ref, g_ref, o_ref):
        xv = x_ref[...]
        r = lax.rsqrt(jnp.mean(xv * xv, axis=-1, keepdims=True) + EPS)
        o_ref[...] = (xv * r * g_ref[...]).astype(o_ref.dtype)

    row = pl.BlockSpec((tr, Dm), lambda i: (i, 0))
    return pl.pallas_call(
        body, name=name, grid=(S // tr,),
        in_specs=[row, pl.BlockSpec((1, Dm), lambda i: (0, 0))], out_specs=row,
        out_shape=jax.ShapeDtypeStruct((S, Dm), BF), compiler_params=_params(("parallel",)),
    )(x, g)


def _rms_bwd(x, g, dh, dres, name):
    S, Dm = x.shape
    tr = _pick(S, (512,))

    def body(x_ref, g_ref, dh_ref, dres_ref, dx_ref, dxb_ref, dg_ref):
        xv = x_ref[...]
        r = lax.rsqrt(jnp.mean(xv * xv, axis=-1, keepdims=True) + EPS)
        xh = xv * r
        dy = dh_ref[...].astype(F32)
        dxh = dy * g_ref[...]
        dx = r * (dxh - xh * jnp.mean(dxh * xh, axis=-1, keepdims=True)) + dres_ref[...]
        dx_ref[...] = dx
        dxb_ref[...] = dx.astype(dxb_ref.dtype)

        @pl.when(pl.program_id(0) == 0)
        def _():
            dg_ref[...] = jnp.zeros_like(dg_ref)

        dg_ref[...] += _row_sum(dy * xh)

    row = pl.BlockSpec((tr, Dm), lambda i: (i, 0))
    vec = pl.BlockSpec((1, Dm), lambda i: (0, 0))
    return pl.pallas_call(
        body, name=name, grid=(S // tr,),
        in_specs=[row, vec, row, row], out_specs=[row, row, vec],
        out_shape=[jax.ShapeDtypeStruct((S, Dm), F32), jax.ShapeDtypeStruct((S, Dm), BF),
                   jax.ShapeDtypeStruct((1, Dm), F32)],
        compiler_params=_params(("arbitrary",)),
    )(x, g, dh, dres)


def _loss_and_grad(y, target):
    S, Dm = y.shape
    tr = _pick(S, (512,))
    n = S // tr

    def body(y_ref, t_ref, dy_ref, dyb_ref, loss_ref, acc_ref):
        i = pl.program_id(0)
        e = y_ref[...] - t_ref[...]
        d = e * (1.0 / Dm)
        dy_ref[...] = d
        dyb_ref[...] = d.astype(dyb_ref.dtype)

        @pl.when(i == 0)
        def _():
            acc_ref[...] = jnp.zeros_like(acc_ref)

        acc_ref[...] += _row_sum(e * e)

        @pl.when(i == n - 1)
        def _():
            loss_ref[...] = jnp.broadcast_to(_all_sum(acc_ref[...]) * (0.5 / Dm), loss_ref.shape)

    row = pl.BlockSpec((tr, Dm), lambda i: (i, 0))
    return pl.pallas_call(
        body, name="loss_head", grid=(n,),
        in_specs=[row, row], out_specs=[row, row, pl.BlockSpec((8, 128), lambda i: (0, 0))],
        out_shape=[jax.ShapeDtypeStruct((S, Dm), F32), jax.ShapeDtypeStruct((S, Dm), BF),
                   jax.ShapeDtypeStruct((8, 128), F32)],
        scratch_shapes=[pltpu.VMEM((1, Dm), F32)],
        compiler_params=_params(("arbitrary",)),
    )(y, target)


def _partner(x, half):
    if half == 64:
        return pltpu.roll(x, 64, 1)
    lane = lax.broadcasted_iota(jnp.int32, x.shape, 1)
    return jnp.where((lane % (2 * half)) < half, pltpu.roll(x, 128 - half, 1), pltpu.roll(x, half, 1))


def _rope(x, cos, sin, half):
    return x * cos + _partner(x, half) * sin


def _rope_t(dy, cos, sin, half):
    return dy * cos - _partner(dy, half) * sin


def _head_norm(x):
    r = lax.rsqrt(jnp.mean(x * x, axis=-1, keepdims=True) + EPS)
    return x * r, r


def _head_norm_bwd(dxh, xh, r):
    return r * (dxh - xh * jnp.mean(dxh * xh, axis=-1, keepdims=True))


def _cols(ref, off, width=HEAD_DIM):
    return ref[:, off:off + width]


def _prep_even_fwd(proj, cos, sin, gq, gk):
    S = proj.shape[0]
    tr = _pick(S, (256,))

    def body(p_ref, cos_ref, sin_ref, gq_ref, gk_ref, qr_ref, kr_ref, vr_ref, qs_ref, ks_ref, vs_ref):
        cos_v, sin_v = cos_ref[...], sin_ref[...]
        for h in range(RET_HEADS):
            o = h * RET_DK
            qr_ref[:, o:o + RET_DK] = _rope(_cols(p_ref, OFF_QA + o), cos_v, sin_v, 64).astype(qr_ref.dtype)
            kr_ref[:, o:o + RET_DK] = (_rope(_cols(p_ref, OFF_KA + o), cos_v, sin_v, 64) * RET_SCALE).astype(kr_ref.dtype)
        vr_ref[...] = p_ref[:, OFF_VA:OFF_VA + 1024].astype(vr_ref.dtype)
        for h in range(SWA_HEADS):
            o = h * HEAD_DIM
            xh, _ = _head_norm(_cols(p_ref, OFF_QB + o))
            qs_ref[:, o:o + HEAD_DIM] = (xh * gq_ref[...] * ATT_SCALE).astype(qs_ref.dtype)
        for h in range(SWA_KV_HEADS):
            o = h * HEAD_DIM
            xh, _ = _head_norm(_cols(p_ref, OFF_KB + o))
            ks_ref[:, o:o + HEAD_DIM] = (xh * gk_ref[...]).astype(ks_ref.dtype)
        vs_ref[...] = p_ref[:, OFF_VB:OFF_VB + 256].astype(vs_ref.dtype)

    def row(w):
        return pl.BlockSpec((tr, w), lambda i: (i, 0))

    vec = pl.BlockSpec((1, HEAD_DIM), lambda i: (0, 0))
    widths = (512, 512, 1024, 1024, 256, 256)
    return pl.pallas_call(
        body, name="prep_even_fwd", grid=(S // tr,),
        in_specs=[row(EVEN_IN), row(128), row(128), vec, vec],
        out_specs=[row(w) for w in widths],
        out_shape=[jax.ShapeDtypeStruct((S, w), BF) for w in widths],
        compiler_params=_params(("parallel",)),
    )(proj, cos, sin, gq, gk)


def _prep_even_bwd(proj, cos, sin, gq, gk, dqr, dkr, dvr, dga, dqs, dks, dvs):
    S = proj.shape[0]
    tr = _pick(S, (256,))

    def body(p_ref, cos_ref, sin_ref, gq_ref, gk_ref, dqr_ref, dkr_ref, dvr_ref, dga_ref, dqs_ref, dks_ref,
             dvs_ref, dp_ref, dgq_ref, dgk_ref):
        cos_v, sin_v = cos_ref[...], sin_ref[...]
        dt = dp_ref.dtype
        for h in range(RET_HEADS):
            o = h * RET_DK
            dp_ref[:, OFF_QA + o:OFF_QA + o + RET_DK] = _rope_t(_cols(dqr_ref, o).astype(F32), cos_v, sin_v, 64).astype(dt)
            dp_ref[:, OFF_KA + o:OFF_KA + o + RET_DK] = _rope_t(_cols(dkr_ref, o).astype(F32) * RET_SCALE, cos_v, sin_v, 64).astype(dt)
        dp_ref[:, OFF_VA:OFF_VA + 1024] = dvr_ref[...].astype(dt)
        dp_ref[:, OFF_GA:OFF_GA + 1024] = dga_ref[...].astype(dt)
        dgq = jnp.zeros((1, HEAD_DIM), F32)
        for h in range(SWA_HEADS):
            o = h * HEAD_DIM
            xh, r = _head_norm(_cols(p_ref, OFF_QB + o))
            dy = _cols(dqs_ref, o).astype(F32) * ATT_SCALE
            dgq = dgq + _row_sum(dy * xh)
            dp_ref[:, OFF_QB + o:OFF_QB + o + HEAD_DIM] = _head_norm_bwd(dy * gq_ref[...], xh, r).astype(dt)
        dgk = jnp.zeros((1, HEAD_DIM), F32)
        for h in range(SWA_KV_HEADS):
            o = h * HEAD_DIM
            xh, r = _head_norm(_cols(p_ref, OFF_KB + o))
            dy = _cols(dks_ref, o)
            dgk = dgk + _row_sum(dy * xh)
            dp_ref[:, OFF_KB + o:OFF_KB + o + HEAD_DIM] = _head_norm_bwd(dy * gk_ref[...], xh, r).astype(dt)
        dp_ref[:, OFF_VB:OFF_VB + 256] = dvs_ref[...].astype(dt)

        @pl.when(pl.program_id(0) == 0)
        def _():
            dgq_ref[...] = jnp.zeros_like(dgq_ref)
            dgk_ref[...] = jnp.zeros_like(dgk_ref)

        dgq_ref[...] += dgq
        dgk_ref[...] += dgk

    def row(w):
        return pl.BlockSpec((tr, w), lambda i: (i, 0))

    vec = pl.BlockSpec((1, HEAD_DIM), lambda i: (0, 0))
    return pl.pallas_call(
        body, name="prep_even_bwd", grid=(S // tr,),
        in_specs=[row(EVEN_IN), row(128), row(128), vec, vec, row(512), row(512), row(1024), row(1024),
                  row(1024), row(256), row(256)],
        out_specs=[row(EVEN_IN), vec, vec],
        out_shape=[jax.ShapeDtypeStruct((S, EVEN_IN), BF), jax.ShapeDtypeStruct((1, HEAD_DIM), F32),
                   jax.ShapeDtypeStruct((1, HEAD_DIM), F32)],
        compiler_params=_params(("arbitrary",)),
    )(proj, cos, sin, gq, gk, dqr, dkr, dvr, dga, dqs, dks, dvs)


RET_CHUNK = 512
def _log_sigmoid_tile(logit_tile):
    def body(x_ref, o_ref):
        xv = x_ref[...]
        t = jnp.exp(-jnp.abs(xv))
        log1p_t = jnp.where(t < 1e-3, t * (1.0 - 0.5 * t), jnp.log(1.0 + t))
        o_ref[...] = jnp.minimum(xv, 0.0) - log1p_t

    full = pl.BlockSpec((8, 128), lambda: (0, 0))
    return pl.pallas_call(body, name="log_sigmoid", in_specs=[full], out_specs=full,
                          out_shape=jax.ShapeDtypeStruct((8, 128), F32))(logit_tile)


def _decay(diff, lf, lb):
    return jnp.exp(jnp.where(diff >= 0, lf * diff, -(lb * diff)))


def _col_iota(n):
    return lax.broadcasted_iota(jnp.int32, (n, 1), 0).astype(F32)


def _ret_scan(x, z, lg, asc, desc, name):
    S = x.shape[0]
    C = _pick(S, (RET_CHUNK,))
    nc = S // C
    (arow, aoff), (drow, doff) = asc, desc

    def body(lg_ref, xa_ref, za_ref, xd_ref, zd_ref, asc_ref, desc_ref, sa_ref, sd_ref):
        h, t = pl.program_id(0), pl.program_id(1)
        la, ld = lg_ref[arow, h], lg_ref[drow, h]

        @pl.when(t == 0)
        def _():
            sa_ref[...] = jnp.zeros_like(sa_ref)
            sd_ref[...] = jnp.zeros_like(sd_ref)

        asc_ref[0, 0] = sa_ref[...]
        desc_ref[0, 0] = sd_ref[...]
        j = _col_iota(C)
        xa = (xa_ref[...].astype(F32) * jnp.exp(la * (C - 1 + aoff - j))).astype(xa_ref.dtype)
        xd = (xd_ref[...].astype(F32) * jnp.exp(ld * (j + doff))).astype(xd_ref.dtype)
        sa_ref[...] = jnp.exp(jnp.full((1, RET_DV), la * C, F32)) * sa_ref[...] + _dot(xa, za_ref[...], TN)
        sd_ref[...] = jnp.exp(jnp.full((1, RET_DV), ld * C, F32)) * sd_ref[...] + _dot(xd, zd_ref[...], TN)

    state = jax.ShapeDtypeStruct((RET_HEADS, nc, RET_DK, RET_DV), F32)
    return pl.pallas_call(
        body, name=name, grid=(RET_HEADS, nc),
        in_specs=[SMEM_SPEC,
                  pl.BlockSpec((C, RET_DK), lambda h, t: (t, h)), pl.BlockSpec((C, RET_DV), lambda h, t: (t, h)),
                  pl.BlockSpec((C, RET_DK), lambda h, t: (nc - 1 - t, h)), pl.BlockSpec((C, RET_DV), lambda h, t: (nc - 1 - t, h))],
        out_specs=[pl.BlockSpec((1, 1, RET_DK, RET_DV), lambda h, t: (h, t, 0, 0)),
                   pl.BlockSpec((1, 1, RET_DK, RET_DV), lambda h, t: (h, nc - 1 - t, 0, 0))],
        out_shape=[state, state],
        scratch_shapes=[pltpu.VMEM((RET_DK, RET_DV), F32), pltpu.VMEM((RET_DK, RET_DV), F32)],
        compiler_params=_params(("parallel", "arbitrary")),
    )(lg, x, z, x, z)


def _ret_fwd(q, k, v, lg, sf, sb):
    S = q.shape[0]
    C = _pick(S, (RET_CHUNK,))

    def body(lg_ref, q_ref, k_ref, v_ref, sf_ref, sb_ref, y_ref):
        h = pl.program_id(0)
        lf, lb = lg_ref[0, h], lg_ref[1, h]
        qv = q_ref[...]
        dt = qv.dtype
        diff = (lax.broadcasted_iota(jnp.int32, (C, C), 0) - lax.broadcasted_iota(jnp.int32, (C, C), 1)).astype(F32)
        y = _dot((_dot(qv, k_ref[...], NT) * _decay(diff, lf, lb)).astype(dt), v_ref[...])
        r = _col_iota(C)
        qf = qv.astype(F32)
        y = y + _dot((qf * jnp.exp(lf * (r + 1.0))).astype(dt), sf_ref[0, 0].astype(dt))
        y_ref[...] = y + _dot((qf * jnp.exp(lb * (C - r))).astype(dt), sb_ref[0, 0].astype(dt))

    state = pl.BlockSpec((1, 1, RET_DK, RET_DV), lambda h, c: (h, c, 0, 0))
    qk = pl.BlockSpec((C, RET_DK), lambda h, c: (c, h))
    vy = pl.BlockSpec((C, RET_DV), lambda h, c: (c, h))
    return pl.pallas_call(
        body, name="retention_fwd", grid=(RET_HEADS, S // C),
        in_specs=[SMEM_SPEC, qk, qk, vy, state, state], out_specs=vy,
        out_shape=jax.ShapeDtypeStruct((S, RET_HEADS * RET_DV), F32),
        compiler_params=_params(("parallel", "parallel")),
    )(lg, q, k, v, sf, sb)


def _ret_bwd(q, k, v, dy, lg, logit, sf, sb, hf, hb):
    S = q.shape[0]
    C = _pick(S, (RET_CHUNK,))
    nc = S // C

    def body(lg_ref, logit_ref, q_ref, k_ref, v_ref, dy_ref, sf_ref, sb_ref, hf_ref, hb_ref,
             dq_ref, dk_ref, dv_ref, dlg_ref, acc_ref):
        h, c = pl.program_id(0), pl.program_id(1)
        lf, lb = lg_ref[0, h], lg_ref[1, h]

        @pl.when(c == 0)
        def _():
            acc_ref[...] = jnp.zeros_like(acc_ref)

        qv, kv, vv, dyv = q_ref[...], k_ref[...], v_ref[...], dy_ref[...]
        dt = qv.dtype
        qf, kf = qv.astype(F32), kv.astype(F32)
        diff = (lax.broadcasted_iota(jnp.int32, (C, C), 0) - lax.broadcasted_iota(jnp.int32, (C, C), 1)).astype(F32)
        dec = _decay(diff, lf, lb)
        sc = _dot(qv, kv, NT) * dec
        dp = _dot(dyv, vv, NT)
        da = (dp * dec).astype(dt)
        dq = _dot(da, kv)
        dk = _dot(da, qv, TN)
        dv = _dot(sc.astype(dt), dyv, TN)
        w = sc * dp * diff
        tot_w, tot_f = _all_sum(w), _all_sum(jnp.where(diff >= 0, w, 0.0))
        d_lf, d_lb = tot_f, tot_f - tot_w
        r = _col_iota(C)
        a, b = jnp.exp(lf * (r + 1.0)), jnp.exp(lb * (C - r))
        e, f = jnp.exp(lf * (C - 1.0 - r)), jnp.exp(lb * r)
        sfv, sbv, hfv, hbv = sf_ref[0, 0], sb_ref[0, 0], hf_ref[0, 0], hb_ref[0, 0]
        t_f, t_b = _dot(dyv, sfv.astype(dt), NT), _dot(dyv, sbv.astype(dt), NT)
        u_f, u_b = _dot(vv, hfv.astype(dt), NT), _dot(vv, hbv.astype(dt), NT)
        dq_ref[...] = (dq + a * t_f + b * t_b).astype(dq_ref.dtype)
        dk_ref[...] = (dk + e * u_f + f * u_b).astype(dk_ref.dtype)
        dv_ref[...] = (dv + _dot((kf * e).astype(dt), hfv.astype(dt)) + _dot((kf * f).astype(dt), hbv.astype(dt))).astype(dv_ref.dtype)
        row_q_f = jnp.sum(qf * t_f, axis=-1, keepdims=True)
        row_q_b = jnp.sum(qf * t_b, axis=-1, keepdims=True)
        row_k_f = jnp.sum(kf * u_f, axis=-1, keepdims=True)
        row_k_b = jnp.sum(kf * u_b, axis=-1, keepdims=True)
        gf_c = jnp.exp(jnp.full((1, 1), lf * C, F32))
        gb_c = jnp.exp(jnp.full((1, 1), lb * C, F32))
        d_lf = d_lf + _all_sum((r + 1.0) * a * row_q_f + (C - 1.0 - r) * e * row_k_f) + C * gf_c * _all_sum(hfv * sfv)
        d_lb = d_lb + _all_sum((C - r) * b * row_q_b + r * f * row_k_b) + C * gb_c * _all_sum(hbv * sbv)
        acc_ref[0] += jnp.broadcast_to(d_lf, (8, 128))
        acc_ref[1] += jnp.broadcast_to(d_lb, (8, 128))

        @pl.when(c == nc - 1)
        def _():
            for d in range(2):
                gate = 1.0 / (1.0 + jnp.exp(jnp.full((8, 128), logit_ref[d, h], F32)))
                dlg_ref[0, d] = acc_ref[d] * gate

    state = pl.BlockSpec((1, 1, RET_DK, RET_DV), lambda h, c: (h, c, 0, 0))
    qk = pl.BlockSpec((C, RET_DK), lambda h, c: (c, h))
    vy = pl.BlockSpec((C, RET_DV), lambda h, c: (c, h))
    return pl.pallas_call(
        body, name="retention_bwd", grid=(RET_HEADS, nc),
        in_specs=[SMEM_SPEC, SMEM_SPEC, qk, qk, vy, vy, state, state, state, state],
        out_specs=[qk, qk, vy, pl.BlockSpec((1, 2, 8, 128), lambda h, c: (h, 0, 0, 0))],
        out_shape=[jax.ShapeDtypeStruct((S, RET_HEADS * RET_DK), BF), jax.ShapeDtypeStruct((S, RET_HEADS * RET_DK), BF),
                   jax.ShapeDtypeStruct((S, RET_HEADS * RET_DV), BF),
                   jax.ShapeDtypeStruct((RET_HEADS, 2, 8, 128), F32)],
        scratch_shapes=[pltpu.VMEM((2, 8, 128), F32)],
        compiler_params=_params(("parallel", "arbitrary")),
    )(lg, logit, q, k, v, dy, sf, sb, hf, hb)


def _ret_post_fwd(y, proj, gn):
    S = y.shape[0]
    tr = _pick(S, (512,))

    def body(y_ref, g_ref, gn_ref, o_ref):
        for h in range(RET_HEADS):
            o = h * RET_DV
            yh, _ = _head_norm(_cols(y_ref, o, RET_DV))
            gate = _cols(g_ref, o, RET_DV)
            o_ref[:, o:o + RET_DV] = (gate * _sigmoid(gate) * (yh * gn_ref[:, o:o + RET_DV])).astype(o_ref.dtype)

    row = pl.BlockSpec((tr, 1024), lambda i: (i, 0))
    return pl.pallas_call(
        body, name="retention_post_fwd", grid=(S // tr,),
        in_specs=[row, pl.BlockSpec((tr, 1024), lambda i: (i, OFF_GA // 1024)), pl.BlockSpec((1, 1024), lambda i: (0, 0))],
        out_specs=row, out_shape=jax.ShapeDtypeStruct((S, 1024), BF), compiler_params=_params(("parallel",)),
    )(y, proj, gn)


def _ret_post_bwd(y, proj, gn, do):
    S = y.shape[0]
    tr = _pick(S, (512,))

    def body(y_ref, g_ref, gn_ref, do_ref, dy_ref, dg_ref, dgn_ref):
        @pl.when(pl.program_id(0) == 0)
        def _():
            dgn_ref[...] = jnp.zeros_like(dgn_ref)

        for h in range(RET_HEADS):
            o = h * RET_DV
            yh, r = _head_norm(_cols(y_ref, o, RET_DV))
            gate = _cols(g_ref, o, RET_DV)
            gnh = gn_ref[:, o:o + RET_DV]
            dout = _cols(do_ref, o, RET_DV).astype(F32)
            sg = _sigmoid(gate)
            dz = dout * (gate * sg)
            dg_ref[:, o:o + RET_DV] = (dout * (yh * gnh) * (sg * (1.0 + gate * (1.0 - sg)))).astype(dg_ref.dtype)
            dgn_ref[:, o:o + RET_DV] += _row_sum(dz * yh)
            dy_ref[:, o:o + RET_DV] = _head_norm_bwd(dz * gnh, yh, r).astype(dy_ref.dtype)

    row = pl.BlockSpec((tr, 1024), lambda i: (i, 0))
    vec = pl.BlockSpec((1, 1024), lambda i: (0, 0))
    return pl.pallas_call(
        body, name="retention_post_bwd", grid=(S // tr,),
        in_specs=[row, pl.BlockSpec((tr, 1024), lambda i: (i, OFF_GA // 1024)), vec, row],
        out_specs=[row, row, vec],
        out_shape=[jax.ShapeDtypeStruct((S, 1024), BF), jax.ShapeDtypeStruct((S, 1024), BF),
                   jax.ShapeDtypeStruct((1, 1024), F32)],
        compiler_params=_params(("arbitrary",)),
    )(y, proj, gn, do)


def _t5_bucket_map():
    r = jnp.arange(BLOCK)
    j = jnp.arange(3 * BLOCK)
    rel = j[None, :] - BLOCK - r[:, None]
    nb = T5_BUCKETS // 2
    max_exact = nb // 2
    ret = jnp.where(rel > 0, nb, 0)
    n = jnp.abs(rel)
    nf = jnp.maximum(n, 1).astype(jnp.float32)
    large = max_exact + (jnp.log(nf / max_exact) / math.log(T5_MAX_DIST / max_exact)
                         * (nb - max_exact)).astype(jnp.int32)
    large = jnp.minimum(large, nb - 1)
    bucket = ret + jnp.where(n < max_exact, n, large)
    return jnp.where(jnp.abs(rel) <= WINDOW, bucket, -1).astype(jnp.int32)


def _t5_bias(table, bucket):
    def body(t_ref, b_ref, o_ref):
        bk = b_ref[...]
        for h in range(SWA_HEADS):
            acc = jnp.full(bk.shape, NEG_INF, F32)
            for b in range(T5_BUCKETS):
                acc = jnp.where(bk == b, t_ref[b, h], acc)
            o_ref[h] = acc

    return pl.pallas_call(
        body, name="t5_bias", in_specs=[SMEM_SPEC, pl.BlockSpec((BLOCK, 3 * BLOCK), lambda: (0, 0))],
        out_specs=pl.BlockSpec((SWA_HEADS, BLOCK, 3 * BLOCK), lambda: (0, 0, 0)),
        out_shape=jax.ShapeDtypeStruct((SWA_HEADS, BLOCK, 3 * BLOCK), F32),
    )(table, bucket)


def _t5_table_grad(dbias, bucket):
    def body(d_ref, b_ref, o_ref):
        bk = b_ref[...]
        lane = lax.broadcasted_iota(jnp.int32, (1, 128), 1)
        for b in range(T5_BUCKETS):
            hit = bk == b
            row = jnp.zeros((1, 128), F32)
            for h in range(SWA_HEADS):
                row = row + jnp.where(lane == h, _all_sum(jnp.where(hit, d_ref[h], 0.0)), 0.0)
            o_ref[b:b + 1, :] = row

    return pl.pallas_call(
        body, name="t5_table_grad",
        in_specs=[pl.BlockSpec((SWA_HEADS, BLOCK, 3 * BLOCK), lambda: (0, 0, 0)), pl.BlockSpec((BLOCK, 3 * BLOCK), lambda: (0, 0))],
        out_specs=pl.BlockSpec((T5_BUCKETS, 128), lambda: (0, 0)),
        out_shape=jax.ShapeDtypeStruct((T5_BUCKETS, 128), F32),
    )(dbias, bucket)


def _swa_scores(i, nb, q, kw, bias_h, sink_h):
    s = _dot(q, kw, NT) + bias_h
    col = lax.broadcasted_iota(jnp.int32, s.shape, 1)
    first_col = jnp.where(i == 0, BLOCK, 0)
    end_col = jnp.where(i == nb - 1, 2 * BLOCK, 3 * BLOCK)
    s = jnp.where((col < first_col) | (col >= end_col), NEG_INF, s)
    m = jnp.maximum(jnp.max(s, axis=-1, keepdims=True), sink_h)
    p = jnp.exp(s - m)
    e_sink = jnp.exp(sink_h - m)
    inv = 1.0 / (jnp.sum(p, axis=-1, keepdims=True) + e_sink)
    return p * inv, e_sink * inv


def _swa_group(q_ref, sink_ref, kh, G):
    heads = range(kh * G, (kh + 1) * G)
    q4 = jnp.concatenate([_cols(q_ref, h * HEAD_DIM) for h in heads], axis=0)
    sink4 = jnp.concatenate([jnp.full((BLOCK, 1), sink_ref[0, h], F32) for h in heads], axis=0)
    return q4, sink4


def _swa_window(ref, i, nb, off):
    prev, nxt = jnp.maximum(i - 1, 0), jnp.minimum(i + 1, nb - 1)
    rows = [pl.ds(pl.multiple_of(b * BLOCK, BLOCK), BLOCK) for b in (prev, i, nxt)]
    return jnp.concatenate([ref[r, off:off + HEAD_DIM] for r in rows], axis=0), rows


def _swa_fwd(q, k, v, bias, sink, carried=None):
    S = q.shape[0]
    nb = S // BLOCK
    G = SWA_HEADS // SWA_KV_HEADS

    def body(sink_ref, q_ref, k_ref, v_ref, bias_ref, o_ref):
        i = pl.program_id(0)
        for kh in range(SWA_KV_HEADS):
            kw, _ = _swa_window(k_ref, i, nb, kh * HEAD_DIM)
            vw, _ = _swa_window(v_ref, i, nb, kh * HEAD_DIM)
            q4, sink4 = _swa_group(q_ref, sink_ref, kh, G)
            p, _ = _swa_scores(i, nb, q4, kw, bias_ref[kh], sink4)
            o4 = _dot(p.astype(vw.dtype), vw).astype(o_ref.dtype)
            for g in range(G):
                h = kh * G + g
                o_ref[:, h * HEAD_DIM:(h + 1) * HEAD_DIM] = o4[g * BLOCK:(g + 1) * BLOCK]

    full_kv = pl.BlockSpec((S, SWA_KV_HEADS * HEAD_DIM), lambda i: (0, 0))
    (o,), landed = _carry_call(
        body, (sink, q, k, v, bias.reshape(SWA_KV_HEADS, G * BLOCK, 3 * BLOCK)), name="swa_fwd", grid=(nb,),
        in_specs=[SMEM_SPEC, pl.BlockSpec((BLOCK, 1024), lambda i: (i, 0)), full_kv, full_kv,
                  pl.BlockSpec((SWA_KV_HEADS, G * BLOCK, 3 * BLOCK), lambda i: (0, 0, 0))],
        out_specs=[pl.BlockSpec((BLOCK, 1024), lambda i: (i, 0))],
        out_shape=[jax.ShapeDtypeStruct((S, 1024), BF)], semantics=("parallel",), carried=carried)
    return o, landed


def _swa_bwd(q, k, v, do, bias, sink, carried=None):
    S = q.shape[0]
    nb = S // BLOCK
    G = SWA_HEADS // SWA_KV_HEADS

    def body(sink_ref, q_ref, k_ref, v_ref, do_ref, bias_ref, dq_ref, dk_ref, dv_ref, dbias_ref, dsink_ref):
        i = pl.program_id(0)

        @pl.when(i == 0)
        def _():
            dk_ref[...] = jnp.zeros_like(dk_ref)
            dv_ref[...] = jnp.zeros_like(dv_ref)
            dbias_ref[...] = jnp.zeros_like(dbias_ref)
            dsink_ref[...] = jnp.zeros_like(dsink_ref)

        for kh in range(SWA_KV_HEADS):
            off = kh * HEAD_DIM
            kw, rows = _swa_window(k_ref, i, nb, off)
            vw, _ = _swa_window(v_ref, i, nb, off)
            q4, sink4 = _swa_group(q_ref, sink_ref, kh, G)
            p, p_sink = _swa_scores(i, nb, q4, kw, bias_ref[kh], sink4)
            do4 = jnp.concatenate([_cols(do_ref, (kh * G + g) * HEAD_DIM) for g in range(G)], axis=0).astype(vw.dtype)
            dp = _dot(do4, vw, NT)
            delta = jnp.sum(p * dp, axis=-1, keepdims=True)
            ds = p * (dp - delta)
            dsb = ds.astype(q4.dtype)
            dq4 = _dot(dsb, kw)
            dkw = _dot(dsb, q4, TN)
            dvw = _dot(p.astype(do4.dtype), do4, TN)
            dbias_ref[kh] += ds
            sink_term = p_sink * delta
            for g in range(G):
                h = kh * G + g
                dq_ref[:, h * HEAD_DIM:(h + 1) * HEAD_DIM] = dq4[g * BLOCK:(g + 1) * BLOCK].astype(dq_ref.dtype)
                dsink_ref[h:h + 1, :] += jnp.broadcast_to(-_row_sum(sink_term[g * BLOCK:(g + 1) * BLOCK]), (1, 128))
            for b, r in enumerate(rows):
                dk_ref[r, off:off + HEAD_DIM] += dkw[b * BLOCK:(b + 1) * BLOCK]
                dv_ref[r, off:off + HEAD_DIM] += dvw[b * BLOCK:(b + 1) * BLOCK]

    full_kv = pl.BlockSpec((S, SWA_KV_HEADS * HEAD_DIM), lambda i: (0, 0))
    blk = pl.BlockSpec((BLOCK, 1024), lambda i: (i, 0))
    bias_spec = pl.BlockSpec((SWA_KV_HEADS, G * BLOCK, 3 * BLOCK), lambda i: (0, 0, 0))
    outs, landed = _carry_call(
        body, (sink, q, k, v, do, bias.reshape(SWA_KV_HEADS, G * BLOCK, 3 * BLOCK)), name="swa_bwd", grid=(nb,),
        in_specs=[SMEM_SPEC, blk, full_kv, full_kv, blk, bias_spec],
        out_specs=[blk, full_kv, full_kv, bias_spec, pl.BlockSpec((8, 128), lambda i: (0, 0))],
        out_shape=[jax.ShapeDtypeStruct((S, 1024), BF), jax.ShapeDtypeStruct((S, 256), F32),
                   jax.ShapeDtypeStruct((S, 256), F32),
                   jax.ShapeDtypeStruct((SWA_KV_HEADS, G * BLOCK, 3 * BLOCK), F32), jax.ShapeDtypeStruct((8, 128), F32)],
        vmem=VMEM_LARGE, semantics=("arbitrary",), carried=carried)
    dq, dk, dv, dbias, dsink = outs
    return dq, dk, dv, dbias.reshape(SWA_HEADS, BLOCK, 3 * BLOCK), dsink, landed


def _prep_odd_fwd(proj, cos, sin, gq, gk):
    S = proj.shape[0]
    tr = _pick(S, (512,))

    def body(p_ref, cos_ref, sin_ref, gq_ref, gk_ref, q_ref, k_ref, v_ref):
        cos_v, sin_v = cos_ref[...], sin_ref[...]
        for h in range(AX_HEADS):
            o = h * HEAD_DIM
            xh, _ = _head_norm(_cols(p_ref, o))
            q_ref[:, o:o + HEAD_DIM] = (_rope(xh * gq_ref[...], cos_v, sin_v, 32) * AX_SCALE).astype(q_ref.dtype)
        for h in range(AX_KV_HEADS):
            o = h * HEAD_DIM
            xh, _ = _head_norm(_cols(p_ref, 1024 + o))
            k_ref[:, o:o + HEAD_DIM] = _rope(xh * gk_ref[...], cos_v, sin_v, 32).astype(k_ref.dtype)
        v_ref[...] = p_ref[:, 1280:1536].astype(v_ref.dtype)

    def row(w):
        return pl.BlockSpec((tr, w), lambda i: (i, 0))

    vec = pl.BlockSpec((1, HEAD_DIM), lambda i: (0, 0))
    return pl.pallas_call(
        body, name="prep_odd_fwd", grid=(S // tr,),
        in_specs=[row(ODD_IN), row(128), row(128), vec, vec], out_specs=[row(1024), row(256), row(256)],
        out_shape=[jax.ShapeDtypeStruct((S, w), BF) for w in (1024, 256, 256)],
        compiler_params=_params(("parallel",)),
    )(proj, cos, sin, gq, gk)


def _prep_odd_bwd(proj, cos, sin, gq, gk, dq, dk, dv):
    S = proj.shape[0]
    tr = _pick(S, (512,))

    def body(p_ref, cos_ref, sin_ref, gq_ref, gk_ref, dq_ref, dk_ref, dv_ref, dp_ref, dgq_ref, dgk_ref):
        cos_v, sin_v = cos_ref[...], sin_ref[...]
        dt = dp_ref.dtype
        dgq = jnp.zeros((1, HEAD_DIM), F32)
        for h in range(AX_HEADS):
            o = h * HEAD_DIM
            xh, r = _head_norm(_cols(p_ref, o))
            dy = _rope_t(_cols(dq_ref, o).astype(F32) * AX_SCALE, cos_v, sin_v, 32)
            dgq = dgq + _row_sum(dy * xh)
            dp_ref[:, o:o + HEAD_DIM] = _head_norm_bwd(dy * gq_ref[...], xh, r).astype(dt)
        dgk = jnp.zeros((1, HEAD_DIM), F32)
        for h in range(AX_KV_HEADS):
            o = h * HEAD_DIM
            xh, r = _head_norm(_cols(p_ref, 1024 + o))
            dy = _rope_t(_cols(dk_ref, o), cos_v, sin_v, 32)
            dgk = dgk + _row_sum(dy * xh)
            dp_ref[:, 1024 + o:1024 + o + HEAD_DIM] = _head_norm_bwd(dy * gk_ref[...], xh, r).astype(dt)
        dp_ref[:, 1280:1536] = dv_ref[...].astype(dt)

        @pl.when(pl.program_id(0) == 0)
        def _():
            dgq_ref[...] = jnp.zeros_like(dgq_ref)
            dgk_ref[...] = jnp.zeros_like(dgk_ref)

        dgq_ref[...] += dgq
        dgk_ref[...] += dgk

    def row(w):
        return pl.BlockSpec((tr, w), lambda i: (i, 0))

    vec = pl.BlockSpec((1, HEAD_DIM), lambda i: (0, 0))
    return pl.pallas_call(
        body, name="prep_odd_bwd", grid=(S // tr,),
        in_specs=[row(ODD_IN), row(128), row(128), vec, vec, row(1024), row(256), row(256)],
        out_specs=[row(ODD_IN), vec, vec],
        out_shape=[jax.ShapeDtypeStruct((S, ODD_IN), BF), jax.ShapeDtypeStruct((1, HEAD_DIM), F32),
                   jax.ShapeDtypeStruct((1, HEAD_DIM), F32)],
        compiler_params=_params(("arbitrary",)),
    )(proj, cos, sin, gq, gk, dq, dk, dv)


def _loop_unrolled(n, factor, step, init):
    while n % factor:
        factor //= 2

    def trip(t, carry):
        for u in range(factor):
            carry = step(factor * t + u, carry)
        return carry

    return lax.fori_loop(0, n // factor, trip, init)


def _flash_fwd(q, k, v, carried=None):
    S = q.shape[0]
    tq = _pick(S, (512,))
    tk = _pick(S, (1024, 512))
    nk = S // tk
    G = AX_HEADS // AX_KV_HEADS

    def body(q_ref, k_ref, v_ref, o_ref, lse_ref):
        qv = q_ref[...]

        def step(j, carry):
            m, l, acc = carry
            rows = pl.ds(pl.multiple_of(j * tk, tk), tk)
            s = _dot(qv, k_ref[rows, :], NT)
            m_new = jnp.maximum(m, jnp.max(s, axis=-1, keepdims=True))
            alpha = jnp.exp2(m - m_new)
            p = jnp.exp2(s - m_new)
            l = alpha * l + jnp.sum(p, axis=-1, keepdims=True)
            acc = alpha * acc + _dot(p.astype(v_ref.dtype), v_ref[rows, :])
            return m_new, l, acc

        init = (jnp.full((tq, 1), NEG_INF, F32), jnp.zeros((tq, 1), F32), jnp.zeros((tq, HEAD_DIM), F32))
        m, l, acc = _loop_unrolled(nk, 8, step, init)
        o_ref[...] = (acc / l).astype(o_ref.dtype)
        lse_ref[0] = jnp.broadcast_to(m + jnp.log2(l), (tq, 128))

    (o, lse), landed = _carry_call(
        body, (q, k, v), name="flash_fwd", grid=(AX_HEADS, S // tq),
        in_specs=[pl.BlockSpec((tq, HEAD_DIM), lambda h, i: (i, h)),
                  pl.BlockSpec((S, HEAD_DIM), lambda h, i: (0, h // G)),
                  pl.BlockSpec((S, HEAD_DIM), lambda h, i: (0, h // G))],
        out_specs=[pl.BlockSpec((tq, HEAD_DIM), lambda h, i: (i, h)),
                   pl.BlockSpec((1, tq, 128), lambda h, i: (h, i, 0))],
        out_shape=[jax.ShapeDtypeStruct((S, AX_HEADS * HEAD_DIM), BF), jax.ShapeDtypeStruct((AX_HEADS, S, 128), F32)],
        semantics=("parallel", "parallel"), carried=carried)
    return o, lse, landed


def _flash_bwd(q, k, v, o, do, lse, carried=None):
    S = q.shape[0]
    tq = _pick(S, (512,))
    tk = _pick(S, (1024, 512))
    nq, nk = S // tq, S // tk
    G = AX_HEADS // AX_KV_HEADS

    def body(q_ref, k_ref, v_ref, o_ref, do_ref, lse_ref, dq_ref, dk_ref, dv_ref):
        g, i = pl.program_id(1), pl.program_id(2)

        @pl.when((g == 0) & (i == 0))
        def _():
            dk_ref[...] = jnp.zeros_like(dk_ref)
            dv_ref[...] = jnp.zeros_like(dv_ref)

        qv = q_ref[...]
        do_f = do_ref[...].astype(F32)
        dob = do_f.astype(qv.dtype)
        dob_ln2 = (do_f * LN2).astype(qv.dtype)
        delta = jnp.sum(do_f * o_ref[...].astype(F32), axis=-1, keepdims=True) * LN2
        lse_col = lse_ref[0][:, 0:1]

        def step(j, dq):
            rows = pl.ds(pl.multiple_of(j * tk, tk), tk)
            kj, vj = k_ref[rows, :], v_ref[rows, :]
            p = jnp.exp2(_dot(qv, kj, NT) - lse_col)
            dp = _dot(dob_ln2, vj, NT)
            ds = (p * (dp - delta)).astype(qv.dtype)
            dk_ref[rows, :] += _dot(ds, qv, TN)
            dv_ref[rows, :] += _dot(p.astype(dob.dtype), dob, TN)
            return dq + _dot(ds, kj)

        dq_ref[...] = _loop_unrolled(nk, 4, step, jnp.zeros((tq, HEAD_DIM), F32)).astype(dq_ref.dtype)

    q_spec = pl.BlockSpec((tq, HEAD_DIM), lambda kh, g, i: (i, kh * G + g))
    kv_spec = pl.BlockSpec((S, HEAD_DIM), lambda kh, g, i: (0, kh))
    (dq, dk, dv), landed = _carry_call(
        body, (q, k, v, o, do, lse), name="flash_bwd", grid=(AX_KV_HEADS, G, nq),
        in_specs=[q_spec, kv_spec, kv_spec, q_spec, q_spec,
                  pl.BlockSpec((1, tq, 128), lambda kh, g, i: (kh * G + g, i, 0))],
        out_specs=[q_spec, kv_spec, kv_spec],
        out_shape=[jax.ShapeDtypeStruct((S, AX_HEADS * HEAD_DIM), BF), jax.ShapeDtypeStruct((S, 256), F32),
                   jax.ShapeDtypeStruct((S, 256), F32)],
        vmem=VMEM_LARGE, semantics=("arbitrary", "arbitrary", "arbitrary"), carried=carried)
    return dq, dk, dv, landed


def _rope_angles(pos, dim, theta):
    inv = theta ** (-jnp.arange(0, dim, 2, dtype=jnp.float32) / dim)
    return pos.astype(jnp.float32)[:, None] * inv[None, :]


def _rope_tables(S):
    ang = _rope_angles(jnp.arange(S), RET_DK, RET_THETA)
    c, s = jnp.cos(ang), jnp.sin(ang)
    ret = (jnp.concatenate([c, c], -1), jnp.concatenate([-s, s], -1))
    rows = S // GRID_W
    row = jnp.repeat(jnp.arange(rows), GRID_W)
    col = jnp.tile(jnp.arange(GRID_W), rows)
    ar, ac = _rope_angles(row, HEAD_DIM // 2, AX_THETA), _rope_angles(col, HEAD_DIM // 2, AX_THETA)
    cr, sr, cc, sc = jnp.cos(ar), jnp.sin(ar), jnp.cos(ac), jnp.sin(ac)
    ax = (jnp.concatenate([cr, cr, cc, cc], -1), jnp.concatenate([-sr, sr, -sc, sc], -1))
    return ret, ax


def _pad_tile(a):
    return jnp.pad(a.astype(F32), ((0, 8 - a.shape[0]), (0, 128 - a.shape[1])))


def _relu2_epilogue(acc):
    r = jnp.maximum(acc, 0.0)
    return acc, r * r


def _relu2_bwd_epilogue(acc, u):
    return (acc * (2.0 * jnp.maximum(u.astype(F32), 0.0)),)


def _add_epilogue(acc, res):
    return (acc + res,)


def _mlp_fwd(x, g, w_up, w_down, tag):
    h = _rms_fwd(x, g, f"mlp_norm_{tag}")
    u, a = _mm(h, w_up, name=f"mlp_up_{tag}", out_dtypes=(BF, BF), epilogue=_relu2_epilogue)
    y = _mm(a, w_down, name=f"mlp_down_{tag}", extras=(x,), epilogue=_add_epilogue)
    return y, (h, u, a)


def _mlp_bwd(x, g, w_up, w_down, saved, dy, dyb, tag):
    h, u, a = saved
    du = _mm(dyb, w_down, tb=True, name=f"mlp_down_dx_{tag}", out_dtypes=(BF,), extras=(u,), epilogue=_relu2_bwd_epilogue)
    dw_down = _mm(a, dyb, ta=True, name=f"mlp_down_dw_{tag}", out_dtypes=(BF,))
    dw_up = _mm(h, du, ta=True, name=f"mlp_up_dw_{tag}", out_dtypes=(BF,))
    dh = _mm(du, w_up, tb=True, name=f"mlp_up_dx_{tag}")
    dx, dxb, dg = _rms_bwd(x, g, dh, dy, f"mlp_norm_bwd_{tag}")
    return dx, dxb, dg, dw_up, dw_down


COL_SHARDED = ("in_even", "in_odd", "up0", "up1")


def _assemble(key, g):
    if key in COL_SHARDED:
        return g.transpose(1, 0, 2).reshape(g.shape[1], N_DEV * g.shape[2])
    return g.reshape(N_DEV * g.shape[1], g.shape[2])


def _split(key, full):
    rows, cols = full.shape
    if key in COL_SHARDED:
        return full.reshape(rows, N_DEV, cols // N_DEV).transpose(1, 0, 2)
    return full.reshape(N_DEV, rows // N_DEV, cols)


def _local_step(x, target, W, P, late=None):
    S = x.shape[0]
    W = dict(W)
    landed = {}

    def gather_while(stage):
        return None if late is None else _Exchange([s for _, s in late[stage]], gather=True)

    def arrived(stage, outs):
        for (key, _), g in zip([] if late is None else late[stage], outs):
            W[key] = _assemble(key, g)

    def exchange_while(grads):
        return None if late is None else _Exchange([_split(k, g) for k, g in grads], gather=False)

    def left(grads, outs):
        for (key, _), l in zip(grads, outs):
            landed[key] = l

    (cos_r, sin_r), (cos_a, sin_a) = _rope_tables(S)
    bucket = _t5_bucket_map()
    logit = P["ret_decay_logit"]
    lg = _log_sigmoid_tile(_pad_tile(logit))
    bias = _t5_bias(P["t5_table"], bucket)
    nmix, nmlp = P["norm_mix"], P["norm_mlp"]

    h0 = _rms_fwd(x, nmix[0:1], "mix_norm_0")
    proj_e = _mm(h0, W["in_even"], name="in_even")
    qr, kr, vr, qs, ks, vs = _prep_even_fwd(proj_e, cos_r, sin_r, P["swa_q_norm"], P["swa_k_norm"])
    sf, sb = _ret_scan(kr, vr, lg, (0, 0), (1, 0), "retention_states")
    y_ret = _ret_fwd(qr, kr, vr, lg, sf, sb)
    oa = _ret_post_fwd(y_ret, proj_e, P["ret_norm"])
    ob, outs = _swa_fwd(qs, ks, vs, bias, P["swa_sink"], gather_while("swa_fwd"))
    arrived("swa_fwd", outs)
    wo_a, wo_b = W["out_even"][:1024], W["out_even"][1024:]
    x1 = _mm(oa, wo_a, name="out_even_a", extras=(x,), epilogue=_add_epilogue)
    x1 = _mm(ob, wo_b, name="out_even_b", extras=(x1,), epilogue=_add_epilogue)
    x2, mlp0 = _mlp_fwd(x1, nmlp[0:1], W["up0"], W["down0"], "0")

    h2 = _rms_fwd(x2, nmix[1:2], "mix_norm_1")
    proj_o = _mm(h2, W["in_odd"], name="in_odd")
    qx, kx, vx = _prep_odd_fwd(proj_o, cos_a, sin_a, P["ax_q_norm"], P["ax_k_norm"])
    ox, lse, outs = _flash_fwd(qx, kx, vx, gather_while("flash_fwd"))
    arrived("flash_fwd", outs)
    x3 = _mm(ox, W["out_odd"], name="out_odd", extras=(x2,), epilogue=_add_epilogue)
    x4, mlp1 = _mlp_fwd(x3, nmlp[1:2], W["up1"], W["down1"], "1")

    d4, d4b, loss_tile = _loss_and_grad(x4, target)

    d3, d3b, dnmlp1, dw_up1, dw_down1 = _mlp_bwd(x3, nmlp[1:2], W["up1"], W["down1"], mlp1, d4, d4b, "1")
    dox = _mm(d3b, W["out_odd"], tb=True, name="out_odd_dx")
    dw_out_odd = _mm(ox, d3b, ta=True, name="out_odd_dw", out_dtypes=(BF,))
    grads1 = [("up1", dw_up1), ("down1", dw_down1), ("out_odd", dw_out_odd)]
    dqx, dkx, dvx, outs = _flash_bwd(qx, kx, vx, ox, dox, lse, exchange_while(grads1))
    left(grads1, outs)
    dproj_o, dgq_ax, dgk_ax = _prep_odd_bwd(proj_o, cos_a, sin_a, P["ax_q_norm"], P["ax_k_norm"], dqx, dkx, dvx)
    dw_in_odd = _mm(h2, dproj_o, ta=True, name="in_odd_dw", out_dtypes=(BF,))
    dh2 = _mm(dproj_o, W["in_odd"], tb=True, name="in_odd_dx")
    d2, d2b, dnmix1 = _rms_bwd(x2, nmix[1:2], dh2, d3, "mix_norm_bwd_1")

    d1, d1b, dnmlp0, dw_up0, dw_down0 = _mlp_bwd(x1, nmlp[0:1], W["up0"], W["down0"], mlp0, d2, d2b, "0")
    doa = _mm(d1b, wo_a, tb=True, name="out_even_a_dx")
    dob = _mm(d1b, wo_b, tb=True, name="out_even_b_dx")
    dw_out_even = jnp.concatenate([_mm(oa, d1b, ta=True, name="out_even_a_dw", out_dtypes=(BF,)),
                                   _mm(ob, d1b, ta=True, name="out_even_b_dw", out_dtypes=(BF,))], axis=0)
    dy_ret, dga, dret_norm = _ret_post_bwd(y_ret, proj_e, P["ret_norm"], doa)
    hb, hf = _ret_scan(qr, dy_ret, lg, (1, 1), (0, 1), "retention_state_grads")
    dqr, dkr, dvr, dlogit = _ret_bwd(qr, kr, vr, dy_ret, lg, logit, sf, sb, hf, hb)
    grads0 = [("in_odd", dw_in_odd), ("up0", dw_up0), ("down0", dw_down0), ("out_even", dw_out_even)]
    dqs, dks, dvs, dbias, dsink, outs = _swa_bwd(qs, ks, vs, dob, bias, P["swa_sink"], exchange_while(grads0))
    left(grads0, outs)
    dt5 = _t5_table_grad(dbias, bucket)
    dproj_e, dgq_swa, dgk_swa = _prep_even_bwd(proj_e, cos_r, sin_r, P["swa_q_norm"], P["swa_k_norm"],
                                               dqr, dkr, dvr, dga, dqs, dks, dvs)
    dw_in_even = _mm(h0, dproj_e, ta=True, name="in_even_dw", out_dtypes=(BF,))
    dh0 = _mm(dproj_e, W["in_even"], tb=True, name="in_even_dx")
    dx, _, dnmix0 = _rms_bwd(x, nmix[0:1], dh0, d1, "mix_norm_bwd_0")

    if late is None:
        dW = dict(grads1 + grads0, in_even=dw_in_even)
    else:
        dW = ({"in_even": dw_in_even}, landed)
    dP = {"norm_mix": jnp.concatenate([dnmix0, dnmix1], 0), "norm_mlp": jnp.concatenate([dnmlp0, dnmlp1], 0),
          "ret_decay_logit": dlogit[:, :, 0, 0].T, "ret_norm": dret_norm,
          "swa_q_norm": dgq_swa, "swa_k_norm": dgk_swa, "swa_sink": dsink[:, 0][None, :],
          "t5_table": dt5[:, :SWA_HEADS], "ax_q_norm": dgq_ax, "ax_k_norm": dgk_ax}
    return loss_tile, dx, dW, dP


def _cast_shards(shards):
    n = len(shards)

    def body(*refs):
        for i_ref, o_ref in zip(refs[:n], refs[n:]):
            o_ref[...] = i_ref[...].astype(o_ref.dtype)

    return pl.pallas_call(body, name="cast_shards", in_specs=[VMEM_SPEC] * n, out_specs=[VMEM_SPEC] * n,
                          out_shape=[jax.ShapeDtypeStruct(s.shape, BF) for s in shards],
                          compiler_params=pltpu.CompilerParams(vmem_limit_bytes=VMEM_SMALL))(*shards)


def _all_gather(shards):
    n = len(shards)

    def body(*refs):
        ins, outs, stage = refs[:n], refs[n:2 * n], refs[2 * n:3 * n]
        send_sems, recv_sems, local_sems = refs[3 * n:]
        me = _my_place()
        sibling = _flip(me, 1)
        chips = [_flip(me, 4), _flip(me, 2), _flip(me, 6)]

        def copy(a, k, block, to, src=None):
            dst = outs[a].at[_index(block)]
            return pltpu.make_async_remote_copy(
                src_ref=dst if src is None else src, dst_ref=dst,
                send_sem=send_sems.at[a, k], recv_sem=recv_sems.at[a, k], device_id=to, device_id_type=MESH)

        first, mine = [], []
        for a in range(n):
            stage[a][...] = ins[a][...].astype(stage[a].dtype)
            mine.append(pltpu.make_async_copy(stage[a], outs[a].at[_index(me)], local_sems.at[a]))
            mine[-1].start()
            first.append(copy(a, 0, me, sibling, src=stage[a]))
            first += [copy(a, 1 + j, me, chip, src=stage[a]) for j, chip in enumerate(chips)]
        for cp in first:
            cp.start()
        passed = []
        for a in range(n):
            for j, chip in enumerate(chips):
                copy(a, 1 + j, chip, me).wait_recv()
                passed.append(copy(a, 4 + j, chip, sibling))
                passed[-1].start()
        for a in range(n):
            copy(a, 0, sibling, me).wait_recv()
            for j, chip in enumerate(chips):
                copy(a, 4 + j, _flip(chip, 1), me).wait_recv()
        for cp in first + passed:
            cp.wait_send()
        for cp in mine:
            cp.wait()

    return pl.pallas_call(
        body, name="weights_all_gather",
        in_specs=[VMEM_SPEC] * n, out_specs=[ANY_SPEC] * n,
        out_shape=[jax.ShapeDtypeStruct((N_DEV,) + s.shape, BF) for s in shards],
        scratch_shapes=[pltpu.VMEM(s.shape, BF) for s in shards]
        + [pltpu.SemaphoreType.DMA((n, 7)), pltpu.SemaphoreType.DMA((n, 7)), pltpu.SemaphoreType.DMA((n,))],
        compiler_params=pltpu.CompilerParams(vmem_limit_bytes=VMEM_SMALL),
    )(*shards)


def _all_reduce_small(part):
    R, C = part.shape

    def body(x_ref, o_ref, land_ref, send_sems, recv_sems):
        me = _my_place()
        land_ref[_index(me)] = x_ref[...]
        copies = []
        for k in range(1, N_DEV):
            peer = _flip(me, k)
            copies.append(pltpu.make_async_remote_copy(
                src_ref=x_ref, dst_ref=land_ref.at[_index(me)],
                send_sem=send_sems.at[k - 1], recv_sem=recv_sems.at[k - 1], device_id=peer, device_id_type=MESH))
            copies[-1].start()
        for k in range(1, N_DEV):
            peer = _flip(me, k)
            pltpu.make_async_remote_copy(
                src_ref=x_ref, dst_ref=land_ref.at[_index(peer)],
                send_sem=send_sems.at[k - 1], recv_sem=recv_sems.at[k - 1], device_id=peer, device_id_type=MESH).wait_recv()
        for cp in copies:
            cp.wait_send()
        acc = land_ref[0]
        for s in range(1, N_DEV):
            acc = acc + land_ref[s]
        o_ref[...] = acc

    return pl.pallas_call(
        body, name="small_all_reduce", in_specs=[VMEM_SPEC], out_specs=VMEM_SPEC,
        out_shape=jax.ShapeDtypeStruct((R, C), F32),
        scratch_shapes=[pltpu.VMEM((N_DEV, R, C), F32), pltpu.SemaphoreType.DMA((7,)), pltpu.SemaphoreType.DMA((7,))],
    )(part)


def _adamw_math(w, g, m, v):
    m = ADAM_B1 * m + (1.0 - ADAM_B1) * g
    v = ADAM_B2 * v + (1.0 - ADAM_B2) * jnp.square(g)
    m_hat = m / (1.0 - ADAM_B1 ** ADAM_STEP)
    v_hat = v / (1.0 - ADAM_B2 ** ADAM_STEP)
    delta = -ADAM_LR * (m_hat / (jnp.sqrt(v_hat) + ADAM_EPS) + ADAM_WD * w)
    return delta, m, v


def _sum_and_adamw(landed, w, m, v, name):
    R, C = w.shape
    tr = _pick(R, (256, 128))

    def body(l_ref, w_ref, m_ref, v_ref, g_ref, d_ref, nm_ref, nv_ref):
        g = l_ref[0].astype(F32)
        for s in range(1, N_DEV):
            g = g + l_ref[s].astype(F32)
        g_ref[...] = g
        d_ref[...], nm_ref[...], nv_ref[...] = _adamw_math(w_ref[...], g, m_ref[...], v_ref[...])

    row = pl.BlockSpec((tr, C), lambda i: (i, 0))
    return pl.pallas_call(
        body, name=name, grid=(R // tr,),
        in_specs=[pl.BlockSpec((N_DEV, tr, C), lambda i: (0, i, 0)), row, row, row], out_specs=[row] * 4,
        out_shape=[jax.ShapeDtypeStruct((R, C), F32)] * 4, compiler_params=_params(("parallel",)),
    )(landed, w, m, v)


def _adamw_small(w, g, m, v):
    def body(w_ref, g_ref, m_ref, v_ref, d_ref, nm_ref, nv_ref):
        d_ref[...], nm_ref[...], nv_ref[...] = _adamw_math(w_ref[...], g_ref[...], m_ref[...], v_ref[...])

    full = pl.BlockSpec(w.shape, lambda: (0, 0))
    return pl.pallas_call(body, name="adamw_small", in_specs=[full] * 4, out_specs=[full] * 3,
                          out_shape=[jax.ShapeDtypeStruct(w.shape, F32)] * 3)(w, g, m, v)


MATRICES = ("w_in_even", "w_out_even", "w_in_odd", "w_out_odd", "w_mlp_up", "w_mlp_down")
SMALL = ("norm_mix", "norm_mlp", "ret_decay_logit", "ret_norm", "swa_q_norm", "swa_k_norm", "swa_sink",
         "t5_table", "ax_q_norm", "ax_k_norm")
MATRIX_OF = {"in_even": ("w_in_even", 0), "out_even": ("w_out_even", 0), "in_odd": ("w_in_odd", 0),
             "out_odd": ("w_out_odd", 0), "up0": ("w_mlp_up", 0), "up1": ("w_mlp_up", 1),
             "down0": ("w_mlp_down", 0), "down1": ("w_mlp_down", 1)}
GATHER_FIRST = ("in_even", "in_odd", "out_even")
GATHER_WHILE = {"swa_fwd": ("up0", "down0"), "flash_fwd": ("out_odd", "up1", "down1")}


SMALL_ROWS = 8
SMALL_AT = {"norm_mix": (0, 0), "norm_mlp": (2, 0), "ret_norm": (4, 0), "swa_q_norm": (5, 0), "swa_k_norm": (5, 128),
            "ax_q_norm": (5, 256), "ax_k_norm": (5, 384), "swa_sink": (5, 512), "ret_decay_logit": (5, 640),
            "t5_table": (6, 0)}
LOSS_AT = (5, 768)


def _pack_small(arrays, loss=None):
    buf = jnp.zeros((SMALL_ROWS, 1024), F32)
    for name, (r, c) in SMALL_AT.items():
        a = arrays[name].astype(F32)
        a = a.reshape(1, -1) if name in ("ret_decay_logit", "t5_table") else a.reshape(-1, a.shape[-1])
        buf = lax.dynamic_update_slice(buf, a, (r, c))
    if loss is not None:
        buf = lax.dynamic_update_slice(buf, loss.reshape(1, 1), LOSS_AT)
    return buf


def _unpack_small(buf, like):
    out = {}
    for name, (r, c) in SMALL_AT.items():
        shape = like[name].shape
        rows = 1 if name in ("ret_decay_logit", "t5_table") else math.prod(shape[:-1])
        cols = math.prod(shape) // rows
        out[name] = buf[r:r + rows, c:c + cols].reshape(shape)
    return out


def kernel(x, norm_mix, norm_mlp, w_in_even, w_out_even, ret_decay_logit, ret_norm, swa_q_norm, swa_k_norm, swa_sink, t5_table, w_in_odd, w_out_odd, ax_q_norm, ax_k_norm, w_mlp_up, w_mlp_down, loss_target, m_norm_mix, m_norm_mlp, m_w_in_even, m_w_out_even, m_ret_decay_logit, m_ret_norm, m_swa_q_norm, m_swa_k_norm, m_swa_sink, m_t5_table, m_w_in_odd, m_w_out_odd, m_ax_q_norm, m_ax_k_norm, m_w_mlp_up, m_w_mlp_down, v_norm_mix, v_norm_mlp, v_w_in_even, v_w_out_even, v_ret_decay_logit, v_ret_norm, v_swa_q_norm, v_swa_k_norm, v_swa_sink, v_t5_table, v_w_in_odd, v_w_out_odd, v_ax_q_norm, v_ax_k_norm, v_w_mlp_up, v_w_mlp_down):
    given = dict(locals())
    weights = {n: given[n] for n in MATRICES + SMALL}
    moments_m = {n: given["m_" + n] for n in MATRICES + SMALL}
    moments_v = {n: given["v_" + n] for n in MATRICES + SMALL}

    def shard(table, key):
        arg, layer = MATRIX_OF[key]
        return table[arg][layer]

    gathered = _all_gather([shard(weights, k) for k in GATHER_FIRST])
    W = {k: _assemble(k, g) for k, g in zip(GATHER_FIRST, gathered)}
    late_keys = [k for keys in GATHER_WHILE.values() for k in keys]
    cast = dict(zip(late_keys, _cast_shards([shard(weights, k) for k in late_keys])))
    late = {stage: [(k, cast[k]) for k in keys] for stage, keys in GATHER_WHILE.items()}
    P = {"norm_mix": norm_mix, "norm_mlp": norm_mlp, "ret_decay_logit": ret_decay_logit[0], "ret_norm": ret_norm,
         "swa_q_norm": swa_q_norm, "swa_k_norm": swa_k_norm, "swa_sink": swa_sink, "t5_table": t5_table,
         "ax_q_norm": ax_q_norm, "ax_k_norm": ax_k_norm}

    loss_tile, dx, (dW, landed), dP = _local_step(x[0], loss_target[0], W, P, late)

    landed["in_even"], = _exchange_blocks([_split("in_even", dW["in_even"])], False, "grads_exchange_last")
    per_key = {k: _sum_and_adamw(landed[k], shard(weights, k), shard(moments_m, k), shard(moments_v, k), "adamw_" + k)
               for k in MATRIX_OF}
    grads, deltas, new_m, new_v = {}, {}, {}, {}
    for i, out in enumerate((grads, deltas, new_m, new_v)):
        for n in MATRICES:
            out[n] = jnp.stack([per_key[k][i] for k, (arg, _) in MATRIX_OF.items() if arg == n])

    dP["ret_decay_logit"] = dP["ret_decay_logit"][None]
    total = _all_reduce_small(_pack_small(dP, loss_tile[0, 0]))
    loss = total[LOSS_AT[0], LOSS_AT[1]]
    small_d, small_m, small_v = _adamw_small(_pack_small(weights), total, _pack_small(moments_m), _pack_small(moments_v))
    like = {n: weights[n] for n in SMALL}
    for out, buf in ((grads, total), (deltas, small_d), (new_m, small_m), (new_v, small_v)):
        out.update(_unpack_small(buf, like))

    order = ("norm_mix", "norm_mlp", "w_in_even", "w_out_even", "ret_decay_logit", "ret_norm", "swa_q_norm", "swa_k_norm",
             "swa_sink", "t5_table", "w_in_odd", "w_out_odd", "ax_q_norm", "ax_k_norm", "w_mlp_up", "w_mlp_down")
    return (loss, dx[None], *[grads[n] for n in order], *[deltas[n] for n in order],
            *[new_m[n] for n in order], *[new_v[n] for n in order])
```

```python
import math

import jax
import jax.numpy as jnp
from jax import lax
from jax.experimental import pallas as pl
from jax.experimental.pallas import tpu as pltpu

F32 = jnp.float32
BF = jnp.bfloat16

D_MODEL = 1024
HEAD_DIM = 128
EPS = 1e-6
NEG_INF = -1e30
RET_HEADS, RET_DK, RET_DV = 4, 128, 256
RET_THETA = 10000.0
SWA_HEADS, SWA_KV_HEADS, WINDOW, BLOCK = 8, 2, 128, 128
T5_BUCKETS, T5_MAX_DIST = 32, 128
AX_HEADS, AX_KV_HEADS, AX_THETA, GRID_W = 8, 2, 10000.0, 64
D_FF = 4096
ATT_SCALE = HEAD_DIM ** -0.5
LN2 = math.log(2.0)
AX_SCALE = ATT_SCALE / LN2
RET_SCALE = RET_DK ** -0.5
N_DEV = 8

ADAM_LR, ADAM_B1, ADAM_B2, ADAM_EPS, ADAM_WD, ADAM_STEP = 0.001, 0.9, 0.999, 1e-08, 0.01, 10

MIB = 1024 * 1024
VMEM_SMALL = 40 * MIB
VMEM_LARGE = 56 * MIB

OFF_QA, OFF_KA, OFF_VA, OFF_GA, OFF_QB, OFF_KB, OFF_VB = 0, 512, 1024, 2048, 3072, 4096, 4352
EVEN_IN = 4608
ODD_IN = 1536

NT = (((1,), (1,)), ((), ()))
TN = (((0,), (0,)), ((), ()))
NN = (((1,), (0,)), ((), ()))


def _dot(a, b, dims=NN):
    return lax.dot_general(a, b, dims, preferred_element_type=F32)


def _params(sem=None, vmem=VMEM_SMALL):
    return pltpu.CompilerParams(dimension_semantics=sem, vmem_limit_bytes=vmem)


def _pick(n, prefs):
    for p in prefs:
        if n % p == 0:
            return p
    return n


def _row_sum(x):
    return jnp.sum(x, axis=0, keepdims=True)


def _all_sum(x):
    return jnp.sum(jnp.sum(x, axis=0, keepdims=True), axis=1, keepdims=True)


def _sigmoid(x):
    return 1.0 / (1.0 + jnp.exp(-x))


SMEM_SPEC = pl.BlockSpec(memory_space=pltpu.SMEM)
ANY_SPEC = pl.BlockSpec(memory_space=pl.ANY)
VMEM_SPEC = pl.BlockSpec(memory_space=pltpu.VMEM)
MESH = pl.DeviceIdType.MESH


def _my_place():
    return lax.axis_index("x"), lax.axis_index("y"), lax.axis_index("c")


def _flip(place, k):
    x, y, c = place
    return (1 - x if k & 4 else x, 1 - y if k & 2 else y, 1 - c if k & 1 else c)


def _index(place):
    x, y, c = place
    return 4 * x + 2 * y + c


class _Exchange:
    def __init__(self, sources, gather):
        self.sources, self.gather, self.n = list(sources), gather, len(sources)
        self.out_shape = [jax.ShapeDtypeStruct(((N_DEV,) + s.shape) if gather else s.shape, s.dtype) for s in self.sources]
        self.scratch = [pltpu.SemaphoreType.DMA((self.n, 7)), pltpu.SemaphoreType.DMA((self.n, 7)),
                        pltpu.SemaphoreType.DMA((self.n,))]

    def _source(self, ins, a, place):
        return ins[a] if self.gather else ins[a].at[_index(place)]

    def _local(self, ins, outs, sems):
        me = _my_place()
        return [pltpu.make_async_copy(self._source(ins, a, me), outs[a].at[_index(me)], sems[2].at[a]) for a in range(self.n)]

    def _remote(self, ins, outs, sems, arriving):
        send_sems, recv_sems, _ = sems
        me = _my_place()
        copies = []
        for a in range(self.n):
            for k in range(1, N_DEV):
                peer = _flip(me, k)
                copies.append(pltpu.make_async_remote_copy(
                    src_ref=self._source(ins, a, peer), dst_ref=outs[a].at[_index(peer if arriving else me)],
                    send_sem=send_sems.at[a, k - 1], recv_sem=recv_sems.at[a, k - 1], device_id=peer, device_id_type=MESH))
        return copies

    def start(self, ins, outs, sems):
        for cp in self._local(ins, outs, sems) + self._remote(ins, outs, sems, arriving=False):
            cp.start()

    def wait(self, ins, outs, sems):
        for cp in self._remote(ins, outs, sems, arriving=True):
            cp.wait_recv()
        for cp in self._remote(ins, outs, sems, arriving=False):
            cp.wait_send()
        for cp in self._local(ins, outs, sems):
            cp.wait()


def _carry_call(body, args, *, name, grid, in_specs, out_specs, out_shape, scratch_shapes=(), vmem=VMEM_SMALL,
                semantics=None, carried=None):
    if carried is None:
        outs = pl.pallas_call(body, name=name, grid=grid, in_specs=in_specs, out_specs=out_specs, out_shape=out_shape,
                              scratch_shapes=list(scratch_shapes), compiler_params=_params(semantics, vmem))(*args)
        return list(outs), []
    ni, no, ns, nc = len(in_specs), len(out_specs), len(scratch_shapes), carried.n

    def full_body(*refs):
        ins, cin = refs[:ni], refs[ni:ni + nc]
        outs, cout = refs[ni + nc:ni + nc + no], refs[ni + nc + no:ni + 2 * nc + no]
        scratch, sems = refs[ni + 2 * nc + no:ni + 2 * nc + no + ns], refs[ni + 2 * nc + no + ns:]
        ids = [pl.program_id(d) for d in range(len(grid))]
        first, last = ids[0] == 0, ids[0] == grid[0] - 1
        for d in range(1, len(grid)):
            first, last = first & (ids[d] == 0), last & (ids[d] == grid[d] - 1)

        @pl.when(first)
        def _():
            carried.start(cin, cout, sems)

        body(*ins, *outs, *scratch)

        @pl.when(last)
        def _():
            carried.wait(cin, cout, sems)

    outs = pl.pallas_call(
        full_body, name=name, grid=grid, in_specs=list(in_specs) + [ANY_SPEC] * nc,
        out_specs=list(out_specs) + [ANY_SPEC] * nc, out_shape=list(out_shape) + carried.out_shape,
        scratch_shapes=list(scratch_shapes) + carried.scratch,
        compiler_params=_params(("arbitrary",) * len(grid), vmem))(*args, *carried.sources)
    return list(outs[:no]), list(outs[no:])


def _mm(a, b, *, name, ta=False, tb=False, out_dtypes=(F32,), extras=(), epilogue=None,
        tm=1024, tn=1024, tk=1024, carried=None):
    M, K = (a.shape[1], a.shape[0]) if ta else a.shape
    N = b.shape[0] if tb else b.shape[1]
    assert K == (b.shape[1] if tb else b.shape[0])
    tm = _pick(M, (tm, 512, 256, 128))
    tn = _pick(N, (tn, 1536, 512, 384, 256, 128))
    tk = _pick(K, (tk, 1536, 512, 256, 128))
    nk = K // tk
    ne, no = len(extras), len(out_dtypes)
    dims = (((0 if ta else 1,), (1 if tb else 0,)), ((), ()))
    if epilogue is None:
        epilogue = lambda acc: (acc,)

    def body(a_ref, b_ref, *rest):
        extra_refs, out_refs = rest[:ne], rest[ne:ne + no]

        def finish(acc):
            outs = epilogue(acc, *[r[...] for r in extra_refs])
            for o_ref, o in zip(out_refs, outs):
                o_ref[...] = o.astype(o_ref.dtype)

        part = _dot(a_ref[...], b_ref[...], dims)
        if nk == 1:
            finish(part)
        else:
            acc_ref = rest[-1]
            k = pl.program_id(2)

            @pl.when(k == 0)
            def _():
                acc_ref[...] = part

            @pl.when(k > 0)
            def _():
                acc_ref[...] += part

            @pl.when(k == nk - 1)
            def _():
                finish(acc_ref[...])

    a_spec = pl.BlockSpec((tk, tm), lambda i, j, k: (k, i)) if ta else pl.BlockSpec((tm, tk), lambda i, j, k: (i, k))
    b_spec = pl.BlockSpec((tn, tk), lambda i, j, k: (j, k)) if tb else pl.BlockSpec((tk, tn), lambda i, j, k: (k, j))
    o_spec = pl.BlockSpec((tm, tn), lambda i, j, k: (i, j))
    outs, landed = _carry_call(
        body, (a, b, *extras), name=name, grid=(M // tm, N // tn, nk),
        in_specs=[a_spec, b_spec] + [o_spec] * ne,
        out_specs=[o_spec] * no,
        out_shape=[jax.ShapeDtypeStruct((M, N), dt) for dt in out_dtypes],
        scratch_shapes=[pltpu.VMEM((tm, tn), F32)] if nk > 1 else [],
        semantics=("parallel", "parallel", "arbitrary"), carried=carried)
    outs = outs[0] if no == 1 else outs
    return outs if carried is None else (outs, landed)


def _rms_fwd(x, g, name):
    S, Dm = x.shape
    tr = _pick(S, (512,))

    def body(x_ref, g_ref, o_ref):
        xv = x_ref[...]
        r = lax.rsqrt(jnp.mean(xv * xv, axis=-1, keepdims=True) + EPS)
        o_ref[...] = (xv * r * g_ref[...]).astype(o_ref.dtype)

    row = pl.BlockSpec((tr, Dm), lambda i: (i, 0))
    return pl.pallas_call(
        body, name=name, grid=(S // tr,),
        in_specs=[row, pl.BlockSpec((1, Dm), lambda i: (0, 0))], out_specs=row,
        out_shape=jax.ShapeDtypeStruct((S, Dm), BF), compiler_params=_params(("parallel",)),
    )(x, g)


def _rms_bwd(x, g, dh, dres, name):
    S, Dm = x.shape
    tr = _pick(S, (512,))

    def body(x_ref, g_ref, dh_ref, dres_ref, dx_ref, dxb_ref, dg_ref):
        xv = x_ref[...]
        r = lax.rsqrt(jnp.mean(xv * xv, axis=-1, keepdims=True) + EPS)
        xh = xv * r
        dy = dh_ref[...].astype(F32)
        dxh = dy * g_ref[...]
        dx = r * (dxh - xh * jnp.mean(dxh * xh, axis=-1, keepdims=True)) + dres_ref[...]
        dx_ref[...] = dx
        dxb_ref[...] = dx.astype(dxb_ref.dtype)

        @pl.when(pl.program_id(0) == 0)
        def _():
            dg_ref[...] = jnp.zeros_like(dg_ref)

        dg_ref[...] += _row_sum(dy * xh)

    row = pl.BlockSpec((tr, Dm), lambda i: (i, 0))
    vec = pl.BlockSpec((1, Dm), lambda i: (0, 0))
    return pl.pallas_call(
        body, name=name, grid=(S // tr,),
        in_specs=[row, vec, row, row], out_specs=[row, row, vec],
        out_shape=[jax.ShapeDtypeStruct((S, Dm), F32), jax.ShapeDtypeStruct((S, Dm), BF),
                   jax.ShapeDtypeStruct((1, Dm), F32)],
        compiler_params=_params(("arbitrary",)),
    )(x, g, dh, dres)


def _loss_and_grad(y, target):
    S, Dm = y.shape
    tr = _pick(S, (512,))
    n = S // tr

    def body(y_ref, t_ref, dy_ref, dyb_ref, loss_ref, acc_ref):
        i = pl.program_id(0)
        e = y_ref[...] - t_ref[...]
        d = e * (1.0 / Dm)
        dy_ref[...] = d
        dyb_ref[...] = d.astype(dyb_ref.dtype)

        @pl.when(i == 0)
        def _():
            acc_ref[...] = jnp.zeros_like(acc_ref)

        acc_ref[...] += _row_sum(e * e)

        @pl.when(i == n - 1)
        def _():
            loss_ref[...] = jnp.broadcast_to(_all_sum(acc_ref[...]) * (0.5 / Dm), loss_ref.shape)

    row = pl.BlockSpec((tr, Dm), lambda i: (i, 0))
    return pl.pallas_call(
        body, name="loss_head", grid=(n,),
        in_specs=[row, row], out_specs=[row, row, pl.BlockSpec((8, 128), lambda i: (0, 0))],
        out_shape=[jax.ShapeDtypeStruct((S, Dm), F32), jax.ShapeDtypeStruct((S, Dm), BF),
                   jax.ShapeDtypeStruct((8, 128), F32)],
        scratch_shapes=[pltpu.VMEM((1, Dm), F32)],
        compiler_params=_params(("arbitrary",)),
    )(y, target)


def _partner(x, half):
    if half == 64:
        return pltpu.roll(x, 64, 1)
    lane = lax.broadcasted_iota(jnp.int32, x.shape, 1)
    return jnp.where((lane % (2 * half)) < half, pltpu.roll(x, 128 - half, 1), pltpu.roll(x, half, 1))


def _rope(x, cos, sin, half):
    return x * cos + _partner(x, half) * sin


def _rope_t(dy, cos, sin, half):
    return dy * cos - _partner(dy, half) * sin


def _head_norm(x):
    r = lax.rsqrt(jnp.mean(x * x, axis=-1, keepdims=True) + EPS)
    return x * r, r


def _head_norm_bwd(dxh, xh, r):
    return r * (dxh - xh * jnp.mean(dxh * xh, axis=-1, keepdims=True))


def _cols(ref, off, width=HEAD_DIM):
    return ref[:, off:off + width]


def _prep_even_fwd(proj, cos, sin, gq, gk):
    S = proj.shape[0]
    tr = _pick(S, (256,))

    def body(p_ref, cos_ref, sin_ref, gq_ref, gk_ref, qr_ref, kr_ref, vr_ref, qs_ref, ks_ref, vs_ref):
        cos_v, sin_v = cos_ref[...], sin_ref[...]
        for h in range(RET_HEADS):
            o = h * RET_DK
            qr_ref[:, o:o + RET_DK] = _rope(_cols(p_ref, OFF_QA + o), cos_v, sin_v, 64).astype(qr_ref.dtype)
            kr_ref[:, o:o + RET_DK] = (_rope(_cols(p_ref, OFF_KA + o), cos_v, sin_v, 64) * RET_SCALE).astype(kr_ref.dtype)
        vr_ref[...] = p_ref[:, OFF_VA:OFF_VA + 1024].astype(vr_ref.dtype)
        for h in range(SWA_HEADS):
            o = h * HEAD_DIM
            xh, _ = _head_norm(_cols(p_ref, OFF_QB + o))
            qs_ref[:, o:o + HEAD_DIM] = (xh * gq_ref[...] * ATT_SCALE).astype(qs_ref.dtype)
        for h in range(SWA_KV_HEADS):
            o = h * HEAD_DIM
            xh, _ = _head_norm(_cols(p_ref, OFF_KB + o))
            ks_ref[:, o:o + HEAD_DIM] = (xh * gk_ref[...]).astype(ks_ref.dtype)
        vs_ref[...] = p_ref[:, OFF_VB:OFF_VB + 256].astype(vs_ref.dtype)

    def row(w):
        return pl.BlockSpec((tr, w), lambda i: (i, 0))

    vec = pl.BlockSpec((1, HEAD_DIM), lambda i: (0, 0))
    widths = (512, 512, 1024, 1024, 256, 256)
    return pl.pallas_call(
        body, name="prep_even_fwd", grid=(S // tr,),
        in_specs=[row(EVEN_IN), row(128), row(128), vec, vec],
        out_specs=[row(w) for w in widths],
        out_shape=[jax.ShapeDtypeStruct((S, w), BF) for w in widths],
        compiler_params=_params(("parallel",)),
    )(proj, cos, sin, gq, gk)


def _prep_even_bwd(proj, cos, sin, gq, gk, dqr, dkr, dvr, dga, dqs, dks, dvs):
    S = proj.shape[0]
    tr = _pick(S, (256,))

    def body(p_ref, cos_ref, sin_ref, gq_ref, gk_ref, dqr_ref, dkr_ref, dvr_ref, dga_ref, dqs_ref, dks_ref,
             dvs_ref, dp_ref, dgq_ref, dgk_ref):
        cos_v, sin_v = cos_ref[...], sin_ref[...]
        dt = dp_ref.dtype
        for h in range(RET_HEADS):
            o = h * RET_DK
            dp_ref[:, OFF_QA + o:OFF_QA + o + RET_DK] = _rope_t(_cols(dqr_ref, o).astype(F32), cos_v, sin_v, 64).astype(dt)
            dp_ref[:, OFF_KA + o:OFF_KA + o + RET_DK] = _rope_t(_cols(dkr_ref, o).astype(F32) * RET_SCALE, cos_v, sin_v, 64).astype(dt)
        dp_ref[:, OFF_VA:OFF_VA + 1024] = dvr_ref[...].astype(dt)
        dp_ref[:, OFF_GA:OFF_GA + 1024] = dga_ref[...].astype(dt)
        dgq = jnp.zeros((1, HEAD_DIM), F32)
        for h in range(SWA_HEADS):
            o = h * HEAD_DIM
            xh, r = _head_norm(_cols(p_ref, OFF_QB + o))
            dy = _cols(dqs_ref, o).astype(F32) * ATT_SCALE
            dgq = dgq + _row_sum(dy * xh)
            dp_ref[:, OFF_QB + o:OFF_QB + o + HEAD_DIM] = _head_norm_bwd(dy * gq_ref[...], xh, r).astype(dt)
        dgk = jnp.zeros((1, HEAD_DIM), F32)
        for h in range(SWA_KV_HEADS):
            o = h * HEAD_DIM
            xh, r = _head_norm(_cols(p_ref, OFF_KB + o))
            dy = _cols(dks_ref, o)
            dgk = dgk + _row_sum(dy * xh)
            dp_ref[:, OFF_KB + o:OFF_KB + o + HEAD_DIM] = _head_norm_bwd(dy * gk_ref[...], xh, r).astype(dt)
        dp_ref[:, OFF_VB:OFF_VB + 256] = dvs_ref[...].astype(dt)

        @pl.when(pl.program_id(0) == 0)
        def _():
            dgq_ref[...] = jnp.zeros_like(dgq_ref)
            dgk_ref[...] = jnp.zeros_like(dgk_ref)

        dgq_ref[...] += dgq
        dgk_ref[...] += dgk

    def row(w):
        return pl.BlockSpec((tr, w), lambda i: (i, 0))

    vec = pl.BlockSpec((1, HEAD_DIM), lambda i: (0, 0))
    return pl.pallas_call(
        body, name="prep_even_bwd", grid=(S // tr,),
        in_specs=[row(EVEN_IN), row(128), row(128), vec, vec, row(512), row(512), row(1024), row(1024),
                  row(1024), row(256), row(256)],
        out_specs=[row(EVEN_IN), vec, vec],
        out_shape=[jax.ShapeDtypeStruct((S, EVEN_IN), BF), jax.ShapeDtypeStruct((1, HEAD_DIM), F32),
                   jax.ShapeDtypeStruct((1, HEAD_DIM), F32)],
        compiler_params=_params(("arbitrary",)),
    )(proj, cos, sin, gq, gk, dqr, dkr, dvr, dga, dqs, dks, dvs)


RET_CHUNK = 512
def _log_sigmoid_tile(logit_tile):
    def body(x_ref, o_ref):
        xv = x_ref[...]
        t = jnp.exp(-jnp.abs(xv))
        log1p_t = jnp.where(t < 1e-3, t * (1.0 - 0.5 * t), jnp.log(1.0 + t))
        o_ref[...] = jnp.minimum(xv, 0.0) - log1p_t

    full = pl.BlockSpec((8, 128), lambda: (0, 0))
    return pl.pallas_call(body, name="log_sigmoid", in_specs=[full], out_specs=full,
                          out_shape=jax.ShapeDtypeStruct((8, 128), F32))(logit_tile)


def _decay(diff, lf, lb):
    return jnp.exp(jnp.where(diff >= 0, lf * diff, -(lb * diff)))


def _col_iota(n):
    return lax.broadcasted_iota(jnp.int32, (n, 1), 0).astype(F32)


def _ret_scan(x, z, lg, asc, desc, name):
    S = x.shape[0]
    C = _pick(S, (RET_CHUNK,))
    nc = S // C
    (arow, aoff), (drow, doff) = asc, desc

    def body(lg_ref, xa_ref, za_ref, xd_ref, zd_ref, asc_ref, desc_ref, sa_ref, sd_ref):
        h, t = pl.program_id(0), pl.program_id(1)
        la, ld = lg_ref[arow, h], lg_ref[drow, h]

        @pl.when(t == 0)
        def _():
            sa_ref[...] = jnp.zeros_like(sa_ref)
            sd_ref[...] = jnp.zeros_like(sd_ref)

        asc_ref[0, 0] = sa_ref[...]
        desc_ref[0, 0] = sd_ref[...]
        j = _col_iota(C)
        xa = (xa_ref[...].astype(F32) * jnp.exp(la * (C - 1 + aoff - j))).astype(xa_ref.dtype)
        xd = (xd_ref[...].astype(F32) * jnp.exp(ld * (j + doff))).astype(xd_ref.dtype)
        sa_ref[...] = jnp.exp(jnp.full((1, RET_DV), la * C, F32)) * sa_ref[...] + _dot(xa, za_ref[...], TN)
        sd_ref[...] = jnp.exp(jnp.full((1, RET_DV), ld * C, F32)) * sd_ref[...] + _dot(xd, zd_ref[...], TN)

    state = jax.ShapeDtypeStruct((RET_HEADS, nc, RET_DK, RET_DV), F32)
    return pl.pallas_call(
        body, name=name, grid=(RET_HEADS, nc),
        in_specs=[SMEM_SPEC,
                  pl.BlockSpec((C, RET_DK), lambda h, t: (t, h)), pl.BlockSpec((C, RET_DV), lambda h, t: (t, h)),
                  pl.BlockSpec((C, RET_DK), lambda h, t: (nc - 1 - t, h)), pl.BlockSpec((C, RET_DV), lambda h, t: (nc - 1 - t, h))],
        out_specs=[pl.BlockSpec((1, 1, RET_DK, RET_DV), lambda h, t: (h, t, 0, 0)),
                   pl.BlockSpec((1, 1, RET_DK, RET_DV), lambda h, t: (h, nc - 1 - t, 0, 0))],
        out_shape=[state, state],
        scratch_shapes=[pltpu.VMEM((RET_DK, RET_DV), F32), pltpu.VMEM((RET_DK, RET_DV), F32)],
        compiler_params=_params(("parallel", "arbitrary")),
    )(lg, x, z, x, z)


def _ret_fwd(q, k, v, lg, sf, sb):
    S = q.shape[0]
    C = _pick(S, (RET_CHUNK,))

    def body(lg_ref, q_ref, k_ref, v_ref, sf_ref, sb_ref, y_ref):
        h = pl.program_id(0)
        lf, lb = lg_ref[0, h], lg_ref[1, h]
        qv = q_ref[...]
        dt = qv.dtype
        diff = (lax.broadcasted_iota(jnp.int32, (C, C), 0) - lax.broadcasted_iota(jnp.int32, (C, C), 1)).astype(F32)
        y = _dot((_dot(qv, k_ref[...], NT) * _decay(diff, lf, lb)).astype(dt), v_ref[...])
        r = _col_iota(C)
        qf = qv.astype(F32)
        y = y + _dot((qf * jnp.exp(lf * (r + 1.0))).astype(dt), sf_ref[0, 0].astype(dt))
        y_ref[...] = y + _dot((qf * jnp.exp(lb * (C - r))).astype(dt), sb_ref[0, 0].astype(dt))

    state = pl.BlockSpec((1, 1, RET_DK, RET_DV), lambda h, c: (h, c, 0, 0))
    qk = pl.BlockSpec((C, RET_DK), lambda h, c: (c, h))
    vy = pl.BlockSpec((C, RET_DV), lambda h, c: (c, h))
    return pl.pallas_call(
        body, name="retention_fwd", grid=(RET_HEADS, S // C),
        in_specs=[SMEM_SPEC, qk, qk, vy, state, state], out_specs=vy,
        out_shape=jax.ShapeDtypeStruct((S, RET_HEADS * RET_DV), F32),
        compiler_params=_params(("parallel", "parallel")),
    )(lg, q, k, v, sf, sb)


def _ret_bwd(q, k, v, dy, lg, logit, sf, sb, hf, hb):
    S = q.shape[0]
    C = _pick(S, (RET_CHUNK,))
    nc = S // C

    def body(lg_ref, logit_ref, q_ref, k_ref, v_ref, dy_ref, sf_ref, sb_ref, hf_ref, hb_ref,
             dq_ref, dk_ref, dv_ref, dlg_ref, acc_ref):
        h, c = pl.program_id(0), pl.program_id(1)
        lf, lb = lg_ref[0, h], lg_ref[1, h]

        @pl.when(c == 0)
        def _():
            acc_ref[...] = jnp.zeros_like(acc_ref)

        qv, kv, vv, dyv = q_ref[...], k_ref[...], v_ref[...], dy_ref[...]
        dt = qv.dtype
        qf, kf = qv.astype(F32), kv.astype(F32)
        diff = (lax.broadcasted_iota(jnp.int32, (C, C), 0) - lax.broadcasted_iota(jnp.int32, (C, C), 1)).astype(F32)
        dec = _decay(diff, lf, lb)
        sc = _dot(qv, kv, NT) * dec
        dp = _dot(dyv, vv, NT)
        da = (dp * dec).astype(dt)
        dq = _dot(da, kv)
        dk = _dot(da, qv, TN)
        dv = _dot(sc.astype(dt), dyv, TN)
        w = sc * dp * diff
        tot_w, tot_f = _all_sum(w), _all_sum(jnp.where(diff >= 0, w, 0.0))
        d_lf, d_lb = tot_f, tot_f - tot_w
        r = _col_iota(C)
        a, b = jnp.exp(lf * (r + 1.0)), jnp.exp(lb * (C - r))
        e, f = jnp.exp(lf * (C - 1.0 - r)), jnp.exp(lb * r)
        sfv, sbv, hfv, hbv = sf_ref[0, 0], sb_ref[0, 0], hf_ref[0, 0], hb_ref[0, 0]
        t_f, t_b = _dot(dyv, sfv.astype(dt), NT), _dot(dyv, sbv.astype(dt), NT)
        u_f, u_b = _dot(vv, hfv.astype(dt), NT), _dot(vv, hbv.astype(dt), NT)
        dq_ref[...] = (dq + a * t_f + b * t_b).astype(dq_ref.dtype)
        dk_ref[...] = (dk + e * u_f + f * u_b).astype(dk_ref.dtype)
        dv_ref[...] = (dv + _dot((kf * e).astype(dt), hfv.astype(dt)) + _dot((kf * f).astype(dt), hbv.astype(dt))).astype(dv_ref.dtype)
        row_q_f = jnp.sum(qf * t_f, axis=-1, keepdims=True)
        row_q_b = jnp.sum(qf * t_b, axis=-1, keepdims=True)
        row_k_f = jnp.sum(kf * u_f, axis=-1, keepdims=True)
        row_k_b = jnp.sum(kf * u_b, axis=-1, keepdims=True)
        gf_c = jnp.exp(jnp.full((1, 1), lf * C, F32))
        gb_c = jnp.exp(jnp.full((1, 1), lb * C, F32))
        d_lf = d_lf + _all_sum((r + 1.0) * a * row_q_f + (C - 1.0 - r) * e * row_k_f) + C * gf_c * _all_sum(hfv * sfv)
        d_lb = d_lb + _all_sum((C - r) * b * row_q_b + r * f * row_k_b) + C * gb_c * _all_sum(hbv * sbv)
        acc_ref[0] += jnp.broadcast_to(d_lf, (8, 128))
        acc_ref[1] += jnp.broadcast_to(d_lb, (8, 128))

        @pl.when(c == nc - 1)
        def _():
            for d in range(2):
                gate = 1.0 / (1.0 + jnp.exp(jnp.full((8, 128), logit_ref[d, h], F32)))
                dlg_ref[0, d] = acc_ref[d] * gate

    state = pl.BlockSpec((1, 1, RET_DK, RET_DV), lambda h, c: (h, c, 0, 0))
    qk = pl.BlockSpec((C, RET_DK), lambda h, c: (c, h))
    vy = pl.BlockSpec((C, RET_DV), lambda h, c: (c, h))
    return pl.pallas_call(
        body, name="retention_bwd", grid=(RET_HEADS, nc),
        in_specs=[SMEM_SPEC, SMEM_SPEC, qk, qk, vy, vy, state, state, state, state],
        out_specs=[qk, qk, vy, pl.BlockSpec((1, 2, 8, 128), lambda h, c: (h, 0, 0, 0))],
        out_shape=[jax.ShapeDtypeStruct((S, RET_HEADS * RET_DK), BF), jax.ShapeDtypeStruct((S, RET_HEADS * RET_DK), BF),
                   jax.ShapeDtypeStruct((S, RET_HEADS * RET_DV), BF),
                   jax.ShapeDtypeStruct((RET_HEADS, 2, 8, 128), F32)],
        scratch_shapes=[pltpu.VMEM((2, 8, 128), F32)],
        compiler_params=_params(("parallel", "arbitrary")),
    )(lg, logit, q, k, v, dy, sf, sb, hf, hb)


def _ret_post_fwd(y, proj, gn):
    S = y.shape[0]
    tr = _pick(S, (512,))

    def body(y_ref, g_ref, gn_ref, o_ref):
        for h in range(RET_HEADS):
            o = h * RET_DV
            yh, _ = _head_norm(_cols(y_ref, o, RET_DV))
            gate = _cols(g_ref, o, RET_DV)
            o_ref[:, o:o + RET_DV] = (gate * _sigmoid(gate) * (yh * gn_ref[:, o:o + RET_DV])).astype(o_ref.dtype)

    row = pl.BlockSpec((tr, 1024), lambda i: (i, 0))
    return pl.pallas_call(
        body, name="retention_post_fwd", grid=(S // tr,),
        in_specs=[row, pl.BlockSpec((tr, 1024), lambda i: (i, OFF_GA // 1024)), pl.BlockSpec((1, 1024), lambda i: (0, 0))],
        out_specs=row, out_shape=jax.ShapeDtypeStruct((S, 1024), BF), compiler_params=_params(("parallel",)),
    )(y, proj, gn)


def _ret_post_bwd(y, proj, gn, do):
    S = y.shape[0]
    tr = _pick(S, (512,))

    def body(y_ref, g_ref, gn_ref, do_ref, dy_ref, dg_ref, dgn_ref):
        @pl.when(pl.program_id(0) == 0)
        def _():
            dgn_ref[...] = jnp.zeros_like(dgn_ref)

        for h in range(RET_HEADS):
            o = h * RET_DV
            yh, r = _head_norm(_cols(y_ref, o, RET_DV))
            gate = _cols(g_ref, o, RET_DV)
            gnh = gn_ref[:, o:o + RET_DV]
            dout = _cols(do_ref, o, RET_DV).astype(F32)
            sg = _sigmoid(gate)
            dz = dout * (gate * sg)
            dg_ref[:, o:o + RET_DV] = (dout * (yh * gnh) * (sg * (1.0 + gate * (1.0 - sg)))).astype(dg_ref.dtype)
            dgn_ref[:, o:o + RET_DV] += _row_sum(dz * yh)
            dy_ref[:, o:o + RET_DV] = _head_norm_bwd(dz * gnh, yh, r).astype(dy_ref.dtype)

    row = pl.BlockSpec((tr, 1024), lambda i: (i, 0))
    vec = pl.BlockSpec((1, 1024), lambda i: (0, 0))
    return pl.pallas_call(
        body, name="retention_post_bwd", grid=(S // tr,),
        in_specs=[row, pl.BlockSpec((tr, 1024), lambda i: (i, OFF_GA // 1024)), vec, row],
        out_specs=[row, row, vec],
        out_shape=[jax.ShapeDtypeStruct((S, 1024), BF), jax.ShapeDtypeStruct((S, 1024), BF),
                   jax.ShapeDtypeStruct((1, 1024), F32)],
        compiler_params=_params(("arbitrary",)),
    )(y, proj, gn, do)


def _t5_bucket_map():
    r = jnp.arange(BLOCK)
    j = jnp.arange(3 * BLOCK)
    rel = j[None, :] - BLOCK - r[:, None]
    nb = T5_BUCKETS // 2
    max_exact = nb // 2
    ret = jnp.where(rel > 0, nb, 0)
    n = jnp.abs(rel)
    nf = jnp.maximum(n, 1).astype(jnp.float32)
    large = max_exact + (jnp.log(nf / max_exact) / math.log(T5_MAX_DIST / max_exact)
                         * (nb - max_exact)).astype(jnp.int32)
    large = jnp.minimum(large, nb - 1)
    bucket = ret + jnp.where(n < max_exact, n, large)
    return jnp.where(jnp.abs(rel) <= WINDOW, bucket, -1).astype(jnp.int32)


SWA_G = SWA_HEADS // SWA_KV_HEADS
SWA_LANES = SWA_G * BLOCK


def _t5_bias(table, bucket_t):
    def body(t_ref, b_ref, o_ref):
        bk = b_ref[...]
        for h in range(SWA_HEADS):
            acc = jnp.full(bk.shape, NEG_INF, F32)
            for b in range(T5_BUCKETS):
                acc = jnp.where(bk == b, t_ref[b, h], acc)
            o_ref[h // SWA_G, :, (h % SWA_G) * BLOCK:(h % SWA_G + 1) * BLOCK] = acc

    return pl.pallas_call(
        body, name="t5_bias", in_specs=[SMEM_SPEC, pl.BlockSpec((3 * BLOCK, BLOCK), lambda: (0, 0))],
        out_specs=pl.BlockSpec((SWA_KV_HEADS, 3 * BLOCK, SWA_LANES), lambda: (0, 0, 0)),
        out_shape=jax.ShapeDtypeStruct((SWA_KV_HEADS, 3 * BLOCK, SWA_LANES), F32),
    )(table, bucket_t)


def _t5_table_grad(dbias, bucket_t):
    def body(d_ref, b_ref, o_ref):
        bk = b_ref[...]
        lane = lax.broadcasted_iota(jnp.int32, (1, 128), 1)
        for b in range(T5_BUCKETS):
            hit = bk == b
            row = jnp.zeros((1, 128), F32)
            for h in range(SWA_HEADS):
                d = d_ref[h // SWA_G, :, (h % SWA_G) * BLOCK:(h % SWA_G + 1) * BLOCK]
                row = row + jnp.where(lane == h, _all_sum(jnp.where(hit, d, 0.0)), 0.0)
            o_ref[b:b + 1, :] = row

    return pl.pallas_call(
        body, name="t5_table_grad",
        in_specs=[pl.BlockSpec((SWA_KV_HEADS, 3 * BLOCK, SWA_LANES), lambda: (0, 0, 0)),
                  pl.BlockSpec((3 * BLOCK, BLOCK), lambda: (0, 0))],
        out_specs=pl.BlockSpec((T5_BUCKETS, 128), lambda: (0, 0)),
        out_shape=jax.ShapeDtypeStruct((T5_BUCKETS, 128), F32),
    )(dbias, bucket_t)


def _swa_scores(i, nb, q4, kw, bias_t, sink_row):
    s = _dot(kw, q4, NT) + bias_t
    row = lax.broadcasted_iota(jnp.int32, s.shape, 0)
    first_row = jnp.where(i == 0, BLOCK, 0)
    end_row = jnp.where(i == nb - 1, 2 * BLOCK, 3 * BLOCK)
    s = jnp.where((row < first_row) | (row >= end_row), NEG_INF, s)
    m = jnp.maximum(jnp.max(s, axis=0, keepdims=True), sink_row)
    p = jnp.exp(s - m)
    e_sink = jnp.exp(sink_row - m)
    inv = 1.0 / (jnp.sum(p, axis=0, keepdims=True) + e_sink)
    return p * inv, e_sink * inv


def _swa_group(q_ref, sink_ref, kh):
    heads = range(kh * SWA_G, (kh + 1) * SWA_G)
    q4 = jnp.concatenate([_cols(q_ref, h * HEAD_DIM) for h in heads], axis=0)
    sink_row = jnp.concatenate([jnp.full((1, BLOCK), sink_ref[0, h], F32) for h in heads], axis=1)
    return q4, sink_row


def _swa_unstack(ref, kh, x_t):
    for g in range(SWA_G):
        h = kh * SWA_G + g
        ref[:, h * HEAD_DIM:(h + 1) * HEAD_DIM] = x_t[:, g * BLOCK:(g + 1) * BLOCK].T.astype(ref.dtype)


def _swa_window(ref, i, nb, off):
    prev, nxt = jnp.maximum(i - 1, 0), jnp.minimum(i + 1, nb - 1)
    rows = [pl.ds(pl.multiple_of(b * BLOCK, BLOCK), BLOCK) for b in (prev, i, nxt)]
    return jnp.concatenate([ref[r, off:off + HEAD_DIM] for r in rows], axis=0), rows


def _swa_fwd(q, k, v, bias, sink, carried=None):
    S = q.shape[0]
    nb = S // BLOCK

    def body(sink_ref, q_ref, k_ref, v_ref, bias_ref, o_ref):
        i = pl.program_id(0)
        for kh in range(SWA_KV_HEADS):
            kw, _ = _swa_window(k_ref, i, nb, kh * HEAD_DIM)
            vw, _ = _swa_window(v_ref, i, nb, kh * HEAD_DIM)
            q4, sink_row = _swa_group(q_ref, sink_ref, kh)
            p, _ = _swa_scores(i, nb, q4, kw, bias_ref[kh], sink_row)
            _swa_unstack(o_ref, kh, _dot(vw, p.astype(vw.dtype), TN))

    full_kv = pl.BlockSpec((S, SWA_KV_HEADS * HEAD_DIM), lambda i: (0, 0))
    (o,), landed = _carry_call(
        body, (sink, q, k, v, bias), name="swa_fwd", grid=(nb,),
        in_specs=[SMEM_SPEC, pl.BlockSpec((BLOCK, 1024), lambda i: (i, 0)), full_kv, full_kv,
                  pl.BlockSpec((SWA_KV_HEADS, 3 * BLOCK, SWA_LANES), lambda i: (0, 0, 0))],
        out_specs=[pl.BlockSpec((BLOCK, 1024), lambda i: (i, 0))],
        out_shape=[jax.ShapeDtypeStruct((S, 1024), BF)], semantics=("parallel",), carried=carried)
    return o, landed


def _swa_bwd(q, k, v, do, bias, sink, carried=None):
    S = q.shape[0]
    nb = S // BLOCK

    def body(sink_ref, q_ref, k_ref, v_ref, do_ref, bias_ref, dq_ref, dk_ref, dv_ref, dbias_ref, dsink_ref):
        i = pl.program_id(0)

        @pl.when(i == 0)
        def _():
            dk_ref[...] = jnp.zeros_like(dk_ref)
            dv_ref[...] = jnp.zeros_like(dv_ref)
            dbias_ref[...] = jnp.zeros_like(dbias_ref)
            dsink_ref[...] = jnp.zeros_like(dsink_ref)

        for kh in range(SWA_KV_HEADS):
            off = kh * HEAD_DIM
            kw, rows = _swa_window(k_ref, i, nb, off)
            vw, _ = _swa_window(v_ref, i, nb, off)
            q4, sink_row = _swa_group(q_ref, sink_ref, kh)
            p, p_sink = _swa_scores(i, nb, q4, kw, bias_ref[kh], sink_row)
            do4 = jnp.concatenate([_cols(do_ref, (kh * SWA_G + g) * HEAD_DIM) for g in range(SWA_G)], axis=0).astype(vw.dtype)
            dp = _dot(vw, do4, NT)
            delta = jnp.sum(p * dp, axis=0, keepdims=True)
            ds = p * (dp - delta)
            dsb = ds.astype(q4.dtype)
            _swa_unstack(dq_ref, kh, _dot(kw, dsb, TN))
            dkw = _dot(dsb, q4)
            dvw = _dot(p.astype(do4.dtype), do4)
            dbias_ref[kh] += ds
            sink_term = p_sink * delta
            for g in range(SWA_G):
                h = kh * SWA_G + g
                dsink_ref[h:h + 1, :] += jnp.broadcast_to(-_all_sum(sink_term[:, g * BLOCK:(g + 1) * BLOCK]), (1, 128))
            for b, r in enumerate(rows):
                dk_ref[r, off:off + HEAD_DIM] += dkw[b * BLOCK:(b + 1) * BLOCK]
                dv_ref[r, off:off + HEAD_DIM] += dvw[b * BLOCK:(b + 1) * BLOCK]

    full_kv = pl.BlockSpec((S, SWA_KV_HEADS * HEAD_DIM), lambda i: (0, 0))
    blk = pl.BlockSpec((BLOCK, 1024), lambda i: (i, 0))
    bias_spec = pl.BlockSpec((SWA_KV_HEADS, 3 * BLOCK, SWA_LANES), lambda i: (0, 0, 0))
    outs, landed = _carry_call(
        body, (sink, q, k, v, do, bias), name="swa_bwd", grid=(nb,),
        in_specs=[SMEM_SPEC, blk, full_kv, full_kv, blk, bias_spec],
        out_specs=[blk, full_kv, full_kv, bias_spec, pl.BlockSpec((8, 128), lambda i: (0, 0))],
        out_shape=[jax.ShapeDtypeStruct((S, 1024), BF), jax.ShapeDtypeStruct((S, 256), F32),
                   jax.ShapeDtypeStruct((S, 256), F32),
                   jax.ShapeDtypeStruct((SWA_KV_HEADS, 3 * BLOCK, SWA_LANES), F32), jax.ShapeDtypeStruct((8, 128), F32)],
        vmem=VMEM_LARGE, semantics=("arbitrary",), carried=carried)
    return (*outs, landed)


def _prep_odd_fwd(proj, cos, sin, gq, gk):
    S = proj.shape[0]
    tr = _pick(S, (512,))

    def body(p_ref, cos_ref, sin_ref, gq_ref, gk_ref, q_ref, k_ref, v_ref):
        cos_v, sin_v = cos_ref[...], sin_ref[...]
        for h in range(AX_HEADS):
            o = h * HEAD_DIM
            xh, _ = _head_norm(_cols(p_ref, o))
            q_ref[:, o:o + HEAD_DIM] = (_rope(xh * gq_ref[...], cos_v, sin_v, 32) * AX_SCALE).astype(q_ref.dtype)
        for h in range(AX_KV_HEADS):
            o = h * HEAD_DIM
            xh, _ = _head_norm(_cols(p_ref, 1024 + o))
            k_ref[:, o:o + HEAD_DIM] = _rope(xh * gk_ref[...], cos_v, sin_v, 32).astype(k_ref.dtype)
        v_ref[...] = p_ref[:, 1280:1536].astype(v_ref.dtype)

    def row(w):
        return pl.BlockSpec((tr, w), lambda i: (i, 0))

    vec = pl.BlockSpec((1, HEAD_DIM), lambda i: (0, 0))
    return pl.pallas_call(
        body, name="prep_odd_fwd", grid=(S // tr,),
        in_specs=[row(ODD_IN), row(128), row(128), vec, vec], out_specs=[row(1024), row(256), row(256)],
        out_shape=[jax.ShapeDtypeStruct((S, w), BF) for w in (1024, 256, 256)],
        compiler_params=_params(("parallel",)),
    )(proj, cos, sin, gq, gk)


def _prep_odd_bwd(proj, cos, sin, gq, gk, dq, dk, dv):
    S = proj.shape[0]
    tr = _pick(S, (512,))

    def body(p_ref, cos_ref, sin_ref, gq_ref, gk_ref, dq_ref, dk_ref, dv_ref, dp_ref, dgq_ref, dgk_ref):
        cos_v, sin_v = cos_ref[...], sin_ref[...]
        dt = dp_ref.dtype
        dgq = jnp.zeros((1, HEAD_DIM), F32)
        for h in range(AX_HEADS):
            o = h * HEAD_DIM
            xh, r = _head_norm(_cols(p_ref, o))
            dy = _rope_t(_cols(dq_ref, o).astype(F32) * AX_SCALE, cos_v, sin_v, 32)
            dgq = dgq + _row_sum(dy * xh)
            dp_ref[:, o:o + HEAD_DIM] = _head_norm_bwd(dy * gq_ref[...], xh, r).astype(dt)
        dgk = jnp.zeros((1, HEAD_DIM), F32)
        for h in range(AX_KV_HEADS):
            o = h * HEAD_DIM
            xh, r = _head_norm(_cols(p_ref, 1024 + o))
            dy = _rope_t(_cols(dk_ref, o), cos_v, sin_v, 32)
            dgk = dgk + _row_sum(dy * xh)
            dp_ref[:, 1024 + o:1024 + o + HEAD_DIM] = _head_norm_bwd(dy * gk_ref[...], xh, r).astype(dt)
        dp_ref[:, 1280:1536] = dv_ref[...].astype(dt)

        @pl.when(pl.program_id(0) == 0)
        def _():
            dgq_ref[...] = jnp.zeros_like(dgq_ref)
            dgk_ref[...] = jnp.zeros_like(dgk_ref)

        dgq_ref[...] += dgq
        dgk_ref[...] += dgk

    def row(w):
        return pl.BlockSpec((tr, w), lambda i: (i, 0))

    vec = pl.BlockSpec((1, HEAD_DIM), lambda i: (0, 0))
    return pl.pallas_call(
        body, name="prep_odd_bwd", grid=(S // tr,),
        in_specs=[row(ODD_IN), row(128), row(128), vec, vec, row(1024), row(256), row(256)],
        out_specs=[row(ODD_IN), vec, vec],
        out_shape=[jax.ShapeDtypeStruct((S, ODD_IN), BF), jax.ShapeDtypeStruct((1, HEAD_DIM), F32),
                   jax.ShapeDtypeStruct((1, HEAD_DIM), F32)],
        compiler_params=_params(("arbitrary",)),
    )(proj, cos, sin, gq, gk, dq, dk, dv)


def _loop_unrolled(n, factor, step, init):
    while n % factor:
        factor //= 2

    def trip(t, carry):
        for u in range(factor):
            carry = step(factor * t + u, carry)
        return carry

    return lax.fori_loop(0, n // factor, trip, init)


def _flash_fwd(q, k, v, carried=None):
    S = q.shape[0]
    tq = _pick(S, (512,))
    tk = _pick(S, (1024, 512))
    nk = S // tk
    G = AX_HEADS // AX_KV_HEADS

    def body(q_ref, k_ref, v_ref, o_ref, lse_ref):
        qv = q_ref[...]

        def step(j, carry):
            m, l, acc = carry
            rows = pl.ds(pl.multiple_of(j * tk, tk), tk)
            s = _dot(qv, k_ref[rows, :], NT)
            m_new = jnp.maximum(m, jnp.max(s, axis=-1, keepdims=True))
            alpha = jnp.exp2(m - m_new)
            p = jnp.exp2(s - m_new)
            l = alpha * l + jnp.sum(p, axis=-1, keepdims=True)
            acc = alpha * acc + _dot(p.astype(v_ref.dtype), v_ref[rows, :])
            return m_new, l, acc

        init = (jnp.full((tq, 1), NEG_INF, F32), jnp.zeros((tq, 1), F32), jnp.zeros((tq, HEAD_DIM), F32))
        m, l, acc = _loop_unrolled(nk, 8, step, init)
        o_ref[...] = (acc / l).astype(o_ref.dtype)
        lse_ref[0] = jnp.broadcast_to(m + jnp.log2(l), (tq, 128))

    (o, lse), landed = _carry_call(
        body, (q, k, v), name="flash_fwd", grid=(AX_HEADS, S // tq),
        in_specs=[pl.BlockSpec((tq, HEAD_DIM), lambda h, i: (i, h)),
                  pl.BlockSpec((S, HEAD_DIM), lambda h, i: (0, h // G)),
                  pl.BlockSpec((S, HEAD_DIM), lambda h, i: (0, h // G))],
        out_specs=[pl.BlockSpec((tq, HEAD_DIM), lambda h, i: (i, h)),
                   pl.BlockSpec((1, tq, 128), lambda h, i: (h, i, 0))],
        out_shape=[jax.ShapeDtypeStruct((S, AX_HEADS * HEAD_DIM), BF), jax.ShapeDtypeStruct((AX_HEADS, S, 128), F32)],
        semantics=("parallel", "parallel"), carried=carried)
    return o, lse, landed


def _flash_bwd(q, k, v, o, do, lse, carried=None):
    S = q.shape[0]
    tq = _pick(S, (512,))
    tk = _pick(S, (1024, 512))
    nq, nk = S // tq, S // tk
    G = AX_HEADS // AX_KV_HEADS

    def body(q_ref, k_ref, v_ref, o_ref, do_ref, lse_ref, dq_ref, dk_ref, dv_ref):
        g, i = pl.program_id(1), pl.program_id(2)

        @pl.when((g == 0) & (i == 0))
        def _():
            dk_ref[...] = jnp.zeros_like(dk_ref)
            dv_ref[...] = jnp.zeros_like(dv_ref)

        qv = q_ref[...]
        do_f = do_ref[...].astype(F32)
        dob = do_f.astype(qv.dtype)
        dob_ln2 = (do_f * LN2).astype(qv.dtype)
        delta = jnp.sum(do_f * o_ref[...].astype(F32), axis=-1, keepdims=True) * LN2
        lse_col = lse_ref[0][:, 0:1]

        def step(j, dq):
            rows = pl.ds(pl.multiple_of(j * tk, tk), tk)
            kj, vj = k_ref[rows, :], v_ref[rows, :]
            p = jnp.exp2(_dot(qv, kj, NT) - lse_col)
            dp = _dot(dob_ln2, vj, NT)
            ds = (p * (dp - delta)).astype(qv.dtype)
            dk_ref[rows, :] += _dot(ds, qv, TN)
            dv_ref[rows, :] += _dot(p.astype(dob.dtype), dob, TN)
            return dq + _dot(ds, kj)

        dq_ref[...] = _loop_unrolled(nk, 4, step, jnp.zeros((tq, HEAD_DIM), F32)).astype(dq_ref.dtype)

    q_spec = pl.BlockSpec((tq, HEAD_DIM), lambda kh, g, i: (i, kh * G + g))
    kv_spec = pl.BlockSpec((S, HEAD_DIM), lambda kh, g, i: (0, kh))
    (dq, dk, dv), landed = _carry_call(
        body, (q, k, v, o, do, lse), name="flash_bwd", grid=(AX_KV_HEADS, G, nq),
        in_specs=[q_spec, kv_spec, kv_spec, q_spec, q_spec,
                  pl.BlockSpec((1, tq, 128), lambda kh, g, i: (kh * G + g, i, 0))],
        out_specs=[q_spec, kv_spec, kv_spec],
        out_shape=[jax.ShapeDtypeStruct((S, AX_HEADS * HEAD_DIM), BF), jax.ShapeDtypeStruct((S, 256), F32),
                   jax.ShapeDtypeStruct((S, 256), F32)],
        vmem=VMEM_LARGE, semantics=("arbitrary", "arbitrary", "arbitrary"), carried=carried)
    return dq, dk, dv, landed


def _rope_angles(pos, dim, theta):
    inv = theta ** (-jnp.arange(0, dim, 2, dtype=jnp.float32) / dim)
    return pos.astype(jnp.float32)[:, None] * inv[None, :]


def _rope_tables(S):
    ang = _rope_angles(jnp.arange(S), RET_DK, RET_THETA)
    c, s = jnp.cos(ang), jnp.sin(ang)
    ret = (jnp.concatenate([c, c], -1), jnp.concatenate([-s, s], -1))
    rows = S // GRID_W
    ar, ac = _rope_angles(jnp.arange(rows), HEAD_DIM // 2, AX_THETA), _rope_angles(jnp.arange(GRID_W), HEAD_DIM // 2, AX_THETA)
    cr, sr = jnp.repeat(jnp.cos(ar), GRID_W, axis=0), jnp.repeat(jnp.sin(ar), GRID_W, axis=0)
    cc, sc = jnp.tile(jnp.cos(ac), (rows, 1)), jnp.tile(jnp.sin(ac), (rows, 1))
    ax = (jnp.concatenate([cr, cr, cc, cc], -1), jnp.concatenate([-sr, sr, -sc, sc], -1))
    return ret, ax


def _pad_tile(a):
    return jnp.pad(a.astype(F32), ((0, 8 - a.shape[0]), (0, 128 - a.shape[1])))


def _relu2_epilogue(acc):
    r = jnp.maximum(acc, 0.0)
    return acc, r * r


def _relu2_bwd_epilogue(acc, u):
    return (acc * (2.0 * jnp.maximum(u.astype(F32), 0.0)),)


def _add_epilogue(acc, res):
    return (acc + res,)


def _mlp_fwd(x, g, w_up, w_down, tag):
    h = _rms_fwd(x, g, f"mlp_norm_{tag}")
    u, a = _mm(h, w_up, name=f"mlp_up_{tag}", out_dtypes=(BF, BF), epilogue=_relu2_epilogue)
    y = _mm(a, w_down, name=f"mlp_down_{tag}", extras=(x,), epilogue=_add_epilogue)
    return y, (h, u, a)


def _mlp_bwd(x, g, w_up, w_down, saved, dy, dyb, tag):
    h, u, a = saved
    du = _mm(dyb, w_down, tb=True, name=f"mlp_down_dx_{tag}", out_dtypes=(BF,), extras=(u,), epilogue=_relu2_bwd_epilogue)
    dw_down = _mm(a, dyb, ta=True, name=f"mlp_down_dw_{tag}", out_dtypes=(BF,))
    dw_up = _mm(h, du, ta=True, name=f"mlp_up_dw_{tag}", out_dtypes=(BF,))
    dh = _mm(du, w_up, tb=True, name=f"mlp_up_dx_{tag}")
    dx, dxb, dg = _rms_bwd(x, g, dh, dy, f"mlp_norm_bwd_{tag}")
    return dx, dxb, dg, dw_up, dw_down


COL_SHARDED = ("in_even", "in_odd", "up0", "up1")


def _assemble(key, g):
    if key in COL_SHARDED:
        return g.transpose(1, 0, 2).reshape(g.shape[1], N_DEV * g.shape[2])
    return g.reshape(N_DEV * g.shape[1], g.shape[2])


def _split(key, full):
    rows, cols = full.shape
    if key in COL_SHARDED:
        return full.reshape(rows, N_DEV, cols // N_DEV).transpose(1, 0, 2)
    return full.reshape(N_DEV, rows // N_DEV, cols)


def _local_step(x, target, W, P, late=None):
    S = x.shape[0]
    W = dict(W)
    landed = {}

    def gather_while(stage):
        return None if late is None else _Exchange([s for _, s in late[stage]], gather=True)

    def arrived(stage, outs):
        for (key, _), g in zip([] if late is None else late[stage], outs):
            W[key] = _assemble(key, g)

    def exchange_while(grads):
        return None if late is None else _Exchange([_split(k, g) for k, g in grads], gather=False)

    def left(grads, outs):
        for (key, _), l in zip(grads, outs):
            landed[key] = l

    (cos_r, sin_r), (cos_a, sin_a) = _rope_tables(S)
    bucket = _t5_bucket_map().T
    logit = P["ret_decay_logit"]
    lg = _log_sigmoid_tile(_pad_tile(logit))
    bias = _t5_bias(P["t5_table"], bucket)
    nmix, nmlp = P["norm_mix"], P["norm_mlp"]

    h0 = _rms_fwd(x, nmix[0:1], "mix_norm_0")
    proj_e = _mm(h0, W["in_even"], name="in_even")
    qr, kr, vr, qs, ks, vs = _prep_even_fwd(proj_e, cos_r, sin_r, P["swa_q_norm"], P["swa_k_norm"])
    sf, sb = _ret_scan(kr, vr, lg, (0, 0), (1, 0), "retention_states")
    y_ret = _ret_fwd(qr, kr, vr, lg, sf, sb)
    oa = _ret_post_fwd(y_ret, proj_e, P["ret_norm"])
    ob, outs = _swa_fwd(qs, ks, vs, bias, P["swa_sink"], gather_while("swa_fwd"))
    arrived("swa_fwd", outs)
    wo_a, wo_b = W["out_even"][:1024], W["out_even"][1024:]
    x1 = _mm(oa, wo_a, name="out_even_a", extras=(x,), epilogue=_add_epilogue)
    x1 = _mm(ob, wo_b, name="out_even_b", extras=(x1,), epilogue=_add_epilogue)
    x2, mlp0 = _mlp_fwd(x1, nmlp[0:1], W["up0"], W["down0"], "0")

    h2 = _rms_fwd(x2, nmix[1:2], "mix_norm_1")
    proj_o = _mm(h2, W["in_odd"], name="in_odd")
    qx, kx, vx = _prep_odd_fwd(proj_o, cos_a, sin_a, P["ax_q_norm"], P["ax_k_norm"])
    ox, lse, outs = _flash_fwd(qx, kx, vx, gather_while("flash_fwd"))
    arrived("flash_fwd", outs)
    x3 = _mm(ox, W["out_odd"], name="out_odd", extras=(x2,), epilogue=_add_epilogue)
    x4, mlp1 = _mlp_fwd(x3, nmlp[1:2], W["up1"], W["down1"], "1")

    d4, d4b, loss_tile = _loss_and_grad(x4, target)

    d3, d3b, dnmlp1, dw_up1, dw_down1 = _mlp_bwd(x3, nmlp[1:2], W["up1"], W["down1"], mlp1, d4, d4b, "1")
    dox = _mm(d3b, W["out_odd"], tb=True, name="out_odd_dx")
    dw_out_odd = _mm(ox, d3b, ta=True, name="out_odd_dw", out_dtypes=(BF,))
    grads1 = [("up1", dw_up1), ("down1", dw_down1), ("out_odd", dw_out_odd)]
    dqx, dkx, dvx, outs = _flash_bwd(qx, kx, vx, ox, dox, lse, exchange_while(grads1))
    left(grads1, outs)
    dproj_o, dgq_ax, dgk_ax = _prep_odd_bwd(proj_o, cos_a, sin_a, P["ax_q_norm"], P["ax_k_norm"], dqx, dkx, dvx)
    dw_in_odd = _mm(h2, dproj_o, ta=True, name="in_odd_dw", out_dtypes=(BF,))
    dh2 = _mm(dproj_o, W["in_odd"], tb=True, name="in_odd_dx")
    d2, d2b, dnmix1 = _rms_bwd(x2, nmix[1:2], dh2, d3, "mix_norm_bwd_1")

    d1, d1b, dnmlp0, dw_up0, dw_down0 = _mlp_bwd(x1, nmlp[0:1], W["up0"], W["down0"], mlp0, d2, d2b, "0")
    doa = _mm(d1b, wo_a, tb=True, name="out_even_a_dx")
    dob = _mm(d1b, wo_b, tb=True, name="out_even_b_dx")
    dw_out_even = jnp.concatenate([_mm(oa, d1b, ta=True, name="out_even_a_dw", out_dtypes=(BF,)),
                                   _mm(ob, d1b, ta=True, name="out_even_b_dw", out_dtypes=(BF,))], axis=0)
    dy_ret, dga, dret_norm = _ret_post_bwd(y_ret, proj_e, P["ret_norm"], doa)
    hb, hf = _ret_scan(qr, dy_ret, lg, (1, 1), (0, 1), "retention_state_grads")
    dqr, dkr, dvr, dlogit = _ret_bwd(qr, kr, vr, dy_ret, lg, logit, sf, sb, hf, hb)
    grads0 = [("in_odd", dw_in_odd), ("up0", dw_up0), ("down0", dw_down0), ("out_even", dw_out_even)]
    dqs, dks, dvs, dbias, dsink, outs = _swa_bwd(qs, ks, vs, dob, bias, P["swa_sink"], exchange_while(grads0))
    left(grads0, outs)
    dt5 = _t5_table_grad(dbias, bucket)
    dproj_e, dgq_swa, dgk_swa = _prep_even_bwd(proj_e, cos_r, sin_r, P["swa_q_norm"], P["swa_k_norm"],
                                               dqr, dkr, dvr, dga, dqs, dks, dvs)
    dw_in_even = _mm(h0, dproj_e, ta=True, name="in_even_dw", out_dtypes=(BF,))
    grads_last = [("in_even", dw_in_even)]
    if late is None:
        dh0 = _mm(dproj_e, W["in_even"], tb=True, name="in_even_dx")
    else:
        dh0, outs = _mm(dproj_e, W["in_even"], tb=True, name="in_even_dx", carried=exchange_while(grads_last))
        left(grads_last, outs)
    dx, _, dnmix0 = _rms_bwd(x, nmix[0:1], dh0, d1, "mix_norm_bwd_0")

    dW = dict(grads1 + grads0 + grads_last) if late is None else landed
    dP = {"norm_mix": jnp.concatenate([dnmix0, dnmix1], 0), "norm_mlp": jnp.concatenate([dnmlp0, dnmlp1], 0),
          "ret_decay_logit": dlogit[:, :, 0, 0].T, "ret_norm": dret_norm,
          "swa_q_norm": dgq_swa, "swa_k_norm": dgk_swa, "swa_sink": dsink[:, 0][None, :],
          "t5_table": dt5[:, :SWA_HEADS], "ax_q_norm": dgq_ax, "ax_k_norm": dgk_ax}
    return loss_tile, dx, dW, dP


def _cast_shards(shards):
    n = len(shards)

    def body(*refs):
        for i_ref, o_ref in zip(refs[:n], refs[n:]):
            o_ref[...] = i_ref[...].astype(o_ref.dtype)

    return pl.pallas_call(body, name="cast_shards", in_specs=[VMEM_SPEC] * n, out_specs=[VMEM_SPEC] * n,
                          out_shape=[jax.ShapeDtypeStruct(s.shape, BF) for s in shards],
                          compiler_params=pltpu.CompilerParams(vmem_limit_bytes=VMEM_SMALL))(*shards)


def _all_gather(shards):
    n = len(shards)

    def body(*refs):
        ins, outs, stage = refs[:n], refs[n:2 * n], refs[2 * n:3 * n]
        send_sems, recv_sems, local_sems = refs[3 * n:]
        me = _my_place()
        sibling = _flip(me, 1)
        chips = [_flip(me, 4), _flip(me, 2), _flip(me, 6)]

        def copy(a, k, block, to, src=None):
            dst = outs[a].at[_index(block)]
            return pltpu.make_async_remote_copy(
                src_ref=dst if src is None else src, dst_ref=dst,
                send_sem=send_sems.at[a, k], recv_sem=recv_sems.at[a, k], device_id=to, device_id_type=MESH)

        first, mine = [], []
        for a in range(n):
            stage[a][...] = ins[a][...].astype(stage[a].dtype)
            mine.append(pltpu.make_async_copy(stage[a], outs[a].at[_index(me)], local_sems.at[a]))
            mine[-1].start()
            first.append(copy(a, 0, me, sibling, src=stage[a]))
            first += [copy(a, 1 + j, me, chip, src=stage[a]) for j, chip in enumerate(chips)]
        for cp in first:
            cp.start()
        passed = []
        for a in range(n):
            for j, chip in enumerate(chips):
                copy(a, 1 + j, chip, me).wait_recv()
                passed.append(copy(a, 4 + j, chip, sibling))
                passed[-1].start()
        for a in range(n):
            copy(a, 0, sibling, me).wait_recv()
            for j, chip in enumerate(chips):
                copy(a, 4 + j, _flip(chip, 1), me).wait_recv()
        for cp in first + passed:
            cp.wait_send()
        for cp in mine:
            cp.wait()

    return pl.pallas_call(
        body, name="weights_all_gather",
        in_specs=[VMEM_SPEC] * n, out_specs=[ANY_SPEC] * n,
        out_shape=[jax.ShapeDtypeStruct((N_DEV,) + s.shape, BF) for s in shards],
        scratch_shapes=[pltpu.VMEM(s.shape, BF) for s in shards]
        + [pltpu.SemaphoreType.DMA((n, 7)), pltpu.SemaphoreType.DMA((n, 7)), pltpu.SemaphoreType.DMA((n,))],
        compiler_params=pltpu.CompilerParams(vmem_limit_bytes=VMEM_SMALL),
    )(*shards)


def _all_reduce_small(part):
    R, C = part.shape

    def body(x_ref, o_ref, land_ref, send_sems, recv_sems):
        me = _my_place()
        land_ref[_index(me)] = x_ref[...]
        copies = []
        for k in range(1, N_DEV):
            peer = _flip(me, k)
            copies.append(pltpu.make_async_remote_copy(
                src_ref=x_ref, dst_ref=land_ref.at[_index(me)],
                send_sem=send_sems.at[k - 1], recv_sem=recv_sems.at[k - 1], device_id=peer, device_id_type=MESH))
            copies[-1].start()
        for k in range(1, N_DEV):
            peer = _flip(me, k)
            pltpu.make_async_remote_copy(
                src_ref=x_ref, dst_ref=land_ref.at[_index(peer)],
                send_sem=send_sems.at[k - 1], recv_sem=recv_sems.at[k - 1], device_id=peer, device_id_type=MESH).wait_recv()
        for cp in copies:
            cp.wait_send()
        acc = land_ref[0]
        for s in range(1, N_DEV):
            acc = acc + land_ref[s]
        o_ref[...] = acc

    return pl.pallas_call(
        body, name="small_all_reduce", in_specs=[VMEM_SPEC], out_specs=VMEM_SPEC,
        out_shape=jax.ShapeDtypeStruct((R, C), F32),
        scratch_shapes=[pltpu.VMEM((N_DEV, R, C), F32), pltpu.SemaphoreType.DMA((7,)), pltpu.SemaphoreType.DMA((7,))],
    )(part)


def _adamw_math(w, g, m, v):
    m = ADAM_B1 * m + (1.0 - ADAM_B1) * g
    v = ADAM_B2 * v + (1.0 - ADAM_B2) * jnp.square(g)
    m_hat = m / (1.0 - ADAM_B1 ** ADAM_STEP)
    v_hat = v / (1.0 - ADAM_B2 ** ADAM_STEP)
    delta = -ADAM_LR * (m_hat / (jnp.sqrt(v_hat) + ADAM_EPS) + ADAM_WD * w)
    return delta, m, v


def _sum_and_adamw(landed, w, m, v, name):
    R, C = w.shape
    tr = _pick(R, (256, 128))

    def body(l_ref, w_ref, m_ref, v_ref, g_ref, d_ref, nm_ref, nv_ref):
        g = l_ref[0].astype(F32)
        for s in range(1, N_DEV):
            g = g + l_ref[s].astype(F32)
        g_ref[...] = g
        d_ref[...], nm_ref[...], nv_ref[...] = _adamw_math(w_ref[...], g, m_ref[...], v_ref[...])

    row = pl.BlockSpec((tr, C), lambda i: (i, 0))
    return pl.pallas_call(
        body, name=name, grid=(R // tr,),
        in_specs=[pl.BlockSpec((N_DEV, tr, C), lambda i: (0, i, 0)), row, row, row], out_specs=[row] * 4,
        out_shape=[jax.ShapeDtypeStruct((R, C), F32)] * 4, compiler_params=_params(("parallel",)),
    )(landed, w, m, v)


def _adamw_small(w, g, m, v):
    def body(w_ref, g_ref, m_ref, v_ref, d_ref, nm_ref, nv_ref):
        d_ref[...], nm_ref[...], nv_ref[...] = _adamw_math(w_ref[...], g_ref[...], m_ref[...], v_ref[...])

    full = pl.BlockSpec(w.shape, lambda: (0, 0))
    return pl.pallas_call(body, name="adamw_small", in_specs=[full] * 4, out_specs=[full] * 3,
                          out_shape=[jax.ShapeDtypeStruct(w.shape, F32)] * 3)(w, g, m, v)


MATRICES = ("w_in_even", "w_out_even", "w_in_odd", "w_out_odd", "w_mlp_up", "w_mlp_down")
SMALL = ("norm_mix", "norm_mlp", "ret_decay_logit", "ret_norm", "swa_q_norm", "swa_k_norm", "swa_sink",
         "t5_table", "ax_q_norm", "ax_k_norm")
MATRIX_OF = {"in_even": ("w_in_even", 0), "out_even": ("w_out_even", 0), "in_odd": ("w_in_odd", 0),
             "out_odd": ("w_out_odd", 0), "up0": ("w_mlp_up", 0), "up1": ("w_mlp_up", 1),
             "down0": ("w_mlp_down", 0), "down1": ("w_mlp_down", 1)}
GATHER_FIRST = ("in_even", "in_odd", "out_even")
GATHER_WHILE = {"swa_fwd": ("up0", "down0"), "flash_fwd": ("out_odd", "up1", "down1")}


SMALL_ROWS = 8
SMALL_AT = {"norm_mix": (0, 0), "norm_mlp": (2, 0), "ret_norm": (4, 0), "swa_q_norm": (5, 0), "swa_k_norm": (5, 128),
            "ax_q_norm": (5, 256), "ax_k_norm": (5, 384), "swa_sink": (5, 512), "ret_decay_logit": (5, 640),
            "t5_table": (6, 0)}
LOSS_AT = (5, 768)


def _pack_small(arrays, loss=None):
    buf = jnp.zeros((SMALL_ROWS, 1024), F32)
    for name, (r, c) in SMALL_AT.items():
        a = arrays[name].astype(F32)
        a = a.reshape(1, -1) if name in ("ret_decay_logit", "t5_table") else a.reshape(-1, a.shape[-1])
        buf = lax.dynamic_update_slice(buf, a, (r, c))
    if loss is not None:
        buf = lax.dynamic_update_slice(buf, loss.reshape(1, 1), LOSS_AT)
    return buf


def _unpack_small(buf, like):
    out = {}
    for name, (r, c) in SMALL_AT.items():
        shape = like[name].shape
        rows = 1 if name in ("ret_decay_logit", "t5_table") else math.prod(shape[:-1])
        cols = math.prod(shape) // rows
        out[name] = buf[r:r + rows, c:c + cols].reshape(shape)
    return out


def kernel(x, norm_mix, norm_mlp, w_in_even, w_out_even, ret_decay_logit, ret_norm, swa_q_norm, swa_k_norm, swa_sink, t5_table, w_in_odd, w_out_odd, ax_q_norm, ax_k_norm, w_mlp_up, w_mlp_down, loss_target, m_norm_mix, m_norm_mlp, m_w_in_even, m_w_out_even, m_ret_decay_logit, m_ret_norm, m_swa_q_norm, m_swa_k_norm, m_swa_sink, m_t5_table, m_w_in_odd, m_w_out_odd, m_ax_q_norm, m_ax_k_norm, m_w_mlp_up, m_w_mlp_down, v_norm_mix, v_norm_mlp, v_w_in_even, v_w_out_even, v_ret_decay_logit, v_ret_norm, v_swa_q_norm, v_swa_k_norm, v_swa_sink, v_t5_table, v_w_in_odd, v_w_out_odd, v_ax_q_norm, v_ax_k_norm, v_w_mlp_up, v_w_mlp_down):
    given = dict(locals())
    weights = {n: given[n] for n in MATRICES + SMALL}
    moments_m = {n: given["m_" + n] for n in MATRICES + SMALL}
    moments_v = {n: given["v_" + n] for n in MATRICES + SMALL}

    def shard(table, key):
        arg, layer = MATRIX_OF[key]
        return table[arg][layer]

    gathered = _all_gather([shard(weights, k) for k in GATHER_FIRST])
    W = {k: _assemble(k, g) for k, g in zip(GATHER_FIRST, gathered)}
    late_keys = [k for keys in GATHER_WHILE.values() for k in keys]
    cast = dict(zip(late_keys, _cast_shards([shard(weights, k) for k in late_keys])))
    late = {stage: [(k, cast[k]) for k in keys] for stage, keys in GATHER_WHILE.items()}
    P = {"norm_mix": norm_mix, "norm_mlp": norm_mlp, "ret_decay_logit": ret_decay_logit[0], "ret_norm": ret_norm,
         "swa_q_norm": swa_q_norm, "swa_k_norm": swa_k_norm, "swa_sink": swa_sink, "t5_table": t5_table,
         "ax_q_norm": ax_q_norm, "ax_k_norm": ax_k_norm}

    loss_tile, dx, landed, dP = _local_step(x[0], loss_target[0], W, P, late)

    per_key = {k: _sum_and_adamw(landed[k], shard(weights, k), shard(moments_m, k), shard(moments_v, k), "adamw_" + k)
               for k in MATRIX_OF}
    grads, deltas, new_m, new_v = {}, {}, {}, {}
    for i, out in enumerate((grads, deltas, new_m, new_v)):
        for n in MATRICES:
            out[n] = jnp.stack([per_key[k][i] for k, (arg, _) in MATRIX_OF.items() if arg == n])

    dP["ret_decay_logit"] = dP["ret_decay_logit"][None]
    total = _all_reduce_small(_pack_small(dP, loss_tile[0, 0]))
    loss = total[LOSS_AT[0], LOSS_AT[1]]
    small_d, small_m, small_v = _adamw_small(_pack_small(weights), total, _pack_small(moments_m), _pack_small(moments_v))
    like = {n: weights[n] for n in SMALL}
    for out, buf in ((grads, total), (deltas, small_d), (new_m, small_m), (new_v, small_v)):
        out.update(_unpack_small(buf, like))

    order = ("norm_mix", "norm_mlp", "w_in_even", "w_out_even", "ret_decay_logit", "ret_norm", "swa_q_norm", "swa_k_norm",
             "swa_sink", "t5_table", "w_in_odd", "w_out_odd", "ax_q_norm", "ax_k_norm", "w_mlp_up", "w_mlp_down")
    return (loss, dx[None], *[grads[n] for n in order], *[deltas[n] for n in order],
            *[new_m[n] for n in order], *[new_v[n] for n in order])
```

```python
import math

import jax
import jax.numpy as jnp
from jax import lax
from jax.experimental import pallas as pl
from jax.experimental.pallas import tpu as pltpu

F32 = jnp.float32
BF = jnp.bfloat16

D_MODEL = 1024
HEAD_DIM = 128
EPS = 1e-6
NEG_INF = -1e30
RET_HEADS, RET_DK, RET_DV = 4, 128, 256
RET_THETA = 10000.0
SWA_HEADS, SWA_KV_HEADS, WINDOW, BLOCK = 8, 2, 128, 128
T5_BUCKETS, T5_MAX_DIST = 32, 128
AX_HEADS, AX_KV_HEADS, AX_THETA, GRID_W = 8, 2, 10000.0, 64
D_FF = 4096
ATT_SCALE = HEAD_DIM ** -0.5
LN2 = math.log(2.0)
AX_SCALE = ATT_SCALE / LN2
RET_SCALE = RET_DK ** -0.5
N_DEV = 8

ADAM_LR, ADAM_B1, ADAM_B2, ADAM_EPS, ADAM_WD, ADAM_STEP = 0.001, 0.9, 0.999, 1e-08, 0.01, 10

MIB = 1024 * 1024
VMEM_SMALL = 40 * MIB
VMEM_LARGE = 56 * MIB

OFF_QA, OFF_KA, OFF_VA, OFF_GA, OFF_QB, OFF_KB, OFF_VB = 0, 512, 1024, 2048, 3072, 4096, 4352
EVEN_IN = 4608
ODD_IN = 1536

NT = (((1,), (1,)), ((), ()))
TN = (((0,), (0,)), ((), ()))
NN = (((1,), (0,)), ((), ()))


def _dot(a, b, dims=NN):
    return lax.dot_general(a, b, dims, preferred_element_type=F32)


def _params(sem=None, vmem=VMEM_SMALL):
    return pltpu.CompilerParams(dimension_semantics=sem, vmem_limit_bytes=vmem)


def _pick(n, prefs):
    for p in prefs:
        if n % p == 0:
            return p
    return n


def _row_sum(x):
    return jnp.sum(x, axis=0, keepdims=True)


def _all_sum(x):
    return jnp.sum(jnp.sum(x, axis=0, keepdims=True), axis=1, keepdims=True)


def _sigmoid(x):
    return 1.0 / (1.0 + jnp.exp(-x))


SMEM_SPEC = pl.BlockSpec(memory_space=pltpu.SMEM)
ANY_SPEC = pl.BlockSpec(memory_space=pl.ANY)
VMEM_SPEC = pl.BlockSpec(memory_space=pltpu.VMEM)
MESH = pl.DeviceIdType.MESH


def _my_place():
    return lax.axis_index("x"), lax.axis_index("y"), lax.axis_index("c")


def _flip(place, k):
    x, y, c = place
    return (1 - x if k & 4 else x, 1 - y if k & 2 else y, 1 - c if k & 1 else c)


def _index(place):
    x, y, c = place
    return 4 * x + 2 * y + c


class _Exchange:
    def __init__(self, sources, gather):
        self.sources, self.gather, self.n = list(sources), gather, len(sources)
        self.out_shape = [jax.ShapeDtypeStruct(((N_DEV,) + s.shape) if gather else s.shape, s.dtype) for s in self.sources]
        self.scratch = [pltpu.SemaphoreType.DMA((self.n, 7)), pltpu.SemaphoreType.DMA((self.n, 7)),
                        pltpu.SemaphoreType.DMA((self.n,))]

    def _source(self, ins, a, place):
        return ins[a] if self.gather else ins[a].at[_index(place)]

    def _local(self, ins, outs, sems):
        me = _my_place()
        return [pltpu.make_async_copy(self._source(ins, a, me), outs[a].at[_index(me)], sems[2].at[a]) for a in range(self.n)]

    def _remote(self, ins, outs, sems, arriving):
        send_sems, recv_sems, _ = sems
        me = _my_place()
        copies = []
        for a in range(self.n):
            for k in range(1, N_DEV):
                peer = _flip(me, k)
                copies.append(pltpu.make_async_remote_copy(
                    src_ref=self._source(ins, a, peer), dst_ref=outs[a].at[_index(peer if arriving else me)],
                    send_sem=send_sems.at[a, k - 1], recv_sem=recv_sems.at[a, k - 1], device_id=peer, device_id_type=MESH))
        return copies

    def start(self, ins, outs, sems):
        for cp in self._local(ins, outs, sems) + self._remote(ins, outs, sems, arriving=False):
            cp.start()

    def wait(self, ins, outs, sems):
        for cp in self._remote(ins, outs, sems, arriving=True):
            cp.wait_recv()
        for cp in self._remote(ins, outs, sems, arriving=False):
            cp.wait_send()
        for cp in self._local(ins, outs, sems):
            cp.wait()


def _carry_call(body, args, *, name, grid, in_specs, out_specs, out_shape, scratch_shapes=(), vmem=VMEM_SMALL,
                semantics=None, carried=None):
    if carried is None:
        outs = pl.pallas_call(body, name=name, grid=grid, in_specs=in_specs, out_specs=out_specs, out_shape=out_shape,
                              scratch_shapes=list(scratch_shapes), compiler_params=_params(semantics, vmem))(*args)
        return list(outs), []
    ni, no, ns, nc = len(in_specs), len(out_specs), len(scratch_shapes), carried.n

    def full_body(*refs):
        ins, cin = refs[:ni], refs[ni:ni + nc]
        outs, cout = refs[ni + nc:ni + nc + no], refs[ni + nc + no:ni + 2 * nc + no]
        scratch, sems = refs[ni + 2 * nc + no:ni + 2 * nc + no + ns], refs[ni + 2 * nc + no + ns:]
        ids = [pl.program_id(d) for d in range(len(grid))]
        first, last = ids[0] == 0, ids[0] == grid[0] - 1
        for d in range(1, len(grid)):
            first, last = first & (ids[d] == 0), last & (ids[d] == grid[d] - 1)

        @pl.when(first)
        def _():
            carried.start(cin, cout, sems)

        body(*ins, *outs, *scratch)

        @pl.when(last)
        def _():
            carried.wait(cin, cout, sems)

    outs = pl.pallas_call(
        full_body, name=name, grid=grid, in_specs=list(in_specs) + [ANY_SPEC] * nc,
        out_specs=list(out_specs) + [ANY_SPEC] * nc, out_shape=list(out_shape) + carried.out_shape,
        scratch_shapes=list(scratch_shapes) + carried.scratch,
        compiler_params=_params(("arbitrary",) * len(grid), vmem))(*args, *carried.sources)
    return list(outs[:no]), list(outs[no:])


def _mm(a, b, *, name, ta=False, tb=False, out_dtypes=(F32,), extras=(), epilogue=None,
        tm=1024, tn=1024, tk=1024, carried=None):
    M, K = (a.shape[1], a.shape[0]) if ta else a.shape
    N = b.shape[0] if tb else b.shape[1]
    assert K == (b.shape[1] if tb else b.shape[0])
    tm = _pick(M, (tm, 512, 256, 128))
    tn = _pick(N, (tn, 1536, 512, 384, 256, 128))
    tk = _pick(K, (tk, 1536, 512, 256, 128))
    nk = K // tk
    ne, no = len(extras), len(out_dtypes)
    dims = (((0 if ta else 1,), (1 if tb else 0,)), ((), ()))
    if epilogue is None:
        epilogue = lambda acc: (acc,)

    def body(a_ref, b_ref, *rest):
        extra_refs, out_refs = rest[:ne], rest[ne:ne + no]

        def finish(acc):
            outs = epilogue(acc, *[r[...] for r in extra_refs])
            for o_ref, o in zip(out_refs, outs):
                o_ref[...] = o.astype(o_ref.dtype)

        part = _dot(a_ref[...], b_ref[...], dims)
        if nk == 1:
            finish(part)
        else:
            acc_ref = rest[-1]
            k = pl.program_id(2)

            @pl.when(k == 0)
            def _():
                acc_ref[...] = part

            @pl.when(k > 0)
            def _():
                acc_ref[...] += part

            @pl.when(k == nk - 1)
            def _():
                finish(acc_ref[...])

    a_spec = pl.BlockSpec((tk, tm), lambda i, j, k: (k, i)) if ta else pl.BlockSpec((tm, tk), lambda i, j, k: (i, k))
    b_spec = pl.BlockSpec((tn, tk), lambda i, j, k: (j, k)) if tb else pl.BlockSpec((tk, tn), lambda i, j, k: (k, j))
    o_spec = pl.BlockSpec((tm, tn), lambda i, j, k: (i, j))
    outs, landed = _carry_call(
        body, (a, b, *extras), name=name, grid=(M // tm, N // tn, nk),
        in_specs=[a_spec, b_spec] + [o_spec] * ne,
        out_specs=[o_spec] * no,
        out_shape=[jax.ShapeDtypeStruct((M, N), dt) for dt in out_dtypes],
        scratch_shapes=[pltpu.VMEM((tm, tn), F32)] if nk > 1 else [],
        semantics=("parallel", "parallel", "arbitrary"), carried=carried)
    outs = outs[0] if no == 1 else outs
    return outs if carried is None else (outs, landed)


def _rms_fwd(x, g, name):
    S, Dm = x.shape
    tr = _pick(S, (512,))

    def body(x_ref, g_ref, o_ref):
        xv = x_ref[...]
        r = lax.rsqrt(jnp.mean(xv * xv, axis=-1, keepdims=True) + EPS)
        o_ref[...] = (xv * r * g_ref[...]).astype(o_ref.dtype)

    row = pl.BlockSpec((tr, Dm), lambda i: (i, 0))
    return pl.pallas_call(
        body, name=name, grid=(S // tr,),
        in_specs=[row, pl.BlockSpec((1, Dm), lambda i: (0, 0))], out_specs=row,
        out_shape=jax.ShapeDtypeStruct((S, Dm), BF), compiler_params=_params(("parallel",)),
    )(x, g)


def _rms_bwd(x, g, dh, dres, name):
    S, Dm = x.shape
    tr = _pick(S, (512,))

    def body(x_ref, g_ref, dh_ref, dres_ref, dx_ref, dxb_ref, dg_ref):
        xv = x_ref[...]
        r = lax.rsqrt(jnp.mean(xv * xv, axis=-1, keepdims=True) + EPS)
        xh = xv * r
        dy = dh_ref[...].astype(F32)
        dxh = dy * g_ref[...]
        dx = r * (dxh - xh * jnp.mean(dxh * xh, axis=-1, keepdims=True)) + dres_ref[...]
        dx_ref[...] = dx
        dxb_ref[...] = dx.astype(dxb_ref.dtype)

        @pl.when(pl.program_id(0) == 0)
        def _():
            dg_ref[...] = jnp.zeros_like(dg_ref)

        dg_ref[...] += _row_sum(dy * xh)

    row = pl.BlockSpec((tr, Dm), lambda i: (i, 0))
    vec = pl.BlockSpec((1, Dm), lambda i: (0, 0))
    return pl.pallas_call(
        body, name=name, grid=(S // tr,),
        in_specs=[row, vec, row, row], out_specs=[row, row, vec],
        out_shape=[jax.ShapeDtypeStruct((S, Dm), F32), jax.ShapeDtypeStruct((S, Dm), BF),
                   jax.ShapeDtypeStruct((1, Dm), F32)],
        compiler_params=_params(("arbitrary",)),
    )(x, g, dh, dres)


def _loss_and_grad(y, target):
    S, Dm = y.shape
    tr = _pick(S, (512,))
    n = S // tr

    def body(y_ref, t_ref, dy_ref, dyb_ref, loss_ref, acc_ref):
        i = pl.program_id(0)
        e = y_ref[...] - t_ref[...]
        d = e * (1.0 / Dm)
        dy_ref[...] = d
        dyb_ref[...] = d.astype(dyb_ref.dtype)

        @pl.when(i == 0)
        def _():
            acc_ref[...] = jnp.zeros_like(acc_ref)

        acc_ref[...] += _row_sum(e * e)

        @pl.when(i == n - 1)
        def _():
            loss_ref[...] = jnp.broadcast_to(_all_sum(acc_ref[...]) * (0.5 / Dm), loss_ref.shape)

    row = pl.BlockSpec((tr, Dm), lambda i: (i, 0))
    return pl.pallas_call(
        body, name="loss_head", grid=(n,),
        in_specs=[row, row], out_specs=[row, row, pl.BlockSpec((8, 128), lambda i: (0, 0))],
        out_shape=[jax.ShapeDtypeStruct((S, Dm), F32), jax.ShapeDtypeStruct((S, Dm), BF),
                   jax.ShapeDtypeStruct((8, 128), F32)],
        scratch_shapes=[pltpu.VMEM((1, Dm), F32)],
        compiler_params=_params(("arbitrary",)),
    )(y, target)


def _partner(x, half):
    if half == 64:
        return pltpu.roll(x, 64, 1)
    lane = lax.broadcasted_iota(jnp.int32, x.shape, 1)
    return jnp.where((lane % (2 * half)) < half, pltpu.roll(x, 128 - half, 1), pltpu.roll(x, half, 1))


def _rope(x, cos, sin, half):
    return x * cos + _partner(x, half) * sin


def _rope_t(dy, cos, sin, half):
    return dy * cos - _partner(dy, half) * sin


def _head_norm(x):
    r = lax.rsqrt(jnp.mean(x * x, axis=-1, keepdims=True) + EPS)
    return x * r, r


def _head_norm_bwd(dxh, xh, r):
    return r * (dxh - xh * jnp.mean(dxh * xh, axis=-1, keepdims=True))


def _cols(ref, off, width=HEAD_DIM):
    return ref[:, off:off + width]


def _prep_even_fwd(proj, cos, sin, gq, gk, carried=None):
    S = proj.shape[0]
    tr = _pick(S, (256,))

    def body(p_ref, cos_ref, sin_ref, gq_ref, gk_ref, qr_ref, kr_ref, vr_ref, qs_ref, ks_ref, vs_ref):
        cos_v, sin_v = cos_ref[...], sin_ref[...]
        for h in range(RET_HEADS):
            o = h * RET_DK
            qr_ref[:, o:o + RET_DK] = _rope(_cols(p_ref, OFF_QA + o), cos_v, sin_v, 64).astype(qr_ref.dtype)
            kr_ref[:, o:o + RET_DK] = (_rope(_cols(p_ref, OFF_KA + o), cos_v, sin_v, 64) * RET_SCALE).astype(kr_ref.dtype)
        vr_ref[...] = p_ref[:, OFF_VA:OFF_VA + 1024].astype(vr_ref.dtype)
        for h in range(SWA_HEADS):
            o = h * HEAD_DIM
            xh, _ = _head_norm(_cols(p_ref, OFF_QB + o))
            qs_ref[:, o:o + HEAD_DIM] = (xh * gq_ref[...] * ATT_SCALE).astype(qs_ref.dtype)
        for h in range(SWA_KV_HEADS):
            o = h * HEAD_DIM
            xh, _ = _head_norm(_cols(p_ref, OFF_KB + o))
            ks_ref[:, o:o + HEAD_DIM] = (xh * gk_ref[...]).astype(ks_ref.dtype)
        vs_ref[...] = p_ref[:, OFF_VB:OFF_VB + 256].astype(vs_ref.dtype)

    def row(w):
        return pl.BlockSpec((tr, w), lambda i: (i, 0))

    vec = pl.BlockSpec((1, HEAD_DIM), lambda i: (0, 0))
    widths = (512, 512, 1024, 1024, 256, 256)
    return _carry_call(
        body, (proj, cos, sin, gq, gk), name="prep_even_fwd", grid=(S // tr,),
        in_specs=[row(EVEN_IN), row(128), row(128), vec, vec],
        out_specs=[row(w) for w in widths],
        out_shape=[jax.ShapeDtypeStruct((S, w), BF) for w in widths],
        semantics=("parallel",), carried=carried)


def _prep_even_bwd(proj, cos, sin, gq, gk, dqr, dkr, dvr, dga, dqs, dks, dvs):
    S = proj.shape[0]
    tr = _pick(S, (256,))

    def body(p_ref, cos_ref, sin_ref, gq_ref, gk_ref, dqr_ref, dkr_ref, dvr_ref, dga_ref, dqs_ref, dks_ref,
             dvs_ref, dp_ref, dgq_ref, dgk_ref):
        cos_v, sin_v = cos_ref[...], sin_ref[...]
        dt = dp_ref.dtype
        for h in range(RET_HEADS):
            o = h * RET_DK
            dp_ref[:, OFF_QA + o:OFF_QA + o + RET_DK] = _rope_t(_cols(dqr_ref, o).astype(F32), cos_v, sin_v, 64).astype(dt)
            dp_ref[:, OFF_KA + o:OFF_KA + o + RET_DK] = _rope_t(_cols(dkr_ref, o).astype(F32) * RET_SCALE, cos_v, sin_v, 64).astype(dt)
        dp_ref[:, OFF_VA:OFF_VA + 1024] = dvr_ref[...].astype(dt)
        dp_ref[:, OFF_GA:OFF_GA + 1024] = dga_ref[...].astype(dt)
        dgq = jnp.zeros((1, HEAD_DIM), F32)
        for h in range(SWA_HEADS):
            o = h * HEAD_DIM
            xh, r = _head_norm(_cols(p_ref, OFF_QB + o))
            dy = _cols(dqs_ref, o).astype(F32) * ATT_SCALE
            dgq = dgq + _row_sum(dy * xh)
            dp_ref[:, OFF_QB + o:OFF_QB + o + HEAD_DIM] = _head_norm_bwd(dy * gq_ref[...], xh, r).astype(dt)
        dgk = jnp.zeros((1, HEAD_DIM), F32)
        for h in range(SWA_KV_HEADS):
            o = h * HEAD_DIM
            xh, r = _head_norm(_cols(p_ref, OFF_KB + o))
            dy = _cols(dks_ref, o)
            dgk = dgk + _row_sum(dy * xh)
            dp_ref[:, OFF_KB + o:OFF_KB + o + HEAD_DIM] = _head_norm_bwd(dy * gk_ref[...], xh, r).astype(dt)
        dp_ref[:, OFF_VB:OFF_VB + 256] = dvs_ref[...].astype(dt)

        @pl.when(pl.program_id(0) == 0)
        def _():
            dgq_ref[...] = jnp.zeros_like(dgq_ref)
            dgk_ref[...] = jnp.zeros_like(dgk_ref)

        dgq_ref[...] += dgq
        dgk_ref[...] += dgk

    def row(w):
        return pl.BlockSpec((tr, w), lambda i: (i, 0))

    vec = pl.BlockSpec((1, HEAD_DIM), lambda i: (0, 0))
    return pl.pallas_call(
        body, name="prep_even_bwd", grid=(S // tr,),
        in_specs=[row(EVEN_IN), row(128), row(128), vec, vec, row(512), row(512), row(1024), row(1024),
                  row(1024), row(256), row(256)],
        out_specs=[row(EVEN_IN), vec, vec],
        out_shape=[jax.ShapeDtypeStruct((S, EVEN_IN), BF), jax.ShapeDtypeStruct((1, HEAD_DIM), F32),
                   jax.ShapeDtypeStruct((1, HEAD_DIM), F32)],
        compiler_params=_params(("arbitrary",)),
    )(proj, cos, sin, gq, gk, dqr, dkr, dvr, dga, dqs, dks, dvs)


RET_CHUNK = 512
def _log_sigmoid_tile(logit_tile):
    def body(x_ref, o_ref):
        xv = x_ref[...]
        t = jnp.exp(-jnp.abs(xv))
        log1p_t = jnp.where(t < 1e-3, t * (1.0 - 0.5 * t), jnp.log(1.0 + t))
        o_ref[...] = jnp.minimum(xv, 0.0) - log1p_t

    full = pl.BlockSpec((8, 128), lambda: (0, 0))
    return pl.pallas_call(body, name="log_sigmoid", in_specs=[full], out_specs=full,
                          out_shape=jax.ShapeDtypeStruct((8, 128), F32))(logit_tile)


def _decay(diff, lf, lb):
    return jnp.exp(jnp.where(diff >= 0, lf * diff, -(lb * diff)))


def _col_iota(n):
    return lax.broadcasted_iota(jnp.int32, (n, 1), 0).astype(F32)


def _ret_scan(x, z, lg, asc, desc, name):
    S = x.shape[0]
    C = _pick(S, (RET_CHUNK,))
    nc = S // C
    (arow, aoff), (drow, doff) = asc, desc

    def body(lg_ref, xa_ref, za_ref, xd_ref, zd_ref, asc_ref, desc_ref, sa_ref, sd_ref):
        h, t = pl.program_id(0), pl.program_id(1)
        la, ld = lg_ref[arow, h], lg_ref[drow, h]

        @pl.when(t == 0)
        def _():
            sa_ref[...] = jnp.zeros_like(sa_ref)
            sd_ref[...] = jnp.zeros_like(sd_ref)

        asc_ref[0, 0] = sa_ref[...]
        desc_ref[0, 0] = sd_ref[...]
        j = _col_iota(C)
        xa = (xa_ref[...].astype(F32) * jnp.exp(la * (C - 1 + aoff - j))).astype(xa_ref.dtype)
        xd = (xd_ref[...].astype(F32) * jnp.exp(ld * (j + doff))).astype(xd_ref.dtype)
        sa_ref[...] = jnp.exp(jnp.full((1, RET_DV), la * C, F32)) * sa_ref[...] + _dot(xa, za_ref[...], TN)
        sd_ref[...] = jnp.exp(jnp.full((1, RET_DV), ld * C, F32)) * sd_ref[...] + _dot(xd, zd_ref[...], TN)

    state = jax.ShapeDtypeStruct((RET_HEADS, nc, RET_DK, RET_DV), F32)
    return pl.pallas_call(
        body, name=name, grid=(RET_HEADS, nc),
        in_specs=[SMEM_SPEC,
                  pl.BlockSpec((C, RET_DK), lambda h, t: (t, h)), pl.BlockSpec((C, RET_DV), lambda h, t: (t, h)),
                  pl.BlockSpec((C, RET_DK), lambda h, t: (nc - 1 - t, h)), pl.BlockSpec((C, RET_DV), lambda h, t: (nc - 1 - t, h))],
        out_specs=[pl.BlockSpec((1, 1, RET_DK, RET_DV), lambda h, t: (h, t, 0, 0)),
                   pl.BlockSpec((1, 1, RET_DK, RET_DV), lambda h, t: (h, nc - 1 - t, 0, 0))],
        out_shape=[state, state],
        scratch_shapes=[pltpu.VMEM((RET_DK, RET_DV), F32), pltpu.VMEM((RET_DK, RET_DV), F32)],
        compiler_params=_params(("parallel", "arbitrary")),
    )(lg, x, z, x, z)


def _ret_fwd(q, k, v, lg, sf, sb):
    S = q.shape[0]
    C = _pick(S, (RET_CHUNK,))

    def body(lg_ref, q_ref, k_ref, v_ref, sf_ref, sb_ref, y_ref):
        h = pl.program_id(0)
        lf, lb = lg_ref[0, h], lg_ref[1, h]
        qv = q_ref[...]
        dt = qv.dtype
        diff = (lax.broadcasted_iota(jnp.int32, (C, C), 0) - lax.broadcasted_iota(jnp.int32, (C, C), 1)).astype(F32)
        y = _dot((_dot(qv, k_ref[...], NT) * _decay(diff, lf, lb)).astype(dt), v_ref[...])
        r = _col_iota(C)
        qf = qv.astype(F32)
        y = y + _dot((qf * jnp.exp(lf * (r + 1.0))).astype(dt), sf_ref[0, 0].astype(dt))
        y_ref[...] = y + _dot((qf * jnp.exp(lb * (C - r))).astype(dt), sb_ref[0, 0].astype(dt))

    state = pl.BlockSpec((1, 1, RET_DK, RET_DV), lambda h, c: (h, c, 0, 0))
    qk = pl.BlockSpec((C, RET_DK), lambda h, c: (c, h))
    vy = pl.BlockSpec((C, RET_DV), lambda h, c: (c, h))
    return pl.pallas_call(
        body, name="retention_fwd", grid=(RET_HEADS, S // C),
        in_specs=[SMEM_SPEC, qk, qk, vy, state, state], out_specs=vy,
        out_shape=jax.ShapeDtypeStruct((S, RET_HEADS * RET_DV), F32),
        compiler_params=_params(("parallel", "parallel")),
    )(lg, q, k, v, sf, sb)


def _ret_bwd(q, k, v, dy, lg, logit, sf, sb, hf, hb):
    S = q.shape[0]
    C = _pick(S, (RET_CHUNK,))
    nc = S // C

    def body(lg_ref, logit_ref, q_ref, k_ref, v_ref, dy_ref, sf_ref, sb_ref, hf_ref, hb_ref,
             dq_ref, dk_ref, dv_ref, dlg_ref, acc_ref):
        h, c = pl.program_id(0), pl.program_id(1)
        lf, lb = lg_ref[0, h], lg_ref[1, h]

        @pl.when(c == 0)
        def _():
            acc_ref[...] = jnp.zeros_like(acc_ref)

        qv, kv, vv, dyv = q_ref[...], k_ref[...], v_ref[...], dy_ref[...]
        dt = qv.dtype
        qf, kf = qv.astype(F32), kv.astype(F32)
        diff = (lax.broadcasted_iota(jnp.int32, (C, C), 0) - lax.broadcasted_iota(jnp.int32, (C, C), 1)).astype(F32)
        dec = _decay(diff, lf, lb)
        sc = _dot(qv, kv, NT) * dec
        dp = _dot(dyv, vv, NT)
        da = (dp * dec).astype(dt)
        dq = _dot(da, kv)
        dk = _dot(da, qv, TN)
        dv = _dot(sc.astype(dt), dyv, TN)
        w = sc * dp * diff
        tot_w, tot_f = _all_sum(w), _all_sum(jnp.where(diff >= 0, w, 0.0))
        d_lf, d_lb = tot_f, tot_f - tot_w
        r = _col_iota(C)
        a, b = jnp.exp(lf * (r + 1.0)), jnp.exp(lb * (C - r))
        e, f = jnp.exp(lf * (C - 1.0 - r)), jnp.exp(lb * r)
        sfv, sbv, hfv, hbv = sf_ref[0, 0], sb_ref[0, 0], hf_ref[0, 0], hb_ref[0, 0]
        t_f, t_b = _dot(dyv, sfv.astype(dt), NT), _dot(dyv, sbv.astype(dt), NT)
        u_f, u_b = _dot(vv, hfv.astype(dt), NT), _dot(vv, hbv.astype(dt), NT)
        dq_ref[...] = (dq + a * t_f + b * t_b).astype(dq_ref.dtype)
        dk_ref[...] = (dk + e * u_f + f * u_b).astype(dk_ref.dtype)
        dv_ref[...] = (dv + _dot((kf * e).astype(dt), hfv.astype(dt)) + _dot((kf * f).astype(dt), hbv.astype(dt))).astype(dv_ref.dtype)
        row_q_f = jnp.sum(qf * t_f, axis=-1, keepdims=True)
        row_q_b = jnp.sum(qf * t_b, axis=-1, keepdims=True)
        row_k_f = jnp.sum(kf * u_f, axis=-1, keepdims=True)
        row_k_b = jnp.sum(kf * u_b, axis=-1, keepdims=True)
        gf_c = jnp.exp(jnp.full((1, 1), lf * C, F32))
        gb_c = jnp.exp(jnp.full((1, 1), lb * C, F32))
        d_lf = d_lf + _all_sum((r + 1.0) * a * row_q_f + (C - 1.0 - r) * e * row_k_f) + C * gf_c * _all_sum(hfv * sfv)
        d_lb = d_lb + _all_sum((C - r) * b * row_q_b + r * f * row_k_b) + C * gb_c * _all_sum(hbv * sbv)
        acc_ref[0] += jnp.broadcast_to(d_lf, (8, 128))
        acc_ref[1] += jnp.broadcast_to(d_lb, (8, 128))

        @pl.when(c == nc - 1)
        def _():
            for d in range(2):
                gate = 1.0 / (1.0 + jnp.exp(jnp.full((8, 128), logit_ref[d, h], F32)))
                dlg_ref[0, d] = acc_ref[d] * gate

    state = pl.BlockSpec((1, 1, RET_DK, RET_DV), lambda h, c: (h, c, 0, 0))
    qk = pl.BlockSpec((C, RET_DK), lambda h, c: (c, h))
    vy = pl.BlockSpec((C, RET_DV), lambda h, c: (c, h))
    return pl.pallas_call(
        body, name="retention_bwd", grid=(RET_HEADS, nc),
        in_specs=[SMEM_SPEC, SMEM_SPEC, qk, qk, vy, vy, state, state, state, state],
        out_specs=[qk, qk, vy, pl.BlockSpec((1, 2, 8, 128), lambda h, c: (h, 0, 0, 0))],
        out_shape=[jax.ShapeDtypeStruct((S, RET_HEADS * RET_DK), BF), jax.ShapeDtypeStruct((S, RET_HEADS * RET_DK), BF),
                   jax.ShapeDtypeStruct((S, RET_HEADS * RET_DV), BF),
                   jax.ShapeDtypeStruct((RET_HEADS, 2, 8, 128), F32)],
        scratch_shapes=[pltpu.VMEM((2, 8, 128), F32)],
        compiler_params=_params(("parallel", "arbitrary")),
    )(lg, logit, q, k, v, dy, sf, sb, hf, hb)


def _ret_post_fwd(y, proj, gn):
    S = y.shape[0]
    tr = _pick(S, (512,))

    def body(y_ref, g_ref, gn_ref, o_ref):
        for h in range(RET_HEADS):
            o = h * RET_DV
            yh, _ = _head_norm(_cols(y_ref, o, RET_DV))
            gate = _cols(g_ref, o, RET_DV)
            o_ref[:, o:o + RET_DV] = (gate * _sigmoid(gate) * (yh * gn_ref[:, o:o + RET_DV])).astype(o_ref.dtype)

    row = pl.BlockSpec((tr, 1024), lambda i: (i, 0))
    return pl.pallas_call(
        body, name="retention_post_fwd", grid=(S // tr,),
        in_specs=[row, pl.BlockSpec((tr, 1024), lambda i: (i, OFF_GA // 1024)), pl.BlockSpec((1, 1024), lambda i: (0, 0))],
        out_specs=row, out_shape=jax.ShapeDtypeStruct((S, 1024), BF), compiler_params=_params(("parallel",)),
    )(y, proj, gn)


def _ret_post_bwd(y, proj, gn, do):
    S = y.shape[0]
    tr = _pick(S, (512,))

    def body(y_ref, g_ref, gn_ref, do_ref, dy_ref, dg_ref, dgn_ref):
        @pl.when(pl.program_id(0) == 0)
        def _():
            dgn_ref[...] = jnp.zeros_like(dgn_ref)

        for h in range(RET_HEADS):
            o = h * RET_DV
            yh, r = _head_norm(_cols(y_ref, o, RET_DV))
            gate = _cols(g_ref, o, RET_DV)
            gnh = gn_ref[:, o:o + RET_DV]
            dout = _cols(do_ref, o, RET_DV).astype(F32)
            sg = _sigmoid(gate)
            dz = dout * (gate * sg)
            dg_ref[:, o:o + RET_DV] = (dout * (yh * gnh) * (sg * (1.0 + gate * (1.0 - sg)))).astype(dg_ref.dtype)
            dgn_ref[:, o:o + RET_DV] += _row_sum(dz * yh)
            dy_ref[:, o:o + RET_DV] = _head_norm_bwd(dz * gnh, yh, r).astype(dy_ref.dtype)

    row = pl.BlockSpec((tr, 1024), lambda i: (i, 0))
    vec = pl.BlockSpec((1, 1024), lambda i: (0, 0))
    return pl.pallas_call(
        body, name="retention_post_bwd", grid=(S // tr,),
        in_specs=[row, pl.BlockSpec((tr, 1024), lambda i: (i, OFF_GA // 1024)), vec, row],
        out_specs=[row, row, vec],
        out_shape=[jax.ShapeDtypeStruct((S, 1024), BF), jax.ShapeDtypeStruct((S, 1024), BF),
                   jax.ShapeDtypeStruct((1, 1024), F32)],
        compiler_params=_params(("arbitrary",)),
    )(y, proj, gn, do)


def _t5_bucket_map():
    r = jnp.arange(BLOCK)
    j = jnp.arange(3 * BLOCK)
    rel = j[None, :] - BLOCK - r[:, None]
    nb = T5_BUCKETS // 2
    max_exact = nb // 2
    ret = jnp.where(rel > 0, nb, 0)
    n = jnp.abs(rel)
    nf = jnp.maximum(n, 1).astype(jnp.float32)
    large = max_exact + (jnp.log(nf / max_exact) / math.log(T5_MAX_DIST / max_exact)
                         * (nb - max_exact)).astype(jnp.int32)
    large = jnp.minimum(large, nb - 1)
    bucket = ret + jnp.where(n < max_exact, n, large)
    return jnp.where(jnp.abs(rel) <= WINDOW, bucket, -1).astype(jnp.int32)


SWA_G = SWA_HEADS // SWA_KV_HEADS
SWA_LANES = SWA_G * BLOCK


def _t5_bias(table, bucket_t):
    def body(t_ref, b_ref, o_ref):
        bk = b_ref[...]
        for h in range(SWA_HEADS):
            acc = jnp.full(bk.shape, NEG_INF, F32)
            for b in range(T5_BUCKETS):
                acc = jnp.where(bk == b, t_ref[b, h], acc)
            o_ref[h // SWA_G, :, (h % SWA_G) * BLOCK:(h % SWA_G + 1) * BLOCK] = acc

    return pl.pallas_call(
        body, name="t5_bias", in_specs=[SMEM_SPEC, pl.BlockSpec((3 * BLOCK, BLOCK), lambda: (0, 0))],
        out_specs=pl.BlockSpec((SWA_KV_HEADS, 3 * BLOCK, SWA_LANES), lambda: (0, 0, 0)),
        out_shape=jax.ShapeDtypeStruct((SWA_KV_HEADS, 3 * BLOCK, SWA_LANES), F32),
    )(table, bucket_t)


def _t5_table_grad(dbias, bucket_t):
    def body(d_ref, b_ref, o_ref):
        bk = b_ref[...]
        lane = lax.broadcasted_iota(jnp.int32, (1, 128), 1)
        for b in range(T5_BUCKETS):
            hit = bk == b
            row = jnp.zeros((1, 128), F32)
            for h in range(SWA_HEADS):
                d = d_ref[h // SWA_G, :, (h % SWA_G) * BLOCK:(h % SWA_G + 1) * BLOCK]
                row = row + jnp.where(lane == h, _all_sum(jnp.where(hit, d, 0.0)), 0.0)
            o_ref[b:b + 1, :] = row

    return pl.pallas_call(
        body, name="t5_table_grad",
        in_specs=[pl.BlockSpec((SWA_KV_HEADS, 3 * BLOCK, SWA_LANES), lambda: (0, 0, 0)),
                  pl.BlockSpec((3 * BLOCK, BLOCK), lambda: (0, 0))],
        out_specs=pl.BlockSpec((T5_BUCKETS, 128), lambda: (0, 0)),
        out_shape=jax.ShapeDtypeStruct((T5_BUCKETS, 128), F32),
    )(dbias, bucket_t)


def _swa_scores(i, nb, q4, kw, bias_t, sink_row):
    s = _dot(kw, q4, NT) + bias_t
    row = lax.broadcasted_iota(jnp.int32, s.shape, 0)
    first_row = jnp.where(i == 0, BLOCK, 0)
    end_row = jnp.where(i == nb - 1, 2 * BLOCK, 3 * BLOCK)
    s = jnp.where((row < first_row) | (row >= end_row), NEG_INF, s)
    m = jnp.maximum(jnp.max(s, axis=0, keepdims=True), sink_row)
    p = jnp.exp(s - m)
    e_sink = jnp.exp(sink_row - m)
    inv = 1.0 / (jnp.sum(p, axis=0, keepdims=True) + e_sink)
    return p * inv, e_sink * inv


def _swa_group(q_ref, sink_ref, kh):
    heads = range(kh * SWA_G, (kh + 1) * SWA_G)
    q4 = jnp.concatenate([_cols(q_ref, h * HEAD_DIM) for h in heads], axis=0)
    sink_row = jnp.concatenate([jnp.full((1, BLOCK), sink_ref[0, h], F32) for h in heads], axis=1)
    return q4, sink_row


def _swa_unstack(ref, kh, x_t):
    for g in range(SWA_G):
        h = kh * SWA_G + g
        ref[:, h * HEAD_DIM:(h + 1) * HEAD_DIM] = x_t[:, g * BLOCK:(g + 1) * BLOCK].T.astype(ref.dtype)


def _swa_window(ref, i, nb, off):
    prev, nxt = jnp.maximum(i - 1, 0), jnp.minimum(i + 1, nb - 1)
    rows = [pl.ds(pl.multiple_of(b * BLOCK, BLOCK), BLOCK) for b in (prev, i, nxt)]
    return jnp.concatenate([ref[r, off:off + HEAD_DIM] for r in rows], axis=0), rows


def _swa_fwd(q, k, v, bias, sink, carried=None):
    S = q.shape[0]
    nb = S // BLOCK

    def body(sink_ref, q_ref, k_ref, v_ref, bias_ref, o_ref):
        i = pl.program_id(0)
        for kh in range(SWA_KV_HEADS):
            kw, _ = _swa_window(k_ref, i, nb, kh * HEAD_DIM)
            vw, _ = _swa_window(v_ref, i, nb, kh * HEAD_DIM)
            q4, sink_row = _swa_group(q_ref, sink_ref, kh)
            p, _ = _swa_scores(i, nb, q4, kw, bias_ref[kh], sink_row)
            _swa_unstack(o_ref, kh, _dot(vw, p.astype(vw.dtype), TN))

    full_kv = pl.BlockSpec((S, SWA_KV_HEADS * HEAD_DIM), lambda i: (0, 0))
    (o,), landed = _carry_call(
        body, (sink, q, k, v, bias), name="swa_fwd", grid=(nb,),
        in_specs=[SMEM_SPEC, pl.BlockSpec((BLOCK, 1024), lambda i: (i, 0)), full_kv, full_kv,
                  pl.BlockSpec((SWA_KV_HEADS, 3 * BLOCK, SWA_LANES), lambda i: (0, 0, 0))],
        out_specs=[pl.BlockSpec((BLOCK, 1024), lambda i: (i, 0))],
        out_shape=[jax.ShapeDtypeStruct((S, 1024), BF)], semantics=("parallel",), carried=carried)
    return o, landed


def _swa_bwd(q, k, v, do, bias, sink, carried=None):
    S = q.shape[0]
    nb = S // BLOCK

    def body(sink_ref, q_ref, k_ref, v_ref, do_ref, bias_ref, dq_ref, dk_ref, dv_ref, dbias_ref, dsink_ref):
        i = pl.program_id(0)

        @pl.when(i == 0)
        def _():
            dk_ref[...] = jnp.zeros_like(dk_ref)
            dv_ref[...] = jnp.zeros_like(dv_ref)
            dbias_ref[...] = jnp.zeros_like(dbias_ref)
            dsink_ref[...] = jnp.zeros_like(dsink_ref)

        for kh in range(SWA_KV_HEADS):
            off = kh * HEAD_DIM
            kw, rows = _swa_window(k_ref, i, nb, off)
            vw, _ = _swa_window(v_ref, i, nb, off)
            q4, sink_row = _swa_group(q_ref, sink_ref, kh)
            p, p_sink = _swa_scores(i, nb, q4, kw, bias_ref[kh], sink_row)
            do4 = jnp.concatenate([_cols(do_ref, (kh * SWA_G + g) * HEAD_DIM) for g in range(SWA_G)], axis=0).astype(vw.dtype)
            dp = _dot(vw, do4, NT)
            delta = jnp.sum(p * dp, axis=0, keepdims=True)
            ds = p * (dp - delta)
            dsb = ds.astype(q4.dtype)
            _swa_unstack(dq_ref, kh, _dot(kw, dsb, TN))
            dkw = _dot(dsb, q4)
            dvw = _dot(p.astype(do4.dtype), do4)
            dbias_ref[kh] += ds
            sink_term = p_sink * delta
            for g in range(SWA_G):
                h = kh * SWA_G + g
                dsink_ref[h:h + 1, :] += jnp.broadcast_to(-_all_sum(sink_term[:, g * BLOCK:(g + 1) * BLOCK]), (1, 128))
            for b, r in enumerate(rows):
                dk_ref[r, off:off + HEAD_DIM] += dkw[b * BLOCK:(b + 1) * BLOCK]
                dv_ref[r, off:off + HEAD_DIM] += dvw[b * BLOCK:(b + 1) * BLOCK]

    full_kv = pl.BlockSpec((S, SWA_KV_HEADS * HEAD_DIM), lambda i: (0, 0))
    blk = pl.BlockSpec((BLOCK, 1024), lambda i: (i, 0))
    bias_spec = pl.BlockSpec((SWA_KV_HEADS, 3 * BLOCK, SWA_LANES), lambda i: (0, 0, 0))
    outs, landed = _carry_call(
        body, (sink, q, k, v, do, bias), name="swa_bwd", grid=(nb,),
        in_specs=[SMEM_SPEC, blk, full_kv, full_kv, blk, bias_spec],
        out_specs=[blk, full_kv, full_kv, bias_spec, pl.BlockSpec((8, 128), lambda i: (0, 0))],
        out_shape=[jax.ShapeDtypeStruct((S, 1024), BF), jax.ShapeDtypeStruct((S, 256), F32),
                   jax.ShapeDtypeStruct((S, 256), F32),
                   jax.ShapeDtypeStruct((SWA_KV_HEADS, 3 * BLOCK, SWA_LANES), F32), jax.ShapeDtypeStruct((8, 128), F32)],
        vmem=VMEM_LARGE, semantics=("arbitrary",), carried=carried)
    return (*outs, landed)


def _prep_odd_fwd(proj, cos, sin, gq, gk):
    S = proj.shape[0]
    tr = _pick(S, (512,))

    def body(p_ref, cos_ref, sin_ref, gq_ref, gk_ref, q_ref, k_ref, v_ref):
        cos_v, sin_v = cos_ref[...], sin_ref[...]
        for h in range(AX_HEADS):
            o = h * HEAD_DIM
            xh, _ = _head_norm(_cols(p_ref, o))
            q_ref[:, o:o + HEAD_DIM] = (_rope(xh * gq_ref[...], cos_v, sin_v, 32) * AX_SCALE).astype(q_ref.dtype)
        for h in range(AX_KV_HEADS):
            o = h * HEAD_DIM
            xh, _ = _head_norm(_cols(p_ref, 1024 + o))
            k_ref[:, o:o + HEAD_DIM] = _rope(xh * gk_ref[...], cos_v, sin_v, 32).astype(k_ref.dtype)
        v_ref[...] = p_ref[:, 1280:1536].astype(v_ref.dtype)

    def row(w):
        return pl.BlockSpec((tr, w), lambda i: (i, 0))

    vec = pl.BlockSpec((1, HEAD_DIM), lambda i: (0, 0))
    return pl.pallas_call(
        body, name="prep_odd_fwd", grid=(S // tr,),
        in_specs=[row(ODD_IN), row(128), row(128), vec, vec], out_specs=[row(1024), row(256), row(256)],
        out_shape=[jax.ShapeDtypeStruct((S, w), BF) for w in (1024, 256, 256)],
        compiler_params=_params(("parallel",)),
    )(proj, cos, sin, gq, gk)


def _prep_odd_bwd(proj, cos, sin, gq, gk, dq, dk, dv):
    S = proj.shape[0]
    tr = _pick(S, (512,))

    def body(p_ref, cos_ref, sin_ref, gq_ref, gk_ref, dq_ref, dk_ref, dv_ref, dp_ref, dgq_ref, dgk_ref):
        cos_v, sin_v = cos_ref[...], sin_ref[...]
        dt = dp_ref.dtype
        dgq = jnp.zeros((1, HEAD_DIM), F32)
        for h in range(AX_HEADS):
            o = h * HEAD_DIM
            xh, r = _head_norm(_cols(p_ref, o))
            dy = _rope_t(_cols(dq_ref, o).astype(F32) * AX_SCALE, cos_v, sin_v, 32)
            dgq = dgq + _row_sum(dy * xh)
            dp_ref[:, o:o + HEAD_DIM] = _head_norm_bwd(dy * gq_ref[...], xh, r).astype(dt)
        dgk = jnp.zeros((1, HEAD_DIM), F32)
        for h in range(AX_KV_HEADS):
            o = h * HEAD_DIM
            xh, r = _head_norm(_cols(p_ref, 1024 + o))
            dy = _rope_t(_cols(dk_ref, o), cos_v, sin_v, 32)
            dgk = dgk + _row_sum(dy * xh)
            dp_ref[:, 1024 + o:1024 + o + HEAD_DIM] = _head_norm_bwd(dy * gk_ref[...], xh, r).astype(dt)
        dp_ref[:, 1280:1536] = dv_ref[...].astype(dt)

        @pl.when(pl.program_id(0) == 0)
        def _():
            dgq_ref[...] = jnp.zeros_like(dgq_ref)
            dgk_ref[...] = jnp.zeros_like(dgk_ref)

        dgq_ref[...] += dgq
        dgk_ref[...] += dgk

    def row(w):
        return pl.BlockSpec((tr, w), lambda i: (i, 0))

    vec = pl.BlockSpec((1, HEAD_DIM), lambda i: (0, 0))
    return pl.pallas_call(
        body, name="prep_odd_bwd", grid=(S // tr,),
        in_specs=[row(ODD_IN), row(128), row(128), vec, vec, row(1024), row(256), row(256)],
        out_specs=[row(ODD_IN), vec, vec],
        out_shape=[jax.ShapeDtypeStruct((S, ODD_IN), BF), jax.ShapeDtypeStruct((1, HEAD_DIM), F32),
                   jax.ShapeDtypeStruct((1, HEAD_DIM), F32)],
        compiler_params=_params(("arbitrary",)),
    )(proj, cos, sin, gq, gk, dq, dk, dv)


def _loop_unrolled(n, factor, step, init):
    while n % factor:
        factor //= 2

    def trip(t, carry):
        for u in range(factor):
            carry = step(factor * t + u, carry)
        return carry

    return lax.fori_loop(0, n // factor, trip, init)


def _flash_fwd(q, k, v, carried=None):
    S = q.shape[0]
    tq = _pick(S, (512,))
    tk = _pick(S, (1024, 512))
    nk = S // tk
    G = AX_HEADS // AX_KV_HEADS

    def body(q_ref, k_ref, v_ref, o_ref, lse_ref):
        qv = q_ref[...]

        def step(j, carry):
            m, l, acc = carry
            rows = pl.ds(pl.multiple_of(j * tk, tk), tk)
            s = _dot(qv, k_ref[rows, :], NT)
            m_new = jnp.maximum(m, jnp.max(s, axis=-1, keepdims=True))
            alpha = jnp.exp2(m - m_new)
            p = jnp.exp2(s - m_new)
            l = alpha * l + jnp.sum(p, axis=-1, keepdims=True)
            acc = alpha * acc + _dot(p.astype(v_ref.dtype), v_ref[rows, :])
            return m_new, l, acc

        init = (jnp.full((tq, 1), NEG_INF, F32), jnp.zeros((tq, 1), F32), jnp.zeros((tq, HEAD_DIM), F32))
        m, l, acc = _loop_unrolled(nk, 8, step, init)
        o_ref[...] = (acc / l).astype(o_ref.dtype)
        lse_ref[0] = jnp.broadcast_to(m + jnp.log2(l), (tq, 128))

    (o, lse), landed = _carry_call(
        body, (q, k, v), name="flash_fwd", grid=(AX_HEADS, S // tq),
        in_specs=[pl.BlockSpec((tq, HEAD_DIM), lambda h, i: (i, h)),
                  pl.BlockSpec((S, HEAD_DIM), lambda h, i: (0, h // G)),
                  pl.BlockSpec((S, HEAD_DIM), lambda h, i: (0, h // G))],
        out_specs=[pl.BlockSpec((tq, HEAD_DIM), lambda h, i: (i, h)),
                   pl.BlockSpec((1, tq, 128), lambda h, i: (h, i, 0))],
        out_shape=[jax.ShapeDtypeStruct((S, AX_HEADS * HEAD_DIM), BF), jax.ShapeDtypeStruct((AX_HEADS, S, 128), F32)],
        semantics=("parallel", "parallel"), carried=carried)
    return o, lse, landed


def _flash_bwd(q, k, v, o, do, lse, carried=None):
    S = q.shape[0]
    tq = _pick(S, (512,))
    tk = _pick(S, (1024, 512))
    nq, nk = S // tq, S // tk
    G = AX_HEADS // AX_KV_HEADS

    def body(q_ref, k_ref, v_ref, o_ref, do_ref, lse_ref, dq_ref, dk_ref, dv_ref):
        g, i = pl.program_id(1), pl.program_id(2)

        @pl.when((g == 0) & (i == 0))
        def _():
            dk_ref[...] = jnp.zeros_like(dk_ref)
            dv_ref[...] = jnp.zeros_like(dv_ref)

        qv = q_ref[...]
        do_f = do_ref[...].astype(F32)
        dob = do_f.astype(qv.dtype)
        dob_ln2 = (do_f * LN2).astype(qv.dtype)
        delta = jnp.sum(do_f * o_ref[...].astype(F32), axis=-1, keepdims=True) * LN2
        lse_col = lse_ref[0][:, 0:1]

        def step(j, dq):
            rows = pl.ds(pl.multiple_of(j * tk, tk), tk)
            kj, vj = k_ref[rows, :], v_ref[rows, :]
            p = jnp.exp2(_dot(qv, kj, NT) - lse_col)
            dp = _dot(dob_ln2, vj, NT)
            ds = (p * (dp - delta)).astype(qv.dtype)
            dk_ref[rows, :] += _dot(ds, qv, TN)
            dv_ref[rows, :] += _dot(p.astype(dob.dtype), dob, TN)
            return dq + _dot(ds, kj)

        dq_ref[...] = _loop_unrolled(nk, 4, step, jnp.zeros((tq, HEAD_DIM), F32)).astype(dq_ref.dtype)

    q_spec = pl.BlockSpec((tq, HEAD_DIM), lambda kh, g, i: (i, kh * G + g))
    kv_spec = pl.BlockSpec((S, HEAD_DIM), lambda kh, g, i: (0, kh))
    (dq, dk, dv), landed = _carry_call(
        body, (q, k, v, o, do, lse), name="flash_bwd", grid=(AX_KV_HEADS, G, nq),
        in_specs=[q_spec, kv_spec, kv_spec, q_spec, q_spec,
                  pl.BlockSpec((1, tq, 128), lambda kh, g, i: (kh * G + g, i, 0))],
        out_specs=[q_spec, kv_spec, kv_spec],
        out_shape=[jax.ShapeDtypeStruct((S, AX_HEADS * HEAD_DIM), BF), jax.ShapeDtypeStruct((S, 256), F32),
                   jax.ShapeDtypeStruct((S, 256), F32)],
        vmem=VMEM_LARGE, semantics=("arbitrary", "arbitrary", "arbitrary"), carried=carried)
    return dq, dk, dv, landed


def _rope_angles(pos, dim, theta):
    inv = theta ** (-jnp.arange(0, dim, 2, dtype=jnp.float32) / dim)
    return pos.astype(jnp.float32)[:, None] * inv[None, :]


def _rope_tables(S):
    ang = _rope_angles(jnp.arange(S), RET_DK, RET_THETA)
    c, s = jnp.cos(ang), jnp.sin(ang)
    ret = (jnp.concatenate([c, c], -1), jnp.concatenate([-s, s], -1))
    rows = S // GRID_W
    ar, ac = _rope_angles(jnp.arange(rows), HEAD_DIM // 2, AX_THETA), _rope_angles(jnp.arange(GRID_W), HEAD_DIM // 2, AX_THETA)
    cr, sr = jnp.repeat(jnp.cos(ar), GRID_W, axis=0), jnp.repeat(jnp.sin(ar), GRID_W, axis=0)
    cc, sc = jnp.tile(jnp.cos(ac), (rows, 1)), jnp.tile(jnp.sin(ac), (rows, 1))
    ax = (jnp.concatenate([cr, cr, cc, cc], -1), jnp.concatenate([-sr, sr, -sc, sc], -1))
    return ret, ax


def _pad_tile(a):
    return jnp.pad(a.astype(F32), ((0, 8 - a.shape[0]), (0, 128 - a.shape[1])))


def _relu2_epilogue(acc):
    r = jnp.maximum(acc, 0.0)
    return acc, r * r


def _relu2_bwd_epilogue(acc, u):
    return (acc * (2.0 * jnp.maximum(u.astype(F32), 0.0)),)


def _add_epilogue(acc, res):
    return (acc + res,)


def _mlp_fwd(x, g, w_up, w_down, tag):
    h = _rms_fwd(x, g, f"mlp_norm_{tag}")
    u, a = _mm(h, w_up, name=f"mlp_up_{tag}", out_dtypes=(BF, BF), epilogue=_relu2_epilogue)
    y = _mm(a, w_down, name=f"mlp_down_{tag}", extras=(x,), epilogue=_add_epilogue)
    return y, (h, u, a)


def _mlp_bwd(x, g, w_up, w_down, saved, dy, dyb, tag, mm_exchange=None):
    h, u, a = saved
    du = _mm(dyb, w_down, tb=True, name=f"mlp_down_dx_{tag}", out_dtypes=(BF,), extras=(u,), epilogue=_relu2_bwd_epilogue)
    dw_down = _mm(a, dyb, ta=True, name=f"mlp_down_dw_{tag}", out_dtypes=(BF,))
    dw_up = _mm(h, du, ta=True, name=f"mlp_up_dw_{tag}", out_dtypes=(BF,))
    if mm_exchange is None:
        dh = _mm(du, w_up, tb=True, name=f"mlp_up_dx_{tag}")
    else:
        dh = mm_exchange([("down" + tag, dw_down)], du, w_up, tb=True, name=f"mlp_up_dx_{tag}")
    dx, dxb, dg = _rms_bwd(x, g, dh, dy, f"mlp_norm_bwd_{tag}")
    return dx, dxb, dg, dw_up, dw_down


COL_SHARDED = ("in_even", "in_odd", "up0", "up1")


def _assemble(key, g):
    if key in COL_SHARDED:
        return g.transpose(1, 0, 2).reshape(g.shape[1], N_DEV * g.shape[2])
    return g.reshape(N_DEV * g.shape[1], g.shape[2])


def _split(key, full):
    rows, cols = full.shape
    if key in COL_SHARDED:
        return full.reshape(rows, N_DEV, cols // N_DEV).transpose(1, 0, 2)
    return full.reshape(N_DEV, rows // N_DEV, cols)


def _local_step(x, target, W, P, late=None):
    S = x.shape[0]
    W = dict(W)
    landed = {}

    def gather_while(stage):
        return None if late is None else _Exchange([s for _, s in late[stage]], gather=True)

    def arrived(stage, outs):
        for (key, _), g in zip([] if late is None else late[stage], outs):
            W[key] = _assemble(key, g)

    def exchange_while(grads):
        return None if late is None else _Exchange([_split(k, g) for k, g in grads], gather=False)

    def left(grads, outs):
        for (key, _), l in zip(grads, outs):
            landed[key] = l

    def mm_gather(stage, *args, **kw):
        if late is None:
            return _mm(*args, **kw)
        out, outs = _mm(*args, carried=gather_while(stage), **kw)
        arrived(stage, outs)
        return out

    def mm_exchange(grads, *args, **kw):
        if late is None:
            return _mm(*args, **kw)
        out, outs = _mm(*args, carried=exchange_while(grads), **kw)
        left(grads, outs)
        return out

    (cos_r, sin_r), (cos_a, sin_a) = _rope_tables(S)
    bucket = _t5_bucket_map().T
    logit = P["ret_decay_logit"]
    lg = _log_sigmoid_tile(_pad_tile(logit))
    bias = _t5_bias(P["t5_table"], bucket)
    nmix, nmlp = P["norm_mix"], P["norm_mlp"]

    h0 = _rms_fwd(x, nmix[0:1], "mix_norm_0")
    proj_e = mm_gather("in_even", h0, W["in_even"], name="in_even")
    (qr, kr, vr, qs, ks, vs), outs = _prep_even_fwd(proj_e, cos_r, sin_r, P["swa_q_norm"], P["swa_k_norm"],
                                                    gather_while("prep_even_fwd"))
    arrived("prep_even_fwd", outs)
    sf, sb = _ret_scan(kr, vr, lg, (0, 0), (1, 0), "retention_states")
    y_ret = _ret_fwd(qr, kr, vr, lg, sf, sb)
    oa = _ret_post_fwd(y_ret, proj_e, P["ret_norm"])
    ob, outs = _swa_fwd(qs, ks, vs, bias, P["swa_sink"], gather_while("swa_fwd"))
    arrived("swa_fwd", outs)
    wo_a, wo_b = W["out_even"][:1024], W["out_even"][1024:]
    x1 = _mm(oa, wo_a, name="out_even_a", extras=(x,), epilogue=_add_epilogue)
    x1 = _mm(ob, wo_b, name="out_even_b", extras=(x1,), epilogue=_add_epilogue)
    x2, mlp0 = _mlp_fwd(x1, nmlp[0:1], W["up0"], W["down0"], "0")

    h2 = _rms_fwd(x2, nmix[1:2], "mix_norm_1")
    proj_o = _mm(h2, W["in_odd"], name="in_odd")
    qx, kx, vx = _prep_odd_fwd(proj_o, cos_a, sin_a, P["ax_q_norm"], P["ax_k_norm"])
    ox, lse, outs = _flash_fwd(qx, kx, vx, gather_while("flash_fwd"))
    arrived("flash_fwd", outs)
    x3 = _mm(ox, W["out_odd"], name="out_odd", extras=(x2,), epilogue=_add_epilogue)
    x4, mlp1 = _mlp_fwd(x3, nmlp[1:2], W["up1"], W["down1"], "1")

    d4, d4b, loss_tile = _loss_and_grad(x4, target)

    d3, d3b, dnmlp1, dw_up1, dw_down1 = _mlp_bwd(x3, nmlp[1:2], W["up1"], W["down1"], mlp1, d4, d4b, "1")
    dox = _mm(d3b, W["out_odd"], tb=True, name="out_odd_dx")
    dw_out_odd = _mm(ox, d3b, ta=True, name="out_odd_dw", out_dtypes=(BF,))
    grads1 = [("up1", dw_up1), ("down1", dw_down1), ("out_odd", dw_out_odd)]
    dqx, dkx, dvx, outs = _flash_bwd(qx, kx, vx, ox, dox, lse, exchange_while(grads1))
    left(grads1, outs)
    dproj_o, dgq_ax, dgk_ax = _prep_odd_bwd(proj_o, cos_a, sin_a, P["ax_q_norm"], P["ax_k_norm"], dqx, dkx, dvx)
    dw_in_odd = _mm(h2, dproj_o, ta=True, name="in_odd_dw", out_dtypes=(BF,))
    dh2 = _mm(dproj_o, W["in_odd"], tb=True, name="in_odd_dx")
    d2, d2b, dnmix1 = _rms_bwd(x2, nmix[1:2], dh2, d3, "mix_norm_bwd_1")

    d1, d1b, dnmlp0, dw_up0, dw_down0 = _mlp_bwd(x1, nmlp[0:1], W["up0"], W["down0"], mlp0, d2, d2b, "0", mm_exchange)
    doa = _mm(d1b, wo_a, tb=True, name="out_even_a_dx")
    dob = _mm(d1b, wo_b, tb=True, name="out_even_b_dx")
    dw_out_even = jnp.concatenate([_mm(oa, d1b, ta=True, name="out_even_a_dw", out_dtypes=(BF,)),
                                   _mm(ob, d1b, ta=True, name="out_even_b_dw", out_dtypes=(BF,))], axis=0)
    dy_ret, dga, dret_norm = _ret_post_bwd(y_ret, proj_e, P["ret_norm"], doa)
    hb, hf = _ret_scan(qr, dy_ret, lg, (1, 1), (0, 1), "retention_state_grads")
    dqr, dkr, dvr, dlogit = _ret_bwd(qr, kr, vr, dy_ret, lg, logit, sf, sb, hf, hb)
    grads0 = [("in_odd", dw_in_odd), ("out_even", dw_out_even)]
    dqs, dks, dvs, dbias, dsink, outs = _swa_bwd(qs, ks, vs, dob, bias, P["swa_sink"], exchange_while(grads0))
    left(grads0, outs)
    dt5 = _t5_table_grad(dbias, bucket)
    dproj_e, dgq_swa, dgk_swa = _prep_even_bwd(proj_e, cos_r, sin_r, P["swa_q_norm"], P["swa_k_norm"],
                                               dqr, dkr, dvr, dga, dqs, dks, dvs)
    dw_in_even = mm_exchange([("up0", dw_up0)], h0, dproj_e, ta=True, name="in_even_dw", out_dtypes=(BF,))
    dh0 = mm_exchange([("in_even", dw_in_even)], dproj_e, W["in_even"], tb=True, name="in_even_dx")
    dx, _, dnmix0 = _rms_bwd(x, nmix[0:1], dh0, d1, "mix_norm_bwd_0")

    if late is None:
        dW = dict(grads1 + grads0, up0=dw_up0, down0=dw_down0, in_even=dw_in_even)
    else:
        dW = landed
    dP = {"norm_mix": jnp.concatenate([dnmix0, dnmix1], 0), "norm_mlp": jnp.concatenate([dnmlp0, dnmlp1], 0),
          "ret_decay_logit": dlogit[:, :, 0, 0].T, "ret_norm": dret_norm,
          "swa_q_norm": dgq_swa, "swa_k_norm": dgk_swa, "swa_sink": dsink[:, 0][None, :],
          "t5_table": dt5[:, :SWA_HEADS], "ax_q_norm": dgq_ax, "ax_k_norm": dgk_ax}
    return loss_tile, dx, dW, dP


def _cast_shards(shards):
    n = len(shards)

    def body(*refs):
        for i_ref, o_ref in zip(refs[:n], refs[n:]):
            o_ref[...] = i_ref[...].astype(o_ref.dtype)

    return pl.pallas_call(body, name="cast_shards", in_specs=[VMEM_SPEC] * n, out_specs=[VMEM_SPEC] * n,
                          out_shape=[jax.ShapeDtypeStruct(s.shape, BF) for s in shards],
                          compiler_params=pltpu.CompilerParams(vmem_limit_bytes=VMEM_SMALL))(*shards)


def _all_gather(shards):
    n = len(shards)

    def body(*refs):
        ins, outs, stage = refs[:n], refs[n:2 * n], refs[2 * n:3 * n]
        send_sems, recv_sems, local_sems = refs[3 * n:]
        me = _my_place()
        sibling = _flip(me, 1)
        chips = [_flip(me, 4), _flip(me, 2), _flip(me, 6)]

        def copy(a, k, block, to, src=None):
            dst = outs[a].at[_index(block)]
            return pltpu.make_async_remote_copy(
                src_ref=dst if src is None else src, dst_ref=dst,
                send_sem=send_sems.at[a, k], recv_sem=recv_sems.at[a, k], device_id=to, device_id_type=MESH)

        first, mine = [], []
        for a in range(n):
            stage[a][...] = ins[a][...].astype(stage[a].dtype)
            mine.append(pltpu.make_async_copy(stage[a], outs[a].at[_index(me)], local_sems.at[a]))
            mine[-1].start()
            first.append(copy(a, 0, me, sibling, src=stage[a]))
            first += [copy(a, 1 + j, me, chip, src=stage[a]) for j, chip in enumerate(chips)]
        for cp in first:
            cp.start()
        passed = []
        for a in range(n):
            for j, chip in enumerate(chips):
                copy(a, 1 + j, chip, me).wait_recv()
                passed.append(copy(a, 4 + j, chip, sibling))
                passed[-1].start()
        for a in range(n):
            copy(a, 0, sibling, me).wait_recv()
            for j, chip in enumerate(chips):
                copy(a, 4 + j, _flip(chip, 1), me).wait_recv()
        for cp in first + passed:
            cp.wait_send()
        for cp in mine:
            cp.wait()

    return pl.pallas_call(
        body, name="weights_all_gather",
        in_specs=[VMEM_SPEC] * n, out_specs=[ANY_SPEC] * n,
        out_shape=[jax.ShapeDtypeStruct((N_DEV,) + s.shape, BF) for s in shards],
        scratch_shapes=[pltpu.VMEM(s.shape, BF) for s in shards]
        + [pltpu.SemaphoreType.DMA((n, 7)), pltpu.SemaphoreType.DMA((n, 7)), pltpu.SemaphoreType.DMA((n,))],
        compiler_params=pltpu.CompilerParams(vmem_limit_bytes=VMEM_SMALL),
    )(*shards)


def _all_reduce_small(part):
    R, C = part.shape

    def body(x_ref, o_ref, land_ref, send_sems, recv_sems):
        me = _my_place()
        land_ref[_index(me)] = x_ref[...]
        copies = []
        for k in range(1, N_DEV):
            peer = _flip(me, k)
            copies.append(pltpu.make_async_remote_copy(
                src_ref=x_ref, dst_ref=land_ref.at[_index(me)],
                send_sem=send_sems.at[k - 1], recv_sem=recv_sems.at[k - 1], device_id=peer, device_id_type=MESH))
            copies[-1].start()
        for k in range(1, N_DEV):
            peer = _flip(me, k)
            pltpu.make_async_remote_copy(
                src_ref=x_ref, dst_ref=land_ref.at[_index(peer)],
                send_sem=send_sems.at[k - 1], recv_sem=recv_sems.at[k - 1], device_id=peer, device_id_type=MESH).wait_recv()
        for cp in copies:
            cp.wait_send()
        acc = land_ref[0]
        for s in range(1, N_DEV):
            acc = acc + land_ref[s]
        o_ref[...] = acc

    return pl.pallas_call(
        body, name="small_all_reduce", in_specs=[VMEM_SPEC], out_specs=VMEM_SPEC,
        out_shape=jax.ShapeDtypeStruct((R, C), F32),
        scratch_shapes=[pltpu.VMEM((N_DEV, R, C), F32), pltpu.SemaphoreType.DMA((7,)), pltpu.SemaphoreType.DMA((7,))],
    )(part)


def _adamw_math(w, g, m, v):
    m = ADAM_B1 * m + (1.0 - ADAM_B1) * g
    v = ADAM_B2 * v + (1.0 - ADAM_B2) * jnp.square(g)
    m_hat = m / (1.0 - ADAM_B1 ** ADAM_STEP)
    v_hat = v / (1.0 - ADAM_B2 ** ADAM_STEP)
    delta = -ADAM_LR * (m_hat / (jnp.sqrt(v_hat) + ADAM_EPS) + ADAM_WD * w)
    return delta, m, v


def _sum_and_adamw(landed, w, m, v, name):
    R, C = w.shape
    tr = _pick(R, (256, 128))

    def body(l_ref, w_ref, m_ref, v_ref, g_ref, d_ref, nm_ref, nv_ref):
        g = l_ref[0].astype(F32)
        for s in range(1, N_DEV):
            g = g + l_ref[s].astype(F32)
        g_ref[...] = g
        d_ref[...], nm_ref[...], nv_ref[...] = _adamw_math(w_ref[...], g, m_ref[...], v_ref[...])

    row = pl.BlockSpec((tr, C), lambda i: (i, 0))
    return pl.pallas_call(
        body, name=name, grid=(R // tr,),
        in_specs=[pl.BlockSpec((N_DEV, tr, C), lambda i: (0, i, 0)), row, row, row], out_specs=[row] * 4,
        out_shape=[jax.ShapeDtypeStruct((R, C), F32)] * 4, compiler_params=_params(("parallel",)),
    )(landed, w, m, v)


def _adamw_small(w, g, m, v):
    def body(w_ref, g_ref, m_ref, v_ref, d_ref, nm_ref, nv_ref):
        d_ref[...], nm_ref[...], nv_ref[...] = _adamw_math(w_ref[...], g_ref[...], m_ref[...], v_ref[...])

    full = pl.BlockSpec(w.shape, lambda: (0, 0))
    return pl.pallas_call(body, name="adamw_small", in_specs=[full] * 4, out_specs=[full] * 3,
                          out_shape=[jax.ShapeDtypeStruct(w.shape, F32)] * 3)(w, g, m, v)


MATRICES = ("w_in_even", "w_out_even", "w_in_odd", "w_out_odd", "w_mlp_up", "w_mlp_down")
SMALL = ("norm_mix", "norm_mlp", "ret_decay_logit", "ret_norm", "swa_q_norm", "swa_k_norm", "swa_sink",
         "t5_table", "ax_q_norm", "ax_k_norm")
MATRIX_OF = {"in_even": ("w_in_even", 0), "out_even": ("w_out_even", 0), "in_odd": ("w_in_odd", 0),
             "out_odd": ("w_out_odd", 0), "up0": ("w_mlp_up", 0), "up1": ("w_mlp_up", 1),
             "down0": ("w_mlp_down", 0), "down1": ("w_mlp_down", 1)}
GATHER_FIRST = ("in_even",)
GATHER_WHILE = {"in_even": ("up0",), "prep_even_fwd": ("out_even", "in_odd"), "swa_fwd": ("down0",),
                "flash_fwd": ("out_odd", "up1", "down1")}


SMALL_ROWS = 8
SMALL_AT = {"norm_mix": (0, 0), "norm_mlp": (2, 0), "ret_norm": (4, 0), "swa_q_norm": (5, 0), "swa_k_norm": (5, 128),
            "ax_q_norm": (5, 256), "ax_k_norm": (5, 384), "swa_sink": (5, 512), "ret_decay_logit": (5, 640),
            "t5_table": (6, 0)}
LOSS_AT = (5, 768)


def _pack_small(arrays, loss=None):
    buf = jnp.zeros((SMALL_ROWS, 1024), F32)
    for name, (r, c) in SMALL_AT.items():
        a = arrays[name].astype(F32)
        a = a.reshape(1, -1) if name in ("ret_decay_logit", "t5_table") else a.reshape(-1, a.shape[-1])
        buf = lax.dynamic_update_slice(buf, a, (r, c))
    if loss is not None:
        buf = lax.dynamic_update_slice(buf, loss.reshape(1, 1), LOSS_AT)
    return buf


def _unpack_small(buf, like):
    out = {}
    for name, (r, c) in SMALL_AT.items():
        shape = like[name].shape
        rows = 1 if name in ("ret_decay_logit", "t5_table") else math.prod(shape[:-1])
        cols = math.prod(shape) // rows
        out[name] = buf[r:r + rows, c:c + cols].reshape(shape)
    return out


def kernel(x, norm_mix, norm_mlp, w_in_even, w_out_even, ret_decay_logit, ret_norm, swa_q_norm, swa_k_norm, swa_sink, t5_table, w_in_odd, w_out_odd, ax_q_norm, ax_k_norm, w_mlp_up, w_mlp_down, loss_target, m_norm_mix, m_norm_mlp, m_w_in_even, m_w_out_even, m_ret_decay_logit, m_ret_norm, m_swa_q_norm, m_swa_k_norm, m_swa_sink, m_t5_table, m_w_in_odd, m_w_out_odd, m_ax_q_norm, m_ax_k_norm, m_w_mlp_up, m_w_mlp_down, v_norm_mix, v_norm_mlp, v_w_in_even, v_w_out_even, v_ret_decay_logit, v_ret_norm, v_swa_q_norm, v_swa_k_norm, v_swa_sink, v_t5_table, v_w_in_odd, v_w_out_odd, v_ax_q_norm, v_ax_k_norm, v_w_mlp_up, v_w_mlp_down):
    given = dict(locals())
    weights = {n: given[n] for n in MATRICES + SMALL}
    moments_m = {n: given["m_" + n] for n in MATRICES + SMALL}
    moments_v = {n: given["v_" + n] for n in MATRICES + SMALL}

    def shard(table, key):
        arg, layer = MATRIX_OF[key]
        return table[arg][layer]

    gathered = _all_gather([shard(weights, k) for k in GATHER_FIRST])
    W = {k: _assemble(k, g) for k, g in zip(GATHER_FIRST, gathered)}
    late_keys = [k for keys in GATHER_WHILE.values() for k in keys]
    cast = dict(zip(late_keys, _cast_shards([shard(weights, k) for k in late_keys])))
    late = {stage: [(k, cast[k]) for k in keys] for stage, keys in GATHER_WHILE.items()}
    P = {"norm_mix": norm_mix, "norm_mlp": norm_mlp, "ret_decay_logit": ret_decay_logit[0], "ret_norm": ret_norm,
         "swa_q_norm": swa_q_norm, "swa_k_norm": swa_k_norm, "swa_sink": swa_sink, "t5_table": t5_table,
         "ax_q_norm": ax_q_norm, "ax_k_norm": ax_k_norm}

    loss_tile, dx, landed, dP = _local_step(x[0], loss_target[0], W, P, late)

    per_key = {k: _sum_and_adamw(landed[k], shard(weights, k), shard(moments_m, k), shard(moments_v, k), "adamw_" + k)
               for k in MATRIX_OF}
    grads, deltas, new_m, new_v = {}, {}, {}, {}
    for i, out in enumerate((grads, deltas, new_m, new_v)):
        for n in MATRICES:
            out[n] = jnp.stack([per_key[k][i] for k, (arg, _) in MATRIX_OF.items() if arg == n])

    dP["ret_decay_logit"] = dP["ret_decay_logit"][None]
    total = _all_reduce_small(_pack_small(dP, loss_tile[0, 0]))
    loss = total[LOSS_AT[0], LOSS_AT[1]]
    small_d, small_m, small_v = _adamw_small(_pack_small(weights), total, _pack_small(moments_m), _pack_small(moments_v))
    like = {n: weights[n] for n in SMALL}
    for out, buf in ((grads, total), (deltas, small_d), (new_m, small_m), (new_v, small_v)):
        out.update(_unpack_small(buf, like))

    order = ("norm_mix", "norm_mlp", "w_in_even", "w_out_even", "ret_decay_logit", "ret_norm", "swa_q_norm", "swa_k_norm",
             "swa_sink", "t5_table", "w_in_odd", "w_out_odd", "ax_q_norm", "ax_k_norm", "w_mlp_up", "w_mlp_down")
    return (loss, dx[None], *[grads[n] for n in order], *[deltas[n] for n in order],
            *[new_m[n] for n in order], *[new_v[n] for n in order])
```

```python
import math

import jax
import jax.numpy as jnp
from jax import lax
from jax.experimental import pallas as pl
from jax.experimental.pallas import tpu as pltpu

F32 = jnp.float32
BF = jnp.bfloat16

D_MODEL = 1024
HEAD_DIM = 128
EPS = 1e-6
NEG_INF = -1e30
RET_HEADS, RET_DK, RET_DV = 4, 128, 256
RET_THETA = 10000.0
SWA_HEADS, SWA_KV_HEADS, WINDOW, BLOCK = 8, 2, 128, 128
T5_BUCKETS, T5_MAX_DIST = 32, 128
AX_HEADS, AX_KV_HEADS, AX_THETA, GRID_W = 8, 2, 10000.0, 64
D_FF = 4096
ATT_SCALE = HEAD_DIM ** -0.5
LN2 = math.log(2.0)
AX_SCALE = ATT_SCALE / LN2
RET_SCALE = RET_DK ** -0.5
N_DEV = 8

ADAM_LR, ADAM_B1, ADAM_B2, ADAM_EPS, ADAM_WD, ADAM_STEP = 0.001, 0.9, 0.999, 1e-08, 0.01, 10

MIB = 1024 * 1024
VMEM_SMALL = 40 * MIB
VMEM_LARGE = 56 * MIB

OFF_QA, OFF_KA, OFF_VA, OFF_GA, OFF_QB, OFF_KB, OFF_VB = 0, 512, 1024, 2048, 3072, 4096, 4352
EVEN_IN = 4608
ODD_IN = 1536

NT = (((1,), (1,)), ((), ()))
TN = (((0,), (0,)), ((), ()))
NN = (((1,), (0,)), ((), ()))


def _dot(a, b, dims=NN):
    return lax.dot_general(a, b, dims, preferred_element_type=F32)


def _params(sem=None, vmem=VMEM_SMALL):
    return pltpu.CompilerParams(dimension_semantics=sem, vmem_limit_bytes=vmem)


def _pick(n, prefs):
    for p in prefs:
        if n % p == 0:
            return p
    return n


def _row_sum(x):
    return jnp.sum(x, axis=0, keepdims=True)


def _all_sum(x):
    return jnp.sum(jnp.sum(x, axis=0, keepdims=True), axis=1, keepdims=True)


def _sigmoid(x):
    return 1.0 / (1.0 + jnp.exp(-x))


SMEM_SPEC = pl.BlockSpec(memory_space=pltpu.SMEM)
ANY_SPEC = pl.BlockSpec(memory_space=pl.ANY)
VMEM_SPEC = pl.BlockSpec(memory_space=pltpu.VMEM)
MESH = pl.DeviceIdType.MESH


def _my_place():
    return lax.axis_index("x"), lax.axis_index("y"), lax.axis_index("c")


def _flip(place, k):
    x, y, c = place
    return (1 - x if k & 4 else x, 1 - y if k & 2 else y, 1 - c if k & 1 else c)


def _index(place):
    x, y, c = place
    return 4 * x + 2 * y + c


class _Exchange:
    def __init__(self, sources, gather):
        self.sources, self.gather, self.n = list(sources), gather, len(sources)
        self.out_shape = [jax.ShapeDtypeStruct(((N_DEV,) + s.shape) if gather else s.shape, s.dtype) for s in self.sources]
        self.scratch = [pltpu.SemaphoreType.DMA((self.n, 7)), pltpu.SemaphoreType.DMA((self.n, 7)),
                        pltpu.SemaphoreType.DMA((self.n,))]

    def _source(self, ins, a, place):
        return ins[a] if self.gather else ins[a].at[_index(place)]

    def _local(self, ins, outs, sems):
        me = _my_place()
        return [pltpu.make_async_copy(self._source(ins, a, me), outs[a].at[_index(me)], sems[2].at[a]) for a in range(self.n)]

    def _remote(self, ins, outs, sems, arriving):
        send_sems, recv_sems, _ = sems
        me = _my_place()
        copies = []
        for a in range(self.n):
            for k in range(1, N_DEV):
                peer = _flip(me, k)
                copies.append(pltpu.make_async_remote_copy(
                    src_ref=self._source(ins, a, peer), dst_ref=outs[a].at[_index(peer if arriving else me)],
                    send_sem=send_sems.at[a, k - 1], recv_sem=recv_sems.at[a, k - 1], device_id=peer, device_id_type=MESH))
        return copies

    def start(self, ins, outs, sems):
        for cp in self._local(ins, outs, sems) + self._remote(ins, outs, sems, arriving=False):
            cp.start()

    def wait(self, ins, outs, sems):
        for cp in self._remote(ins, outs, sems, arriving=True):
            cp.wait_recv()
        for cp in self._remote(ins, outs, sems, arriving=False):
            cp.wait_send()
        for cp in self._local(ins, outs, sems):
            cp.wait()


def _carry_call(body, args, *, name, grid, in_specs, out_specs, out_shape, scratch_shapes=(), vmem=VMEM_SMALL,
                semantics=None, carried=None):
    if carried is None:
        outs = pl.pallas_call(body, name=name, grid=grid, in_specs=in_specs, out_specs=out_specs, out_shape=out_shape,
                              scratch_shapes=list(scratch_shapes), compiler_params=_params(semantics, vmem))(*args)
        return list(outs), []
    ni, no, ns, nc = len(in_specs), len(out_specs), len(scratch_shapes), carried.n

    def full_body(*refs):
        ins, cin = refs[:ni], refs[ni:ni + nc]
        outs, cout = refs[ni + nc:ni + nc + no], refs[ni + nc + no:ni + 2 * nc + no]
        scratch, sems = refs[ni + 2 * nc + no:ni + 2 * nc + no + ns], refs[ni + 2 * nc + no + ns:]
        ids = [pl.program_id(d) for d in range(len(grid))]
        first, last = ids[0] == 0, ids[0] == grid[0] - 1
        for d in range(1, len(grid)):
            first, last = first & (ids[d] == 0), last & (ids[d] == grid[d] - 1)

        @pl.when(first)
        def _():
            carried.start(cin, cout, sems)

        body(*ins, *outs, *scratch)

        @pl.when(last)
        def _():
            carried.wait(cin, cout, sems)

    outs = pl.pallas_call(
        full_body, name=name, grid=grid, in_specs=list(in_specs) + [ANY_SPEC] * nc,
        out_specs=list(out_specs) + [ANY_SPEC] * nc, out_shape=list(out_shape) + carried.out_shape,
        scratch_shapes=list(scratch_shapes) + carried.scratch,
        compiler_params=_params(("arbitrary",) * len(grid), vmem))(*args, *carried.sources)
    return list(outs[:no]), list(outs[no:])


def _mm(a, b, *, name, ta=False, tb=False, out_dtypes=(F32,), extras=(), rows=(), n_row_sums=0, epilogue=None,
        tm=1024, tn=1024, tk=1024, vmem=VMEM_SMALL, carried=None):
    M, K = (a.shape[1], a.shape[0]) if ta else a.shape
    N = b.shape[0] if tb else b.shape[1]
    assert K == (b.shape[1] if tb else b.shape[0])
    tm = _pick(M, (tm, 512, 256, 128))
    tn = _pick(N, (tn, 1536, 512, 384, 256, 128))
    tk = _pick(K, (tk, 1536, 512, 256, 128))
    nk = K // tk
    assert n_row_sums == 0 or tn == N
    ne, nr, no = len(extras), len(rows), len(out_dtypes)
    dims = (((0 if ta else 1,), (1 if tb else 0,)), ((), ()))
    if epilogue is None:
        epilogue = lambda acc: (acc,)

    def body(a_ref, b_ref, *rest):
        extra_refs, out_refs = rest[:ne + nr], rest[ne + nr:ne + nr + no]
        sum_refs = rest[ne + nr + no:ne + nr + no + n_row_sums]
        first_tile = pl.program_id(0) == 0

        def finish(acc):
            outs = epilogue(acc, *[r[...] for r in extra_refs])
            for o_ref, o in zip(out_refs, outs[:no]):
                o_ref[...] = o.astype(o_ref.dtype)
            for s_ref, contribution in zip(sum_refs, outs[no:]):
                @pl.when(first_tile)
                def _(s_ref=s_ref, contribution=contribution):
                    s_ref[...] = contribution

                @pl.when(jnp.logical_not(first_tile))
                def _(s_ref=s_ref, contribution=contribution):
                    s_ref[...] += contribution

        part = _dot(a_ref[...], b_ref[...], dims)
        if nk == 1:
            finish(part)
        else:
            acc_ref = rest[-1]
            k = pl.program_id(2)

            @pl.when(k == 0)
            def _():
                acc_ref[...] = part

            @pl.when(k > 0)
            def _():
                acc_ref[...] += part

            @pl.when(k == nk - 1)
            def _():
                finish(acc_ref[...])

    a_spec = pl.BlockSpec((tk, tm), lambda i, j, k: (k, i)) if ta else pl.BlockSpec((tm, tk), lambda i, j, k: (i, k))
    b_spec = pl.BlockSpec((tn, tk), lambda i, j, k: (j, k)) if tb else pl.BlockSpec((tk, tn), lambda i, j, k: (k, j))
    o_spec = pl.BlockSpec((tm, tn), lambda i, j, k: (i, j))
    row_spec = pl.BlockSpec((1, tn), lambda i, j, k: (0, j))
    outs, landed = _carry_call(
        body, (a, b, *extras, *rows), name=name, grid=(M // tm, N // tn, nk),
        in_specs=[a_spec, b_spec] + [o_spec] * ne + [row_spec] * nr,
        out_specs=[o_spec] * no + [row_spec] * n_row_sums,
        out_shape=[jax.ShapeDtypeStruct((M, N), dt) for dt in out_dtypes] + [jax.ShapeDtypeStruct((1, N), F32)] * n_row_sums,
        scratch_shapes=[pltpu.VMEM((tm, tn), F32)] if nk > 1 else [],
        vmem=vmem, semantics=("arbitrary" if n_row_sums else "parallel", "parallel", "arbitrary"), carried=carried)
    outs = outs[0] if len(outs) == 1 else outs
    return outs if carried is None else (outs, landed)


def _rms_fwd(x, g, name):
    S, Dm = x.shape
    tr = _pick(S, (512,))

    def body(x_ref, g_ref, o_ref):
        xv = x_ref[...]
        r = lax.rsqrt(jnp.mean(xv * xv, axis=-1, keepdims=True) + EPS)
        o_ref[...] = (xv * r * g_ref[...]).astype(o_ref.dtype)

    row = pl.BlockSpec((tr, Dm), lambda i: (i, 0))
    return pl.pallas_call(
        body, name=name, grid=(S // tr,),
        in_specs=[row, pl.BlockSpec((1, Dm), lambda i: (0, 0))], out_specs=row,
        out_shape=jax.ShapeDtypeStruct((S, Dm), BF), compiler_params=_params(("parallel",)),
    )(x, g)


def _loss_and_grad(y, target):
    S, Dm = y.shape
    tr = _pick(S, (512,))
    n = S // tr

    def body(y_ref, t_ref, dy_ref, dyb_ref, loss_ref, acc_ref):
        i = pl.program_id(0)
        e = y_ref[...] - t_ref[...]
        d = e * (1.0 / Dm)
        dy_ref[...] = d
        dyb_ref[...] = d.astype(dyb_ref.dtype)

        @pl.when(i == 0)
        def _():
            acc_ref[...] = jnp.zeros_like(acc_ref)

        acc_ref[...] += _row_sum(e * e)

        @pl.when(i == n - 1)
        def _():
            loss_ref[...] = jnp.broadcast_to(_all_sum(acc_ref[...]) * (0.5 / Dm), loss_ref.shape)

    row = pl.BlockSpec((tr, Dm), lambda i: (i, 0))
    return pl.pallas_call(
        body, name="loss_head", grid=(n,),
        in_specs=[row, row], out_specs=[row, row, pl.BlockSpec((8, 128), lambda i: (0, 0))],
        out_shape=[jax.ShapeDtypeStruct((S, Dm), F32), jax.ShapeDtypeStruct((S, Dm), BF),
                   jax.ShapeDtypeStruct((8, 128), F32)],
        scratch_shapes=[pltpu.VMEM((1, Dm), F32)],
        compiler_params=_params(("arbitrary",)),
    )(y, target)


def _partner(x, half):
    if half == 64:
        return pltpu.roll(x, 64, 1)
    lane = lax.broadcasted_iota(jnp.int32, x.shape, 1)
    return jnp.where((lane % (2 * half)) < half, pltpu.roll(x, 128 - half, 1), pltpu.roll(x, half, 1))


def _rope(x, cos, sin, half):
    return x * cos + _partner(x, half) * sin


def _rope_t(dy, cos, sin, half):
    return dy * cos - _partner(dy, half) * sin


def _head_norm(x):
    r = lax.rsqrt(jnp.mean(x * x, axis=-1, keepdims=True) + EPS)
    return x * r, r


def _head_norm_bwd(dxh, xh, r):
    return r * (dxh - xh * jnp.mean(dxh * xh, axis=-1, keepdims=True))


def _cols(ref, off, width=HEAD_DIM):
    return ref[:, off:off + width]


def _prep_even_fwd(proj, cos, sin, gq, gk, carried=None):
    S = proj.shape[0]
    tr = _pick(S, (256,))

    def body(p_ref, cos_ref, sin_ref, gq_ref, gk_ref, qr_ref, kr_ref, vr_ref, qs_ref, ks_ref, vs_ref):
        cos_v, sin_v = cos_ref[...], sin_ref[...]
        for h in range(RET_HEADS):
            o = h * RET_DK
            qr_ref[:, o:o + RET_DK] = _rope(_cols(p_ref, OFF_QA + o), cos_v, sin_v, 64).astype(qr_ref.dtype)
            kr_ref[:, o:o + RET_DK] = (_rope(_cols(p_ref, OFF_KA + o), cos_v, sin_v, 64) * RET_SCALE).astype(kr_ref.dtype)
        vr_ref[...] = p_ref[:, OFF_VA:OFF_VA + 1024].astype(vr_ref.dtype)
        for h in range(SWA_HEADS):
            o = h * HEAD_DIM
            xh, _ = _head_norm(_cols(p_ref, OFF_QB + o))
            qs_ref[:, o:o + HEAD_DIM] = (xh * gq_ref[...] * ATT_SCALE).astype(qs_ref.dtype)
        for h in range(SWA_KV_HEADS):
            o = h * HEAD_DIM
            xh, _ = _head_norm(_cols(p_ref, OFF_KB + o))
            ks_ref[:, o:o + HEAD_DIM] = (xh * gk_ref[...]).astype(ks_ref.dtype)
        vs_ref[...] = p_ref[:, OFF_VB:OFF_VB + 256].astype(vs_ref.dtype)

    def row(w):
        return pl.BlockSpec((tr, w), lambda i: (i, 0))

    vec = pl.BlockSpec((1, HEAD_DIM), lambda i: (0, 0))
    widths = (512, 512, 1024, 1024, 256, 256)
    return _carry_call(
        body, (proj, cos, sin, gq, gk), name="prep_even_fwd", grid=(S // tr,),
        in_specs=[row(EVEN_IN), row(128), row(128), vec, vec],
        out_specs=[row(w) for w in widths],
        out_shape=[jax.ShapeDtypeStruct((S, w), BF) for w in widths],
        semantics=("parallel",), carried=carried)


def _prep_even_bwd(proj, cos, sin, gq, gk, dqr, dkr, dvr, dga, dqs, dks, dvs):
    S = proj.shape[0]
    tr = _pick(S, (256,))

    def body(p_ref, cos_ref, sin_ref, gq_ref, gk_ref, dqr_ref, dkr_ref, dvr_ref, dga_ref, dqs_ref, dks_ref,
             dvs_ref, dp_ref, dgq_ref, dgk_ref):
        cos_v, sin_v = cos_ref[...], sin_ref[...]
        dt = dp_ref.dtype
        for h in range(RET_HEADS):
            o = h * RET_DK
            dp_ref[:, OFF_QA + o:OFF_QA + o + RET_DK] = _rope_t(_cols(dqr_ref, o).astype(F32), cos_v, sin_v, 64).astype(dt)
            dp_ref[:, OFF_KA + o:OFF_KA + o + RET_DK] = _rope_t(_cols(dkr_ref, o).astype(F32) * RET_SCALE, cos_v, sin_v, 64).astype(dt)
        dp_ref[:, OFF_VA:OFF_VA + 1024] = dvr_ref[...].astype(dt)
        dp_ref[:, OFF_GA:OFF_GA + 1024] = dga_ref[...].astype(dt)
        dgq = jnp.zeros((1, HEAD_DIM), F32)
        for h in range(SWA_HEADS):
            o = h * HEAD_DIM
            xh, r = _head_norm(_cols(p_ref, OFF_QB + o))
            dy = _cols(dqs_ref, o).astype(F32) * ATT_SCALE
            dgq = dgq + _row_sum(dy * xh)
            dp_ref[:, OFF_QB + o:OFF_QB + o + HEAD_DIM] = _head_norm_bwd(dy * gq_ref[...], xh, r).astype(dt)
        dgk = jnp.zeros((1, HEAD_DIM), F32)
        for h in range(SWA_KV_HEADS):
            o = h * HEAD_DIM
            xh, r = _head_norm(_cols(p_ref, OFF_KB + o))
            dy = _cols(dks_ref, o)
            dgk = dgk + _row_sum(dy * xh)
            dp_ref[:, OFF_KB + o:OFF_KB + o + HEAD_DIM] = _head_norm_bwd(dy * gk_ref[...], xh, r).astype(dt)
        dp_ref[:, OFF_VB:OFF_VB + 256] = dvs_ref[...].astype(dt)

        @pl.when(pl.program_id(0) == 0)
        def _():
            dgq_ref[...] = jnp.zeros_like(dgq_ref)
            dgk_ref[...] = jnp.zeros_like(dgk_ref)

        dgq_ref[...] += dgq
        dgk_ref[...] += dgk

    def row(w):
        return pl.BlockSpec((tr, w), lambda i: (i, 0))

    vec = pl.BlockSpec((1, HEAD_DIM), lambda i: (0, 0))
    return pl.pallas_call(
        body, name="prep_even_bwd", grid=(S // tr,),
        in_specs=[row(EVEN_IN), row(128), row(128), vec, vec, row(512), row(512), row(1024), row(1024),
                  row(1024), row(256), row(256)],
        out_specs=[row(EVEN_IN), vec, vec],
        out_shape=[jax.ShapeDtypeStruct((S, EVEN_IN), BF), jax.ShapeDtypeStruct((1, HEAD_DIM), F32),
                   jax.ShapeDtypeStruct((1, HEAD_DIM), F32)],
        compiler_params=_params(("arbitrary",)),
    )(proj, cos, sin, gq, gk, dqr, dkr, dvr, dga, dqs, dks, dvs)


RET_CHUNK = 512
def _log_sigmoid_tile(logit_tile):
    def body(x_ref, o_ref):
        xv = x_ref[...]
        t = jnp.exp(-jnp.abs(xv))
        log1p_t = jnp.where(t < 1e-3, t * (1.0 - 0.5 * t), jnp.log(1.0 + t))
        o_ref[...] = jnp.minimum(xv, 0.0) - log1p_t

    full = pl.BlockSpec((8, 128), lambda: (0, 0))
    return pl.pallas_call(body, name="log_sigmoid", in_specs=[full], out_specs=full,
                          out_shape=jax.ShapeDtypeStruct((8, 128), F32))(logit_tile)


def _decay(diff, lf, lb):
    return jnp.exp(jnp.where(diff >= 0, lf * diff, -(lb * diff)))


def _col_iota(n):
    return lax.broadcasted_iota(jnp.int32, (n, 1), 0).astype(F32)


def _ret_scan(x, z, lg, asc, desc, name):
    S = x.shape[0]
    C = _pick(S, (RET_CHUNK,))
    nc = S // C
    (arow, aoff), (drow, doff) = asc, desc

    def body(lg_ref, xa_ref, za_ref, xd_ref, zd_ref, asc_ref, desc_ref, sa_ref, sd_ref):
        h, t = pl.program_id(0), pl.program_id(1)
        la, ld = lg_ref[arow, h], lg_ref[drow, h]

        @pl.when(t == 0)
        def _():
            sa_ref[...] = jnp.zeros_like(sa_ref)
            sd_ref[...] = jnp.zeros_like(sd_ref)

        asc_ref[0, 0] = sa_ref[...]
        desc_ref[0, 0] = sd_ref[...]
        j = _col_iota(C)
        xa = (xa_ref[...].astype(F32) * jnp.exp(la * (C - 1 + aoff - j))).astype(xa_ref.dtype)
        xd = (xd_ref[...].astype(F32) * jnp.exp(ld * (j + doff))).astype(xd_ref.dtype)
        sa_ref[...] = jnp.exp(jnp.full((1, RET_DV), la * C, F32)) * sa_ref[...] + _dot(xa, za_ref[...], TN)
        sd_ref[...] = jnp.exp(jnp.full((1, RET_DV), ld * C, F32)) * sd_ref[...] + _dot(xd, zd_ref[...], TN)

    state = jax.ShapeDtypeStruct((RET_HEADS, nc, RET_DK, RET_DV), F32)
    return pl.pallas_call(
        body, name=name, grid=(RET_HEADS, nc),
        in_specs=[SMEM_SPEC,
                  pl.BlockSpec((C, RET_DK), lambda h, t: (t, h)), pl.BlockSpec((C, RET_DV), lambda h, t: (t, h)),
                  pl.BlockSpec((C, RET_DK), lambda h, t: (nc - 1 - t, h)), pl.BlockSpec((C, RET_DV), lambda h, t: (nc - 1 - t, h))],
        out_specs=[pl.BlockSpec((1, 1, RET_DK, RET_DV), lambda h, t: (h, t, 0, 0)),
                   pl.BlockSpec((1, 1, RET_DK, RET_DV), lambda h, t: (h, nc - 1 - t, 0, 0))],
        out_shape=[state, state],
        scratch_shapes=[pltpu.VMEM((RET_DK, RET_DV), F32), pltpu.VMEM((RET_DK, RET_DV), F32)],
        compiler_params=_params(("parallel", "arbitrary")),
    )(lg, x, z, x, z)


def _ret_fwd(q, k, v, lg, sf, sb):
    S = q.shape[0]
    C = _pick(S, (RET_CHUNK,))

    def body(lg_ref, q_ref, k_ref, v_ref, sf_ref, sb_ref, y_ref):
        h = pl.program_id(0)
        lf, lb = lg_ref[0, h], lg_ref[1, h]
        qv = q_ref[...]
        dt = qv.dtype
        diff = (lax.broadcasted_iota(jnp.int32, (C, C), 0) - lax.broadcasted_iota(jnp.int32, (C, C), 1)).astype(F32)
        y = _dot((_dot(qv, k_ref[...], NT) * _decay(diff, lf, lb)).astype(dt), v_ref[...])
        r = _col_iota(C)
        qf = qv.astype(F32)
        y = y + _dot((qf * jnp.exp(lf * (r + 1.0))).astype(dt), sf_ref[0, 0].astype(dt))
        y_ref[...] = y + _dot((qf * jnp.exp(lb * (C - r))).astype(dt), sb_ref[0, 0].astype(dt))

    state = pl.BlockSpec((1, 1, RET_DK, RET_DV), lambda h, c: (h, c, 0, 0))
    qk = pl.BlockSpec((C, RET_DK), lambda h, c: (c, h))
    vy = pl.BlockSpec((C, RET_DV), lambda h, c: (c, h))
    return pl.pallas_call(
        body, name="retention_fwd", grid=(RET_HEADS, S // C),
        in_specs=[SMEM_SPEC, qk, qk, vy, state, state], out_specs=vy,
        out_shape=jax.ShapeDtypeStruct((S, RET_HEADS * RET_DV), F32),
        compiler_params=_params(("parallel", "parallel")),
    )(lg, q, k, v, sf, sb)


def _ret_bwd(q, k, v, dy, lg, logit, sf, sb, hf, hb):
    S = q.shape[0]
    C = _pick(S, (RET_CHUNK,))
    nc = S // C

    def body(lg_ref, logit_ref, q_ref, k_ref, v_ref, dy_ref, sf_ref, sb_ref, hf_ref, hb_ref,
             dq_ref, dk_ref, dv_ref, dlg_ref, acc_ref):
        h, c = pl.program_id(0), pl.program_id(1)
        lf, lb = lg_ref[0, h], lg_ref[1, h]

        @pl.when(c == 0)
        def _():
            acc_ref[...] = jnp.zeros_like(acc_ref)

        qv, kv, vv, dyv = q_ref[...], k_ref[...], v_ref[...], dy_ref[...]
        dt = qv.dtype
        qf, kf = qv.astype(F32), kv.astype(F32)
        diff = (lax.broadcasted_iota(jnp.int32, (C, C), 0) - lax.broadcasted_iota(jnp.int32, (C, C), 1)).astype(F32)
        dec = _decay(diff, lf, lb)
        sc = _dot(qv, kv, NT) * dec
        dp = _dot(dyv, vv, NT)
        da = (dp * dec).astype(dt)
        dq = _dot(da, kv)
        dk = _dot(da, qv, TN)
        dv = _dot(sc.astype(dt), dyv, TN)
        w = sc * dp * diff
        tot_w, tot_f = _all_sum(w), _all_sum(jnp.where(diff >= 0, w, 0.0))
        d_lf, d_lb = tot_f, tot_f - tot_w
        r = _col_iota(C)
        a, b = jnp.exp(lf * (r + 1.0)), jnp.exp(lb * (C - r))
        e, f = jnp.exp(lf * (C - 1.0 - r)), jnp.exp(lb * r)
        sfv, sbv, hfv, hbv = sf_ref[0, 0], sb_ref[0, 0], hf_ref[0, 0], hb_ref[0, 0]
        t_f, t_b = _dot(dyv, sfv.astype(dt), NT), _dot(dyv, sbv.astype(dt), NT)
        u_f, u_b = _dot(vv, hfv.astype(dt), NT), _dot(vv, hbv.astype(dt), NT)
        dq_ref[...] = (dq + a * t_f + b * t_b).astype(dq_ref.dtype)
        dk_ref[...] = (dk + e * u_f + f * u_b).astype(dk_ref.dtype)
        dv_ref[...] = (dv + _dot((kf * e).astype(dt), hfv.astype(dt)) + _dot((kf * f).astype(dt), hbv.astype(dt))).astype(dv_ref.dtype)
        row_q_f = jnp.sum(qf * t_f, axis=-1, keepdims=True)
        row_q_b = jnp.sum(qf * t_b, axis=-1, keepdims=True)
        row_k_f = jnp.sum(kf * u_f, axis=-1, keepdims=True)
        row_k_b = jnp.sum(kf * u_b, axis=-1, keepdims=True)
        gf_c = jnp.exp(jnp.full((1, 1), lf * C, F32))
        gb_c = jnp.exp(jnp.full((1, 1), lb * C, F32))
        d_lf = d_lf + _all_sum((r + 1.0) * a * row_q_f + (C - 1.0 - r) * e * row_k_f) + C * gf_c * _all_sum(hfv * sfv)
        d_lb = d_lb + _all_sum((C - r) * b * row_q_b + r * f * row_k_b) + C * gb_c * _all_sum(hbv * sbv)
        acc_ref[0] += jnp.broadcast_to(d_lf, (8, 128))
        acc_ref[1] += jnp.broadcast_to(d_lb, (8, 128))

        @pl.when(c == nc - 1)
        def _():
            for d in range(2):
                gate = 1.0 / (1.0 + jnp.exp(jnp.full((8, 128), logit_ref[d, h], F32)))
                dlg_ref[0, d] = acc_ref[d] * gate

    state = pl.BlockSpec((1, 1, RET_DK, RET_DV), lambda h, c: (h, c, 0, 0))
    qk = pl.BlockSpec((C, RET_DK), lambda h, c: (c, h))
    vy = pl.BlockSpec((C, RET_DV), lambda h, c: (c, h))
    return pl.pallas_call(
        body, name="retention_bwd", grid=(RET_HEADS, nc),
        in_specs=[SMEM_SPEC, SMEM_SPEC, qk, qk, vy, vy, state, state, state, state],
        out_specs=[qk, qk, vy, pl.BlockSpec((1, 2, 8, 128), lambda h, c: (h, 0, 0, 0))],
        out_shape=[jax.ShapeDtypeStruct((S, RET_HEADS * RET_DK), BF), jax.ShapeDtypeStruct((S, RET_HEADS * RET_DK), BF),
                   jax.ShapeDtypeStruct((S, RET_HEADS * RET_DV), BF),
                   jax.ShapeDtypeStruct((RET_HEADS, 2, 8, 128), F32)],
        scratch_shapes=[pltpu.VMEM((2, 8, 128), F32)],
        compiler_params=_params(("parallel", "arbitrary")),
    )(lg, logit, q, k, v, dy, sf, sb, hf, hb)


def _ret_post_fwd(y, proj, gn):
    S = y.shape[0]
    tr = _pick(S, (512,))

    def body(y_ref, g_ref, gn_ref, o_ref):
        for h in range(RET_HEADS):
            o = h * RET_DV
            yh, _ = _head_norm(_cols(y_ref, o, RET_DV))
            gate = _cols(g_ref, o, RET_DV)
            o_ref[:, o:o + RET_DV] = (gate * _sigmoid(gate) * (yh * gn_ref[:, o:o + RET_DV])).astype(o_ref.dtype)

    row = pl.BlockSpec((tr, 1024), lambda i: (i, 0))
    return pl.pallas_call(
        body, name="retention_post_fwd", grid=(S // tr,),
        in_specs=[row, pl.BlockSpec((tr, 1024), lambda i: (i, OFF_GA // 1024)), pl.BlockSpec((1, 1024), lambda i: (0, 0))],
        out_specs=row, out_shape=jax.ShapeDtypeStruct((S, 1024), BF), compiler_params=_params(("parallel",)),
    )(y, proj, gn)


def _ret_post_bwd(y, proj, gn, do):
    S = y.shape[0]
    tr = _pick(S, (512,))

    def body(y_ref, g_ref, gn_ref, do_ref, dy_ref, dg_ref, dgn_ref):
        @pl.when(pl.program_id(0) == 0)
        def _():
            dgn_ref[...] = jnp.zeros_like(dgn_ref)

        for h in range(RET_HEADS):
            o = h * RET_DV
            yh, r = _head_norm(_cols(y_ref, o, RET_DV))
            gate = _cols(g_ref, o, RET_DV)
            gnh = gn_ref[:, o:o + RET_DV]
            dout = _cols(do_ref, o, RET_DV).astype(F32)
            sg = _sigmoid(gate)
            dz = dout * (gate * sg)
            dg_ref[:, o:o + RET_DV] = (dout * (yh * gnh) * (sg * (1.0 + gate * (1.0 - sg)))).astype(dg_ref.dtype)
            dgn_ref[:, o:o + RET_DV] += _row_sum(dz * yh)
            dy_ref[:, o:o + RET_DV] = _head_norm_bwd(dz * gnh, yh, r).astype(dy_ref.dtype)

    row = pl.BlockSpec((tr, 1024), lambda i: (i, 0))
    vec = pl.BlockSpec((1, 1024), lambda i: (0, 0))
    return pl.pallas_call(
        body, name="retention_post_bwd", grid=(S // tr,),
        in_specs=[row, pl.BlockSpec((tr, 1024), lambda i: (i, OFF_GA // 1024)), vec, row],
        out_specs=[row, row, vec],
        out_shape=[jax.ShapeDtypeStruct((S, 1024), BF), jax.ShapeDtypeStruct((S, 1024), BF),
                   jax.ShapeDtypeStruct((1, 1024), F32)],
        compiler_params=_params(("arbitrary",)),
    )(y, proj, gn, do)


def _t5_bucket_map():
    r = jnp.arange(BLOCK)
    j = jnp.arange(3 * BLOCK)
    rel = j[None, :] - BLOCK - r[:, None]
    nb = T5_BUCKETS // 2
    max_exact = nb // 2
    ret = jnp.where(rel > 0, nb, 0)
    n = jnp.abs(rel)
    nf = jnp.maximum(n, 1).astype(jnp.float32)
    large = max_exact + (jnp.log(nf / max_exact) / math.log(T5_MAX_DIST / max_exact)
                         * (nb - max_exact)).astype(jnp.int32)
    large = jnp.minimum(large, nb - 1)
    bucket = ret + jnp.where(n < max_exact, n, large)
    return jnp.where(jnp.abs(rel) <= WINDOW, bucket, -1).astype(jnp.int32)


SWA_G = SWA_HEADS // SWA_KV_HEADS
SWA_LANES = SWA_G * BLOCK


def _t5_bias(table, bucket_t):
    def body(t_ref, b_ref, o_ref):
        bk = b_ref[...]
        for h in range(SWA_HEADS):
            acc = jnp.full(bk.shape, NEG_INF, F32)
            for b in range(T5_BUCKETS):
                acc = jnp.where(bk == b, t_ref[b, h], acc)
            o_ref[h // SWA_G, :, (h % SWA_G) * BLOCK:(h % SWA_G + 1) * BLOCK] = acc

    return pl.pallas_call(
        body, name="t5_bias", in_specs=[SMEM_SPEC, pl.BlockSpec((3 * BLOCK, BLOCK), lambda: (0, 0))],
        out_specs=pl.BlockSpec((SWA_KV_HEADS, 3 * BLOCK, SWA_LANES), lambda: (0, 0, 0)),
        out_shape=jax.ShapeDtypeStruct((SWA_KV_HEADS, 3 * BLOCK, SWA_LANES), F32),
    )(table, bucket_t)


def _t5_table_grad(dbias, bucket_t):
    def body(d_ref, b_ref, o_ref):
        bk = b_ref[...]
        lane = lax.broadcasted_iota(jnp.int32, (1, 128), 1)
        for b in range(T5_BUCKETS):
            hit = bk == b
            row = jnp.zeros((1, 128), F32)
            for h in range(SWA_HEADS):
                d = d_ref[h // SWA_G, :, (h % SWA_G) * BLOCK:(h % SWA_G + 1) * BLOCK]
                row = row + jnp.where(lane == h, _all_sum(jnp.where(hit, d, 0.0)), 0.0)
            o_ref[b:b + 1, :] = row

    return pl.pallas_call(
        body, name="t5_table_grad",
        in_specs=[pl.BlockSpec((SWA_KV_HEADS, 3 * BLOCK, SWA_LANES), lambda: (0, 0, 0)),
                  pl.BlockSpec((3 * BLOCK, BLOCK), lambda: (0, 0))],
        out_specs=pl.BlockSpec((T5_BUCKETS, 128), lambda: (0, 0)),
        out_shape=jax.ShapeDtypeStruct((T5_BUCKETS, 128), F32),
    )(dbias, bucket_t)


def _swa_scores(i, nb, q4, kw, bias_t, sink_row):
    s = _dot(kw, q4, NT) + bias_t
    row = lax.broadcasted_iota(jnp.int32, s.shape, 0)
    first_row = jnp.where(i == 0, BLOCK, 0)
    end_row = jnp.where(i == nb - 1, 2 * BLOCK, 3 * BLOCK)
    s = jnp.where((row < first_row) | (row >= end_row), NEG_INF, s)
    m = jnp.maximum(jnp.max(s, axis=0, keepdims=True), sink_row)
    p = jnp.exp(s - m)
    e_sink = jnp.exp(sink_row - m)
    inv = 1.0 / (jnp.sum(p, axis=0, keepdims=True) + e_sink)
    return p * inv, e_sink * inv


def _swa_group(q_ref, sink_ref, kh):
    heads = range(kh * SWA_G, (kh + 1) * SWA_G)
    q4 = jnp.concatenate([_cols(q_ref, h * HEAD_DIM) for h in heads], axis=0)
    sink_row = jnp.concatenate([jnp.full((1, BLOCK), sink_ref[0, h], F32) for h in heads], axis=1)
    return q4, sink_row


def _swa_unstack(ref, kh, x_t):
    for g in range(SWA_G):
        h = kh * SWA_G + g
        ref[:, h * HEAD_DIM:(h + 1) * HEAD_DIM] = x_t[:, g * BLOCK:(g + 1) * BLOCK].T.astype(ref.dtype)


def _swa_window(ref, i, nb, off):
    prev, nxt = jnp.maximum(i - 1, 0), jnp.minimum(i + 1, nb - 1)
    rows = [pl.ds(pl.multiple_of(b * BLOCK, BLOCK), BLOCK) for b in (prev, i, nxt)]
    return jnp.concatenate([ref[r, off:off + HEAD_DIM] for r in rows], axis=0), rows


def _swa_fwd(q, k, v, bias, sink, carried=None):
    S = q.shape[0]
    nb = S // BLOCK

    def body(sink_ref, q_ref, k_ref, v_ref, bias_ref, o_ref):
        i = pl.program_id(0)
        for kh in range(SWA_KV_HEADS):
            kw, _ = _swa_window(k_ref, i, nb, kh * HEAD_DIM)
            vw, _ = _swa_window(v_ref, i, nb, kh * HEAD_DIM)
            q4, sink_row = _swa_group(q_ref, sink_ref, kh)
            p, _ = _swa_scores(i, nb, q4, kw, bias_ref[kh], sink_row)
            _swa_unstack(o_ref, kh, _dot(vw, p.astype(vw.dtype), TN))

    full_kv = pl.BlockSpec((S, SWA_KV_HEADS * HEAD_DIM), lambda i: (0, 0))
    (o,), landed = _carry_call(
        body, (sink, q, k, v, bias), name="swa_fwd", grid=(nb,),
        in_specs=[SMEM_SPEC, pl.BlockSpec((BLOCK, 1024), lambda i: (i, 0)), full_kv, full_kv,
                  pl.BlockSpec((SWA_KV_HEADS, 3 * BLOCK, SWA_LANES), lambda i: (0, 0, 0))],
        out_specs=[pl.BlockSpec((BLOCK, 1024), lambda i: (i, 0))],
        out_shape=[jax.ShapeDtypeStruct((S, 1024), BF)], semantics=("parallel",), carried=carried)
    return o, landed


def _swa_bwd(q, k, v, do, bias, sink, carried=None):
    S = q.shape[0]
    nb = S // BLOCK

    def body(sink_ref, q_ref, k_ref, v_ref, do_ref, bias_ref, dq_ref, dk_ref, dv_ref, dbias_ref, dsink_ref):
        i = pl.program_id(0)

        @pl.when(i == 0)
        def _():
            dk_ref[...] = jnp.zeros_like(dk_ref)
            dv_ref[...] = jnp.zeros_like(dv_ref)
            dbias_ref[...] = jnp.zeros_like(dbias_ref)
            dsink_ref[...] = jnp.zeros_like(dsink_ref)

        for kh in range(SWA_KV_HEADS):
            off = kh * HEAD_DIM
            kw, rows = _swa_window(k_ref, i, nb, off)
            vw, _ = _swa_window(v_ref, i, nb, off)
            q4, sink_row = _swa_group(q_ref, sink_ref, kh)
            p, p_sink = _swa_scores(i, nb, q4, kw, bias_ref[kh], sink_row)
            do4 = jnp.concatenate([_cols(do_ref, (kh * SWA_G + g) * HEAD_DIM) for g in range(SWA_G)], axis=0).astype(vw.dtype)
            dp = _dot(vw, do4, NT)
            delta = jnp.sum(p * dp, axis=0, keepdims=True)
            ds = p * (dp - delta)
            dsb = ds.astype(q4.dtype)
            _swa_unstack(dq_ref, kh, _dot(kw, dsb, TN))
            dkw = _dot(dsb, q4)
            dvw = _dot(p.astype(do4.dtype), do4)
            dbias_ref[kh] += ds
            sink_term = p_sink * delta
            for g in range(SWA_G):
                h = kh * SWA_G + g
                dsink_ref[h:h + 1, :] += jnp.broadcast_to(-_all_sum(sink_term[:, g * BLOCK:(g + 1) * BLOCK]), (1, 128))
            for b, r in enumerate(rows):
                dk_ref[r, off:off + HEAD_DIM] += dkw[b * BLOCK:(b + 1) * BLOCK]
                dv_ref[r, off:off + HEAD_DIM] += dvw[b * BLOCK:(b + 1) * BLOCK]

    full_kv = pl.BlockSpec((S, SWA_KV_HEADS * HEAD_DIM), lambda i: (0, 0))
    blk = pl.BlockSpec((BLOCK, 1024), lambda i: (i, 0))
    bias_spec = pl.BlockSpec((SWA_KV_HEADS, 3 * BLOCK, SWA_LANES), lambda i: (0, 0, 0))
    outs, landed = _carry_call(
        body, (sink, q, k, v, do, bias), name="swa_bwd", grid=(nb,),
        in_specs=[SMEM_SPEC, blk, full_kv, full_kv, blk, bias_spec],
        out_specs=[blk, full_kv, full_kv, bias_spec, pl.BlockSpec((8, 128), lambda i: (0, 0))],
        out_shape=[jax.ShapeDtypeStruct((S, 1024), BF), jax.ShapeDtypeStruct((S, 256), F32),
                   jax.ShapeDtypeStruct((S, 256), F32),
                   jax.ShapeDtypeStruct((SWA_KV_HEADS, 3 * BLOCK, SWA_LANES), F32), jax.ShapeDtypeStruct((8, 128), F32)],
        vmem=VMEM_LARGE, semantics=("arbitrary",), carried=carried)
    return (*outs, landed)


def _prep_odd_fwd(proj, cos, sin, gq, gk):
    S = proj.shape[0]
    tr = _pick(S, (512,))

    def body(p_ref, cos_ref, sin_ref, gq_ref, gk_ref, q_ref, k_ref, v_ref):
        cos_v, sin_v = cos_ref[...], sin_ref[...]
        for h in range(AX_HEADS):
            o = h * HEAD_DIM
            xh, _ = _head_norm(_cols(p_ref, o))
            q_ref[:, o:o + HEAD_DIM] = (_rope(xh * gq_ref[...], cos_v, sin_v, 32) * AX_SCALE).astype(q_ref.dtype)
        for h in range(AX_KV_HEADS):
            o = h * HEAD_DIM
            xh, _ = _head_norm(_cols(p_ref, 1024 + o))
            k_ref[:, o:o + HEAD_DIM] = _rope(xh * gk_ref[...], cos_v, sin_v, 32).astype(k_ref.dtype)
        v_ref[...] = p_ref[:, 1280:1536].astype(v_ref.dtype)

    def row(w):
        return pl.BlockSpec((tr, w), lambda i: (i, 0))

    vec = pl.BlockSpec((1, HEAD_DIM), lambda i: (0, 0))
    return pl.pallas_call(
        body, name="prep_odd_fwd", grid=(S // tr,),
        in_specs=[row(ODD_IN), row(128), row(128), vec, vec], out_specs=[row(1024), row(256), row(256)],
        out_shape=[jax.ShapeDtypeStruct((S, w), BF) for w in (1024, 256, 256)],
        compiler_params=_params(("parallel",)),
    )(proj, cos, sin, gq, gk)


def _prep_odd_bwd(proj, cos, sin, gq, gk, dq, dk, dv):
    S = proj.shape[0]
    tr = _pick(S, (512,))

    def body(p_ref, cos_ref, sin_ref, gq_ref, gk_ref, dq_ref, dk_ref, dv_ref, dp_ref, dgq_ref, dgk_ref):
        cos_v, sin_v = cos_ref[...], sin_ref[...]
        dt = dp_ref.dtype
        dgq = jnp.zeros((1, HEAD_DIM), F32)
        for h in range(AX_HEADS):
            o = h * HEAD_DIM
            xh, r = _head_norm(_cols(p_ref, o))
            dy = _rope_t(_cols(dq_ref, o).astype(F32) * AX_SCALE, cos_v, sin_v, 32)
            dgq = dgq + _row_sum(dy * xh)
            dp_ref[:, o:o + HEAD_DIM] = _head_norm_bwd(dy * gq_ref[...], xh, r).astype(dt)
        dgk = jnp.zeros((1, HEAD_DIM), F32)
        for h in range(AX_KV_HEADS):
            o = h * HEAD_DIM
            xh, r = _head_norm(_cols(p_ref, 1024 + o))
            dy = _rope_t(_cols(dk_ref, o), cos_v, sin_v, 32)
            dgk = dgk + _row_sum(dy * xh)
            dp_ref[:, 1024 + o:1024 + o + HEAD_DIM] = _head_norm_bwd(dy * gk_ref[...], xh, r).astype(dt)
        dp_ref[:, 1280:1536] = dv_ref[...].astype(dt)

        @pl.when(pl.program_id(0) == 0)
        def _():
            dgq_ref[...] = jnp.zeros_like(dgq_ref)
            dgk_ref[...] = jnp.zeros_like(dgk_ref)

        dgq_ref[...] += dgq
        dgk_ref[...] += dgk

    def row(w):
        return pl.BlockSpec((tr, w), lambda i: (i, 0))

    vec = pl.BlockSpec((1, HEAD_DIM), lambda i: (0, 0))
    return pl.pallas_call(
        body, name="prep_odd_bwd", grid=(S // tr,),
        in_specs=[row(ODD_IN), row(128), row(128), vec, vec, row(1024), row(256), row(256)],
        out_specs=[row(ODD_IN), vec, vec],
        out_shape=[jax.ShapeDtypeStruct((S, ODD_IN), BF), jax.ShapeDtypeStruct((1, HEAD_DIM), F32),
                   jax.ShapeDtypeStruct((1, HEAD_DIM), F32)],
        compiler_params=_params(("arbitrary",)),
    )(proj, cos, sin, gq, gk, dq, dk, dv)


def _loop_unrolled(n, factor, step, init):
    while n % factor:
        factor //= 2

    def trip(t, carry):
        for u in range(factor):
            carry = step(factor * t + u, carry)
        return carry

    return lax.fori_loop(0, n // factor, trip, init)


def _flash_fwd(q, k, v, carried=None):
    S = q.shape[0]
    tq = _pick(S, (512,))
    tk = _pick(S, (1024, 512))
    nk = S // tk
    G = AX_HEADS // AX_KV_HEADS

    def body(q_ref, k_ref, v_ref, o_ref, lse_ref):
        qv = q_ref[...]

        def step(j, carry):
            m, l, acc = carry
            rows = pl.ds(pl.multiple_of(j * tk, tk), tk)
            s = _dot(qv, k_ref[rows, :], NT)
            m_new = jnp.maximum(m, jnp.max(s, axis=-1, keepdims=True))
            alpha = jnp.exp2(m - m_new)
            p = jnp.exp2(s - m_new)
            l = alpha * l + jnp.sum(p, axis=-1, keepdims=True)
            acc = alpha * acc + _dot(p.astype(v_ref.dtype), v_ref[rows, :])
            return m_new, l, acc

        init = (jnp.full((tq, 1), NEG_INF, F32), jnp.zeros((tq, 1), F32), jnp.zeros((tq, HEAD_DIM), F32))
        m, l, acc = _loop_unrolled(nk, 8, step, init)
        o_ref[...] = (acc / l).astype(o_ref.dtype)
        lse_ref[0] = jnp.broadcast_to(m + jnp.log2(l), (tq, 128))

    (o, lse), landed = _carry_call(
        body, (q, k, v), name="flash_fwd", grid=(AX_HEADS, S // tq),
        in_specs=[pl.BlockSpec((tq, HEAD_DIM), lambda h, i: (i, h)),
                  pl.BlockSpec((S, HEAD_DIM), lambda h, i: (0, h // G)),
                  pl.BlockSpec((S, HEAD_DIM), lambda h, i: (0, h // G))],
        out_specs=[pl.BlockSpec((tq, HEAD_DIM), lambda h, i: (i, h)),
                   pl.BlockSpec((1, tq, 128), lambda h, i: (h, i, 0))],
        out_shape=[jax.ShapeDtypeStruct((S, AX_HEADS * HEAD_DIM), BF), jax.ShapeDtypeStruct((AX_HEADS, S, 128), F32)],
        semantics=("parallel", "parallel"), carried=carried)
    return o, lse, landed


def _flash_bwd(q, k, v, o, do, lse, carried=None):
    S = q.shape[0]
    tq = _pick(S, (512,))
    tk = _pick(S, (1024, 512))
    nq, nk = S // tq, S // tk
    G = AX_HEADS // AX_KV_HEADS

    def body(q_ref, k_ref, v_ref, o_ref, do_ref, lse_ref, dq_ref, dk_ref, dv_ref):
        g, i = pl.program_id(1), pl.program_id(2)

        @pl.when((g == 0) & (i == 0))
        def _():
            dk_ref[...] = jnp.zeros_like(dk_ref)
            dv_ref[...] = jnp.zeros_like(dv_ref)

        qv = q_ref[...]
        do_f = do_ref[...].astype(F32)
        dob = do_f.astype(qv.dtype)
        dob_ln2 = (do_f * LN2).astype(qv.dtype)
        delta = jnp.sum(do_f * o_ref[...].astype(F32), axis=-1, keepdims=True) * LN2
        lse_col = lse_ref[0][:, 0:1]

        def step(j, dq):
            rows = pl.ds(pl.multiple_of(j * tk, tk), tk)
            kj, vj = k_ref[rows, :], v_ref[rows, :]
            p = jnp.exp2(_dot(qv, kj, NT) - lse_col)
            dp = _dot(dob_ln2, vj, NT)
            ds = (p * (dp - delta)).astype(qv.dtype)
            dk_ref[rows, :] += _dot(ds, qv, TN)
            dv_ref[rows, :] += _dot(p.astype(dob.dtype), dob, TN)
            return dq + _dot(ds, kj)

        dq_ref[...] = _loop_unrolled(nk, 4, step, jnp.zeros((tq, HEAD_DIM), F32)).astype(dq_ref.dtype)

    q_spec = pl.BlockSpec((tq, HEAD_DIM), lambda kh, g, i: (i, kh * G + g))
    kv_spec = pl.BlockSpec((S, HEAD_DIM), lambda kh, g, i: (0, kh))
    (dq, dk, dv), landed = _carry_call(
        body, (q, k, v, o, do, lse), name="flash_bwd", grid=(AX_KV_HEADS, G, nq),
        in_specs=[q_spec, kv_spec, kv_spec, q_spec, q_spec,
                  pl.BlockSpec((1, tq, 128), lambda kh, g, i: (kh * G + g, i, 0))],
        out_specs=[q_spec, kv_spec, kv_spec],
        out_shape=[jax.ShapeDtypeStruct((S, AX_HEADS * HEAD_DIM), BF), jax.ShapeDtypeStruct((S, 256), F32),
                   jax.ShapeDtypeStruct((S, 256), F32)],
        vmem=VMEM_LARGE, semantics=("arbitrary", "arbitrary", "arbitrary"), carried=carried)
    return dq, dk, dv, landed


def _rope_angles(pos, dim, theta):
    inv = theta ** (-jnp.arange(0, dim, 2, dtype=jnp.float32) / dim)
    return pos.astype(jnp.float32)[:, None] * inv[None, :]


def _rope_tables(S):
    ang = _rope_angles(jnp.arange(S), RET_DK, RET_THETA)
    c, s = jnp.cos(ang), jnp.sin(ang)
    ret = (jnp.concatenate([c, c], -1), jnp.concatenate([-s, s], -1))
    rows = S // GRID_W
    ar, ac = _rope_angles(jnp.arange(rows), HEAD_DIM // 2, AX_THETA), _rope_angles(jnp.arange(GRID_W), HEAD_DIM // 2, AX_THETA)
    cr, sr = jnp.repeat(jnp.cos(ar), GRID_W, axis=0), jnp.repeat(jnp.sin(ar), GRID_W, axis=0)
    cc, sc = jnp.tile(jnp.cos(ac), (rows, 1)), jnp.tile(jnp.sin(ac), (rows, 1))
    ax = (jnp.concatenate([cr, cr, cc, cc], -1), jnp.concatenate([-sr, sr, -sc, sc], -1))
    return ret, ax


def _pad_tile(a):
    return jnp.pad(a.astype(F32), ((0, 8 - a.shape[0]), (0, 128 - a.shape[1])))


def _relu2_epilogue(acc):
    r = jnp.maximum(acc, 0.0)
    return acc, r * r


def _relu2_bwd_epilogue(acc, u):
    return (acc * (2.0 * jnp.maximum(u.astype(F32), 0.0)),)


def _add_epilogue(acc, res):
    return (acc + res,)


def _add_norm_epilogue(acc, res, g):
    y = acc + res
    r = lax.rsqrt(jnp.mean(y * y, axis=-1, keepdims=True) + EPS)
    return y, y * r * g


def _residual_mm(a, w, res, g_next, name):
    return _mm(a, w, name=name, extras=(res,), rows=(g_next,), out_dtypes=(F32, BF), epilogue=_add_norm_epilogue,
               vmem=VMEM_LARGE)


def _rms_bwd_epilogue(dh, x, dres, g):
    r = lax.rsqrt(jnp.mean(x * x, axis=-1, keepdims=True) + EPS)
    xh = x * r
    dxh = dh * g
    dx = r * (dxh - xh * jnp.mean(dxh * xh, axis=-1, keepdims=True)) + dres
    return dx, dx, _row_sum(dh * xh)


def _norm_bwd_mm(mm, dproj, w, x, dres, g, name):
    return mm(dproj, w, tb=True, name=name, out_dtypes=(F32, BF), extras=(x, dres), rows=(g,),
              epilogue=_rms_bwd_epilogue, n_row_sums=1, tm=512)


def _mlp_fwd(x, h, w_up, w_down, tag, g_next=None):
    u, a = _mm(h, w_up, name=f"mlp_up_{tag}", out_dtypes=(BF, BF), epilogue=_relu2_epilogue)
    if g_next is None:
        y, h_next = _mm(a, w_down, name=f"mlp_down_{tag}", extras=(x,), epilogue=_add_epilogue), None
    else:
        y, h_next = _residual_mm(a, w_down, x, g_next, f"mlp_down_{tag}")
    return y, h_next, (h, u, a)


def _mlp_bwd(x, g, w_up, w_down, saved, dy, dyb, tag, mm_exchange=None):
    h, u, a = saved
    du = _mm(dyb, w_down, tb=True, name=f"mlp_down_dx_{tag}", out_dtypes=(BF,), extras=(u,), epilogue=_relu2_bwd_epilogue)
    dw_down = _mm(a, dyb, ta=True, name=f"mlp_down_dw_{tag}", out_dtypes=(BF,))
    dw_up = _mm(h, du, ta=True, name=f"mlp_up_dw_{tag}", out_dtypes=(BF,))
    if mm_exchange is None:
        mm = _mm
    else:
        mm = lambda *args, **kw: mm_exchange([("down" + tag, dw_down)], *args, **kw)
    dx, dxb, dg = _norm_bwd_mm(mm, du, w_up, x, dy, g, f"mlp_up_dx_{tag}")
    return dx, dxb, dg, dw_up, dw_down


COL_SHARDED = ("in_even", "in_odd", "up0", "up1")


def _assemble(key, g):
    if key in COL_SHARDED:
        return g.transpose(1, 0, 2).reshape(g.shape[1], N_DEV * g.shape[2])
    return g.reshape(N_DEV * g.shape[1], g.shape[2])


def _split(key, full):
    rows, cols = full.shape
    if key in COL_SHARDED:
        return full.reshape(rows, N_DEV, cols // N_DEV).transpose(1, 0, 2)
    return full.reshape(N_DEV, rows // N_DEV, cols)


def _local_step(x, target, W, P, late=None):
    S = x.shape[0]
    W = dict(W)
    landed = {}

    def gather_while(stage):
        return None if late is None else _Exchange([s for _, s in late[stage]], gather=True)

    def arrived(stage, outs):
        for (key, _), g in zip([] if late is None else late[stage], outs):
            W[key] = _assemble(key, g)

    def exchange_while(grads):
        return None if late is None else _Exchange([_split(k, g) for k, g in grads], gather=False)

    def left(grads, outs):
        for (key, _), l in zip(grads, outs):
            landed[key] = l

    def mm_gather(stage, *args, **kw):
        if late is None:
            return _mm(*args, **kw)
        out, outs = _mm(*args, carried=gather_while(stage), **kw)
        arrived(stage, outs)
        return out

    def mm_exchange(grads, *args, **kw):
        if late is None:
            return _mm(*args, **kw)
        out, outs = _mm(*args, carried=exchange_while(grads), **kw)
        left(grads, outs)
        return out

    (cos_r, sin_r), (cos_a, sin_a) = _rope_tables(S)
    bucket = _t5_bucket_map().T
    logit = P["ret_decay_logit"]
    lg = _log_sigmoid_tile(_pad_tile(logit))
    bias = _t5_bias(P["t5_table"], bucket)
    nmix, nmlp = P["norm_mix"], P["norm_mlp"]

    h0 = _rms_fwd(x, nmix[0:1], "mix_norm_0")
    proj_e = mm_gather("in_even", h0, W["in_even"], name="in_even")
    (qr, kr, vr, qs, ks, vs), outs = _prep_even_fwd(proj_e, cos_r, sin_r, P["swa_q_norm"], P["swa_k_norm"],
                                                    gather_while("prep_even_fwd"))
    arrived("prep_even_fwd", outs)
    sf, sb = _ret_scan(kr, vr, lg, (0, 0), (1, 0), "retention_states")
    y_ret = _ret_fwd(qr, kr, vr, lg, sf, sb)
    oa = _ret_post_fwd(y_ret, proj_e, P["ret_norm"])
    ob, outs = _swa_fwd(qs, ks, vs, bias, P["swa_sink"], gather_while("swa_fwd"))
    arrived("swa_fwd", outs)
    wo_a, wo_b = W["out_even"][:1024], W["out_even"][1024:]
    x1 = _mm(oa, wo_a, name="out_even_a", extras=(x,), epilogue=_add_epilogue)
    x1, h1 = _residual_mm(ob, wo_b, x1, nmlp[0:1], "out_even_b")
    x2, h2, mlp0 = _mlp_fwd(x1, h1, W["up0"], W["down0"], "0", g_next=nmix[1:2])

    proj_o = _mm(h2, W["in_odd"], name="in_odd")
    qx, kx, vx = _prep_odd_fwd(proj_o, cos_a, sin_a, P["ax_q_norm"], P["ax_k_norm"])
    ox, lse, outs = _flash_fwd(qx, kx, vx, gather_while("flash_fwd"))
    arrived("flash_fwd", outs)
    x3, h3 = _residual_mm(ox, W["out_odd"], x2, nmlp[1:2], "out_odd")
    x4, _, mlp1 = _mlp_fwd(x3, h3, W["up1"], W["down1"], "1")

    d4, d4b, loss_tile = _loss_and_grad(x4, target)

    d3, d3b, dnmlp1, dw_up1, dw_down1 = _mlp_bwd(x3, nmlp[1:2], W["up1"], W["down1"], mlp1, d4, d4b, "1")
    dox = _mm(d3b, W["out_odd"], tb=True, name="out_odd_dx")
    dw_out_odd = _mm(ox, d3b, ta=True, name="out_odd_dw", out_dtypes=(BF,))
    grads1 = [("up1", dw_up1), ("down1", dw_down1), ("out_odd", dw_out_odd)]
    dqx, dkx, dvx, outs = _flash_bwd(qx, kx, vx, ox, dox, lse, exchange_while(grads1))
    left(grads1, outs)
    dproj_o, dgq_ax, dgk_ax = _prep_odd_bwd(proj_o, cos_a, sin_a, P["ax_q_norm"], P["ax_k_norm"], dqx, dkx, dvx)
    dw_in_odd = _mm(h2, dproj_o, ta=True, name="in_odd_dw", out_dtypes=(BF,))
    d2, d2b, dnmix1 = _norm_bwd_mm(_mm, dproj_o, W["in_odd"], x2, d3, nmix[1:2], "in_odd_dx")

    d1, d1b, dnmlp0, dw_up0, dw_down0 = _mlp_bwd(x1, nmlp[0:1], W["up0"], W["down0"], mlp0, d2, d2b, "0", mm_exchange)
    doa = _mm(d1b, wo_a, tb=True, name="out_even_a_dx")
    dob = _mm(d1b, wo_b, tb=True, name="out_even_b_dx")
    dw_out_even = jnp.concatenate([_mm(oa, d1b, ta=True, name="out_even_a_dw", out_dtypes=(BF,)),
                                   _mm(ob, d1b, ta=True, name="out_even_b_dw", out_dtypes=(BF,))], axis=0)
    dy_ret, dga, dret_norm = _ret_post_bwd(y_ret, proj_e, P["ret_norm"], doa)
    hb, hf = _ret_scan(qr, dy_ret, lg, (1, 1), (0, 1), "retention_state_grads")
    dqr, dkr, dvr, dlogit = _ret_bwd(qr, kr, vr, dy_ret, lg, logit, sf, sb, hf, hb)
    grads0 = [("in_odd", dw_in_odd), ("out_even", dw_out_even)]
    dqs, dks, dvs, dbias, dsink, outs = _swa_bwd(qs, ks, vs, dob, bias, P["swa_sink"], exchange_while(grads0))
    left(grads0, outs)
    dt5 = _t5_table_grad(dbias, bucket)
    dproj_e, dgq_swa, dgk_swa = _prep_even_bwd(proj_e, cos_r, sin_r, P["swa_q_norm"], P["swa_k_norm"],
                                               dqr, dkr, dvr, dga, dqs, dks, dvs)
    dw_in_even = mm_exchange([("up0", dw_up0)], h0, dproj_e, ta=True, name="in_even_dw", out_dtypes=(BF,))
    dx, _, dnmix0 = _norm_bwd_mm(lambda *args, **kw: mm_exchange([("in_even", dw_in_even)], *args, **kw),
                                 dproj_e, W["in_even"], x, d1, nmix[0:1], "in_even_dx")

    if late is None:
        dW = dict(grads1 + grads0, up0=dw_up0, down0=dw_down0, in_even=dw_in_even)
    else:
        dW = landed
    dP = {"norm_mix": jnp.concatenate([dnmix0, dnmix1], 0), "norm_mlp": jnp.concatenate([dnmlp0, dnmlp1], 0),
          "ret_decay_logit": dlogit[:, :, 0, 0].T, "ret_norm": dret_norm,
          "swa_q_norm": dgq_swa, "swa_k_norm": dgk_swa, "swa_sink": dsink[:, 0][None, :],
          "t5_table": dt5[:, :SWA_HEADS], "ax_q_norm": dgq_ax, "ax_k_norm": dgk_ax}
    return loss_tile, dx, dW, dP


def _cast_shards(shards):
    n = len(shards)

    def body(*refs):
        for i_ref, o_ref in zip(refs[:n], refs[n:]):
            o_ref[...] = i_ref[...].astype(o_ref.dtype)

    return pl.pallas_call(body, name="cast_shards", in_specs=[VMEM_SPEC] * n, out_specs=[VMEM_SPEC] * n,
                          out_shape=[jax.ShapeDtypeStruct(s.shape, BF) for s in shards],
                          compiler_params=pltpu.CompilerParams(vmem_limit_bytes=VMEM_SMALL))(*shards)


def _all_gather(shards):
    n = len(shards)

    def body(*refs):
        ins, outs, stage = refs[:n], refs[n:2 * n], refs[2 * n:3 * n]
        send_sems, recv_sems, local_sems = refs[3 * n:]
        me = _my_place()
        sibling = _flip(me, 1)
        chips = [_flip(me, 4), _flip(me, 2), _flip(me, 6)]

        def copy(a, k, block, to, src=None):
            dst = outs[a].at[_index(block)]
            return pltpu.make_async_remote_copy(
                src_ref=dst if src is None else src, dst_ref=dst,
                send_sem=send_sems.at[a, k], recv_sem=recv_sems.at[a, k], device_id=to, device_id_type=MESH)

        first, mine = [], []
        for a in range(n):
            stage[a][...] = ins[a][...].astype(stage[a].dtype)
            mine.append(pltpu.make_async_copy(stage[a], outs[a].at[_index(me)], local_sems.at[a]))
            mine[-1].start()
            first.append(copy(a, 0, me, sibling, src=stage[a]))
            first += [copy(a, 1 + j, me, chip, src=stage[a]) for j, chip in enumerate(chips)]
        for cp in first:
            cp.start()
        passed = []
        for a in range(n):
            for j, chip in enumerate(chips):
                copy(a, 1 + j, chip, me).wait_recv()
                passed.append(copy(a, 4 + j, chip, sibling))
                passed[-1].start()
        for a in range(n):
            copy(a, 0, sibling, me).wait_recv()
            for j, chip in enumerate(chips):
                copy(a, 4 + j, _flip(chip, 1), me).wait_recv()
        for cp in first + passed:
            cp.wait_send()
        for cp in mine:
            cp.wait()

    return pl.pallas_call(
        body, name="weights_all_gather",
        in_specs=[VMEM_SPEC] * n, out_specs=[ANY_SPEC] * n,
        out_shape=[jax.ShapeDtypeStruct((N_DEV,) + s.shape, BF) for s in shards],
        scratch_shapes=[pltpu.VMEM(s.shape, BF) for s in shards]
        + [pltpu.SemaphoreType.DMA((n, 7)), pltpu.SemaphoreType.DMA((n, 7)), pltpu.SemaphoreType.DMA((n,))],
        compiler_params=pltpu.CompilerParams(vmem_limit_bytes=VMEM_SMALL),
    )(*shards)


def _all_reduce_small(part):
    R, C = part.shape

    def body(x_ref, o_ref, land_ref, send_sems, recv_sems):
        me = _my_place()
        land_ref[_index(me)] = x_ref[...]
        copies = []
        for k in range(1, N_DEV):
            peer = _flip(me, k)
            copies.append(pltpu.make_async_remote_copy(
                src_ref=x_ref, dst_ref=land_ref.at[_index(me)],
                send_sem=send_sems.at[k - 1], recv_sem=recv_sems.at[k - 1], device_id=peer, device_id_type=MESH))
            copies[-1].start()
        for k in range(1, N_DEV):
            peer = _flip(me, k)
            pltpu.make_async_remote_copy(
                src_ref=x_ref, dst_ref=land_ref.at[_index(peer)],
                send_sem=send_sems.at[k - 1], recv_sem=recv_sems.at[k - 1], device_id=peer, device_id_type=MESH).wait_recv()
        for cp in copies:
            cp.wait_send()
        acc = land_ref[0]
        for s in range(1, N_DEV):
            acc = acc + land_ref[s]
        o_ref[...] = acc

    return pl.pallas_call(
        body, name="small_all_reduce", in_specs=[VMEM_SPEC], out_specs=VMEM_SPEC,
        out_shape=jax.ShapeDtypeStruct((R, C), F32),
        scratch_shapes=[pltpu.VMEM((N_DEV, R, C), F32), pltpu.SemaphoreType.DMA((7,)), pltpu.SemaphoreType.DMA((7,))],
    )(part)


def _adamw_math(w, g, m, v):
    m = ADAM_B1 * m + (1.0 - ADAM_B1) * g
    v = ADAM_B2 * v + (1.0 - ADAM_B2) * jnp.square(g)
    m_hat = m / (1.0 - ADAM_B1 ** ADAM_STEP)
    v_hat = v / (1.0 - ADAM_B2 ** ADAM_STEP)
    delta = -ADAM_LR * (m_hat / (jnp.sqrt(v_hat) + ADAM_EPS) + ADAM_WD * w)
    return delta, m, v


def _sum_and_adamw(landed, w, m, v, name):
    R, C = w.shape
    tr = _pick(R, (256, 128))

    def body(l_ref, w_ref, m_ref, v_ref, g_ref, d_ref, nm_ref, nv_ref):
        g = l_ref[0].astype(F32)
        for s in range(1, N_DEV):
            g = g + l_ref[s].astype(F32)
        g_ref[...] = g
        d_ref[...], nm_ref[...], nv_ref[...] = _adamw_math(w_ref[...], g, m_ref[...], v_ref[...])

    row = pl.BlockSpec((tr, C), lambda i: (i, 0))
    return pl.pallas_call(
        body, name=name, grid=(R // tr,),
        in_specs=[pl.BlockSpec((N_DEV, tr, C), lambda i: (0, i, 0)), row, row, row], out_specs=[row] * 4,
        out_shape=[jax.ShapeDtypeStruct((R, C), F32)] * 4, compiler_params=_params(("parallel",)),
    )(landed, w, m, v)


def _adamw_small(w, g, m, v):
    def body(w_ref, g_ref, m_ref, v_ref, d_ref, nm_ref, nv_ref):
        d_ref[...], nm_ref[...], nv_ref[...] = _adamw_math(w_ref[...], g_ref[...], m_ref[...], v_ref[...])

    full = pl.BlockSpec(w.shape, lambda: (0, 0))
    return pl.pallas_call(body, name="adamw_small", in_specs=[full] * 4, out_specs=[full] * 3,
                          out_shape=[jax.ShapeDtypeStruct(w.shape, F32)] * 3)(w, g, m, v)


MATRICES = ("w_in_even", "w_out_even", "w_in_odd", "w_out_odd", "w_mlp_up", "w_mlp_down")
SMALL = ("norm_mix", "norm_mlp", "ret_decay_logit", "ret_norm", "swa_q_norm", "swa_k_norm", "swa_sink",
         "t5_table", "ax_q_norm", "ax_k_norm")
MATRIX_OF = {"in_even": ("w_in_even", 0), "out_even": ("w_out_even", 0), "in_odd": ("w_in_odd", 0),
             "out_odd": ("w_out_odd", 0), "up0": ("w_mlp_up", 0), "up1": ("w_mlp_up", 1),
             "down0": ("w_mlp_down", 0), "down1": ("w_mlp_down", 1)}
GATHER_FIRST = ("in_even",)
GATHER_WHILE = {"in_even": ("up0",), "prep_even_fwd": ("out_even", "in_odd"), "swa_fwd": ("down0",),
                "flash_fwd": ("out_odd", "up1", "down1")}


SMALL_ROWS = 8
SMALL_AT = {"norm_mix": (0, 0), "norm_mlp": (2, 0), "ret_norm": (4, 0), "swa_q_norm": (5, 0), "swa_k_norm": (5, 128),
            "ax_q_norm": (5, 256), "ax_k_norm": (5, 384), "swa_sink": (5, 512), "ret_decay_logit": (5, 640),
            "t5_table": (6, 0)}
LOSS_AT = (5, 768)


def _pack_small(arrays, loss=None):
    buf = jnp.zeros((SMALL_ROWS, 1024), F32)
    for name, (r, c) in SMALL_AT.items():
        a = arrays[name].astype(F32)
        a = a.reshape(1, -1) if name in ("ret_decay_logit", "t5_table") else a.reshape(-1, a.shape[-1])
        buf = lax.dynamic_update_slice(buf, a, (r, c))
    if loss is not None:
        buf = lax.dynamic_update_slice(buf, loss.reshape(1, 1), LOSS_AT)
    return buf


def _unpack_small(buf, like):
    out = {}
    for name, (r, c) in SMALL_AT.items():
        shape = like[name].shape
        rows = 1 if name in ("ret_decay_logit", "t5_table") else math.prod(shape[:-1])
        cols = math.prod(shape) // rows
        out[name] = buf[r:r + rows, c:c + cols].reshape(shape)
    return out


def kernel(x, norm_mix, norm_mlp, w_in_even, w_out_even, ret_decay_logit, ret_norm, swa_q_norm, swa_k_norm, swa_sink, t5_table, w_in_odd, w_out_odd, ax_q_norm, ax_k_norm, w_mlp_up, w_mlp_down, loss_target, m_norm_mix, m_norm_mlp, m_w_in_even, m_w_out_even, m_ret_decay_logit, m_ret_norm, m_swa_q_norm, m_swa_k_norm, m_swa_sink, m_t5_table, m_w_in_odd, m_w_out_odd, m_ax_q_norm, m_ax_k_norm, m_w_mlp_up, m_w_mlp_down, v_norm_mix, v_norm_mlp, v_w_in_even, v_w_out_even, v_ret_decay_logit, v_ret_norm, v_swa_q_norm, v_swa_k_norm, v_swa_sink, v_t5_table, v_w_in_odd, v_w_out_odd, v_ax_q_norm, v_ax_k_norm, v_w_mlp_up, v_w_mlp_down):
    given = dict(locals())
    weights = {n: given[n] for n in MATRICES + SMALL}
    moments_m = {n: given["m_" + n] for n in MATRICES + SMALL}
    moments_v = {n: given["v_" + n] for n in MATRICES + SMALL}

    def shard(table, key):
        arg, layer = MATRIX_OF[key]
        return table[arg][layer]

    gathered = _all_gather([shard(weights, k) for k in GATHER_FIRST])
    W = {k: _assemble(k, g) for k, g in zip(GATHER_FIRST, gathered)}
    late_keys = [k for keys in GATHER_WHILE.values() for k in keys]
    cast = dict(zip(late_keys, _cast_shards([shard(weights, k) for k in late_keys])))
    late = {stage: [(k, cast[k]) for k in keys] for stage, keys in GATHER_WHILE.items()}
    P = {"norm_mix": norm_mix, "norm_mlp": norm_mlp, "ret_decay_logit": ret_decay_logit[0], "ret_norm": ret_norm,
         "swa_q_norm": swa_q_norm, "swa_k_norm": swa_k_norm, "swa_sink": swa_sink, "t5_table": t5_table,
         "ax_q_norm": ax_q_norm, "ax_k_norm": ax_k_norm}

    loss_tile, dx, landed, dP = _local_step(x[0], loss_target[0], W, P, late)

    per_key = {k: _sum_and_adamw(landed[k], shard(weights, k), shard(moments_m, k), shard(moments_v, k), "adamw_" + k)
               for k in MATRIX_OF}
    grads, deltas, new_m, new_v = {}, {}, {}, {}
    for i, out in enumerate((grads, deltas, new_m, new_v)):
        for n in MATRICES:
            out[n] = jnp.stack([per_key[k][i] for k, (arg, _) in MATRIX_OF.items() if arg == n])

    dP["ret_decay_logit"] = dP["ret_decay_logit"][None]
    total = _all_reduce_small(_pack_small(dP, loss_tile[0, 0]))
    loss = total[LOSS_AT[0], LOSS_AT[1]]
    small_d, small_m, small_v = _adamw_small(_pack_small(weights), total, _pack_small(moments_m), _pack_small(moments_v))
    like = {n: weights[n] for n in SMALL}
    for out, buf in ((grads, total), (deltas, small_d), (new_m, small_m), (new_v, small_v)):
        out.update(_unpack_small(buf, like))

    order = ("norm_mix", "norm_mlp", "w_in_even", "w_out_even", "ret_decay_logit", "ret_norm", "swa_q_norm", "swa_k_norm",
             "swa_sink", "t5_table", "w_in_odd", "w_out_odd", "ax_q_norm", "ax_k_norm", "w_mlp_up", "w_mlp_down")
    return (loss, dx[None], *[grads[n] for n in order], *[deltas[n] for n in order],
            *[new_m[n] for n in order], *[new_v[n] for n in order])
```

```python
import math

import jax
import jax.numpy as jnp
from jax import lax
from jax.experimental import pallas as pl
from jax.experimental.pallas import tpu as pltpu

F32 = jnp.float32
BF = jnp.bfloat16

D_MODEL = 1024
HEAD_DIM = 128
EPS = 1e-6
NEG_INF = -1e30
RET_HEADS, RET_DK, RET_DV = 4, 128, 256
RET_THETA = 10000.0
SWA_HEADS, SWA_KV_HEADS, WINDOW, BLOCK = 8, 2, 128, 128
T5_BUCKETS, T5_MAX_DIST = 32, 128
AX_HEADS, AX_KV_HEADS, AX_THETA, GRID_W = 8, 2, 10000.0, 64
D_FF = 4096
ATT_SCALE = HEAD_DIM ** -0.5
LN2 = math.log(2.0)
AX_SCALE = ATT_SCALE / LN2
RET_SCALE = RET_DK ** -0.5
N_DEV = 8

ADAM_LR, ADAM_B1, ADAM_B2, ADAM_EPS, ADAM_WD, ADAM_STEP = 0.001, 0.9, 0.999, 1e-08, 0.01, 10

MIB = 1024 * 1024
VMEM_SMALL = 40 * MIB
VMEM_LARGE = 56 * MIB

OFF_QA, OFF_KA, OFF_VA, OFF_GA, OFF_QB, OFF_KB, OFF_VB = 0, 512, 1024, 2048, 3072, 4096, 4352
EVEN_IN = 4608
ODD_IN = 1536

NT = (((1,), (1,)), ((), ()))
TN = (((0,), (0,)), ((), ()))
NN = (((1,), (0,)), ((), ()))


def _dot(a, b, dims=NN):
    return lax.dot_general(a, b, dims, preferred_element_type=F32)


def _params(sem=None, vmem=VMEM_SMALL):
    return pltpu.CompilerParams(dimension_semantics=sem, vmem_limit_bytes=vmem)


def _pick(n, prefs):
    for p in prefs:
        if n % p == 0:
            return p
    return n


def _row_sum(x):
    return jnp.sum(x, axis=0, keepdims=True)


def _all_sum(x):
    return jnp.sum(jnp.sum(x, axis=0, keepdims=True), axis=1, keepdims=True)


def _sigmoid(x):
    return 1.0 / (1.0 + jnp.exp(-x))


SMEM_SPEC = pl.BlockSpec(memory_space=pltpu.SMEM)
ANY_SPEC = pl.BlockSpec(memory_space=pl.ANY)
VMEM_SPEC = pl.BlockSpec(memory_space=pltpu.VMEM)
MESH = pl.DeviceIdType.MESH


def _my_place():
    return lax.axis_index("x"), lax.axis_index("y"), lax.axis_index("c")


def _flip(place, k):
    x, y, c = place
    return (1 - x if k & 4 else x, 1 - y if k & 2 else y, 1 - c if k & 1 else c)


def _index(place):
    x, y, c = place
    return 4 * x + 2 * y + c


class _Exchange:
    def __init__(self, sources, gather):
        self.sources, self.gather, self.n = list(sources), gather, len(sources)
        self.out_shape = [jax.ShapeDtypeStruct(((N_DEV,) + s.shape) if gather else s.shape, s.dtype) for s in self.sources]
        self.scratch = [pltpu.SemaphoreType.DMA((self.n, 7)), pltpu.SemaphoreType.DMA((self.n, 7)),
                        pltpu.SemaphoreType.DMA((self.n,))]

    def _source(self, ins, a, place):
        return ins[a] if self.gather else ins[a].at[_index(place)]

    def _local(self, ins, outs, sems):
        me = _my_place()
        return [pltpu.make_async_copy(self._source(ins, a, me), outs[a].at[_index(me)], sems[2].at[a]) for a in range(self.n)]

    def _remote(self, ins, outs, sems, arriving):
        send_sems, recv_sems, _ = sems
        me = _my_place()
        copies = []
        for a in range(self.n):
            for k in range(1, N_DEV):
                peer = _flip(me, k)
                copies.append(pltpu.make_async_remote_copy(
                    src_ref=self._source(ins, a, peer), dst_ref=outs[a].at[_index(peer if arriving else me)],
                    send_sem=send_sems.at[a, k - 1], recv_sem=recv_sems.at[a, k - 1], device_id=peer, device_id_type=MESH))
        return copies

    def start(self, ins, outs, sems):
        for cp in self._local(ins, outs, sems) + self._remote(ins, outs, sems, arriving=False):
            cp.start()

    def wait(self, ins, outs, sems):
        for cp in self._remote(ins, outs, sems, arriving=True):
            cp.wait_recv()
        for cp in self._remote(ins, outs, sems, arriving=False):
            cp.wait_send()
        for cp in self._local(ins, outs, sems):
            cp.wait()


def _carry_call(body, args, *, name, grid, in_specs, out_specs, out_shape, scratch_shapes=(), vmem=VMEM_SMALL,
                semantics=None, carried=None):
    if carried is None:
        outs = pl.pallas_call(body, name=name, grid=grid, in_specs=in_specs, out_specs=out_specs, out_shape=out_shape,
                              scratch_shapes=list(scratch_shapes), compiler_params=_params(semantics, vmem))(*args)
        return list(outs), []
    ni, no, ns, nc = len(in_specs), len(out_specs), len(scratch_shapes), carried.n

    def full_body(*refs):
        ins, cin = refs[:ni], refs[ni:ni + nc]
        outs, cout = refs[ni + nc:ni + nc + no], refs[ni + nc + no:ni + 2 * nc + no]
        scratch, sems = refs[ni + 2 * nc + no:ni + 2 * nc + no + ns], refs[ni + 2 * nc + no + ns:]
        ids = [pl.program_id(d) for d in range(len(grid))]
        first, last = ids[0] == 0, ids[0] == grid[0] - 1
        for d in range(1, len(grid)):
            first, last = first & (ids[d] == 0), last & (ids[d] == grid[d] - 1)

        @pl.when(first)
        def _():
            carried.start(cin, cout, sems)

        body(*ins, *outs, *scratch)

        @pl.when(last)
        def _():
            carried.wait(cin, cout, sems)

    outs = pl.pallas_call(
        full_body, name=name, grid=grid, in_specs=list(in_specs) + [ANY_SPEC] * nc,
        out_specs=list(out_specs) + [ANY_SPEC] * nc, out_shape=list(out_shape) + carried.out_shape,
        scratch_shapes=list(scratch_shapes) + carried.scratch,
        compiler_params=_params(("arbitrary",) * len(grid), vmem))(*args, *carried.sources)
    return list(outs[:no]), list(outs[no:])


def _mm(a, b, *, name, ta=False, tb=False, out_dtypes=(F32,), extras=(), rows=(), n_row_sums=0, epilogue=None,
        tm=1024, tn=1024, tk=1024, vmem=VMEM_SMALL, carried=None):
    M, K = (a.shape[1], a.shape[0]) if ta else a.shape
    N = b.shape[0] if tb else b.shape[1]
    assert K == (b.shape[1] if tb else b.shape[0])
    tm = _pick(M, (tm, 512, 256, 128))
    tn = _pick(N, (tn, 1536, 512, 384, 256, 128))
    tk = _pick(K, (tk, 1536, 512, 256, 128))
    nk = K // tk
    assert n_row_sums == 0 or tn == N
    ne, nr, no = len(extras), len(rows), len(out_dtypes)
    dims = (((0 if ta else 1,), (1 if tb else 0,)), ((), ()))
    if epilogue is None:
        epilogue = lambda acc: (acc,)

    def body(a_ref, b_ref, *rest):
        extra_refs, out_refs = rest[:ne + nr], rest[ne + nr:ne + nr + no]
        sum_refs = rest[ne + nr + no:ne + nr + no + n_row_sums]
        first_tile = pl.program_id(0) == 0

        def finish(acc):
            outs = epilogue(acc, *[r[...] for r in extra_refs])
            for o_ref, o in zip(out_refs, outs[:no]):
                o_ref[...] = o.astype(o_ref.dtype)
            for s_ref, contribution in zip(sum_refs, outs[no:]):
                @pl.when(first_tile)
                def _(s_ref=s_ref, contribution=contribution):
                    s_ref[...] = contribution

                @pl.when(jnp.logical_not(first_tile))
                def _(s_ref=s_ref, contribution=contribution):
                    s_ref[...] += contribution

        part = _dot(a_ref[...], b_ref[...], dims)
        if nk == 1:
            finish(part)
        else:
            acc_ref = rest[-1]
            k = pl.program_id(2)

            @pl.when(k == 0)
            def _():
                acc_ref[...] = part

            @pl.when(k > 0)
            def _():
                acc_ref[...] += part

            @pl.when(k == nk - 1)
            def _():
                finish(acc_ref[...])

    a_spec = pl.BlockSpec((tk, tm), lambda i, j, k: (k, i)) if ta else pl.BlockSpec((tm, tk), lambda i, j, k: (i, k))
    b_spec = pl.BlockSpec((tn, tk), lambda i, j, k: (j, k)) if tb else pl.BlockSpec((tk, tn), lambda i, j, k: (k, j))
    o_spec = pl.BlockSpec((tm, tn), lambda i, j, k: (i, j))
    row_spec = pl.BlockSpec((1, tn), lambda i, j, k: (0, j))
    outs, landed = _carry_call(
        body, (a, b, *extras, *rows), name=name, grid=(M // tm, N // tn, nk),
        in_specs=[a_spec, b_spec] + [o_spec] * ne + [row_spec] * nr,
        out_specs=[o_spec] * no + [row_spec] * n_row_sums,
        out_shape=[jax.ShapeDtypeStruct((M, N), dt) for dt in out_dtypes] + [jax.ShapeDtypeStruct((1, N), F32)] * n_row_sums,
        scratch_shapes=[pltpu.VMEM((tm, tn), F32)] if nk > 1 else [],
        vmem=vmem, semantics=("arbitrary" if n_row_sums else "parallel", "parallel", "arbitrary"), carried=carried)
    outs = outs[0] if len(outs) == 1 else outs
    return outs if carried is None else (outs, landed)


def _rms_fwd(x, g, name):
    S, Dm = x.shape
    tr = _pick(S, (512,))

    def body(x_ref, g_ref, o_ref):
        xv = x_ref[...]
        r = lax.rsqrt(jnp.mean(xv * xv, axis=-1, keepdims=True) + EPS)
        o_ref[...] = (xv * r * g_ref[...]).astype(o_ref.dtype)

    row = pl.BlockSpec((tr, Dm), lambda i: (i, 0))
    return pl.pallas_call(
        body, name=name, grid=(S // tr,),
        in_specs=[row, pl.BlockSpec((1, Dm), lambda i: (0, 0))], out_specs=row,
        out_shape=jax.ShapeDtypeStruct((S, Dm), BF), compiler_params=_params(("parallel",)),
    )(x, g)


def _loss_and_grad(y, target):
    S, Dm = y.shape
    tr = _pick(S, (512,))
    n = S // tr

    def body(y_ref, t_ref, dy_ref, dyb_ref, loss_ref, acc_ref):
        i = pl.program_id(0)
        e = y_ref[...] - t_ref[...]
        d = e * (1.0 / Dm)
        dy_ref[...] = d
        dyb_ref[...] = d.astype(dyb_ref.dtype)

        @pl.when(i == 0)
        def _():
            acc_ref[...] = jnp.zeros_like(acc_ref)

        acc_ref[...] += _row_sum(e * e)

        @pl.when(i == n - 1)
        def _():
            loss_ref[...] = jnp.broadcast_to(_all_sum(acc_ref[...]) * (0.5 / Dm), loss_ref.shape)

    row = pl.BlockSpec((tr, Dm), lambda i: (i, 0))
    return pl.pallas_call(
        body, name="loss_head", grid=(n,),
        in_specs=[row, row], out_specs=[row, row, pl.BlockSpec((8, 128), lambda i: (0, 0))],
        out_shape=[jax.ShapeDtypeStruct((S, Dm), F32), jax.ShapeDtypeStruct((S, Dm), BF),
                   jax.ShapeDtypeStruct((8, 128), F32)],
        scratch_shapes=[pltpu.VMEM((1, Dm), F32)],
        compiler_params=_params(("arbitrary",)),
    )(y, target)


def _partner(x, half):
    if half == 64:
        return pltpu.roll(x, 64, 1)
    lane = lax.broadcasted_iota(jnp.int32, x.shape, 1)
    return jnp.where((lane % (2 * half)) < half, pltpu.roll(x, 128 - half, 1), pltpu.roll(x, half, 1))


def _rope(x, cos, sin, half):
    return x * cos + _partner(x, half) * sin


def _rope_t(dy, cos, sin, half):
    return dy * cos - _partner(dy, half) * sin


def _head_norm(x):
    r = lax.rsqrt(jnp.mean(x * x, axis=-1, keepdims=True) + EPS)
    return x * r, r


def _head_norm_bwd(dxh, xh, r):
    return r * (dxh - xh * jnp.mean(dxh * xh, axis=-1, keepdims=True))


def _cols(ref, off, width=HEAD_DIM):
    return ref[:, off:off + width]


def _prep_even_fwd(proj, cos, sin, gq, gk, carried=None):
    S = proj.shape[0]
    tr = _pick(S, (256,))

    def body(p_ref, cos_ref, sin_ref, gq_ref, gk_ref, qr_ref, kr_ref, vr_ref, qs_ref, ks_ref, vs_ref):
        cos_v, sin_v = cos_ref[...], sin_ref[...]
        for h in range(RET_HEADS):
            o = h * RET_DK
            qr_ref[:, o:o + RET_DK] = _rope(_cols(p_ref, OFF_QA + o), cos_v, sin_v, 64).astype(qr_ref.dtype)
            kr_ref[:, o:o + RET_DK] = (_rope(_cols(p_ref, OFF_KA + o), cos_v, sin_v, 64) * RET_SCALE).astype(kr_ref.dtype)
        vr_ref[...] = p_ref[:, OFF_VA:OFF_VA + 1024].astype(vr_ref.dtype)
        for h in range(SWA_HEADS):
            o = h * HEAD_DIM
            xh, _ = _head_norm(_cols(p_ref, OFF_QB + o))
            qs_ref[:, o:o + HEAD_DIM] = (xh * gq_ref[...] * ATT_SCALE).astype(qs_ref.dtype)
        for h in range(SWA_KV_HEADS):
            o = h * HEAD_DIM
            xh, _ = _head_norm(_cols(p_ref, OFF_KB + o))
            ks_ref[:, o:o + HEAD_DIM] = (xh * gk_ref[...]).astype(ks_ref.dtype)
        vs_ref[...] = p_ref[:, OFF_VB:OFF_VB + 256].astype(vs_ref.dtype)

    def row(w):
        return pl.BlockSpec((tr, w), lambda i: (i, 0))

    vec = pl.BlockSpec((1, HEAD_DIM), lambda i: (0, 0))
    widths = (512, 512, 1024, 1024, 256, 256)
    return _carry_call(
        body, (proj, cos, sin, gq, gk), name="prep_even_fwd", grid=(S // tr,),
        in_specs=[row(EVEN_IN), row(128), row(128), vec, vec],
        out_specs=[row(w) for w in widths],
        out_shape=[jax.ShapeDtypeStruct((S, w), BF) for w in widths],
        semantics=("parallel",), carried=carried)


def _prep_even_bwd(proj, cos, sin, gq, gk, dqr, dkr, dvr, dga, dqs, dks, dvs):
    S = proj.shape[0]
    tr = _pick(S, (256,))

    def body(p_ref, cos_ref, sin_ref, gq_ref, gk_ref, dqr_ref, dkr_ref, dvr_ref, dga_ref, dqs_ref, dks_ref,
             dvs_ref, dp_ref, dgq_ref, dgk_ref):
        cos_v, sin_v = cos_ref[...], sin_ref[...]
        dt = dp_ref.dtype
        for h in range(RET_HEADS):
            o = h * RET_DK
            dp_ref[:, OFF_QA + o:OFF_QA + o + RET_DK] = _rope_t(_cols(dqr_ref, o).astype(F32), cos_v, sin_v, 64).astype(dt)
            dp_ref[:, OFF_KA + o:OFF_KA + o + RET_DK] = _rope_t(_cols(dkr_ref, o).astype(F32) * RET_SCALE, cos_v, sin_v, 64).astype(dt)
        dp_ref[:, OFF_VA:OFF_VA + 1024] = dvr_ref[...].astype(dt)
        dp_ref[:, OFF_GA:OFF_GA + 1024] = dga_ref[...].astype(dt)
        dgq = jnp.zeros((1, HEAD_DIM), F32)
        for h in range(SWA_HEADS):
            o = h * HEAD_DIM
            xh, r = _head_norm(_cols(p_ref, OFF_QB + o))
            dy = _cols(dqs_ref, o).astype(F32) * ATT_SCALE
            dgq = dgq + _row_sum(dy * xh)
            dp_ref[:, OFF_QB + o:OFF_QB + o + HEAD_DIM] = _head_norm_bwd(dy * gq_ref[...], xh, r).astype(dt)
        dgk = jnp.zeros((1, HEAD_DIM), F32)
        for h in range(SWA_KV_HEADS):
            o = h * HEAD_DIM
            xh, r = _head_norm(_cols(p_ref, OFF_KB + o))
            dy = _cols(dks_ref, o)
            dgk = dgk + _row_sum(dy * xh)
            dp_ref[:, OFF_KB + o:OFF_KB + o + HEAD_DIM] = _head_norm_bwd(dy * gk_ref[...], xh, r).astype(dt)
        dp_ref[:, OFF_VB:OFF_VB + 256] = dvs_ref[...].astype(dt)

        @pl.when(pl.program_id(0) == 0)
        def _():
            dgq_ref[...] = jnp.zeros_like(dgq_ref)
            dgk_ref[...] = jnp.zeros_like(dgk_ref)

        dgq_ref[...] += dgq
        dgk_ref[...] += dgk

    def row(w):
        return pl.BlockSpec((tr, w), lambda i: (i, 0))

    vec = pl.BlockSpec((1, HEAD_DIM), lambda i: (0, 0))
    return pl.pallas_call(
        body, name="prep_even_bwd", grid=(S // tr,),
        in_specs=[row(EVEN_IN), row(128), row(128), vec, vec, row(512), row(512), row(1024), row(1024),
                  row(1024), row(256), row(256)],
        out_specs=[row(EVEN_IN), vec, vec],
        out_shape=[jax.ShapeDtypeStruct((S, EVEN_IN), BF), jax.ShapeDtypeStruct((1, HEAD_DIM), F32),
                   jax.ShapeDtypeStruct((1, HEAD_DIM), F32)],
        compiler_params=_params(("arbitrary",)),
    )(proj, cos, sin, gq, gk, dqr, dkr, dvr, dga, dqs, dks, dvs)


RET_CHUNK = 512
def _log_sigmoid_tile(logit_tile):
    def body(x_ref, o_ref):
        xv = x_ref[...]
        t = jnp.exp(-jnp.abs(xv))
        log1p_t = jnp.where(t < 1e-3, t * (1.0 - 0.5 * t), jnp.log(1.0 + t))
        o_ref[...] = jnp.minimum(xv, 0.0) - log1p_t

    full = pl.BlockSpec((8, 128), lambda: (0, 0))
    return pl.pallas_call(body, name="log_sigmoid", in_specs=[full], out_specs=full,
                          out_shape=jax.ShapeDtypeStruct((8, 128), F32))(logit_tile)


def _decay(diff, lf, lb):
    return jnp.exp(jnp.where(diff >= 0, lf * diff, -(lb * diff)))


def _col_iota(n):
    return lax.broadcasted_iota(jnp.int32, (n, 1), 0).astype(F32)


def _ret_scan(x, z, lg, asc, desc, name):
    S = x.shape[0]
    C = _pick(S, (RET_CHUNK,))
    nc = S // C
    (arow, aoff), (drow, doff) = asc, desc

    def body(lg_ref, xa_ref, za_ref, xd_ref, zd_ref, asc_ref, desc_ref, sa_ref, sd_ref):
        t = pl.program_id(0)

        @pl.when(t == 0)
        def _():
            sa_ref[...] = jnp.zeros_like(sa_ref)
            sd_ref[...] = jnp.zeros_like(sd_ref)

        j = _col_iota(C)
        for h in range(RET_HEADS):
            la, ld = lg_ref[arow, h], lg_ref[drow, h]
            kc, vc = slice(h * RET_DK, (h + 1) * RET_DK), slice(h * RET_DV, (h + 1) * RET_DV)
            asc_ref[h, 0] = sa_ref[h]
            desc_ref[h, 0] = sd_ref[h]
            xa = (xa_ref[:, kc].astype(F32) * jnp.exp(la * (C - 1 + aoff - j))).astype(xa_ref.dtype)
            xd = (xd_ref[:, kc].astype(F32) * jnp.exp(ld * (j + doff))).astype(xd_ref.dtype)
            sa_ref[h] = jnp.exp(jnp.full((1, RET_DV), la * C, F32)) * sa_ref[h] + _dot(xa, za_ref[:, vc], TN)
            sd_ref[h] = jnp.exp(jnp.full((1, RET_DV), ld * C, F32)) * sd_ref[h] + _dot(xd, zd_ref[:, vc], TN)

    state = jax.ShapeDtypeStruct((RET_HEADS, nc, RET_DK, RET_DV), F32)
    qk_w, v_w = RET_HEADS * RET_DK, RET_HEADS * RET_DV
    return pl.pallas_call(
        body, name=name, grid=(nc,),
        in_specs=[SMEM_SPEC,
                  pl.BlockSpec((C, qk_w), lambda t: (t, 0)), pl.BlockSpec((C, v_w), lambda t: (t, 0)),
                  pl.BlockSpec((C, qk_w), lambda t: (nc - 1 - t, 0)), pl.BlockSpec((C, v_w), lambda t: (nc - 1 - t, 0))],
        out_specs=[pl.BlockSpec((RET_HEADS, 1, RET_DK, RET_DV), lambda t: (0, t, 0, 0)),
                   pl.BlockSpec((RET_HEADS, 1, RET_DK, RET_DV), lambda t: (0, nc - 1 - t, 0, 0))],
        out_shape=[state, state],
        scratch_shapes=[pltpu.VMEM((RET_HEADS, RET_DK, RET_DV), F32), pltpu.VMEM((RET_HEADS, RET_DK, RET_DV), F32)],
        compiler_params=_params(("arbitrary",)),
    )(lg, x, z, x, z)


def _ret_fwd(q, k, v, lg, sf, sb, carried=None):
    S = q.shape[0]
    C = _pick(S, (RET_CHUNK,))

    def body(lg_ref, q_ref, k_ref, v_ref, sf_ref, sb_ref, y_ref):
        h = pl.program_id(0)
        lf, lb = lg_ref[0, h], lg_ref[1, h]
        qv = q_ref[...]
        dt = qv.dtype
        diff = (lax.broadcasted_iota(jnp.int32, (C, C), 0) - lax.broadcasted_iota(jnp.int32, (C, C), 1)).astype(F32)
        y = _dot((_dot(qv, k_ref[...], NT) * _decay(diff, lf, lb)).astype(dt), v_ref[...])
        r = _col_iota(C)
        qf = qv.astype(F32)
        y = y + _dot((qf * jnp.exp(lf * (r + 1.0))).astype(dt), sf_ref[0, 0].astype(dt))
        y_ref[...] = y + _dot((qf * jnp.exp(lb * (C - r))).astype(dt), sb_ref[0, 0].astype(dt))

    state = pl.BlockSpec((1, 1, RET_DK, RET_DV), lambda h, c: (h, c, 0, 0))
    qk = pl.BlockSpec((C, RET_DK), lambda h, c: (c, h))
    vy = pl.BlockSpec((C, RET_DV), lambda h, c: (c, h))
    (y,), landed = _carry_call(
        body, (lg, q, k, v, sf, sb), name="retention_fwd", grid=(RET_HEADS, S // C),
        in_specs=[SMEM_SPEC, qk, qk, vy, state, state], out_specs=[vy],
        out_shape=[jax.ShapeDtypeStruct((S, RET_HEADS * RET_DV), F32)],
        semantics=("parallel", "parallel"), carried=carried)
    return y, landed


def _ret_bwd(q, k, v, dy, lg, logit, sf, sb, hf, hb):
    S = q.shape[0]
    C = _pick(S, (RET_CHUNK,))
    nc = S // C

    def body(lg_ref, logit_ref, q_ref, k_ref, v_ref, dy_ref, sf_ref, sb_ref, hf_ref, hb_ref,
             dq_ref, dk_ref, dv_ref, dlg_ref, acc_ref):
        h, c = pl.program_id(0), pl.program_id(1)
        lf, lb = lg_ref[0, h], lg_ref[1, h]

        @pl.when(c == 0)
        def _():
            acc_ref[...] = jnp.zeros_like(acc_ref)

        qv, kv, vv, dyv = q_ref[...], k_ref[...], v_ref[...], dy_ref[...]
        dt = qv.dtype
        qf, kf = qv.astype(F32), kv.astype(F32)
        diff = (lax.broadcasted_iota(jnp.int32, (C, C), 0) - lax.broadcasted_iota(jnp.int32, (C, C), 1)).astype(F32)
        dec = _decay(diff, lf, lb)
        sc = _dot(qv, kv, NT) * dec
        dp = _dot(dyv, vv, NT)
        da = (dp * dec).astype(dt)
        dq = _dot(da, kv)
        dk = _dot(da, qv, TN)
        dv = _dot(sc.astype(dt), dyv, TN)
        w = sc * dp * diff
        tot_w, tot_f = _all_sum(w), _all_sum(jnp.where(diff >= 0, w, 0.0))
        d_lf, d_lb = tot_f, tot_f - tot_w
        r = _col_iota(C)
        a, b = jnp.exp(lf * (r + 1.0)), jnp.exp(lb * (C - r))
        e, f = jnp.exp(lf * (C - 1.0 - r)), jnp.exp(lb * r)
        sfv, sbv, hfv, hbv = sf_ref[0, 0], sb_ref[0, 0], hf_ref[0, 0], hb_ref[0, 0]
        t_f, t_b = _dot(dyv, sfv.astype(dt), NT), _dot(dyv, sbv.astype(dt), NT)
        u_f, u_b = _dot(vv, hfv.astype(dt), NT), _dot(vv, hbv.astype(dt), NT)
        dq_ref[...] = (dq + a * t_f + b * t_b).astype(dq_ref.dtype)
        dk_ref[...] = (dk + e * u_f + f * u_b).astype(dk_ref.dtype)
        dv_ref[...] = (dv + _dot((kf * e).astype(dt), hfv.astype(dt)) + _dot((kf * f).astype(dt), hbv.astype(dt))).astype(dv_ref.dtype)
        row_q_f = jnp.sum(qf * t_f, axis=-1, keepdims=True)
        row_q_b = jnp.sum(qf * t_b, axis=-1, keepdims=True)
        row_k_f = jnp.sum(kf * u_f, axis=-1, keepdims=True)
        row_k_b = jnp.sum(kf * u_b, axis=-1, keepdims=True)
        gf_c = jnp.exp(jnp.full((1, 1), lf * C, F32))
        gb_c = jnp.exp(jnp.full((1, 1), lb * C, F32))
        d_lf = d_lf + _all_sum((r + 1.0) * a * row_q_f + (C - 1.0 - r) * e * row_k_f) + C * gf_c * _all_sum(hfv * sfv)
        d_lb = d_lb + _all_sum((C - r) * b * row_q_b + r * f * row_k_b) + C * gb_c * _all_sum(hbv * sbv)
        acc_ref[0] += jnp.broadcast_to(d_lf, (8, 128))
        acc_ref[1] += jnp.broadcast_to(d_lb, (8, 128))

        @pl.when(c == nc - 1)
        def _():
            for d in range(2):
                gate = 1.0 / (1.0 + jnp.exp(jnp.full((8, 128), logit_ref[d, h], F32)))
                dlg_ref[0, d] = acc_ref[d] * gate

    state = pl.BlockSpec((1, 1, RET_DK, RET_DV), lambda h, c: (h, c, 0, 0))
    qk = pl.BlockSpec((C, RET_DK), lambda h, c: (c, h))
    vy = pl.BlockSpec((C, RET_DV), lambda h, c: (c, h))
    return pl.pallas_call(
        body, name="retention_bwd", grid=(RET_HEADS, nc),
        in_specs=[SMEM_SPEC, SMEM_SPEC, qk, qk, vy, vy, state, state, state, state],
        out_specs=[qk, qk, vy, pl.BlockSpec((1, 2, 8, 128), lambda h, c: (h, 0, 0, 0))],
        out_shape=[jax.ShapeDtypeStruct((S, RET_HEADS * RET_DK), BF), jax.ShapeDtypeStruct((S, RET_HEADS * RET_DK), BF),
                   jax.ShapeDtypeStruct((S, RET_HEADS * RET_DV), BF),
                   jax.ShapeDtypeStruct((RET_HEADS, 2, 8, 128), F32)],
        scratch_shapes=[pltpu.VMEM((2, 8, 128), F32)],
        compiler_params=_params(("parallel", "arbitrary")),
    )(lg, logit, q, k, v, dy, sf, sb, hf, hb)


def _ret_post_fwd(y, proj, gn):
    S = y.shape[0]
    tr = _pick(S, (512,))

    def body(y_ref, g_ref, gn_ref, o_ref):
        for h in range(RET_HEADS):
            o = h * RET_DV
            yh, _ = _head_norm(_cols(y_ref, o, RET_DV))
            gate = _cols(g_ref, o, RET_DV)
            o_ref[:, o:o + RET_DV] = (gate * _sigmoid(gate) * (yh * gn_ref[:, o:o + RET_DV])).astype(o_ref.dtype)

    row = pl.BlockSpec((tr, 1024), lambda i: (i, 0))
    return pl.pallas_call(
        body, name="retention_post_fwd", grid=(S // tr,),
        in_specs=[row, pl.BlockSpec((tr, 1024), lambda i: (i, OFF_GA // 1024)), pl.BlockSpec((1, 1024), lambda i: (0, 0))],
        out_specs=row, out_shape=jax.ShapeDtypeStruct((S, 1024), BF), compiler_params=_params(("parallel",)),
    )(y, proj, gn)


def _ret_post_bwd(y, proj, gn, do):
    S = y.shape[0]
    tr = _pick(S, (512,))

    def body(y_ref, g_ref, gn_ref, do_ref, dy_ref, dg_ref, dgn_ref):
        @pl.when(pl.program_id(0) == 0)
        def _():
            dgn_ref[...] = jnp.zeros_like(dgn_ref)

        for h in range(RET_HEADS):
            o = h * RET_DV
            yh, r = _head_norm(_cols(y_ref, o, RET_DV))
            gate = _cols(g_ref, o, RET_DV)
            gnh = gn_ref[:, o:o + RET_DV]
            dout = _cols(do_ref, o, RET_DV).astype(F32)
            sg = _sigmoid(gate)
            dz = dout * (gate * sg)
            dg_ref[:, o:o + RET_DV] = (dout * (yh * gnh) * (sg * (1.0 + gate * (1.0 - sg)))).astype(dg_ref.dtype)
            dgn_ref[:, o:o + RET_DV] += _row_sum(dz * yh)
            dy_ref[:, o:o + RET_DV] = _head_norm_bwd(dz * gnh, yh, r).astype(dy_ref.dtype)

    row = pl.BlockSpec((tr, 1024), lambda i: (i, 0))
    vec = pl.BlockSpec((1, 1024), lambda i: (0, 0))
    return pl.pallas_call(
        body, name="retention_post_bwd", grid=(S // tr,),
        in_specs=[row, pl.BlockSpec((tr, 1024), lambda i: (i, OFF_GA // 1024)), vec, row],
        out_specs=[row, row, vec],
        out_shape=[jax.ShapeDtypeStruct((S, 1024), BF), jax.ShapeDtypeStruct((S, 1024), BF),
                   jax.ShapeDtypeStruct((1, 1024), F32)],
        compiler_params=_params(("arbitrary",)),
    )(y, proj, gn, do)


def _t5_bucket_map():
    r = jnp.arange(BLOCK)
    j = jnp.arange(3 * BLOCK)
    rel = j[None, :] - BLOCK - r[:, None]
    nb = T5_BUCKETS // 2
    max_exact = nb // 2
    ret = jnp.where(rel > 0, nb, 0)
    n = jnp.abs(rel)
    nf = jnp.maximum(n, 1).astype(jnp.float32)
    large = max_exact + (jnp.log(nf / max_exact) / math.log(T5_MAX_DIST / max_exact)
                         * (nb - max_exact)).astype(jnp.int32)
    large = jnp.minimum(large, nb - 1)
    bucket = ret + jnp.where(n < max_exact, n, large)
    return jnp.where(jnp.abs(rel) <= WINDOW, bucket, -1).astype(jnp.int32)


SWA_G = SWA_HEADS // SWA_KV_HEADS
SWA_LANES = SWA_G * BLOCK


def _t5_bias(table, bucket_t):
    def body(t_ref, b_ref, o_ref):
        bk = b_ref[...]
        for h in range(SWA_HEADS):
            acc = jnp.full(bk.shape, NEG_INF, F32)
            for b in range(T5_BUCKETS):
                acc = jnp.where(bk == b, t_ref[b, h], acc)
            o_ref[h // SWA_G, :, (h % SWA_G) * BLOCK:(h % SWA_G + 1) * BLOCK] = acc

    return pl.pallas_call(
        body, name="t5_bias", in_specs=[SMEM_SPEC, pl.BlockSpec((3 * BLOCK, BLOCK), lambda: (0, 0))],
        out_specs=pl.BlockSpec((SWA_KV_HEADS, 3 * BLOCK, SWA_LANES), lambda: (0, 0, 0)),
        out_shape=jax.ShapeDtypeStruct((SWA_KV_HEADS, 3 * BLOCK, SWA_LANES), F32),
    )(table, bucket_t)


def _t5_table_grad(dbias, bucket_t):
    def body(d_ref, b_ref, o_ref):
        bk = b_ref[...]
        lane = lax.broadcasted_iota(jnp.int32, (1, 128), 1)
        for b in range(T5_BUCKETS):
            hit = bk == b
            row = jnp.zeros((1, 128), F32)
            for h in range(SWA_HEADS):
                d = d_ref[h // SWA_G, :, (h % SWA_G) * BLOCK:(h % SWA_G + 1) * BLOCK]
                row = row + jnp.where(lane == h, _all_sum(jnp.where(hit, d, 0.0)), 0.0)
            o_ref[b:b + 1, :] = row

    return pl.pallas_call(
        body, name="t5_table_grad",
        in_specs=[pl.BlockSpec((SWA_KV_HEADS, 3 * BLOCK, SWA_LANES), lambda: (0, 0, 0)),
                  pl.BlockSpec((3 * BLOCK, BLOCK), lambda: (0, 0))],
        out_specs=pl.BlockSpec((T5_BUCKETS, 128), lambda: (0, 0)),
        out_shape=jax.ShapeDtypeStruct((T5_BUCKETS, 128), F32),
    )(dbias, bucket_t)


def _swa_scores(i, nb, q4, kw, bias_t, sink_row):
    s = _dot(kw, q4, NT) + bias_t
    row = lax.broadcasted_iota(jnp.int32, s.shape, 0)
    first_row = jnp.where(i == 0, BLOCK, 0)
    end_row = jnp.where(i == nb - 1, 2 * BLOCK, 3 * BLOCK)
    s = jnp.where((row < first_row) | (row >= end_row), NEG_INF, s)
    m = jnp.maximum(jnp.max(s, axis=0, keepdims=True), sink_row)
    p = jnp.exp(s - m)
    e_sink = jnp.exp(sink_row - m)
    inv = 1.0 / (jnp.sum(p, axis=0, keepdims=True) + e_sink)
    return p * inv, e_sink * inv


def _swa_group(q_ref, sink_ref, kh):
    heads = range(kh * SWA_G, (kh + 1) * SWA_G)
    q4 = jnp.concatenate([_cols(q_ref, h * HEAD_DIM) for h in heads], axis=0)
    sink_row = jnp.concatenate([jnp.full((1, BLOCK), sink_ref[0, h], F32) for h in heads], axis=1)
    return q4, sink_row


def _swa_unstack(ref, kh, x_t):
    for g in range(SWA_G):
        h = kh * SWA_G + g
        ref[:, h * HEAD_DIM:(h + 1) * HEAD_DIM] = x_t[:, g * BLOCK:(g + 1) * BLOCK].T.astype(ref.dtype)


def _swa_window(ref, i, nb, off):
    prev, nxt = jnp.maximum(i - 1, 0), jnp.minimum(i + 1, nb - 1)
    rows = [pl.ds(pl.multiple_of(b * BLOCK, BLOCK), BLOCK) for b in (prev, i, nxt)]
    return jnp.concatenate([ref[r, off:off + HEAD_DIM] for r in rows], axis=0), rows


def _swa_fwd(q, k, v, bias, sink, carried=None):
    S = q.shape[0]
    nb = S // BLOCK

    def body(sink_ref, q_ref, k_ref, v_ref, bias_ref, o_ref):
        i = pl.program_id(0)
        for kh in range(SWA_KV_HEADS):
            kw, _ = _swa_window(k_ref, i, nb, kh * HEAD_DIM)
            vw, _ = _swa_window(v_ref, i, nb, kh * HEAD_DIM)
            q4, sink_row = _swa_group(q_ref, sink_ref, kh)
            p, _ = _swa_scores(i, nb, q4, kw, bias_ref[kh], sink_row)
            _swa_unstack(o_ref, kh, _dot(vw, p.astype(vw.dtype), TN))

    full_kv = pl.BlockSpec((S, SWA_KV_HEADS * HEAD_DIM), lambda i: (0, 0))
    (o,), landed = _carry_call(
        body, (sink, q, k, v, bias), name="swa_fwd", grid=(nb,),
        in_specs=[SMEM_SPEC, pl.BlockSpec((BLOCK, 1024), lambda i: (i, 0)), full_kv, full_kv,
                  pl.BlockSpec((SWA_KV_HEADS, 3 * BLOCK, SWA_LANES), lambda i: (0, 0, 0))],
        out_specs=[pl.BlockSpec((BLOCK, 1024), lambda i: (i, 0))],
        out_shape=[jax.ShapeDtypeStruct((S, 1024), BF)], semantics=("parallel",), carried=carried)
    return o, landed


def _swa_bwd(q, k, v, do, bias, sink, carried=None):
    S = q.shape[0]
    nb = S // BLOCK

    def body(sink_ref, q_ref, k_ref, v_ref, do_ref, bias_ref, dq_ref, dk_ref, dv_ref, dbias_ref, dsink_ref):
        i = pl.program_id(0)

        @pl.when(i == 0)
        def _():
            dk_ref[...] = jnp.zeros_like(dk_ref)
            dv_ref[...] = jnp.zeros_like(dv_ref)
            dbias_ref[...] = jnp.zeros_like(dbias_ref)
            dsink_ref[...] = jnp.zeros_like(dsink_ref)

        for kh in range(SWA_KV_HEADS):
            off = kh * HEAD_DIM
            kw, rows = _swa_window(k_ref, i, nb, off)
            vw, _ = _swa_window(v_ref, i, nb, off)
            q4, sink_row = _swa_group(q_ref, sink_ref, kh)
            p, p_sink = _swa_scores(i, nb, q4, kw, bias_ref[kh], sink_row)
            do4 = jnp.concatenate([_cols(do_ref, (kh * SWA_G + g) * HEAD_DIM) for g in range(SWA_G)], axis=0).astype(vw.dtype)
            dp = _dot(vw, do4, NT)
            delta = jnp.sum(p * dp, axis=0, keepdims=True)
            ds = p * (dp - delta)
            dsb = ds.astype(q4.dtype)
            _swa_unstack(dq_ref, kh, _dot(kw, dsb, TN))
            dkw = _dot(dsb, q4)
            dvw = _dot(p.astype(do4.dtype), do4)
            dbias_ref[kh] += ds
            sink_term = p_sink * delta
            for g in range(SWA_G):
                h = kh * SWA_G + g
                dsink_ref[h:h + 1, :] += jnp.broadcast_to(-_all_sum(sink_term[:, g * BLOCK:(g + 1) * BLOCK]), (1, 128))
            for b, r in enumerate(rows):
                dk_ref[r, off:off + HEAD_DIM] += dkw[b * BLOCK:(b + 1) * BLOCK]
                dv_ref[r, off:off + HEAD_DIM] += dvw[b * BLOCK:(b + 1) * BLOCK]

    full_kv = pl.BlockSpec((S, SWA_KV_HEADS * HEAD_DIM), lambda i: (0, 0))
    blk = pl.BlockSpec((BLOCK, 1024), lambda i: (i, 0))
    bias_spec = pl.BlockSpec((SWA_KV_HEADS, 3 * BLOCK, SWA_LANES), lambda i: (0, 0, 0))
    outs, landed = _carry_call(
        body, (sink, q, k, v, do, bias), name="swa_bwd", grid=(nb,),
        in_specs=[SMEM_SPEC, blk, full_kv, full_kv, blk, bias_spec],
        out_specs=[blk, full_kv, full_kv, bias_spec, pl.BlockSpec((8, 128), lambda i: (0, 0))],
        out_shape=[jax.ShapeDtypeStruct((S, 1024), BF), jax.ShapeDtypeStruct((S, 256), F32),
                   jax.ShapeDtypeStruct((S, 256), F32),
                   jax.ShapeDtypeStruct((SWA_KV_HEADS, 3 * BLOCK, SWA_LANES), F32), jax.ShapeDtypeStruct((8, 128), F32)],
        vmem=VMEM_LARGE, semantics=("arbitrary",), carried=carried)
    return (*outs, landed)


def _prep_odd_fwd(proj, cos, sin, gq, gk):
    S = proj.shape[0]
    tr = _pick(S, (512,))

    def body(p_ref, cos_ref, sin_ref, gq_ref, gk_ref, q_ref, k_ref, v_ref):
        cos_v, sin_v = cos_ref[...], sin_ref[...]
        for h in range(AX_HEADS):
            o = h * HEAD_DIM
            xh, _ = _head_norm(_cols(p_ref, o))
            q_ref[:, o:o + HEAD_DIM] = (_rope(xh * gq_ref[...], cos_v, sin_v, 32) * AX_SCALE).astype(q_ref.dtype)
        for h in range(AX_KV_HEADS):
            o = h * HEAD_DIM
            xh, _ = _head_norm(_cols(p_ref, 1024 + o))
            k_ref[:, o:o + HEAD_DIM] = _rope(xh * gk_ref[...], cos_v, sin_v, 32).astype(k_ref.dtype)
        v_ref[...] = p_ref[:, 1280:1536].astype(v_ref.dtype)

    def row(w):
        return pl.BlockSpec((tr, w), lambda i: (i, 0))

    vec = pl.BlockSpec((1, HEAD_DIM), lambda i: (0, 0))
    return pl.pallas_call(
        body, name="prep_odd_fwd", grid=(S // tr,),
        in_specs=[row(ODD_IN), row(128), row(128), vec, vec], out_specs=[row(1024), row(256), row(256)],
        out_shape=[jax.ShapeDtypeStruct((S, w), BF) for w in (1024, 256, 256)],
        compiler_params=_params(("parallel",)),
    )(proj, cos, sin, gq, gk)


def _prep_odd_bwd(proj, cos, sin, gq, gk, dq, dk, dv):
    S = proj.shape[0]
    tr = _pick(S, (512,))

    def body(p_ref, cos_ref, sin_ref, gq_ref, gk_ref, dq_ref, dk_ref, dv_ref, dp_ref, dgq_ref, dgk_ref):
        cos_v, sin_v = cos_ref[...], sin_ref[...]
        dt = dp_ref.dtype
        dgq = jnp.zeros((1, HEAD_DIM), F32)
        for h in range(AX_HEADS):
            o = h * HEAD_DIM
            xh, r = _head_norm(_cols(p_ref, o))
            dy = _rope_t(_cols(dq_ref, o).astype(F32) * AX_SCALE, cos_v, sin_v, 32)
            dgq = dgq + _row_sum(dy * xh)
            dp_ref[:, o:o + HEAD_DIM] = _head_norm_bwd(dy * gq_ref[...], xh, r).astype(dt)
        dgk = jnp.zeros((1, HEAD_DIM), F32)
        for h in range(AX_KV_HEADS):
            o = h * HEAD_DIM
            xh, r = _head_norm(_cols(p_ref, 1024 + o))
            dy = _rope_t(_cols(dk_ref, o), cos_v, sin_v, 32)
            dgk = dgk + _row_sum(dy * xh)
            dp_ref[:, 1024 + o:1024 + o + HEAD_DIM] = _head_norm_bwd(dy * gk_ref[...], xh, r).astype(dt)
        dp_ref[:, 1280:1536] = dv_ref[...].astype(dt)

        @pl.when(pl.program_id(0) == 0)
        def _():
            dgq_ref[...] = jnp.zeros_like(dgq_ref)
            dgk_ref[...] = jnp.zeros_like(dgk_ref)

        dgq_ref[...] += dgq
        dgk_ref[...] += dgk

    def row(w):
        return pl.BlockSpec((tr, w), lambda i: (i, 0))

    vec = pl.BlockSpec((1, HEAD_DIM), lambda i: (0, 0))
    return pl.pallas_call(
        body, name="prep_odd_bwd", grid=(S // tr,),
        in_specs=[row(ODD_IN), row(128), row(128), vec, vec, row(1024), row(256), row(256)],
        out_specs=[row(ODD_IN), vec, vec],
        out_shape=[jax.ShapeDtypeStruct((S, ODD_IN), BF), jax.ShapeDtypeStruct((1, HEAD_DIM), F32),
                   jax.ShapeDtypeStruct((1, HEAD_DIM), F32)],
        compiler_params=_params(("arbitrary",)),
    )(proj, cos, sin, gq, gk, dq, dk, dv)


def _loop_unrolled(n, factor, step, init):
    while n % factor:
        factor //= 2

    def trip(t, carry):
        for u in range(factor):
            carry = step(factor * t + u, carry)
        return carry

    return lax.fori_loop(0, n // factor, trip, init)


def _flash_fwd(q, k, v, carried=None):
    S = q.shape[0]
    tq = _pick(S, (512,))
    tk = _pick(S, (1024, 512))
    nk = S // tk
    G = AX_HEADS // AX_KV_HEADS

    def body(q_ref, k_ref, v_ref, o_ref, lse_ref):
        qv = q_ref[...]

        def step(j, carry):
            m, l, acc = carry
            rows = pl.ds(pl.multiple_of(j * tk, tk), tk)
            s = _dot(qv, k_ref[rows, :], NT)
            m_new = jnp.maximum(m, jnp.max(s, axis=-1, keepdims=True))
            alpha = jnp.exp2(m - m_new)
            p = jnp.exp2(s - m_new)
            l = alpha * l + jnp.sum(p, axis=-1, keepdims=True)
            acc = alpha * acc + _dot(p.astype(v_ref.dtype), v_ref[rows, :])
            return m_new, l, acc

        init = (jnp.full((tq, 1), NEG_INF, F32), jnp.zeros((tq, 1), F32), jnp.zeros((tq, HEAD_DIM), F32))
        m, l, acc = _loop_unrolled(nk, 8, step, init)
        o_ref[...] = (acc / l).astype(o_ref.dtype)
        lse_ref[0] = jnp.broadcast_to(m + jnp.log2(l), (tq, 128))

    (o, lse), landed = _carry_call(
        body, (q, k, v), name="flash_fwd", grid=(AX_HEADS, S // tq),
        in_specs=[pl.BlockSpec((tq, HEAD_DIM), lambda h, i: (i, h)),
                  pl.BlockSpec((S, HEAD_DIM), lambda h, i: (0, h // G)),
                  pl.BlockSpec((S, HEAD_DIM), lambda h, i: (0, h // G))],
        out_specs=[pl.BlockSpec((tq, HEAD_DIM), lambda h, i: (i, h)),
                   pl.BlockSpec((1, tq, 128), lambda h, i: (h, i, 0))],
        out_shape=[jax.ShapeDtypeStruct((S, AX_HEADS * HEAD_DIM), BF), jax.ShapeDtypeStruct((AX_HEADS, S, 128), F32)],
        semantics=("parallel", "parallel"), carried=carried)
    return o, lse, landed


def _flash_bwd(q, k, v, o, do, lse, carried=None):
    S = q.shape[0]
    tq = _pick(S, (512,))
    tk = _pick(S, (1024, 512))
    nq, nk = S // tq, S // tk
    G = AX_HEADS // AX_KV_HEADS

    def body(q_ref, k_ref, v_ref, o_ref, do_ref, lse_ref, dq_ref, dk_ref, dv_ref):
        g, i = pl.program_id(1), pl.program_id(2)

        @pl.when((g == 0) & (i == 0))
        def _():
            dk_ref[...] = jnp.zeros_like(dk_ref)
            dv_ref[...] = jnp.zeros_like(dv_ref)

        qv = q_ref[...]
        do_f = do_ref[...].astype(F32)
        dob = do_f.astype(qv.dtype)
        dob_ln2 = (do_f * LN2).astype(qv.dtype)
        delta = jnp.sum(do_f * o_ref[...].astype(F32), axis=-1, keepdims=True) * LN2
        lse_col = lse_ref[0][:, 0:1]

        def step(j, dq):
            rows = pl.ds(pl.multiple_of(j * tk, tk), tk)
            kj, vj = k_ref[rows, :], v_ref[rows, :]
            p = jnp.exp2(_dot(qv, kj, NT) - lse_col)
            dp = _dot(dob_ln2, vj, NT)
            ds = (p * (dp - delta)).astype(qv.dtype)
            dk_ref[rows, :] += _dot(ds, qv, TN)
            dv_ref[rows, :] += _dot(p.astype(dob.dtype), dob, TN)
            return dq + _dot(ds, kj)

        dq_ref[...] = _loop_unrolled(nk, 4, step, jnp.zeros((tq, HEAD_DIM), F32)).astype(dq_ref.dtype)

    q_spec = pl.BlockSpec((tq, HEAD_DIM), lambda kh, g, i: (i, kh * G + g))
    kv_spec = pl.BlockSpec((S, HEAD_DIM), lambda kh, g, i: (0, kh))
    (dq, dk, dv), landed = _carry_call(
        body, (q, k, v, o, do, lse), name="flash_bwd", grid=(AX_KV_HEADS, G, nq),
        in_specs=[q_spec, kv_spec, kv_spec, q_spec, q_spec,
                  pl.BlockSpec((1, tq, 128), lambda kh, g, i: (kh * G + g, i, 0))],
        out_specs=[q_spec, kv_spec, kv_spec],
        out_shape=[jax.ShapeDtypeStruct((S, AX_HEADS * HEAD_DIM), BF), jax.ShapeDtypeStruct((S, 256), F32),
                   jax.ShapeDtypeStruct((S, 256), F32)],
        vmem=VMEM_LARGE, semantics=("arbitrary", "arbitrary", "arbitrary"), carried=carried)
    return dq, dk, dv, landed


def _rope_angles(pos, dim, theta):
    inv = theta ** (-jnp.arange(0, dim, 2, dtype=jnp.float32) / dim)
    return pos.astype(jnp.float32)[:, None] * inv[None, :]


def _rope_tables(S):
    ang = _rope_angles(jnp.arange(S), RET_DK, RET_THETA)
    c, s = jnp.cos(ang), jnp.sin(ang)
    ret = (jnp.concatenate([c, c], -1), jnp.concatenate([-s, s], -1))
    rows = S // GRID_W
    ar, ac = _rope_angles(jnp.arange(rows), HEAD_DIM // 2, AX_THETA), _rope_angles(jnp.arange(GRID_W), HEAD_DIM // 2, AX_THETA)
    cr, sr = jnp.repeat(jnp.cos(ar), GRID_W, axis=0), jnp.repeat(jnp.sin(ar), GRID_W, axis=0)
    cc, sc = jnp.tile(jnp.cos(ac), (rows, 1)), jnp.tile(jnp.sin(ac), (rows, 1))
    ax = (jnp.concatenate([cr, cr, cc, cc], -1), jnp.concatenate([-sr, sr, -sc, sc], -1))
    return ret, ax


def _pad_tile(a):
    return jnp.pad(a.astype(F32), ((0, 8 - a.shape[0]), (0, 128 - a.shape[1])))


def _relu2_epilogue(acc):
    r = jnp.maximum(acc, 0.0)
    return acc, r * r


def _relu2_bwd_epilogue(acc, u):
    return (acc * (2.0 * jnp.maximum(u.astype(F32), 0.0)),)


def _add_epilogue(acc, res):
    return (acc + res,)


def _add_norm_epilogue(acc, res, g):
    y = acc + res
    r = lax.rsqrt(jnp.mean(y * y, axis=-1, keepdims=True) + EPS)
    return y, y * r * g


def _residual_mm(a, w, res, g_next, name):
    return _mm(a, w, name=name, extras=(res,), rows=(g_next,), out_dtypes=(F32, BF), epilogue=_add_norm_epilogue,
               vmem=VMEM_LARGE)


def _rms_bwd_epilogue(dh, x, dres, g):
    r = lax.rsqrt(jnp.mean(x * x, axis=-1, keepdims=True) + EPS)
    xh = x * r
    dxh = dh * g
    dx = r * (dxh - xh * jnp.mean(dxh * xh, axis=-1, keepdims=True)) + dres
    return dx, dx, _row_sum(dh * xh)


def _norm_bwd_mm(mm, dproj, w, x, dres, g, name):
    return mm(dproj, w, tb=True, name=name, out_dtypes=(F32, BF), extras=(x, dres), rows=(g,),
              epilogue=_rms_bwd_epilogue, n_row_sums=1, tm=512)


def _mlp_fwd(x, h, w_up, w_down, tag, g_next=None):
    u, a = _mm(h, w_up, name=f"mlp_up_{tag}", out_dtypes=(BF, BF), epilogue=_relu2_epilogue)
    if g_next is None:
        y, h_next = _mm(a, w_down, name=f"mlp_down_{tag}", extras=(x,), epilogue=_add_epilogue), None
    else:
        y, h_next = _residual_mm(a, w_down, x, g_next, f"mlp_down_{tag}")
    return y, h_next, (h, u, a)


def _mlp_bwd(x, g, w_up, w_down, saved, dy, dyb, tag, mm_exchange=None):
    h, u, a = saved
    du = _mm(dyb, w_down, tb=True, name=f"mlp_down_dx_{tag}", out_dtypes=(BF,), extras=(u,), epilogue=_relu2_bwd_epilogue)
    dw_down = _mm(a, dyb, ta=True, name=f"mlp_down_dw_{tag}", out_dtypes=(BF,))
    dw_up = _mm(h, du, ta=True, name=f"mlp_up_dw_{tag}", out_dtypes=(BF,))
    if mm_exchange is None:
        mm = _mm
    else:
        mm = lambda *args, **kw: mm_exchange([("down" + tag, dw_down)], *args, **kw)
    dx, dxb, dg = _norm_bwd_mm(mm, du, w_up, x, dy, g, f"mlp_up_dx_{tag}")
    return dx, dxb, dg, dw_up, dw_down


COL_SHARDED = ("in_even", "in_odd", "up0", "up1")


def _assemble(key, g):
    if key in COL_SHARDED:
        return g.transpose(1, 0, 2).reshape(g.shape[1], N_DEV * g.shape[2])
    return g.reshape(N_DEV * g.shape[1], g.shape[2])


def _split(key, full):
    rows, cols = full.shape
    if key in COL_SHARDED:
        return full.reshape(rows, N_DEV, cols // N_DEV).transpose(1, 0, 2)
    return full.reshape(N_DEV, rows // N_DEV, cols)


def _local_step(x, target, W, P, late=None):
    S = x.shape[0]
    W = dict(W)
    landed = {}

    def gather_while(stage):
        return None if late is None else _Exchange([s for _, s in late[stage]], gather=True)

    def arrived(stage, outs):
        for (key, _), g in zip([] if late is None else late[stage], outs):
            W[key] = _assemble(key, g)

    def exchange_while(grads):
        return None if late is None else _Exchange([_split(k, g) for k, g in grads], gather=False)

    def left(grads, outs):
        for (key, _), l in zip(grads, outs):
            landed[key] = l

    def mm_gather(stage, *args, **kw):
        if late is None:
            return _mm(*args, **kw)
        out, outs = _mm(*args, carried=gather_while(stage), **kw)
        arrived(stage, outs)
        return out

    def mm_exchange(grads, *args, **kw):
        if late is None:
            return _mm(*args, **kw)
        out, outs = _mm(*args, carried=exchange_while(grads), **kw)
        left(grads, outs)
        return out

    (cos_r, sin_r), (cos_a, sin_a) = _rope_tables(S)
    bucket = _t5_bucket_map().T
    logit = P["ret_decay_logit"]
    lg = _log_sigmoid_tile(_pad_tile(logit))
    bias = _t5_bias(P["t5_table"], bucket)
    nmix, nmlp = P["norm_mix"], P["norm_mlp"]

    h0 = _rms_fwd(x, nmix[0:1], "mix_norm_0")
    proj_e = mm_gather("in_even", h0, W["in_even"], name="in_even")
    (qr, kr, vr, qs, ks, vs), outs = _prep_even_fwd(proj_e, cos_r, sin_r, P["swa_q_norm"], P["swa_k_norm"],
                                                    gather_while("prep_even_fwd"))
    arrived("prep_even_fwd", outs)
    sf, sb = _ret_scan(kr, vr, lg, (0, 0), (1, 0), "retention_states")
    y_ret, outs = _ret_fwd(qr, kr, vr, lg, sf, sb, gather_while("retention_fwd"))
    arrived("retention_fwd", outs)
    oa = _ret_post_fwd(y_ret, proj_e, P["ret_norm"])
    ob, outs = _swa_fwd(qs, ks, vs, bias, P["swa_sink"], gather_while("swa_fwd"))
    arrived("swa_fwd", outs)
    wo_a, wo_b = W["out_even"][:1024], W["out_even"][1024:]
    x1 = _mm(oa, wo_a, name="out_even_a", extras=(x,), epilogue=_add_epilogue)
    x1, h1 = _residual_mm(ob, wo_b, x1, nmlp[0:1], "out_even_b")
    x2, h2, mlp0 = _mlp_fwd(x1, h1, W["up0"], W["down0"], "0", g_next=nmix[1:2])

    proj_o = _mm(h2, W["in_odd"], name="in_odd")
    qx, kx, vx = _prep_odd_fwd(proj_o, cos_a, sin_a, P["ax_q_norm"], P["ax_k_norm"])
    ox, lse, outs = _flash_fwd(qx, kx, vx, gather_while("flash_fwd"))
    arrived("flash_fwd", outs)
    x3, h3 = _residual_mm(ox, W["out_odd"], x2, nmlp[1:2], "out_odd")
    x4, _, mlp1 = _mlp_fwd(x3, h3, W["up1"], W["down1"], "1")

    d4, d4b, loss_tile = _loss_and_grad(x4, target)

    d3, d3b, dnmlp1, dw_up1, dw_down1 = _mlp_bwd(x3, nmlp[1:2], W["up1"], W["down1"], mlp1, d4, d4b, "1")
    dox = _mm(d3b, W["out_odd"], tb=True, name="out_odd_dx")
    dw_out_odd = _mm(ox, d3b, ta=True, name="out_odd_dw", out_dtypes=(BF,))
    grads1 = [("up1", dw_up1), ("down1", dw_down1), ("out_odd", dw_out_odd)]
    dqx, dkx, dvx, outs = _flash_bwd(qx, kx, vx, ox, dox, lse, exchange_while(grads1))
    left(grads1, outs)
    dproj_o, dgq_ax, dgk_ax = _prep_odd_bwd(proj_o, cos_a, sin_a, P["ax_q_norm"], P["ax_k_norm"], dqx, dkx, dvx)
    dw_in_odd = _mm(h2, dproj_o, ta=True, name="in_odd_dw", out_dtypes=(BF,))
    d2, d2b, dnmix1 = _norm_bwd_mm(_mm, dproj_o, W["in_odd"], x2, d3, nmix[1:2], "in_odd_dx")

    d1, d1b, dnmlp0, dw_up0, dw_down0 = _mlp_bwd(x1, nmlp[0:1], W["up0"], W["down0"], mlp0, d2, d2b, "0", mm_exchange)
    doa = _mm(d1b, wo_a, tb=True, name="out_even_a_dx")
    dob = _mm(d1b, wo_b, tb=True, name="out_even_b_dx")
    dw_out_even = jnp.concatenate([_mm(oa, d1b, ta=True, name="out_even_a_dw", out_dtypes=(BF,)),
                                   _mm(ob, d1b, ta=True, name="out_even_b_dw", out_dtypes=(BF,))], axis=0)
    dy_ret, dga, dret_norm = _ret_post_bwd(y_ret, proj_e, P["ret_norm"], doa)
    hb, hf = _ret_scan(qr, dy_ret, lg, (1, 1), (0, 1), "retention_state_grads")
    dqr, dkr, dvr, dlogit = _ret_bwd(qr, kr, vr, dy_ret, lg, logit, sf, sb, hf, hb)
    grads0 = [("in_odd", dw_in_odd), ("out_even", dw_out_even)]
    dqs, dks, dvs, dbias, dsink, outs = _swa_bwd(qs, ks, vs, dob, bias, P["swa_sink"], exchange_while(grads0))
    left(grads0, outs)
    dt5 = _t5_table_grad(dbias, bucket)
    dproj_e, dgq_swa, dgk_swa = _prep_even_bwd(proj_e, cos_r, sin_r, P["swa_q_norm"], P["swa_k_norm"],
                                               dqr, dkr, dvr, dga, dqs, dks, dvs)
    dw_in_even = mm_exchange([("up0", dw_up0)], h0, dproj_e, ta=True, name="in_even_dw", out_dtypes=(BF,))
    dx, _, dnmix0 = _norm_bwd_mm(lambda *args, **kw: mm_exchange([("in_even", dw_in_even)], *args, **kw),
                                 dproj_e, W["in_even"], x, d1, nmix[0:1], "in_even_dx")

    if late is None:
        dW = dict(grads1 + grads0, up0=dw_up0, down0=dw_down0, in_even=dw_in_even)
    else:
        dW = landed
    dP = {"norm_mix": jnp.concatenate([dnmix0, dnmix1], 0), "norm_mlp": jnp.concatenate([dnmlp0, dnmlp1], 0),
          "ret_decay_logit": dlogit[:, :, 0, 0].T, "ret_norm": dret_norm,
          "swa_q_norm": dgq_swa, "swa_k_norm": dgk_swa, "swa_sink": dsink[:, 0][None, :],
          "t5_table": dt5[:, :SWA_HEADS], "ax_q_norm": dgq_ax, "ax_k_norm": dgk_ax}
    return loss_tile, dx, dW, dP


def _cast_shards(shards):
    n = len(shards)

    def body(*refs):
        for i_ref, o_ref in zip(refs[:n], refs[n:]):
            o_ref[...] = i_ref[...].astype(o_ref.dtype)

    return pl.pallas_call(body, name="cast_shards", in_specs=[VMEM_SPEC] * n, out_specs=[VMEM_SPEC] * n,
                          out_shape=[jax.ShapeDtypeStruct(s.shape, BF) for s in shards],
                          compiler_params=pltpu.CompilerParams(vmem_limit_bytes=VMEM_SMALL))(*shards)


def _all_gather(shards):
    n = len(shards)

    def body(*refs):
        ins, outs, stage = refs[:n], refs[n:2 * n], refs[2 * n:3 * n]
        send_sems, recv_sems, local_sems = refs[3 * n:]
        me = _my_place()
        sibling = _flip(me, 1)
        chips = [_flip(me, 4), _flip(me, 2), _flip(me, 6)]

        def copy(a, k, block, to, src=None):
            dst = outs[a].at[_index(block)]
            return pltpu.make_async_remote_copy(
                src_ref=dst if src is None else src, dst_ref=dst,
                send_sem=send_sems.at[a, k], recv_sem=recv_sems.at[a, k], device_id=to, device_id_type=MESH)

        first, mine = [], []
        for a in range(n):
            stage[a][...] = ins[a][...].astype(stage[a].dtype)
            mine.append(pltpu.make_async_copy(stage[a], outs[a].at[_index(me)], local_sems.at[a]))
            mine[-1].start()
            first.append(copy(a, 0, me, sibling, src=stage[a]))
            first += [copy(a, 1 + j, me, chip, src=stage[a]) for j, chip in enumerate(chips)]
        for cp in first:
            cp.start()
        passed = []
        for a in range(n):
            for j, chip in enumerate(chips):
                copy(a, 1 + j, chip, me).wait_recv()
                passed.append(copy(a, 4 + j, chip, sibling))
                passed[-1].start()
        for a in range(n):
            copy(a, 0, sibling, me).wait_recv()
            for j, chip in enumerate(chips):
                copy(a, 4 + j, _flip(chip, 1), me).wait_recv()
        for cp in first + passed:
            cp.wait_send()
        for cp in mine:
            cp.wait()

    return pl.pallas_call(
        body, name="weights_all_gather",
        in_specs=[VMEM_SPEC] * n, out_specs=[ANY_SPEC] * n,
        out_shape=[jax.ShapeDtypeStruct((N_DEV,) + s.shape, BF) for s in shards],
        scratch_shapes=[pltpu.VMEM(s.shape, BF) for s in shards]
        + [pltpu.SemaphoreType.DMA((n, 7)), pltpu.SemaphoreType.DMA((n, 7)), pltpu.SemaphoreType.DMA((n,))],
        compiler_params=pltpu.CompilerParams(vmem_limit_bytes=VMEM_SMALL),
    )(*shards)


def _all_reduce_small(part):
    R, C = part.shape

    def body(x_ref, o_ref, land_ref, send_sems, recv_sems):
        me = _my_place()
        land_ref[_index(me)] = x_ref[...]
        copies = []
        for k in range(1, N_DEV):
            peer = _flip(me, k)
            copies.append(pltpu.make_async_remote_copy(
                src_ref=x_ref, dst_ref=land_ref.at[_index(me)],
                send_sem=send_sems.at[k - 1], recv_sem=recv_sems.at[k - 1], device_id=peer, device_id_type=MESH))
            copies[-1].start()
        for k in range(1, N_DEV):
            peer = _flip(me, k)
            pltpu.make_async_remote_copy(
                src_ref=x_ref, dst_ref=land_ref.at[_index(peer)],
                send_sem=send_sems.at[k - 1], recv_sem=recv_sems.at[k - 1], device_id=peer, device_id_type=MESH).wait_recv()
        for cp in copies:
            cp.wait_send()
        acc = land_ref[0]
        for s in range(1, N_DEV):
            acc = acc + land_ref[s]
        o_ref[...] = acc

    return pl.pallas_call(
        body, name="small_all_reduce", in_specs=[VMEM_SPEC], out_specs=VMEM_SPEC,
        out_shape=jax.ShapeDtypeStruct((R, C), F32),
        scratch_shapes=[pltpu.VMEM((N_DEV, R, C), F32), pltpu.SemaphoreType.DMA((7,)), pltpu.SemaphoreType.DMA((7,))],
    )(part)


def _adamw_math(w, g, m, v):
    m = ADAM_B1 * m + (1.0 - ADAM_B1) * g
    v = ADAM_B2 * v + (1.0 - ADAM_B2) * jnp.square(g)
    m_hat = m / (1.0 - ADAM_B1 ** ADAM_STEP)
    v_hat = v / (1.0 - ADAM_B2 ** ADAM_STEP)
    delta = -ADAM_LR * (m_hat / (jnp.sqrt(v_hat) + ADAM_EPS) + ADAM_WD * w)
    return delta, m, v


def _sum_and_adamw(landed, w, m, v, name):
    R, C = w.shape
    tr = _pick(R, (256, 128))

    def body(l_ref, w_ref, m_ref, v_ref, g_ref, d_ref, nm_ref, nv_ref):
        g = l_ref[0].astype(F32)
        for s in range(1, N_DEV):
            g = g + l_ref[s].astype(F32)
        g_ref[...] = g
        d_ref[...], nm_ref[...], nv_ref[...] = _adamw_math(w_ref[...], g, m_ref[...], v_ref[...])

    row = pl.BlockSpec((tr, C), lambda i: (i, 0))
    return pl.pallas_call(
        body, name=name, grid=(R // tr,),
        in_specs=[pl.BlockSpec((N_DEV, tr, C), lambda i: (0, i, 0)), row, row, row], out_specs=[row] * 4,
        out_shape=[jax.ShapeDtypeStruct((R, C), F32)] * 4, compiler_params=_params(("parallel",)),
    )(landed, w, m, v)


def _adamw_small(w, g, m, v):
    def body(w_ref, g_ref, m_ref, v_ref, d_ref, nm_ref, nv_ref):
        d_ref[...], nm_ref[...], nv_ref[...] = _adamw_math(w_ref[...], g_ref[...], m_ref[...], v_ref[...])

    full = pl.BlockSpec(w.shape, lambda: (0, 0))
    return pl.pallas_call(body, name="adamw_small", in_specs=[full] * 4, out_specs=[full] * 3,
                          out_shape=[jax.ShapeDtypeStruct(w.shape, F32)] * 3)(w, g, m, v)


MATRICES = ("w_in_even", "w_out_even", "w_in_odd", "w_out_odd", "w_mlp_up", "w_mlp_down")
SMALL = ("norm_mix", "norm_mlp", "ret_decay_logit", "ret_norm", "swa_q_norm", "swa_k_norm", "swa_sink",
         "t5_table", "ax_q_norm", "ax_k_norm")
MATRIX_OF = {"in_even": ("w_in_even", 0), "out_even": ("w_out_even", 0), "in_odd": ("w_in_odd", 0),
             "out_odd": ("w_out_odd", 0), "up0": ("w_mlp_up", 0), "up1": ("w_mlp_up", 1),
             "down0": ("w_mlp_down", 0), "down1": ("w_mlp_down", 1)}
GATHER_FIRST = ("in_even",)
GATHER_WHILE = {"in_even": ("up0",), "prep_even_fwd": ("out_even",), "retention_fwd": ("in_odd",), "swa_fwd": ("down0",),
                "flash_fwd": ("out_odd", "up1", "down1")}


SMALL_ROWS = 8
SMALL_AT = {"norm_mix": (0, 0), "norm_mlp": (2, 0), "ret_norm": (4, 0), "swa_q_norm": (5, 0), "swa_k_norm": (5, 128),
            "ax_q_norm": (5, 256), "ax_k_norm": (5, 384), "swa_sink": (5, 512), "ret_decay_logit": (5, 640),
            "t5_table": (6, 0)}
LOSS_AT = (5, 768)


def _pack_small(arrays, loss=None):
    buf = jnp.zeros((SMALL_ROWS, 1024), F32)
    for name, (r, c) in SMALL_AT.items():
        a = arrays[name].astype(F32)
        a = a.reshape(1, -1) if name in ("ret_decay_logit", "t5_table") else a.reshape(-1, a.shape[-1])
        buf = lax.dynamic_update_slice(buf, a, (r, c))
    if loss is not None:
        buf = lax.dynamic_update_slice(buf, loss.reshape(1, 1), LOSS_AT)
    return buf


def _unpack_small(buf, like):
    out = {}
    for name, (r, c) in SMALL_AT.items():
        shape = like[name].shape
        rows = 1 if name in ("ret_decay_logit", "t5_table") else math.prod(shape[:-1])
        cols = math.prod(shape) // rows
        out[name] = buf[r:r + rows, c:c + cols].reshape(shape)
    return out


def kernel(x, norm_mix, norm_mlp, w_in_even, w_out_even, ret_decay_logit, ret_norm, swa_q_norm, swa_k_norm, swa_sink, t5_table, w_in_odd, w_out_odd, ax_q_norm, ax_k_norm, w_mlp_up, w_mlp_down, loss_target, m_norm_mix, m_norm_mlp, m_w_in_even, m_w_out_even, m_ret_decay_logit, m_ret_norm, m_swa_q_norm, m_swa_k_norm, m_swa_sink, m_t5_table, m_w_in_odd, m_w_out_odd, m_ax_q_norm, m_ax_k_norm, m_w_mlp_up, m_w_mlp_down, v_norm_mix, v_norm_mlp, v_w_in_even, v_w_out_even, v_ret_decay_logit, v_ret_norm, v_swa_q_norm, v_swa_k_norm, v_swa_sink, v_t5_table, v_w_in_odd, v_w_out_odd, v_ax_q_norm, v_ax_k_norm, v_w_mlp_up, v_w_mlp_down):
    given = dict(locals())
    weights = {n: given[n] for n in MATRICES + SMALL}
    moments_m = {n: given["m_" + n] for n in MATRICES + SMALL}
    moments_v = {n: given["v_" + n] for n in MATRICES + SMALL}

    def shard(table, key):
        arg, layer = MATRIX_OF[key]
        return table[arg][layer]

    gathered = _all_gather([shard(weights, k) for k in GATHER_FIRST])
    W = {k: _assemble(k, g) for k, g in zip(GATHER_FIRST, gathered)}
    late_keys = [k for keys in GATHER_WHILE.values() for k in keys]
    cast = dict(zip(late_keys, _cast_shards([shard(weights, k) for k in late_keys])))
    late = {stage: [(k, cast[k]) for k in keys] for stage, keys in GATHER_WHILE.items()}
    P = {"norm_mix": norm_mix, "norm_mlp": norm_mlp, "ret_decay_logit": ret_decay_logit[0], "ret_norm": ret_norm,
         "swa_q_norm": swa_q_norm, "swa_k_norm": swa_k_norm, "swa_sink": swa_sink, "t5_table": t5_table,
         "ax_q_norm": ax_q_norm, "ax_k_norm": ax_k_norm}

    loss_tile, dx, landed, dP = _local_step(x[0], loss_target[0], W, P, late)

    per_key = {k: _sum_and_adamw(landed[k], shard(weights, k), shard(moments_m, k), shard(moments_v, k), "adamw_" + k)
               for k in MATRIX_OF}
    grads, deltas, new_m, new_v = {}, {}, {}, {}
    for i, out in enumerate((grads, deltas, new_m, new_v)):
        for n in MATRICES:
            out[n] = jnp.stack([per_key[k][i] for k, (arg, _) in MATRIX_OF.items() if arg == n])

    dP["ret_decay_logit"] = dP["ret_decay_logit"][None]
    total = _all_reduce_small(_pack_small(dP, loss_tile[0, 0]))
    loss = total[LOSS_AT[0], LOSS_AT[1]]
    small_d, small_m, small_v = _adamw_small(_pack_small(weights), total, _pack_small(moments_m), _pack_small(moments_v))
    like = {n: weights[n] for n in SMALL}
    for out, buf in ((grads, total), (deltas, small_d), (new_m, small_m), (new_v, small_v)):
        out.update(_unpack_small(buf, like))

    order = ("norm_mix", "norm_mlp", "w_in_even", "w_out_even", "ret_decay_logit", "ret_norm", "swa_q_norm", "swa_k_norm",
             "swa_sink", "t5_table", "w_in_odd", "w_out_odd", "ax_q_norm", "ax_k_norm", "w_mlp_up", "w_mlp_down")
    return (loss, dx[None], *[grads[n] for n in order], *[deltas[n] for n in order],
            *[new_m[n] for n in order], *[new_v[n] for n in order])
```

```python
import math

import jax
import jax.numpy as jnp
from jax import lax
from jax.experimental import pallas as pl
from jax.experimental.pallas import tpu as pltpu

F32 = jnp.float32
BF = jnp.bfloat16

D_MODEL = 1024
HEAD_DIM = 128
EPS = 1e-6
NEG_INF = -1e30
RET_HEADS, RET_DK, RET_DV = 4, 128, 256
RET_THETA = 10000.0
SWA_HEADS, SWA_KV_HEADS, WINDOW, BLOCK = 8, 2, 128, 128
T5_BUCKETS, T5_MAX_DIST = 32, 128
AX_HEADS, AX_KV_HEADS, AX_THETA, GRID_W = 8, 2, 10000.0, 64
D_FF = 4096
ATT_SCALE = HEAD_DIM ** -0.5
LN2 = math.log(2.0)
AX_SCALE = ATT_SCALE / LN2
RET_SCALE = RET_DK ** -0.5
N_DEV = 8

ADAM_LR, ADAM_B1, ADAM_B2, ADAM_EPS, ADAM_WD, ADAM_STEP = 0.001, 0.9, 0.999, 1e-08, 0.01, 10

MIB = 1024 * 1024
VMEM_SMALL = 40 * MIB
VMEM_LARGE = 56 * MIB

OFF_QA, OFF_KA, OFF_VA, OFF_GA, OFF_QB, OFF_KB, OFF_VB = 0, 512, 1024, 2048, 3072, 4096, 4352
EVEN_IN = 4608
ODD_IN = 1536

NT = (((1,), (1,)), ((), ()))
TN = (((0,), (0,)), ((), ()))
NN = (((1,), (0,)), ((), ()))


def _dot(a, b, dims=NN):
    return lax.dot_general(a, b, dims, preferred_element_type=F32)


def _params(sem=None, vmem=VMEM_SMALL):
    return pltpu.CompilerParams(dimension_semantics=sem, vmem_limit_bytes=vmem)


def _pick(n, prefs):
    for p in prefs:
        if n % p == 0:
            return p
    return n


def _row_sum(x):
    return jnp.sum(x, axis=0, keepdims=True)


def _all_sum(x):
    return jnp.sum(jnp.sum(x, axis=0, keepdims=True), axis=1, keepdims=True)


def _sigmoid(x):
    return 1.0 / (1.0 + jnp.exp(-x))


SMEM_SPEC = pl.BlockSpec(memory_space=pltpu.SMEM)
ANY_SPEC = pl.BlockSpec(memory_space=pl.ANY)
VMEM_SPEC = pl.BlockSpec(memory_space=pltpu.VMEM)
MESH = pl.DeviceIdType.MESH


def _my_place():
    return lax.axis_index("x"), lax.axis_index("y"), lax.axis_index("c")


def _flip(place, k):
    x, y, c = place
    return (1 - x if k & 4 else x, 1 - y if k & 2 else y, 1 - c if k & 1 else c)


def _index(place):
    x, y, c = place
    return 4 * x + 2 * y + c


class _Exchange:
    def __init__(self, sources, gather):
        self.sources, self.gather, self.n = list(sources), gather, len(sources)
        self.out_shape = [jax.ShapeDtypeStruct(((N_DEV,) + s.shape) if gather else s.shape, s.dtype) for s in self.sources]
        self.scratch = [pltpu.SemaphoreType.DMA((self.n, 7)), pltpu.SemaphoreType.DMA((self.n, 7)),
                        pltpu.SemaphoreType.DMA((self.n,))]

    def _source(self, ins, a, place):
        return ins[a] if self.gather else ins[a].at[_index(place)]

    def _local(self, ins, outs, sems):
        me = _my_place()
        return [pltpu.make_async_copy(self._source(ins, a, me), outs[a].at[_index(me)], sems[2].at[a]) for a in range(self.n)]

    def _remote(self, ins, outs, sems, arriving):
        send_sems, recv_sems, _ = sems
        me = _my_place()
        copies = []
        for a in range(self.n):
            for k in range(1, N_DEV):
                peer = _flip(me, k)
                copies.append(pltpu.make_async_remote_copy(
                    src_ref=self._source(ins, a, peer), dst_ref=outs[a].at[_index(peer if arriving else me)],
                    send_sem=send_sems.at[a, k - 1], recv_sem=recv_sems.at[a, k - 1], device_id=peer, device_id_type=MESH))
        return copies

    def start(self, ins, outs, sems):
        for cp in self._local(ins, outs, sems) + self._remote(ins, outs, sems, arriving=False):
            cp.start()

    def wait(self, ins, outs, sems):
        for cp in self._remote(ins, outs, sems, arriving=True):
            cp.wait_recv()
        for cp in self._remote(ins, outs, sems, arriving=False):
            cp.wait_send()
        for cp in self._local(ins, outs, sems):
            cp.wait()


def _carry_call(body, args, *, name, grid, in_specs, out_specs, out_shape, scratch_shapes=(), vmem=VMEM_SMALL,
                semantics=None, carried=None):
    if carried is None:
        outs = pl.pallas_call(body, name=name, grid=grid, in_specs=in_specs, out_specs=out_specs, out_shape=out_shape,
                              scratch_shapes=list(scratch_shapes), compiler_params=_params(semantics, vmem))(*args)
        return list(outs), []
    ni, no, ns, nc = len(in_specs), len(out_specs), len(scratch_shapes), carried.n

    def full_body(*refs):
        ins, cin = refs[:ni], refs[ni:ni + nc]
        outs, cout = refs[ni + nc:ni + nc + no], refs[ni + nc + no:ni + 2 * nc + no]
        scratch, sems = refs[ni + 2 * nc + no:ni + 2 * nc + no + ns], refs[ni + 2 * nc + no + ns:]
        ids = [pl.program_id(d) for d in range(len(grid))]
        first, last = ids[0] == 0, ids[0] == grid[0] - 1
        for d in range(1, len(grid)):
            first, last = first & (ids[d] == 0), last & (ids[d] == grid[d] - 1)

        @pl.when(first)
        def _():
            carried.start(cin, cout, sems)

        body(*ins, *outs, *scratch)

        @pl.when(last)
        def _():
            carried.wait(cin, cout, sems)

    outs = pl.pallas_call(
        full_body, name=name, grid=grid, in_specs=list(in_specs) + [ANY_SPEC] * nc,
        out_specs=list(out_specs) + [ANY_SPEC] * nc, out_shape=list(out_shape) + carried.out_shape,
        scratch_shapes=list(scratch_shapes) + carried.scratch,
        compiler_params=_params(("arbitrary",) * len(grid), vmem))(*args, *carried.sources)
    return list(outs[:no]), list(outs[no:])


def _mm(a, b, *, name, ta=False, tb=False, out_dtypes=(F32,), extras=(), rows=(), n_row_sums=0, epilogue=None,
        tm=1024, tn=1024, tk=1024, vmem=VMEM_SMALL, carried=None):
    M, K = (a.shape[1], a.shape[0]) if ta else a.shape
    N = b.shape[0] if tb else b.shape[1]
    assert K == (b.shape[1] if tb else b.shape[0])
    tm = _pick(M, (tm, 512, 256, 128))
    tn = _pick(N, (tn, 1536, 512, 384, 256, 128))
    tk = _pick(K, (tk, 1536, 512, 256, 128))
    nk = K // tk
    assert n_row_sums == 0 or tn == N
    ne, nr, no = len(extras), len(rows), len(out_dtypes)
    dims = (((0 if ta else 1,), (1 if tb else 0,)), ((), ()))
    if epilogue is None:
        epilogue = lambda acc: (acc,)

    def body(a_ref, b_ref, *rest):
        extra_refs, out_refs = rest[:ne + nr], rest[ne + nr:ne + nr + no]
        sum_refs = rest[ne + nr + no:ne + nr + no + n_row_sums]
        first_tile = pl.program_id(0) == 0

        def finish(acc):
            outs = epilogue(acc, *[r[...] for r in extra_refs])
            for o_ref, o in zip(out_refs, outs[:no]):
                o_ref[...] = o.astype(o_ref.dtype)
            for s_ref, contribution in zip(sum_refs, outs[no:]):
                @pl.when(first_tile)
                def _(s_ref=s_ref, contribution=contribution):
                    s_ref[...] = contribution

                @pl.when(jnp.logical_not(first_tile))
                def _(s_ref=s_ref, contribution=contribution):
                    s_ref[...] += contribution

        part = _dot(a_ref[...], b_ref[...], dims)
        if nk == 1:
            finish(part)
        else:
            acc_ref = rest[-1]
            k = pl.program_id(2)

            @pl.when(k == 0)
            def _():
                acc_ref[...] = part

            @pl.when(k > 0)
            def _():
                acc_ref[...] += part

            @pl.when(k == nk - 1)
            def _():
                finish(acc_ref[...])

    a_spec = pl.BlockSpec((tk, tm), lambda i, j, k: (k, i)) if ta else pl.BlockSpec((tm, tk), lambda i, j, k: (i, k))
    b_spec = pl.BlockSpec((tn, tk), lambda i, j, k: (j, k)) if tb else pl.BlockSpec((tk, tn), lambda i, j, k: (k, j))
    o_spec = pl.BlockSpec((tm, tn), lambda i, j, k: (i, j))
    row_spec = pl.BlockSpec((1, tn), lambda i, j, k: (0, j))
    outs, landed = _carry_call(
        body, (a, b, *extras, *rows), name=name, grid=(M // tm, N // tn, nk),
        in_specs=[a_spec, b_spec] + [o_spec] * ne + [row_spec] * nr,
        out_specs=[o_spec] * no + [row_spec] * n_row_sums,
        out_shape=[jax.ShapeDtypeStruct((M, N), dt) for dt in out_dtypes] + [jax.ShapeDtypeStruct((1, N), F32)] * n_row_sums,
        scratch_shapes=[pltpu.VMEM((tm, tn), F32)] if nk > 1 else [],
        vmem=vmem, semantics=("arbitrary" if n_row_sums else "parallel", "parallel", "arbitrary"), carried=carried)
    outs = outs[0] if len(outs) == 1 else outs
    return outs if carried is None else (outs, landed)


def _rms_fwd(x, g, name):
    S, Dm = x.shape
    tr = _pick(S, (512,))

    def body(x_ref, g_ref, o_ref):
        xv = x_ref[...]
        r = lax.rsqrt(jnp.mean(xv * xv, axis=-1, keepdims=True) + EPS)
        o_ref[...] = (xv * r * g_ref[...]).astype(o_ref.dtype)

    row = pl.BlockSpec((tr, Dm), lambda i: (i, 0))
    return pl.pallas_call(
        body, name=name, grid=(S // tr,),
        in_specs=[row, pl.BlockSpec((1, Dm), lambda i: (0, 0))], out_specs=row,
        out_shape=jax.ShapeDtypeStruct((S, Dm), BF), compiler_params=_params(("parallel",)),
    )(x, g)


def _partner(x, half):
    if half == 64:
        return pltpu.roll(x, 64, 1)
    lane = lax.broadcasted_iota(jnp.int32, x.shape, 1)
    return jnp.where((lane % (2 * half)) < half, pltpu.roll(x, 128 - half, 1), pltpu.roll(x, half, 1))


def _rope(x, cos, sin, half):
    return x * cos + _partner(x, half) * sin


def _rope_t(dy, cos, sin, half):
    return dy * cos - _partner(dy, half) * sin


def _head_norm(x):
    r = lax.rsqrt(jnp.mean(x * x, axis=-1, keepdims=True) + EPS)
    return x * r, r


def _head_norm_bwd(dxh, xh, r):
    return r * (dxh - xh * jnp.mean(dxh * xh, axis=-1, keepdims=True))


def _cols(ref, off, width=HEAD_DIM):
    return ref[:, off:off + width]


def _prep_even_fwd(proj, cos, sin, gq, gk, carried=None):
    S = proj.shape[0]
    tr = _pick(S, (256,))

    def body(p_ref, cos_ref, sin_ref, gq_ref, gk_ref, qr_ref, kr_ref, vr_ref, qs_ref, ks_ref, vs_ref):
        cos_v, sin_v = cos_ref[...], sin_ref[...]
        for h in range(RET_HEADS):
            o = h * RET_DK
            qr_ref[:, o:o + RET_DK] = _rope(_cols(p_ref, OFF_QA + o), cos_v, sin_v, 64).astype(qr_ref.dtype)
            kr_ref[:, o:o + RET_DK] = (_rope(_cols(p_ref, OFF_KA + o), cos_v, sin_v, 64) * RET_SCALE).astype(kr_ref.dtype)
        vr_ref[...] = p_ref[:, OFF_VA:OFF_VA + 1024].astype(vr_ref.dtype)
        for h in range(SWA_HEADS):
            o = h * HEAD_DIM
            xh, _ = _head_norm(_cols(p_ref, OFF_QB + o))
            qs_ref[:, o:o + HEAD_DIM] = (xh * gq_ref[...] * ATT_SCALE).astype(qs_ref.dtype)
        for h in range(SWA_KV_HEADS):
            o = h * HEAD_DIM
            xh, _ = _head_norm(_cols(p_ref, OFF_KB + o))
            ks_ref[:, o:o + HEAD_DIM] = (xh * gk_ref[...]).astype(ks_ref.dtype)
        vs_ref[...] = p_ref[:, OFF_VB:OFF_VB + 256].astype(vs_ref.dtype)

    def row(w):
        return pl.BlockSpec((tr, w), lambda i: (i, 0))

    vec = pl.BlockSpec((1, HEAD_DIM), lambda i: (0, 0))
    widths = (512, 512, 1024, 1024, 256, 256)
    return _carry_call(
        body, (proj, cos, sin, gq, gk), name="prep_even_fwd", grid=(S // tr,),
        in_specs=[row(EVEN_IN), row(128), row(128), vec, vec],
        out_specs=[row(w) for w in widths],
        out_shape=[jax.ShapeDtypeStruct((S, w), BF) for w in widths],
        semantics=("parallel",), carried=carried)


def _prep_even_bwd(proj, cos, sin, gq, gk, dqr, dkr, dvr, dga, dqs, dks, dvs):
    S = proj.shape[0]
    tr = _pick(S, (256,))

    def body(p_ref, cos_ref, sin_ref, gq_ref, gk_ref, dqr_ref, dkr_ref, dvr_ref, dga_ref, dqs_ref, dks_ref,
             dvs_ref, dp_ref, dgq_ref, dgk_ref):
        cos_v, sin_v = cos_ref[...], sin_ref[...]
        dt = dp_ref.dtype
        for h in range(RET_HEADS):
            o = h * RET_DK
            dp_ref[:, OFF_QA + o:OFF_QA + o + RET_DK] = _rope_t(_cols(dqr_ref, o).astype(F32), cos_v, sin_v, 64).astype(dt)
            dp_ref[:, OFF_KA + o:OFF_KA + o + RET_DK] = _rope_t(_cols(dkr_ref, o).astype(F32) * RET_SCALE, cos_v, sin_v, 64).astype(dt)
        dp_ref[:, OFF_VA:OFF_VA + 1024] = dvr_ref[...].astype(dt)
        dp_ref[:, OFF_GA:OFF_GA + 1024] = dga_ref[...].astype(dt)
        dgq = jnp.zeros((1, HEAD_DIM), F32)
        for h in range(SWA_HEADS):
            o = h * HEAD_DIM
            xh, r = _head_norm(_cols(p_ref, OFF_QB + o))
            dy = _cols(dqs_ref, o).astype(F32) * ATT_SCALE
            dgq = dgq + _row_sum(dy * xh)
            dp_ref[:, OFF_QB + o:OFF_QB + o + HEAD_DIM] = _head_norm_bwd(dy * gq_ref[...], xh, r).astype(dt)
        dgk = jnp.zeros((1, HEAD_DIM), F32)
        for h in range(SWA_KV_HEADS):
            o = h * HEAD_DIM
            xh, r = _head_norm(_cols(p_ref, OFF_KB + o))
            dy = _cols(dks_ref, o)
            dgk = dgk + _row_sum(dy * xh)
            dp_ref[:, OFF_KB + o:OFF_KB + o + HEAD_DIM] = _head_norm_bwd(dy * gk_ref[...], xh, r).astype(dt)
        dp_ref[:, OFF_VB:OFF_VB + 256] = dvs_ref[...].astype(dt)

        @pl.when(pl.program_id(0) == 0)
        def _():
            dgq_ref[...] = jnp.zeros_like(dgq_ref)
            dgk_ref[...] = jnp.zeros_like(dgk_ref)

        dgq_ref[...] += dgq
        dgk_ref[...] += dgk

    def row(w):
        return pl.BlockSpec((tr, w), lambda i: (i, 0))

    vec = pl.BlockSpec((1, HEAD_DIM), lambda i: (0, 0))
    return pl.pallas_call(
        body, name="prep_even_bwd", grid=(S // tr,),
        in_specs=[row(EVEN_IN), row(128), row(128), vec, vec, row(512), row(512), row(1024), row(1024),
                  row(1024), row(256), row(256)],
        out_specs=[row(EVEN_IN), vec, vec],
        out_shape=[jax.ShapeDtypeStruct((S, EVEN_IN), BF), jax.ShapeDtypeStruct((1, HEAD_DIM), F32),
                   jax.ShapeDtypeStruct((1, HEAD_DIM), F32)],
        compiler_params=_params(("arbitrary",)),
    )(proj, cos, sin, gq, gk, dqr, dkr, dvr, dga, dqs, dks, dvs)


RET_CHUNK = 512
def _log_sigmoid_tile(logit_tile):
    def body(x_ref, o_ref):
        xv = x_ref[...]
        t = jnp.exp(-jnp.abs(xv))
        log1p_t = jnp.where(t < 1e-3, t * (1.0 - 0.5 * t), jnp.log(1.0 + t))
        o_ref[...] = jnp.minimum(xv, 0.0) - log1p_t

    full = pl.BlockSpec((8, 128), lambda: (0, 0))
    return pl.pallas_call(body, name="log_sigmoid", in_specs=[full], out_specs=full,
                          out_shape=jax.ShapeDtypeStruct((8, 128), F32))(logit_tile)


def _decay(diff, lf, lb):
    return jnp.exp(jnp.where(diff >= 0, lf * diff, -(lb * diff)))


def _col_iota(n):
    return lax.broadcasted_iota(jnp.int32, (n, 1), 0).astype(F32)


def _ret_scan(x, z, lg, asc, desc, name):
    S = x.shape[0]
    C = _pick(S, (RET_CHUNK,))
    nc = S // C
    (arow, aoff), (drow, doff) = asc, desc

    def body(lg_ref, xa_ref, za_ref, xd_ref, zd_ref, asc_ref, desc_ref, sa_ref, sd_ref):
        t = pl.program_id(0)

        @pl.when(t == 0)
        def _():
            sa_ref[...] = jnp.zeros_like(sa_ref)
            sd_ref[...] = jnp.zeros_like(sd_ref)

        j = _col_iota(C)
        for h in range(RET_HEADS):
            la, ld = lg_ref[arow, h], lg_ref[drow, h]
            kc, vc = slice(h * RET_DK, (h + 1) * RET_DK), slice(h * RET_DV, (h + 1) * RET_DV)
            asc_ref[h, 0] = sa_ref[h]
            desc_ref[h, 0] = sd_ref[h]
            xa = (xa_ref[:, kc].astype(F32) * jnp.exp(la * (C - 1 + aoff - j))).astype(xa_ref.dtype)
            xd = (xd_ref[:, kc].astype(F32) * jnp.exp(ld * (j + doff))).astype(xd_ref.dtype)
            sa_ref[h] = jnp.exp(jnp.full((1, RET_DV), la * C, F32)) * sa_ref[h] + _dot(xa, za_ref[:, vc], TN)
            sd_ref[h] = jnp.exp(jnp.full((1, RET_DV), ld * C, F32)) * sd_ref[h] + _dot(xd, zd_ref[:, vc], TN)

    state = jax.ShapeDtypeStruct((RET_HEADS, nc, RET_DK, RET_DV), F32)
    qk_w, v_w = RET_HEADS * RET_DK, RET_HEADS * RET_DV
    return pl.pallas_call(
        body, name=name, grid=(nc,),
        in_specs=[SMEM_SPEC,
                  pl.BlockSpec((C, qk_w), lambda t: (t, 0)), pl.BlockSpec((C, v_w), lambda t: (t, 0)),
                  pl.BlockSpec((C, qk_w), lambda t: (nc - 1 - t, 0)), pl.BlockSpec((C, v_w), lambda t: (nc - 1 - t, 0))],
        out_specs=[pl.BlockSpec((RET_HEADS, 1, RET_DK, RET_DV), lambda t: (0, t, 0, 0)),
                   pl.BlockSpec((RET_HEADS, 1, RET_DK, RET_DV), lambda t: (0, nc - 1 - t, 0, 0))],
        out_shape=[state, state],
        scratch_shapes=[pltpu.VMEM((RET_HEADS, RET_DK, RET_DV), F32), pltpu.VMEM((RET_HEADS, RET_DK, RET_DV), F32)],
        compiler_params=_params(("arbitrary",)),
    )(lg, x, z, x, z)


def _head_cols(h):
    return slice(h * RET_DK, (h + 1) * RET_DK), slice(h * RET_DV, (h + 1) * RET_DV)


def _ret_fwd(q, k, v, lg, sf, sb, proj, gn, carried=None):
    S = q.shape[0]
    C = _pick(S, (RET_CHUNK,))

    def body(lg_ref, q_ref, k_ref, v_ref, sf_ref, sb_ref, gate_ref, gn_ref, y_ref, o_ref):
        diff = (lax.broadcasted_iota(jnp.int32, (C, C), 0) - lax.broadcasted_iota(jnp.int32, (C, C), 1)).astype(F32)
        r = _col_iota(C)
        for h in range(RET_HEADS):
            lf, lb = lg_ref[0, h], lg_ref[1, h]
            kc, vc = _head_cols(h)
            qv = q_ref[:, kc]
            dt = qv.dtype
            y = _dot((_dot(qv, k_ref[:, kc], NT) * _decay(diff, lf, lb)).astype(dt), v_ref[:, vc])
            qf = qv.astype(F32)
            y = y + _dot((qf * jnp.exp(lf * (r + 1.0))).astype(dt), sf_ref[h, 0].astype(dt))
            y = y + _dot((qf * jnp.exp(lb * (C - r))).astype(dt), sb_ref[h, 0].astype(dt))
            y_ref[:, vc] = y
            yh, _ = _head_norm(y)
            gate = gate_ref[:, vc]
            o_ref[:, vc] = (gate * _sigmoid(gate) * (yh * gn_ref[:, vc])).astype(o_ref.dtype)

    state = pl.BlockSpec((RET_HEADS, 1, RET_DK, RET_DV), lambda c: (0, c, 0, 0))
    qk = pl.BlockSpec((C, RET_HEADS * RET_DK), lambda c: (c, 0))
    wide = pl.BlockSpec((C, 1024), lambda c: (c, 0))
    (y, o), landed = _carry_call(
        body, (lg, q, k, v, sf, sb, proj, gn), name="retention_fwd", grid=(S // C,),
        in_specs=[SMEM_SPEC, qk, qk, wide, state, state, pl.BlockSpec((C, 1024), lambda c: (c, OFF_GA // 1024)),
                  pl.BlockSpec((1, 1024), lambda c: (0, 0))],
        out_specs=[wide, wide],
        out_shape=[jax.ShapeDtypeStruct((S, 1024), F32), jax.ShapeDtypeStruct((S, 1024), BF)],
        semantics=("parallel",), carried=carried)
    return y, o, landed


def _ret_bwd(q, k, v, dy, lg, logit, sf, sb, hf, hb):
    S = q.shape[0]
    C = _pick(S, (RET_CHUNK,))
    nc = S // C

    def body(lg_ref, logit_ref, q_ref, k_ref, v_ref, dy_ref, sf_ref, sb_ref, hf_ref, hb_ref,
             dq_ref, dk_ref, dv_ref, dlg_ref, acc_ref):
        c = pl.program_id(0)

        @pl.when(c == 0)
        def _():
            acc_ref[...] = jnp.zeros_like(acc_ref)

        diff = (lax.broadcasted_iota(jnp.int32, (C, C), 0) - lax.broadcasted_iota(jnp.int32, (C, C), 1)).astype(F32)
        r = _col_iota(C)
        for h in range(RET_HEADS):
            one_head(h, diff, r, lg_ref, q_ref, k_ref, v_ref, dy_ref, sf_ref, sb_ref, hf_ref, hb_ref,
                     dq_ref, dk_ref, dv_ref, acc_ref)

        @pl.when(c == nc - 1)
        def _():
            for h in range(RET_HEADS):
                for d in range(2):
                    gate = 1.0 / (1.0 + jnp.exp(jnp.full((8, 128), logit_ref[d, h], F32)))
                    dlg_ref[h, d] = acc_ref[h, d] * gate

    def one_head(h, diff, r, lg_ref, q_ref, k_ref, v_ref, dy_ref, sf_ref, sb_ref, hf_ref, hb_ref,
                 dq_ref, dk_ref, dv_ref, acc_ref):
        lf, lb = lg_ref[0, h], lg_ref[1, h]
        kc, vc = _head_cols(h)
        qv, kv, vv, dyv = q_ref[:, kc], k_ref[:, kc], v_ref[:, vc], dy_ref[:, vc]
        dt = qv.dtype
        qf, kf = qv.astype(F32), kv.astype(F32)
        dec = _decay(diff, lf, lb)
        sc = _dot(qv, kv, NT) * dec
        dp = _dot(dyv, vv, NT)
        da = (dp * dec).astype(dt)
        dq = _dot(da, kv)
        dk = _dot(da, qv, TN)
        dv = _dot(sc.astype(dt), dyv, TN)
        w = sc * dp * diff
        tot_w, tot_f = _all_sum(w), _all_sum(jnp.where(diff >= 0, w, 0.0))
        d_lf, d_lb = tot_f, tot_f - tot_w
        a, b = jnp.exp(lf * (r + 1.0)), jnp.exp(lb * (C - r))
        e, f = jnp.exp(lf * (C - 1.0 - r)), jnp.exp(lb * r)
        sfv, sbv, hfv, hbv = sf_ref[h, 0], sb_ref[h, 0], hf_ref[h, 0], hb_ref[h, 0]
        t_f, t_b = _dot(dyv, sfv.astype(dt), NT), _dot(dyv, sbv.astype(dt), NT)
        u_f, u_b = _dot(vv, hfv.astype(dt), NT), _dot(vv, hbv.astype(dt), NT)
        dq_ref[:, kc] = (dq + a * t_f + b * t_b).astype(dq_ref.dtype)
        dk_ref[:, kc] = (dk + e * u_f + f * u_b).astype(dk_ref.dtype)
        dv_ref[:, vc] = (dv + _dot((kf * e).astype(dt), hfv.astype(dt)) + _dot((kf * f).astype(dt), hbv.astype(dt))).astype(dv_ref.dtype)
        row_q_f = jnp.sum(qf * t_f, axis=-1, keepdims=True)
        row_q_b = jnp.sum(qf * t_b, axis=-1, keepdims=True)
        row_k_f = jnp.sum(kf * u_f, axis=-1, keepdims=True)
        row_k_b = jnp.sum(kf * u_b, axis=-1, keepdims=True)
        gf_c = jnp.exp(jnp.full((1, 1), lf * C, F32))
        gb_c = jnp.exp(jnp.full((1, 1), lb * C, F32))
        d_lf = d_lf + _all_sum((r + 1.0) * a * row_q_f + (C - 1.0 - r) * e * row_k_f) + C * gf_c * _all_sum(hfv * sfv)
        d_lb = d_lb + _all_sum((C - r) * b * row_q_b + r * f * row_k_b) + C * gb_c * _all_sum(hbv * sbv)
        acc_ref[h, 0] += jnp.broadcast_to(d_lf, (8, 128))
        acc_ref[h, 1] += jnp.broadcast_to(d_lb, (8, 128))

    state = pl.BlockSpec((RET_HEADS, 1, RET_DK, RET_DV), lambda c: (0, c, 0, 0))
    qk = pl.BlockSpec((C, RET_HEADS * RET_DK), lambda c: (c, 0))
    vy = pl.BlockSpec((C, RET_HEADS * RET_DV), lambda c: (c, 0))
    return pl.pallas_call(
        body, name="retention_bwd", grid=(nc,),
        in_specs=[SMEM_SPEC, SMEM_SPEC, qk, qk, vy, vy, state, state, state, state],
        out_specs=[qk, qk, vy, pl.BlockSpec((RET_HEADS, 2, 8, 128), lambda c: (0, 0, 0, 0))],
        out_shape=[jax.ShapeDtypeStruct((S, RET_HEADS * RET_DK), BF), jax.ShapeDtypeStruct((S, RET_HEADS * RET_DK), BF),
                   jax.ShapeDtypeStruct((S, RET_HEADS * RET_DV), BF),
                   jax.ShapeDtypeStruct((RET_HEADS, 2, 8, 128), F32)],
        scratch_shapes=[pltpu.VMEM((RET_HEADS, 2, 8, 128), F32)],
        compiler_params=_params(("arbitrary",)),
    )(lg, logit, q, k, v, dy, sf, sb, hf, hb)


def _ret_post_bwd(y, proj, gn, do):
    S = y.shape[0]
    tr = _pick(S, (512,))

    def body(y_ref, g_ref, gn_ref, do_ref, dy_ref, dg_ref, dgn_ref):
        @pl.when(pl.program_id(0) == 0)
        def _():
            dgn_ref[...] = jnp.zeros_like(dgn_ref)

        for h in range(RET_HEADS):
            o = h * RET_DV
            yh, r = _head_norm(_cols(y_ref, o, RET_DV))
            gate = _cols(g_ref, o, RET_DV)
            gnh = gn_ref[:, o:o + RET_DV]
            dout = _cols(do_ref, o, RET_DV).astype(F32)
            sg = _sigmoid(gate)
            dz = dout * (gate * sg)
            dg_ref[:, o:o + RET_DV] = (dout * (yh * gnh) * (sg * (1.0 + gate * (1.0 - sg)))).astype(dg_ref.dtype)
            dgn_ref[:, o:o + RET_DV] += _row_sum(dz * yh)
            dy_ref[:, o:o + RET_DV] = _head_norm_bwd(dz * gnh, yh, r).astype(dy_ref.dtype)

    row = pl.BlockSpec((tr, 1024), lambda i: (i, 0))
    vec = pl.BlockSpec((1, 1024), lambda i: (0, 0))
    return pl.pallas_call(
        body, name="retention_post_bwd", grid=(S // tr,),
        in_specs=[row, pl.BlockSpec((tr, 1024), lambda i: (i, OFF_GA // 1024)), vec, row],
        out_specs=[row, row, vec],
        out_shape=[jax.ShapeDtypeStruct((S, 1024), BF), jax.ShapeDtypeStruct((S, 1024), BF),
                   jax.ShapeDtypeStruct((1, 1024), F32)],
        compiler_params=_params(("arbitrary",)),
    )(y, proj, gn, do)


def _t5_bucket_map():
    r = jnp.arange(BLOCK)
    j = jnp.arange(3 * BLOCK)
    rel = j[None, :] - BLOCK - r[:, None]
    nb = T5_BUCKETS // 2
    max_exact = nb // 2
    ret = jnp.where(rel > 0, nb, 0)
    n = jnp.abs(rel)
    nf = jnp.maximum(n, 1).astype(jnp.float32)
    large = max_exact + (jnp.log(nf / max_exact) / math.log(T5_MAX_DIST / max_exact)
                         * (nb - max_exact)).astype(jnp.int32)
    large = jnp.minimum(large, nb - 1)
    bucket = ret + jnp.where(n < max_exact, n, large)
    return jnp.where(jnp.abs(rel) <= WINDOW, bucket, -1).astype(jnp.int32)


SWA_G = SWA_HEADS // SWA_KV_HEADS
SWA_LANES = SWA_G * BLOCK


def _t5_bias(table, bucket_t):
    def body(t_ref, b_ref, o_ref):
        bk = b_ref[...]
        for h in range(SWA_HEADS):
            acc = jnp.full(bk.shape, NEG_INF, F32)
            for b in range(T5_BUCKETS):
                acc = jnp.where(bk == b, t_ref[b, h], acc)
            o_ref[h // SWA_G, :, (h % SWA_G) * BLOCK:(h % SWA_G + 1) * BLOCK] = acc

    return pl.pallas_call(
        body, name="t5_bias", in_specs=[SMEM_SPEC, pl.BlockSpec((3 * BLOCK, BLOCK), lambda: (0, 0))],
        out_specs=pl.BlockSpec((SWA_KV_HEADS, 3 * BLOCK, SWA_LANES), lambda: (0, 0, 0)),
        out_shape=jax.ShapeDtypeStruct((SWA_KV_HEADS, 3 * BLOCK, SWA_LANES), F32),
    )(table, bucket_t)


def _t5_table_grad(dbias, bucket_t):
    def body(d_ref, b_ref, o_ref):
        bk = b_ref[...]
        lane = lax.broadcasted_iota(jnp.int32, (1, 128), 1)
        for b in range(T5_BUCKETS):
            hit = bk == b
            row = jnp.zeros((1, 128), F32)
            for h in range(SWA_HEADS):
                d = d_ref[h // SWA_G, :, (h % SWA_G) * BLOCK:(h % SWA_G + 1) * BLOCK]
                row = row + jnp.where(lane == h, _all_sum(jnp.where(hit, d, 0.0)), 0.0)
            o_ref[b:b + 1, :] = row

    return pl.pallas_call(
        body, name="t5_table_grad",
        in_specs=[pl.BlockSpec((SWA_KV_HEADS, 3 * BLOCK, SWA_LANES), lambda: (0, 0, 0)),
                  pl.BlockSpec((3 * BLOCK, BLOCK), lambda: (0, 0))],
        out_specs=pl.BlockSpec((T5_BUCKETS, 128), lambda: (0, 0)),
        out_shape=jax.ShapeDtypeStruct((T5_BUCKETS, 128), F32),
    )(dbias, bucket_t)


def _swa_scores(i, nb, q4, kw, bias_t, sink_row):
    s = _dot(kw, q4, NT) + bias_t
    row = lax.broadcasted_iota(jnp.int32, s.shape, 0)
    first_row = jnp.where(i == 0, BLOCK, 0)
    end_row = jnp.where(i == nb - 1, 2 * BLOCK, 3 * BLOCK)
    s = jnp.where((row < first_row) | (row >= end_row), NEG_INF, s)
    m = jnp.maximum(jnp.max(s, axis=0, keepdims=True), sink_row)
    p = jnp.exp(s - m)
    e_sink = jnp.exp(sink_row - m)
    inv = 1.0 / (jnp.sum(p, axis=0, keepdims=True) + e_sink)
    return p * inv, e_sink * inv


def _swa_group(q_ref, sink_ref, kh):
    heads = range(kh * SWA_G, (kh + 1) * SWA_G)
    q4 = jnp.concatenate([_cols(q_ref, h * HEAD_DIM) for h in heads], axis=0)
    sink_row = jnp.concatenate([jnp.full((1, BLOCK), sink_ref[0, h], F32) for h in heads], axis=1)
    return q4, sink_row


def _swa_unstack(ref, kh, x_t):
    for g in range(SWA_G):
        h = kh * SWA_G + g
        ref[:, h * HEAD_DIM:(h + 1) * HEAD_DIM] = x_t[:, g * BLOCK:(g + 1) * BLOCK].T.astype(ref.dtype)


def _swa_window(ref, i, nb, off):
    prev, nxt = jnp.maximum(i - 1, 0), jnp.minimum(i + 1, nb - 1)
    rows = [pl.ds(pl.multiple_of(b * BLOCK, BLOCK), BLOCK) for b in (prev, i, nxt)]
    return jnp.concatenate([ref[r, off:off + HEAD_DIM] for r in rows], axis=0), rows


def _swa_fwd(q, k, v, bias, sink, carried=None):
    S = q.shape[0]
    nb = S // BLOCK

    def body(sink_ref, q_ref, k_ref, v_ref, bias_ref, o_ref):
        i = pl.program_id(0)
        for kh in range(SWA_KV_HEADS):
            kw, _ = _swa_window(k_ref, i, nb, kh * HEAD_DIM)
            vw, _ = _swa_window(v_ref, i, nb, kh * HEAD_DIM)
            q4, sink_row = _swa_group(q_ref, sink_ref, kh)
            p, _ = _swa_scores(i, nb, q4, kw, bias_ref[kh], sink_row)
            _swa_unstack(o_ref, kh, _dot(vw, p.astype(vw.dtype), TN))

    full_kv = pl.BlockSpec((S, SWA_KV_HEADS * HEAD_DIM), lambda i: (0, 0))
    (o,), landed = _carry_call(
        body, (sink, q, k, v, bias), name="swa_fwd", grid=(nb,),
        in_specs=[SMEM_SPEC, pl.BlockSpec((BLOCK, 1024), lambda i: (i, 0)), full_kv, full_kv,
                  pl.BlockSpec((SWA_KV_HEADS, 3 * BLOCK, SWA_LANES), lambda i: (0, 0, 0))],
        out_specs=[pl.BlockSpec((BLOCK, 1024), lambda i: (i, 0))],
        out_shape=[jax.ShapeDtypeStruct((S, 1024), BF)], semantics=("parallel",), carried=carried)
    return o, landed


def _swa_bwd(q, k, v, do, bias, sink, carried=None):
    S = q.shape[0]
    nb = S // BLOCK

    def body(sink_ref, q_ref, k_ref, v_ref, do_ref, bias_ref, dq_ref, dk_ref, dv_ref, dbias_ref, dsink_ref):
        i = pl.program_id(0)

        @pl.when(i == 0)
        def _():
            dk_ref[...] = jnp.zeros_like(dk_ref)
            dv_ref[...] = jnp.zeros_like(dv_ref)
            dbias_ref[...] = jnp.zeros_like(dbias_ref)
            dsink_ref[...] = jnp.zeros_like(dsink_ref)

        for kh in range(SWA_KV_HEADS):
            off = kh * HEAD_DIM
            kw, rows = _swa_window(k_ref, i, nb, off)
            vw, _ = _swa_window(v_ref, i, nb, off)
            q4, sink_row = _swa_group(q_ref, sink_ref, kh)
            p, p_sink = _swa_scores(i, nb, q4, kw, bias_ref[kh], sink_row)
            do4 = jnp.concatenate([_cols(do_ref, (kh * SWA_G + g) * HEAD_DIM) for g in range(SWA_G)], axis=0).astype(vw.dtype)
            dp = _dot(vw, do4, NT)
            delta = jnp.sum(p * dp, axis=0, keepdims=True)
            ds = p * (dp - delta)
            dsb = ds.astype(q4.dtype)
            _swa_unstack(dq_ref, kh, _dot(kw, dsb, TN))
            dkw = _dot(dsb, q4)
            dvw = _dot(p.astype(do4.dtype), do4)
            dbias_ref[kh] += ds
            sink_term = p_sink * delta
            for g in range(SWA_G):
                h = kh * SWA_G + g
                dsink_ref[h:h + 1, :] += jnp.broadcast_to(-_all_sum(sink_term[:, g * BLOCK:(g + 1) * BLOCK]), (1, 128))
            for b, r in enumerate(rows):
                dk_ref[r, off:off + HEAD_DIM] += dkw[b * BLOCK:(b + 1) * BLOCK]
                dv_ref[r, off:off + HEAD_DIM] += dvw[b * BLOCK:(b + 1) * BLOCK]

    full_kv = pl.BlockSpec((S, SWA_KV_HEADS * HEAD_DIM), lambda i: (0, 0))
    blk = pl.BlockSpec((BLOCK, 1024), lambda i: (i, 0))
    bias_spec = pl.BlockSpec((SWA_KV_HEADS, 3 * BLOCK, SWA_LANES), lambda i: (0, 0, 0))
    outs, landed = _carry_call(
        body, (sink, q, k, v, do, bias), name="swa_bwd", grid=(nb,),
        in_specs=[SMEM_SPEC, blk, full_kv, full_kv, blk, bias_spec],
        out_specs=[blk, full_kv, full_kv, bias_spec, pl.BlockSpec((8, 128), lambda i: (0, 0))],
        out_shape=[jax.ShapeDtypeStruct((S, 1024), BF), jax.ShapeDtypeStruct((S, 256), F32),
                   jax.ShapeDtypeStruct((S, 256), F32),
                   jax.ShapeDtypeStruct((SWA_KV_HEADS, 3 * BLOCK, SWA_LANES), F32), jax.ShapeDtypeStruct((8, 128), F32)],
        vmem=VMEM_LARGE, semantics=("arbitrary",), carried=carried)
    return (*outs, landed)


def _prep_odd_fwd(proj, cos, sin, gq, gk):
    S = proj.shape[0]
    tr = _pick(S, (512,))

    def body(p_ref, cos_ref, sin_ref, gq_ref, gk_ref, q_ref, k_ref, v_ref):
        cos_v, sin_v = cos_ref[...], sin_ref[...]
        for h in range(AX_HEADS):
            o = h * HEAD_DIM
            xh, _ = _head_norm(_cols(p_ref, o))
            q_ref[:, o:o + HEAD_DIM] = (_rope(xh * gq_ref[...], cos_v, sin_v, 32) * AX_SCALE).astype(q_ref.dtype)
        for h in range(AX_KV_HEADS):
            o = h * HEAD_DIM
            xh, _ = _head_norm(_cols(p_ref, 1024 + o))
            k_ref[:, o:o + HEAD_DIM] = _rope(xh * gk_ref[...], cos_v, sin_v, 32).astype(k_ref.dtype)
        v_ref[...] = p_ref[:, 1280:1536].astype(v_ref.dtype)

    def row(w):
        return pl.BlockSpec((tr, w), lambda i: (i, 0))

    vec = pl.BlockSpec((1, HEAD_DIM), lambda i: (0, 0))
    return pl.pallas_call(
        body, name="prep_odd_fwd", grid=(S // tr,),
        in_specs=[row(ODD_IN), row(128), row(128), vec, vec], out_specs=[row(1024), row(256), row(256)],
        out_shape=[jax.ShapeDtypeStruct((S, w), BF) for w in (1024, 256, 256)],
        compiler_params=_params(("parallel",)),
    )(proj, cos, sin, gq, gk)


def _prep_odd_bwd(proj, cos, sin, gq, gk, dq, dk, dv):
    S = proj.shape[0]
    tr = _pick(S, (512,))

    def body(p_ref, cos_ref, sin_ref, gq_ref, gk_ref, dq_ref, dk_ref, dv_ref, dp_ref, dgq_ref, dgk_ref):
        cos_v, sin_v = cos_ref[...], sin_ref[...]
        dt = dp_ref.dtype
        dgq = jnp.zeros((1, HEAD_DIM), F32)
        for h in range(AX_HEADS):
            o = h * HEAD_DIM
            xh, r = _head_norm(_cols(p_ref, o))
            dy = _rope_t(_cols(dq_ref, o).astype(F32) * AX_SCALE, cos_v, sin_v, 32)
            dgq = dgq + _row_sum(dy * xh)
            dp_ref[:, o:o + HEAD_DIM] = _head_norm_bwd(dy * gq_ref[...], xh, r).astype(dt)
        dgk = jnp.zeros((1, HEAD_DIM), F32)
        for h in range(AX_KV_HEADS):
            o = h * HEAD_DIM
            xh, r = _head_norm(_cols(p_ref, 1024 + o))
            dy = _rope_t(_cols(dk_ref, o), cos_v, sin_v, 32)
            dgk = dgk + _row_sum(dy * xh)
            dp_ref[:, 1024 + o:1024 + o + HEAD_DIM] = _head_norm_bwd(dy * gk_ref[...], xh, r).astype(dt)
        dp_ref[:, 1280:1536] = dv_ref[...].astype(dt)

        @pl.when(pl.program_id(0) == 0)
        def _():
            dgq_ref[...] = jnp.zeros_like(dgq_ref)
            dgk_ref[...] = jnp.zeros_like(dgk_ref)

        dgq_ref[...] += dgq
        dgk_ref[...] += dgk

    def row(w):
        return pl.BlockSpec((tr, w), lambda i: (i, 0))

    vec = pl.BlockSpec((1, HEAD_DIM), lambda i: (0, 0))
    return pl.pallas_call(
        body, name="prep_odd_bwd", grid=(S // tr,),
        in_specs=[row(ODD_IN), row(128), row(128), vec, vec, row(1024), row(256), row(256)],
        out_specs=[row(ODD_IN), vec, vec],
        out_shape=[jax.ShapeDtypeStruct((S, ODD_IN), BF), jax.ShapeDtypeStruct((1, HEAD_DIM), F32),
                   jax.ShapeDtypeStruct((1, HEAD_DIM), F32)],
        compiler_params=_params(("arbitrary",)),
    )(proj, cos, sin, gq, gk, dq, dk, dv)


def _loop_unrolled(n, factor, step, init):
    while n % factor:
        factor //= 2

    def trip(t, carry):
        for u in range(factor):
            carry = step(factor * t + u, carry)
        return carry

    return lax.fori_loop(0, n // factor, trip, init)


def _flash_fwd(q, k, v, carried=None):
    S = q.shape[0]
    tq = _pick(S, (512,))
    tk = _pick(S, (1024, 512))
    nk = S // tk
    G = AX_HEADS // AX_KV_HEADS

    def body(q_ref, k_ref, v_ref, o_ref, lse_ref):
        qv = q_ref[...]

        def step(j, carry):
            m, l, acc = carry
            rows = pl.ds(pl.multiple_of(j * tk, tk), tk)
            s = _dot(qv, k_ref[rows, :], NT)
            m_new = jnp.maximum(m, jnp.max(s, axis=-1, keepdims=True))
            alpha = jnp.exp2(m - m_new)
            p = jnp.exp2(s - m_new)
            l = alpha * l + jnp.sum(p, axis=-1, keepdims=True)
            acc = alpha * acc + _dot(p.astype(v_ref.dtype), v_ref[rows, :])
            return m_new, l, acc

        init = (jnp.full((tq, 1), NEG_INF, F32), jnp.zeros((tq, 1), F32), jnp.zeros((tq, HEAD_DIM), F32))
        m, l, acc = _loop_unrolled(nk, 8, step, init)
        o_ref[...] = (acc / l).astype(o_ref.dtype)
        lse_ref[0] = jnp.broadcast_to(m + jnp.log2(l), (tq, 128))

    (o, lse), landed = _carry_call(
        body, (q, k, v), name="flash_fwd", grid=(AX_HEADS, S // tq),
        in_specs=[pl.BlockSpec((tq, HEAD_DIM), lambda h, i: (i, h)),
                  pl.BlockSpec((S, HEAD_DIM), lambda h, i: (0, h // G)),
                  pl.BlockSpec((S, HEAD_DIM), lambda h, i: (0, h // G))],
        out_specs=[pl.BlockSpec((tq, HEAD_DIM), lambda h, i: (i, h)),
                   pl.BlockSpec((1, tq, 128), lambda h, i: (h, i, 0))],
        out_shape=[jax.ShapeDtypeStruct((S, AX_HEADS * HEAD_DIM), BF), jax.ShapeDtypeStruct((AX_HEADS, S, 128), F32)],
        semantics=("parallel", "parallel"), carried=carried)
    return o, lse, landed


def _flash_bwd(q, k, v, o, do, lse, carried=None):
    S = q.shape[0]
    tq = _pick(S, (512,))
    tk = _pick(S, (1024, 512))
    nq, nk = S // tq, S // tk
    G = AX_HEADS // AX_KV_HEADS

    def body(q_ref, k_ref, v_ref, o_ref, do_ref, lse_ref, dq_ref, dk_ref, dv_ref):
        g, i = pl.program_id(1), pl.program_id(2)

        @pl.when((g == 0) & (i == 0))
        def _():
            dk_ref[...] = jnp.zeros_like(dk_ref)
            dv_ref[...] = jnp.zeros_like(dv_ref)

        qv = q_ref[...]
        do_f = do_ref[...].astype(F32)
        dob = do_f.astype(qv.dtype)
        dob_ln2 = (do_f * LN2).astype(qv.dtype)
        delta = jnp.sum(do_f * o_ref[...].astype(F32), axis=-1, keepdims=True) * LN2
        lse_col = lse_ref[0][:, 0:1]

        def step(j, dq):
            rows = pl.ds(pl.multiple_of(j * tk, tk), tk)
            kj, vj = k_ref[rows, :], v_ref[rows, :]
            p = jnp.exp2(_dot(qv, kj, NT) - lse_col)
            dp = _dot(dob_ln2, vj, NT)
            ds = (p * (dp - delta)).astype(qv.dtype)
            dk_ref[rows, :] += _dot(ds, qv, TN)
            dv_ref[rows, :] += _dot(p.astype(dob.dtype), dob, TN)
            return dq + _dot(ds, kj)

        dq_ref[...] = _loop_unrolled(nk, 4, step, jnp.zeros((tq, HEAD_DIM), F32)).astype(dq_ref.dtype)

    q_spec = pl.BlockSpec((tq, HEAD_DIM), lambda kh, g, i: (i, kh * G + g))
    kv_spec = pl.BlockSpec((S, HEAD_DIM), lambda kh, g, i: (0, kh))
    (dq, dk, dv), landed = _carry_call(
        body, (q, k, v, o, do, lse), name="flash_bwd", grid=(AX_KV_HEADS, G, nq),
        in_specs=[q_spec, kv_spec, kv_spec, q_spec, q_spec,
                  pl.BlockSpec((1, tq, 128), lambda kh, g, i: (kh * G + g, i, 0))],
        out_specs=[q_spec, kv_spec, kv_spec],
        out_shape=[jax.ShapeDtypeStruct((S, AX_HEADS * HEAD_DIM), BF), jax.ShapeDtypeStruct((S, 256), F32),
                   jax.ShapeDtypeStruct((S, 256), F32)],
        vmem=VMEM_LARGE, semantics=("arbitrary", "arbitrary", "arbitrary"), carried=carried)
    return dq, dk, dv, landed


def _rope_angles(pos, dim, theta):
    inv = theta ** (-jnp.arange(0, dim, 2, dtype=jnp.float32) / dim)
    return pos.astype(jnp.float32)[:, None] * inv[None, :]


def _rope_tables(S):
    ang = _rope_angles(jnp.arange(S), RET_DK, RET_THETA)
    c, s = jnp.cos(ang), jnp.sin(ang)
    ret = (jnp.concatenate([c, c], -1), jnp.concatenate([-s, s], -1))
    rows = S // GRID_W
    ar, ac = _rope_angles(jnp.arange(rows), HEAD_DIM // 2, AX_THETA), _rope_angles(jnp.arange(GRID_W), HEAD_DIM // 2, AX_THETA)
    cr, sr = jnp.repeat(jnp.cos(ar), GRID_W, axis=0), jnp.repeat(jnp.sin(ar), GRID_W, axis=0)
    cc, sc = jnp.tile(jnp.cos(ac), (rows, 1)), jnp.tile(jnp.sin(ac), (rows, 1))
    ax = (jnp.concatenate([cr, cr, cc, cc], -1), jnp.concatenate([-sr, sr, -sc, sc], -1))
    return ret, ax


def _pad_tile(a):
    return jnp.pad(a.astype(F32), ((0, 8 - a.shape[0]), (0, 128 - a.shape[1])))


def _relu2_epilogue(acc):
    r = jnp.maximum(acc, 0.0)
    return acc, r * r


def _relu2_bwd_epilogue(acc, u):
    return (acc * (2.0 * jnp.maximum(u.astype(F32), 0.0)),)


def _add_epilogue(acc, res):
    return (acc + res,)


def _add_norm_epilogue(acc, res, g):
    y = acc + res
    r = lax.rsqrt(jnp.mean(y * y, axis=-1, keepdims=True) + EPS)
    return y, y * r * g


def _residual_mm(a, w, res, g_next, name):
    return _mm(a, w, name=name, extras=(res,), rows=(g_next,), out_dtypes=(F32, BF), epilogue=_add_norm_epilogue,
               vmem=VMEM_LARGE)


def _rms_bwd_epilogue(dh, x, dres, g):
    r = lax.rsqrt(jnp.mean(x * x, axis=-1, keepdims=True) + EPS)
    xh = x * r
    dxh = dh * g
    dx = r * (dxh - xh * jnp.mean(dxh * xh, axis=-1, keepdims=True)) + dres
    return dx, dx, _row_sum(dh * xh)


def _norm_bwd_mm(mm, dproj, w, x, dres, g, name):
    return mm(dproj, w, tb=True, name=name, out_dtypes=(F32, BF), extras=(x, dres), rows=(g,),
              epilogue=_rms_bwd_epilogue, n_row_sums=1, tm=512)


def _loss_epilogue(acc, res, target):
    e = acc + res - target
    d = e * (1.0 / D_MODEL)
    return d, d, _row_sum(e * e)


def _mlp_fwd(x, h, w_up, w_down, tag, g_next=None, target=None):
    u, a = _mm(h, w_up, name=f"mlp_up_{tag}", out_dtypes=(BF, BF), epilogue=_relu2_epilogue)
    if target is not None:
        return _mm(a, w_down, name=f"mlp_down_{tag}", extras=(x, target), out_dtypes=(F32, BF), epilogue=_loss_epilogue,
                   n_row_sums=1, tm=512), (h, u, a)
    if g_next is None:
        y, h_next = _mm(a, w_down, name=f"mlp_down_{tag}", extras=(x,), epilogue=_add_epilogue), None
    else:
        y, h_next = _residual_mm(a, w_down, x, g_next, f"mlp_down_{tag}")
    return y, h_next, (h, u, a)


def _mlp_bwd(x, g, w_up, w_down, saved, dy, dyb, tag, mm_exchange=None):
    h, u, a = saved
    du = _mm(dyb, w_down, tb=True, name=f"mlp_down_dx_{tag}", out_dtypes=(BF,), extras=(u,), epilogue=_relu2_bwd_epilogue)
    dw_down = _mm(a, dyb, ta=True, name=f"mlp_down_dw_{tag}", out_dtypes=(BF,))
    dw_up = _mm(h, du, ta=True, name=f"mlp_up_dw_{tag}", out_dtypes=(BF,))
    if mm_exchange is None:
        mm = _mm
    else:
        mm = lambda *args, **kw: mm_exchange([("down" + tag, dw_down)], *args, **kw)
    dx, dxb, dg = _norm_bwd_mm(mm, du, w_up, x, dy, g, f"mlp_up_dx_{tag}")
    return dx, dxb, dg, dw_up, dw_down


COL_SHARDED = ("in_even", "in_odd", "up0", "up1")


def _assemble(key, g):
    if key in COL_SHARDED:
        return g.transpose(1, 0, 2).reshape(g.shape[1], N_DEV * g.shape[2])
    return g.reshape(N_DEV * g.shape[1], g.shape[2])


def _split(key, full):
    rows, cols = full.shape
    if key in COL_SHARDED:
        return full.reshape(rows, N_DEV, cols // N_DEV).transpose(1, 0, 2)
    return full.reshape(N_DEV, rows // N_DEV, cols)


def _local_step(x, target, W, P, late=None):
    S = x.shape[0]
    W = dict(W)
    landed = {}

    def gather_while(stage):
        return None if late is None else _Exchange([s for _, s in late[stage]], gather=True)

    def arrived(stage, outs):
        for (key, _), g in zip([] if late is None else late[stage], outs):
            W[key] = _assemble(key, g)

    def exchange_while(grads):
        return None if late is None else _Exchange([_split(k, g) for k, g in grads], gather=False)

    def left(grads, outs):
        for (key, _), l in zip(grads, outs):
            landed[key] = l

    def mm_gather(stage, *args, **kw):
        if late is None:
            return _mm(*args, **kw)
        out, outs = _mm(*args, carried=gather_while(stage), **kw)
        arrived(stage, outs)
        return out

    def mm_exchange(grads, *args, **kw):
        if late is None:
            return _mm(*args, **kw)
        out, outs = _mm(*args, carried=exchange_while(grads), **kw)
        left(grads, outs)
        return out

    (cos_r, sin_r), (cos_a, sin_a) = _rope_tables(S)
    bucket = _t5_bucket_map().T
    logit = P["ret_decay_logit"]
    lg = _log_sigmoid_tile(_pad_tile(logit))
    bias = _t5_bias(P["t5_table"], bucket)
    nmix, nmlp = P["norm_mix"], P["norm_mlp"]

    h0 = _rms_fwd(x, nmix[0:1], "mix_norm_0")
    proj_e = mm_gather("in_even", h0, W["in_even"], name="in_even")
    (qr, kr, vr, qs, ks, vs), outs = _prep_even_fwd(proj_e, cos_r, sin_r, P["swa_q_norm"], P["swa_k_norm"],
                                                    gather_while("prep_even_fwd"))
    arrived("prep_even_fwd", outs)
    sf, sb = _ret_scan(kr, vr, lg, (0, 0), (1, 0), "retention_states")
    y_ret, oa, outs = _ret_fwd(qr, kr, vr, lg, sf, sb, proj_e, P["ret_norm"], gather_while("retention_fwd"))
    arrived("retention_fwd", outs)
    ob, outs = _swa_fwd(qs, ks, vs, bias, P["swa_sink"], gather_while("swa_fwd"))
    arrived("swa_fwd", outs)
    wo_a, wo_b = W["out_even"][:1024], W["out_even"][1024:]
    x1 = _mm(oa, wo_a, name="out_even_a", extras=(x,), epilogue=_add_epilogue)
    x1, h1 = _residual_mm(ob, wo_b, x1, nmlp[0:1], "out_even_b")
    x2, h2, mlp0 = _mlp_fwd(x1, h1, W["up0"], W["down0"], "0", g_next=nmix[1:2])

    proj_o = _mm(h2, W["in_odd"], name="in_odd")
    qx, kx, vx = _prep_odd_fwd(proj_o, cos_a, sin_a, P["ax_q_norm"], P["ax_k_norm"])
    ox, lse, outs = _flash_fwd(qx, kx, vx, gather_while("flash_fwd"))
    arrived("flash_fwd", outs)
    x3, h3 = _residual_mm(ox, W["out_odd"], x2, nmlp[1:2], "out_odd")
    (d4, d4b, loss_row), mlp1 = _mlp_fwd(x3, h3, W["up1"], W["down1"], "1", target=target)

    d3, d3b, dnmlp1, dw_up1, dw_down1 = _mlp_bwd(x3, nmlp[1:2], W["up1"], W["down1"], mlp1, d4, d4b, "1")
    dox = _mm(d3b, W["out_odd"], tb=True, name="out_odd_dx")
    dw_out_odd = _mm(ox, d3b, ta=True, name="out_odd_dw", out_dtypes=(BF,))
    grads1 = [("up1", dw_up1), ("down1", dw_down1), ("out_odd", dw_out_odd)]
    dqx, dkx, dvx, outs = _flash_bwd(qx, kx, vx, ox, dox, lse, exchange_while(grads1))
    left(grads1, outs)
    dproj_o, dgq_ax, dgk_ax = _prep_odd_bwd(proj_o, cos_a, sin_a, P["ax_q_norm"], P["ax_k_norm"], dqx, dkx, dvx)
    dw_in_odd = _mm(h2, dproj_o, ta=True, name="in_odd_dw", out_dtypes=(BF,))
    d2, d2b, dnmix1 = _norm_bwd_mm(_mm, dproj_o, W["in_odd"], x2, d3, nmix[1:2], "in_odd_dx")

    d1, d1b, dnmlp0, dw_up0, dw_down0 = _mlp_bwd(x1, nmlp[0:1], W["up0"], W["down0"], mlp0, d2, d2b, "0", mm_exchange)
    doa = _mm(d1b, wo_a, tb=True, name="out_even_a_dx")
    dob = _mm(d1b, wo_b, tb=True, name="out_even_b_dx")
    dw_out_even = jnp.concatenate([_mm(oa, d1b, ta=True, name="out_even_a_dw", out_dtypes=(BF,)),
                                   _mm(ob, d1b, ta=True, name="out_even_b_dw", out_dtypes=(BF,))], axis=0)
    dy_ret, dga, dret_norm = _ret_post_bwd(y_ret, proj_e, P["ret_norm"], doa)
    hb, hf = _ret_scan(qr, dy_ret, lg, (1, 1), (0, 1), "retention_state_grads")
    dqr, dkr, dvr, dlogit = _ret_bwd(qr, kr, vr, dy_ret, lg, logit, sf, sb, hf, hb)
    grads0 = [("in_odd", dw_in_odd), ("out_even", dw_out_even)]
    dqs, dks, dvs, dbias, dsink, outs = _swa_bwd(qs, ks, vs, dob, bias, P["swa_sink"], exchange_while(grads0))
    left(grads0, outs)
    dt5 = _t5_table_grad(dbias, bucket)
    dproj_e, dgq_swa, dgk_swa = _prep_even_bwd(proj_e, cos_r, sin_r, P["swa_q_norm"], P["swa_k_norm"],
                                               dqr, dkr, dvr, dga, dqs, dks, dvs)
    dw_in_even = mm_exchange([("up0", dw_up0)], h0, dproj_e, ta=True, name="in_even_dw", out_dtypes=(BF,))
    dx, _, dnmix0 = _norm_bwd_mm(lambda *args, **kw: mm_exchange([("in_even", dw_in_even)], *args, **kw),
                                 dproj_e, W["in_even"], x, d1, nmix[0:1], "in_even_dx")

    if late is None:
        dW = dict(grads1 + grads0, up0=dw_up0, down0=dw_down0, in_even=dw_in_even)
    else:
        dW = landed
    dP = {"norm_mix": jnp.concatenate([dnmix0, dnmix1], 0), "norm_mlp": jnp.concatenate([dnmlp0, dnmlp1], 0),
          "ret_decay_logit": dlogit[:, :, 0, 0].T, "ret_norm": dret_norm,
          "swa_q_norm": dgq_swa, "swa_k_norm": dgk_swa, "swa_sink": dsink[:, 0][None, :],
          "t5_table": dt5[:, :SWA_HEADS], "ax_q_norm": dgq_ax, "ax_k_norm": dgk_ax}
    return loss_row, dx, dW, dP


def _cast_shards(shards):
    n = len(shards)

    def body(*refs):
        for i_ref, o_ref in zip(refs[:n], refs[n:]):
            o_ref[...] = i_ref[...].astype(o_ref.dtype)

    return pl.pallas_call(body, name="cast_shards", in_specs=[VMEM_SPEC] * n, out_specs=[VMEM_SPEC] * n,
                          out_shape=[jax.ShapeDtypeStruct(s.shape, BF) for s in shards],
                          compiler_params=pltpu.CompilerParams(vmem_limit_bytes=VMEM_SMALL))(*shards)


def _all_gather(shards):
    n = len(shards)

    def body(*refs):
        ins, outs, stage = refs[:n], refs[n:2 * n], refs[2 * n:3 * n]
        send_sems, recv_sems, local_sems = refs[3 * n:]
        me = _my_place()
        sibling = _flip(me, 1)
        chips = [_flip(me, 4), _flip(me, 2), _flip(me, 6)]

        def copy(a, k, block, to, src=None):
            dst = outs[a].at[_index(block)]
            return pltpu.make_async_remote_copy(
                src_ref=dst if src is None else src, dst_ref=dst,
                send_sem=send_sems.at[a, k], recv_sem=recv_sems.at[a, k], device_id=to, device_id_type=MESH)

        first, mine = [], []
        for a in range(n):
            stage[a][...] = ins[a][...].astype(stage[a].dtype)
            mine.append(pltpu.make_async_copy(stage[a], outs[a].at[_index(me)], local_sems.at[a]))
            mine[-1].start()
            first.append(copy(a, 0, me, sibling, src=stage[a]))
            first += [copy(a, 1 + j, me, chip, src=stage[a]) for j, chip in enumerate(chips)]
        for cp in first:
            cp.start()
        passed = []
        for a in range(n):
            for j, chip in enumerate(chips):
                copy(a, 1 + j, chip, me).wait_recv()
                passed.append(copy(a, 4 + j, chip, sibling))
                passed[-1].start()
        for a in range(n):
            copy(a, 0, sibling, me).wait_recv()
            for j, chip in enumerate(chips):
                copy(a, 4 + j, _flip(chip, 1), me).wait_recv()
        for cp in first + passed:
            cp.wait_send()
        for cp in mine:
            cp.wait()

    return pl.pallas_call(
        body, name="weights_all_gather",
        in_specs=[VMEM_SPEC] * n, out_specs=[ANY_SPEC] * n,
        out_shape=[jax.ShapeDtypeStruct((N_DEV,) + s.shape, BF) for s in shards],
        scratch_shapes=[pltpu.VMEM(s.shape, BF) for s in shards]
        + [pltpu.SemaphoreType.DMA((n, 7)), pltpu.SemaphoreType.DMA((n, 7)), pltpu.SemaphoreType.DMA((n,))],
        compiler_params=pltpu.CompilerParams(vmem_limit_bytes=VMEM_SMALL),
    )(*shards)


def _all_reduce_small(part):
    R, C = part.shape

    def body(x_ref, o_ref, land_ref, send_sems, recv_sems):
        me = _my_place()
        land_ref[_index(me)] = x_ref[...]
        copies = []
        for k in range(1, N_DEV):
            peer = _flip(me, k)
            copies.append(pltpu.make_async_remote_copy(
                src_ref=x_ref, dst_ref=land_ref.at[_index(me)],
                send_sem=send_sems.at[k - 1], recv_sem=recv_sems.at[k - 1], device_id=peer, device_id_type=MESH))
            copies[-1].start()
        for k in range(1, N_DEV):
            peer = _flip(me, k)
            pltpu.make_async_remote_copy(
                src_ref=x_ref, dst_ref=land_ref.at[_index(peer)],
                send_sem=send_sems.at[k - 1], recv_sem=recv_sems.at[k - 1], device_id=peer, device_id_type=MESH).wait_recv()
        for cp in copies:
            cp.wait_send()
        acc = land_ref[0]
        for s in range(1, N_DEV):
            acc = acc + land_ref[s]
        o_ref[...] = acc
        o_ref[LOSS_ROW:LOSS_ROW + 1, :] = jnp.broadcast_to(_all_sum(acc[LOSS_ROW:LOSS_ROW + 1, :]) * (0.5 / D_MODEL), (1, C))

    return pl.pallas_call(
        body, name="small_all_reduce", in_specs=[VMEM_SPEC], out_specs=VMEM_SPEC,
        out_shape=jax.ShapeDtypeStruct((R, C), F32),
        scratch_shapes=[pltpu.VMEM((N_DEV, R, C), F32), pltpu.SemaphoreType.DMA((7,)), pltpu.SemaphoreType.DMA((7,))],
    )(part)


def _adamw_math(w, g, m, v):
    m = ADAM_B1 * m + (1.0 - ADAM_B1) * g
    v = ADAM_B2 * v + (1.0 - ADAM_B2) * jnp.square(g)
    m_hat = m / (1.0 - ADAM_B1 ** ADAM_STEP)
    v_hat = v / (1.0 - ADAM_B2 ** ADAM_STEP)
    delta = -ADAM_LR * (m_hat / (jnp.sqrt(v_hat) + ADAM_EPS) + ADAM_WD * w)
    return delta, m, v


def _sum_and_adamw(landed, w, m, v, name):
    R, C = w.shape
    tr = _pick(R, (256, 128))

    def body(l_ref, w_ref, m_ref, v_ref, g_ref, d_ref, nm_ref, nv_ref):
        g = l_ref[0].astype(F32)
        for s in range(1, N_DEV):
            g = g + l_ref[s].astype(F32)
        g_ref[...] = g
        d_ref[...], nm_ref[...], nv_ref[...] = _adamw_math(w_ref[...], g, m_ref[...], v_ref[...])

    row = pl.BlockSpec((tr, C), lambda i: (i, 0))
    return pl.pallas_call(
        body, name=name, grid=(R // tr,),
        in_specs=[pl.BlockSpec((N_DEV, tr, C), lambda i: (0, i, 0)), row, row, row], out_specs=[row] * 4,
        out_shape=[jax.ShapeDtypeStruct((R, C), F32)] * 4, compiler_params=_params(("parallel",)),
    )(landed, w, m, v)


def _adamw_small(w, g, m, v):
    def body(w_ref, g_ref, m_ref, v_ref, d_ref, nm_ref, nv_ref):
        d_ref[...], nm_ref[...], nv_ref[...] = _adamw_math(w_ref[...], g_ref[...], m_ref[...], v_ref[...])

    full = pl.BlockSpec(w.shape, lambda: (0, 0))
    return pl.pallas_call(body, name="adamw_small", in_specs=[full] * 4, out_specs=[full] * 3,
                          out_shape=[jax.ShapeDtypeStruct(w.shape, F32)] * 3)(w, g, m, v)


MATRICES = ("w_in_even", "w_out_even", "w_in_odd", "w_out_odd", "w_mlp_up", "w_mlp_down")
SMALL = ("norm_mix", "norm_mlp", "ret_decay_logit", "ret_norm", "swa_q_norm", "swa_k_norm", "swa_sink",
         "t5_table", "ax_q_norm", "ax_k_norm")
MATRIX_OF = {"in_even": ("w_in_even", 0), "out_even": ("w_out_even", 0), "in_odd": ("w_in_odd", 0),
             "out_odd": ("w_out_odd", 0), "up0": ("w_mlp_up", 0), "up1": ("w_mlp_up", 1),
             "down0": ("w_mlp_down", 0), "down1": ("w_mlp_down", 1)}
GATHER_FIRST = ("in_even",)
GATHER_WHILE = {"in_even": ("up0",), "prep_even_fwd": ("out_even",), "retention_fwd": ("in_odd",), "swa_fwd": ("down0",),
                "flash_fwd": ("out_odd", "up1", "down1")}


SMALL_ROWS = 8
SMALL_AT = {"norm_mix": (0, 0), "norm_mlp": (2, 0), "ret_norm": (4, 0), "swa_q_norm": (5, 0), "swa_k_norm": (5, 128),
            "ax_q_norm": (5, 256), "ax_k_norm": (5, 384), "swa_sink": (5, 512), "ret_decay_logit": (5, 640),
            "t5_table": (6, 0)}
LOSS_ROW = 7


def _pack_small(arrays, loss_row=None):
    buf = jnp.zeros((SMALL_ROWS, 1024), F32)
    for name, (r, c) in SMALL_AT.items():
        a = arrays[name].astype(F32)
        a = a.reshape(1, -1) if name in ("ret_decay_logit", "t5_table") else a.reshape(-1, a.shape[-1])
        buf = lax.dynamic_update_slice(buf, a, (r, c))
    if loss_row is not None:
        buf = lax.dynamic_update_slice(buf, loss_row, (LOSS_ROW, 0))
    return buf


def _unpack_small(buf, like):
    out = {}
    for name, (r, c) in SMALL_AT.items():
        shape = like[name].shape
        rows = 1 if name in ("ret_decay_logit", "t5_table") else math.prod(shape[:-1])
        cols = math.prod(shape) // rows
        out[name] = buf[r:r + rows, c:c + cols].reshape(shape)
    return out


def kernel(x, norm_mix, norm_mlp, w_in_even, w_out_even, ret_decay_logit, ret_norm, swa_q_norm, swa_k_norm, swa_sink, t5_table, w_in_odd, w_out_odd, ax_q_norm, ax_k_norm, w_mlp_up, w_mlp_down, loss_target, m_norm_mix, m_norm_mlp, m_w_in_even, m_w_out_even, m_ret_decay_logit, m_ret_norm, m_swa_q_norm, m_swa_k_norm, m_swa_sink, m_t5_table, m_w_in_odd, m_w_out_odd, m_ax_q_norm, m_ax_k_norm, m_w_mlp_up, m_w_mlp_down, v_norm_mix, v_norm_mlp, v_w_in_even, v_w_out_even, v_ret_decay_logit, v_ret_norm, v_swa_q_norm, v_swa_k_norm, v_swa_sink, v_t5_table, v_w_in_odd, v_w_out_odd, v_ax_q_norm, v_ax_k_norm, v_w_mlp_up, v_w_mlp_down):
    given = dict(locals())
    weights = {n: given[n] for n in MATRICES + SMALL}
    moments_m = {n: given["m_" + n] for n in MATRICES + SMALL}
    moments_v = {n: given["v_" + n] for n in MATRICES + SMALL}

    def shard(table, key):
        arg, layer = MATRIX_OF[key]
        return table[arg][layer]

    gathered = _all_gather([shard(weights, k) for k in GATHER_FIRST])
    W = {k: _assemble(k, g) for k, g in zip(GATHER_FIRST, gathered)}
    late_keys = [k for keys in GATHER_WHILE.values() for k in keys]
    cast = dict(zip(late_keys, _cast_shards([shard(weights, k) for k in late_keys])))
    late = {stage: [(k, cast[k]) for k in keys] for stage, keys in GATHER_WHILE.items()}
    P = {"norm_mix": norm_mix, "norm_mlp": norm_mlp, "ret_decay_logit": ret_decay_logit[0], "ret_norm": ret_norm,
         "swa_q_norm": swa_q_norm, "swa_k_norm": swa_k_norm, "swa_sink": swa_sink, "t5_table": t5_table,
         "ax_q_norm": ax_q_norm, "ax_k_norm": ax_k_norm}

    loss_row, dx, landed, dP = _local_step(x[0], loss_target[0], W, P, late)

    per_key = {k: _sum_and_adamw(landed[k], shard(weights, k), shard(moments_m, k), shard(moments_v, k), "adamw_" + k)
               for k in MATRIX_OF}
    grads, deltas, new_m, new_v = {}, {}, {}, {}
    for i, out in enumerate((grads, deltas, new_m, new_v)):
        for n in MATRICES:
            out[n] = jnp.stack([per_key[k][i] for k, (arg, _) in MATRIX_OF.items() if arg == n])

    dP["ret_decay_logit"] = dP["ret_decay_logit"][None]
    total = _all_reduce_small(_pack_small(dP, loss_row))
    loss = total[LOSS_ROW, 0]
    small_d, small_m, small_v = _adamw_small(_pack_small(weights), total, _pack_small(moments_m), _pack_small(moments_v))
    like = {n: weights[n] for n in SMALL}
    for out, buf in ((grads, total), (deltas, small_d), (new_m, small_m), (new_v, small_v)):
        out.update(_unpack_small(buf, like))

    order = ("norm_mix", "norm_mlp", "w_in_even", "w_out_even", "ret_decay_logit", "ret_norm", "swa_q_norm", "swa_k_norm",
             "swa_sink", "t5_table", "w_in_odd", "w_out_odd", "ax_q_norm", "ax_k_norm", "w_mlp_up", "w_mlp_down")
    return (loss, dx[None], *[grads[n] for n in order], *[deltas[n] for n in order],
            *[new_m[n] for n in order], *[new_v[n] for n in order])
```

```python
import math

import jax
import jax.numpy as jnp
from jax import lax
from jax.experimental import pallas as pl
from jax.experimental.pallas import tpu as pltpu

F32 = jnp.float32
BF = jnp.bfloat16

D_MODEL = 1024
HEAD_DIM = 128
EPS = 1e-6
NEG_INF = -1e30
RET_HEADS, RET_DK, RET_DV = 4, 128, 256
RET_THETA = 10000.0
SWA_HEADS, SWA_KV_HEADS, WINDOW, BLOCK = 8, 2, 128, 128
T5_BUCKETS, T5_MAX_DIST = 32, 128
AX_HEADS, AX_KV_HEADS, AX_THETA, GRID_W = 8, 2, 10000.0, 64
D_FF = 4096
ATT_SCALE = HEAD_DIM ** -0.5
LN2 = math.log(2.0)
AX_SCALE = ATT_SCALE / LN2
RET_SCALE = RET_DK ** -0.5
N_DEV = 8

ADAM_LR, ADAM_B1, ADAM_B2, ADAM_EPS, ADAM_WD, ADAM_STEP = 0.001, 0.9, 0.999, 1e-08, 0.01, 10

MIB = 1024 * 1024
VMEM_SMALL = 40 * MIB
VMEM_LARGE = 56 * MIB

OFF_QA, OFF_KA, OFF_VA, OFF_GA, OFF_QB, OFF_KB, OFF_VB = 0, 512, 1024, 2048, 3072, 4096, 4352
EVEN_IN = 4608
ODD_IN = 1536

NT = (((1,), (1,)), ((), ()))
TN = (((0,), (0,)), ((), ()))
NN = (((1,), (0,)), ((), ()))


def _dot(a, b, dims=NN):
    return lax.dot_general(a, b, dims, preferred_element_type=F32)


def _params(sem=None, vmem=VMEM_SMALL):
    return pltpu.CompilerParams(dimension_semantics=sem, vmem_limit_bytes=vmem)


def _pick(n, prefs):
    for p in prefs:
        if n % p == 0:
            return p
    return n


def _row_sum(x):
    return jnp.sum(x, axis=0, keepdims=True)


def _all_sum(x):
    return jnp.sum(jnp.sum(x, axis=0, keepdims=True), axis=1, keepdims=True)


def _sigmoid(x):
    return 1.0 / (1.0 + jnp.exp(-x))


SMEM_SPEC = pl.BlockSpec(memory_space=pltpu.SMEM)
ANY_SPEC = pl.BlockSpec(memory_space=pl.ANY)
VMEM_SPEC = pl.BlockSpec(memory_space=pltpu.VMEM)
MESH = pl.DeviceIdType.MESH


def _my_place():
    return lax.axis_index("x"), lax.axis_index("y"), lax.axis_index("c")


def _flip(place, k):
    x, y, c = place
    return (1 - x if k & 4 else x, 1 - y if k & 2 else y, 1 - c if k & 1 else c)


def _index(place):
    x, y, c = place
    return 4 * x + 2 * y + c


class _Exchange:
    def __init__(self, sources, gather):
        self.sources, self.gather, self.n = list(sources), gather, len(sources)
        self.out_shape = [jax.ShapeDtypeStruct(((N_DEV,) + s.shape) if gather else s.shape, s.dtype) for s in self.sources]
        self.scratch = [pltpu.SemaphoreType.DMA((self.n, 7)), pltpu.SemaphoreType.DMA((self.n, 7)),
                        pltpu.SemaphoreType.DMA((self.n,))]

    def _source(self, ins, a, place):
        return ins[a] if self.gather else ins[a].at[_index(place)]

    def _local(self, ins, outs, sems):
        me = _my_place()
        return [pltpu.make_async_copy(self._source(ins, a, me), outs[a].at[_index(me)], sems[2].at[a]) for a in range(self.n)]

    def _remote(self, ins, outs, sems, arriving):
        send_sems, recv_sems, _ = sems
        me = _my_place()
        copies = []
        for a in range(self.n):
            for k in range(1, N_DEV):
                peer = _flip(me, k)
                copies.append(pltpu.make_async_remote_copy(
                    src_ref=self._source(ins, a, peer), dst_ref=outs[a].at[_index(peer if arriving else me)],
                    send_sem=send_sems.at[a, k - 1], recv_sem=recv_sems.at[a, k - 1], device_id=peer, device_id_type=MESH))
        return copies

    def start(self, ins, outs, sems):
        for cp in self._local(ins, outs, sems) + self._remote(ins, outs, sems, arriving=False):
            cp.start()

    def wait(self, ins, outs, sems):
        for cp in self._remote(ins, outs, sems, arriving=True):
            cp.wait_recv()
        for cp in self._remote(ins, outs, sems, arriving=False):
            cp.wait_send()
        for cp in self._local(ins, outs, sems):
            cp.wait()


def _carry_call(body, args, *, name, grid, in_specs, out_specs, out_shape, scratch_shapes=(), vmem=VMEM_SMALL,
                semantics=None, carried=None):
    if carried is None:
        outs = pl.pallas_call(body, name=name, grid=grid, in_specs=in_specs, out_specs=out_specs, out_shape=out_shape,
                              scratch_shapes=list(scratch_shapes), compiler_params=_params(semantics, vmem))(*args)
        return list(outs), []
    ni, no, ns, nc = len(in_specs), len(out_specs), len(scratch_shapes), carried.n

    def full_body(*refs):
        ins, cin = refs[:ni], refs[ni:ni + nc]
        outs, cout = refs[ni + nc:ni + nc + no], refs[ni + nc + no:ni + 2 * nc + no]
        scratch, sems = refs[ni + 2 * nc + no:ni + 2 * nc + no + ns], refs[ni + 2 * nc + no + ns:]
        ids = [pl.program_id(d) for d in range(len(grid))]
        first, last = ids[0] == 0, ids[0] == grid[0] - 1
        for d in range(1, len(grid)):
            first, last = first & (ids[d] == 0), last & (ids[d] == grid[d] - 1)

        @pl.when(first)
        def _():
            carried.start(cin, cout, sems)

        body(*ins, *outs, *scratch)

        @pl.when(last)
        def _():
            carried.wait(cin, cout, sems)

    outs = pl.pallas_call(
        full_body, name=name, grid=grid, in_specs=list(in_specs) + [ANY_SPEC] * nc,
        out_specs=list(out_specs) + [ANY_SPEC] * nc, out_shape=list(out_shape) + carried.out_shape,
        scratch_shapes=list(scratch_shapes) + carried.scratch,
        compiler_params=_params(("arbitrary",) * len(grid), vmem))(*args, *carried.sources)
    return list(outs[:no]), list(outs[no:])


def _mm(a, b, *, name, ta=False, tb=False, out_dtypes=(F32,), extras=(), rows=(), n_row_sums=0, epilogue=None,
        tm=1024, tn=1024, tk=1024, vmem=VMEM_SMALL, carried=None):
    M, K = (a.shape[1], a.shape[0]) if ta else a.shape
    N = b.shape[0] if tb else b.shape[1]
    assert K == (b.shape[1] if tb else b.shape[0])
    tm = _pick(M, (tm, 512, 256, 128))
    tn = _pick(N, (tn, 1536, 512, 384, 256, 128))
    tk = _pick(K, (2048, 2304, tk, 1536, 512, 256, 128))
    nk = K // tk
    assert n_row_sums == 0 or tn == N
    ne, nr, no = len(extras), len(rows), len(out_dtypes)
    dims = (((0 if ta else 1,), (1 if tb else 0,)), ((), ()))
    if epilogue is None:
        epilogue = lambda acc: (acc,)

    def body(a_ref, b_ref, *rest):
        extra_refs, out_refs = rest[:ne + nr], rest[ne + nr:ne + nr + no]
        sum_refs = rest[ne + nr + no:ne + nr + no + n_row_sums]
        first_tile = pl.program_id(0) == 0

        def finish(acc):
            outs = epilogue(acc, *[r[...] for r in extra_refs])
            for o_ref, o in zip(out_refs, outs[:no]):
                o_ref[...] = o.astype(o_ref.dtype)
            for s_ref, contribution in zip(sum_refs, outs[no:]):
                @pl.when(first_tile)
                def _(s_ref=s_ref, contribution=contribution):
                    s_ref[...] = contribution

                @pl.when(jnp.logical_not(first_tile))
                def _(s_ref=s_ref, contribution=contribution):
                    s_ref[...] += contribution

        part = _dot(a_ref[...], b_ref[...], dims)
        if nk == 1:
            finish(part)
        else:
            acc_ref = rest[-1]
            k = pl.program_id(2)

            @pl.when(k == 0)
            def _():
                acc_ref[...] = part

            @pl.when(k > 0)
            def _():
                acc_ref[...] += part

            @pl.when(k == nk - 1)
            def _():
                finish(acc_ref[...])

    a_spec = pl.BlockSpec((tk, tm), lambda i, j, k: (k, i)) if ta else pl.BlockSpec((tm, tk), lambda i, j, k: (i, k))
    b_spec = pl.BlockSpec((tn, tk), lambda i, j, k: (j, k)) if tb else pl.BlockSpec((tk, tn), lambda i, j, k: (k, j))
    o_spec = pl.BlockSpec((tm, tn), lambda i, j, k: (i, j))
    row_spec = pl.BlockSpec((1, tn), lambda i, j, k: (0, j))
    outs, landed = _carry_call(
        body, (a, b, *extras, *rows), name=name, grid=(M // tm, N // tn, nk),
        in_specs=[a_spec, b_spec] + [o_spec] * ne + [row_spec] * nr,
        out_specs=[o_spec] * no + [row_spec] * n_row_sums,
        out_shape=[jax.ShapeDtypeStruct((M, N), dt) for dt in out_dtypes] + [jax.ShapeDtypeStruct((1, N), F32)] * n_row_sums,
        scratch_shapes=[pltpu.VMEM((tm, tn), F32)] if nk > 1 else [],
        vmem=vmem, semantics=("arbitrary" if n_row_sums else "parallel", "parallel", "arbitrary"), carried=carried)
    outs = outs[0] if len(outs) == 1 else outs
    return outs if carried is None else (outs, landed)


def _rms_fwd(x, g, name):
    S, Dm = x.shape
    tr = _pick(S, (512,))

    def body(x_ref, g_ref, o_ref):
        xv = x_ref[...]
        r = lax.rsqrt(jnp.mean(xv * xv, axis=-1, keepdims=True) + EPS)
        o_ref[...] = (xv * r * g_ref[...]).astype(o_ref.dtype)

    row = pl.BlockSpec((tr, Dm), lambda i: (i, 0))
    return pl.pallas_call(
        body, name=name, grid=(S // tr,),
        in_specs=[row, pl.BlockSpec((1, Dm), lambda i: (0, 0))], out_specs=row,
        out_shape=jax.ShapeDtypeStruct((S, Dm), BF), compiler_params=_params(("parallel",)),
    )(x, g)


def _partner(x, half):
    if half == 64:
        return pltpu.roll(x, 64, 1)
    lane = lax.broadcasted_iota(jnp.int32, x.shape, 1)
    return jnp.where((lane % (2 * half)) < half, pltpu.roll(x, 128 - half, 1), pltpu.roll(x, half, 1))


def _rope(x, cos, sin, half):
    return x * cos + _partner(x, half) * sin


def _rope_t(dy, cos, sin, half):
    return dy * cos - _partner(dy, half) * sin


def _head_norm(x):
    r = lax.rsqrt(jnp.mean(x * x, axis=-1, keepdims=True) + EPS)
    return x * r, r


def _head_norm_bwd(dxh, xh, r):
    return r * (dxh - xh * jnp.mean(dxh * xh, axis=-1, keepdims=True))


def _cols(ref, off, width=HEAD_DIM):
    return ref[:, off:off + width]


def _prep_even_fwd(proj, cos, sin, gq, gk, carried=None):
    S = proj.shape[0]
    tr = _pick(S, (256,))

    def body(p_ref, cos_ref, sin_ref, gq_ref, gk_ref, qr_ref, kr_ref, vr_ref, qs_ref, ks_ref, vs_ref):
        cos_v, sin_v = cos_ref[...], sin_ref[...]
        for h in range(RET_HEADS):
            o = h * RET_DK
            qr_ref[:, o:o + RET_DK] = _rope(_cols(p_ref, OFF_QA + o), cos_v, sin_v, 64).astype(qr_ref.dtype)
            kr_ref[:, o:o + RET_DK] = (_rope(_cols(p_ref, OFF_KA + o), cos_v, sin_v, 64) * RET_SCALE).astype(kr_ref.dtype)
        vr_ref[...] = p_ref[:, OFF_VA:OFF_VA + 1024].astype(vr_ref.dtype)
        for h in range(SWA_HEADS):
            o = h * HEAD_DIM
            xh, _ = _head_norm(_cols(p_ref, OFF_QB + o))
            qs_ref[:, o:o + HEAD_DIM] = (xh * gq_ref[...] * ATT_SCALE).astype(qs_ref.dtype)
        for h in range(SWA_KV_HEADS):
            o = h * HEAD_DIM
            xh, _ = _head_norm(_cols(p_ref, OFF_KB + o))
            ks_ref[:, o:o + HEAD_DIM] = (xh * gk_ref[...]).astype(ks_ref.dtype)
        vs_ref[...] = p_ref[:, OFF_VB:OFF_VB + 256].astype(vs_ref.dtype)

    def row(w):
        return pl.BlockSpec((tr, w), lambda i: (i, 0))

    vec = pl.BlockSpec((1, HEAD_DIM), lambda i: (0, 0))
    widths = (512, 512, 1024, 1024, 256, 256)
    return _carry_call(
        body, (proj, cos, sin, gq, gk), name="prep_even_fwd", grid=(S // tr,),
        in_specs=[row(EVEN_IN), row(128), row(128), vec, vec],
        out_specs=[row(w) for w in widths],
        out_shape=[jax.ShapeDtypeStruct((S, w), BF) for w in widths],
        semantics=("parallel",), carried=carried)


def _prep_even_bwd(proj, cos, sin, gq, gk, dqr, dkr, dvr, dga, dqs, dks, dvs):
    S = proj.shape[0]
    tr = _pick(S, (256,))

    def body(p_ref, cos_ref, sin_ref, gq_ref, gk_ref, dqr_ref, dkr_ref, dvr_ref, dga_ref, dqs_ref, dks_ref,
             dvs_ref, dp_ref, dgq_ref, dgk_ref):
        cos_v, sin_v = cos_ref[...], sin_ref[...]
        dt = dp_ref.dtype
        for h in range(RET_HEADS):
            o = h * RET_DK
            dp_ref[:, OFF_QA + o:OFF_QA + o + RET_DK] = _rope_t(_cols(dqr_ref, o).astype(F32), cos_v, sin_v, 64).astype(dt)
            dp_ref[:, OFF_KA + o:OFF_KA + o + RET_DK] = _rope_t(_cols(dkr_ref, o).astype(F32) * RET_SCALE, cos_v, sin_v, 64).astype(dt)
        dp_ref[:, OFF_VA:OFF_VA + 1024] = dvr_ref[...].astype(dt)
        dp_ref[:, OFF_GA:OFF_GA + 1024] = dga_ref[...].astype(dt)
        dgq = jnp.zeros((1, HEAD_DIM), F32)
        for h in range(SWA_HEADS):
            o = h * HEAD_DIM
            xh, r = _head_norm(_cols(p_ref, OFF_QB + o))
            dy = _cols(dqs_ref, o).astype(F32) * ATT_SCALE
            dgq = dgq + _row_sum(dy * xh)
            dp_ref[:, OFF_QB + o:OFF_QB + o + HEAD_DIM] = _head_norm_bwd(dy * gq_ref[...], xh, r).astype(dt)
        dgk = jnp.zeros((1, HEAD_DIM), F32)
        for h in range(SWA_KV_HEADS):
            o = h * HEAD_DIM
            xh, r = _head_norm(_cols(p_ref, OFF_KB + o))
            dy = _cols(dks_ref, o)
            dgk = dgk + _row_sum(dy * xh)
            dp_ref[:, OFF_KB + o:OFF_KB + o + HEAD_DIM] = _head_norm_bwd(dy * gk_ref[...], xh, r).astype(dt)
        dp_ref[:, OFF_VB:OFF_VB + 256] = dvs_ref[...].astype(dt)

        @pl.when(pl.program_id(0) == 0)
        def _():
            dgq_ref[...] = jnp.zeros_like(dgq_ref)
            dgk_ref[...] = jnp.zeros_like(dgk_ref)

        dgq_ref[...] += dgq
        dgk_ref[...] += dgk

    def row(w):
        return pl.BlockSpec((tr, w), lambda i: (i, 0))

    vec = pl.BlockSpec((1, HEAD_DIM), lambda i: (0, 0))
    return pl.pallas_call(
        body, name="prep_even_bwd", grid=(S // tr,),
        in_specs=[row(EVEN_IN), row(128), row(128), vec, vec, row(512), row(512), row(1024), row(1024),
                  row(1024), row(256), row(256)],
        out_specs=[row(EVEN_IN), vec, vec],
        out_shape=[jax.ShapeDtypeStruct((S, EVEN_IN), BF), jax.ShapeDtypeStruct((1, HEAD_DIM), F32),
                   jax.ShapeDtypeStruct((1, HEAD_DIM), F32)],
        compiler_params=_params(("arbitrary",)),
    )(proj, cos, sin, gq, gk, dqr, dkr, dvr, dga, dqs, dks, dvs)


RET_CHUNK = 512
def _log_sigmoid_tile(logit_tile):
    def body(x_ref, o_ref):
        xv = x_ref[...]
        t = jnp.exp(-jnp.abs(xv))
        log1p_t = jnp.where(t < 1e-3, t * (1.0 - 0.5 * t), jnp.log(1.0 + t))
        o_ref[...] = jnp.minimum(xv, 0.0) - log1p_t

    full = pl.BlockSpec((8, 128), lambda: (0, 0))
    return pl.pallas_call(body, name="log_sigmoid", in_specs=[full], out_specs=full,
                          out_shape=jax.ShapeDtypeStruct((8, 128), F32))(logit_tile)


def _decay(diff, lf, lb):
    return jnp.exp(jnp.where(diff >= 0, lf * diff, -(lb * diff)))


def _col_iota(n):
    return lax.broadcasted_iota(jnp.int32, (n, 1), 0).astype(F32)


def _ret_scan(x, z, lg, asc, desc, name):
    S = x.shape[0]
    C = _pick(S, (RET_CHUNK,))
    nc = S // C
    (arow, aoff), (drow, doff) = asc, desc

    def body(lg_ref, xa_ref, za_ref, xd_ref, zd_ref, asc_ref, desc_ref, sa_ref, sd_ref):
        t = pl.program_id(0)

        @pl.when(t == 0)
        def _():
            sa_ref[...] = jnp.zeros_like(sa_ref)
            sd_ref[...] = jnp.zeros_like(sd_ref)

        j = _col_iota(C)
        for h in range(RET_HEADS):
            la, ld = lg_ref[arow, h], lg_ref[drow, h]
            kc, vc = slice(h * RET_DK, (h + 1) * RET_DK), slice(h * RET_DV, (h + 1) * RET_DV)
            asc_ref[h, 0] = sa_ref[h]
            desc_ref[h, 0] = sd_ref[h]
            xa = (xa_ref[:, kc].astype(F32) * jnp.exp(la * (C - 1 + aoff - j))).astype(xa_ref.dtype)
            xd = (xd_ref[:, kc].astype(F32) * jnp.exp(ld * (j + doff))).astype(xd_ref.dtype)
            sa_ref[h] = jnp.exp(jnp.full((1, RET_DV), la * C, F32)) * sa_ref[h] + _dot(xa, za_ref[:, vc], TN)
            sd_ref[h] = jnp.exp(jnp.full((1, RET_DV), ld * C, F32)) * sd_ref[h] + _dot(xd, zd_ref[:, vc], TN)

    state = jax.ShapeDtypeStruct((RET_HEADS, nc, RET_DK, RET_DV), F32)
    qk_w, v_w = RET_HEADS * RET_DK, RET_HEADS * RET_DV
    return pl.pallas_call(
        body, name=name, grid=(nc,),
        in_specs=[SMEM_SPEC,
                  pl.BlockSpec((C, qk_w), lambda t: (t, 0)), pl.BlockSpec((C, v_w), lambda t: (t, 0)),
                  pl.BlockSpec((C, qk_w), lambda t: (nc - 1 - t, 0)), pl.BlockSpec((C, v_w), lambda t: (nc - 1 - t, 0))],
        out_specs=[pl.BlockSpec((RET_HEADS, 1, RET_DK, RET_DV), lambda t: (0, t, 0, 0)),
                   pl.BlockSpec((RET_HEADS, 1, RET_DK, RET_DV), lambda t: (0, nc - 1 - t, 0, 0))],
        out_shape=[state, state],
        scratch_shapes=[pltpu.VMEM((RET_HEADS, RET_DK, RET_DV), F32), pltpu.VMEM((RET_HEADS, RET_DK, RET_DV), F32)],
        compiler_params=_params(("arbitrary",)),
    )(lg, x, z, x, z)


def _head_cols(h):
    return slice(h * RET_DK, (h + 1) * RET_DK), slice(h * RET_DV, (h + 1) * RET_DV)


def _ret_fwd(q, k, v, lg, sf, sb, proj, gn, carried=None):
    S = q.shape[0]
    C = _pick(S, (RET_CHUNK,))

    def body(lg_ref, q_ref, k_ref, v_ref, sf_ref, sb_ref, gate_ref, gn_ref, y_ref, o_ref):
        diff = (lax.broadcasted_iota(jnp.int32, (C, C), 0) - lax.broadcasted_iota(jnp.int32, (C, C), 1)).astype(F32)
        r = _col_iota(C)
        for h in range(RET_HEADS):
            lf, lb = lg_ref[0, h], lg_ref[1, h]
            kc, vc = _head_cols(h)
            qv = q_ref[:, kc]
            dt = qv.dtype
            y = _dot((_dot(qv, k_ref[:, kc], NT) * _decay(diff, lf, lb)).astype(dt), v_ref[:, vc])
            qf = qv.astype(F32)
            y = y + _dot((qf * jnp.exp(lf * (r + 1.0))).astype(dt), sf_ref[h, 0].astype(dt))
            y = y + _dot((qf * jnp.exp(lb * (C - r))).astype(dt), sb_ref[h, 0].astype(dt))
            y_ref[:, vc] = y
            yh, _ = _head_norm(y)
            gate = gate_ref[:, vc]
            o_ref[:, vc] = (gate * _sigmoid(gate) * (yh * gn_ref[:, vc])).astype(o_ref.dtype)

    state = pl.BlockSpec((RET_HEADS, 1, RET_DK, RET_DV), lambda c: (0, c, 0, 0))
    qk = pl.BlockSpec((C, RET_HEADS * RET_DK), lambda c: (c, 0))
    wide = pl.BlockSpec((C, 1024), lambda c: (c, 0))
    (y, o), landed = _carry_call(
        body, (lg, q, k, v, sf, sb, proj, gn), name="retention_fwd", grid=(S // C,),
        in_specs=[SMEM_SPEC, qk, qk, wide, state, state, pl.BlockSpec((C, 1024), lambda c: (c, OFF_GA // 1024)),
                  pl.BlockSpec((1, 1024), lambda c: (0, 0))],
        out_specs=[wide, wide],
        out_shape=[jax.ShapeDtypeStruct((S, 1024), F32), jax.ShapeDtypeStruct((S, 1024), BF)],
        semantics=("parallel",), carried=carried)
    return y, o, landed


def _ret_bwd(q, k, v, dy, lg, logit, sf, sb, hf, hb):
    S = q.shape[0]
    C = _pick(S, (RET_CHUNK,))
    nc = S // C

    def body(lg_ref, logit_ref, q_ref, k_ref, v_ref, dy_ref, sf_ref, sb_ref, hf_ref, hb_ref,
             dq_ref, dk_ref, dv_ref, dlg_ref, acc_ref):
        c = pl.program_id(0)

        @pl.when(c == 0)
        def _():
            acc_ref[...] = jnp.zeros_like(acc_ref)

        diff = (lax.broadcasted_iota(jnp.int32, (C, C), 0) - lax.broadcasted_iota(jnp.int32, (C, C), 1)).astype(F32)
        r = _col_iota(C)
        for h in range(RET_HEADS):
            one_head(h, diff, r, lg_ref, q_ref, k_ref, v_ref, dy_ref, sf_ref, sb_ref, hf_ref, hb_ref,
                     dq_ref, dk_ref, dv_ref, acc_ref)

        @pl.when(c == nc - 1)
        def _():
            for h in range(RET_HEADS):
                for d in range(2):
                    gate = 1.0 / (1.0 + jnp.exp(jnp.full((8, 128), logit_ref[d, h], F32)))
                    dlg_ref[h, d] = acc_ref[h, d] * gate

    def one_head(h, diff, r, lg_ref, q_ref, k_ref, v_ref, dy_ref, sf_ref, sb_ref, hf_ref, hb_ref,
                 dq_ref, dk_ref, dv_ref, acc_ref):
        lf, lb = lg_ref[0, h], lg_ref[1, h]
        kc, vc = _head_cols(h)
        qv, kv, vv, dyv = q_ref[:, kc], k_ref[:, kc], v_ref[:, vc], dy_ref[:, vc]
        dt = qv.dtype
        qf, kf = qv.astype(F32), kv.astype(F32)
        dec = _decay(diff, lf, lb)
        sc = _dot(qv, kv, NT) * dec
        dp = _dot(dyv, vv, NT)
        da = (dp * dec).astype(dt)
        dq = _dot(da, kv)
        dk = _dot(da, qv, TN)
        dv = _dot(sc.astype(dt), dyv, TN)
        w = sc * dp * diff
        tot_w, tot_f = _all_sum(w), _all_sum(jnp.where(diff >= 0, w, 0.0))
        d_lf, d_lb = tot_f, tot_f - tot_w
        a, b = jnp.exp(lf * (r + 1.0)), jnp.exp(lb * (C - r))
        e, f = jnp.exp(lf * (C - 1.0 - r)), jnp.exp(lb * r)
        sfv, sbv, hfv, hbv = sf_ref[h, 0], sb_ref[h, 0], hf_ref[h, 0], hb_ref[h, 0]
        t_f, t_b = _dot(dyv, sfv.astype(dt), NT), _dot(dyv, sbv.astype(dt), NT)
        u_f, u_b = _dot(vv, hfv.astype(dt), NT), _dot(vv, hbv.astype(dt), NT)
        dq_ref[:, kc] = (dq + a * t_f + b * t_b).astype(dq_ref.dtype)
        dk_ref[:, kc] = (dk + e * u_f + f * u_b).astype(dk_ref.dtype)
        dv_ref[:, vc] = (dv + _dot((kf * e).astype(dt), hfv.astype(dt)) + _dot((kf * f).astype(dt), hbv.astype(dt))).astype(dv_ref.dtype)
        row_q_f = jnp.sum(qf * t_f, axis=-1, keepdims=True)
        row_q_b = jnp.sum(qf * t_b, axis=-1, keepdims=True)
        row_k_f = jnp.sum(kf * u_f, axis=-1, keepdims=True)
        row_k_b = jnp.sum(kf * u_b, axis=-1, keepdims=True)
        gf_c = jnp.exp(jnp.full((1, 1), lf * C, F32))
        gb_c = jnp.exp(jnp.full((1, 1), lb * C, F32))
        d_lf = d_lf + _all_sum((r + 1.0) * a * row_q_f + (C - 1.0 - r) * e * row_k_f) + C * gf_c * _all_sum(hfv * sfv)
        d_lb = d_lb + _all_sum((C - r) * b * row_q_b + r * f * row_k_b) + C * gb_c * _all_sum(hbv * sbv)
        acc_ref[h, 0] += jnp.broadcast_to(d_lf, (8, 128))
        acc_ref[h, 1] += jnp.broadcast_to(d_lb, (8, 128))

    state = pl.BlockSpec((RET_HEADS, 1, RET_DK, RET_DV), lambda c: (0, c, 0, 0))
    qk = pl.BlockSpec((C, RET_HEADS * RET_DK), lambda c: (c, 0))
    vy = pl.BlockSpec((C, RET_HEADS * RET_DV), lambda c: (c, 0))
    return pl.pallas_call(
        body, name="retention_bwd", grid=(nc,),
        in_specs=[SMEM_SPEC, SMEM_SPEC, qk, qk, vy, vy, state, state, state, state],
        out_specs=[qk, qk, vy, pl.BlockSpec((RET_HEADS, 2, 8, 128), lambda c: (0, 0, 0, 0))],
        out_shape=[jax.ShapeDtypeStruct((S, RET_HEADS * RET_DK), BF), jax.ShapeDtypeStruct((S, RET_HEADS * RET_DK), BF),
                   jax.ShapeDtypeStruct((S, RET_HEADS * RET_DV), BF),
                   jax.ShapeDtypeStruct((RET_HEADS, 2, 8, 128), F32)],
        scratch_shapes=[pltpu.VMEM((RET_HEADS, 2, 8, 128), F32)],
        compiler_params=_params(("arbitrary",)),
    )(lg, logit, q, k, v, dy, sf, sb, hf, hb)


def _ret_post_bwd(y, proj, gn, do):
    S = y.shape[0]
    tr = _pick(S, (512,))

    def body(y_ref, g_ref, gn_ref, do_ref, dy_ref, dg_ref, dgn_ref):
        @pl.when(pl.program_id(0) == 0)
        def _():
            dgn_ref[...] = jnp.zeros_like(dgn_ref)

        for h in range(RET_HEADS):
            o = h * RET_DV
            yh, r = _head_norm(_cols(y_ref, o, RET_DV))
            gate = _cols(g_ref, o, RET_DV)
            gnh = gn_ref[:, o:o + RET_DV]
            dout = _cols(do_ref, o, RET_DV).astype(F32)
            sg = _sigmoid(gate)
            dz = dout * (gate * sg)
            dg_ref[:, o:o + RET_DV] = (dout * (yh * gnh) * (sg * (1.0 + gate * (1.0 - sg)))).astype(dg_ref.dtype)
            dgn_ref[:, o:o + RET_DV] += _row_sum(dz * yh)
            dy_ref[:, o:o + RET_DV] = _head_norm_bwd(dz * gnh, yh, r).astype(dy_ref.dtype)

    row = pl.BlockSpec((tr, 1024), lambda i: (i, 0))
    vec = pl.BlockSpec((1, 1024), lambda i: (0, 0))
    return pl.pallas_call(
        body, name="retention_post_bwd", grid=(S // tr,),
        in_specs=[row, pl.BlockSpec((tr, 1024), lambda i: (i, OFF_GA // 1024)), vec, row],
        out_specs=[row, row, vec],
        out_shape=[jax.ShapeDtypeStruct((S, 1024), BF), jax.ShapeDtypeStruct((S, 1024), BF),
                   jax.ShapeDtypeStruct((1, 1024), F32)],
        compiler_params=_params(("arbitrary",)),
    )(y, proj, gn, do)


def _t5_bucket_map():
    r = jnp.arange(BLOCK)
    j = jnp.arange(3 * BLOCK)
    rel = j[None, :] - BLOCK - r[:, None]
    nb = T5_BUCKETS // 2
    max_exact = nb // 2
    ret = jnp.where(rel > 0, nb, 0)
    n = jnp.abs(rel)
    nf = jnp.maximum(n, 1).astype(jnp.float32)
    large = max_exact + (jnp.log(nf / max_exact) / math.log(T5_MAX_DIST / max_exact)
                         * (nb - max_exact)).astype(jnp.int32)
    large = jnp.minimum(large, nb - 1)
    bucket = ret + jnp.where(n < max_exact, n, large)
    return jnp.where(jnp.abs(rel) <= WINDOW, bucket, -1).astype(jnp.int32)


SWA_G = SWA_HEADS // SWA_KV_HEADS
SWA_LANES = SWA_G * BLOCK


def _t5_bias(table, bucket_t):
    def body(t_ref, b_ref, o_ref):
        bk = b_ref[...]
        for h in range(SWA_HEADS):
            acc = jnp.full(bk.shape, NEG_INF, F32)
            for b in range(T5_BUCKETS):
                acc = jnp.where(bk == b, t_ref[b, h], acc)
            o_ref[h // SWA_G, :, (h % SWA_G) * BLOCK:(h % SWA_G + 1) * BLOCK] = acc

    return pl.pallas_call(
        body, name="t5_bias", in_specs=[SMEM_SPEC, pl.BlockSpec((3 * BLOCK, BLOCK), lambda: (0, 0))],
        out_specs=pl.BlockSpec((SWA_KV_HEADS, 3 * BLOCK, SWA_LANES), lambda: (0, 0, 0)),
        out_shape=jax.ShapeDtypeStruct((SWA_KV_HEADS, 3 * BLOCK, SWA_LANES), F32),
    )(table, bucket_t)


def _t5_table_grad(dbias, bucket_t):
    def body(d_ref, b_ref, o_ref):
        bk = b_ref[...]
        lane = lax.broadcasted_iota(jnp.int32, (1, 128), 1)
        for b in range(T5_BUCKETS):
            hit = bk == b
            row = jnp.zeros((1, 128), F32)
            for h in range(SWA_HEADS):
                d = d_ref[h // SWA_G, :, (h % SWA_G) * BLOCK:(h % SWA_G + 1) * BLOCK]
                row = row + jnp.where(lane == h, _all_sum(jnp.where(hit, d, 0.0)), 0.0)
            o_ref[b:b + 1, :] = row

    return pl.pallas_call(
        body, name="t5_table_grad",
        in_specs=[pl.BlockSpec((SWA_KV_HEADS, 3 * BLOCK, SWA_LANES), lambda: (0, 0, 0)),
                  pl.BlockSpec((3 * BLOCK, BLOCK), lambda: (0, 0))],
        out_specs=pl.BlockSpec((T5_BUCKETS, 128), lambda: (0, 0)),
        out_shape=jax.ShapeDtypeStruct((T5_BUCKETS, 128), F32),
    )(dbias, bucket_t)


def _swa_scores(i, nb, q4, kw, bias_t, sink_row):
    s = _dot(kw, q4, NT) + bias_t
    row = lax.broadcasted_iota(jnp.int32, s.shape, 0)
    first_row = jnp.where(i == 0, BLOCK, 0)
    end_row = jnp.where(i == nb - 1, 2 * BLOCK, 3 * BLOCK)
    s = jnp.where((row < first_row) | (row >= end_row), NEG_INF, s)
    m = jnp.maximum(jnp.max(s, axis=0, keepdims=True), sink_row)
    p = jnp.exp(s - m)
    e_sink = jnp.exp(sink_row - m)
    inv = 1.0 / (jnp.sum(p, axis=0, keepdims=True) + e_sink)
    return p * inv, e_sink * inv


def _swa_group(q_ref, sink_ref, kh):
    heads = range(kh * SWA_G, (kh + 1) * SWA_G)
    q4 = jnp.concatenate([_cols(q_ref, h * HEAD_DIM) for h in heads], axis=0)
    sink_row = jnp.concatenate([jnp.full((1, BLOCK), sink_ref[0, h], F32) for h in heads], axis=1)
    return q4, sink_row


def _swa_unstack(ref, kh, x_t):
    for g in range(SWA_G):
        h = kh * SWA_G + g
        ref[:, h * HEAD_DIM:(h + 1) * HEAD_DIM] = x_t[:, g * BLOCK:(g + 1) * BLOCK].T.astype(ref.dtype)


def _swa_window(ref, i, nb, off):
    prev, nxt = jnp.maximum(i - 1, 0), jnp.minimum(i + 1, nb - 1)
    rows = [pl.ds(pl.multiple_of(b * BLOCK, BLOCK), BLOCK) for b in (prev, i, nxt)]
    return jnp.concatenate([ref[r, off:off + HEAD_DIM] for r in rows], axis=0), rows


def _swa_fwd(q, k, v, bias, sink, carried=None):
    S = q.shape[0]
    nb = S // BLOCK

    def body(sink_ref, q_ref, k_ref, v_ref, bias_ref, o_ref):
        i = pl.program_id(0)
        for kh in range(SWA_KV_HEADS):
            kw, _ = _swa_window(k_ref, i, nb, kh * HEAD_DIM)
            vw, _ = _swa_window(v_ref, i, nb, kh * HEAD_DIM)
            q4, sink_row = _swa_group(q_ref, sink_ref, kh)
            p, _ = _swa_scores(i, nb, q4, kw, bias_ref[kh], sink_row)
            _swa_unstack(o_ref, kh, _dot(vw, p.astype(vw.dtype), TN))

    full_kv = pl.BlockSpec((S, SWA_KV_HEADS * HEAD_DIM), lambda i: (0, 0))
    (o,), landed = _carry_call(
        body, (sink, q, k, v, bias), name="swa_fwd", grid=(nb,),
        in_specs=[SMEM_SPEC, pl.BlockSpec((BLOCK, 1024), lambda i: (i, 0)), full_kv, full_kv,
                  pl.BlockSpec((SWA_KV_HEADS, 3 * BLOCK, SWA_LANES), lambda i: (0, 0, 0))],
        out_specs=[pl.BlockSpec((BLOCK, 1024), lambda i: (i, 0))],
        out_shape=[jax.ShapeDtypeStruct((S, 1024), BF)], semantics=("parallel",), carried=carried)
    return o, landed


def _swa_bwd(q, k, v, do, bias, sink, carried=None):
    S = q.shape[0]
    nb = S // BLOCK

    def body(sink_ref, q_ref, k_ref, v_ref, do_ref, bias_ref, dq_ref, dk_ref, dv_ref, dbias_ref, dsink_ref):
        i = pl.program_id(0)

        @pl.when(i == 0)
        def _():
            dk_ref[...] = jnp.zeros_like(dk_ref)
            dv_ref[...] = jnp.zeros_like(dv_ref)
            dbias_ref[...] = jnp.zeros_like(dbias_ref)
            dsink_ref[...] = jnp.zeros_like(dsink_ref)

        for kh in range(SWA_KV_HEADS):
            off = kh * HEAD_DIM
            kw, rows = _swa_window(k_ref, i, nb, off)
            vw, _ = _swa_window(v_ref, i, nb, off)
            q4, sink_row = _swa_group(q_ref, sink_ref, kh)
            p, p_sink = _swa_scores(i, nb, q4, kw, bias_ref[kh], sink_row)
            do4 = jnp.concatenate([_cols(do_ref, (kh * SWA_G + g) * HEAD_DIM) for g in range(SWA_G)], axis=0).astype(vw.dtype)
            dp = _dot(vw, do4, NT)
            delta = jnp.sum(p * dp, axis=0, keepdims=True)
            ds = p * (dp - delta)
            dsb = ds.astype(q4.dtype)
            _swa_unstack(dq_ref, kh, _dot(kw, dsb, TN))
            dkw = _dot(dsb, q4)
            dvw = _dot(p.astype(do4.dtype), do4)
            dbias_ref[kh] += ds
            sink_term = p_sink * delta
            for g in range(SWA_G):
                h = kh * SWA_G + g
                dsink_ref[h:h + 1, :] += jnp.broadcast_to(-_all_sum(sink_term[:, g * BLOCK:(g + 1) * BLOCK]), (1, 128))
            for b, r in enumerate(rows):
                dk_ref[r, off:off + HEAD_DIM] += dkw[b * BLOCK:(b + 1) * BLOCK]
                dv_ref[r, off:off + HEAD_DIM] += dvw[b * BLOCK:(b + 1) * BLOCK]

    full_kv = pl.BlockSpec((S, SWA_KV_HEADS * HEAD_DIM), lambda i: (0, 0))
    blk = pl.BlockSpec((BLOCK, 1024), lambda i: (i, 0))
    bias_spec = pl.BlockSpec((SWA_KV_HEADS, 3 * BLOCK, SWA_LANES), lambda i: (0, 0, 0))
    outs, landed = _carry_call(
        body, (sink, q, k, v, do, bias), name="swa_bwd", grid=(nb,),
        in_specs=[SMEM_SPEC, blk, full_kv, full_kv, blk, bias_spec],
        out_specs=[blk, full_kv, full_kv, bias_spec, pl.BlockSpec((8, 128), lambda i: (0, 0))],
        out_shape=[jax.ShapeDtypeStruct((S, 1024), BF), jax.ShapeDtypeStruct((S, 256), F32),
                   jax.ShapeDtypeStruct((S, 256), F32),
                   jax.ShapeDtypeStruct((SWA_KV_HEADS, 3 * BLOCK, SWA_LANES), F32), jax.ShapeDtypeStruct((8, 128), F32)],
        vmem=VMEM_LARGE, semantics=("arbitrary",), carried=carried)
    return (*outs, landed)


def _prep_odd_fwd(proj, cos, sin, gq, gk):
    S = proj.shape[0]
    tr = _pick(S, (512,))

    def body(p_ref, cos_ref, sin_ref, gq_ref, gk_ref, q_ref, k_ref, v_ref):
        cos_v, sin_v = cos_ref[...], sin_ref[...]
        for h in range(AX_HEADS):
            o = h * HEAD_DIM
            xh, _ = _head_norm(_cols(p_ref, o))
            q_ref[:, o:o + HEAD_DIM] = (_rope(xh * gq_ref[...], cos_v, sin_v, 32) * AX_SCALE).astype(q_ref.dtype)
        for h in range(AX_KV_HEADS):
            o = h * HEAD_DIM
            xh, _ = _head_norm(_cols(p_ref, 1024 + o))
            k_ref[:, o:o + HEAD_DIM] = _rope(xh * gk_ref[...], cos_v, sin_v, 32).astype(k_ref.dtype)
        v_ref[...] = p_ref[:, 1280:1536].astype(v_ref.dtype)

    def row(w):
        return pl.BlockSpec((tr, w), lambda i: (i, 0))

    vec = pl.BlockSpec((1, HEAD_DIM), lambda i: (0, 0))
    return pl.pallas_call(
        body, name="prep_odd_fwd", grid=(S // tr,),
        in_specs=[row(ODD_IN), row(128), row(128), vec, vec], out_specs=[row(1024), row(256), row(256)],
        out_shape=[jax.ShapeDtypeStruct((S, w), BF) for w in (1024, 256, 256)],
        compiler_params=_params(("parallel",)),
    )(proj, cos, sin, gq, gk)


def _prep_odd_bwd(proj, cos, sin, gq, gk, dq, dk, dv):
    S = proj.shape[0]
    tr = _pick(S, (512,))

    def body(p_ref, cos_ref, sin_ref, gq_ref, gk_ref, dq_ref, dk_ref, dv_ref, dp_ref, dgq_ref, dgk_ref):
        cos_v, sin_v = cos_ref[...], sin_ref[...]
        dt = dp_ref.dtype
        dgq = jnp.zeros((1, HEAD_DIM), F32)
        for h in range(AX_HEADS):
            o = h * HEAD_DIM
            xh, r = _head_norm(_cols(p_ref, o))
            dy = _rope_t(_cols(dq_ref, o).astype(F32) * AX_SCALE, cos_v, sin_v, 32)
            dgq = dgq + _row_sum(dy * xh)
            dp_ref[:, o:o + HEAD_DIM] = _head_norm_bwd(dy * gq_ref[...], xh, r).astype(dt)
        dgk = jnp.zeros((1, HEAD_DIM), F32)
        for h in range(AX_KV_HEADS):
            o = h * HEAD_DIM
            xh, r = _head_norm(_cols(p_ref, 1024 + o))
            dy = _rope_t(_cols(dk_ref, o), cos_v, sin_v, 32)
            dgk = dgk + _row_sum(dy * xh)
            dp_ref[:, 1024 + o:1024 + o + HEAD_DIM] = _head_norm_bwd(dy * gk_ref[...], xh, r).astype(dt)
        dp_ref[:, 1280:1536] = dv_ref[...].astype(dt)

        @pl.when(pl.program_id(0) == 0)
        def _():
            dgq_ref[...] = jnp.zeros_like(dgq_ref)
            dgk_ref[...] = jnp.zeros_like(dgk_ref)

        dgq_ref[...] += dgq
        dgk_ref[...] += dgk

    def row(w):
        return pl.BlockSpec((tr, w), lambda i: (i, 0))

    vec = pl.BlockSpec((1, HEAD_DIM), lambda i: (0, 0))
    return pl.pallas_call(
        body, name="prep_odd_bwd", grid=(S // tr,),
        in_specs=[row(ODD_IN), row(128), row(128), vec, vec, row(1024), row(256), row(256)],
        out_specs=[row(ODD_IN), vec, vec],
        out_shape=[jax.ShapeDtypeStruct((S, ODD_IN), BF), jax.ShapeDtypeStruct((1, HEAD_DIM), F32),
                   jax.ShapeDtypeStruct((1, HEAD_DIM), F32)],
        compiler_params=_params(("arbitrary",)),
    )(proj, cos, sin, gq, gk, dq, dk, dv)


def _loop_unrolled(n, factor, step, init):
    while n % factor:
        factor //= 2

    def trip(t, carry):
        for u in range(factor):
            carry = step(factor * t + u, carry)
        return carry

    return lax.fori_loop(0, n // factor, trip, init)


def _flash_fwd(q, k, v, carried=None):
    S = q.shape[0]
    tq = _pick(S, (512,))
    tk = _pick(S, (1024, 512))
    nk = S // tk
    G = AX_HEADS // AX_KV_HEADS

    def body(q_ref, k_ref, v_ref, o_ref, lse_ref):
        qv = q_ref[...]

        def step(j, carry):
            m, l, acc = carry
            rows = pl.ds(pl.multiple_of(j * tk, tk), tk)
            s = _dot(qv, k_ref[rows, :], NT)
            m_new = jnp.maximum(m, jnp.max(s, axis=-1, keepdims=True))
            alpha = jnp.exp2(m - m_new)
            p = jnp.exp2(s - m_new)
            l = alpha * l + jnp.sum(p, axis=-1, keepdims=True)
            acc = alpha * acc + _dot(p.astype(v_ref.dtype), v_ref[rows, :])
            return m_new, l, acc

        init = (jnp.full((tq, 1), NEG_INF, F32), jnp.zeros((tq, 1), F32), jnp.zeros((tq, HEAD_DIM), F32))
        m, l, acc = _loop_unrolled(nk, 8, step, init)
        o_ref[...] = (acc / l).astype(o_ref.dtype)
        lse_ref[0] = jnp.broadcast_to(m + jnp.log2(l), (tq, 128))

    (o, lse), landed = _carry_call(
        body, (q, k, v), name="flash_fwd", grid=(AX_HEADS, S // tq),
        in_specs=[pl.BlockSpec((tq, HEAD_DIM), lambda h, i: (i, h)),
                  pl.BlockSpec((S, HEAD_DIM), lambda h, i: (0, h // G)),
                  pl.BlockSpec((S, HEAD_DIM), lambda h, i: (0, h // G))],
        out_specs=[pl.BlockSpec((tq, HEAD_DIM), lambda h, i: (i, h)),
                   pl.BlockSpec((1, tq, 128), lambda h, i: (h, i, 0))],
        out_shape=[jax.ShapeDtypeStruct((S, AX_HEADS * HEAD_DIM), BF), jax.ShapeDtypeStruct((AX_HEADS, S, 128), F32)],
        semantics=("parallel", "parallel"), carried=carried)
    return o, lse, landed


def _flash_bwd(q, k, v, o, do, lse, carried=None):
    S = q.shape[0]
    tq = _pick(S, (512,))
    tk = _pick(S, (1024, 512))
    nq, nk = S // tq, S // tk
    G = AX_HEADS // AX_KV_HEADS

    def body(q_ref, k_ref, v_ref, o_ref, do_ref, lse_ref, dq_ref, dk_ref, dv_ref):
        g, i = pl.program_id(1), pl.program_id(2)

        @pl.when((g == 0) & (i == 0))
        def _():
            dk_ref[...] = jnp.zeros_like(dk_ref)
            dv_ref[...] = jnp.zeros_like(dv_ref)

        qv = q_ref[...]
        do_f = do_ref[...].astype(F32)
        dob = do_f.astype(qv.dtype)
        dob_ln2 = (do_f * LN2).astype(qv.dtype)
        delta = jnp.sum(do_f * o_ref[...].astype(F32), axis=-1, keepdims=True) * LN2
        lse_col = lse_ref[0][:, 0:1]

        def step(j, dq):
            rows = pl.ds(pl.multiple_of(j * tk, tk), tk)
            kj, vj = k_ref[rows, :], v_ref[rows, :]
            p = jnp.exp2(_dot(qv, kj, NT) - lse_col)
            dp = _dot(dob_ln2, vj, NT)
            ds = (p * (dp - delta)).astype(qv.dtype)
            dk_ref[rows, :] += _dot(ds, qv, TN)
            dv_ref[rows, :] += _dot(p.astype(dob.dtype), dob, TN)
            return dq + _dot(ds, kj)

        dq_ref[...] = _loop_unrolled(nk, 4, step, jnp.zeros((tq, HEAD_DIM), F32)).astype(dq_ref.dtype)

    q_spec = pl.BlockSpec((tq, HEAD_DIM), lambda kh, g, i: (i, kh * G + g))
    kv_spec = pl.BlockSpec((S, HEAD_DIM), lambda kh, g, i: (0, kh))
    (dq, dk, dv), landed = _carry_call(
        body, (q, k, v, o, do, lse), name="flash_bwd", grid=(AX_KV_HEADS, G, nq),
        in_specs=[q_spec, kv_spec, kv_spec, q_spec, q_spec,
                  pl.BlockSpec((1, tq, 128), lambda kh, g, i: (kh * G + g, i, 0))],
        out_specs=[q_spec, kv_spec, kv_spec],
        out_shape=[jax.ShapeDtypeStruct((S, AX_HEADS * HEAD_DIM), BF), jax.ShapeDtypeStruct((S, 256), F32),
                   jax.ShapeDtypeStruct((S, 256), F32)],
        vmem=VMEM_LARGE, semantics=("arbitrary", "arbitrary", "arbitrary"), carried=carried)
    return dq, dk, dv, landed


def _rope_angles(pos, dim, theta):
    inv = theta ** (-jnp.arange(0, dim, 2, dtype=jnp.float32) / dim)
    return pos.astype(jnp.float32)[:, None] * inv[None, :]


def _rope_tables(S):
    ang = _rope_angles(jnp.arange(S), RET_DK, RET_THETA)
    c, s = jnp.cos(ang), jnp.sin(ang)
    ret = (jnp.concatenate([c, c], -1), jnp.concatenate([-s, s], -1))
    rows = S // GRID_W
    ar, ac = _rope_angles(jnp.arange(rows), HEAD_DIM // 2, AX_THETA), _rope_angles(jnp.arange(GRID_W), HEAD_DIM // 2, AX_THETA)
    cr, sr = jnp.repeat(jnp.cos(ar), GRID_W, axis=0), jnp.repeat(jnp.sin(ar), GRID_W, axis=0)
    cc, sc = jnp.tile(jnp.cos(ac), (rows, 1)), jnp.tile(jnp.sin(ac), (rows, 1))
    ax = (jnp.concatenate([cr, cr, cc, cc], -1), jnp.concatenate([-sr, sr, -sc, sc], -1))
    return ret, ax


def _pad_tile(a):
    return jnp.pad(a.astype(F32), ((0, 8 - a.shape[0]), (0, 128 - a.shape[1])))


def _relu2_epilogue(acc):
    r = jnp.maximum(acc, 0.0)
    return acc, r * r


def _relu2_bwd_epilogue(acc, u):
    return (acc * (2.0 * jnp.maximum(u.astype(F32), 0.0)),)


def _add_epilogue(acc, res):
    return (acc + res,)


def _add_norm_epilogue(acc, res, g):
    y = acc + res
    r = lax.rsqrt(jnp.mean(y * y, axis=-1, keepdims=True) + EPS)
    return y, y * r * g


def _residual_mm(a, w, res, g_next, name):
    return _mm(a, w, name=name, extras=(res,), rows=(g_next,), out_dtypes=(F32, BF), epilogue=_add_norm_epilogue,
               vmem=VMEM_LARGE)


def _rms_bwd_epilogue(dh, x, dres, g):
    r = lax.rsqrt(jnp.mean(x * x, axis=-1, keepdims=True) + EPS)
    xh = x * r
    dxh = dh * g
    dx = r * (dxh - xh * jnp.mean(dxh * xh, axis=-1, keepdims=True)) + dres
    return dx, dx, _row_sum(dh * xh)


def _norm_bwd_mm(mm, dproj, w, x, dres, g, name):
    return mm(dproj, w, tb=True, name=name, out_dtypes=(F32, BF), extras=(x, dres), rows=(g,),
              epilogue=_rms_bwd_epilogue, n_row_sums=1, tm=512)


def _loss_epilogue(acc, res, target):
    e = acc + res - target
    d = e * (1.0 / D_MODEL)
    return d, d, _row_sum(e * e)


def _mlp_fwd(x, h, w_up, w_down, tag, g_next=None, target=None):
    u, a = _mm(h, w_up, name=f"mlp_up_{tag}", out_dtypes=(BF, BF), epilogue=_relu2_epilogue)
    if target is not None:
        return _mm(a, w_down, name=f"mlp_down_{tag}", extras=(x, target), out_dtypes=(F32, BF), epilogue=_loss_epilogue,
                   n_row_sums=1, tm=512), (h, u, a)
    if g_next is None:
        y, h_next = _mm(a, w_down, name=f"mlp_down_{tag}", extras=(x,), epilogue=_add_epilogue), None
    else:
        y, h_next = _residual_mm(a, w_down, x, g_next, f"mlp_down_{tag}")
    return y, h_next, (h, u, a)


def _mlp_bwd(x, g, w_up, w_down, saved, dy, dyb, tag, mm_exchange=None):
    h, u, a = saved
    du = _mm(dyb, w_down, tb=True, name=f"mlp_down_dx_{tag}", out_dtypes=(BF,), extras=(u,), epilogue=_relu2_bwd_epilogue)
    dw_down = _mm(a, dyb, ta=True, name=f"mlp_down_dw_{tag}", out_dtypes=(BF,))
    dw_up = _mm(h, du, ta=True, name=f"mlp_up_dw_{tag}", out_dtypes=(BF,))
    if mm_exchange is None:
        mm = _mm
    else:
        mm = lambda *args, **kw: mm_exchange([("down" + tag, dw_down)], *args, **kw)
    dx, dxb, dg = _norm_bwd_mm(mm, du, w_up, x, dy, g, f"mlp_up_dx_{tag}")
    return dx, dxb, dg, dw_up, dw_down


COL_SHARDED = ("in_even", "in_odd", "up0", "up1")


def _assemble(key, g):
    if key in COL_SHARDED:
        return g.transpose(1, 0, 2).reshape(g.shape[1], N_DEV * g.shape[2])
    return g.reshape(N_DEV * g.shape[1], g.shape[2])


def _split(key, full):
    rows, cols = full.shape
    if key in COL_SHARDED:
        return full.reshape(rows, N_DEV, cols // N_DEV).transpose(1, 0, 2)
    return full.reshape(N_DEV, rows // N_DEV, cols)


def _local_step(x, target, W, P, late=None):
    S = x.shape[0]
    W = dict(W)
    landed = {}

    def gather_while(stage):
        return None if late is None else _Exchange([s for _, s in late[stage]], gather=True)

    def arrived(stage, outs):
        for (key, _), g in zip([] if late is None else late[stage], outs):
            W[key] = _assemble(key, g)

    def exchange_while(grads):
        return None if late is None else _Exchange([_split(k, g) for k, g in grads], gather=False)

    def left(grads, outs):
        for (key, _), l in zip(grads, outs):
            landed[key] = l

    def mm_gather(stage, *args, **kw):
        if late is None:
            return _mm(*args, **kw)
        out, outs = _mm(*args, carried=gather_while(stage), **kw)
        arrived(stage, outs)
        return out

    def mm_exchange(grads, *args, **kw):
        if late is None:
            return _mm(*args, **kw)
        out, outs = _mm(*args, carried=exchange_while(grads), **kw)
        left(grads, outs)
        return out

    (cos_r, sin_r), (cos_a, sin_a) = _rope_tables(S)
    bucket = _t5_bucket_map().T
    logit = P["ret_decay_logit"]
    lg = _log_sigmoid_tile(_pad_tile(logit))
    bias = _t5_bias(P["t5_table"], bucket)
    nmix, nmlp = P["norm_mix"], P["norm_mlp"]

    h0 = _rms_fwd(x, nmix[0:1], "mix_norm_0")
    proj_e = mm_gather("in_even", h0, W["in_even"], name="in_even")
    (qr, kr, vr, qs, ks, vs), outs = _prep_even_fwd(proj_e, cos_r, sin_r, P["swa_q_norm"], P["swa_k_norm"],
                                                    gather_while("prep_even_fwd"))
    arrived("prep_even_fwd", outs)
    sf, sb = _ret_scan(kr, vr, lg, (0, 0), (1, 0), "retention_states")
    y_ret, oa, outs = _ret_fwd(qr, kr, vr, lg, sf, sb, proj_e, P["ret_norm"], gather_while("retention_fwd"))
    arrived("retention_fwd", outs)
    ob, outs = _swa_fwd(qs, ks, vs, bias, P["swa_sink"], gather_while("swa_fwd"))
    arrived("swa_fwd", outs)
    wo_a, wo_b = W["out_even"][:1024], W["out_even"][1024:]
    x1 = _mm(oa, wo_a, name="out_even_a", extras=(x,), epilogue=_add_epilogue)
    x1, h1 = _residual_mm(ob, wo_b, x1, nmlp[0:1], "out_even_b")
    x2, h2, mlp0 = _mlp_fwd(x1, h1, W["up0"], W["down0"], "0", g_next=nmix[1:2])

    proj_o = _mm(h2, W["in_odd"], name="in_odd")
    qx, kx, vx = _prep_odd_fwd(proj_o, cos_a, sin_a, P["ax_q_norm"], P["ax_k_norm"])
    ox, lse, outs = _flash_fwd(qx, kx, vx, gather_while("flash_fwd"))
    arrived("flash_fwd", outs)
    x3, h3 = _residual_mm(ox, W["out_odd"], x2, nmlp[1:2], "out_odd")
    (d4, d4b, loss_row), mlp1 = _mlp_fwd(x3, h3, W["up1"], W["down1"], "1", target=target)

    d3, d3b, dnmlp1, dw_up1, dw_down1 = _mlp_bwd(x3, nmlp[1:2], W["up1"], W["down1"], mlp1, d4, d4b, "1")
    dox = _mm(d3b, W["out_odd"], tb=True, name="out_odd_dx")
    dw_out_odd = _mm(ox, d3b, ta=True, name="out_odd_dw", out_dtypes=(BF,))
    grads1 = [("up1", dw_up1), ("down1", dw_down1), ("out_odd", dw_out_odd)]
    dqx, dkx, dvx, outs = _flash_bwd(qx, kx, vx, ox, dox, lse, exchange_while(grads1))
    left(grads1, outs)
    dproj_o, dgq_ax, dgk_ax = _prep_odd_bwd(proj_o, cos_a, sin_a, P["ax_q_norm"], P["ax_k_norm"], dqx, dkx, dvx)
    dw_in_odd = _mm(h2, dproj_o, ta=True, name="in_odd_dw", out_dtypes=(BF,))
    d2, d2b, dnmix1 = _norm_bwd_mm(_mm, dproj_o, W["in_odd"], x2, d3, nmix[1:2], "in_odd_dx")

    d1, d1b, dnmlp0, dw_up0, dw_down0 = _mlp_bwd(x1, nmlp[0:1], W["up0"], W["down0"], mlp0, d2, d2b, "0", mm_exchange)
    doa = _mm(d1b, wo_a, tb=True, name="out_even_a_dx")
    dob = _mm(d1b, wo_b, tb=True, name="out_even_b_dx")
    dw_out_even = jnp.concatenate([_mm(oa, d1b, ta=True, name="out_even_a_dw", out_dtypes=(BF,)),
                                   _mm(ob, d1b, ta=True, name="out_even_b_dw", out_dtypes=(BF,))], axis=0)
    dy_ret, dga, dret_norm = _ret_post_bwd(y_ret, proj_e, P["ret_norm"], doa)
    hb, hf = _ret_scan(qr, dy_ret, lg, (1, 1), (0, 1), "retention_state_grads")
    dqr, dkr, dvr, dlogit = _ret_bwd(qr, kr, vr, dy_ret, lg, logit, sf, sb, hf, hb)
    grads0 = [("in_odd", dw_in_odd), ("out_even", dw_out_even)]
    dqs, dks, dvs, dbias, dsink, outs = _swa_bwd(qs, ks, vs, dob, bias, P["swa_sink"], exchange_while(grads0))
    left(grads0, outs)
    dt5 = _t5_table_grad(dbias, bucket)
    dproj_e, dgq_swa, dgk_swa = _prep_even_bwd(proj_e, cos_r, sin_r, P["swa_q_norm"], P["swa_k_norm"],
                                               dqr, dkr, dvr, dga, dqs, dks, dvs)
    dw_in_even = mm_exchange([("up0", dw_up0)], h0, dproj_e, ta=True, name="in_even_dw", out_dtypes=(BF,))
    dx, _, dnmix0 = _norm_bwd_mm(lambda *args, **kw: mm_exchange([("in_even", dw_in_even)], *args, **kw),
                                 dproj_e, W["in_even"], x, d1, nmix[0:1], "in_even_dx")

    if late is None:
        dW = dict(grads1 + grads0, up0=dw_up0, down0=dw_down0, in_even=dw_in_even)
    else:
        dW = landed
    dP = {"norm_mix": jnp.concatenate([dnmix0, dnmix1], 0), "norm_mlp": jnp.concatenate([dnmlp0, dnmlp1], 0),
          "ret_decay_logit": dlogit[:, :, 0, 0].T, "ret_norm": dret_norm,
          "swa_q_norm": dgq_swa, "swa_k_norm": dgk_swa, "swa_sink": dsink[:, 0][None, :],
          "t5_table": dt5[:, :SWA_HEADS], "ax_q_norm": dgq_ax, "ax_k_norm": dgk_ax}
    return loss_row, dx, dW, dP


def _cast_shards(shards):
    n = len(shards)

    def body(*refs):
        for i_ref, o_ref in zip(refs[:n], refs[n:]):
            o_ref[...] = i_ref[...].astype(o_ref.dtype)

    return pl.pallas_call(body, name="cast_shards", in_specs=[VMEM_SPEC] * n, out_specs=[VMEM_SPEC] * n,
                          out_shape=[jax.ShapeDtypeStruct(s.shape, BF) for s in shards],
                          compiler_params=pltpu.CompilerParams(vmem_limit_bytes=VMEM_SMALL))(*shards)


def _all_gather(shards):
    n = len(shards)

    def body(*refs):
        ins, outs, stage = refs[:n], refs[n:2 * n], refs[2 * n:3 * n]
        send_sems, recv_sems, local_sems = refs[3 * n:]
        me = _my_place()
        sibling = _flip(me, 1)
        chips = [_flip(me, 4), _flip(me, 2), _flip(me, 6)]

        def copy(a, k, block, to, src=None):
            dst = outs[a].at[_index(block)]
            return pltpu.make_async_remote_copy(
                src_ref=dst if src is None else src, dst_ref=dst,
                send_sem=send_sems.at[a, k], recv_sem=recv_sems.at[a, k], device_id=to, device_id_type=MESH)

        first, mine = [], []
        for a in range(n):
            stage[a][...] = ins[a][...].astype(stage[a].dtype)
            mine.append(pltpu.make_async_copy(stage[a], outs[a].at[_index(me)], local_sems.at[a]))
            mine[-1].start()
            first.append(copy(a, 0, me, sibling, src=stage[a]))
            first += [copy(a, 1 + j, me, chip, src=stage[a]) for j, chip in enumerate(chips)]
        for cp in first:
            cp.start()
        passed = []
        for a in range(n):
            for j, chip in enumerate(chips):
                copy(a, 1 + j, chip, me).wait_recv()
                passed.append(copy(a, 4 + j, chip, sibling))
                passed[-1].start()
        for a in range(n):
            copy(a, 0, sibling, me).wait_recv()
            for j, chip in enumerate(chips):
                copy(a, 4 + j, _flip(chip, 1), me).wait_recv()
        for cp in first + passed:
            cp.wait_send()
        for cp in mine:
            cp.wait()

    return pl.pallas_call(
        body, name="weights_all_gather",
        in_specs=[VMEM_SPEC] * n, out_specs=[ANY_SPEC] * n,
        out_shape=[jax.ShapeDtypeStruct((N_DEV,) + s.shape, BF) for s in shards],
        scratch_shapes=[pltpu.VMEM(s.shape, BF) for s in shards]
        + [pltpu.SemaphoreType.DMA((n, 7)), pltpu.SemaphoreType.DMA((n, 7)), pltpu.SemaphoreType.DMA((n,))],
        compiler_params=pltpu.CompilerParams(vmem_limit_bytes=VMEM_SMALL),
    )(*shards)


def _all_reduce_small(part):
    R, C = part.shape

    def body(x_ref, o_ref, land_ref, send_sems, recv_sems):
        me = _my_place()
        land_ref[_index(me)] = x_ref[...]
        copies = []
        for k in range(1, N_DEV):
            peer = _flip(me, k)
            copies.append(pltpu.make_async_remote_copy(
                src_ref=x_ref, dst_ref=land_ref.at[_index(me)],
                send_sem=send_sems.at[k - 1], recv_sem=recv_sems.at[k - 1], device_id=peer, device_id_type=MESH))
            copies[-1].start()
        for k in range(1, N_DEV):
            peer = _flip(me, k)
            pltpu.make_async_remote_copy(
                src_ref=x_ref, dst_ref=land_ref.at[_index(peer)],
                send_sem=send_sems.at[k - 1], recv_sem=recv_sems.at[k - 1], device_id=peer, device_id_type=MESH).wait_recv()
        for cp in copies:
            cp.wait_send()
        acc = land_ref[0]
        for s in range(1, N_DEV):
            acc = acc + land_ref[s]
        o_ref[...] = acc
        o_ref[LOSS_ROW:LOSS_ROW + 1, :] = jnp.broadcast_to(_all_sum(acc[LOSS_ROW:LOSS_ROW + 1, :]) * (0.5 / D_MODEL), (1, C))

    return pl.pallas_call(
        body, name="small_all_reduce", in_specs=[VMEM_SPEC], out_specs=VMEM_SPEC,
        out_shape=jax.ShapeDtypeStruct((R, C), F32),
        scratch_shapes=[pltpu.VMEM((N_DEV, R, C), F32), pltpu.SemaphoreType.DMA((7,)), pltpu.SemaphoreType.DMA((7,))],
    )(part)


def _adamw_math(w, g, m, v):
    m = ADAM_B1 * m + (1.0 - ADAM_B1) * g
    v = ADAM_B2 * v + (1.0 - ADAM_B2) * jnp.square(g)
    m_hat = m / (1.0 - ADAM_B1 ** ADAM_STEP)
    v_hat = v / (1.0 - ADAM_B2 ** ADAM_STEP)
    delta = -ADAM_LR * (m_hat / (jnp.sqrt(v_hat) + ADAM_EPS) + ADAM_WD * w)
    return delta, m, v


def _sum_and_adamw(landed, w, m, v, name):
    R, C = w.shape
    tr = _pick(R, (256, 128))

    def body(l_ref, w_ref, m_ref, v_ref, g_ref, d_ref, nm_ref, nv_ref):
        g = l_ref[0].astype(F32)
        for s in range(1, N_DEV):
            g = g + l_ref[s].astype(F32)
        g_ref[...] = g
        d_ref[...], nm_ref[...], nv_ref[...] = _adamw_math(w_ref[...], g, m_ref[...], v_ref[...])

    row = pl.BlockSpec((tr, C), lambda i: (i, 0))
    return pl.pallas_call(
        body, name=name, grid=(R // tr,),
        in_specs=[pl.BlockSpec((N_DEV, tr, C), lambda i: (0, i, 0)), row, row, row], out_specs=[row] * 4,
        out_shape=[jax.ShapeDtypeStruct((R, C), F32)] * 4, compiler_params=_params(("parallel",)),
    )(landed, w, m, v)


def _adamw_small(w, g, m, v):
    def body(w_ref, g_ref, m_ref, v_ref, d_ref, nm_ref, nv_ref):
        d_ref[...], nm_ref[...], nv_ref[...] = _adamw_math(w_ref[...], g_ref[...], m_ref[...], v_ref[...])

    full = pl.BlockSpec(w.shape, lambda: (0, 0))
    return pl.pallas_call(body, name="adamw_small", in_specs=[full] * 4, out_specs=[full] * 3,
                          out_shape=[jax.ShapeDtypeStruct(w.shape, F32)] * 3)(w, g, m, v)


MATRICES = ("w_in_even", "w_out_even", "w_in_odd", "w_out_odd", "w_mlp_up", "w_mlp_down")
SMALL = ("norm_mix", "norm_mlp", "ret_decay_logit", "ret_norm", "swa_q_norm", "swa_k_norm", "swa_sink",
         "t5_table", "ax_q_norm", "ax_k_norm")
MATRIX_OF = {"in_even": ("w_in_even", 0), "out_even": ("w_out_even", 0), "in_odd": ("w_in_odd", 0),
             "out_odd": ("w_out_odd", 0), "up0": ("w_mlp_up", 0), "up1": ("w_mlp_up", 1),
             "down0": ("w_mlp_down", 0), "down1": ("w_mlp_down", 1)}
GATHER_FIRST = ("in_even",)
GATHER_WHILE = {"in_even": ("up0",), "prep_even_fwd": ("out_even",), "retention_fwd": ("in_odd",), "swa_fwd": ("down0",),
                "flash_fwd": ("out_odd", "up1", "down1")}


SMALL_ROWS = 8
SMALL_AT = {"norm_mix": (0, 0), "norm_mlp": (2, 0), "ret_norm": (4, 0), "swa_q_norm": (5, 0), "swa_k_norm": (5, 128),
            "ax_q_norm": (5, 256), "ax_k_norm": (5, 384), "swa_sink": (5, 512), "ret_decay_logit": (5, 640),
            "t5_table": (6, 0)}
LOSS_ROW = 7


def _pack_small(arrays, loss_row=None):
    buf = jnp.zeros((SMALL_ROWS, 1024), F32)
    for name, (r, c) in SMALL_AT.items():
        a = arrays[name].astype(F32)
        a = a.reshape(1, -1) if name in ("ret_decay_logit", "t5_table") else a.reshape(-1, a.shape[-1])
        buf = lax.dynamic_update_slice(buf, a, (r, c))
    if loss_row is not None:
        buf = lax.dynamic_update_slice(buf, loss_row, (LOSS_ROW, 0))
    return buf


def _unpack_small(buf, like):
    out = {}
    for name, (r, c) in SMALL_AT.items():
        shape = like[name].shape
        rows = 1 if name in ("ret_decay_logit", "t5_table") else math.prod(shape[:-1])
        cols = math.prod(shape) // rows
        out[name] = buf[r:r + rows, c:c + cols].reshape(shape)
    return out


def kernel(x, norm_mix, norm_mlp, w_in_even, w_out_even, ret_decay_logit, ret_norm, swa_q_norm, swa_k_norm, swa_sink, t5_table, w_in_odd, w_out_odd, ax_q_norm, ax_k_norm, w_mlp_up, w_mlp_down, loss_target, m_norm_mix, m_norm_mlp, m_w_in_even, m_w_out_even, m_ret_decay_logit, m_ret_norm, m_swa_q_norm, m_swa_k_norm, m_swa_sink, m_t5_table, m_w_in_odd, m_w_out_odd, m_ax_q_norm, m_ax_k_norm, m_w_mlp_up, m_w_mlp_down, v_norm_mix, v_norm_mlp, v_w_in_even, v_w_out_even, v_ret_decay_logit, v_ret_norm, v_swa_q_norm, v_swa_k_norm, v_swa_sink, v_t5_table, v_w_in_odd, v_w_out_odd, v_ax_q_norm, v_ax_k_norm, v_w_mlp_up, v_w_mlp_down):
    given = dict(locals())
    weights = {n: given[n] for n in MATRICES + SMALL}
    moments_m = {n: given["m_" + n] for n in MATRICES + SMALL}
    moments_v = {n: given["v_" + n] for n in MATRICES + SMALL}

    def shard(table, key):
        arg, layer = MATRIX_OF[key]
        return table[arg][layer]

    gathered = _all_gather([shard(weights, k) for k in GATHER_FIRST])
    W = {k: _assemble(k, g) for k, g in zip(GATHER_FIRST, gathered)}
    late_keys = [k for keys in GATHER_WHILE.values() for k in keys]
    cast = dict(zip(late_keys, _cast_shards([shard(weights, k) for k in late_keys])))
    late = {stage: [(k, cast[k]) for k in keys] for stage, keys in GATHER_WHILE.items()}
    P = {"norm_mix": norm_mix, "norm_mlp": norm_mlp, "ret_decay_logit": ret_decay_logit[0], "ret_norm": ret_norm,
         "swa_q_norm": swa_q_norm, "swa_k_norm": swa_k_norm, "swa_sink": swa_sink, "t5_table": t5_table,
         "ax_q_norm": ax_q_norm, "ax_k_norm": ax_k_norm}

    loss_row, dx, landed, dP = _local_step(x[0], loss_target[0], W, P, late)

    per_key = {k: _sum_and_adamw(landed[k], shard(weights, k), shard(moments_m, k), shard(moments_v, k), "adamw_" + k)
               for k in MATRIX_OF}
    grads, deltas, new_m, new_v = {}, {}, {}, {}
    for i, out in enumerate((grads, deltas, new_m, new_v)):
        for n in MATRICES:
            out[n] = jnp.stack([per_key[k][i] for k, (arg, _) in MATRIX_OF.items() if arg == n])

    dP["ret_decay_logit"] = dP["ret_decay_logit"][None]
    total = _all_reduce_small(_pack_small(dP, loss_row))
    loss = total[LOSS_ROW, 0]
    small_d, small_m, small_v = _adamw_small(_pack_small(weights), total, _pack_small(moments_m), _pack_small(moments_v))
    like = {n: weights[n] for n in SMALL}
    for out, buf in ((grads, total), (deltas, small_d), (new_m, small_m), (new_v, small_v)):
        out.update(_unpack_small(buf, like))

    order = ("norm_mix", "norm_mlp", "w_in_even", "w_out_even", "ret_decay_logit", "ret_norm", "swa_q_norm", "swa_k_norm",
             "swa_sink", "t5_table", "w_in_odd", "w_out_odd", "ax_q_norm", "ax_k_norm", "w_mlp_up", "w_mlp_down")
    return (loss, dx[None], *[grads[n] for n in order], *[deltas[n] for n in order],
            *[new_m[n] for n in order], *[new_v[n] for n in order])
```

```python
import math

import jax
import jax.numpy as jnp
from jax import lax
from jax.experimental import pallas as pl
from jax.experimental.pallas import tpu as pltpu

F32 = jnp.float32
BF = jnp.bfloat16

D_MODEL = 1024
HEAD_DIM = 128
EPS = 1e-6
NEG_INF = -1e30
RET_HEADS, RET_DK, RET_DV = 4, 128, 256
RET_THETA = 10000.0
SWA_HEADS, SWA_KV_HEADS, WINDOW, BLOCK = 8, 2, 128, 128
T5_BUCKETS, T5_MAX_DIST = 32, 128
AX_HEADS, AX_KV_HEADS, AX_THETA, GRID_W = 8, 2, 10000.0, 64
D_FF = 4096
ATT_SCALE = HEAD_DIM ** -0.5
LN2 = math.log(2.0)
AX_SCALE = ATT_SCALE / LN2
RET_SCALE = RET_DK ** -0.5
N_DEV = 8

ADAM_LR, ADAM_B1, ADAM_B2, ADAM_EPS, ADAM_WD, ADAM_STEP = 0.001, 0.9, 0.999, 1e-08, 0.01, 10

MIB = 1024 * 1024
VMEM_SMALL = 40 * MIB
VMEM_LARGE = 56 * MIB

OFF_QA, OFF_KA, OFF_VA, OFF_GA, OFF_QB, OFF_KB, OFF_VB = 0, 512, 1024, 2048, 3072, 4096, 4352
EVEN_IN = 4608
ODD_IN = 1536

NT = (((1,), (1,)), ((), ()))
TN = (((0,), (0,)), ((), ()))
NN = (((1,), (0,)), ((), ()))


def _dot(a, b, dims=NN):
    return lax.dot_general(a, b, dims, preferred_element_type=F32)


def _params(sem=None, vmem=VMEM_SMALL):
    return pltpu.CompilerParams(dimension_semantics=sem, vmem_limit_bytes=vmem)


def _pick(n, prefs):
    for p in prefs:
        if n % p == 0:
            return p
    return n


def _row_sum(x):
    return jnp.sum(x, axis=0, keepdims=True)


def _all_sum(x):
    return jnp.sum(jnp.sum(x, axis=0, keepdims=True), axis=1, keepdims=True)


def _sigmoid(x):
    return 1.0 / (1.0 + jnp.exp(-x))


SMEM_SPEC = pl.BlockSpec(memory_space=pltpu.SMEM)
ANY_SPEC = pl.BlockSpec(memory_space=pl.ANY)
VMEM_SPEC = pl.BlockSpec(memory_space=pltpu.VMEM)
MESH = pl.DeviceIdType.MESH


def _my_place():
    return lax.axis_index("x"), lax.axis_index("y"), lax.axis_index("c")


def _flip(place, k):
    x, y, c = place
    return (1 - x if k & 4 else x, 1 - y if k & 2 else y, 1 - c if k & 1 else c)


def _index(place):
    x, y, c = place
    return 4 * x + 2 * y + c


class _Exchange:
    def __init__(self, sources, gather):
        self.sources, self.gather, self.n = list(sources), gather, len(sources)
        self.out_shape = [jax.ShapeDtypeStruct(((N_DEV,) + s.shape) if gather else s.shape, s.dtype) for s in self.sources]
        self.scratch = [pltpu.SemaphoreType.DMA((self.n, 7)), pltpu.SemaphoreType.DMA((self.n, 7)),
                        pltpu.SemaphoreType.DMA((self.n,))]

    def _source(self, ins, a, place):
        return ins[a] if self.gather else ins[a].at[_index(place)]

    def _local(self, ins, outs, sems):
        me = _my_place()
        return [pltpu.make_async_copy(self._source(ins, a, me), outs[a].at[_index(me)], sems[2].at[a]) for a in range(self.n)]

    def _remote(self, ins, outs, sems, arriving):
        send_sems, recv_sems, _ = sems
        me = _my_place()
        copies = []
        for a in range(self.n):
            for k in range(1, N_DEV):
                peer = _flip(me, k)
                copies.append(pltpu.make_async_remote_copy(
                    src_ref=self._source(ins, a, peer), dst_ref=outs[a].at[_index(peer if arriving else me)],
                    send_sem=send_sems.at[a, k - 1], recv_sem=recv_sems.at[a, k - 1], device_id=peer, device_id_type=MESH))
        return copies

    def start(self, ins, outs, sems):
        for cp in self._local(ins, outs, sems) + self._remote(ins, outs, sems, arriving=False):
            cp.start()

    def wait(self, ins, outs, sems):
        for cp in self._remote(ins, outs, sems, arriving=True):
            cp.wait_recv()
        for cp in self._remote(ins, outs, sems, arriving=False):
            cp.wait_send()
        for cp in self._local(ins, outs, sems):
            cp.wait()


def _carry_call(body, args, *, name, grid, in_specs, out_specs, out_shape, scratch_shapes=(), vmem=VMEM_SMALL,
                semantics=None, carried=None):
    if carried is None:
        outs = pl.pallas_call(body, name=name, grid=grid, in_specs=in_specs, out_specs=out_specs, out_shape=out_shape,
                              scratch_shapes=list(scratch_shapes), compiler_params=_params(semantics, vmem))(*args)
        return list(outs), []
    ni, no, ns, nc = len(in_specs), len(out_specs), len(scratch_shapes), carried.n

    def full_body(*refs):
        ins, cin = refs[:ni], refs[ni:ni + nc]
        outs, cout = refs[ni + nc:ni + nc + no], refs[ni + nc + no:ni + 2 * nc + no]
        scratch, sems = refs[ni + 2 * nc + no:ni + 2 * nc + no + ns], refs[ni + 2 * nc + no + ns:]
        ids = [pl.program_id(d) for d in range(len(grid))]
        first, last = ids[0] == 0, ids[0] == grid[0] - 1
        for d in range(1, len(grid)):
            first, last = first & (ids[d] == 0), last & (ids[d] == grid[d] - 1)

        @pl.when(first)
        def _():
            carried.start(cin, cout, sems)

        body(*ins, *outs, *scratch)

        @pl.when(last)
        def _():
            carried.wait(cin, cout, sems)

    outs = pl.pallas_call(
        full_body, name=name, grid=grid, in_specs=list(in_specs) + [ANY_SPEC] * nc,
        out_specs=list(out_specs) + [ANY_SPEC] * nc, out_shape=list(out_shape) + carried.out_shape,
        scratch_shapes=list(scratch_shapes) + carried.scratch,
        compiler_params=_params(("arbitrary",) * len(grid), vmem))(*args, *carried.sources)
    return list(outs[:no]), list(outs[no:])


def _mm(a, b, *, name, ta=False, tb=False, out_dtypes=(F32,), extras=(), rows=(), n_row_sums=0, epilogue=None,
        tm=1024, tn=1024, tk=1024, vmem=VMEM_SMALL, carried=None):
    M, K = (a.shape[1], a.shape[0]) if ta else a.shape
    N = b.shape[0] if tb else b.shape[1]
    assert K == (b.shape[1] if tb else b.shape[0])
    tm = _pick(M, (tm, 512, 256, 128))
    tn = _pick(N, (tn, 1536, 512, 384, 256, 128))
    tk = _pick(K, (2048, 2304, tk, 1536, 512, 256, 128))
    nk = K // tk
    assert n_row_sums == 0 or tn == N
    ne, nr, no = len(extras), len(rows), len(out_dtypes)
    dims = (((0 if ta else 1,), (1 if tb else 0,)), ((), ()))
    if epilogue is None:
        epilogue = lambda acc: (acc,)

    def body(a_ref, b_ref, *rest):
        extra_refs, out_refs = rest[:ne + nr], rest[ne + nr:ne + nr + no]
        sum_refs = rest[ne + nr + no:ne + nr + no + n_row_sums]
        first_tile = pl.program_id(0) == 0

        def finish(acc):
            outs = epilogue(acc, *[r[...] for r in extra_refs])
            for o_ref, o in zip(out_refs, outs[:no]):
                o_ref[...] = o.astype(o_ref.dtype)
            for s_ref, contribution in zip(sum_refs, outs[no:]):
                @pl.when(first_tile)
                def _(s_ref=s_ref, contribution=contribution):
                    s_ref[...] = contribution

                @pl.when(jnp.logical_not(first_tile))
                def _(s_ref=s_ref, contribution=contribution):
                    s_ref[...] += contribution

        part = _dot(a_ref[...], b_ref[...], dims)
        if nk == 1:
            finish(part)
        else:
            acc_ref = rest[-1]
            k = pl.program_id(2)

            @pl.when(k == 0)
            def _():
                acc_ref[...] = part

            @pl.when(k > 0)
            def _():
                acc_ref[...] += part

            @pl.when(k == nk - 1)
            def _():
                finish(acc_ref[...])

    a_spec = pl.BlockSpec((tk, tm), lambda i, j, k: (k, i)) if ta else pl.BlockSpec((tm, tk), lambda i, j, k: (i, k))
    b_spec = pl.BlockSpec((tn, tk), lambda i, j, k: (j, k)) if tb else pl.BlockSpec((tk, tn), lambda i, j, k: (k, j))
    o_spec = pl.BlockSpec((tm, tn), lambda i, j, k: (i, j))
    row_spec = pl.BlockSpec((1, tn), lambda i, j, k: (0, j))
    outs, landed = _carry_call(
        body, (a, b, *extras, *rows), name=name, grid=(M // tm, N // tn, nk),
        in_specs=[a_spec, b_spec] + [o_spec] * ne + [row_spec] * nr,
        out_specs=[o_spec] * no + [row_spec] * n_row_sums,
        out_shape=[jax.ShapeDtypeStruct((M, N), dt) for dt in out_dtypes] + [jax.ShapeDtypeStruct((1, N), F32)] * n_row_sums,
        scratch_shapes=[pltpu.VMEM((tm, tn), F32)] if nk > 1 else [],
        vmem=vmem, semantics=("arbitrary" if n_row_sums else "parallel", "parallel", "arbitrary"), carried=carried)
    outs = outs[0] if len(outs) == 1 else outs
    return outs if carried is None else (outs, landed)


def _rms_fwd(x, g, name):
    S, Dm = x.shape
    tr = _pick(S, (512,))

    def body(x_ref, g_ref, o_ref):
        xv = x_ref[...]
        r = lax.rsqrt(jnp.mean(xv * xv, axis=-1, keepdims=True) + EPS)
        o_ref[...] = (xv * r * g_ref[...]).astype(o_ref.dtype)

    row = pl.BlockSpec((tr, Dm), lambda i: (i, 0))
    return pl.pallas_call(
        body, name=name, grid=(S // tr,),
        in_specs=[row, pl.BlockSpec((1, Dm), lambda i: (0, 0))], out_specs=row,
        out_shape=jax.ShapeDtypeStruct((S, Dm), BF), compiler_params=_params(("parallel",)),
    )(x, g)


def _partner(x, half):
    if half == 64:
        return pltpu.roll(x, 64, 1)
    lane = lax.broadcasted_iota(jnp.int32, x.shape, 1)
    return jnp.where((lane % (2 * half)) < half, pltpu.roll(x, 128 - half, 1), pltpu.roll(x, half, 1))


def _rope(x, cos, sin, half):
    return x * cos + _partner(x, half) * sin


def _rope_t(dy, cos, sin, half):
    return dy * cos - _partner(dy, half) * sin


def _head_norm(x):
    r = lax.rsqrt(jnp.mean(x * x, axis=-1, keepdims=True) + EPS)
    return x * r, r


def _head_norm_bwd(dxh, xh, r):
    return r * (dxh - xh * jnp.mean(dxh * xh, axis=-1, keepdims=True))


def _cols(ref, off, width=HEAD_DIM):
    return ref[:, off:off + width]


def _prep_even_fwd(proj, cos, sin, gq, gk, carried=None):
    S = proj.shape[0]
    tr = _pick(S, (256,))

    def body(p_ref, cos_ref, sin_ref, gq_ref, gk_ref, qr_ref, kr_ref, vr_ref, qs_ref, ks_ref, vs_ref):
        cos_v, sin_v = cos_ref[...], sin_ref[...]
        for h in range(RET_HEADS):
            o = h * RET_DK
            qr_ref[:, o:o + RET_DK] = _rope(_cols(p_ref, OFF_QA + o), cos_v, sin_v, 64).astype(qr_ref.dtype)
            kr_ref[:, o:o + RET_DK] = (_rope(_cols(p_ref, OFF_KA + o), cos_v, sin_v, 64) * RET_SCALE).astype(kr_ref.dtype)
        vr_ref[...] = p_ref[:, OFF_VA:OFF_VA + 1024].astype(vr_ref.dtype)
        for h in range(SWA_HEADS):
            o = h * HEAD_DIM
            xh, _ = _head_norm(_cols(p_ref, OFF_QB + o))
            qs_ref[:, o:o + HEAD_DIM] = (xh * gq_ref[...] * ATT_SCALE).astype(qs_ref.dtype)
        for h in range(SWA_KV_HEADS):
            o = h * HEAD_DIM
            xh, _ = _head_norm(_cols(p_ref, OFF_KB + o))
            ks_ref[:, o:o + HEAD_DIM] = (xh * gk_ref[...]).astype(ks_ref.dtype)
        vs_ref[...] = p_ref[:, OFF_VB:OFF_VB + 256].astype(vs_ref.dtype)

    def row(w):
        return pl.BlockSpec((tr, w), lambda i: (i, 0))

    vec = pl.BlockSpec((1, HEAD_DIM), lambda i: (0, 0))
    widths = (512, 512, 1024, 1024, 256, 256)
    return _carry_call(
        body, (proj, cos, sin, gq, gk), name="prep_even_fwd", grid=(S // tr,),
        in_specs=[row(EVEN_IN), row(128), row(128), vec, vec],
        out_specs=[row(w) for w in widths],
        out_shape=[jax.ShapeDtypeStruct((S, w), BF) for w in widths],
        semantics=("parallel",), carried=carried)


def _prep_even_bwd(proj, cos, sin, gq, gk, dqr, dkr, dvr, dga, dqs, dks, dvs):
    S = proj.shape[0]
    tr = _pick(S, (256,))

    def body(p_ref, cos_ref, sin_ref, gq_ref, gk_ref, dqr_ref, dkr_ref, dvr_ref, dga_ref, dqs_ref, dks_ref,
             dvs_ref, dp_ref, dgq_ref, dgk_ref):
        cos_v, sin_v = cos_ref[...], sin_ref[...]
        dt = dp_ref.dtype
        for h in range(RET_HEADS):
            o = h * RET_DK
            dp_ref[:, OFF_QA + o:OFF_QA + o + RET_DK] = _rope_t(_cols(dqr_ref, o).astype(F32), cos_v, sin_v, 64).astype(dt)
            dp_ref[:, OFF_KA + o:OFF_KA + o + RET_DK] = _rope_t(_cols(dkr_ref, o).astype(F32) * RET_SCALE, cos_v, sin_v, 64).astype(dt)
        dp_ref[:, OFF_VA:OFF_VA + 1024] = dvr_ref[...].astype(dt)
        dp_ref[:, OFF_GA:OFF_GA + 1024] = dga_ref[...].astype(dt)
        dgq = jnp.zeros((1, HEAD_DIM), F32)
        for h in range(SWA_HEADS):
            o = h * HEAD_DIM
            xh, r = _head_norm(_cols(p_ref, OFF_QB + o))
            dy = _cols(dqs_ref, o).astype(F32) * ATT_SCALE
            dgq = dgq + _row_sum(dy * xh)
            dp_ref[:, OFF_QB + o:OFF_QB + o + HEAD_DIM] = _head_norm_bwd(dy * gq_ref[...], xh, r).astype(dt)
        dgk = jnp.zeros((1, HEAD_DIM), F32)
        for h in range(SWA_KV_HEADS):
            o = h * HEAD_DIM
            xh, r = _head_norm(_cols(p_ref, OFF_KB + o))
            dy = _cols(dks_ref, o)
            dgk = dgk + _row_sum(dy * xh)
            dp_ref[:, OFF_KB + o:OFF_KB + o + HEAD_DIM] = _head_norm_bwd(dy * gk_ref[...], xh, r).astype(dt)
        dp_ref[:, OFF_VB:OFF_VB + 256] = dvs_ref[...].astype(dt)

        @pl.when(pl.program_id(0) == 0)
        def _():
            dgq_ref[...] = jnp.zeros_like(dgq_ref)
            dgk_ref[...] = jnp.zeros_like(dgk_ref)

        dgq_ref[...] += dgq
        dgk_ref[...] += dgk

    def row(w):
        return pl.BlockSpec((tr, w), lambda i: (i, 0))

    vec = pl.BlockSpec((1, HEAD_DIM), lambda i: (0, 0))
    return pl.pallas_call(
        body, name="prep_even_bwd", grid=(S // tr,),
        in_specs=[row(EVEN_IN), row(128), row(128), vec, vec, row(512), row(512), row(1024), row(1024),
                  row(1024), row(256), row(256)],
        out_specs=[row(EVEN_IN), vec, vec],
        out_shape=[jax.ShapeDtypeStruct((S, EVEN_IN), BF), jax.ShapeDtypeStruct((1, HEAD_DIM), F32),
                   jax.ShapeDtypeStruct((1, HEAD_DIM), F32)],
        compiler_params=_params(("arbitrary",)),
    )(proj, cos, sin, gq, gk, dqr, dkr, dvr, dga, dqs, dks, dvs)


RET_CHUNK = 512
def _log_sigmoid_tile(logit_tile):
    def body(x_ref, o_ref):
        xv = x_ref[...]
        t = jnp.exp(-jnp.abs(xv))
        log1p_t = jnp.where(t < 1e-3, t * (1.0 - 0.5 * t), jnp.log(1.0 + t))
        o_ref[...] = jnp.minimum(xv, 0.0) - log1p_t

    full = pl.BlockSpec((8, 128), lambda: (0, 0))
    return pl.pallas_call(body, name="log_sigmoid", in_specs=[full], out_specs=full,
                          out_shape=jax.ShapeDtypeStruct((8, 128), F32))(logit_tile)


def _decay(diff, lf, lb):
    return jnp.exp(jnp.where(diff >= 0, lf * diff, -(lb * diff)))


def _col_iota(n):
    return lax.broadcasted_iota(jnp.int32, (n, 1), 0).astype(F32)


def _ret_scan(x, z, lg, asc, desc, name):
    S = x.shape[0]
    C = _pick(S, (RET_CHUNK,))
    nc = S // C
    (arow, aoff), (drow, doff) = asc, desc

    def body(lg_ref, xa_ref, za_ref, xd_ref, zd_ref, asc_ref, desc_ref, sa_ref, sd_ref):
        t = pl.program_id(0)

        @pl.when(t == 0)
        def _():
            sa_ref[...] = jnp.zeros_like(sa_ref)
            sd_ref[...] = jnp.zeros_like(sd_ref)

        j = _col_iota(C)
        for h in range(RET_HEADS):
            la, ld = lg_ref[arow, h], lg_ref[drow, h]
            kc, vc = slice(h * RET_DK, (h + 1) * RET_DK), slice(h * RET_DV, (h + 1) * RET_DV)
            asc_ref[h, 0] = sa_ref[h]
            desc_ref[h, 0] = sd_ref[h]
            xa = (xa_ref[:, kc].astype(F32) * jnp.exp(la * (C - 1 + aoff - j))).astype(xa_ref.dtype)
            xd = (xd_ref[:, kc].astype(F32) * jnp.exp(ld * (j + doff))).astype(xd_ref.dtype)
            sa_ref[h] = jnp.exp(jnp.full((1, RET_DV), la * C, F32)) * sa_ref[h] + _dot(xa, za_ref[:, vc], TN)
            sd_ref[h] = jnp.exp(jnp.full((1, RET_DV), ld * C, F32)) * sd_ref[h] + _dot(xd, zd_ref[:, vc], TN)

    state = jax.ShapeDtypeStruct((RET_HEADS, nc, RET_DK, RET_DV), F32)
    qk_w, v_w = RET_HEADS * RET_DK, RET_HEADS * RET_DV
    return pl.pallas_call(
        body, name=name, grid=(nc,),
        in_specs=[SMEM_SPEC,
                  pl.BlockSpec((C, qk_w), lambda t: (t, 0)), pl.BlockSpec((C, v_w), lambda t: (t, 0)),
                  pl.BlockSpec((C, qk_w), lambda t: (nc - 1 - t, 0)), pl.BlockSpec((C, v_w), lambda t: (nc - 1 - t, 0))],
        out_specs=[pl.BlockSpec((RET_HEADS, 1, RET_DK, RET_DV), lambda t: (0, t, 0, 0)),
                   pl.BlockSpec((RET_HEADS, 1, RET_DK, RET_DV), lambda t: (0, nc - 1 - t, 0, 0))],
        out_shape=[state, state],
        scratch_shapes=[pltpu.VMEM((RET_HEADS, RET_DK, RET_DV), F32), pltpu.VMEM((RET_HEADS, RET_DK, RET_DV), F32)],
        compiler_params=_params(("arbitrary",)),
    )(lg, x, z, x, z)


def _head_cols(h):
    return slice(h * RET_DK, (h + 1) * RET_DK), slice(h * RET_DV, (h + 1) * RET_DV)


def _ret_fwd(q, k, v, lg, sf, sb, proj, gn, carried=None):
    S = q.shape[0]
    C = _pick(S, (RET_CHUNK,))

    def body(lg_ref, q_ref, k_ref, v_ref, sf_ref, sb_ref, gate_ref, gn_ref, y_ref, o_ref):
        diff = (lax.broadcasted_iota(jnp.int32, (C, C), 0) - lax.broadcasted_iota(jnp.int32, (C, C), 1)).astype(F32)
        r = _col_iota(C)
        for h in range(RET_HEADS):
            lf, lb = lg_ref[0, h], lg_ref[1, h]
            kc, vc = _head_cols(h)
            qv = q_ref[:, kc]
            dt = qv.dtype
            y = _dot((_dot(qv, k_ref[:, kc], NT) * _decay(diff, lf, lb)).astype(dt), v_ref[:, vc])
            qf = qv.astype(F32)
            y = y + _dot((qf * jnp.exp(lf * (r + 1.0))).astype(dt), sf_ref[h, 0].astype(dt))
            y = y + _dot((qf * jnp.exp(lb * (C - r))).astype(dt), sb_ref[h, 0].astype(dt))
            y_ref[:, vc] = y
            yh, _ = _head_norm(y)
            gate = gate_ref[:, vc]
            o_ref[:, vc] = (gate * _sigmoid(gate) * (yh * gn_ref[:, vc])).astype(o_ref.dtype)

    state = pl.BlockSpec((RET_HEADS, 1, RET_DK, RET_DV), lambda c: (0, c, 0, 0))
    qk = pl.BlockSpec((C, RET_HEADS * RET_DK), lambda c: (c, 0))
    wide = pl.BlockSpec((C, 1024), lambda c: (c, 0))
    (y, o), landed = _carry_call(
        body, (lg, q, k, v, sf, sb, proj, gn), name="retention_fwd", grid=(S // C,),
        in_specs=[SMEM_SPEC, qk, qk, wide, state, state, pl.BlockSpec((C, 1024), lambda c: (c, OFF_GA // 1024)),
                  pl.BlockSpec((1, 1024), lambda c: (0, 0))],
        out_specs=[wide, wide],
        out_shape=[jax.ShapeDtypeStruct((S, 1024), F32), jax.ShapeDtypeStruct((S, 1024), BF)],
        semantics=("parallel",), carried=carried)
    return y, o, landed


def _ret_bwd(q, k, v, dy, lg, logit, sf, sb, hf, hb):
    S = q.shape[0]
    C = _pick(S, (RET_CHUNK,))
    nc = S // C

    def body(lg_ref, logit_ref, q_ref, k_ref, v_ref, dy_ref, sf_ref, sb_ref, hf_ref, hb_ref,
             dq_ref, dk_ref, dv_ref, dlg_ref, acc_ref):
        c = pl.program_id(0)

        @pl.when(c == 0)
        def _():
            acc_ref[...] = jnp.zeros_like(acc_ref)

        diff = (lax.broadcasted_iota(jnp.int32, (C, C), 0) - lax.broadcasted_iota(jnp.int32, (C, C), 1)).astype(F32)
        r = _col_iota(C)
        for h in range(RET_HEADS):
            one_head(h, diff, r, lg_ref, q_ref, k_ref, v_ref, dy_ref, sf_ref, sb_ref, hf_ref, hb_ref,
                     dq_ref, dk_ref, dv_ref, acc_ref)

        @pl.when(c == nc - 1)
        def _():
            for h in range(RET_HEADS):
                for d in range(2):
                    gate = 1.0 / (1.0 + jnp.exp(jnp.full((8, 128), logit_ref[d, h], F32)))
                    dlg_ref[h, d] = acc_ref[h, d] * gate

    def one_head(h, diff, r, lg_ref, q_ref, k_ref, v_ref, dy_ref, sf_ref, sb_ref, hf_ref, hb_ref,
                 dq_ref, dk_ref, dv_ref, acc_ref):
        lf, lb = lg_ref[0, h], lg_ref[1, h]
        kc, vc = _head_cols(h)
        qv, kv, vv, dyv = q_ref[:, kc], k_ref[:, kc], v_ref[:, vc], dy_ref[:, vc]
        dt = qv.dtype
        qf, kf = qv.astype(F32), kv.astype(F32)
        dec = _decay(diff, lf, lb)
        sc = _dot(qv, kv, NT) * dec
        dp = _dot(dyv, vv, NT)
        da = (dp * dec).astype(dt)
        dq = _dot(da, kv)
        dk = _dot(da, qv, TN)
        dv = _dot(sc.astype(dt), dyv, TN)
        w = sc * dp * diff
        tot_w, tot_f = _all_sum(w), _all_sum(jnp.where(diff >= 0, w, 0.0))
        d_lf, d_lb = tot_f, tot_f - tot_w
        a, b = jnp.exp(lf * (r + 1.0)), jnp.exp(lb * (C - r))
        e, f = jnp.exp(lf * (C - 1.0 - r)), jnp.exp(lb * r)
        sfv, sbv, hfv, hbv = sf_ref[h, 0], sb_ref[h, 0], hf_ref[h, 0], hb_ref[h, 0]
        t_f, t_b = _dot(dyv, sfv.astype(dt), NT), _dot(dyv, sbv.astype(dt), NT)
        u_f, u_b = _dot(vv, hfv.astype(dt), NT), _dot(vv, hbv.astype(dt), NT)
        dq_ref[:, kc] = (dq + a * t_f + b * t_b).astype(dq_ref.dtype)
        dk_ref[:, kc] = (dk + e * u_f + f * u_b).astype(dk_ref.dtype)
        dv_ref[:, vc] = (dv + _dot((kf * e).astype(dt), hfv.astype(dt)) + _dot((kf * f).astype(dt), hbv.astype(dt))).astype(dv_ref.dtype)
        row_q_f = jnp.sum(qf * t_f, axis=-1, keepdims=True)
        row_q_b = jnp.sum(qf * t_b, axis=-1, keepdims=True)
        row_k_f = jnp.sum(kf * u_f, axis=-1, keepdims=True)
        row_k_b = jnp.sum(kf * u_b, axis=-1, keepdims=True)
        gf_c = jnp.exp(jnp.full((1, 1), lf * C, F32))
        gb_c = jnp.exp(jnp.full((1, 1), lb * C, F32))
        d_lf = d_lf + _all_sum((r + 1.0) * a * row_q_f + (C - 1.0 - r) * e * row_k_f) + C * gf_c * _all_sum(hfv * sfv)
        d_lb = d_lb + _all_sum((C - r) * b * row_q_b + r * f * row_k_b) + C * gb_c * _all_sum(hbv * sbv)
        acc_ref[h, 0] += jnp.broadcast_to(d_lf, (8, 128))
        acc_ref[h, 1] += jnp.broadcast_to(d_lb, (8, 128))

    state = pl.BlockSpec((RET_HEADS, 1, RET_DK, RET_DV), lambda c: (0, c, 0, 0))
    qk = pl.BlockSpec((C, RET_HEADS * RET_DK), lambda c: (c, 0))
    vy = pl.BlockSpec((C, RET_HEADS * RET_DV), lambda c: (c, 0))
    return pl.pallas_call(
        body, name="retention_bwd", grid=(nc,),
        in_specs=[SMEM_SPEC, SMEM_SPEC, qk, qk, vy, vy, state, state, state, state],
        out_specs=[qk, qk, vy, pl.BlockSpec((RET_HEADS, 2, 8, 128), lambda c: (0, 0, 0, 0))],
        out_shape=[jax.ShapeDtypeStruct((S, RET_HEADS * RET_DK), BF), jax.ShapeDtypeStruct((S, RET_HEADS * RET_DK), BF),
                   jax.ShapeDtypeStruct((S, RET_HEADS * RET_DV), BF),
                   jax.ShapeDtypeStruct((RET_HEADS, 2, 8, 128), F32)],
        scratch_shapes=[pltpu.VMEM((RET_HEADS, 2, 8, 128), F32)],
        compiler_params=_params(("arbitrary",)),
    )(lg, logit, q, k, v, dy, sf, sb, hf, hb)


def _ret_post_bwd(y, proj, gn, do):
    S = y.shape[0]
    tr = _pick(S, (512,))

    def body(y_ref, g_ref, gn_ref, do_ref, dy_ref, dg_ref, dgn_ref):
        @pl.when(pl.program_id(0) == 0)
        def _():
            dgn_ref[...] = jnp.zeros_like(dgn_ref)

        for h in range(RET_HEADS):
            o = h * RET_DV
            yh, r = _head_norm(_cols(y_ref, o, RET_DV))
            gate = _cols(g_ref, o, RET_DV)
            gnh = gn_ref[:, o:o + RET_DV]
            dout = _cols(do_ref, o, RET_DV).astype(F32)
            sg = _sigmoid(gate)
            dz = dout * (gate * sg)
            dg_ref[:, o:o + RET_DV] = (dout * (yh * gnh) * (sg * (1.0 + gate * (1.0 - sg)))).astype(dg_ref.dtype)
            dgn_ref[:, o:o + RET_DV] += _row_sum(dz * yh)
            dy_ref[:, o:o + RET_DV] = _head_norm_bwd(dz * gnh, yh, r).astype(dy_ref.dtype)

    row = pl.BlockSpec((tr, 1024), lambda i: (i, 0))
    vec = pl.BlockSpec((1, 1024), lambda i: (0, 0))
    return pl.pallas_call(
        body, name="retention_post_bwd", grid=(S // tr,),
        in_specs=[row, pl.BlockSpec((tr, 1024), lambda i: (i, OFF_GA // 1024)), vec, row],
        out_specs=[row, row, vec],
        out_shape=[jax.ShapeDtypeStruct((S, 1024), BF), jax.ShapeDtypeStruct((S, 1024), BF),
                   jax.ShapeDtypeStruct((1, 1024), F32)],
        compiler_params=_params(("arbitrary",)),
    )(y, proj, gn, do)


def _t5_bucket_map():
    r = jnp.arange(BLOCK)
    j = jnp.arange(3 * BLOCK)
    rel = j[None, :] - BLOCK - r[:, None]
    nb = T5_BUCKETS // 2
    max_exact = nb // 2
    ret = jnp.where(rel > 0, nb, 0)
    n = jnp.abs(rel)
    nf = jnp.maximum(n, 1).astype(jnp.float32)
    large = max_exact + (jnp.log(nf / max_exact) / math.log(T5_MAX_DIST / max_exact)
                         * (nb - max_exact)).astype(jnp.int32)
    large = jnp.minimum(large, nb - 1)
    bucket = ret + jnp.where(n < max_exact, n, large)
    return jnp.where(jnp.abs(rel) <= WINDOW, bucket, -1).astype(jnp.int32)


SWA_G = SWA_HEADS // SWA_KV_HEADS
SWA_LANES = SWA_G * BLOCK


def _t5_bias(table, bucket_t):
    def body(t_ref, b_ref, o_ref):
        bk = b_ref[...]
        for h in range(SWA_HEADS):
            acc = jnp.full(bk.shape, NEG_INF, F32)
            for b in range(T5_BUCKETS):
                acc = jnp.where(bk == b, t_ref[b, h], acc)
            o_ref[h // SWA_G, :, (h % SWA_G) * BLOCK:(h % SWA_G + 1) * BLOCK] = acc

    return pl.pallas_call(
        body, name="t5_bias", in_specs=[SMEM_SPEC, pl.BlockSpec((3 * BLOCK, BLOCK), lambda: (0, 0))],
        out_specs=pl.BlockSpec((SWA_KV_HEADS, 3 * BLOCK, SWA_LANES), lambda: (0, 0, 0)),
        out_shape=jax.ShapeDtypeStruct((SWA_KV_HEADS, 3 * BLOCK, SWA_LANES), F32),
    )(table, bucket_t)


def _t5_table_grad(dbias, bucket_t):
    def body(d_ref, b_ref, o_ref):
        bk = b_ref[...]
        lane = lax.broadcasted_iota(jnp.int32, (1, 128), 1)
        for b in range(T5_BUCKETS):
            hit = bk == b
            row = jnp.zeros((1, 128), F32)
            for h in range(SWA_HEADS):
                d = d_ref[h // SWA_G, :, (h % SWA_G) * BLOCK:(h % SWA_G + 1) * BLOCK]
                row = row + jnp.where(lane == h, _all_sum(jnp.where(hit, d, 0.0)), 0.0)
            o_ref[b:b + 1, :] = row

    return pl.pallas_call(
        body, name="t5_table_grad",
        in_specs=[pl.BlockSpec((SWA_KV_HEADS, 3 * BLOCK, SWA_LANES), lambda: (0, 0, 0)),
                  pl.BlockSpec((3 * BLOCK, BLOCK), lambda: (0, 0))],
        out_specs=pl.BlockSpec((T5_BUCKETS, 128), lambda: (0, 0)),
        out_shape=jax.ShapeDtypeStruct((T5_BUCKETS, 128), F32),
    )(dbias, bucket_t)


def _swa_scores(i, nb, q4, kw, bias_t, sink_row):
    s = _dot(kw, q4, NT) + bias_t
    row = lax.broadcasted_iota(jnp.int32, s.shape, 0)
    first_row = jnp.where(i == 0, BLOCK, 0)
    end_row = jnp.where(i == nb - 1, 2 * BLOCK, 3 * BLOCK)
    s = jnp.where((row < first_row) | (row >= end_row), NEG_INF, s)
    m = jnp.maximum(jnp.max(s, axis=0, keepdims=True), sink_row)
    p = jnp.exp(s - m)
    e_sink = jnp.exp(sink_row - m)
    inv = 1.0 / (jnp.sum(p, axis=0, keepdims=True) + e_sink)
    return p * inv, e_sink * inv


def _swa_group(q_ref, sink_ref, kh):
    heads = range(kh * SWA_G, (kh + 1) * SWA_G)
    q4 = jnp.concatenate([_cols(q_ref, h * HEAD_DIM) for h in heads], axis=0)
    sink_row = jnp.concatenate([jnp.full((1, BLOCK), sink_ref[0, h], F32) for h in heads], axis=1)
    return q4, sink_row


def _swa_unstack(ref, kh, x_t):
    for g in range(SWA_G):
        h = kh * SWA_G + g
        ref[:, h * HEAD_DIM:(h + 1) * HEAD_DIM] = x_t[:, g * BLOCK:(g + 1) * BLOCK].T.astype(ref.dtype)


def _swa_window(ref, i, nb, off):
    prev, nxt = jnp.maximum(i - 1, 0), jnp.minimum(i + 1, nb - 1)
    rows = [pl.ds(pl.multiple_of(b * BLOCK, BLOCK), BLOCK) for b in (prev, i, nxt)]
    return jnp.concatenate([ref[r, off:off + HEAD_DIM] for r in rows], axis=0), rows


def _swa_fwd(q, k, v, bias, sink, carried=None):
    S = q.shape[0]
    nb = S // BLOCK

    def body(sink_ref, q_ref, k_ref, v_ref, bias_ref, o_ref):
        i = pl.program_id(0)
        for kh in range(SWA_KV_HEADS):
            kw, _ = _swa_window(k_ref, i, nb, kh * HEAD_DIM)
            vw, _ = _swa_window(v_ref, i, nb, kh * HEAD_DIM)
            q4, sink_row = _swa_group(q_ref, sink_ref, kh)
            p, _ = _swa_scores(i, nb, q4, kw, bias_ref[kh], sink_row)
            _swa_unstack(o_ref, kh, _dot(vw, p.astype(vw.dtype), TN))

    full_kv = pl.BlockSpec((S, SWA_KV_HEADS * HEAD_DIM), lambda i: (0, 0))
    (o,), landed = _carry_call(
        body, (sink, q, k, v, bias), name="swa_fwd", grid=(nb,),
        in_specs=[SMEM_SPEC, pl.BlockSpec((BLOCK, 1024), lambda i: (i, 0)), full_kv, full_kv,
                  pl.BlockSpec((SWA_KV_HEADS, 3 * BLOCK, SWA_LANES), lambda i: (0, 0, 0))],
        out_specs=[pl.BlockSpec((BLOCK, 1024), lambda i: (i, 0))],
        out_shape=[jax.ShapeDtypeStruct((S, 1024), BF)], semantics=("parallel",), carried=carried)
    return o, landed


def _swa_bwd(q, k, v, do, bias, sink, carried=None):
    S = q.shape[0]
    nb = S // BLOCK

    def body(sink_ref, q_ref, k_ref, v_ref, do_ref, bias_ref, dq_ref, dk_ref, dv_ref, dbias_ref, dsink_ref):
        i = pl.program_id(0)

        @pl.when(i == 0)
        def _():
            dk_ref[...] = jnp.zeros_like(dk_ref)
            dv_ref[...] = jnp.zeros_like(dv_ref)
            dbias_ref[...] = jnp.zeros_like(dbias_ref)
            dsink_ref[...] = jnp.zeros_like(dsink_ref)

        for kh in range(SWA_KV_HEADS):
            off = kh * HEAD_DIM
            kw, rows = _swa_window(k_ref, i, nb, off)
            vw, _ = _swa_window(v_ref, i, nb, off)
            q4, sink_row = _swa_group(q_ref, sink_ref, kh)
            p, p_sink = _swa_scores(i, nb, q4, kw, bias_ref[kh], sink_row)
            do4 = jnp.concatenate([_cols(do_ref, (kh * SWA_G + g) * HEAD_DIM) for g in range(SWA_G)], axis=0).astype(vw.dtype)
            dp = _dot(vw, do4, NT)
            delta = jnp.sum(p * dp, axis=0, keepdims=True)
            ds = p * (dp - delta)
            dsb = ds.astype(q4.dtype)
            _swa_unstack(dq_ref, kh, _dot(kw, dsb, TN))
            dkw = _dot(dsb, q4)
            dvw = _dot(p.astype(do4.dtype), do4)
            dbias_ref[kh] += ds
            sink_term = p_sink * delta
            for g in range(SWA_G):
                h = kh * SWA_G + g
                dsink_ref[h:h + 1, :] += jnp.broadcast_to(-_all_sum(sink_term[:, g * BLOCK:(g + 1) * BLOCK]), (1, 128))
            for b, r in enumerate(rows):
                dk_ref[r, off:off + HEAD_DIM] += dkw[b * BLOCK:(b + 1) * BLOCK]
                dv_ref[r, off:off + HEAD_DIM] += dvw[b * BLOCK:(b + 1) * BLOCK]

    full_kv = pl.BlockSpec((S, SWA_KV_HEADS * HEAD_DIM), lambda i: (0, 0))
    blk = pl.BlockSpec((BLOCK, 1024), lambda i: (i, 0))
    bias_spec = pl.BlockSpec((SWA_KV_HEADS, 3 * BLOCK, SWA_LANES), lambda i: (0, 0, 0))
    outs, landed = _carry_call(
        body, (sink, q, k, v, do, bias), name="swa_bwd", grid=(nb,),
        in_specs=[SMEM_SPEC, blk, full_kv, full_kv, blk, bias_spec],
        out_specs=[blk, full_kv, full_kv, bias_spec, pl.BlockSpec((8, 128), lambda i: (0, 0))],
        out_shape=[jax.ShapeDtypeStruct((S, 1024), BF), jax.ShapeDtypeStruct((S, 256), F32),
                   jax.ShapeDtypeStruct((S, 256), F32),
                   jax.ShapeDtypeStruct((SWA_KV_HEADS, 3 * BLOCK, SWA_LANES), F32), jax.ShapeDtypeStruct((8, 128), F32)],
        vmem=VMEM_LARGE, semantics=("arbitrary",), carried=carried)
    return (*outs, landed)


def _prep_odd_fwd(proj, cos, sin, gq, gk):
    S = proj.shape[0]
    tr = _pick(S, (512,))

    def body(p_ref, cos_ref, sin_ref, gq_ref, gk_ref, q_ref, k_ref, v_ref):
        cos_v, sin_v = cos_ref[...], sin_ref[...]
        for h in range(AX_HEADS):
            o = h * HEAD_DIM
            xh, _ = _head_norm(_cols(p_ref, o))
            q_ref[:, o:o + HEAD_DIM] = (_rope(xh * gq_ref[...], cos_v, sin_v, 32) * AX_SCALE).astype(q_ref.dtype)
        for h in range(AX_KV_HEADS):
            o = h * HEAD_DIM
            xh, _ = _head_norm(_cols(p_ref, 1024 + o))
            k_ref[:, o:o + HEAD_DIM] = _rope(xh * gk_ref[...], cos_v, sin_v, 32).astype(k_ref.dtype)
        v_ref[...] = p_ref[:, 1280:1536].astype(v_ref.dtype)

    def row(w):
        return pl.BlockSpec((tr, w), lambda i: (i, 0))

    vec = pl.BlockSpec((1, HEAD_DIM), lambda i: (0, 0))
    return pl.pallas_call(
        body, name="prep_odd_fwd", grid=(S // tr,),
        in_specs=[row(ODD_IN), row(128), row(128), vec, vec], out_specs=[row(1024), row(256), row(256)],
        out_shape=[jax.ShapeDtypeStruct((S, w), BF) for w in (1024, 256, 256)],
        compiler_params=_params(("parallel",)),
    )(proj, cos, sin, gq, gk)


def _prep_odd_bwd(proj, cos, sin, gq, gk, dq, dk, dv):
    S = proj.shape[0]
    tr = _pick(S, (512,))

    def body(p_ref, cos_ref, sin_ref, gq_ref, gk_ref, dq_ref, dk_ref, dv_ref, dp_ref, dgq_ref, dgk_ref):
        cos_v, sin_v = cos_ref[...], sin_ref[...]
        dt = dp_ref.dtype
        dgq = jnp.zeros((1, HEAD_DIM), F32)
        for h in range(AX_HEADS):
            o = h * HEAD_DIM
            xh, r = _head_norm(_cols(p_ref, o))
            dy = _rope_t(_cols(dq_ref, o).astype(F32) * AX_SCALE, cos_v, sin_v, 32)
            dgq = dgq + _row_sum(dy * xh)
            dp_ref[:, o:o + HEAD_DIM] = _head_norm_bwd(dy * gq_ref[...], xh, r).astype(dt)
        dgk = jnp.zeros((1, HEAD_DIM), F32)
        for h in range(AX_KV_HEADS):
            o = h * HEAD_DIM
            xh, r = _head_norm(_cols(p_ref, 1024 + o))
            dy = _rope_t(_cols(dk_ref, o), cos_v, sin_v, 32)
            dgk = dgk + _row_sum(dy * xh)
            dp_ref[:, 1024 + o:1024 + o + HEAD_DIM] = _head_norm_bwd(dy * gk_ref[...], xh, r).astype(dt)
        dp_ref[:, 1280:1536] = dv_ref[...].astype(dt)

        @pl.when(pl.program_id(0) == 0)
        def _():
            dgq_ref[...] = jnp.zeros_like(dgq_ref)
            dgk_ref[...] = jnp.zeros_like(dgk_ref)

        dgq_ref[...] += dgq
        dgk_ref[...] += dgk

    def row(w):
        return pl.BlockSpec((tr, w), lambda i: (i, 0))

    vec = pl.BlockSpec((1, HEAD_DIM), lambda i: (0, 0))
    return pl.pallas_call(
        body, name="prep_odd_bwd", grid=(S // tr,),
        in_specs=[row(ODD_IN), row(128), row(128), vec, vec, row(1024), row(256), row(256)],
        out_specs=[row(ODD_IN), vec, vec],
        out_shape=[jax.ShapeDtypeStruct((S, ODD_IN), BF), jax.ShapeDtypeStruct((1, HEAD_DIM), F32),
                   jax.ShapeDtypeStruct((1, HEAD_DIM), F32)],
        compiler_params=_params(("arbitrary",)),
    )(proj, cos, sin, gq, gk, dq, dk, dv)


def _loop_unrolled(n, factor, step, init):
    while n % factor:
        factor //= 2

    def trip(t, carry):
        for u in range(factor):
            carry = step(factor * t + u, carry)
        return carry

    return lax.fori_loop(0, n // factor, trip, init)


def _flash_fwd(q, k, v, carried=None):
    S = q.shape[0]
    tq = _pick(S, (1024, 512))
    tk = _pick(S, (1024, 512))
    nk = S // tk
    G = AX_HEADS // AX_KV_HEADS

    def body(q_ref, k_ref, v_ref, o_ref, lse_ref):
        qv = q_ref[...]

        def step(j, carry):
            m, l, acc = carry
            rows = pl.ds(pl.multiple_of(j * tk, tk), tk)
            s = _dot(qv, k_ref[rows, :], NT)
            m_new = jnp.maximum(m, jnp.max(s, axis=-1, keepdims=True))
            alpha = jnp.exp2(m - m_new)
            p = jnp.exp2(s - m_new)
            l = alpha * l + jnp.sum(p, axis=-1, keepdims=True)
            acc = alpha * acc + _dot(p.astype(v_ref.dtype), v_ref[rows, :])
            return m_new, l, acc

        init = (jnp.full((tq, 1), NEG_INF, F32), jnp.zeros((tq, 1), F32), jnp.zeros((tq, HEAD_DIM), F32))
        m, l, acc = _loop_unrolled(nk, 8, step, init)
        o_ref[...] = (acc / l).astype(o_ref.dtype)
        lse_ref[0] = jnp.broadcast_to(m + jnp.log2(l), (tq, 128))

    (o, lse), landed = _carry_call(
        body, (q, k, v), name="flash_fwd", grid=(AX_HEADS, S // tq),
        in_specs=[pl.BlockSpec((tq, HEAD_DIM), lambda h, i: (i, h)),
                  pl.BlockSpec((S, HEAD_DIM), lambda h, i: (0, h // G)),
                  pl.BlockSpec((S, HEAD_DIM), lambda h, i: (0, h // G))],
        out_specs=[pl.BlockSpec((tq, HEAD_DIM), lambda h, i: (i, h)),
                   pl.BlockSpec((1, tq, 128), lambda h, i: (h, i, 0))],
        out_shape=[jax.ShapeDtypeStruct((S, AX_HEADS * HEAD_DIM), BF), jax.ShapeDtypeStruct((AX_HEADS, S, 128), F32)],
        semantics=("parallel", "parallel"), carried=carried)
    return o, lse, landed


def _flash_bwd(q, k, v, o, do, lse, carried=None):
    S = q.shape[0]
    tq = _pick(S, (1024, 512))
    tk = _pick(S, (1024, 512))
    nq, nk = S // tq, S // tk
    G = AX_HEADS // AX_KV_HEADS

    def body(q_ref, k_ref, v_ref, o_ref, do_ref, lse_ref, dq_ref, dk_ref, dv_ref):
        g, i = pl.program_id(1), pl.program_id(2)

        @pl.when((g == 0) & (i == 0))
        def _():
            dk_ref[...] = jnp.zeros_like(dk_ref)
            dv_ref[...] = jnp.zeros_like(dv_ref)

        qv = q_ref[...]
        do_f = do_ref[...].astype(F32)
        dob = do_f.astype(qv.dtype)
        dob_ln2 = (do_f * LN2).astype(qv.dtype)
        delta = jnp.sum(do_f * o_ref[...].astype(F32), axis=-1, keepdims=True) * LN2
        lse_col = lse_ref[0][:, 0:1]

        def step(j, dq):
            rows = pl.ds(pl.multiple_of(j * tk, tk), tk)
            kj, vj = k_ref[rows, :], v_ref[rows, :]
            p = jnp.exp2(_dot(qv, kj, NT) - lse_col)
            dp = _dot(dob_ln2, vj, NT)
            ds = (p * (dp - delta)).astype(qv.dtype)
            dk_ref[rows, :] += _dot(ds, qv, TN)
            dv_ref[rows, :] += _dot(p.astype(dob.dtype), dob, TN)
            return dq + _dot(ds, kj)

        dq_ref[...] = _loop_unrolled(nk, 8, step, jnp.zeros((tq, HEAD_DIM), F32)).astype(dq_ref.dtype)

    q_spec = pl.BlockSpec((tq, HEAD_DIM), lambda kh, g, i: (i, kh * G + g))
    kv_spec = pl.BlockSpec((S, HEAD_DIM), lambda kh, g, i: (0, kh))
    (dq, dk, dv), landed = _carry_call(
        body, (q, k, v, o, do, lse), name="flash_bwd", grid=(AX_KV_HEADS, G, nq),
        in_specs=[q_spec, kv_spec, kv_spec, q_spec, q_spec,
                  pl.BlockSpec((1, tq, 128), lambda kh, g, i: (kh * G + g, i, 0))],
        out_specs=[q_spec, kv_spec, kv_spec],
        out_shape=[jax.ShapeDtypeStruct((S, AX_HEADS * HEAD_DIM), BF), jax.ShapeDtypeStruct((S, 256), F32),
                   jax.ShapeDtypeStruct((S, 256), F32)],
        vmem=VMEM_LARGE, semantics=("arbitrary", "arbitrary", "arbitrary"), carried=carried)
    return dq, dk, dv, landed


def _rope_angles(pos, dim, theta):
    inv = theta ** (-jnp.arange(0, dim, 2, dtype=jnp.float32) / dim)
    return pos.astype(jnp.float32)[:, None] * inv[None, :]


def _rope_tables(S):
    ang = _rope_angles(jnp.arange(S), RET_DK, RET_THETA)
    c, s = jnp.cos(ang), jnp.sin(ang)
    ret = (jnp.concatenate([c, c], -1), jnp.concatenate([-s, s], -1))
    rows = S // GRID_W
    ar, ac = _rope_angles(jnp.arange(rows), HEAD_DIM // 2, AX_THETA), _rope_angles(jnp.arange(GRID_W), HEAD_DIM // 2, AX_THETA)
    cr, sr = jnp.repeat(jnp.cos(ar), GRID_W, axis=0), jnp.repeat(jnp.sin(ar), GRID_W, axis=0)
    cc, sc = jnp.tile(jnp.cos(ac), (rows, 1)), jnp.tile(jnp.sin(ac), (rows, 1))
    ax = (jnp.concatenate([cr, cr, cc, cc], -1), jnp.concatenate([-sr, sr, -sc, sc], -1))
    return ret, ax


def _pad_tile(a):
    return jnp.pad(a.astype(F32), ((0, 8 - a.shape[0]), (0, 128 - a.shape[1])))


def _relu2_epilogue(acc):
    r = jnp.maximum(acc, 0.0)
    return acc, r * r


def _relu2_bwd_epilogue(acc, u):
    return (acc * (2.0 * jnp.maximum(u.astype(F32), 0.0)),)


def _add_epilogue(acc, res):
    return (acc + res,)


def _add_norm_epilogue(acc, res, g):
    y = acc + res
    r = lax.rsqrt(jnp.mean(y * y, axis=-1, keepdims=True) + EPS)
    return y, y * r * g


def _residual_mm(a, w, res, g_next, name):
    return _mm(a, w, name=name, extras=(res,), rows=(g_next,), out_dtypes=(F32, BF), epilogue=_add_norm_epilogue,
               vmem=VMEM_LARGE)


def _rms_bwd_epilogue(dh, x, dres, g):
    r = lax.rsqrt(jnp.mean(x * x, axis=-1, keepdims=True) + EPS)
    xh = x * r
    dxh = dh * g
    dx = r * (dxh - xh * jnp.mean(dxh * xh, axis=-1, keepdims=True)) + dres
    return dx, dx, _row_sum(dh * xh)


def _norm_bwd_mm(mm, dproj, w, x, dres, g, name):
    return mm(dproj, w, tb=True, name=name, out_dtypes=(F32, BF), extras=(x, dres), rows=(g,),
              epilogue=_rms_bwd_epilogue, n_row_sums=1, tm=512)


def _loss_epilogue(acc, res, target):
    e = acc + res - target
    d = e * (1.0 / D_MODEL)
    return d, d, _row_sum(e * e)


def _mlp_fwd(x, h, w_up, w_down, tag, g_next=None, target=None):
    u, a = _mm(h, w_up, name=f"mlp_up_{tag}", out_dtypes=(BF, BF), epilogue=_relu2_epilogue)
    if target is not None:
        return _mm(a, w_down, name=f"mlp_down_{tag}", extras=(x, target), out_dtypes=(F32, BF), epilogue=_loss_epilogue,
                   n_row_sums=1, tm=512), (h, u, a)
    if g_next is None:
        y, h_next = _mm(a, w_down, name=f"mlp_down_{tag}", extras=(x,), epilogue=_add_epilogue), None
    else:
        y, h_next = _residual_mm(a, w_down, x, g_next, f"mlp_down_{tag}")
    return y, h_next, (h, u, a)


def _mlp_bwd(x, g, w_up, w_down, saved, dy, dyb, tag, mm_exchange=None):
    h, u, a = saved
    du = _mm(dyb, w_down, tb=True, name=f"mlp_down_dx_{tag}", out_dtypes=(BF,), extras=(u,), epilogue=_relu2_bwd_epilogue)
    dw_down = _mm(a, dyb, ta=True, name=f"mlp_down_dw_{tag}", out_dtypes=(BF,))
    dw_up = _mm(h, du, ta=True, name=f"mlp_up_dw_{tag}", out_dtypes=(BF,))
    if mm_exchange is None:
        mm = _mm
    else:
        mm = lambda *args, **kw: mm_exchange([("down" + tag, dw_down)], *args, **kw)
    dx, dxb, dg = _norm_bwd_mm(mm, du, w_up, x, dy, g, f"mlp_up_dx_{tag}")
    return dx, dxb, dg, dw_up, dw_down


COL_SHARDED = ("in_even", "in_odd", "up0", "up1")


def _assemble(key, g):
    if key in COL_SHARDED:
        return g.transpose(1, 0, 2).reshape(g.shape[1], N_DEV * g.shape[2])
    return g.reshape(N_DEV * g.shape[1], g.shape[2])


def _split(key, full):
    rows, cols = full.shape
    if key in COL_SHARDED:
        return full.reshape(rows, N_DEV, cols // N_DEV).transpose(1, 0, 2)
    return full.reshape(N_DEV, rows // N_DEV, cols)


def _local_step(x, target, W, P, late=None):
    S = x.shape[0]
    W = dict(W)
    landed = {}

    def gather_while(stage):
        return None if late is None else _Exchange([s for _, s in late[stage]], gather=True)

    def arrived(stage, outs):
        for (key, _), g in zip([] if late is None else late[stage], outs):
            W[key] = _assemble(key, g)

    def exchange_while(grads):
        return None if late is None else _Exchange([_split(k, g) for k, g in grads], gather=False)

    def left(grads, outs):
        for (key, _), l in zip(grads, outs):
            landed[key] = l

    def mm_gather(stage, *args, **kw):
        if late is None:
            return _mm(*args, **kw)
        out, outs = _mm(*args, carried=gather_while(stage), **kw)
        arrived(stage, outs)
        return out

    def mm_exchange(grads, *args, **kw):
        if late is None:
            return _mm(*args, **kw)
        out, outs = _mm(*args, carried=exchange_while(grads), **kw)
        left(grads, outs)
        return out

    (cos_r, sin_r), (cos_a, sin_a) = _rope_tables(S)
    bucket = _t5_bucket_map().T
    logit = P["ret_decay_logit"]
    lg = _log_sigmoid_tile(_pad_tile(logit))
    bias = _t5_bias(P["t5_table"], bucket)
    nmix, nmlp = P["norm_mix"], P["norm_mlp"]

    h0 = _rms_fwd(x, nmix[0:1], "mix_norm_0")
    proj_e = mm_gather("in_even", h0, W["in_even"], name="in_even")
    (qr, kr, vr, qs, ks, vs), outs = _prep_even_fwd(proj_e, cos_r, sin_r, P["swa_q_norm"], P["swa_k_norm"],
                                                    gather_while("prep_even_fwd"))
    arrived("prep_even_fwd", outs)
    sf, sb = _ret_scan(kr, vr, lg, (0, 0), (1, 0), "retention_states")
    y_ret, oa, outs = _ret_fwd(qr, kr, vr, lg, sf, sb, proj_e, P["ret_norm"], gather_while("retention_fwd"))
    arrived("retention_fwd", outs)
    ob, outs = _swa_fwd(qs, ks, vs, bias, P["swa_sink"], gather_while("swa_fwd"))
    arrived("swa_fwd", outs)
    wo_a, wo_b = W["out_even"][:1024], W["out_even"][1024:]
    x1 = _mm(oa, wo_a, name="out_even_a", extras=(x,), epilogue=_add_epilogue)
    x1, h1 = _residual_mm(ob, wo_b, x1, nmlp[0:1], "out_even_b")
    x2, h2, mlp0 = _mlp_fwd(x1, h1, W["up0"], W["down0"], "0", g_next=nmix[1:2])

    proj_o = _mm(h2, W["in_odd"], name="in_odd")
    qx, kx, vx = _prep_odd_fwd(proj_o, cos_a, sin_a, P["ax_q_norm"], P["ax_k_norm"])
    ox, lse, outs = _flash_fwd(qx, kx, vx, gather_while("flash_fwd"))
    arrived("flash_fwd", outs)
    x3, h3 = _residual_mm(ox, W["out_odd"], x2, nmlp[1:2], "out_odd")
    (d4, d4b, loss_row), mlp1 = _mlp_fwd(x3, h3, W["up1"], W["down1"], "1", target=target)

    d3, d3b, dnmlp1, dw_up1, dw_down1 = _mlp_bwd(x3, nmlp[1:2], W["up1"], W["down1"], mlp1, d4, d4b, "1")
    dox = _mm(d3b, W["out_odd"], tb=True, name="out_odd_dx")
    dw_out_odd = _mm(ox, d3b, ta=True, name="out_odd_dw", out_dtypes=(BF,))
    grads1 = [("up1", dw_up1), ("down1", dw_down1), ("out_odd", dw_out_odd)]
    dqx, dkx, dvx, outs = _flash_bwd(qx, kx, vx, ox, dox, lse, exchange_while(grads1))
    left(grads1, outs)
    dproj_o, dgq_ax, dgk_ax = _prep_odd_bwd(proj_o, cos_a, sin_a, P["ax_q_norm"], P["ax_k_norm"], dqx, dkx, dvx)
    dw_in_odd = _mm(h2, dproj_o, ta=True, name="in_odd_dw", out_dtypes=(BF,))
    d2, d2b, dnmix1 = _norm_bwd_mm(_mm, dproj_o, W["in_odd"], x2, d3, nmix[1:2], "in_odd_dx")

    d1, d1b, dnmlp0, dw_up0, dw_down0 = _mlp_bwd(x1, nmlp[0:1], W["up0"], W["down0"], mlp0, d2, d2b, "0", mm_exchange)
    doa = _mm(d1b, wo_a, tb=True, name="out_even_a_dx")
    dob = _mm(d1b, wo_b, tb=True, name="out_even_b_dx")
    dw_out_even = jnp.concatenate([_mm(oa, d1b, ta=True, name="out_even_a_dw", out_dtypes=(BF,)),
                                   _mm(ob, d1b, ta=True, name="out_even_b_dw", out_dtypes=(BF,))], axis=0)
    dy_ret, dga, dret_norm = _ret_post_bwd(y_ret, proj_e, P["ret_norm"], doa)
    hb, hf = _ret_scan(qr, dy_ret, lg, (1, 1), (0, 1), "retention_state_grads")
    dqr, dkr, dvr, dlogit = _ret_bwd(qr, kr, vr, dy_ret, lg, logit, sf, sb, hf, hb)
    grads0 = [("in_odd", dw_in_odd), ("out_even", dw_out_even)]
    dqs, dks, dvs, dbias, dsink, outs = _swa_bwd(qs, ks, vs, dob, bias, P["swa_sink"], exchange_while(grads0))
    left(grads0, outs)
    dt5 = _t5_table_grad(dbias, bucket)
    dproj_e, dgq_swa, dgk_swa = _prep_even_bwd(proj_e, cos_r, sin_r, P["swa_q_norm"], P["swa_k_norm"],
                                               dqr, dkr, dvr, dga, dqs, dks, dvs)
    dw_in_even = mm_exchange([("up0", dw_up0)], h0, dproj_e, ta=True, name="in_even_dw", out_dtypes=(BF,))
    dx, _, dnmix0 = _norm_bwd_mm(lambda *args, **kw: mm_exchange([("in_even", dw_in_even)], *args, **kw),
                                 dproj_e, W["in_even"], x, d1, nmix[0:1], "in_even_dx")

    if late is None:
        dW = dict(grads1 + grads0, up0=dw_up0, down0=dw_down0, in_even=dw_in_even)
    else:
        dW = landed
    dP = {"norm_mix": jnp.concatenate([dnmix0, dnmix1], 0), "norm_mlp": jnp.concatenate([dnmlp0, dnmlp1], 0),
          "ret_decay_logit": dlogit[:, :, 0, 0].T, "ret_norm": dret_norm,
          "swa_q_norm": dgq_swa, "swa_k_norm": dgk_swa, "swa_sink": dsink[:, 0][None, :],
          "t5_table": dt5[:, :SWA_HEADS], "ax_q_norm": dgq_ax, "ax_k_norm": dgk_ax}
    return loss_row, dx, dW, dP


def _cast_shards(shards):
    n = len(shards)

    def body(*refs):
        for i_ref, o_ref in zip(refs[:n], refs[n:]):
            o_ref[...] = i_ref[...].astype(o_ref.dtype)

    return pl.pallas_call(body, name="cast_shards", in_specs=[VMEM_SPEC] * n, out_specs=[VMEM_SPEC] * n,
                          out_shape=[jax.ShapeDtypeStruct(s.shape, BF) for s in shards],
                          compiler_params=pltpu.CompilerParams(vmem_limit_bytes=VMEM_SMALL))(*shards)


def _all_gather(shards):
    n = len(shards)

    def body(*refs):
        ins, outs, stage = refs[:n], refs[n:2 * n], refs[2 * n:3 * n]
        send_sems, recv_sems, local_sems = refs[3 * n:]
        me = _my_place()
        sibling = _flip(me, 1)
        chips = [_flip(me, 4), _flip(me, 2), _flip(me, 6)]

        def copy(a, k, block, to, src=None):
            dst = outs[a].at[_index(block)]
            return pltpu.make_async_remote_copy(
                src_ref=dst if src is None else src, dst_ref=dst,
                send_sem=send_sems.at[a, k], recv_sem=recv_sems.at[a, k], device_id=to, device_id_type=MESH)

        first, mine = [], []
        for a in range(n):
            stage[a][...] = ins[a][...].astype(stage[a].dtype)
            mine.append(pltpu.make_async_copy(stage[a], outs[a].at[_index(me)], local_sems.at[a]))
            mine[-1].start()
            first.append(copy(a, 0, me, sibling, src=stage[a]))
            first += [copy(a, 1 + j, me, chip, src=stage[a]) for j, chip in enumerate(chips)]
        for cp in first:
            cp.start()
        passed = []
        for a in range(n):
            for j, chip in enumerate(chips):
                copy(a, 1 + j, chip, me).wait_recv()
                passed.append(copy(a, 4 + j, chip, sibling))
                passed[-1].start()
        for a in range(n):
            copy(a, 0, sibling, me).wait_recv()
            for j, chip in enumerate(chips):
                copy(a, 4 + j, _flip(chip, 1), me).wait_recv()
        for cp in first + passed:
            cp.wait_send()
        for cp in mine:
            cp.wait()

    return pl.pallas_call(
        body, name="weights_all_gather",
        in_specs=[VMEM_SPEC] * n, out_specs=[ANY_SPEC] * n,
        out_shape=[jax.ShapeDtypeStruct((N_DEV,) + s.shape, BF) for s in shards],
        scratch_shapes=[pltpu.VMEM(s.shape, BF) for s in shards]
        + [pltpu.SemaphoreType.DMA((n, 7)), pltpu.SemaphoreType.DMA((n, 7)), pltpu.SemaphoreType.DMA((n,))],
        compiler_params=pltpu.CompilerParams(vmem_limit_bytes=VMEM_SMALL),
    )(*shards)


def _all_reduce_small(part):
    R, C = part.shape

    def body(x_ref, o_ref, land_ref, send_sems, recv_sems):
        me = _my_place()
        land_ref[_index(me)] = x_ref[...]
        copies = []
        for k in range(1, N_DEV):
            peer = _flip(me, k)
            copies.append(pltpu.make_async_remote_copy(
                src_ref=x_ref, dst_ref=land_ref.at[_index(me)],
                send_sem=send_sems.at[k - 1], recv_sem=recv_sems.at[k - 1], device_id=peer, device_id_type=MESH))
            copies[-1].start()
        for k in range(1, N_DEV):
            peer = _flip(me, k)
            pltpu.make_async_remote_copy(
                src_ref=x_ref, dst_ref=land_ref.at[_index(peer)],
                send_sem=send_sems.at[k - 1], recv_sem=recv_sems.at[k - 1], device_id=peer, device_id_type=MESH).wait_recv()
        for cp in copies:
            cp.wait_send()
        acc = land_ref[0]
        for s in range(1, N_DEV):
            acc = acc + land_ref[s]
        o_ref[...] = acc
        o_ref[LOSS_ROW:LOSS_ROW + 1, :] = jnp.broadcast_to(_all_sum(acc[LOSS_ROW:LOSS_ROW + 1, :]) * (0.5 / D_MODEL), (1, C))

    return pl.pallas_call(
        body, name="small_all_reduce", in_specs=[VMEM_SPEC], out_specs=VMEM_SPEC,
        out_shape=jax.ShapeDtypeStruct((R, C), F32),
        scratch_shapes=[pltpu.VMEM((N_DEV, R, C), F32), pltpu.SemaphoreType.DMA((7,)), pltpu.SemaphoreType.DMA((7,))],
    )(part)


def _adamw_math(w, g, m, v):
    m = ADAM_B1 * m + (1.0 - ADAM_B1) * g
    v = ADAM_B2 * v + (1.0 - ADAM_B2) * jnp.square(g)
    m_hat = m / (1.0 - ADAM_B1 ** ADAM_STEP)
    v_hat = v / (1.0 - ADAM_B2 ** ADAM_STEP)
    delta = -ADAM_LR * (m_hat / (jnp.sqrt(v_hat) + ADAM_EPS) + ADAM_WD * w)
    return delta, m, v


def _sum_and_adamw(landed, w, m, v, name):
    R, C = w.shape
    tr = _pick(R, (256, 128))

    def body(l_ref, w_ref, m_ref, v_ref, g_ref, d_ref, nm_ref, nv_ref):
        g = l_ref[0].astype(F32)
        for s in range(1, N_DEV):
            g = g + l_ref[s].astype(F32)
        g_ref[...] = g
        d_ref[...], nm_ref[...], nv_ref[...] = _adamw_math(w_ref[...], g, m_ref[...], v_ref[...])

    row = pl.BlockSpec((tr, C), lambda i: (i, 0))
    return pl.pallas_call(
        body, name=name, grid=(R // tr,),
        in_specs=[pl.BlockSpec((N_DEV, tr, C), lambda i: (0, i, 0)), row, row, row], out_specs=[row] * 4,
        out_shape=[jax.ShapeDtypeStruct((R, C), F32)] * 4, compiler_params=_params(("parallel",)),
    )(landed, w, m, v)


def _adamw_small(w, g, m, v):
    def body(w_ref, g_ref, m_ref, v_ref, d_ref, nm_ref, nv_ref):
        d_ref[...], nm_ref[...], nv_ref[...] = _adamw_math(w_ref[...], g_ref[...], m_ref[...], v_ref[...])

    full = pl.BlockSpec(w.shape, lambda: (0, 0))
    return pl.pallas_call(body, name="adamw_small", in_specs=[full] * 4, out_specs=[full] * 3,
                          out_shape=[jax.ShapeDtypeStruct(w.shape, F32)] * 3)(w, g, m, v)


MATRICES = ("w_in_even", "w_out_even", "w_in_odd", "w_out_odd", "w_mlp_up", "w_mlp_down")
SMALL = ("norm_mix", "norm_mlp", "ret_decay_logit", "ret_norm", "swa_q_norm", "swa_k_norm", "swa_sink",
         "t5_table", "ax_q_norm", "ax_k_norm")
MATRIX_OF = {"in_even": ("w_in_even", 0), "out_even": ("w_out_even", 0), "in_odd": ("w_in_odd", 0),
             "out_odd": ("w_out_odd", 0), "up0": ("w_mlp_up", 0), "up1": ("w_mlp_up", 1),
             "down0": ("w_mlp_down", 0), "down1": ("w_mlp_down", 1)}
GATHER_FIRST = ("in_even",)
GATHER_WHILE = {"in_even": ("up0",), "prep_even_fwd": ("out_even",), "retention_fwd": ("in_odd",), "swa_fwd": ("down0",),
                "flash_fwd": ("out_odd", "up1", "down1")}


SMALL_ROWS = 8
SMALL_AT = {"norm_mix": (0, 0), "norm_mlp": (2, 0), "ret_norm": (4, 0), "swa_q_norm": (5, 0), "swa_k_norm": (5, 128),
            "ax_q_norm": (5, 256), "ax_k_norm": (5, 384), "swa_sink": (5, 512), "ret_decay_logit": (5, 640),
            "t5_table": (6, 0)}
LOSS_ROW = 7


def _pack_small(arrays, loss_row=None):
    buf = jnp.zeros((SMALL_ROWS, 1024), F32)
    for name, (r, c) in SMALL_AT.items():
        a = arrays[name].astype(F32)
        a = a.reshape(1, -1) if name in ("ret_decay_logit", "t5_table") else a.reshape(-1, a.shape[-1])
        buf = lax.dynamic_update_slice(buf, a, (r, c))
    if loss_row is not None:
        buf = lax.dynamic_update_slice(buf, loss_row, (LOSS_ROW, 0))
    return buf


def _unpack_small(buf, like):
    out = {}
    for name, (r, c) in SMALL_AT.items():
        shape = like[name].shape
        rows = 1 if name in ("ret_decay_logit", "t5_table") else math.prod(shape[:-1])
        cols = math.prod(shape) // rows
        out[name] = buf[r:r + rows, c:c + cols].reshape(shape)
    return out


def kernel(x, norm_mix, norm_mlp, w_in_even, w_out_even, ret_decay_logit, ret_norm, swa_q_norm, swa_k_norm, swa_sink, t5_table, w_in_odd, w_out_odd, ax_q_norm, ax_k_norm, w_mlp_up, w_mlp_down, loss_target, m_norm_mix, m_norm_mlp, m_w_in_even, m_w_out_even, m_ret_decay_logit, m_ret_norm, m_swa_q_norm, m_swa_k_norm, m_swa_sink, m_t5_table, m_w_in_odd, m_w_out_odd, m_ax_q_norm, m_ax_k_norm, m_w_mlp_up, m_w_mlp_down, v_norm_mix, v_norm_mlp, v_w_in_even, v_w_out_even, v_ret_decay_logit, v_ret_norm, v_swa_q_norm, v_swa_k_norm, v_swa_sink, v_t5_table, v_w_in_odd, v_w_out_odd, v_ax_q_norm, v_ax_k_norm, v_w_mlp_up, v_w_mlp_down):
    given = dict(locals())
    weights = {n: given[n] for n in MATRICES + SMALL}
    moments_m = {n: given["m_" + n] for n in MATRICES + SMALL}
    moments_v = {n: given["v_" + n] for n in MATRICES + SMALL}

    def shard(table, key):
        arg, layer = MATRIX_OF[key]
        return table[arg][layer]

    gathered = _all_gather([shard(weights, k) for k in GATHER_FIRST])
    W = {k: _assemble(k, g) for k, g in zip(GATHER_FIRST, gathered)}
    late_keys = [k for keys in GATHER_WHILE.values() for k in keys]
    cast = dict(zip(late_keys, _cast_shards([shard(weights, k) for k in late_keys])))
    late = {stage: [(k, cast[k]) for k in keys] for stage, keys in GATHER_WHILE.items()}
    P = {"norm_mix": norm_mix, "norm_mlp": norm_mlp, "ret_decay_logit": ret_decay_logit[0], "ret_norm": ret_norm,
         "swa_q_norm": swa_q_norm, "swa_k_norm": swa_k_norm, "swa_sink": swa_sink, "t5_table": t5_table,
         "ax_q_norm": ax_q_norm, "ax_k_norm": ax_k_norm}

    loss_row, dx, landed, dP = _local_step(x[0], loss_target[0], W, P, late)

    per_key = {k: _sum_and_adamw(landed[k], shard(weights, k), shard(moments_m, k), shard(moments_v, k), "adamw_" + k)
               for k in MATRIX_OF}
    grads, deltas, new_m, new_v = {}, {}, {}, {}
    for i, out in enumerate((grads, deltas, new_m, new_v)):
        for n in MATRICES:
            out[n] = jnp.stack([per_key[k][i] for k, (arg, _) in MATRIX_OF.items() if arg == n])

    dP["ret_decay_logit"] = dP["ret_decay_logit"][None]
    total = _all_reduce_small(_pack_small(dP, loss_row))
    loss = total[LOSS_ROW, 0]
    small_d, small_m, small_v = _adamw_small(_pack_small(weights), total, _pack_small(moments_m), _pack_small(moments_v))
    like = {n: weights[n] for n in SMALL}
    for out, buf in ((grads, total), (deltas, small_d), (new_m, small_m), (new_v, small_v)):
        out.update(_unpack_small(buf, like))

    order = ("norm_mix", "norm_mlp", "w_in_even", "w_out_even", "ret_decay_logit", "ret_norm", "swa_q_norm", "swa_k_norm",
             "swa_sink", "t5_table", "w_in_odd", "w_out_odd", "ax_q_norm", "ax_k_norm", "w_mlp_up", "w_mlp_down")
    return (loss, dx[None], *[grads[n] for n in order], *[deltas[n] for n in order],
            *[new_m[n] for n in order], *[new_v[n] for n in order])
```

```python
import math

import jax
import jax.numpy as jnp
from jax import lax
from jax.experimental import pallas as pl
from jax.experimental.pallas import tpu as pltpu

F32 = jnp.float32
BF = jnp.bfloat16

D_MODEL = 1024
HEAD_DIM = 128
EPS = 1e-6
NEG_INF = -1e30
RET_HEADS, RET_DK, RET_DV = 4, 128, 256
RET_THETA = 10000.0
SWA_HEADS, SWA_KV_HEADS, WINDOW, BLOCK = 8, 2, 128, 128
T5_BUCKETS, T5_MAX_DIST = 32, 128
AX_HEADS, AX_KV_HEADS, AX_THETA, GRID_W = 8, 2, 10000.0, 64
D_FF = 4096
ATT_SCALE = HEAD_DIM ** -0.5
LN2 = math.log(2.0)
AX_SCALE = ATT_SCALE / LN2
RET_SCALE = RET_DK ** -0.5
N_DEV = 8

ADAM_LR, ADAM_B1, ADAM_B2, ADAM_EPS, ADAM_WD, ADAM_STEP = 0.001, 0.9, 0.999, 1e-08, 0.01, 10

MIB = 1024 * 1024
VMEM_SMALL = 40 * MIB
VMEM_LARGE = 56 * MIB

OFF_QA, OFF_KA, OFF_VA, OFF_GA, OFF_QB, OFF_KB, OFF_VB = 0, 512, 1024, 2048, 3072, 4096, 4352
EVEN_IN = 4608
ODD_IN = 1536

NT = (((1,), (1,)), ((), ()))
TN = (((0,), (0,)), ((), ()))
NN = (((1,), (0,)), ((), ()))


def _dot(a, b, dims=NN):
    return lax.dot_general(a, b, dims, preferred_element_type=F32)


def _params(sem=None, vmem=VMEM_SMALL):
    return pltpu.CompilerParams(dimension_semantics=sem, vmem_limit_bytes=vmem)


def _pick(n, prefs):
    for p in prefs:
        if n % p == 0:
            return p
    return n


def _row_sum(x):
    return jnp.sum(x, axis=0, keepdims=True)


def _all_sum(x):
    return jnp.sum(jnp.sum(x, axis=0, keepdims=True), axis=1, keepdims=True)


def _sigmoid(x):
    return 1.0 / (1.0 + jnp.exp(-x))


SMEM_SPEC = pl.BlockSpec(memory_space=pltpu.SMEM)
ANY_SPEC = pl.BlockSpec(memory_space=pl.ANY)
VMEM_SPEC = pl.BlockSpec(memory_space=pltpu.VMEM)
MESH = pl.DeviceIdType.MESH


def _my_place():
    return lax.axis_index("x"), lax.axis_index("y"), lax.axis_index("c")


def _flip(place, k):
    x, y, c = place
    return (1 - x if k & 4 else x, 1 - y if k & 2 else y, 1 - c if k & 1 else c)


def _index(place):
    x, y, c = place
    return 4 * x + 2 * y + c


class _Exchange:
    def __init__(self, sources, gather):
        self.sources, self.gather, self.n = list(sources), gather, len(sources)
        self.out_shape = [jax.ShapeDtypeStruct(((N_DEV,) + s.shape) if gather else s.shape, s.dtype) for s in self.sources]
        self.scratch = [pltpu.SemaphoreType.DMA((self.n, 7)), pltpu.SemaphoreType.DMA((self.n, 7)),
                        pltpu.SemaphoreType.DMA((self.n,))]

    def _source(self, ins, a, place):
        return ins[a] if self.gather else ins[a].at[_index(place)]

    def _local(self, ins, outs, sems):
        me = _my_place()
        return [pltpu.make_async_copy(self._source(ins, a, me), outs[a].at[_index(me)], sems[2].at[a]) for a in range(self.n)]

    def _remote(self, ins, outs, sems, arriving):
        send_sems, recv_sems, _ = sems
        me = _my_place()
        copies = []
        for a in range(self.n):
            for k in range(1, N_DEV):
                peer = _flip(me, k)
                copies.append(pltpu.make_async_remote_copy(
                    src_ref=self._source(ins, a, peer), dst_ref=outs[a].at[_index(peer if arriving else me)],
                    send_sem=send_sems.at[a, k - 1], recv_sem=recv_sems.at[a, k - 1], device_id=peer, device_id_type=MESH))
        return copies

    def start(self, ins, outs, sems):
        for cp in self._local(ins, outs, sems) + self._remote(ins, outs, sems, arriving=False):
            cp.start()

    def wait(self, ins, outs, sems):
        for cp in self._remote(ins, outs, sems, arriving=True):
            cp.wait_recv()
        for cp in self._remote(ins, outs, sems, arriving=False):
            cp.wait_send()
        for cp in self._local(ins, outs, sems):
            cp.wait()


def _carry_call(body, args, *, name, grid, in_specs, out_specs, out_shape, scratch_shapes=(), vmem=VMEM_SMALL,
                semantics=None, carried=None):
    if carried is None:
        outs = pl.pallas_call(body, name=name, grid=grid, in_specs=in_specs, out_specs=out_specs, out_shape=out_shape,
                              scratch_shapes=list(scratch_shapes), compiler_params=_params(semantics, vmem))(*args)
        return list(outs), []
    ni, no, ns, nc = len(in_specs), len(out_specs), len(scratch_shapes), carried.n

    def full_body(*refs):
        ins, cin = refs[:ni], refs[ni:ni + nc]
        outs, cout = refs[ni + nc:ni + nc + no], refs[ni + nc + no:ni + 2 * nc + no]
        scratch, sems = refs[ni + 2 * nc + no:ni + 2 * nc + no + ns], refs[ni + 2 * nc + no + ns:]
        ids = [pl.program_id(d) for d in range(len(grid))]
        first, last = ids[0] == 0, ids[0] == grid[0] - 1
        for d in range(1, len(grid)):
            first, last = first & (ids[d] == 0), last & (ids[d] == grid[d] - 1)

        @pl.when(first)
        def _():
            carried.start(cin, cout, sems)

        body(*ins, *outs, *scratch)

        @pl.when(last)
        def _():
            carried.wait(cin, cout, sems)

    outs = pl.pallas_call(
        full_body, name=name, grid=grid, in_specs=list(in_specs) + [ANY_SPEC] * nc,
        out_specs=list(out_specs) + [ANY_SPEC] * nc, out_shape=list(out_shape) + carried.out_shape,
        scratch_shapes=list(scratch_shapes) + carried.scratch,
        compiler_params=_params(("arbitrary",) * len(grid), vmem))(*args, *carried.sources)
    return list(outs[:no]), list(outs[no:])


def _mm(a, b, *, name, ta=False, tb=False, out_dtypes=(F32,), extras=(), rows=(), n_row_sums=0, epilogue=None,
        tm=1024, tn=1024, tk=1024, vmem=VMEM_SMALL, carried=None):
    M, K = (a.shape[1], a.shape[0]) if ta else a.shape
    N = b.shape[0] if tb else b.shape[1]
    assert K == (b.shape[1] if tb else b.shape[0])
    tm = _pick(M, (tm, 512, 256, 128))
    tn = _pick(N, (tn, 1536, 512, 384, 256, 128))
    tk = _pick(K, (2048, 2304, tk, 1536, 512, 256, 128))
    nk = K // tk
    assert n_row_sums == 0 or tn == N
    ne, nr, no = len(extras), len(rows), len(out_dtypes)
    dims = (((0 if ta else 1,), (1 if tb else 0,)), ((), ()))
    if epilogue is None:
        epilogue = lambda acc: (acc,)

    def body(a_ref, b_ref, *rest):
        extra_refs, out_refs = rest[:ne + nr], rest[ne + nr:ne + nr + no]
        sum_refs = rest[ne + nr + no:ne + nr + no + n_row_sums]
        first_tile = pl.program_id(0) == 0

        def finish(acc):
            outs = epilogue(acc, *[r[...] for r in extra_refs])
            for o_ref, o in zip(out_refs, outs[:no]):
                o_ref[...] = o.astype(o_ref.dtype)
            for s_ref, contribution in zip(sum_refs, outs[no:]):
                @pl.when(first_tile)
                def _(s_ref=s_ref, contribution=contribution):
                    s_ref[...] = contribution

                @pl.when(jnp.logical_not(first_tile))
                def _(s_ref=s_ref, contribution=contribution):
                    s_ref[...] += contribution

        part = _dot(a_ref[...], b_ref[...], dims)
        if nk == 1:
            finish(part)
        else:
            acc_ref = rest[-1]
            k = pl.program_id(2)

            @pl.when(k == 0)
            def _():
                acc_ref[...] = part

            @pl.when(k > 0)
            def _():
                acc_ref[...] += part

            @pl.when(k == nk - 1)
            def _():
                finish(acc_ref[...])

    a_spec = pl.BlockSpec((tk, tm), lambda i, j, k: (k, i)) if ta else pl.BlockSpec((tm, tk), lambda i, j, k: (i, k))
    b_spec = pl.BlockSpec((tn, tk), lambda i, j, k: (j, k)) if tb else pl.BlockSpec((tk, tn), lambda i, j, k: (k, j))
    o_spec = pl.BlockSpec((tm, tn), lambda i, j, k: (i, j))
    row_spec = pl.BlockSpec((1, tn), lambda i, j, k: (0, j))
    outs, landed = _carry_call(
        body, (a, b, *extras, *rows), name=name, grid=(M // tm, N // tn, nk),
        in_specs=[a_spec, b_spec] + [o_spec] * ne + [row_spec] * nr,
        out_specs=[o_spec] * no + [row_spec] * n_row_sums,
        out_shape=[jax.ShapeDtypeStruct((M, N), dt) for dt in out_dtypes] + [jax.ShapeDtypeStruct((1, N), F32)] * n_row_sums,
        scratch_shapes=[pltpu.VMEM((tm, tn), F32)] if nk > 1 else [],
        vmem=vmem, semantics=("arbitrary" if n_row_sums else "parallel", "parallel", "arbitrary"), carried=carried)
    outs = outs[0] if len(outs) == 1 else outs
    return outs if carried is None else (outs, landed)


def _rms_fwd(x, g, name):
    S, Dm = x.shape
    tr = _pick(S, (512,))

    def body(x_ref, g_ref, o_ref):
        xv = x_ref[...]
        r = lax.rsqrt(jnp.mean(xv * xv, axis=-1, keepdims=True) + EPS)
        o_ref[...] = (xv * r * g_ref[...]).astype(o_ref.dtype)

    row = pl.BlockSpec((tr, Dm), lambda i: (i, 0))
    return pl.pallas_call(
        body, name=name, grid=(S // tr,),
        in_specs=[row, pl.BlockSpec((1, Dm), lambda i: (0, 0))], out_specs=row,
        out_shape=jax.ShapeDtypeStruct((S, Dm), BF), compiler_params=_params(("parallel",)),
    )(x, g)


def _partner(x, half):
    if half == 64:
        return pltpu.roll(x, 64, 1)
    lane = lax.broadcasted_iota(jnp.int32, x.shape, 1)
    return jnp.where((lane % (2 * half)) < half, pltpu.roll(x, 128 - half, 1), pltpu.roll(x, half, 1))


def _rope(x, cos, sin, half):
    return x * cos + _partner(x, half) * sin


def _rope_t(dy, cos, sin, half):
    return dy * cos - _partner(dy, half) * sin


def _head_norm(x):
    r = lax.rsqrt(jnp.mean(x * x, axis=-1, keepdims=True) + EPS)
    return x * r, r


def _head_norm_bwd(dxh, xh, r):
    return r * (dxh - xh * jnp.mean(dxh * xh, axis=-1, keepdims=True))


def _cols(ref, off, width=HEAD_DIM):
    return ref[:, off:off + width]


def _prep_even_fwd(proj, cos, sin, gq, gk, carried=None):
    S = proj.shape[0]
    tr = _pick(S, (256,))

    def body(p_ref, cos_ref, sin_ref, gq_ref, gk_ref, qr_ref, kr_ref, vr_ref, qs_ref, ks_ref, vs_ref):
        cos_v, sin_v = cos_ref[...], sin_ref[...]
        for h in range(RET_HEADS):
            o = h * RET_DK
            qr_ref[:, o:o + RET_DK] = _rope(_cols(p_ref, OFF_QA + o), cos_v, sin_v, 64).astype(qr_ref.dtype)
            kr_ref[:, o:o + RET_DK] = (_rope(_cols(p_ref, OFF_KA + o), cos_v, sin_v, 64) * RET_SCALE).astype(kr_ref.dtype)
        vr_ref[...] = p_ref[:, OFF_VA:OFF_VA + 1024].astype(vr_ref.dtype)
        for h in range(SWA_HEADS):
            o = h * HEAD_DIM
            xh, _ = _head_norm(_cols(p_ref, OFF_QB + o))
            qs_ref[:, o:o + HEAD_DIM] = (xh * gq_ref[...] * ATT_SCALE).astype(qs_ref.dtype)
        for h in range(SWA_KV_HEADS):
            o = h * HEAD_DIM
            xh, _ = _head_norm(_cols(p_ref, OFF_KB + o))
            ks_ref[:, o:o + HEAD_DIM] = (xh * gk_ref[...]).astype(ks_ref.dtype)
        vs_ref[...] = p_ref[:, OFF_VB:OFF_VB + 256].astype(vs_ref.dtype)

    def row(w):
        return pl.BlockSpec((tr, w), lambda i: (i, 0))

    vec = pl.BlockSpec((1, HEAD_DIM), lambda i: (0, 0))
    widths = (512, 512, 1024, 1024, 256, 256)
    return _carry_call(
        body, (proj, cos, sin, gq, gk), name="prep_even_fwd", grid=(S // tr,),
        in_specs=[row(EVEN_IN), row(128), row(128), vec, vec],
        out_specs=[row(w) for w in widths],
        out_shape=[jax.ShapeDtypeStruct((S, w), BF) for w in widths],
        semantics=("parallel",), carried=carried)


def _prep_even_bwd(proj, cos, sin, gq, gk, dqr, dkr, dvr, dga, dqs, dks, dvs):
    S = proj.shape[0]
    tr = _pick(S, (256,))

    def body(p_ref, cos_ref, sin_ref, gq_ref, gk_ref, dqr_ref, dkr_ref, dvr_ref, dga_ref, dqs_ref, dks_ref,
             dvs_ref, dp_ref, dgq_ref, dgk_ref):
        cos_v, sin_v = cos_ref[...], sin_ref[...]
        dt = dp_ref.dtype
        for h in range(RET_HEADS):
            o = h * RET_DK
            dp_ref[:, OFF_QA + o:OFF_QA + o + RET_DK] = _rope_t(_cols(dqr_ref, o).astype(F32), cos_v, sin_v, 64).astype(dt)
            dp_ref[:, OFF_KA + o:OFF_KA + o + RET_DK] = _rope_t(_cols(dkr_ref, o).astype(F32) * RET_SCALE, cos_v, sin_v, 64).astype(dt)
        dp_ref[:, OFF_VA:OFF_VA + 1024] = dvr_ref[...].astype(dt)
        dp_ref[:, OFF_GA:OFF_GA + 1024] = dga_ref[...].astype(dt)
        dgq = jnp.zeros((1, HEAD_DIM), F32)
        for h in range(SWA_HEADS):
            o = h * HEAD_DIM
            xh, r = _head_norm(_cols(p_ref, OFF_QB + o))
            dy = _cols(dqs_ref, o).astype(F32) * ATT_SCALE
            dgq = dgq + _row_sum(dy * xh)
            dp_ref[:, OFF_QB + o:OFF_QB + o + HEAD_DIM] = _head_norm_bwd(dy * gq_ref[...], xh, r).astype(dt)
        dgk = jnp.zeros((1, HEAD_DIM), F32)
        for h in range(SWA_KV_HEADS):
            o = h * HEAD_DIM
            xh, r = _head_norm(_cols(p_ref, OFF_KB + o))
            dy = _cols(dks_ref, o)
            dgk = dgk + _row_sum(dy * xh)
            dp_ref[:, OFF_KB + o:OFF_KB + o + HEAD_DIM] = _head_norm_bwd(dy * gk_ref[...], xh, r).astype(dt)
        dp_ref[:, OFF_VB:OFF_VB + 256] = dvs_ref[...].astype(dt)

        @pl.when(pl.program_id(0) == 0)
        def _():
            dgq_ref[...] = jnp.zeros_like(dgq_ref)
            dgk_ref[...] = jnp.zeros_like(dgk_ref)

        dgq_ref[...] += dgq
        dgk_ref[...] += dgk

    def row(w):
        return pl.BlockSpec((tr, w), lambda i: (i, 0))

    vec = pl.BlockSpec((1, HEAD_DIM), lambda i: (0, 0))
    return pl.pallas_call(
        body, name="prep_even_bwd", grid=(S // tr,),
        in_specs=[row(EVEN_IN), row(128), row(128), vec, vec, row(512), row(512), row(1024), row(1024),
                  row(1024), row(256), row(256)],
        out_specs=[row(EVEN_IN), vec, vec],
        out_shape=[jax.ShapeDtypeStruct((S, EVEN_IN), BF), jax.ShapeDtypeStruct((1, HEAD_DIM), F32),
                   jax.ShapeDtypeStruct((1, HEAD_DIM), F32)],
        compiler_params=_params(("arbitrary",)),
    )(proj, cos, sin, gq, gk, dqr, dkr, dvr, dga, dqs, dks, dvs)


RET_CHUNK = 512
def _log_sigmoid_tile(logit_tile):
    def body(x_ref, o_ref):
        xv = x_ref[...]
        t = jnp.exp(-jnp.abs(xv))
        log1p_t = jnp.where(t < 1e-3, t * (1.0 - 0.5 * t), jnp.log(1.0 + t))
        o_ref[...] = jnp.minimum(xv, 0.0) - log1p_t

    full = pl.BlockSpec((8, 128), lambda: (0, 0))
    return pl.pallas_call(body, name="log_sigmoid", in_specs=[full], out_specs=full,
                          out_shape=jax.ShapeDtypeStruct((8, 128), F32))(logit_tile)


def _decay(diff, lf, lb):
    return jnp.exp(jnp.where(diff >= 0, lf * diff, -(lb * diff)))


def _col_iota(n):
    return lax.broadcasted_iota(jnp.int32, (n, 1), 0).astype(F32)


def _ret_scan(x, z, lg, asc, desc, name):
    S = x.shape[0]
    C = _pick(S, (RET_CHUNK,))
    nc = S // C
    (arow, aoff), (drow, doff) = asc, desc

    def body(lg_ref, xa_ref, za_ref, xd_ref, zd_ref, asc_ref, desc_ref, sa_ref, sd_ref):
        t = pl.program_id(0)

        @pl.when(t == 0)
        def _():
            sa_ref[...] = jnp.zeros_like(sa_ref)
            sd_ref[...] = jnp.zeros_like(sd_ref)

        j = _col_iota(C)
        for h in range(RET_HEADS):
            la, ld = lg_ref[arow, h], lg_ref[drow, h]
            kc, vc = slice(h * RET_DK, (h + 1) * RET_DK), slice(h * RET_DV, (h + 1) * RET_DV)
            asc_ref[h, 0] = sa_ref[h]
            desc_ref[h, 0] = sd_ref[h]
            xa = (xa_ref[:, kc].astype(F32) * jnp.exp(la * (C - 1 + aoff - j))).astype(xa_ref.dtype)
            xd = (xd_ref[:, kc].astype(F32) * jnp.exp(ld * (j + doff))).astype(xd_ref.dtype)
            sa_ref[h] = jnp.exp(jnp.full((1, RET_DV), la * C, F32)) * sa_ref[h] + _dot(xa, za_ref[:, vc], TN)
            sd_ref[h] = jnp.exp(jnp.full((1, RET_DV), ld * C, F32)) * sd_ref[h] + _dot(xd, zd_ref[:, vc], TN)

    state = jax.ShapeDtypeStruct((RET_HEADS, nc, RET_DK, RET_DV), F32)
    qk_w, v_w = RET_HEADS * RET_DK, RET_HEADS * RET_DV
    return pl.pallas_call(
        body, name=name, grid=(nc,),
        in_specs=[SMEM_SPEC,
                  pl.BlockSpec((C, qk_w), lambda t: (t, 0)), pl.BlockSpec((C, v_w), lambda t: (t, 0)),
                  pl.BlockSpec((C, qk_w), lambda t: (nc - 1 - t, 0)), pl.BlockSpec((C, v_w), lambda t: (nc - 1 - t, 0))],
        out_specs=[pl.BlockSpec((RET_HEADS, 1, RET_DK, RET_DV), lambda t: (0, t, 0, 0)),
                   pl.BlockSpec((RET_HEADS, 1, RET_DK, RET_DV), lambda t: (0, nc - 1 - t, 0, 0))],
        out_shape=[state, state],
        scratch_shapes=[pltpu.VMEM((RET_HEADS, RET_DK, RET_DV), F32), pltpu.VMEM((RET_HEADS, RET_DK, RET_DV), F32)],
        compiler_params=_params(("arbitrary",)),
    )(lg, x, z, x, z)


def _head_cols(h):
    return slice(h * RET_DK, (h + 1) * RET_DK), slice(h * RET_DV, (h + 1) * RET_DV)


def _ret_fwd(q, k, v, lg, sf, sb, proj, gn, carried=None):
    S = q.shape[0]
    C = _pick(S, (RET_CHUNK,))

    def body(lg_ref, q_ref, k_ref, v_ref, sf_ref, sb_ref, gate_ref, gn_ref, y_ref, o_ref):
        diff = (lax.broadcasted_iota(jnp.int32, (C, C), 0) - lax.broadcasted_iota(jnp.int32, (C, C), 1)).astype(F32)
        r = _col_iota(C)
        for h in range(RET_HEADS):
            lf, lb = lg_ref[0, h], lg_ref[1, h]
            kc, vc = _head_cols(h)
            qv = q_ref[:, kc]
            dt = qv.dtype
            y = _dot((_dot(qv, k_ref[:, kc], NT) * _decay(diff, lf, lb)).astype(dt), v_ref[:, vc])
            qf = qv.astype(F32)
            y = y + _dot((qf * jnp.exp(lf * (r + 1.0))).astype(dt), sf_ref[h, 0].astype(dt))
            y = y + _dot((qf * jnp.exp(lb * (C - r))).astype(dt), sb_ref[h, 0].astype(dt))
            y_ref[:, vc] = y
            yh, _ = _head_norm(y)
            gate = gate_ref[:, vc]
            o_ref[:, vc] = (gate * _sigmoid(gate) * (yh * gn_ref[:, vc])).astype(o_ref.dtype)

    state = pl.BlockSpec((RET_HEADS, 1, RET_DK, RET_DV), lambda c: (0, c, 0, 0))
    qk = pl.BlockSpec((C, RET_HEADS * RET_DK), lambda c: (c, 0))
    wide = pl.BlockSpec((C, 1024), lambda c: (c, 0))
    (y, o), landed = _carry_call(
        body, (lg, q, k, v, sf, sb, proj, gn), name="retention_fwd", grid=(S // C,),
        in_specs=[SMEM_SPEC, qk, qk, wide, state, state, pl.BlockSpec((C, 1024), lambda c: (c, OFF_GA // 1024)),
                  pl.BlockSpec((1, 1024), lambda c: (0, 0))],
        out_specs=[wide, wide],
        out_shape=[jax.ShapeDtypeStruct((S, 1024), F32), jax.ShapeDtypeStruct((S, 1024), BF)],
        semantics=("parallel",), carried=carried)
    return y, o, landed


def _ret_bwd(q, k, v, dy, lg, logit, sf, sb, hf, hb):
    S = q.shape[0]
    C = _pick(S, (RET_CHUNK,))
    nc = S // C

    def body(lg_ref, logit_ref, q_ref, k_ref, v_ref, dy_ref, sf_ref, sb_ref, hf_ref, hb_ref,
             dq_ref, dk_ref, dv_ref, dlg_ref, acc_ref):
        c = pl.program_id(0)

        @pl.when(c == 0)
        def _():
            acc_ref[...] = jnp.zeros_like(acc_ref)

        diff = (lax.broadcasted_iota(jnp.int32, (C, C), 0) - lax.broadcasted_iota(jnp.int32, (C, C), 1)).astype(F32)
        r = _col_iota(C)
        for h in range(RET_HEADS):
            one_head(h, diff, r, lg_ref, q_ref, k_ref, v_ref, dy_ref, sf_ref, sb_ref, hf_ref, hb_ref,
                     dq_ref, dk_ref, dv_ref, acc_ref)

        @pl.when(c == nc - 1)
        def _():
            for h in range(RET_HEADS):
                for d in range(2):
                    gate = 1.0 / (1.0 + jnp.exp(jnp.full((8, 128), logit_ref[d, h], F32)))
                    dlg_ref[h, d] = acc_ref[h, d] * gate

    def one_head(h, diff, r, lg_ref, q_ref, k_ref, v_ref, dy_ref, sf_ref, sb_ref, hf_ref, hb_ref,
                 dq_ref, dk_ref, dv_ref, acc_ref):
        lf, lb = lg_ref[0, h], lg_ref[1, h]
        kc, vc = _head_cols(h)
        qv, kv, vv, dyv = q_ref[:, kc], k_ref[:, kc], v_ref[:, vc], dy_ref[:, vc]
        dt = qv.dtype
        qf, kf = qv.astype(F32), kv.astype(F32)
        dec = _decay(diff, lf, lb)
        sc = _dot(qv, kv, NT) * dec
        dp = _dot(dyv, vv, NT)
        da = (dp * dec).astype(dt)
        dq = _dot(da, kv)
        dk = _dot(da, qv, TN)
        dv = _dot(sc.astype(dt), dyv, TN)
        w = sc * dp * diff
        tot_w, tot_f = _all_sum(w), _all_sum(jnp.where(diff >= 0, w, 0.0))
        d_lf, d_lb = tot_f, tot_f - tot_w
        a, b = jnp.exp(lf * (r + 1.0)), jnp.exp(lb * (C - r))
        e, f = jnp.exp(lf * (C - 1.0 - r)), jnp.exp(lb * r)
        sfv, sbv, hfv, hbv = sf_ref[h, 0], sb_ref[h, 0], hf_ref[h, 0], hb_ref[h, 0]
        t_f, t_b = _dot(dyv, sfv.astype(dt), NT), _dot(dyv, sbv.astype(dt), NT)
        u_f, u_b = _dot(vv, hfv.astype(dt), NT), _dot(vv, hbv.astype(dt), NT)
        dq_ref[:, kc] = (dq + a * t_f + b * t_b).astype(dq_ref.dtype)
        dk_ref[:, kc] = (dk + e * u_f + f * u_b).astype(dk_ref.dtype)
        dv_ref[:, vc] = (dv + _dot((kf * e).astype(dt), hfv.astype(dt)) + _dot((kf * f).astype(dt), hbv.astype(dt))).astype(dv_ref.dtype)
        row_q_f = jnp.sum(qf * t_f, axis=-1, keepdims=True)
        row_q_b = jnp.sum(qf * t_b, axis=-1, keepdims=True)
        row_k_f = jnp.sum(kf * u_f, axis=-1, keepdims=True)
        row_k_b = jnp.sum(kf * u_b, axis=-1, keepdims=True)
        gf_c = jnp.exp(jnp.full((1, 1), lf * C, F32))
        gb_c = jnp.exp(jnp.full((1, 1), lb * C, F32))
        d_lf = d_lf + _all_sum((r + 1.0) * a * row_q_f + (C - 1.0 - r) * e * row_k_f) + C * gf_c * _all_sum(hfv * sfv)
        d_lb = d_lb + _all_sum((C - r) * b * row_q_b + r * f * row_k_b) + C * gb_c * _all_sum(hbv * sbv)
        acc_ref[h, 0] += jnp.broadcast_to(d_lf, (8, 128))
        acc_ref[h, 1] += jnp.broadcast_to(d_lb, (8, 128))

    state = pl.BlockSpec((RET_HEADS, 1, RET_DK, RET_DV), lambda c: (0, c, 0, 0))
    qk = pl.BlockSpec((C, RET_HEADS * RET_DK), lambda c: (c, 0))
    vy = pl.BlockSpec((C, RET_HEADS * RET_DV), lambda c: (c, 0))
    return pl.pallas_call(
        body, name="retention_bwd", grid=(nc,),
        in_specs=[SMEM_SPEC, SMEM_SPEC, qk, qk, vy, vy, state, state, state, state],
        out_specs=[qk, qk, vy, pl.BlockSpec((RET_HEADS, 2, 8, 128), lambda c: (0, 0, 0, 0))],
        out_shape=[jax.ShapeDtypeStruct((S, RET_HEADS * RET_DK), BF), jax.ShapeDtypeStruct((S, RET_HEADS * RET_DK), BF),
                   jax.ShapeDtypeStruct((S, RET_HEADS * RET_DV), BF),
                   jax.ShapeDtypeStruct((RET_HEADS, 2, 8, 128), F32)],
        scratch_shapes=[pltpu.VMEM((RET_HEADS, 2, 8, 128), F32)],
        compiler_params=_params(("arbitrary",)),
    )(lg, logit, q, k, v, dy, sf, sb, hf, hb)


def _ret_post_bwd(y, proj, gn, do):
    S = y.shape[0]
    tr = _pick(S, (512,))

    def body(y_ref, g_ref, gn_ref, do_ref, dy_ref, dg_ref, dgn_ref):
        @pl.when(pl.program_id(0) == 0)
        def _():
            dgn_ref[...] = jnp.zeros_like(dgn_ref)

        for h in range(RET_HEADS):
            o = h * RET_DV
            yh, r = _head_norm(_cols(y_ref, o, RET_DV))
            gate = _cols(g_ref, o, RET_DV)
            gnh = gn_ref[:, o:o + RET_DV]
            dout = _cols(do_ref, o, RET_DV).astype(F32)
            sg = _sigmoid(gate)
            dz = dout * (gate * sg)
            dg_ref[:, o:o + RET_DV] = (dout * (yh * gnh) * (sg * (1.0 + gate * (1.0 - sg)))).astype(dg_ref.dtype)
            dgn_ref[:, o:o + RET_DV] += _row_sum(dz * yh)
            dy_ref[:, o:o + RET_DV] = _head_norm_bwd(dz * gnh, yh, r).astype(dy_ref.dtype)

    row = pl.BlockSpec((tr, 1024), lambda i: (i, 0))
    vec = pl.BlockSpec((1, 1024), lambda i: (0, 0))
    return pl.pallas_call(
        body, name="retention_post_bwd", grid=(S // tr,),
        in_specs=[row, pl.BlockSpec((tr, 1024), lambda i: (i, OFF_GA // 1024)), vec, row],
        out_specs=[row, row, vec],
        out_shape=[jax.ShapeDtypeStruct((S, 1024), BF), jax.ShapeDtypeStruct((S, 1024), BF),
                   jax.ShapeDtypeStruct((1, 1024), F32)],
        compiler_params=_params(("arbitrary",)),
    )(y, proj, gn, do)


def _t5_bucket_map():
    r = jnp.arange(BLOCK)
    j = jnp.arange(3 * BLOCK)
    rel = j[None, :] - BLOCK - r[:, None]
    nb = T5_BUCKETS // 2
    max_exact = nb // 2
    ret = jnp.where(rel > 0, nb, 0)
    n = jnp.abs(rel)
    nf = jnp.maximum(n, 1).astype(jnp.float32)
    large = max_exact + (jnp.log(nf / max_exact) / math.log(T5_MAX_DIST / max_exact)
                         * (nb - max_exact)).astype(jnp.int32)
    large = jnp.minimum(large, nb - 1)
    bucket = ret + jnp.where(n < max_exact, n, large)
    return jnp.where(jnp.abs(rel) <= WINDOW, bucket, -1).astype(jnp.int32)


SWA_G = SWA_HEADS // SWA_KV_HEADS
SWA_LANES = SWA_G * BLOCK


def _t5_bias(table, bucket_t):
    def body(t_ref, b_ref, o_ref):
        bk = b_ref[...]
        for h in range(SWA_HEADS):
            acc = jnp.full(bk.shape, NEG_INF, F32)
            for b in range(T5_BUCKETS):
                acc = jnp.where(bk == b, t_ref[b, h], acc)
            o_ref[h // SWA_G, :, (h % SWA_G) * BLOCK:(h % SWA_G + 1) * BLOCK] = acc

    return pl.pallas_call(
        body, name="t5_bias", in_specs=[SMEM_SPEC, pl.BlockSpec((3 * BLOCK, BLOCK), lambda: (0, 0))],
        out_specs=pl.BlockSpec((SWA_KV_HEADS, 3 * BLOCK, SWA_LANES), lambda: (0, 0, 0)),
        out_shape=jax.ShapeDtypeStruct((SWA_KV_HEADS, 3 * BLOCK, SWA_LANES), F32),
    )(table, bucket_t)


def _t5_table_grad(dbias, bucket_t):
    def body(d_ref, b_ref, o_ref):
        bk = b_ref[...]
        lane = lax.broadcasted_iota(jnp.int32, (1, 128), 1)
        for b in range(T5_BUCKETS):
            hit = bk == b
            row = jnp.zeros((1, 128), F32)
            for h in range(SWA_HEADS):
                d = d_ref[h // SWA_G, :, (h % SWA_G) * BLOCK:(h % SWA_G + 1) * BLOCK]
                row = row + jnp.where(lane == h, _all_sum(jnp.where(hit, d, 0.0)), 0.0)
            o_ref[b:b + 1, :] = row

    return pl.pallas_call(
        body, name="t5_table_grad",
        in_specs=[pl.BlockSpec((SWA_KV_HEADS, 3 * BLOCK, SWA_LANES), lambda: (0, 0, 0)),
                  pl.BlockSpec((3 * BLOCK, BLOCK), lambda: (0, 0))],
        out_specs=pl.BlockSpec((T5_BUCKETS, 128), lambda: (0, 0)),
        out_shape=jax.ShapeDtypeStruct((T5_BUCKETS, 128), F32),
    )(dbias, bucket_t)


def _swa_scores(i, nb, q4, kw, bias_t, sink_row):
    s = _dot(kw, q4, NT) + bias_t
    row = lax.broadcasted_iota(jnp.int32, s.shape, 0)
    first_row = jnp.where(i == 0, BLOCK, 0)
    end_row = jnp.where(i == nb - 1, 2 * BLOCK, 3 * BLOCK)
    s = jnp.where((row < first_row) | (row >= end_row), NEG_INF, s)
    m = jnp.maximum(jnp.max(s, axis=0, keepdims=True), sink_row)
    p = jnp.exp(s - m)
    e_sink = jnp.exp(sink_row - m)
    inv = 1.0 / (jnp.sum(p, axis=0, keepdims=True) + e_sink)
    return p * inv, e_sink * inv


def _swa_group(q_ref, sink_ref, kh):
    heads = range(kh * SWA_G, (kh + 1) * SWA_G)
    q4 = jnp.concatenate([_cols(q_ref, h * HEAD_DIM) for h in heads], axis=0)
    sink_row = jnp.concatenate([jnp.full((1, BLOCK), sink_ref[0, h], F32) for h in heads], axis=1)
    return q4, sink_row


def _swa_unstack(ref, kh, x_t):
    for g in range(SWA_G):
        h = kh * SWA_G + g
        ref[:, h * HEAD_DIM:(h + 1) * HEAD_DIM] = x_t[:, g * BLOCK:(g + 1) * BLOCK].T.astype(ref.dtype)


def _swa_window(ref, i, nb, off):
    prev, nxt = jnp.maximum(i - 1, 0), jnp.minimum(i + 1, nb - 1)
    rows = [pl.ds(pl.multiple_of(b * BLOCK, BLOCK), BLOCK) for b in (prev, i, nxt)]
    return jnp.concatenate([ref[r, off:off + HEAD_DIM] for r in rows], axis=0), rows


def _swa_fwd(q, k, v, bias, sink, carried=None):
    S = q.shape[0]
    nb = S // BLOCK

    def body(sink_ref, q_ref, k_ref, v_ref, bias_ref, o_ref):
        i = pl.program_id(0)
        for kh in range(SWA_KV_HEADS):
            kw, _ = _swa_window(k_ref, i, nb, kh * HEAD_DIM)
            vw, _ = _swa_window(v_ref, i, nb, kh * HEAD_DIM)
            q4, sink_row = _swa_group(q_ref, sink_ref, kh)
            p, _ = _swa_scores(i, nb, q4, kw, bias_ref[kh], sink_row)
            _swa_unstack(o_ref, kh, _dot(vw, p.astype(vw.dtype), TN))

    full_kv = pl.BlockSpec((S, SWA_KV_HEADS * HEAD_DIM), lambda i: (0, 0))
    (o,), landed = _carry_call(
        body, (sink, q, k, v, bias), name="swa_fwd", grid=(nb,),
        in_specs=[SMEM_SPEC, pl.BlockSpec((BLOCK, 1024), lambda i: (i, 0)), full_kv, full_kv,
                  pl.BlockSpec((SWA_KV_HEADS, 3 * BLOCK, SWA_LANES), lambda i: (0, 0, 0))],
        out_specs=[pl.BlockSpec((BLOCK, 1024), lambda i: (i, 0))],
        out_shape=[jax.ShapeDtypeStruct((S, 1024), BF)], semantics=("parallel",), carried=carried)
    return o, landed


def _swa_bwd(q, k, v, do, bias, sink, carried=None):
    S = q.shape[0]
    nb = S // BLOCK

    def body(sink_ref, q_ref, k_ref, v_ref, do_ref, bias_ref, dq_ref, dk_ref, dv_ref, dbias_ref, dsink_ref):
        i = pl.program_id(0)

        @pl.when(i == 0)
        def _():
            dk_ref[...] = jnp.zeros_like(dk_ref)
            dv_ref[...] = jnp.zeros_like(dv_ref)
            dbias_ref[...] = jnp.zeros_like(dbias_ref)
            dsink_ref[...] = jnp.zeros_like(dsink_ref)

        for kh in range(SWA_KV_HEADS):
            off = kh * HEAD_DIM
            kw, rows = _swa_window(k_ref, i, nb, off)
            vw, _ = _swa_window(v_ref, i, nb, off)
            q4, sink_row = _swa_group(q_ref, sink_ref, kh)
            p, p_sink = _swa_scores(i, nb, q4, kw, bias_ref[kh], sink_row)
            do4 = jnp.concatenate([_cols(do_ref, (kh * SWA_G + g) * HEAD_DIM) for g in range(SWA_G)], axis=0).astype(vw.dtype)
            dp = _dot(vw, do4, NT)
            delta = jnp.sum(p * dp, axis=0, keepdims=True)
            ds = p * (dp - delta)
            dsb = ds.astype(q4.dtype)
            _swa_unstack(dq_ref, kh, _dot(kw, dsb, TN))
            dkw = _dot(dsb, q4)
            dvw = _dot(p.astype(do4.dtype), do4)
            dbias_ref[kh] += ds
            sink_term = p_sink * delta
            for g in range(SWA_G):
                h = kh * SWA_G + g
                dsink_ref[h:h + 1, :] += jnp.broadcast_to(-_all_sum(sink_term[:, g * BLOCK:(g + 1) * BLOCK]), (1, 128))
            for b, r in enumerate(rows):
                dk_ref[r, off:off + HEAD_DIM] += dkw[b * BLOCK:(b + 1) * BLOCK]
                dv_ref[r, off:off + HEAD_DIM] += dvw[b * BLOCK:(b + 1) * BLOCK]

    full_kv = pl.BlockSpec((S, SWA_KV_HEADS * HEAD_DIM), lambda i: (0, 0))
    blk = pl.BlockSpec((BLOCK, 1024), lambda i: (i, 0))
    bias_spec = pl.BlockSpec((SWA_KV_HEADS, 3 * BLOCK, SWA_LANES), lambda i: (0, 0, 0))
    outs, landed = _carry_call(
        body, (sink, q, k, v, do, bias), name="swa_bwd", grid=(nb,),
        in_specs=[SMEM_SPEC, blk, full_kv, full_kv, blk, bias_spec],
        out_specs=[blk, full_kv, full_kv, bias_spec, pl.BlockSpec((8, 128), lambda i: (0, 0))],
        out_shape=[jax.ShapeDtypeStruct((S, 1024), BF), jax.ShapeDtypeStruct((S, 256), F32),
                   jax.ShapeDtypeStruct((S, 256), F32),
                   jax.ShapeDtypeStruct((SWA_KV_HEADS, 3 * BLOCK, SWA_LANES), F32), jax.ShapeDtypeStruct((8, 128), F32)],
        vmem=VMEM_LARGE, semantics=("arbitrary",), carried=carried)
    return (*outs, landed)


def _prep_odd_fwd(proj, cos, sin, gq, gk):
    S = proj.shape[0]
    tr = _pick(S, (512,))

    def body(p_ref, cos_ref, sin_ref, gq_ref, gk_ref, q_ref, k_ref, v_ref):
        cos_v, sin_v = cos_ref[...], sin_ref[...]
        for h in range(AX_HEADS):
            o = h * HEAD_DIM
            xh, _ = _head_norm(_cols(p_ref, o))
            q_ref[:, o:o + HEAD_DIM] = (_rope(xh * gq_ref[...], cos_v, sin_v, 32) * AX_SCALE).astype(q_ref.dtype)
        for h in range(AX_KV_HEADS):
            o = h * HEAD_DIM
            xh, _ = _head_norm(_cols(p_ref, 1024 + o))
            k_ref[:, o:o + HEAD_DIM] = _rope(xh * gk_ref[...], cos_v, sin_v, 32).astype(k_ref.dtype)
        v_ref[...] = p_ref[:, 1280:1536].astype(v_ref.dtype)

    def row(w):
        return pl.BlockSpec((tr, w), lambda i: (i, 0))

    vec = pl.BlockSpec((1, HEAD_DIM), lambda i: (0, 0))
    return pl.pallas_call(
        body, name="prep_odd_fwd", grid=(S // tr,),
        in_specs=[row(ODD_IN), row(128), row(128), vec, vec], out_specs=[row(1024), row(256), row(256)],
        out_shape=[jax.ShapeDtypeStruct((S, w), BF) for w in (1024, 256, 256)],
        compiler_params=_params(("parallel",)),
    )(proj, cos, sin, gq, gk)


def _prep_odd_bwd(proj, cos, sin, gq, gk, dq, dk, dv):
    S = proj.shape[0]
    tr = _pick(S, (512,))

    def body(p_ref, cos_ref, sin_ref, gq_ref, gk_ref, dq_ref, dk_ref, dv_ref, dp_ref, dgq_ref, dgk_ref):
        cos_v, sin_v = cos_ref[...], sin_ref[...]
        dt = dp_ref.dtype
        dgq = jnp.zeros((1, HEAD_DIM), F32)
        for h in range(AX_HEADS):
            o = h * HEAD_DIM
            xh, r = _head_norm(_cols(p_ref, o))
            dy = _rope_t(_cols(dq_ref, o).astype(F32) * AX_SCALE, cos_v, sin_v, 32)
            dgq = dgq + _row_sum(dy * xh)
            dp_ref[:, o:o + HEAD_DIM] = _head_norm_bwd(dy * gq_ref[...], xh, r).astype(dt)
        dgk = jnp.zeros((1, HEAD_DIM), F32)
        for h in range(AX_KV_HEADS):
            o = h * HEAD_DIM
            xh, r = _head_norm(_cols(p_ref, 1024 + o))
            dy = _rope_t(_cols(dk_ref, o), cos_v, sin_v, 32)
            dgk = dgk + _row_sum(dy * xh)
            dp_ref[:, 1024 + o:1024 + o + HEAD_DIM] = _head_norm_bwd(dy * gk_ref[...], xh, r).astype(dt)
        dp_ref[:, 1280:1536] = dv_ref[...].astype(dt)

        @pl.when(pl.program_id(0) == 0)
        def _():
            dgq_ref[...] = jnp.zeros_like(dgq_ref)
            dgk_ref[...] = jnp.zeros_like(dgk_ref)

        dgq_ref[...] += dgq
        dgk_ref[...] += dgk

    def row(w):
        return pl.BlockSpec((tr, w), lambda i: (i, 0))

    vec = pl.BlockSpec((1, HEAD_DIM), lambda i: (0, 0))
    return pl.pallas_call(
        body, name="prep_odd_bwd", grid=(S // tr,),
        in_specs=[row(ODD_IN), row(128), row(128), vec, vec, row(1024), row(256), row(256)],
        out_specs=[row(ODD_IN), vec, vec],
        out_shape=[jax.ShapeDtypeStruct((S, ODD_IN), BF), jax.ShapeDtypeStruct((1, HEAD_DIM), F32),
                   jax.ShapeDtypeStruct((1, HEAD_DIM), F32)],
        compiler_params=_params(("arbitrary",)),
    )(proj, cos, sin, gq, gk, dq, dk, dv)


def _loop_unrolled(n, factor, step, init):
    while n % factor:
        factor //= 2

    def trip(t, carry):
        for u in range(factor):
            carry = step(factor * t + u, carry)
        return carry

    return lax.fori_loop(0, n // factor, trip, init)


def _flash_fwd(q, k, v, carried=None):
    S = q.shape[0]
    tq = _pick(S, (1024, 512))
    tk = _pick(S, (2048, 1024, 512))
    nk = S // tk
    G = AX_HEADS // AX_KV_HEADS

    def body(q_ref, k_ref, v_ref, o_ref, lse_ref):
        qv = q_ref[...]

        def step(j, carry):
            m, l, acc = carry
            rows = pl.ds(pl.multiple_of(j * tk, tk), tk)
            s = _dot(qv, k_ref[rows, :], NT)
            m_new = jnp.maximum(m, jnp.max(s, axis=-1, keepdims=True))
            alpha = jnp.exp2(m - m_new)
            p = jnp.exp2(s - m_new)
            l = alpha * l + jnp.sum(p, axis=-1, keepdims=True)
            acc = alpha * acc + _dot(p.astype(v_ref.dtype), v_ref[rows, :])
            return m_new, l, acc

        init = (jnp.full((tq, 1), NEG_INF, F32), jnp.zeros((tq, 1), F32), jnp.zeros((tq, HEAD_DIM), F32))
        m, l, acc = _loop_unrolled(nk, 8, step, init)
        o_ref[...] = (acc / l).astype(o_ref.dtype)
        lse_ref[0] = jnp.broadcast_to(m + jnp.log2(l), (tq, 128))

    (o, lse), landed = _carry_call(
        body, (q, k, v), name="flash_fwd", grid=(AX_HEADS, S // tq),
        in_specs=[pl.BlockSpec((tq, HEAD_DIM), lambda h, i: (i, h)),
                  pl.BlockSpec((S, HEAD_DIM), lambda h, i: (0, h // G)),
                  pl.BlockSpec((S, HEAD_DIM), lambda h, i: (0, h // G))],
        out_specs=[pl.BlockSpec((tq, HEAD_DIM), lambda h, i: (i, h)),
                   pl.BlockSpec((1, tq, 128), lambda h, i: (h, i, 0))],
        out_shape=[jax.ShapeDtypeStruct((S, AX_HEADS * HEAD_DIM), BF), jax.ShapeDtypeStruct((AX_HEADS, S, 128), F32)],
        semantics=("parallel", "parallel"), carried=carried)
    return o, lse, landed


def _flash_bwd(q, k, v, o, do, lse, carried=None):
    S = q.shape[0]
    tq = _pick(S, (1024, 512))
    tk = _pick(S, (1024, 512))
    nq, nk = S // tq, S // tk
    G = AX_HEADS // AX_KV_HEADS

    def body(q_ref, k_ref, v_ref, o_ref, do_ref, lse_ref, dq_ref, dk_ref, dv_ref):
        g, i = pl.program_id(1), pl.program_id(2)

        @pl.when((g == 0) & (i == 0))
        def _():
            dk_ref[...] = jnp.zeros_like(dk_ref)
            dv_ref[...] = jnp.zeros_like(dv_ref)

        qv = q_ref[...]
        do_f = do_ref[...].astype(F32)
        dob = do_f.astype(qv.dtype)
        dob_ln2 = (do_f * LN2).astype(qv.dtype)
        delta = jnp.sum(do_f * o_ref[...].astype(F32), axis=-1, keepdims=True) * LN2
        lse_col = lse_ref[0][:, 0:1]

        def step(j, dq):
            rows = pl.ds(pl.multiple_of(j * tk, tk), tk)
            kj, vj = k_ref[rows, :], v_ref[rows, :]
            p = jnp.exp2(_dot(qv, kj, NT) - lse_col)
            dp = _dot(dob_ln2, vj, NT)
            ds = (p * (dp - delta)).astype(qv.dtype)
            dk_ref[rows, :] += _dot(ds, qv, TN)
            dv_ref[rows, :] += _dot(p.astype(dob.dtype), dob, TN)
            return dq + _dot(ds, kj)

        dq_ref[...] = _loop_unrolled(nk, 8, step, jnp.zeros((tq, HEAD_DIM), F32)).astype(dq_ref.dtype)

    q_spec = pl.BlockSpec((tq, HEAD_DIM), lambda kh, g, i: (i, kh * G + g))
    kv_spec = pl.BlockSpec((S, HEAD_DIM), lambda kh, g, i: (0, kh))
    (dq, dk, dv), landed = _carry_call(
        body, (q, k, v, o, do, lse), name="flash_bwd", grid=(AX_KV_HEADS, G, nq),
        in_specs=[q_spec, kv_spec, kv_spec, q_spec, q_spec,
                  pl.BlockSpec((1, tq, 128), lambda kh, g, i: (kh * G + g, i, 0))],
        out_specs=[q_spec, kv_spec, kv_spec],
        out_shape=[jax.ShapeDtypeStruct((S, AX_HEADS * HEAD_DIM), BF), jax.ShapeDtypeStruct((S, 256), F32),
                   jax.ShapeDtypeStruct((S, 256), F32)],
        vmem=VMEM_LARGE, semantics=("arbitrary", "arbitrary", "arbitrary"), carried=carried)
    return dq, dk, dv, landed


def _rope_angles(pos, dim, theta):
    inv = theta ** (-jnp.arange(0, dim, 2, dtype=jnp.float32) / dim)
    return pos.astype(jnp.float32)[:, None] * inv[None, :]


def _rope_tables(S):
    ang = _rope_angles(jnp.arange(S), RET_DK, RET_THETA)
    c, s = jnp.cos(ang), jnp.sin(ang)
    ret = (jnp.concatenate([c, c], -1), jnp.concatenate([-s, s], -1))
    rows = S // GRID_W
    ar, ac = _rope_angles(jnp.arange(rows), HEAD_DIM // 2, AX_THETA), _rope_angles(jnp.arange(GRID_W), HEAD_DIM // 2, AX_THETA)
    cr, sr = jnp.repeat(jnp.cos(ar), GRID_W, axis=0), jnp.repeat(jnp.sin(ar), GRID_W, axis=0)
    cc, sc = jnp.tile(jnp.cos(ac), (rows, 1)), jnp.tile(jnp.sin(ac), (rows, 1))
    ax = (jnp.concatenate([cr, cr, cc, cc], -1), jnp.concatenate([-sr, sr, -sc, sc], -1))
    return ret, ax


def _pad_tile(a):
    return jnp.pad(a.astype(F32), ((0, 8 - a.shape[0]), (0, 128 - a.shape[1])))


def _relu2_epilogue(acc):
    r = jnp.maximum(acc, 0.0)
    return acc, r * r


def _relu2_bwd_epilogue(acc, u):
    return (acc * (2.0 * jnp.maximum(u.astype(F32), 0.0)),)


def _add_epilogue(acc, res):
    return (acc + res,)


def _add_norm_epilogue(acc, res, g):
    y = acc + res
    r = lax.rsqrt(jnp.mean(y * y, axis=-1, keepdims=True) + EPS)
    return y, y * r * g


def _residual_mm(a, w, res, g_next, name):
    return _mm(a, w, name=name, extras=(res,), rows=(g_next,), out_dtypes=(F32, BF), epilogue=_add_norm_epilogue,
               vmem=VMEM_LARGE)


def _rms_bwd_epilogue(dh, x, dres, g):
    r = lax.rsqrt(jnp.mean(x * x, axis=-1, keepdims=True) + EPS)
    xh = x * r
    dxh = dh * g
    dx = r * (dxh - xh * jnp.mean(dxh * xh, axis=-1, keepdims=True)) + dres
    return dx, dx, _row_sum(dh * xh)


def _norm_bwd_mm(mm, dproj, w, x, dres, g, name):
    return mm(dproj, w, tb=True, name=name, out_dtypes=(F32, BF), extras=(x, dres), rows=(g,),
              epilogue=_rms_bwd_epilogue, n_row_sums=1, tm=512)


def _loss_epilogue(acc, res, target):
    e = acc + res - target
    d = e * (1.0 / D_MODEL)
    return d, d, _row_sum(e * e)


def _mlp_fwd(x, h, w_up, w_down, tag, g_next=None, target=None):
    u, a = _mm(h, w_up, name=f"mlp_up_{tag}", out_dtypes=(BF, BF), epilogue=_relu2_epilogue)
    if target is not None:
        return _mm(a, w_down, name=f"mlp_down_{tag}", extras=(x, target), out_dtypes=(F32, BF), epilogue=_loss_epilogue,
                   n_row_sums=1, tm=512), (h, u, a)
    if g_next is None:
        y, h_next = _mm(a, w_down, name=f"mlp_down_{tag}", extras=(x,), epilogue=_add_epilogue), None
    else:
        y, h_next = _residual_mm(a, w_down, x, g_next, f"mlp_down_{tag}")
    return y, h_next, (h, u, a)


def _mlp_bwd(x, g, w_up, w_down, saved, dy, dyb, tag, mm_exchange=None):
    h, u, a = saved
    du = _mm(dyb, w_down, tb=True, name=f"mlp_down_dx_{tag}", out_dtypes=(BF,), extras=(u,), epilogue=_relu2_bwd_epilogue)
    dw_down = _mm(a, dyb, ta=True, name=f"mlp_down_dw_{tag}", out_dtypes=(BF,))
    dw_up = _mm(h, du, ta=True, name=f"mlp_up_dw_{tag}", out_dtypes=(BF,))
    if mm_exchange is None:
        mm = _mm
    else:
        mm = lambda *args, **kw: mm_exchange([("down" + tag, dw_down)], *args, **kw)
    dx, dxb, dg = _norm_bwd_mm(mm, du, w_up, x, dy, g, f"mlp_up_dx_{tag}")
    return dx, dxb, dg, dw_up, dw_down


COL_SHARDED = ("in_even", "in_odd", "up0", "up1")


def _assemble(key, g):
    if key in COL_SHARDED:
        return g.transpose(1, 0, 2).reshape(g.shape[1], N_DEV * g.shape[2])
    return g.reshape(N_DEV * g.shape[1], g.shape[2])


def _split(key, full):
    rows, cols = full.shape
    if key in COL_SHARDED:
        return full.reshape(rows, N_DEV, cols // N_DEV).transpose(1, 0, 2)
    return full.reshape(N_DEV, rows // N_DEV, cols)


def _local_step(x, target, W, P, late=None):
    S = x.shape[0]
    W = dict(W)
    landed = {}

    def gather_while(stage):
        return None if late is None else _Exchange([s for _, s in late[stage]], gather=True)

    def arrived(stage, outs):
        for (key, _), g in zip([] if late is None else late[stage], outs):
            W[key] = _assemble(key, g)

    def exchange_while(grads):
        return None if late is None else _Exchange([_split(k, g) for k, g in grads], gather=False)

    def left(grads, outs):
        for (key, _), l in zip(grads, outs):
            landed[key] = l

    def mm_gather(stage, *args, **kw):
        if late is None:
            return _mm(*args, **kw)
        out, outs = _mm(*args, carried=gather_while(stage), **kw)
        arrived(stage, outs)
        return out

    def mm_exchange(grads, *args, **kw):
        if late is None:
            return _mm(*args, **kw)
        out, outs = _mm(*args, carried=exchange_while(grads), **kw)
        left(grads, outs)
        return out

    (cos_r, sin_r), (cos_a, sin_a) = _rope_tables(S)
    bucket = _t5_bucket_map().T
    logit = P["ret_decay_logit"]
    lg = _log_sigmoid_tile(_pad_tile(logit))
    bias = _t5_bias(P["t5_table"], bucket)
    nmix, nmlp = P["norm_mix"], P["norm_mlp"]

    h0 = _rms_fwd(x, nmix[0:1], "mix_norm_0")
    proj_e = mm_gather("in_even", h0, W["in_even"], name="in_even")
    (qr, kr, vr, qs, ks, vs), outs = _prep_even_fwd(proj_e, cos_r, sin_r, P["swa_q_norm"], P["swa_k_norm"],
                                                    gather_while("prep_even_fwd"))
    arrived("prep_even_fwd", outs)
    sf, sb = _ret_scan(kr, vr, lg, (0, 0), (1, 0), "retention_states")
    y_ret, oa, outs = _ret_fwd(qr, kr, vr, lg, sf, sb, proj_e, P["ret_norm"], gather_while("retention_fwd"))
    arrived("retention_fwd", outs)
    ob, outs = _swa_fwd(qs, ks, vs, bias, P["swa_sink"], gather_while("swa_fwd"))
    arrived("swa_fwd", outs)
    wo_a, wo_b = W["out_even"][:1024], W["out_even"][1024:]
    x1 = _mm(oa, wo_a, name="out_even_a", extras=(x,), epilogue=_add_epilogue)
    x1, h1 = _residual_mm(ob, wo_b, x1, nmlp[0:1], "out_even_b")
    x2, h2, mlp0 = _mlp_fwd(x1, h1, W["up0"], W["down0"], "0", g_next=nmix[1:2])

    proj_o = _mm(h2, W["in_odd"], name="in_odd")
    qx, kx, vx = _prep_odd_fwd(proj_o, cos_a, sin_a, P["ax_q_norm"], P["ax_k_norm"])
    ox, lse, outs = _flash_fwd(qx, kx, vx, gather_while("flash_fwd"))
    arrived("flash_fwd", outs)
    x3, h3 = _residual_mm(ox, W["out_odd"], x2, nmlp[1:2], "out_odd")
    (d4, d4b, loss_row), mlp1 = _mlp_fwd(x3, h3, W["up1"], W["down1"], "1", target=target)

    d3, d3b, dnmlp1, dw_up1, dw_down1 = _mlp_bwd(x3, nmlp[1:2], W["up1"], W["down1"], mlp1, d4, d4b, "1")
    dox = _mm(d3b, W["out_odd"], tb=True, name="out_odd_dx")
    dw_out_odd = _mm(ox, d3b, ta=True, name="out_odd_dw", out_dtypes=(BF,))
    grads1 = [("up1", dw_up1), ("down1", dw_down1), ("out_odd", dw_out_odd)]
    dqx, dkx, dvx, outs = _flash_bwd(qx, kx, vx, ox, dox, lse, exchange_while(grads1))
    left(grads1, outs)
    dproj_o, dgq_ax, dgk_ax = _prep_odd_bwd(proj_o, cos_a, sin_a, P["ax_q_norm"], P["ax_k_norm"], dqx, dkx, dvx)
    dw_in_odd = _mm(h2, dproj_o, ta=True, name="in_odd_dw", out_dtypes=(BF,))
    d2, d2b, dnmix1 = _norm_bwd_mm(_mm, dproj_o, W["in_odd"], x2, d3, nmix[1:2], "in_odd_dx")

    d1, d1b, dnmlp0, dw_up0, dw_down0 = _mlp_bwd(x1, nmlp[0:1], W["up0"], W["down0"], mlp0, d2, d2b, "0", mm_exchange)
    doa = _mm(d1b, wo_a, tb=True, name="out_even_a_dx")
    dob = _mm(d1b, wo_b, tb=True, name="out_even_b_dx")
    dw_out_even = jnp.concatenate([_mm(oa, d1b, ta=True, name="out_even_a_dw", out_dtypes=(BF,)),
                                   _mm(ob, d1b, ta=True, name="out_even_b_dw", out_dtypes=(BF,))], axis=0)
    dy_ret, dga, dret_norm = _ret_post_bwd(y_ret, proj_e, P["ret_norm"], doa)
    hb, hf = _ret_scan(qr, dy_ret, lg, (1, 1), (0, 1), "retention_state_grads")
    dqr, dkr, dvr, dlogit = _ret_bwd(qr, kr, vr, dy_ret, lg, logit, sf, sb, hf, hb)
    grads0 = [("in_odd", dw_in_odd), ("out_even", dw_out_even)]
    dqs, dks, dvs, dbias, dsink, outs = _swa_bwd(qs, ks, vs, dob, bias, P["swa_sink"], exchange_while(grads0))
    left(grads0, outs)
    dt5 = _t5_table_grad(dbias, bucket)
    dproj_e, dgq_swa, dgk_swa = _prep_even_bwd(proj_e, cos_r, sin_r, P["swa_q_norm"], P["swa_k_norm"],
                                               dqr, dkr, dvr, dga, dqs, dks, dvs)
    dw_in_even = mm_exchange([("up0", dw_up0)], h0, dproj_e, ta=True, name="in_even_dw", out_dtypes=(BF,))
    dx, _, dnmix0 = _norm_bwd_mm(lambda *args, **kw: mm_exchange([("in_even", dw_in_even)], *args, **kw),
                                 dproj_e, W["in_even"], x, d1, nmix[0:1], "in_even_dx")

    if late is None:
        dW = dict(grads1 + grads0, up0=dw_up0, down0=dw_down0, in_even=dw_in_even)
    else:
        dW = landed
    dP = {"norm_mix": jnp.concatenate([dnmix0, dnmix1], 0), "norm_mlp": jnp.concatenate([dnmlp0, dnmlp1], 0),
          "ret_decay_logit": dlogit[:, :, 0, 0].T, "ret_norm": dret_norm,
          "swa_q_norm": dgq_swa, "swa_k_norm": dgk_swa, "swa_sink": dsink[:, 0][None, :],
          "t5_table": dt5[:, :SWA_HEADS], "ax_q_norm": dgq_ax, "ax_k_norm": dgk_ax}
    return loss_row, dx, dW, dP


def _cast_shards(shards):
    n = len(shards)

    def body(*refs):
        for i_ref, o_ref in zip(refs[:n], refs[n:]):
            o_ref[...] = i_ref[...].astype(o_ref.dtype)

    return pl.pallas_call(body, name="cast_shards", in_specs=[VMEM_SPEC] * n, out_specs=[VMEM_SPEC] * n,
                          out_shape=[jax.ShapeDtypeStruct(s.shape, BF) for s in shards],
                          compiler_params=pltpu.CompilerParams(vmem_limit_bytes=VMEM_SMALL))(*shards)


def _all_gather(shards):
    n = len(shards)

    def body(*refs):
        ins, outs, stage = refs[:n], refs[n:2 * n], refs[2 * n:3 * n]
        send_sems, recv_sems, local_sems = refs[3 * n:]
        me = _my_place()
        sibling = _flip(me, 1)
        chips = [_flip(me, 4), _flip(me, 2), _flip(me, 6)]

        def copy(a, k, block, to, src=None):
            dst = outs[a].at[_index(block)]
            return pltpu.make_async_remote_copy(
                src_ref=dst if src is None else src, dst_ref=dst,
                send_sem=send_sems.at[a, k], recv_sem=recv_sems.at[a, k], device_id=to, device_id_type=MESH)

        first, mine = [], []
        for a in range(n):
            stage[a][...] = ins[a][...].astype(stage[a].dtype)
            mine.append(pltpu.make_async_copy(stage[a], outs[a].at[_index(me)], local_sems.at[a]))
            mine[-1].start()
            first.append(copy(a, 0, me, sibling, src=stage[a]))
            first += [copy(a, 1 + j, me, chip, src=stage[a]) for j, chip in enumerate(chips)]
        for cp in first:
            cp.start()
        passed = []
        for a in range(n):
            for j, chip in enumerate(chips):
                copy(a, 1 + j, chip, me).wait_recv()
                passed.append(copy(a, 4 + j, chip, sibling))
                passed[-1].start()
        for a in range(n):
            copy(a, 0, sibling, me).wait_recv()
            for j, chip in enumerate(chips):
                copy(a, 4 + j, _flip(chip, 1), me).wait_recv()
        for cp in first + passed:
            cp.wait_send()
        for cp in mine:
            cp.wait()

    return pl.pallas_call(
        body, name="weights_all_gather",
        in_specs=[VMEM_SPEC] * n, out_specs=[ANY_SPEC] * n,
        out_shape=[jax.ShapeDtypeStruct((N_DEV,) + s.shape, BF) for s in shards],
        scratch_shapes=[pltpu.VMEM(s.shape, BF) for s in shards]
        + [pltpu.SemaphoreType.DMA((n, 7)), pltpu.SemaphoreType.DMA((n, 7)), pltpu.SemaphoreType.DMA((n,))],
        compiler_params=pltpu.CompilerParams(vmem_limit_bytes=VMEM_SMALL),
    )(*shards)


def _all_reduce_small(part):
    R, C = part.shape

    def body(x_ref, o_ref, land_ref, send_sems, recv_sems):
        me = _my_place()
        land_ref[_index(me)] = x_ref[...]
        copies = []
        for k in range(1, N_DEV):
            peer = _flip(me, k)
            copies.append(pltpu.make_async_remote_copy(
                src_ref=x_ref, dst_ref=land_ref.at[_index(me)],
                send_sem=send_sems.at[k - 1], recv_sem=recv_sems.at[k - 1], device_id=peer, device_id_type=MESH))
            copies[-1].start()
        for k in range(1, N_DEV):
            peer = _flip(me, k)
            pltpu.make_async_remote_copy(
                src_ref=x_ref, dst_ref=land_ref.at[_index(peer)],
                send_sem=send_sems.at[k - 1], recv_sem=recv_sems.at[k - 1], device_id=peer, device_id_type=MESH).wait_recv()
        for cp in copies:
            cp.wait_send()
        acc = land_ref[0]
        for s in range(1, N_DEV):
            acc = acc + land_ref[s]
        o_ref[...] = acc
        o_ref[LOSS_ROW:LOSS_ROW + 1, :] = jnp.broadcast_to(_all_sum(acc[LOSS_ROW:LOSS_ROW + 1, :]) * (0.5 / D_MODEL), (1, C))

    return pl.pallas_call(
        body, name="small_all_reduce", in_specs=[VMEM_SPEC], out_specs=VMEM_SPEC,
        out_shape=jax.ShapeDtypeStruct((R, C), F32),
        scratch_shapes=[pltpu.VMEM((N_DEV, R, C), F32), pltpu.SemaphoreType.DMA((7,)), pltpu.SemaphoreType.DMA((7,))],
    )(part)


def _adamw_math(w, g, m, v):
    m = ADAM_B1 * m + (1.0 - ADAM_B1) * g
    v = ADAM_B2 * v + (1.0 - ADAM_B2) * jnp.square(g)
    m_hat = m / (1.0 - ADAM_B1 ** ADAM_STEP)
    v_hat = v / (1.0 - ADAM_B2 ** ADAM_STEP)
    delta = -ADAM_LR * (m_hat / (jnp.sqrt(v_hat) + ADAM_EPS) + ADAM_WD * w)
    return delta, m, v


def _sum_and_adamw(landed, w, m, v, name):
    R, C = w.shape
    tr = _pick(R, (256, 128))

    def body(l_ref, w_ref, m_ref, v_ref, g_ref, d_ref, nm_ref, nv_ref):
        g = l_ref[0].astype(F32)
        for s in range(1, N_DEV):
            g = g + l_ref[s].astype(F32)
        g_ref[...] = g
        d_ref[...], nm_ref[...], nv_ref[...] = _adamw_math(w_ref[...], g, m_ref[...], v_ref[...])

    row = pl.BlockSpec((tr, C), lambda i: (i, 0))
    return pl.pallas_call(
        body, name=name, grid=(R // tr,),
        in_specs=[pl.BlockSpec((N_DEV, tr, C), lambda i: (0, i, 0)), row, row, row], out_specs=[row] * 4,
        out_shape=[jax.ShapeDtypeStruct((R, C), F32)] * 4, compiler_params=_params(("parallel",)),
    )(landed, w, m, v)


def _adamw_small(w, g, m, v):
    def body(w_ref, g_ref, m_ref, v_ref, d_ref, nm_ref, nv_ref):
        d_ref[...], nm_ref[...], nv_ref[...] = _adamw_math(w_ref[...], g_ref[...], m_ref[...], v_ref[...])

    full = pl.BlockSpec(w.shape, lambda: (0, 0))
    return pl.pallas_call(body, name="adamw_small", in_specs=[full] * 4, out_specs=[full] * 3,
                          out_shape=[jax.ShapeDtypeStruct(w.shape, F32)] * 3)(w, g, m, v)


MATRICES = ("w_in_even", "w_out_even", "w_in_odd", "w_out_odd", "w_mlp_up", "w_mlp_down")
SMALL = ("norm_mix", "norm_mlp", "ret_decay_logit", "ret_norm", "swa_q_norm", "swa_k_norm", "swa_sink",
         "t5_table", "ax_q_norm", "ax_k_norm")
MATRIX_OF = {"in_even": ("w_in_even", 0), "out_even": ("w_out_even", 0), "in_odd": ("w_in_odd", 0),
             "out_odd": ("w_out_odd", 0), "up0": ("w_mlp_up", 0), "up1": ("w_mlp_up", 1),
             "down0": ("w_mlp_down", 0), "down1": ("w_mlp_down", 1)}
GATHER_FIRST = ("in_even",)
GATHER_WHILE = {"in_even": ("up0",), "prep_even_fwd": ("out_even",), "retention_fwd": ("in_odd",), "swa_fwd": ("down0",),
                "flash_fwd": ("out_odd", "up1", "down1")}


SMALL_ROWS = 8
SMALL_AT = {"norm_mix": (0, 0), "norm_mlp": (2, 0), "ret_norm": (4, 0), "swa_q_norm": (5, 0), "swa_k_norm": (5, 128),
            "ax_q_norm": (5, 256), "ax_k_norm": (5, 384), "swa_sink": (5, 512), "ret_decay_logit": (5, 640),
            "t5_table": (6, 0)}
LOSS_ROW = 7


def _pack_small(arrays, loss_row=None):
    buf = jnp.zeros((SMALL_ROWS, 1024), F32)
    for name, (r, c) in SMALL_AT.items():
        a = arrays[name].astype(F32)
        a = a.reshape(1, -1) if name in ("ret_decay_logit", "t5_table") else a.reshape(-1, a.shape[-1])
        buf = lax.dynamic_update_slice(buf, a, (r, c))
    if loss_row is not None:
        buf = lax.dynamic_update_slice(buf, loss_row, (LOSS_ROW, 0))
    return buf


def _unpack_small(buf, like):
    out = {}
    for name, (r, c) in SMALL_AT.items():
        shape = like[name].shape
        rows = 1 if name in ("ret_decay_logit", "t5_table") else math.prod(shape[:-1])
        cols = math.prod(shape) // rows
        out[name] = buf[r:r + rows, c:c + cols].reshape(shape)
    return out


def kernel(x, norm_mix, norm_mlp, w_in_even, w_out_even, ret_decay_logit, ret_norm, swa_q_norm, swa_k_norm, swa_sink, t5_table, w_in_odd, w_out_odd, ax_q_norm, ax_k_norm, w_mlp_up, w_mlp_down, loss_target, m_norm_mix, m_norm_mlp, m_w_in_even, m_w_out_even, m_ret_decay_logit, m_ret_norm, m_swa_q_norm, m_swa_k_norm, m_swa_sink, m_t5_table, m_w_in_odd, m_w_out_odd, m_ax_q_norm, m_ax_k_norm, m_w_mlp_up, m_w_mlp_down, v_norm_mix, v_norm_mlp, v_w_in_even, v_w_out_even, v_ret_decay_logit, v_ret_norm, v_swa_q_norm, v_swa_k_norm, v_swa_sink, v_t5_table, v_w_in_odd, v_w_out_odd, v_ax_q_norm, v_ax_k_norm, v_w_mlp_up, v_w_mlp_down):
    given = dict(locals())
    weights = {n: given[n] for n in MATRICES + SMALL}
    moments_m = {n: given["m_" + n] for n in MATRICES + SMALL}
    moments_v = {n: given["v_" + n] for n in MATRICES + SMALL}

    def shard(table, key):
        arg, layer = MATRIX_OF[key]
        return table[arg][layer]

    gathered = _all_gather([shard(weights, k) for k in GATHER_FIRST])
    W = {k: _assemble(k, g) for k, g in zip(GATHER_FIRST, gathered)}
    late_keys = [k for keys in GATHER_WHILE.values() for k in keys]
    cast = dict(zip(late_keys, _cast_shards([shard(weights, k) for k in late_keys])))
    late = {stage: [(k, cast[k]) for k in keys] for stage, keys in GATHER_WHILE.items()}
    P = {"norm_mix": norm_mix, "norm_mlp": norm_mlp, "ret_decay_logit": ret_decay_logit[0], "ret_norm": ret_norm,
         "swa_q_norm": swa_q_norm, "swa_k_norm": swa_k_norm, "swa_sink": swa_sink, "t5_table": t5_table,
         "ax_q_norm": ax_q_norm, "ax_k_norm": ax_k_norm}

    loss_row, dx, landed, dP = _local_step(x[0], loss_target[0], W, P, late)

    per_key = {k: _sum_and_adamw(landed[k], shard(weights, k), shard(moments_m, k), shard(moments_v, k), "adamw_" + k)
               for k in MATRIX_OF}
    grads, deltas, new_m, new_v = {}, {}, {}, {}
    for i, out in enumerate((grads, deltas, new_m, new_v)):
        for n in MATRICES:
            out[n] = jnp.stack([per_key[k][i] for k, (arg, _) in MATRIX_OF.items() if arg == n])

    dP["ret_decay_logit"] = dP["ret_decay_logit"][None]
    total = _all_reduce_small(_pack_small(dP, loss_row))
    loss = total[LOSS_ROW, 0]
    small_d, small_m, small_v = _adamw_small(_pack_small(weights), total, _pack_small(moments_m), _pack_small(moments_v))
    like = {n: weights[n] for n in SMALL}
    for out, buf in ((grads, total), (deltas, small_d), (new_m, small_m), (new_v, small_v)):
        out.update(_unpack_small(buf, like))

    order = ("norm_mix", "norm_mlp", "w_in_even", "w_out_even", "ret_decay_logit", "ret_norm", "swa_q_norm", "swa_k_norm",
             "swa_sink", "t5_table", "w_in_odd", "w_out_odd", "ax_q_norm", "ax_k_norm", "w_mlp_up", "w_mlp_down")
    return (loss, dx[None], *[grads[n] for n in order], *[deltas[n] for n in order],
            *[new_m[n] for n in order], *[new_v[n] for n in order])
```

```python
import math

import jax
import jax.numpy as jnp
from jax import lax
from jax.experimental import pallas as pl
from jax.experimental.pallas import tpu as pltpu

F32 = jnp.float32
BF = jnp.bfloat16

D_MODEL = 1024
HEAD_DIM = 128
EPS = 1e-6
NEG_INF = -1e30
RET_HEADS, RET_DK, RET_DV = 4, 128, 256
RET_THETA = 10000.0
SWA_HEADS, SWA_KV_HEADS, WINDOW, BLOCK = 8, 2, 128, 128
T5_BUCKETS, T5_MAX_DIST = 32, 128
AX_HEADS, AX_KV_HEADS, AX_THETA, GRID_W = 8, 2, 10000.0, 64
D_FF = 4096
ATT_SCALE = HEAD_DIM ** -0.5
LN2 = math.log(2.0)
AX_SCALE = ATT_SCALE / LN2
RET_SCALE = RET_DK ** -0.5
N_DEV = 8

ADAM_LR, ADAM_B1, ADAM_B2, ADAM_EPS, ADAM_WD, ADAM_STEP = 0.001, 0.9, 0.999, 1e-08, 0.01, 10

MIB = 1024 * 1024
VMEM_SMALL = 40 * MIB
VMEM_LARGE = 56 * MIB

OFF_QA, OFF_KA, OFF_VA, OFF_GA, OFF_QB, OFF_KB, OFF_VB = 0, 512, 1024, 2048, 3072, 4096, 4352
EVEN_IN = 4608
ODD_IN = 1536

NT = (((1,), (1,)), ((), ()))
TN = (((0,), (0,)), ((), ()))
NN = (((1,), (0,)), ((), ()))


def _dot(a, b, dims=NN):
    return lax.dot_general(a, b, dims, preferred_element_type=F32)


def _params(sem=None, vmem=VMEM_SMALL):
    return pltpu.CompilerParams(dimension_semantics=sem, vmem_limit_bytes=vmem)


def _pick(n, prefs):
    for p in prefs:
        if n % p == 0:
            return p
    return n


def _row_sum(x):
    return jnp.sum(x, axis=0, keepdims=True)


def _all_sum(x):
    return jnp.sum(jnp.sum(x, axis=0, keepdims=True), axis=1, keepdims=True)


def _sigmoid(x):
    return 1.0 / (1.0 + jnp.exp(-x))


SMEM_SPEC = pl.BlockSpec(memory_space=pltpu.SMEM)
ANY_SPEC = pl.BlockSpec(memory_space=pl.ANY)
VMEM_SPEC = pl.BlockSpec(memory_space=pltpu.VMEM)
MESH = pl.DeviceIdType.MESH


def _my_place():
    return lax.axis_index("x"), lax.axis_index("y"), lax.axis_index("c")


def _flip(place, k):
    x, y, c = place
    return (1 - x if k & 4 else x, 1 - y if k & 2 else y, 1 - c if k & 1 else c)


def _index(place):
    x, y, c = place
    return 4 * x + 2 * y + c


class _Exchange:
    def __init__(self, sources, gather):
        self.sources, self.gather, self.n = list(sources), gather, len(sources)
        self.out_shape = [jax.ShapeDtypeStruct(((N_DEV,) + s.shape) if gather else s.shape, s.dtype) for s in self.sources]
        self.scratch = [pltpu.SemaphoreType.DMA((self.n, 7)), pltpu.SemaphoreType.DMA((self.n, 7)),
                        pltpu.SemaphoreType.DMA((self.n,))]

    def _source(self, ins, a, place):
        return ins[a] if self.gather else ins[a].at[_index(place)]

    def _local(self, ins, outs, sems):
        me = _my_place()
        return [pltpu.make_async_copy(self._source(ins, a, me), outs[a].at[_index(me)], sems[2].at[a]) for a in range(self.n)]

    def _remote(self, ins, outs, sems, arriving):
        send_sems, recv_sems, _ = sems
        me = _my_place()
        copies = []
        for a in range(self.n):
            for k in range(1, N_DEV):
                peer = _flip(me, k)
                copies.append(pltpu.make_async_remote_copy(
                    src_ref=self._source(ins, a, peer), dst_ref=outs[a].at[_index(peer if arriving else me)],
                    send_sem=send_sems.at[a, k - 1], recv_sem=recv_sems.at[a, k - 1], device_id=peer, device_id_type=MESH))
        return copies

    def start(self, ins, outs, sems):
        for cp in self._local(ins, outs, sems) + self._remote(ins, outs, sems, arriving=False):
            cp.start()

    def wait(self, ins, outs, sems):
        for cp in self._remote(ins, outs, sems, arriving=True):
            cp.wait_recv()
        for cp in self._remote(ins, outs, sems, arriving=False):
            cp.wait_send()
        for cp in self._local(ins, outs, sems):
            cp.wait()


def _carry_call(body, args, *, name, grid, in_specs, out_specs, out_shape, scratch_shapes=(), vmem=VMEM_SMALL,
                semantics=None, carried=None):
    if carried is None:
        outs = pl.pallas_call(body, name=name, grid=grid, in_specs=in_specs, out_specs=out_specs, out_shape=out_shape,
                              scratch_shapes=list(scratch_shapes), compiler_params=_params(semantics, vmem))(*args)
        return list(outs), []
    ni, no, ns, nc = len(in_specs), len(out_specs), len(scratch_shapes), carried.n

    def full_body(*refs):
        ins, cin = refs[:ni], refs[ni:ni + nc]
        outs, cout = refs[ni + nc:ni + nc + no], refs[ni + nc + no:ni + 2 * nc + no]
        scratch, sems = refs[ni + 2 * nc + no:ni + 2 * nc + no + ns], refs[ni + 2 * nc + no + ns:]
        ids = [pl.program_id(d) for d in range(len(grid))]
        first, last = ids[0] == 0, ids[0] == grid[0] - 1
        for d in range(1, len(grid)):
            first, last = first & (ids[d] == 0), last & (ids[d] == grid[d] - 1)

        @pl.when(first)
        def _():
            carried.start(cin, cout, sems)

        body(*ins, *outs, *scratch)

        @pl.when(last)
        def _():
            carried.wait(cin, cout, sems)

    outs = pl.pallas_call(
        full_body, name=name, grid=grid, in_specs=list(in_specs) + [ANY_SPEC] * nc,
        out_specs=list(out_specs) + [ANY_SPEC] * nc, out_shape=list(out_shape) + carried.out_shape,
        scratch_shapes=list(scratch_shapes) + carried.scratch,
        compiler_params=_params(("arbitrary",) * len(grid), vmem))(*args, *carried.sources)
    return list(outs[:no]), list(outs[no:])


def _mm(a, b, *, name, ta=False, tb=False, out_dtypes=(F32,), extras=(), rows=(), n_row_sums=0, epilogue=None,
        tm=1024, tn=1024, tk=1024, vmem=VMEM_SMALL, carried=None, b_rows=None):
    M, K = (a.shape[1], a.shape[0]) if ta else a.shape
    b_start, b_size = (0, b.shape[0]) if b_rows is None else b_rows
    N = b_size if tb else b.shape[1]
    assert K == (b.shape[1] if tb else b_size)
    tm = _pick(M, (tm, 512, 256, 128))
    tn = _pick(N, (tn, 1536, 512, 384, 256, 128))
    tk = _pick(K, (2048, 2304, tk, 1536, 512, 256, 128))
    nk = K // tk
    b_off = b_start // (tn if tb else tk)
    assert b_start % (tn if tb else tk) == 0
    assert n_row_sums == 0 or tn == N
    ne, nr, no = len(extras), len(rows), len(out_dtypes)
    dims = (((0 if ta else 1,), (1 if tb else 0,)), ((), ()))
    if epilogue is None:
        epilogue = lambda acc: (acc,)

    def body(a_ref, b_ref, *rest):
        extra_refs, out_refs = rest[:ne + nr], rest[ne + nr:ne + nr + no]
        sum_refs = rest[ne + nr + no:ne + nr + no + n_row_sums]
        first_tile = pl.program_id(0) == 0

        def finish(acc):
            outs = epilogue(acc, *[r[...] for r in extra_refs])
            for o_ref, o in zip(out_refs, outs[:no]):
                o_ref[...] = o.astype(o_ref.dtype)
            for s_ref, contribution in zip(sum_refs, outs[no:]):
                @pl.when(first_tile)
                def _(s_ref=s_ref, contribution=contribution):
                    s_ref[...] = contribution

                @pl.when(jnp.logical_not(first_tile))
                def _(s_ref=s_ref, contribution=contribution):
                    s_ref[...] += contribution

        part = _dot(a_ref[...], b_ref[...], dims)
        if nk == 1:
            finish(part)
        else:
            acc_ref = rest[-1]
            k = pl.program_id(2)

            @pl.when(k == 0)
            def _():
                acc_ref[...] = part

            @pl.when(k > 0)
            def _():
                acc_ref[...] += part

            @pl.when(k == nk - 1)
            def _():
                finish(acc_ref[...])

    a_spec = pl.BlockSpec((tk, tm), lambda i, j, k: (k, i)) if ta else pl.BlockSpec((tm, tk), lambda i, j, k: (i, k))
    b_spec = (pl.BlockSpec((tn, tk), lambda i, j, k: (j + b_off, k)) if tb
              else pl.BlockSpec((tk, tn), lambda i, j, k: (k + b_off, j)))
    o_spec = pl.BlockSpec((tm, tn), lambda i, j, k: (i, j))
    row_spec = pl.BlockSpec((1, tn), lambda i, j, k: (0, j))
    outs, landed = _carry_call(
        body, (a, b, *extras, *rows), name=name, grid=(M // tm, N // tn, nk),
        in_specs=[a_spec, b_spec] + [o_spec] * ne + [row_spec] * nr,
        out_specs=[o_spec] * no + [row_spec] * n_row_sums,
        out_shape=[jax.ShapeDtypeStruct((M, N), dt) for dt in out_dtypes] + [jax.ShapeDtypeStruct((1, N), F32)] * n_row_sums,
        scratch_shapes=[pltpu.VMEM((tm, tn), F32)] if nk > 1 else [],
        vmem=vmem, semantics=("arbitrary" if n_row_sums else "parallel", "parallel", "arbitrary"), carried=carried)
    outs = outs[0] if len(outs) == 1 else outs
    return outs if carried is None else (outs, landed)


def _rms_fwd(x, g, name):
    S, Dm = x.shape
    tr = _pick(S, (512,))

    def body(x_ref, g_ref, o_ref):
        xv = x_ref[...]
        r = lax.rsqrt(jnp.mean(xv * xv, axis=-1, keepdims=True) + EPS)
        o_ref[...] = (xv * r * g_ref[...]).astype(o_ref.dtype)

    row = pl.BlockSpec((tr, Dm), lambda i: (i, 0))
    return pl.pallas_call(
        body, name=name, grid=(S // tr,),
        in_specs=[row, pl.BlockSpec((1, Dm), lambda i: (0, 0))], out_specs=row,
        out_shape=jax.ShapeDtypeStruct((S, Dm), BF), compiler_params=_params(("parallel",)),
    )(x, g)


def _partner(x, half):
    if half == 64:
        return pltpu.roll(x, 64, 1)
    lane = lax.broadcasted_iota(jnp.int32, x.shape, 1)
    return jnp.where((lane % (2 * half)) < half, pltpu.roll(x, 128 - half, 1), pltpu.roll(x, half, 1))


def _rope(x, cos, sin, half):
    return x * cos + _partner(x, half) * sin


def _rope_t(dy, cos, sin, half):
    return dy * cos - _partner(dy, half) * sin


def _head_norm(x):
    r = lax.rsqrt(jnp.mean(x * x, axis=-1, keepdims=True) + EPS)
    return x * r, r


def _head_norm_bwd(dxh, xh, r):
    return r * (dxh - xh * jnp.mean(dxh * xh, axis=-1, keepdims=True))


def _cols(ref, off, width=HEAD_DIM):
    return ref[:, off:off + width]


def _prep_even_fwd(proj, cos, sin, gq, gk, carried=None):
    S = proj.shape[0]
    tr = _pick(S, (256,))

    def body(p_ref, cos_ref, sin_ref, gq_ref, gk_ref, qr_ref, kr_ref, vr_ref, qs_ref, ks_ref, vs_ref):
        cos_v, sin_v = cos_ref[...], sin_ref[...]
        for h in range(RET_HEADS):
            o = h * RET_DK
            qr_ref[:, o:o + RET_DK] = _rope(_cols(p_ref, OFF_QA + o), cos_v, sin_v, 64).astype(qr_ref.dtype)
            kr_ref[:, o:o + RET_DK] = (_rope(_cols(p_ref, OFF_KA + o), cos_v, sin_v, 64) * RET_SCALE).astype(kr_ref.dtype)
        vr_ref[...] = p_ref[:, OFF_VA:OFF_VA + 1024].astype(vr_ref.dtype)
        for h in range(SWA_HEADS):
            o = h * HEAD_DIM
            xh, _ = _head_norm(_cols(p_ref, OFF_QB + o))
            qs_ref[:, o:o + HEAD_DIM] = (xh * gq_ref[...] * ATT_SCALE).astype(qs_ref.dtype)
        for h in range(SWA_KV_HEADS):
            o = h * HEAD_DIM
            xh, _ = _head_norm(_cols(p_ref, OFF_KB + o))
            ks_ref[:, o:o + HEAD_DIM] = (xh * gk_ref[...]).astype(ks_ref.dtype)
        vs_ref[...] = p_ref[:, OFF_VB:OFF_VB + 256].astype(vs_ref.dtype)

    def row(w):
        return pl.BlockSpec((tr, w), lambda i: (i, 0))

    vec = pl.BlockSpec((1, HEAD_DIM), lambda i: (0, 0))
    widths = (512, 512, 1024, 1024, 256, 256)
    return _carry_call(
        body, (proj, cos, sin, gq, gk), name="prep_even_fwd", grid=(S // tr,),
        in_specs=[row(EVEN_IN), row(128), row(128), vec, vec],
        out_specs=[row(w) for w in widths],
        out_shape=[jax.ShapeDtypeStruct((S, w), BF) for w in widths],
        semantics=("parallel",), carried=carried)


def _prep_even_bwd(proj, cos, sin, gq, gk, dqr, dkr, dvr, dga, dqs, dks, dvs):
    S = proj.shape[0]
    tr = _pick(S, (256,))

    def body(p_ref, cos_ref, sin_ref, gq_ref, gk_ref, dqr_ref, dkr_ref, dvr_ref, dga_ref, dqs_ref, dks_ref,
             dvs_ref, dp_ref, dgq_ref, dgk_ref):
        cos_v, sin_v = cos_ref[...], sin_ref[...]
        dt = dp_ref.dtype
        for h in range(RET_HEADS):
            o = h * RET_DK
            dp_ref[:, OFF_QA + o:OFF_QA + o + RET_DK] = _rope_t(_cols(dqr_ref, o).astype(F32), cos_v, sin_v, 64).astype(dt)
            dp_ref[:, OFF_KA + o:OFF_KA + o + RET_DK] = _rope_t(_cols(dkr_ref, o).astype(F32) * RET_SCALE, cos_v, sin_v, 64).astype(dt)
        dp_ref[:, OFF_VA:OFF_VA + 1024] = dvr_ref[...].astype(dt)
        dp_ref[:, OFF_GA:OFF_GA + 1024] = dga_ref[...].astype(dt)
        dgq = jnp.zeros((1, HEAD_DIM), F32)
        for h in range(SWA_HEADS):
            o = h * HEAD_DIM
            xh, r = _head_norm(_cols(p_ref, OFF_QB + o))
            dy = _cols(dqs_ref, o).astype(F32) * ATT_SCALE
            dgq = dgq + _row_sum(dy * xh)
            dp_ref[:, OFF_QB + o:OFF_QB + o + HEAD_DIM] = _head_norm_bwd(dy * gq_ref[...], xh, r).astype(dt)
        dgk = jnp.zeros((1, HEAD_DIM), F32)
        for h in range(SWA_KV_HEADS):
            o = h * HEAD_DIM
            xh, r = _head_norm(_cols(p_ref, OFF_KB + o))
            dy = _cols(dks_ref, o)
            dgk = dgk + _row_sum(dy * xh)
            dp_ref[:, OFF_KB + o:OFF_KB + o + HEAD_DIM] = _head_norm_bwd(dy * gk_ref[...], xh, r).astype(dt)
        dp_ref[:, OFF_VB:OFF_VB + 256] = dvs_ref[...].astype(dt)

        @pl.when(pl.program_id(0) == 0)
        def _():
            dgq_ref[...] = jnp.zeros_like(dgq_ref)
            dgk_ref[...] = jnp.zeros_like(dgk_ref)

        dgq_ref[...] += dgq
        dgk_ref[...] += dgk

    def row(w):
        return pl.BlockSpec((tr, w), lambda i: (i, 0))

    vec = pl.BlockSpec((1, HEAD_DIM), lambda i: (0, 0))
    return pl.pallas_call(
        body, name="prep_even_bwd", grid=(S // tr,),
        in_specs=[row(EVEN_IN), row(128), row(128), vec, vec, row(512), row(512), row(1024), row(1024),
                  row(1024), row(256), row(256)],
        out_specs=[row(EVEN_IN), vec, vec],
        out_shape=[jax.ShapeDtypeStruct((S, EVEN_IN), BF), jax.ShapeDtypeStruct((1, HEAD_DIM), F32),
                   jax.ShapeDtypeStruct((1, HEAD_DIM), F32)],
        compiler_params=_params(("arbitrary",)),
    )(proj, cos, sin, gq, gk, dqr, dkr, dvr, dga, dqs, dks, dvs)


RET_CHUNK = 512
def _log_sigmoid_tile(logit_tile):
    def body(x_ref, o_ref):
        xv = x_ref[...]
        t = jnp.exp(-jnp.abs(xv))
        log1p_t = jnp.where(t < 1e-3, t * (1.0 - 0.5 * t), jnp.log(1.0 + t))
        o_ref[...] = jnp.minimum(xv, 0.0) - log1p_t

    full = pl.BlockSpec((8, 128), lambda: (0, 0))
    return pl.pallas_call(body, name="log_sigmoid", in_specs=[full], out_specs=full,
                          out_shape=jax.ShapeDtypeStruct((8, 128), F32))(logit_tile)


def _decay(diff, lf, lb):
    return jnp.exp(jnp.where(diff >= 0, lf * diff, -(lb * diff)))


def _col_iota(n):
    return lax.broadcasted_iota(jnp.int32, (n, 1), 0).astype(F32)


def _ret_scan(x, z, lg, asc, desc, name):
    S = x.shape[0]
    C = _pick(S, (RET_CHUNK,))
    nc = S // C
    (arow, aoff), (drow, doff) = asc, desc

    def body(lg_ref, xa_ref, za_ref, xd_ref, zd_ref, asc_ref, desc_ref, sa_ref, sd_ref):
        t = pl.program_id(0)

        @pl.when(t == 0)
        def _():
            sa_ref[...] = jnp.zeros_like(sa_ref)
            sd_ref[...] = jnp.zeros_like(sd_ref)

        j = _col_iota(C)
        for h in range(RET_HEADS):
            la, ld = lg_ref[arow, h], lg_ref[drow, h]
            kc, vc = slice(h * RET_DK, (h + 1) * RET_DK), slice(h * RET_DV, (h + 1) * RET_DV)
            asc_ref[h, 0] = sa_ref[h]
            desc_ref[h, 0] = sd_ref[h]
            xa = (xa_ref[:, kc].astype(F32) * jnp.exp(la * (C - 1 + aoff - j))).astype(xa_ref.dtype)
            xd = (xd_ref[:, kc].astype(F32) * jnp.exp(ld * (j + doff))).astype(xd_ref.dtype)
            sa_ref[h] = jnp.exp(jnp.full((1, RET_DV), la * C, F32)) * sa_ref[h] + _dot(xa, za_ref[:, vc], TN)
            sd_ref[h] = jnp.exp(jnp.full((1, RET_DV), ld * C, F32)) * sd_ref[h] + _dot(xd, zd_ref[:, vc], TN)

    state = jax.ShapeDtypeStruct((RET_HEADS, nc, RET_DK, RET_DV), F32)
    qk_w, v_w = RET_HEADS * RET_DK, RET_HEADS * RET_DV
    return pl.pallas_call(
        body, name=name, grid=(nc,),
        in_specs=[SMEM_SPEC,
                  pl.BlockSpec((C, qk_w), lambda t: (t, 0)), pl.BlockSpec((C, v_w), lambda t: (t, 0)),
                  pl.BlockSpec((C, qk_w), lambda t: (nc - 1 - t, 0)), pl.BlockSpec((C, v_w), lambda t: (nc - 1 - t, 0))],
        out_specs=[pl.BlockSpec((RET_HEADS, 1, RET_DK, RET_DV), lambda t: (0, t, 0, 0)),
                   pl.BlockSpec((RET_HEADS, 1, RET_DK, RET_DV), lambda t: (0, nc - 1 - t, 0, 0))],
        out_shape=[state, state],
        scratch_shapes=[pltpu.VMEM((RET_HEADS, RET_DK, RET_DV), F32), pltpu.VMEM((RET_HEADS, RET_DK, RET_DV), F32)],
        compiler_params=_params(("arbitrary",)),
    )(lg, x, z, x, z)


def _head_cols(h):
    return slice(h * RET_DK, (h + 1) * RET_DK), slice(h * RET_DV, (h + 1) * RET_DV)


def _ret_fwd(q, k, v, lg, sf, sb, proj, gn, carried=None):
    S = q.shape[0]
    C = _pick(S, (RET_CHUNK,))

    def body(lg_ref, q_ref, k_ref, v_ref, sf_ref, sb_ref, gate_ref, gn_ref, y_ref, o_ref):
        diff = (lax.broadcasted_iota(jnp.int32, (C, C), 0) - lax.broadcasted_iota(jnp.int32, (C, C), 1)).astype(F32)
        r = _col_iota(C)
        for h in range(RET_HEADS):
            lf, lb = lg_ref[0, h], lg_ref[1, h]
            kc, vc = _head_cols(h)
            qv = q_ref[:, kc]
            dt = qv.dtype
            y = _dot((_dot(qv, k_ref[:, kc], NT) * _decay(diff, lf, lb)).astype(dt), v_ref[:, vc])
            qf = qv.astype(F32)
            y = y + _dot((qf * jnp.exp(lf * (r + 1.0))).astype(dt), sf_ref[h, 0].astype(dt))
            y = y + _dot((qf * jnp.exp(lb * (C - r))).astype(dt), sb_ref[h, 0].astype(dt))
            y_ref[:, vc] = y
            yh, _ = _head_norm(y)
            gate = gate_ref[:, vc]
            o_ref[:, vc] = (gate * _sigmoid(gate) * (yh * gn_ref[:, vc])).astype(o_ref.dtype)

    state = pl.BlockSpec((RET_HEADS, 1, RET_DK, RET_DV), lambda c: (0, c, 0, 0))
    qk = pl.BlockSpec((C, RET_HEADS * RET_DK), lambda c: (c, 0))
    wide = pl.BlockSpec((C, 1024), lambda c: (c, 0))
    (y, o), landed = _carry_call(
        body, (lg, q, k, v, sf, sb, proj, gn), name="retention_fwd", grid=(S // C,),
        in_specs=[SMEM_SPEC, qk, qk, wide, state, state, pl.BlockSpec((C, 1024), lambda c: (c, OFF_GA // 1024)),
                  pl.BlockSpec((1, 1024), lambda c: (0, 0))],
        out_specs=[wide, wide],
        out_shape=[jax.ShapeDtypeStruct((S, 1024), F32), jax.ShapeDtypeStruct((S, 1024), BF)],
        semantics=("parallel",), carried=carried)
    return y, o, landed


def _ret_bwd(q, k, v, dy, lg, logit, sf, sb, hf, hb):
    S = q.shape[0]
    C = _pick(S, (RET_CHUNK,))
    nc = S // C

    def body(lg_ref, logit_ref, q_ref, k_ref, v_ref, dy_ref, sf_ref, sb_ref, hf_ref, hb_ref,
             dq_ref, dk_ref, dv_ref, dlg_ref, acc_ref):
        c = pl.program_id(0)

        @pl.when(c == 0)
        def _():
            acc_ref[...] = jnp.zeros_like(acc_ref)

        diff = (lax.broadcasted_iota(jnp.int32, (C, C), 0) - lax.broadcasted_iota(jnp.int32, (C, C), 1)).astype(F32)
        r = _col_iota(C)
        for h in range(RET_HEADS):
            one_head(h, diff, r, lg_ref, q_ref, k_ref, v_ref, dy_ref, sf_ref, sb_ref, hf_ref, hb_ref,
                     dq_ref, dk_ref, dv_ref, acc_ref)

        @pl.when(c == nc - 1)
        def _():
            for h in range(RET_HEADS):
                for d in range(2):
                    gate = 1.0 / (1.0 + jnp.exp(jnp.full((8, 128), logit_ref[d, h], F32)))
                    dlg_ref[h, d] = acc_ref[h, d] * gate

    def one_head(h, diff, r, lg_ref, q_ref, k_ref, v_ref, dy_ref, sf_ref, sb_ref, hf_ref, hb_ref,
                 dq_ref, dk_ref, dv_ref, acc_ref):
        lf, lb = lg_ref[0, h], lg_ref[1, h]
        kc, vc = _head_cols(h)
        qv, kv, vv, dyv = q_ref[:, kc], k_ref[:, kc], v_ref[:, vc], dy_ref[:, vc]
        dt = qv.dtype
        qf, kf = qv.astype(F32), kv.astype(F32)
        dec = _decay(diff, lf, lb)
        sc = _dot(qv, kv, NT) * dec
        dp = _dot(dyv, vv, NT)
        da = (dp * dec).astype(dt)
        dq = _dot(da, kv)
        dk = _dot(da, qv, TN)
        dv = _dot(sc.astype(dt), dyv, TN)
        w = sc * dp * diff
        tot_w, tot_f = _all_sum(w), _all_sum(jnp.where(diff >= 0, w, 0.0))
        d_lf, d_lb = tot_f, tot_f - tot_w
        a, b = jnp.exp(lf * (r + 1.0)), jnp.exp(lb * (C - r))
        e, f = jnp.exp(lf * (C - 1.0 - r)), jnp.exp(lb * r)
        sfv, sbv, hfv, hbv = sf_ref[h, 0], sb_ref[h, 0], hf_ref[h, 0], hb_ref[h, 0]
        t_f, t_b = _dot(dyv, sfv.astype(dt), NT), _dot(dyv, sbv.astype(dt), NT)
        u_f, u_b = _dot(vv, hfv.astype(dt), NT), _dot(vv, hbv.astype(dt), NT)
        dq_ref[:, kc] = (dq + a * t_f + b * t_b).astype(dq_ref.dtype)
        dk_ref[:, kc] = (dk + e * u_f + f * u_b).astype(dk_ref.dtype)
        dv_ref[:, vc] = (dv + _dot((kf * e).astype(dt), hfv.astype(dt)) + _dot((kf * f).astype(dt), hbv.astype(dt))).astype(dv_ref.dtype)
        row_q_f = jnp.sum(qf * t_f, axis=-1, keepdims=True)
        row_q_b = jnp.sum(qf * t_b, axis=-1, keepdims=True)
        row_k_f = jnp.sum(kf * u_f, axis=-1, keepdims=True)
        row_k_b = jnp.sum(kf * u_b, axis=-1, keepdims=True)
        gf_c = jnp.exp(jnp.full((1, 1), lf * C, F32))
        gb_c = jnp.exp(jnp.full((1, 1), lb * C, F32))
        d_lf = d_lf + _all_sum((r + 1.0) * a * row_q_f + (C - 1.0 - r) * e * row_k_f) + C * gf_c * _all_sum(hfv * sfv)
        d_lb = d_lb + _all_sum((C - r) * b * row_q_b + r * f * row_k_b) + C * gb_c * _all_sum(hbv * sbv)
        acc_ref[h, 0] += jnp.broadcast_to(d_lf, (8, 128))
        acc_ref[h, 1] += jnp.broadcast_to(d_lb, (8, 128))

    state = pl.BlockSpec((RET_HEADS, 1, RET_DK, RET_DV), lambda c: (0, c, 0, 0))
    qk = pl.BlockSpec((C, RET_HEADS * RET_DK), lambda c: (c, 0))
    vy = pl.BlockSpec((C, RET_HEADS * RET_DV), lambda c: (c, 0))
    return pl.pallas_call(
        body, name="retention_bwd", grid=(nc,),
        in_specs=[SMEM_SPEC, SMEM_SPEC, qk, qk, vy, vy, state, state, state, state],
        out_specs=[qk, qk, vy, pl.BlockSpec((RET_HEADS, 2, 8, 128), lambda c: (0, 0, 0, 0))],
        out_shape=[jax.ShapeDtypeStruct((S, RET_HEADS * RET_DK), BF), jax.ShapeDtypeStruct((S, RET_HEADS * RET_DK), BF),
                   jax.ShapeDtypeStruct((S, RET_HEADS * RET_DV), BF),
                   jax.ShapeDtypeStruct((RET_HEADS, 2, 8, 128), F32)],
        scratch_shapes=[pltpu.VMEM((RET_HEADS, 2, 8, 128), F32)],
        compiler_params=_params(("arbitrary",)),
    )(lg, logit, q, k, v, dy, sf, sb, hf, hb)


def _ret_post_bwd(y, proj, gn, do):
    S = y.shape[0]
    tr = _pick(S, (512,))

    def body(y_ref, g_ref, gn_ref, do_ref, dy_ref, dg_ref, dgn_ref):
        @pl.when(pl.program_id(0) == 0)
        def _():
            dgn_ref[...] = jnp.zeros_like(dgn_ref)

        for h in range(RET_HEADS):
            o = h * RET_DV
            yh, r = _head_norm(_cols(y_ref, o, RET_DV))
            gate = _cols(g_ref, o, RET_DV)
            gnh = gn_ref[:, o:o + RET_DV]
            dout = _cols(do_ref, o, RET_DV).astype(F32)
            sg = _sigmoid(gate)
            dz = dout * (gate * sg)
            dg_ref[:, o:o + RET_DV] = (dout * (yh * gnh) * (sg * (1.0 + gate * (1.0 - sg)))).astype(dg_ref.dtype)
            dgn_ref[:, o:o + RET_DV] += _row_sum(dz * yh)
            dy_ref[:, o:o + RET_DV] = _head_norm_bwd(dz * gnh, yh, r).astype(dy_ref.dtype)

    row = pl.BlockSpec((tr, 1024), lambda i: (i, 0))
    vec = pl.BlockSpec((1, 1024), lambda i: (0, 0))
    return pl.pallas_call(
        body, name="retention_post_bwd", grid=(S // tr,),
        in_specs=[row, pl.BlockSpec((tr, 1024), lambda i: (i, OFF_GA // 1024)), vec, row],
        out_specs=[row, row, vec],
        out_shape=[jax.ShapeDtypeStruct((S, 1024), BF), jax.ShapeDtypeStruct((S, 1024), BF),
                   jax.ShapeDtypeStruct((1, 1024), F32)],
        compiler_params=_params(("arbitrary",)),
    )(y, proj, gn, do)


def _t5_bucket_map():
    r = jnp.arange(BLOCK)
    j = jnp.arange(3 * BLOCK)
    rel = j[None, :] - BLOCK - r[:, None]
    nb = T5_BUCKETS // 2
    max_exact = nb // 2
    ret = jnp.where(rel > 0, nb, 0)
    n = jnp.abs(rel)
    nf = jnp.maximum(n, 1).astype(jnp.float32)
    large = max_exact + (jnp.log(nf / max_exact) / math.log(T5_MAX_DIST / max_exact)
                         * (nb - max_exact)).astype(jnp.int32)
    large = jnp.minimum(large, nb - 1)
    bucket = ret + jnp.where(n < max_exact, n, large)
    return jnp.where(jnp.abs(rel) <= WINDOW, bucket, -1).astype(jnp.int32)


SWA_G = SWA_HEADS // SWA_KV_HEADS
SWA_LANES = SWA_G * BLOCK


def _t5_bias(table, bucket_t):
    def body(t_ref, b_ref, o_ref):
        bk = b_ref[...]
        for h in range(SWA_HEADS):
            acc = jnp.full(bk.shape, NEG_INF, F32)
            for b in range(T5_BUCKETS):
                acc = jnp.where(bk == b, t_ref[b, h], acc)
            o_ref[h // SWA_G, :, (h % SWA_G) * BLOCK:(h % SWA_G + 1) * BLOCK] = acc

    return pl.pallas_call(
        body, name="t5_bias", in_specs=[SMEM_SPEC, pl.BlockSpec((3 * BLOCK, BLOCK), lambda: (0, 0))],
        out_specs=pl.BlockSpec((SWA_KV_HEADS, 3 * BLOCK, SWA_LANES), lambda: (0, 0, 0)),
        out_shape=jax.ShapeDtypeStruct((SWA_KV_HEADS, 3 * BLOCK, SWA_LANES), F32),
    )(table, bucket_t)


def _t5_table_grad(dbias, bucket_t):
    def body(d_ref, b_ref, o_ref):
        bk = b_ref[...]
        lane = lax.broadcasted_iota(jnp.int32, (1, 128), 1)
        for b in range(T5_BUCKETS):
            hit = bk == b
            row = jnp.zeros((1, 128), F32)
            for h in range(SWA_HEADS):
                d = d_ref[h // SWA_G, :, (h % SWA_G) * BLOCK:(h % SWA_G + 1) * BLOCK]
                row = row + jnp.where(lane == h, _all_sum(jnp.where(hit, d, 0.0)), 0.0)
            o_ref[b:b + 1, :] = row

    return pl.pallas_call(
        body, name="t5_table_grad",
        in_specs=[pl.BlockSpec((SWA_KV_HEADS, 3 * BLOCK, SWA_LANES), lambda: (0, 0, 0)),
                  pl.BlockSpec((3 * BLOCK, BLOCK), lambda: (0, 0))],
        out_specs=pl.BlockSpec((T5_BUCKETS, 128), lambda: (0, 0)),
        out_shape=jax.ShapeDtypeStruct((T5_BUCKETS, 128), F32),
    )(dbias, bucket_t)


def _swa_scores(i, nb, q4, kw, bias_t, sink_row):
    s = _dot(kw, q4, NT) + bias_t
    row = lax.broadcasted_iota(jnp.int32, s.shape, 0)
    first_row = jnp.where(i == 0, BLOCK, 0)
    end_row = jnp.where(i == nb - 1, 2 * BLOCK, 3 * BLOCK)
    s = jnp.where((row < first_row) | (row >= end_row), NEG_INF, s)
    m = jnp.maximum(jnp.max(s, axis=0, keepdims=True), sink_row)
    p = jnp.exp(s - m)
    e_sink = jnp.exp(sink_row - m)
    inv = 1.0 / (jnp.sum(p, axis=0, keepdims=True) + e_sink)
    return p * inv, e_sink * inv


def _swa_group(q_ref, sink_ref, kh):
    heads = range(kh * SWA_G, (kh + 1) * SWA_G)
    q4 = jnp.concatenate([_cols(q_ref, h * HEAD_DIM) for h in heads], axis=0)
    sink_row = jnp.concatenate([jnp.full((1, BLOCK), sink_ref[0, h], F32) for h in heads], axis=1)
    return q4, sink_row


def _swa_unstack(ref, kh, x_t):
    for g in range(SWA_G):
        h = kh * SWA_G + g
        ref[:, h * HEAD_DIM:(h + 1) * HEAD_DIM] = x_t[:, g * BLOCK:(g + 1) * BLOCK].T.astype(ref.dtype)


def _swa_window(ref, i, nb, off):
    prev, nxt = jnp.maximum(i - 1, 0), jnp.minimum(i + 1, nb - 1)
    rows = [pl.ds(pl.multiple_of(b * BLOCK, BLOCK), BLOCK) for b in (prev, i, nxt)]
    return jnp.concatenate([ref[r, off:off + HEAD_DIM] for r in rows], axis=0), rows


def _swa_fwd(q, k, v, bias, sink, carried=None):
    S = q.shape[0]
    nb = S // BLOCK

    def body(sink_ref, q_ref, k_ref, v_ref, bias_ref, o_ref):
        i = pl.program_id(0)
        for kh in range(SWA_KV_HEADS):
            kw, _ = _swa_window(k_ref, i, nb, kh * HEAD_DIM)
            vw, _ = _swa_window(v_ref, i, nb, kh * HEAD_DIM)
            q4, sink_row = _swa_group(q_ref, sink_ref, kh)
            p, _ = _swa_scores(i, nb, q4, kw, bias_ref[kh], sink_row)
            _swa_unstack(o_ref, kh, _dot(vw, p.astype(vw.dtype), TN))

    full_kv = pl.BlockSpec((S, SWA_KV_HEADS * HEAD_DIM), lambda i: (0, 0))
    (o,), landed = _carry_call(
        body, (sink, q, k, v, bias), name="swa_fwd", grid=(nb,),
        in_specs=[SMEM_SPEC, pl.BlockSpec((BLOCK, 1024), lambda i: (i, 0)), full_kv, full_kv,
                  pl.BlockSpec((SWA_KV_HEADS, 3 * BLOCK, SWA_LANES), lambda i: (0, 0, 0))],
        out_specs=[pl.BlockSpec((BLOCK, 1024), lambda i: (i, 0))],
        out_shape=[jax.ShapeDtypeStruct((S, 1024), BF)], semantics=("parallel",), carried=carried)
    return o, landed


def _swa_bwd(q, k, v, do, bias, sink, carried=None):
    S = q.shape[0]
    nb = S // BLOCK

    def body(sink_ref, q_ref, k_ref, v_ref, do_ref, bias_ref, dq_ref, dk_ref, dv_ref, dbias_ref, dsink_ref):
        i = pl.program_id(0)

        @pl.when(i == 0)
        def _():
            dk_ref[...] = jnp.zeros_like(dk_ref)
            dv_ref[...] = jnp.zeros_like(dv_ref)
            dbias_ref[...] = jnp.zeros_like(dbias_ref)
            dsink_ref[...] = jnp.zeros_like(dsink_ref)

        for kh in range(SWA_KV_HEADS):
            off = kh * HEAD_DIM
            kw, rows = _swa_window(k_ref, i, nb, off)
            vw, _ = _swa_window(v_ref, i, nb, off)
            q4, sink_row = _swa_group(q_ref, sink_ref, kh)
            p, p_sink = _swa_scores(i, nb, q4, kw, bias_ref[kh], sink_row)
            do4 = jnp.concatenate([_cols(do_ref, (kh * SWA_G + g) * HEAD_DIM) for g in range(SWA_G)], axis=0).astype(vw.dtype)
            dp = _dot(vw, do4, NT)
            delta = jnp.sum(p * dp, axis=0, keepdims=True)
            ds = p * (dp - delta)
            dsb = ds.astype(q4.dtype)
            _swa_unstack(dq_ref, kh, _dot(kw, dsb, TN))
            dkw = _dot(dsb, q4)
            dvw = _dot(p.astype(do4.dtype), do4)
            dbias_ref[kh] += ds
            sink_term = p_sink * delta
            for g in range(SWA_G):
                h = kh * SWA_G + g
                dsink_ref[h:h + 1, :] += jnp.broadcast_to(-_all_sum(sink_term[:, g * BLOCK:(g + 1) * BLOCK]), (1, 128))
            for b, r in enumerate(rows):
                dk_ref[r, off:off + HEAD_DIM] += dkw[b * BLOCK:(b + 1) * BLOCK]
                dv_ref[r, off:off + HEAD_DIM] += dvw[b * BLOCK:(b + 1) * BLOCK]

    full_kv = pl.BlockSpec((S, SWA_KV_HEADS * HEAD_DIM), lambda i: (0, 0))
    blk = pl.BlockSpec((BLOCK, 1024), lambda i: (i, 0))
    bias_spec = pl.BlockSpec((SWA_KV_HEADS, 3 * BLOCK, SWA_LANES), lambda i: (0, 0, 0))
    outs, landed = _carry_call(
        body, (sink, q, k, v, do, bias), name="swa_bwd", grid=(nb,),
        in_specs=[SMEM_SPEC, blk, full_kv, full_kv, blk, bias_spec],
        out_specs=[blk, full_kv, full_kv, bias_spec, pl.BlockSpec((8, 128), lambda i: (0, 0))],
        out_shape=[jax.ShapeDtypeStruct((S, 1024), BF), jax.ShapeDtypeStruct((S, 256), F32),
                   jax.ShapeDtypeStruct((S, 256), F32),
                   jax.ShapeDtypeStruct((SWA_KV_HEADS, 3 * BLOCK, SWA_LANES), F32), jax.ShapeDtypeStruct((8, 128), F32)],
        vmem=VMEM_LARGE, semantics=("arbitrary",), carried=carried)
    return (*outs, landed)


def _prep_odd_fwd(proj, cos, sin, gq, gk):
    S = proj.shape[0]
    tr = _pick(S, (512,))

    def body(p_ref, cos_ref, sin_ref, gq_ref, gk_ref, q_ref, k_ref, v_ref):
        cos_v, sin_v = cos_ref[...], sin_ref[...]
        for h in range(AX_HEADS):
            o = h * HEAD_DIM
            xh, _ = _head_norm(_cols(p_ref, o))
            q_ref[:, o:o + HEAD_DIM] = (_rope(xh * gq_ref[...], cos_v, sin_v, 32) * AX_SCALE).astype(q_ref.dtype)
        for h in range(AX_KV_HEADS):
            o = h * HEAD_DIM
            xh, _ = _head_norm(_cols(p_ref, 1024 + o))
            k_ref[:, o:o + HEAD_DIM] = _rope(xh * gk_ref[...], cos_v, sin_v, 32).astype(k_ref.dtype)
        v_ref[...] = p_ref[:, 1280:1536].astype(v_ref.dtype)

    def row(w):
        return pl.BlockSpec((tr, w), lambda i: (i, 0))

    vec = pl.BlockSpec((1, HEAD_DIM), lambda i: (0, 0))
    return pl.pallas_call(
        body, name="prep_odd_fwd", grid=(S // tr,),
        in_specs=[row(ODD_IN), row(128), row(128), vec, vec], out_specs=[row(1024), row(256), row(256)],
        out_shape=[jax.ShapeDtypeStruct((S, w), BF) for w in (1024, 256, 256)],
        compiler_params=_params(("parallel",)),
    )(proj, cos, sin, gq, gk)


def _prep_odd_bwd(proj, cos, sin, gq, gk, dq, dk, dv):
    S = proj.shape[0]
    tr = _pick(S, (512,))

    def body(p_ref, cos_ref, sin_ref, gq_ref, gk_ref, dq_ref, dk_ref, dv_ref, dp_ref, dgq_ref, dgk_ref):
        cos_v, sin_v = cos_ref[...], sin_ref[...]
        dt = dp_ref.dtype
        dgq = jnp.zeros((1, HEAD_DIM), F32)
        for h in range(AX_HEADS):
            o = h * HEAD_DIM
            xh, r = _head_norm(_cols(p_ref, o))
            dy = _rope_t(_cols(dq_ref, o).astype(F32) * AX_SCALE, cos_v, sin_v, 32)
            dgq = dgq + _row_sum(dy * xh)
            dp_ref[:, o:o + HEAD_DIM] = _head_norm_bwd(dy * gq_ref[...], xh, r).astype(dt)
        dgk = jnp.zeros((1, HEAD_DIM), F32)
        for h in range(AX_KV_HEADS):
            o = h * HEAD_DIM
            xh, r = _head_norm(_cols(p_ref, 1024 + o))
            dy = _rope_t(_cols(dk_ref, o), cos_v, sin_v, 32)
            dgk = dgk + _row_sum(dy * xh)
            dp_ref[:, 1024 + o:1024 + o + HEAD_DIM] = _head_norm_bwd(dy * gk_ref[...], xh, r).astype(dt)
        dp_ref[:, 1280:1536] = dv_ref[...].astype(dt)

        @pl.when(pl.program_id(0) == 0)
        def _():
            dgq_ref[...] = jnp.zeros_like(dgq_ref)
            dgk_ref[...] = jnp.zeros_like(dgk_ref)

        dgq_ref[...] += dgq
        dgk_ref[...] += dgk

    def row(w):
        return pl.BlockSpec((tr, w), lambda i: (i, 0))

    vec = pl.BlockSpec((1, HEAD_DIM), lambda i: (0, 0))
    return pl.pallas_call(
        body, name="prep_odd_bwd", grid=(S // tr,),
        in_specs=[row(ODD_IN), row(128), row(128), vec, vec, row(1024), row(256), row(256)],
        out_specs=[row(ODD_IN), vec, vec],
        out_shape=[jax.ShapeDtypeStruct((S, ODD_IN), BF), jax.ShapeDtypeStruct((1, HEAD_DIM), F32),
                   jax.ShapeDtypeStruct((1, HEAD_DIM), F32)],
        compiler_params=_params(("arbitrary",)),
    )(proj, cos, sin, gq, gk, dq, dk, dv)


def _loop_unrolled(n, factor, step, init):
    while n % factor:
        factor //= 2

    def trip(t, carry):
        for u in range(factor):
            carry = step(factor * t + u, carry)
        return carry

    return lax.fori_loop(0, n // factor, trip, init)


def _flash_fwd(q, k, v, carried=None):
    S = q.shape[0]
    tq = _pick(S, (1024, 512))
    tk = _pick(S, (2048, 1024, 512))
    nk = S // tk
    G = AX_HEADS // AX_KV_HEADS

    def body(q_ref, k_ref, v_ref, o_ref, lse_ref):
        qv = q_ref[...]

        def step(j, carry):
            m, l, acc = carry
            rows = pl.ds(pl.multiple_of(j * tk, tk), tk)
            s = _dot(qv, k_ref[rows, :], NT)
            m_new = jnp.maximum(m, jnp.max(s, axis=-1, keepdims=True))
            alpha = jnp.exp2(m - m_new)
            p = jnp.exp2(s - m_new)
            l = alpha * l + jnp.sum(p, axis=-1, keepdims=True)
            acc = alpha * acc + _dot(p.astype(v_ref.dtype), v_ref[rows, :])
            return m_new, l, acc

        init = (jnp.full((tq, 1), NEG_INF, F32), jnp.zeros((tq, 1), F32), jnp.zeros((tq, HEAD_DIM), F32))
        m, l, acc = _loop_unrolled(nk, 8, step, init)
        o_ref[...] = (acc / l).astype(o_ref.dtype)
        lse_ref[0] = jnp.broadcast_to(m + jnp.log2(l), (tq, 128))

    (o, lse), landed = _carry_call(
        body, (q, k, v), name="flash_fwd", grid=(AX_HEADS, S // tq),
        in_specs=[pl.BlockSpec((tq, HEAD_DIM), lambda h, i: (i, h)),
                  pl.BlockSpec((S, HEAD_DIM), lambda h, i: (0, h // G)),
                  pl.BlockSpec((S, HEAD_DIM), lambda h, i: (0, h // G))],
        out_specs=[pl.BlockSpec((tq, HEAD_DIM), lambda h, i: (i, h)),
                   pl.BlockSpec((1, tq, 128), lambda h, i: (h, i, 0))],
        out_shape=[jax.ShapeDtypeStruct((S, AX_HEADS * HEAD_DIM), BF), jax.ShapeDtypeStruct((AX_HEADS, S, 128), F32)],
        semantics=("parallel", "parallel"), carried=carried)
    return o, lse, landed


def _flash_bwd(q, k, v, o, do, lse, carried=None):
    S = q.shape[0]
    tq = _pick(S, (1024, 512))
    tk = _pick(S, (1024, 512))
    nq, nk = S // tq, S // tk
    G = AX_HEADS // AX_KV_HEADS

    def body(q_ref, k_ref, v_ref, o_ref, do_ref, lse_ref, dq_ref, dk_ref, dv_ref):
        g, i = pl.program_id(1), pl.program_id(2)

        @pl.when((g == 0) & (i == 0))
        def _():
            dk_ref[...] = jnp.zeros_like(dk_ref)
            dv_ref[...] = jnp.zeros_like(dv_ref)

        qv = q_ref[...]
        do_f = do_ref[...].astype(F32)
        dob = do_f.astype(qv.dtype)
        dob_ln2 = (do_f * LN2).astype(qv.dtype)
        delta = jnp.sum(do_f * o_ref[...].astype(F32), axis=-1, keepdims=True) * LN2
        lse_col = lse_ref[0][:, 0:1]

        def step(j, dq):
            rows = pl.ds(pl.multiple_of(j * tk, tk), tk)
            kj, vj = k_ref[rows, :], v_ref[rows, :]
            p = jnp.exp2(_dot(qv, kj, NT) - lse_col)
            dp = _dot(dob_ln2, vj, NT)
            ds = (p * (dp - delta)).astype(qv.dtype)
            dk_ref[rows, :] += _dot(ds, qv, TN)
            dv_ref[rows, :] += _dot(p.astype(dob.dtype), dob, TN)
            return dq + _dot(ds, kj)

        dq_ref[...] = _loop_unrolled(nk, 8, step, jnp.zeros((tq, HEAD_DIM), F32)).astype(dq_ref.dtype)

    q_spec = pl.BlockSpec((tq, HEAD_DIM), lambda kh, g, i: (i, kh * G + g))
    kv_spec = pl.BlockSpec((S, HEAD_DIM), lambda kh, g, i: (0, kh))
    (dq, dk, dv), landed = _carry_call(
        body, (q, k, v, o, do, lse), name="flash_bwd", grid=(AX_KV_HEADS, G, nq),
        in_specs=[q_spec, kv_spec, kv_spec, q_spec, q_spec,
                  pl.BlockSpec((1, tq, 128), lambda kh, g, i: (kh * G + g, i, 0))],
        out_specs=[q_spec, kv_spec, kv_spec],
        out_shape=[jax.ShapeDtypeStruct((S, AX_HEADS * HEAD_DIM), BF), jax.ShapeDtypeStruct((S, 256), F32),
                   jax.ShapeDtypeStruct((S, 256), F32)],
        vmem=VMEM_LARGE, semantics=("arbitrary", "arbitrary", "arbitrary"), carried=carried)
    return dq, dk, dv, landed


def _rope_angles(pos, dim, theta):
    inv = theta ** (-jnp.arange(0, dim, 2, dtype=jnp.float32) / dim)
    return pos.astype(jnp.float32)[:, None] * inv[None, :]


def _rope_tables(S):
    ang = _rope_angles(jnp.arange(S), RET_DK, RET_THETA)
    c, s = jnp.cos(ang), jnp.sin(ang)
    ret = (jnp.concatenate([c, c], -1), jnp.concatenate([-s, s], -1))
    rows = S // GRID_W
    ar, ac = _rope_angles(jnp.arange(rows), HEAD_DIM // 2, AX_THETA), _rope_angles(jnp.arange(GRID_W), HEAD_DIM // 2, AX_THETA)
    cr, sr = jnp.repeat(jnp.cos(ar), GRID_W, axis=0), jnp.repeat(jnp.sin(ar), GRID_W, axis=0)
    cc, sc = jnp.tile(jnp.cos(ac), (rows, 1)), jnp.tile(jnp.sin(ac), (rows, 1))
    ax = (jnp.concatenate([cr, cr, cc, cc], -1), jnp.concatenate([-sr, sr, -sc, sc], -1))
    return ret, ax


def _pad_tile(a):
    return jnp.pad(a.astype(F32), ((0, 8 - a.shape[0]), (0, 128 - a.shape[1])))


def _relu2_epilogue(acc):
    r = jnp.maximum(acc, 0.0)
    return acc, r * r


def _relu2_bwd_epilogue(acc, u):
    return (acc * (2.0 * jnp.maximum(u.astype(F32), 0.0)),)


def _add_epilogue(acc, res):
    return (acc + res,)


def _add_norm_epilogue(acc, res, g):
    y = acc + res
    r = lax.rsqrt(jnp.mean(y * y, axis=-1, keepdims=True) + EPS)
    return y, y * r * g


def _residual_mm(a, w, res, g_next, name, b_rows=None):
    return _mm(a, w, name=name, extras=(res,), rows=(g_next,), out_dtypes=(F32, BF), epilogue=_add_norm_epilogue,
               vmem=VMEM_LARGE, b_rows=b_rows)


def _rms_bwd_epilogue(dh, x, dres, g):
    r = lax.rsqrt(jnp.mean(x * x, axis=-1, keepdims=True) + EPS)
    xh = x * r
    dxh = dh * g
    dx = r * (dxh - xh * jnp.mean(dxh * xh, axis=-1, keepdims=True)) + dres
    return dx, dx, _row_sum(dh * xh)


def _norm_bwd_mm(mm, dproj, w, x, dres, g, name):
    return mm(dproj, w, tb=True, name=name, out_dtypes=(F32, BF), extras=(x, dres), rows=(g,),
              epilogue=_rms_bwd_epilogue, n_row_sums=1, tm=512)


def _loss_epilogue(acc, res, target):
    e = acc + res - target
    d = e * (1.0 / D_MODEL)
    return d, d, _row_sum(e * e)


def _mlp_fwd(x, h, w_up, w_down, tag, g_next=None, target=None):
    u, a = _mm(h, w_up, name=f"mlp_up_{tag}", out_dtypes=(BF, BF), epilogue=_relu2_epilogue)
    if target is not None:
        return _mm(a, w_down, name=f"mlp_down_{tag}", extras=(x, target), out_dtypes=(F32, BF), epilogue=_loss_epilogue,
                   n_row_sums=1, tm=512), (h, u, a)
    if g_next is None:
        y, h_next = _mm(a, w_down, name=f"mlp_down_{tag}", extras=(x,), epilogue=_add_epilogue), None
    else:
        y, h_next = _residual_mm(a, w_down, x, g_next, f"mlp_down_{tag}")
    return y, h_next, (h, u, a)


def _mlp_bwd(x, g, w_up, w_down, saved, dy, dyb, tag, mm_exchange=None):
    h, u, a = saved
    du = _mm(dyb, w_down, tb=True, name=f"mlp_down_dx_{tag}", out_dtypes=(BF,), extras=(u,), epilogue=_relu2_bwd_epilogue)
    dw_down = _mm(a, dyb, ta=True, name=f"mlp_down_dw_{tag}", out_dtypes=(BF,))
    dw_up = _mm(h, du, ta=True, name=f"mlp_up_dw_{tag}", out_dtypes=(BF,))
    if mm_exchange is None:
        mm = _mm
    else:
        mm = lambda *args, **kw: mm_exchange([("down" + tag, dw_down)], *args, **kw)
    dx, dxb, dg = _norm_bwd_mm(mm, du, w_up, x, dy, g, f"mlp_up_dx_{tag}")
    return dx, dxb, dg, dw_up, dw_down


COL_SHARDED = ("in_even", "in_odd", "up0", "up1")


def _assemble(key, g):
    if key in COL_SHARDED:
        return g.transpose(1, 0, 2).reshape(g.shape[1], N_DEV * g.shape[2])
    return g.reshape(N_DEV * g.shape[1], g.shape[2])


def _split(key, full):
    rows, cols = full.shape
    if key in COL_SHARDED:
        return full.reshape(rows, N_DEV, cols // N_DEV).transpose(1, 0, 2)
    return full.reshape(N_DEV, rows // N_DEV, cols)


def _local_step(x, target, W, P, late=None):
    S = x.shape[0]
    W = dict(W)
    landed = {}

    def gather_while(stage):
        return None if late is None else _Exchange([s for _, s in late[stage]], gather=True)

    def arrived(stage, outs):
        for (key, _), g in zip([] if late is None else late[stage], outs):
            W[key] = _assemble(key, g)

    def exchange_while(grads):
        return None if late is None else _Exchange([_split(k, g) for k, g in grads], gather=False)

    def left(grads, outs):
        for (key, _), l in zip(grads, outs):
            landed[key] = l

    def mm_gather(stage, *args, **kw):
        if late is None:
            return _mm(*args, **kw)
        out, outs = _mm(*args, carried=gather_while(stage), **kw)
        arrived(stage, outs)
        return out

    def mm_exchange(grads, *args, **kw):
        if late is None:
            return _mm(*args, **kw)
        out, outs = _mm(*args, carried=exchange_while(grads), **kw)
        left(grads, outs)
        return out

    (cos_r, sin_r), (cos_a, sin_a) = _rope_tables(S)
    bucket = _t5_bucket_map().T
    logit = P["ret_decay_logit"]
    lg = _log_sigmoid_tile(_pad_tile(logit))
    bias = _t5_bias(P["t5_table"], bucket)
    nmix, nmlp = P["norm_mix"], P["norm_mlp"]

    h0 = _rms_fwd(x, nmix[0:1], "mix_norm_0")
    proj_e = mm_gather("in_even", h0, W["in_even"], name="in_even")
    (qr, kr, vr, qs, ks, vs), outs = _prep_even_fwd(proj_e, cos_r, sin_r, P["swa_q_norm"], P["swa_k_norm"],
                                                    gather_while("prep_even_fwd"))
    arrived("prep_even_fwd", outs)
    sf, sb = _ret_scan(kr, vr, lg, (0, 0), (1, 0), "retention_states")
    y_ret, oa, outs = _ret_fwd(qr, kr, vr, lg, sf, sb, proj_e, P["ret_norm"], gather_while("retention_fwd"))
    arrived("retention_fwd", outs)
    ob, outs = _swa_fwd(qs, ks, vs, bias, P["swa_sink"], gather_while("swa_fwd"))
    arrived("swa_fwd", outs)
    wo, rows_a, rows_b = W["out_even"], (0, 1024), (1024, 1024)
    x1 = _mm(oa, wo, name="out_even_a", extras=(x,), epilogue=_add_epilogue, b_rows=rows_a)
    x1, h1 = _residual_mm(ob, wo, x1, nmlp[0:1], "out_even_b", b_rows=rows_b)
    x2, h2, mlp0 = _mlp_fwd(x1, h1, W["up0"], W["down0"], "0", g_next=nmix[1:2])

    proj_o = _mm(h2, W["in_odd"], name="in_odd")
    qx, kx, vx = _prep_odd_fwd(proj_o, cos_a, sin_a, P["ax_q_norm"], P["ax_k_norm"])
    ox, lse, outs = _flash_fwd(qx, kx, vx, gather_while("flash_fwd"))
    arrived("flash_fwd", outs)
    x3, h3 = _residual_mm(ox, W["out_odd"], x2, nmlp[1:2], "out_odd")
    (d4, d4b, loss_row), mlp1 = _mlp_fwd(x3, h3, W["up1"], W["down1"], "1", target=target)

    d3, d3b, dnmlp1, dw_up1, dw_down1 = _mlp_bwd(x3, nmlp[1:2], W["up1"], W["down1"], mlp1, d4, d4b, "1")
    dox = _mm(d3b, W["out_odd"], tb=True, name="out_odd_dx")
    dw_out_odd = _mm(ox, d3b, ta=True, name="out_odd_dw", out_dtypes=(BF,))
    grads1 = [("up1", dw_up1), ("down1", dw_down1), ("out_odd", dw_out_odd)]
    dqx, dkx, dvx, outs = _flash_bwd(qx, kx, vx, ox, dox, lse, exchange_while(grads1))
    left(grads1, outs)
    dproj_o, dgq_ax, dgk_ax = _prep_odd_bwd(proj_o, cos_a, sin_a, P["ax_q_norm"], P["ax_k_norm"], dqx, dkx, dvx)
    dw_in_odd = _mm(h2, dproj_o, ta=True, name="in_odd_dw", out_dtypes=(BF,))
    d2, d2b, dnmix1 = _norm_bwd_mm(_mm, dproj_o, W["in_odd"], x2, d3, nmix[1:2], "in_odd_dx")

    d1, d1b, dnmlp0, dw_up0, dw_down0 = _mlp_bwd(x1, nmlp[0:1], W["up0"], W["down0"], mlp0, d2, d2b, "0", mm_exchange)
    doa = _mm(d1b, wo, tb=True, name="out_even_a_dx", b_rows=rows_a)
    dob = _mm(d1b, wo, tb=True, name="out_even_b_dx", b_rows=rows_b)
    dw_out_even = jnp.concatenate([_mm(oa, d1b, ta=True, name="out_even_a_dw", out_dtypes=(BF,)),
                                   _mm(ob, d1b, ta=True, name="out_even_b_dw", out_dtypes=(BF,))], axis=0)
    dy_ret, dga, dret_norm = _ret_post_bwd(y_ret, proj_e, P["ret_norm"], doa)
    hb, hf = _ret_scan(qr, dy_ret, lg, (1, 1), (0, 1), "retention_state_grads")
    dqr, dkr, dvr, dlogit = _ret_bwd(qr, kr, vr, dy_ret, lg, logit, sf, sb, hf, hb)
    grads0 = [("in_odd", dw_in_odd), ("out_even", dw_out_even)]
    dqs, dks, dvs, dbias, dsink, outs = _swa_bwd(qs, ks, vs, dob, bias, P["swa_sink"], exchange_while(grads0))
    left(grads0, outs)
    dt5 = _t5_table_grad(dbias, bucket)
    dproj_e, dgq_swa, dgk_swa = _prep_even_bwd(proj_e, cos_r, sin_r, P["swa_q_norm"], P["swa_k_norm"],
                                               dqr, dkr, dvr, dga, dqs, dks, dvs)
    dw_in_even = mm_exchange([("up0", dw_up0)], h0, dproj_e, ta=True, name="in_even_dw", out_dtypes=(BF,))
    dx, _, dnmix0 = _norm_bwd_mm(lambda *args, **kw: mm_exchange([("in_even", dw_in_even)], *args, **kw),
                                 dproj_e, W["in_even"], x, d1, nmix[0:1], "in_even_dx")

    if late is None:
        dW = dict(grads1 + grads0, up0=dw_up0, down0=dw_down0, in_even=dw_in_even)
    else:
        dW = landed
    dP = {"norm_mix": jnp.concatenate([dnmix0, dnmix1], 0), "norm_mlp": jnp.concatenate([dnmlp0, dnmlp1], 0),
          "ret_decay_logit": dlogit[:, :, 0, 0].T, "ret_norm": dret_norm,
          "swa_q_norm": dgq_swa, "swa_k_norm": dgk_swa, "swa_sink": dsink[:, 0][None, :],
          "t5_table": dt5[:, :SWA_HEADS], "ax_q_norm": dgq_ax, "ax_k_norm": dgk_ax}
    return loss_row, dx, dW, dP


def _cast_shards(shards):
    n = len(shards)

    def body(*refs):
        for i_ref, o_ref in zip(refs[:n], refs[n:]):
            o_ref[...] = i_ref[...].astype(o_ref.dtype)

    return pl.pallas_call(body, name="cast_shards", in_specs=[VMEM_SPEC] * n, out_specs=[VMEM_SPEC] * n,
                          out_shape=[jax.ShapeDtypeStruct(s.shape, BF) for s in shards],
                          compiler_params=pltpu.CompilerParams(vmem_limit_bytes=VMEM_SMALL))(*shards)


def _all_gather(shards):
    n = len(shards)

    def body(*refs):
        ins, outs, stage = refs[:n], refs[n:2 * n], refs[2 * n:3 * n]
        send_sems, recv_sems, local_sems = refs[3 * n:]
        me = _my_place()
        sibling = _flip(me, 1)
        chips = [_flip(me, 4), _flip(me, 2), _flip(me, 6)]

        def copy(a, k, block, to, src=None):
            dst = outs[a].at[_index(block)]
            return pltpu.make_async_remote_copy(
                src_ref=dst if src is None else src, dst_ref=dst,
                send_sem=send_sems.at[a, k], recv_sem=recv_sems.at[a, k], device_id=to, device_id_type=MESH)

        first, mine = [], []
        for a in range(n):
            stage[a][...] = ins[a][...].astype(stage[a].dtype)
            mine.append(pltpu.make_async_copy(stage[a], outs[a].at[_index(me)], local_sems.at[a]))
            mine[-1].start()
            first.append(copy(a, 0, me, sibling, src=stage[a]))
            first += [copy(a, 1 + j, me, chip, src=stage[a]) for j, chip in enumerate(chips)]
        for cp in first:
            cp.start()
        passed = []
        for a in range(n):
            for j, chip in enumerate(chips):
                copy(a, 1 + j, chip, me).wait_recv()
                passed.append(copy(a, 4 + j, chip, sibling))
                passed[-1].start()
        for a in range(n):
            copy(a, 0, sibling, me).wait_recv()
            for j, chip in enumerate(chips):
                copy(a, 4 + j, _flip(chip, 1), me).wait_recv()
        for cp in first + passed:
            cp.wait_send()
        for cp in mine:
            cp.wait()

    return pl.pallas_call(
        body, name="weights_all_gather",
        in_specs=[VMEM_SPEC] * n, out_specs=[ANY_SPEC] * n,
        out_shape=[jax.ShapeDtypeStruct((N_DEV,) + s.shape, BF) for s in shards],
        scratch_shapes=[pltpu.VMEM(s.shape, BF) for s in shards]
        + [pltpu.SemaphoreType.DMA((n, 7)), pltpu.SemaphoreType.DMA((n, 7)), pltpu.SemaphoreType.DMA((n,))],
        compiler_params=pltpu.CompilerParams(vmem_limit_bytes=VMEM_SMALL),
    )(*shards)


def _all_reduce_small(part):
    R, C = part.shape

    def body(x_ref, o_ref, land_ref, send_sems, recv_sems):
        me = _my_place()
        land_ref[_index(me)] = x_ref[...]
        copies = []
        for k in range(1, N_DEV):
            peer = _flip(me, k)
            copies.append(pltpu.make_async_remote_copy(
                src_ref=x_ref, dst_ref=land_ref.at[_index(me)],
                send_sem=send_sems.at[k - 1], recv_sem=recv_sems.at[k - 1], device_id=peer, device_id_type=MESH))
            copies[-1].start()
        for k in range(1, N_DEV):
            peer = _flip(me, k)
            pltpu.make_async_remote_copy(
                src_ref=x_ref, dst_ref=land_ref.at[_index(peer)],
                send_sem=send_sems.at[k - 1], recv_sem=recv_sems.at[k - 1], device_id=peer, device_id_type=MESH).wait_recv()
        for cp in copies:
            cp.wait_send()
        acc = land_ref[0]
        for s in range(1, N_DEV):
            acc = acc + land_ref[s]
        o_ref[...] = acc
        o_ref[LOSS_ROW:LOSS_ROW + 1, :] = jnp.broadcast_to(_all_sum(acc[LOSS_ROW:LOSS_ROW + 1, :]) * (0.5 / D_MODEL), (1, C))

    return pl.pallas_call(
        body, name="small_all_reduce", in_specs=[VMEM_SPEC], out_specs=VMEM_SPEC,
        out_shape=jax.ShapeDtypeStruct((R, C), F32),
        scratch_shapes=[pltpu.VMEM((N_DEV, R, C), F32), pltpu.SemaphoreType.DMA((7,)), pltpu.SemaphoreType.DMA((7,))],
    )(part)


def _adamw_math(w, g, m, v):
    m = ADAM_B1 * m + (1.0 - ADAM_B1) * g
    v = ADAM_B2 * v + (1.0 - ADAM_B2) * jnp.square(g)
    m_hat = m / (1.0 - ADAM_B1 ** ADAM_STEP)
    v_hat = v / (1.0 - ADAM_B2 ** ADAM_STEP)
    delta = -ADAM_LR * (m_hat / (jnp.sqrt(v_hat) + ADAM_EPS) + ADAM_WD * w)
    return delta, m, v


def _sum_and_adamw(landed, w, m, v, name):
    R, C = w.shape
    tr = _pick(R, (256, 128))

    def body(l_ref, w_ref, m_ref, v_ref, g_ref, d_ref, nm_ref, nv_ref):
        g = l_ref[0].astype(F32)
        for s in range(1, N_DEV):
            g = g + l_ref[s].astype(F32)
        g_ref[...] = g
        d_ref[...], nm_ref[...], nv_ref[...] = _adamw_math(w_ref[...], g, m_ref[...], v_ref[...])

    row = pl.BlockSpec((tr, C), lambda i: (i, 0))
    return pl.pallas_call(
        body, name=name, grid=(R // tr,),
        in_specs=[pl.BlockSpec((N_DEV, tr, C), lambda i: (0, i, 0)), row, row, row], out_specs=[row] * 4,
        out_shape=[jax.ShapeDtypeStruct((R, C), F32)] * 4, compiler_params=_params(("parallel",)),
    )(landed, w, m, v)


def _adamw_small(w, g, m, v):
    def body(w_ref, g_ref, m_ref, v_ref, d_ref, nm_ref, nv_ref):
        d_ref[...], nm_ref[...], nv_ref[...] = _adamw_math(w_ref[...], g_ref[...], m_ref[...], v_ref[...])

    full = pl.BlockSpec(w.shape, lambda: (0, 0))
    return pl.pallas_call(body, name="adamw_small", in_specs=[full] * 4, out_specs=[full] * 3,
                          out_shape=[jax.ShapeDtypeStruct(w.shape, F32)] * 3)(w, g, m, v)


MATRICES = ("w_in_even", "w_out_even", "w_in_odd", "w_out_odd", "w_mlp_up", "w_mlp_down")
SMALL = ("norm_mix", "norm_mlp", "ret_decay_logit", "ret_norm", "swa_q_norm", "swa_k_norm", "swa_sink",
         "t5_table", "ax_q_norm", "ax_k_norm")
MATRIX_OF = {"in_even": ("w_in_even", 0), "out_even": ("w_out_even", 0), "in_odd": ("w_in_odd", 0),
             "out_odd": ("w_out_odd", 0), "up0": ("w_mlp_up", 0), "up1": ("w_mlp_up", 1),
             "down0": ("w_mlp_down", 0), "down1": ("w_mlp_down", 1)}
GATHER_FIRST = ("in_even",)
GATHER_WHILE = {"in_even": ("up0",), "prep_even_fwd": ("out_even",), "retention_fwd": ("in_odd",), "swa_fwd": ("down0",),
                "flash_fwd": ("out_odd", "up1", "down1")}


SMALL_ROWS = 8
SMALL_AT = {"norm_mix": (0, 0), "norm_mlp": (2, 0), "ret_norm": (4, 0), "swa_q_norm": (5, 0), "swa_k_norm": (5, 128),
            "ax_q_norm": (5, 256), "ax_k_norm": (5, 384), "swa_sink": (5, 512), "ret_decay_logit": (5, 640),
            "t5_table": (6, 0)}
LOSS_ROW = 7


def _pack_small(arrays, loss_row=None):
    buf = jnp.zeros((SMALL_ROWS, 1024), F32)
    for name, (r, c) in SMALL_AT.items():
        a = arrays[name].astype(F32)
        a = a.reshape(1, -1) if name in ("ret_decay_logit", "t5_table") else a.reshape(-1, a.shape[-1])
        buf = lax.dynamic_update_slice(buf, a, (r, c))
    if loss_row is not None:
        buf = lax.dynamic_update_slice(buf, loss_row, (LOSS_ROW, 0))
    return buf


def _unpack_small(buf, like):
    out = {}
    for name, (r, c) in SMALL_AT.items():
        shape = like[name].shape
        rows = 1 if name in ("ret_decay_logit", "t5_table") else math.prod(shape[:-1])
        cols = math.prod(shape) // rows
        out[name] = buf[r:r + rows, c:c + cols].reshape(shape)
    return out


def kernel(x, norm_mix, norm_mlp, w_in_even, w_out_even, ret_decay_logit, ret_norm, swa_q_norm, swa_k_norm, swa_sink, t5_table, w_in_odd, w_out_odd, ax_q_norm, ax_k_norm, w_mlp_up, w_mlp_down, loss_target, m_norm_mix, m_norm_mlp, m_w_in_even, m_w_out_even, m_ret_decay_logit, m_ret_norm, m_swa_q_norm, m_swa_k_norm, m_swa_sink, m_t5_table, m_w_in_odd, m_w_out_odd, m_ax_q_norm, m_ax_k_norm, m_w_mlp_up, m_w_mlp_down, v_norm_mix, v_norm_mlp, v_w_in_even, v_w_out_even, v_ret_decay_logit, v_ret_norm, v_swa_q_norm, v_swa_k_norm, v_swa_sink, v_t5_table, v_w_in_odd, v_w_out_odd, v_ax_q_norm, v_ax_k_norm, v_w_mlp_up, v_w_mlp_down):
    given = dict(locals())
    weights = {n: given[n] for n in MATRICES + SMALL}
    moments_m = {n: given["m_" + n] for n in MATRICES + SMALL}
    moments_v = {n: given["v_" + n] for n in MATRICES + SMALL}

    def shard(table, key):
        arg, layer = MATRIX_OF[key]
        return table[arg][layer]

    gathered = _all_gather([shard(weights, k) for k in GATHER_FIRST])
    W = {k: _assemble(k, g) for k, g in zip(GATHER_FIRST, gathered)}
    late_keys = [k for keys in GATHER_WHILE.values() for k in keys]
    cast = dict(zip(late_keys, _cast_shards([shard(weights, k) for k in late_keys])))
    late = {stage: [(k, cast[k]) for k in keys] for stage, keys in GATHER_WHILE.items()}
    P = {"norm_mix": norm_mix, "norm_mlp": norm_mlp, "ret_decay_logit": ret_decay_logit[0], "ret_norm": ret_norm,
         "swa_q_norm": swa_q_norm, "swa_k_norm": swa_k_norm, "swa_sink": swa_sink, "t5_table": t5_table,
         "ax_q_norm": ax_q_norm, "ax_k_norm": ax_k_norm}

    loss_row, dx, landed, dP = _local_step(x[0], loss_target[0], W, P, late)

    per_key = {k: _sum_and_adamw(landed[k], shard(weights, k), shard(moments_m, k), shard(moments_v, k), "adamw_" + k)
               for k in MATRIX_OF}
    grads, deltas, new_m, new_v = {}, {}, {}, {}
    for i, out in enumerate((grads, deltas, new_m, new_v)):
        for n in MATRICES:
            out[n] = jnp.stack([per_key[k][i] for k, (arg, _) in MATRIX_OF.items() if arg == n])

    dP["ret_decay_logit"] = dP["ret_decay_logit"][None]
    total = _all_reduce_small(_pack_small(dP, loss_row))
    loss = total[LOSS_ROW, 0]
    small_d, small_m, small_v = _adamw_small(_pack_small(weights), total, _pack_small(moments_m), _pack_small(moments_v))
    like = {n: weights[n] for n in SMALL}
    for out, buf in ((grads, total), (deltas, small_d), (new_m, small_m), (new_v, small_v)):
        out.update(_unpack_small(buf, like))

    order = ("norm_mix", "norm_mlp", "w_in_even", "w_out_even", "ret_decay_logit", "ret_norm", "swa_q_norm", "swa_k_norm",
             "swa_sink", "t5_table", "w_in_odd", "w_out_odd", "ax_q_norm", "ax_k_norm", "w_mlp_up", "w_mlp_down")
    return (loss, dx[None], *[grads[n] for n in order], *[deltas[n] for n in order],
            *[new_m[n] for n in order], *[new_v[n] for n in order])
```
